```python
import math
import jax, jax.numpy as jnp
from jax import lax
import numpy as np

D_MODEL = 1024
BATCH = 4
SEQ = 4096
DEPTH = 4

HEAD_DIM = 64
NSA_HEADS = 4
NSA_KV_HEADS = 1
CMP_LEN = 32
CMP_STRIDE = 16
CMP_HIDDEN = 128
SEL_BLOCK = 64
SEL_TOP = 16
NSA_WINDOW = 512
MLA_HEADS = 4
MLA_Q_RANK = 256
MLA_KV_RANK = 128
MLA_NOPE = 64
MLA_ROPE = 32
MLA_V = 64
ROPE_THETA = 10000.0
SWA_HEADS = 8
SWA_KV_HEADS = 1
SWA_WINDOW = 128
REL_BUCKETS = 32
REL_MAX_DIST = 512
N_BIAS_HEADS = NSA_HEADS + SWA_HEADS

Q_BLOCK = 128
NORM_EPS = 1e-6
NEG = -1e30
BIG = 1e9

W_A = NSA_HEADS * HEAD_DIM
W_B = MLA_HEADS * MLA_V
W_C = SWA_HEADS * HEAD_DIM
D_MIX = W_A + W_B + W_C

A_Q = NSA_HEADS * HEAD_DIM
A_KV = 6 * NSA_KV_HEADS * HEAD_DIM
A_GATE = 3 * NSA_HEADS
B_CQ = MLA_Q_RANK
B_CKV = MLA_KV_RANK
B_KR = MLA_ROPE
C_Q = SWA_HEADS * HEAD_DIM
C_KV = 2 * SWA_KV_HEADS * HEAD_DIM
IN_SIZES = (A_Q, A_KV, A_GATE, B_CQ, B_CKV, B_KR, C_Q, C_KV, D_MIX)
D_IN = A_Q + A_KV + A_GATE + B_CQ + B_CKV + B_KR + C_Q + C_KV + D_MIX

kernel_name = 'hymba_nsa_mla_swa_sandwich'


def rms_norm(x, g):
    xf = x.astype(jnp.float32)
    y = xf * lax.rsqrt(jnp.mean(xf * xf, axis=-1, keepdims=True) + NORM_EPS)
    return (y * g.astype(jnp.float32)).astype(x.dtype)


def split_cols(h):
    outs, o = [], 0
    for s in IN_SIZES:
        outs.append(h[..., o:o + s])
        o += s
    return outs


def t5_bucket(dist):
    dist = jnp.maximum(dist, 0)
    exact = REL_BUCKETS // 2
    large = exact + (jnp.log(jnp.maximum(dist, 1).astype(jnp.float32) / exact)
                     / math.log(REL_MAX_DIST / exact) * (REL_BUCKETS - exact)).astype(jnp.int32)
    large = jnp.minimum(large, REL_BUCKETS - 1)
    return jnp.where(dist < exact, dist, large)


def rope(x, pos):
    half = x.shape[-1] // 2
    inv = ROPE_THETA ** (-jnp.arange(half, dtype=jnp.float32) / half)
    ang = pos.astype(jnp.float32)[:, None] * inv[None, :]
    cos, sin = jnp.cos(ang)[:, None, :], jnp.sin(ang)[:, None, :]
    x1, x2 = x[..., :half].astype(jnp.float32), x[..., half:].astype(jnp.float32)
    return jnp.concatenate([x1 * cos - x2 * sin, x2 * cos + x1 * sin], axis=-1).astype(x.dtype)


def banded_attention(q, k, v, window, bias_heads, sinks):
    B, T, H, dh = q.shape
    G = k.shape[2]
    hpg = H // G
    nq = T // Q_BLOCK
    nb = window // Q_BLOCK
    K = (nb + 1) * Q_BLOCK
    pad = ((0, 0), (nb * Q_BLOCK, 0), (0, 0), (0, 0))

    def band(t):
        tp = jnp.pad(t, pad).reshape(B, nq + nb, Q_BLOCK, G, dh)
        return jnp.concatenate([tp[:, i:i + nq] for i in range(nb + 1)], axis=2)

    kb, vb = band(k), band(v)
    qb = q.reshape(B, nq, Q_BLOCK, G, hpg, dh)
    logits = jnp.einsum('bnqghd,bnkgd->bnghqk', qb, kb).astype(jnp.float32) * (dh ** -0.5)
    i = jnp.arange(Q_BLOCK)[:, None]
    j = jnp.arange(K)[None, :]
    dist = nb * Q_BLOCK + i - j
    kpos = (jnp.arange(nq)[:, None, None] - nb) * Q_BLOCK + j[None]
    mask = (dist >= 0) & (dist < window) & (kpos >= 0)
    bias = bias_heads[:, t5_bucket(dist)].reshape(G, hpg, Q_BLOCK, K)
    logits = jnp.where(mask[None, :, None, None], logits + bias, NEG)
    if sinks is None:
        p = jax.nn.softmax(logits, axis=-1)
    else:
        s = jnp.broadcast_to(sinks.reshape(G, hpg, 1, 1).astype(jnp.float32), logits.shape[:-1] + (1,))
        p = jax.nn.softmax(jnp.concatenate([logits, s], axis=-1), axis=-1)[..., :-1]
    out = jnp.einsum('bnghqk,bnkgd->bnqghd', p.astype(v.dtype), vb)
    return out.reshape(B, T, H, dh)


def causal_block_attention(q, k, v, scale):
    B, T, H, dk = q.shape
    nq = T // Q_BLOCK
    qb = q.reshape(B, nq, Q_BLOCK, H, dk).transpose(1, 0, 2, 3, 4)
    kpos = jnp.arange(T)

    def one(args):
        qc, n = args
        logits = jnp.einsum('bqhd,bkhd->bhqk', qc, k).astype(jnp.float32) * scale
        qpos = n * Q_BLOCK + jnp.arange(Q_BLOCK)
        mask = kpos[None, :] <= qpos[:, None]
        p = jax.nn.softmax(jnp.where(mask, logits, NEG), axis=-1)
        return jnp.einsum('bhqk,bkhd->bqhd', p.astype(v.dtype), v)

    out = lax.map(one, (qb, jnp.arange(nq)))
    return out.transpose(1, 0, 2, 3, 4).reshape(B, T, H, v.shape[-1])


def nsa_attention(q, kv, gates, cmp_pos, cmp_w1, cmp_w2, bias_heads):
    B, T, H, dh = q.shape
    G = kv.shape[3]
    hpg = H // G
    scale = dh ** -0.5
    k_cmp, v_cmp, k_slc, v_slc, k_win, v_win = [kv[:, :, i] for i in range(6)]
    pos = jnp.arange(T)
    qg = q.reshape(B, T, G, hpg, dh)

    nc = T // CMP_STRIDE - 1

    def compress(t, pe, w1, w2):
        ch = t.reshape(B, T // CMP_STRIDE, CMP_STRIDE, G, dh)
        blk = jnp.concatenate([ch[:, :-1], ch[:, 1:]], axis=2) + pe[:, None, :]
        blk = blk.transpose(0, 1, 3, 2, 4).reshape(B, nc, G, CMP_LEN * dh)
        return jax.nn.silu(blk @ w1) @ w2

    kc = compress(k_cmp, cmp_pos[0], cmp_w1[0], cmp_w2[0])
    vc = compress(v_cmp, cmp_pos[1], cmp_w1[1], cmp_w2[1])
    lc = jnp.einsum('btghd,bcgd->bghtc', qg, kc).astype(jnp.float32) * scale
    cend = jnp.arange(nc) * CMP_STRIDE + CMP_LEN - 1
    dist_c = pos[:, None] - cend[None, :]
    valid_c = dist_c >= 0
    lc = lc + bias_heads[:, t5_bucket(dist_c)].reshape(G, hpg, T, nc)
    p_c = jax.nn.softmax(jnp.where(valid_c, lc, NEG), axis=-1) * valid_c
    o_cmp = jnp.einsum('bghtc,bcgd->btghd', p_c.astype(vc.dtype), vc)

    ns = T // SEL_BLOCK
    n_top = min(SEL_TOP, ns)
    sstart = np.arange(ns) * SEL_BLOCK
    cstart = np.arange(nc) * CMP_STRIDE
    overlap = (np.clip(np.minimum(cstart[:, None] + CMP_LEN, sstart[None, :] + SEL_BLOCK)
                       - np.maximum(cstart[:, None], sstart[None, :]), 0, None) / CMP_STRIDE).astype(np.float32)
    imp = jnp.einsum('bghtc,cs->bgts', p_c, jnp.asarray(overlap))
    blk = jnp.arange(ns)[None, :]
    cur = (pos // SEL_BLOCK)[:, None]
    forced = (blk == 0) | (blk == cur) | (blk == cur - 1)
    imp = jnp.where(forced, BIG, jnp.where(blk <= cur, imp, -BIG))
    _, idx = lax.top_k(imp, n_top)

    ksb = k_slc.reshape(B, ns, SEL_BLOCK, G, dh).transpose(0, 3, 1, 2, 4)
    vsb = v_slc.reshape(B, ns, SEL_BLOCK, G, dh).transpose(0, 3, 1, 2, 4)
    nq = T // Q_BLOCK
    qch = qg.reshape(B, nq, Q_BLOCK, G, hpg, dh).transpose(1, 0, 3, 4, 2, 5)
    ich = idx.reshape(B, G, nq, Q_BLOCK, n_top).transpose(2, 0, 1, 3, 4)
    gather = jax.vmap(jax.vmap(lambda blocks, ids: blocks[ids]))
    tok = jnp.arange(SEL_BLOCK)
    bias_g = bias_heads.reshape(G, hpg, REL_BUCKETS)
    group_bias = jax.vmap(lambda tab, bk: jnp.moveaxis(tab[:, bk], 0, 1), in_axes=(0, 1), out_axes=1)
    nk = n_top * SEL_BLOCK

    def sel_block(args):
        qc, ic, n = args
        kg = gather(ksb, ic).reshape(B, G, Q_BLOCK, nk, dh)
        vg = gather(vsb, ic).reshape(B, G, Q_BLOCK, nk, dh)
        kpos = (ic[..., None] * SEL_BLOCK + tok).reshape(B, G, Q_BLOCK, nk)
        qpos = n * Q_BLOCK + jnp.arange(Q_BLOCK)
        dist = qpos[:, None] - kpos
        l = jnp.einsum('bghqd,bgqkd->bghqk', qc, kg).astype(jnp.float32) * scale
        l = l + group_bias(bias_g, t5_bucket(dist))
        p = jax.nn.softmax(jnp.where((dist >= 0)[:, :, None], l, NEG), axis=-1)
        return jnp.einsum('bghqk,bgqkd->bqghd', p.astype(vg.dtype), vg)

    o_slc = lax.map(sel_block, (qch, ich, jnp.arange(nq)))
    o_slc = o_slc.transpose(1, 0, 2, 3, 4, 5).reshape(B, T, H, dh)

    o_win = banded_attention(q, k_win, v_win, NSA_WINDOW, bias_heads, None)

    g = jax.nn.sigmoid(gates.astype(jnp.float32)).astype(q.dtype)
    o = (g[..., 0:1] * o_cmp.reshape(B, T, H, dh) + g[..., 1:2] * o_slc + g[..., 2:3] * o_win)
    return o.reshape(B, T, H * dh)


def hybrid_layer(x, w_in, w_out, g_pre, g_post, cmp_pos, cmp_w1, cmp_w2,
                 q_norm, w_uq, kv_norm, w_ukv, sinks, rel_bias):
    B, T, _ = x.shape
    pos = jnp.arange(T)
    h = rms_norm(x, g_pre)
    a_q, a_kv, a_g, b_cq, b_ckv, b_kr, c_q, c_kv, z = split_cols(h @ w_in)

    o_a = nsa_attention(a_q.reshape(B, T, NSA_HEADS, HEAD_DIM),
                        a_kv.reshape(B, T, 6, NSA_KV_HEADS, HEAD_DIM),
                        a_g.reshape(B, T, NSA_HEADS, 3),
                        cmp_pos, cmp_w1, cmp_w2, rel_bias[:NSA_HEADS])

    qb = (rms_norm(b_cq, q_norm) @ w_uq).reshape(B, T, MLA_HEADS, MLA_NOPE + MLA_ROPE)
    q_full = jnp.concatenate([qb[..., :MLA_NOPE], rope(qb[..., MLA_NOPE:], pos)], axis=-1)
    kvb = (rms_norm(b_ckv, kv_norm) @ w_ukv).reshape(B, T, MLA_HEADS, MLA_NOPE + MLA_V)
    k_rope = rope(b_kr.reshape(B, T, 1, MLA_ROPE), pos)
    k_full = jnp.concatenate([kvb[..., :MLA_NOPE],
                              jnp.broadcast_to(k_rope, (B, T, MLA_HEADS, MLA_ROPE))], axis=-1)
    o_b = causal_block_attention(q_full, k_full, kvb[..., MLA_NOPE:], (MLA_NOPE + MLA_ROPE) ** -0.5)

    ckv = c_kv.reshape(B, T, 2, SWA_KV_HEADS, HEAD_DIM)
    o_c = banded_attention(c_q.reshape(B, T, SWA_HEADS, HEAD_DIM), ckv[:, :, 0], ckv[:, :, 1],
                           SWA_WINDOW, rel_bias[NSA_HEADS:], sinks)

    mixed = jnp.concatenate([o_a, o_b.reshape(B, T, W_B), o_c.reshape(B, T, W_C)], axis=-1) * jax.nn.silu(z)
    return x + rms_norm(mixed @ w_out, g_post)


def setup_inputs(seed: int = 0) -> dict:
    key = jax.random.key(seed)
    ks = jax.random.split(key, 14)
    f32 = jnp.float32

    def nrm(k, shape, scale):
        return jax.random.normal(k, shape, f32) * scale

    return {
        'x': nrm(ks[0], (BATCH, SEQ, D_MODEL), 1.0),
        'w_in': nrm(ks[1], (DEPTH, D_MODEL, D_IN), D_MODEL ** -0.5),
        'w_out': nrm(ks[2], (DEPTH, D_MIX, D_MODEL), D_MIX ** -0.5),
        'norm_pre': 1.0 + nrm(ks[3], (DEPTH, D_MODEL), 0.05),
        'norm_post': 1.0 + nrm(ks[4], (DEPTH, D_MODEL), 0.05),
        'cmp_pos': nrm(ks[5], (DEPTH, 2, CMP_LEN, HEAD_DIM), 0.1),
        'cmp_w1': nrm(ks[6], (DEPTH, 2, CMP_LEN * HEAD_DIM, CMP_HIDDEN), (CMP_LEN * HEAD_DIM) ** -0.5),
        'cmp_w2': nrm(ks[7], (DEPTH, 2, CMP_HIDDEN, HEAD_DIM), CMP_HIDDEN ** -0.5),
        'mla_q_norm': 1.0 + nrm(ks[8], (DEPTH, MLA_Q_RANK), 0.05),
        'mla_w_uq': nrm(ks[9], (DEPTH, MLA_Q_RANK, MLA_HEADS * (MLA_NOPE + MLA_ROPE)), MLA_Q_RANK ** -0.5),
        'mla_kv_norm': 1.0 + nrm(ks[10], (DEPTH, MLA_KV_RANK), 0.05),
        'mla_w_ukv': nrm(ks[11], (DEPTH, MLA_KV_RANK, MLA_HEADS * (MLA_NOPE + MLA_V)), MLA_KV_RANK ** -0.5),
        'swa_sinks': nrm(ks[12], (DEPTH, SWA_HEADS), 0.5),
        'rel_bias': nrm(ks[13], (N_BIAS_HEADS, REL_BUCKETS), 0.5),
    }


def reference(x, w_in, w_out, norm_pre, norm_post, cmp_pos, cmp_w1, cmp_w2,
              mla_q_norm, mla_w_uq, mla_kv_norm, mla_w_ukv, swa_sinks, rel_bias):
    for l in range(DEPTH):
        x = hybrid_layer(x, w_in[l], w_out[l], norm_pre[l], norm_post[l],
                         cmp_pos[l], cmp_w1[l], cmp_w2[l],
                         mla_q_norm[l], mla_w_uq[l], mla_kv_norm[l], mla_w_ukv[l],
                         swa_sinks[l], rel_bias)
    return x
```

```python
import functools
import math

import numpy as np
import jax
import jax.numpy as jnp
from jax import lax
from jax.experimental import pallas as pl
from jax.experimental.pallas import tpu as pltpu

F32 = jnp.float32
BF16 = jnp.bfloat16

D_MODEL = 1024
HEAD_DIM = 64
NSA_HEADS = 4
CMP_LEN = 32
CMP_STRIDE = 16
CMP_HIDDEN = 128
SEL_BLOCK = 64
SEL_TOP = 16
NSA_WINDOW = 512
MLA_HEADS = 4
MLA_Q_RANK = 256
MLA_KV_RANK = 128
MLA_NOPE = 64
MLA_ROPE = 32
MLA_V = 64
ROPE_THETA = 10000.0
SWA_HEADS = 8
SWA_WINDOW = 128
REL_BUCKETS = 32
REL_MAX_DIST = 512
NORM_EPS = 1e-6
NEG = -1e30
BIG = 1e9
IN_SIZES = (256, 384, 12, 256, 128, 32, 512, 128, 1024)

LANES = 128
Q_TILE = 128
FLASH_TILE = 256
GATE_LANE0 = 96
VMEM_LIMIT = 48 * 1024 * 1024

C_AQ, C_CMP, C_SLC, C_WIN, C_BCQ, C_BCKV, C_KRB, C_CQ, C_CKV, C_Z, C_END = (
    0, 256, 384, 512, 640, 896, 1024, 1152, 1664, 1792, 2816)


def _dot(a, b):
    return jnp.dot(a, b, preferred_element_type=F32)


def _dot_nt(a, b):
    return lax.dot_general(a, b, (((1,), (1,)), ((), ())), preferred_element_type=F32)


def _bucket_thresholds():
    d = np.arange(0, 4 * REL_MAX_DIST)
    exact = REL_BUCKETS // 2
    large = exact + (np.log(np.maximum(d, 1).astype(np.float32) / np.float32(exact))
                     / np.float32(math.log(REL_MAX_DIST / exact)) * np.float32(REL_BUCKETS - exact)).astype(np.int32)
    b = np.where(d < exact, d, np.minimum(large, REL_BUCKETS - 1))
    assert np.all(np.diff(b) >= 0) and b[-1] == REL_BUCKETS - 1
    return [int(np.argmax(b >= k)) for k in range(REL_BUCKETS)]


_THRESH = _bucket_thresholds()


def _cparams(sem):
    return pltpu.CompilerParams(dimension_semantics=sem, vmem_limit_bytes=VMEM_LIMIT)


def _table_kernel(h0, a_col, d0, lo, hi, rb_ref, out_ref):
    h = pl.program_id(0) + h0
    rt, cc = out_ref.shape[1], out_ref.shape[2]
    r = pl.program_id(1) * rt + lax.broadcasted_iota(jnp.int32, (rt, cc), 0)
    c = lax.broadcasted_iota(jnp.int32, (rt, cc), 1)
    dist = r + a_col * c + d0
    acc = jnp.full((rt, cc), rb_ref[h, 0], F32)
    for k in range(1, REL_BUCKETS):
        acc = jnp.where(dist >= _THRESH[k], rb_ref[h, k], acc)
    ok = (dist >= lo) & (dist < hi)
    out_ref[0] = jnp.where(ok, acc, NEG)


def _bias_table(rel_bias, h0, nh, rows, cols, a_col, d0, lo, hi, rt):
    return pl.pallas_call(
        functools.partial(_table_kernel, h0, a_col, d0, lo, hi),
        grid=(nh, rows // rt),
        in_specs=[pl.BlockSpec(memory_space=pltpu.SMEM)],
        out_specs=pl.BlockSpec((1, rt, cols), lambda h, r: (h, r, 0)),
        out_shape=jax.ShapeDtypeStruct((nh, rows, cols), F32),
        compiler_params=_cparams(("arbitrary", "arbitrary")),
        name="bias_table",
    )(rel_bias)


def _rms(v, g):
    return v * lax.rsqrt(jnp.mean(v * v, axis=-1, keepdims=True) + NORM_EPS) * g


def _proj_kernel(x_ref, gpre_ref, w_ref, qn_ref, wuq_ref, kvn_ref, wk_ref, wv_ref,
                 rc_ref, rsa_ref, rsb_ref, eaug_ref,
                 qa_ref, cmp_ref, slck_ref, slcv_ref, win_ref, qm_ref, km_ref, vm_ref,
                 krb_ref, cq_ref, ckv_ref, z_ref):
    hb = _rms(x_ref[...], gpre_ref[...]).astype(BF16)

    def proj(c0, c1):
        return _dot(hb, w_ref[:, c0:c1])

    tm = hb.shape[0]
    lane = lax.broadcasted_iota(jnp.int32, (tm, LANES), 1)
    lo = lane < HEAD_DIM

    qa_ref[...] = (proj(C_AQ, C_CMP) * 0.125).astype(BF16)
    cmp_ref[...] = proj(C_CMP, C_SLC)
    slc = proj(C_SLC, C_WIN)
    slcv_ref[...] = slc.astype(BF16)
    slck_ref[...] = jnp.where(lo, slc, eaug_ref[...].astype(F32)).astype(BF16)
    win_ref[...] = proj(C_WIN, C_BCQ).astype(BF16)
    cq_ref[...] = (proj(C_CQ, C_CKV) * 0.125).astype(BF16)
    ckv_ref[...] = proj(C_CKV, C_Z).astype(BF16)
    z_ref[...] = proj(C_Z, C_END)

    rc, rsa, rsb = rc_ref[...], rsa_ref[...], rsb_ref[...]

    def rope(v):
        return v * rc + pltpu.roll(v, 16, 1) * rsa + pltpu.roll(v, 112, 1) * rsb

    cqn = _rms(proj(C_BCQ, C_BCKV), qn_ref[...]).astype(BF16)
    qm = _dot(cqn, wuq_ref[...])
    for h in range(MLA_HEADS):
        qm_ref[:, LANES * h:LANES * (h + 1)] = rope(qm[:, LANES * h:LANES * (h + 1)]).astype(BF16)

    krb = proj(C_KRB, C_CQ)
    krb_ref[...] = krb
    krr = jnp.where(lo | (lane >= HEAD_DIM + MLA_ROPE), 0.0, rope(krb))

    ckvn = _rms(proj(C_BCKV, C_KRB), kvn_ref[...]).astype(BF16)
    kn = _dot(ckvn, wk_ref[...])
    for h in range(MLA_HEADS):
        km_ref[:, LANES * h:LANES * (h + 1)] = (kn[:, LANES * h:LANES * (h + 1)] + krr).astype(BF16)
    vm_ref[...] = _dot(ckvn, wv_ref[...]).astype(BF16)


def _project(x2, gpre, w, qn, wuq, kvn, wk, wv, rope_c, rope_sa, rope_sb, eaug, T, tm):
    BT = x2.shape[0]
    nt = T // tm
    row = lambda i: (i, 0)
    fix = lambda i: (0, 0)
    pos = lambda i: (i % nt, 0)
    widths = [(256, BF16), (128, F32), (128, BF16), (128, BF16), (128, BF16), (512, BF16), (512, BF16),
              (256, BF16), (128, F32), (512, BF16), (128, BF16), (1024, F32)]
    return pl.pallas_call(
        _proj_kernel,
        grid=(BT // tm,),
        in_specs=[pl.BlockSpec((tm, D_MODEL), row), pl.BlockSpec((1, D_MODEL), fix),
                  pl.BlockSpec(w.shape, fix), pl.BlockSpec(qn.shape, fix), pl.BlockSpec(wuq.shape, fix),
                  pl.BlockSpec(kvn.shape, fix), pl.BlockSpec(wk.shape, fix), pl.BlockSpec(wv.shape, fix),
                  pl.BlockSpec((tm, LANES), pos), pl.BlockSpec((tm, LANES), pos), pl.BlockSpec((tm, LANES), pos),
                  pl.BlockSpec((tm, LANES), pos)],
        out_specs=[pl.BlockSpec((tm, wd), row) for wd, _ in widths],
        out_shape=[jax.ShapeDtypeStruct((BT, wd), dt) for wd, dt in widths],
        compiler_params=_cparams(("arbitrary",)),
        name="proj",
    )(x2, gpre, w, qn, wuq, kvn, wk, wv, rope_c, rope_sa, rope_sb, eaug)


def _compress_kernel(c_ref, pet_ref, peb_ref, w1t_ref, w1b_ref, w2_ref, o_ref):
    c = c_ref[0]
    nch = c.shape[0]
    top = (c + pet_ref[...]).astype(BF16)
    bot = (c + peb_ref[...]).astype(BF16)
    out = jnp.zeros((nch, LANES), F32)
    for i in range(2):
        pre = _dot(top, w1t_ref[i]) + pltpu.roll(_dot(bot, w1b_ref[i]), nch - 1, 0)
        hid = pre * (1.0 / (1.0 + jnp.exp(-pre)))
        out = out + _dot(hid.astype(BF16), w2_ref[i])
    o_ref[0] = out.astype(BF16)


def _compress(chunks, pet, peb, w1t, w1b, w2):
    B, nch, width = chunks.shape
    full = lambda a: pl.BlockSpec(a.shape, lambda b: (0,) * a.ndim)
    return pl.pallas_call(
        _compress_kernel,
        grid=(B,),
        in_specs=[pl.BlockSpec((1, nch, width), lambda b: (b, 0, 0)),
                  full(pet), full(peb), full(w1t), full(w1b), full(w2)],
        out_specs=pl.BlockSpec((1, nch, LANES), lambda b: (b, 0, 0)),
        out_shape=jax.ShapeDtypeStruct((B, nch, LANES), BF16),
        compiler_params=_cparams(("arbitrary",)),
        name="compress",
    )(chunks, pet, peb, w1t, w1b, w2)


def _head_slabs(qblk, n_heads):
    q = qblk.astype(F32)
    lo = lax.broadcasted_iota(jnp.int32, (q.shape[0], LANES), 1) < HEAD_DIM
    out = []
    for j in range(n_heads // 2):
        slab = q[:, LANES * j:LANES * (j + 1)]
        out.append(jnp.where(lo, slab, 0.0))
        out.append(jnp.where(lo, pltpu.roll(slab, HEAD_DIM, 1), 0.0))
    return out


def _merge_upper(accs):
    lo = lax.broadcasted_iota(jnp.int32, accs[0].shape, 1) < HEAD_DIM
    slabs = [jnp.where(lo, pltpu.roll(accs[2 * j], HEAD_DIM, 1), accs[2 * j + 1]) for j in range(len(accs) // 2)]
    return jnp.concatenate(slabs, axis=1)


def _cmp_kernel(q_ref, kvc_ref, bc_ref, ovt_ref, ocmp_ref, selb_ref):
    tq = q_ref.shape[1]
    t0 = pl.program_id(1) * tq
    kvc = kvc_ref[0]
    qs = _head_slabs(q_ref[0], NSA_HEADS)
    valid = bc_ref[0] > 0.5 * NEG
    psum = jnp.zeros(valid.shape, F32)
    outs = []
    for h in range(NSA_HEADS):
        s = _dot_nt(qs[h].astype(BF16), kvc) + bc_ref[h]
        m = jnp.max(s, axis=-1, keepdims=True)
        e = jnp.where(valid, jnp.exp(s - m), 0.0)
        l = jnp.sum(e, axis=-1, keepdims=True)
        p = e / jnp.where(l > 0.0, l, 1.0)
        psum = psum + p
        outs.append(_dot(p.astype(BF16), kvc))
    ocmp_ref[0] = _merge_upper(outs)

    hi = psum.astype(BF16)
    lo = (psum - hi.astype(F32)).astype(BF16)
    ovt = ovt_ref[...]
    imp = _dot_nt(ovt, hi) + _dot_nt(ovt, lo)
    ns = imp.shape[0]
    srow = lax.broadcasted_iota(jnp.int32, (ns, tq), 0)
    cur = (t0 + lax.broadcasted_iota(jnp.int32, (ns, tq), 1)) // SEL_BLOCK
    forced = (srow == 0) | (srow == cur) | (srow == cur - 1)
    x = jnp.where(forced, BIG, jnp.where(srow <= cur, imp, -BIG))
    cnt = jnp.zeros((ns, tq), F32)
    for sp in range(ns):
        other = x[sp:sp + 1, :]
        beats = (other > x) | ((other == x) & (srow > sp))
        cnt = cnt + jnp.where(beats, 1.0, 0.0)
    sel = (cnt < float(min(SEL_TOP, ns))) & (srow <= cur)
    sb = jnp.where(sel, 0.0, NEG)
    if ns < HEAD_DIM:
        sb = jnp.concatenate([sb, jnp.full((HEAD_DIM - ns, tq), NEG, F32)], axis=0)
    full = jnp.concatenate([jnp.zeros((HEAD_DIM, tq), F32), sb], axis=0)
    selb_ref[0] = full.T.astype(BF16)


def _cmp_attention(qa, kvc, bc, ovt, T):
    B = qa.shape[0]
    ncp = kvc.shape[1]
    return pl.pallas_call(
        _cmp_kernel,
        grid=(B, T // Q_TILE),
        in_specs=[pl.BlockSpec((1, Q_TILE, 256), lambda b, n: (b, n, 0)),
                  pl.BlockSpec((1, ncp, LANES), lambda b, n: (b, 0, 0)),
                  pl.BlockSpec((NSA_HEADS, Q_TILE, ncp), lambda b, n: (0, n, 0)),
                  pl.BlockSpec(ovt.shape, lambda b, n: (0, 0))],
        out_specs=[pl.BlockSpec((1, Q_TILE, 256), lambda b, n: (b, n, 0)),
                   pl.BlockSpec((1, Q_TILE, LANES), lambda b, n: (b, n, 0))],
        out_shape=[jax.ShapeDtypeStruct((B, T, 256), F32), jax.ShapeDtypeStruct((B, T, LANES), BF16)],
        compiler_params=_cparams(("arbitrary", "arbitrary")),
        name="cmp_select",
    )(qa, kvc, bc, ovt)


def _slc_kernel(b31_ref, q_ref, selb_ref, ka_ref, kv_ref, tab_ref, o_ref, m_s, l_s, acc_s):
    tq = q_ref.shape[1]
    n = pl.program_id(1)
    q0 = n * tq
    qs = _head_slabs(q_ref[0], NSA_HEADS)
    selb = selb_ref[0].astype(F32)
    qa = [(qs[h] + selb).astype(BF16) for h in range(NSA_HEADS)]
    m_s[...] = jnp.full(m_s.shape, -jnp.inf, F32)
    l_s[...] = jnp.zeros(l_s.shape, F32)
    acc_s[...] = jnp.zeros(acc_s.shape, F32)

    def step(k0, bias_of_head):
        ka = ka_ref[0, pl.ds(k0, tq), :]
        kv = kv_ref[0, pl.ds(k0, tq), :]
        for h in range(NSA_HEADS):
            s = _dot_nt(qa[h], ka) + bias_of_head(h)
            m_old = m_s[h]
            m_new = jnp.maximum(m_old, jnp.max(s, axis=-1, keepdims=True))
            p = jnp.exp(s - m_new)
            alpha = jnp.exp(m_old - m_new)
            l_s[h] = alpha * l_s[h] + jnp.sum(p, axis=-1, keepdims=True)
            acc_s[h] = alpha * acc_s[h] + _dot(p.astype(BF16), kv)
            m_s[h] = m_new

    def far(j, carry):
        step(pl.multiple_of(j * tq, tq), lambda h: b31_ref[h])
        return carry

    lax.fori_loop(0, jnp.maximum(n - 2, 0), far, 0)
    for d in range(3):
        k0 = q0 - (2 - d) * tq

        @pl.when(k0 >= 0)
        def _():
            step(pl.multiple_of(jnp.maximum(k0, 0), tq),
                 lambda h: tab_ref[h, (2 - d) * tq:(3 - d) * tq, :])

    o_ref[0] = _merge_upper([acc_s[h] / l_s[h] for h in range(NSA_HEADS)])


def _slc_attention(b31, qa, selb, slck, slcv, tab, T):
    B = qa.shape[0]
    tq = FLASH_TILE
    return pl.pallas_call(
        _slc_kernel,
        grid=(B, T // tq),
        in_specs=[pl.BlockSpec(memory_space=pltpu.SMEM),
                  pl.BlockSpec((1, tq, 256), lambda b, n: (b, n, 0)),
                  pl.BlockSpec((1, tq, LANES), lambda b, n: (b, n, 0)),
                  pl.BlockSpec((1, T, LANES), lambda b, n: (b, 0, 0)),
                  pl.BlockSpec((1, T, LANES), lambda b, n: (b, 0, 0)),
                  pl.BlockSpec(tab.shape, lambda b, n: (0, 0, 0))],
        out_specs=pl.BlockSpec((1, tq, 256), lambda b, n: (b, n, 0)),
        out_shape=jax.ShapeDtypeStruct((B, T, 256), F32),
        scratch_shapes=[pltpu.VMEM((NSA_HEADS, tq, 1), F32), pltpu.VMEM((NSA_HEADS, tq, 1), F32),
                        pltpu.VMEM((NSA_HEADS, tq, LANES), F32)],
        compiler_params=_cparams(("arbitrary", "arbitrary")),
        name="slc_attention",
    )(b31, qa, selb, slck, slcv, tab)


def _band_kernel(n_heads, window, has_sink, *refs):
    if has_sink:
        sink_ref, q_ref, kv_ref, tab_ref, o_ref = refs
    else:
        q_ref, kv_ref, tab_ref, o_ref = refs
    tq = q_ref.shape[1]
    span = window + tq
    q0 = pl.multiple_of(pl.program_id(1) * tq, tq)
    kv = kv_ref[0, pl.ds(q0, span), :]
    qs = _head_slabs(q_ref[0], n_heads)
    kvalid = lax.broadcasted_iota(jnp.int32, (tq, span), 1) >= window - q0
    outs = []
    for h in range(n_heads):
        s = jnp.where(kvalid, _dot_nt(qs[h].astype(BF16), kv) + tab_ref[h], NEG)
        m = jnp.max(s, axis=-1, keepdims=True)
        if has_sink:
            m = jnp.maximum(m, sink_ref[h])
        e = jnp.exp(s - m)
        l = jnp.sum(e, axis=-1, keepdims=True)
        if has_sink:
            l = l + jnp.exp(sink_ref[h] - m)
        outs.append(_dot((e / l).astype(BF16), kv))
    o_ref[0] = _merge_upper(outs)


def _band_attention(q, kv_pad, tab, sinks, n_heads, window, T):
    B = q.shape[0]
    width = n_heads * HEAD_DIM
    has_sink = sinks is not None
    in_specs = [pl.BlockSpec((1, Q_TILE, width), lambda b, n: (b, n, 0)),
                pl.BlockSpec((1, T + window, LANES), lambda b, n: (b, 0, 0)),
                pl.BlockSpec(tab.shape, lambda b, n: (0, 0, 0))]
    args = (q, kv_pad, tab)
    if has_sink:
        in_specs = [pl.BlockSpec(memory_space=pltpu.SMEM)] + in_specs
        args = (sinks,) + args
    return pl.pallas_call(
        functools.partial(_band_kernel, n_heads, window, has_sink),
        grid=(B, T // Q_TILE),
        in_specs=in_specs,
        out_specs=pl.BlockSpec((1, Q_TILE, width), lambda b, n: (b, n, 0)),
        out_shape=jax.ShapeDtypeStruct((B, T, width), F32),
        compiler_params=_cparams(("arbitrary", "arbitrary")),
        name="band_sink" if has_sink else "band_window",
    )(*args)


def _mla_kernel(q_ref, k_ref, v_ref, o_ref, m_s, l_s, acc_s):
    tq = q_ref.shape[1]
    n = pl.program_id(1)
    scale = float((MLA_NOPE + MLA_ROPE) ** -0.5)
    m_s[...] = jnp.full(m_s.shape, -jnp.inf, F32)
    l_s[...] = jnp.zeros(l_s.shape, F32)
    acc_s[...] = jnp.zeros(acc_s.shape, F32)
    row = lax.broadcasted_iota(jnp.int32, (tq, tq), 0)
    col = lax.broadcasted_iota(jnp.int32, (tq, tq), 1)

    def step(k0, diag):
        for h in range(MLA_HEADS):
            q = q_ref[0, :, LANES * h:LANES * (h + 1)]
            k = k_ref[0, pl.ds(k0, tq), LANES * h:LANES * (h + 1)]
            v = v_ref[0, pl.ds(k0, tq), LANES * (h // 2):LANES * (h // 2 + 1)]
            s = _dot_nt(q, k) * scale
            if diag:
                s = jnp.where(col <= row, s, NEG)
            m_old = m_s[h]
            m_new = jnp.maximum(m_old, jnp.max(s, axis=-1, keepdims=True))
            p = jnp.exp(s - m_new)
            alpha = jnp.exp(m_old - m_new)
            l_s[h] = alpha * l_s[h] + jnp.sum(p, axis=-1, keepdims=True)
            acc_s[h] = alpha * acc_s[h] + _dot(p.astype(BF16), v)
            m_s[h] = m_new

    def far(j, carry):
        step(pl.multiple_of(j * tq, tq), False)
        return carry

    lax.fori_loop(0, n, far, 0)
    step(pl.multiple_of(n * tq, tq), True)
    lo = lax.broadcasted_iota(jnp.int32, (tq, LANES), 1) < HEAD_DIM
    outs = [acc_s[h] / l_s[h] for h in range(MLA_HEADS)]
    o_ref[0] = jnp.concatenate([jnp.where(lo, outs[2 * j], outs[2 * j + 1]) for j in range(MLA_HEADS // 2)], axis=1)


def _mla_attention(qm, km, vm, T):
    B = qm.shape[0]
    tq = FLASH_TILE
    return pl.pallas_call(
        _mla_kernel,
        grid=(B, T // tq),
        in_specs=[pl.BlockSpec((1, tq, 512), lambda b, n: (b, n, 0)),
                  pl.BlockSpec((1, T, 512), lambda b, n: (b, 0, 0)),
                  pl.BlockSpec((1, T, 256), lambda b, n: (b, 0, 0))],
        out_specs=pl.BlockSpec((1, tq, 256), lambda b, n: (b, n, 0)),
        out_shape=jax.ShapeDtypeStruct((B, T, 256), F32),
        scratch_shapes=[pltpu.VMEM((MLA_HEADS, tq, 1), F32), pltpu.VMEM((MLA_HEADS, tq, 1), F32),
                        pltpu.VMEM((MLA_HEADS, tq, LANES), F32)],
        compiler_params=_cparams(("arbitrary", "arbitrary")),
        name="mla_attention",
    )(qm, km, vm)


def _out_kernel(x_ref, ocmp_ref, oslc_ref, owin_ref, krb_ref, ob_ref, oc_ref, z_ref, w_ref, gpost_ref, o_ref):
    tm = x_ref.shape[0]
    sig = 1.0 / (1.0 + jnp.exp(-krb_ref[...]))
    lo = lax.broadcasted_iota(jnp.int32, (tm, LANES), 1) < HEAD_DIM

    def gate(h, j):
        c = GATE_LANE0 + 3 * h + j
        return jnp.broadcast_to(sig[:, c:c + 1], (tm, LANES))

    branches = (ocmp_ref, oslc_ref, owin_ref)
    slabs = []
    for p in range(NSA_HEADS // 2):
        acc = jnp.zeros((tm, LANES), F32)
        for j in range(3):
            g = jnp.where(lo, gate(2 * p, j), gate(2 * p + 1, j))
            term = g * branches[j][:, LANES * p:LANES * (p + 1)]
            acc = term if j == 0 else acc + term
        slabs.append(acc)
    z = z_ref[...]
    mixed = jnp.concatenate(slabs + [ob_ref[...], oc_ref[...]], axis=1) * (z * (1.0 / (1.0 + jnp.exp(-z))))
    y = _dot(mixed.astype(BF16), w_ref[...])
    o_ref[...] = x_ref[...] + _rms(y, gpost_ref[...])


def _out_project(x2, ocmp, oslc, owin, krb, ob, oc, z, w, gpost, tm):
    BT = x2.shape[0]
    row = lambda i: (i, 0)
    fix = lambda i: (0, 0)
    spec = lambda a: pl.BlockSpec((tm, a.shape[1]), row)
    return pl.pallas_call(
        _out_kernel,
        grid=(BT // tm,),
        in_specs=[spec(x2), spec(ocmp), spec(oslc), spec(owin), spec(krb), spec(ob), spec(oc), spec(z),
                  pl.BlockSpec(w.shape, fix), pl.BlockSpec(gpost.shape, fix)],
        out_specs=pl.BlockSpec((tm, D_MODEL), row),
        out_shape=jax.ShapeDtypeStruct((BT, D_MODEL), F32),
        compiler_params=_cparams(("arbitrary",)),
        name="out_proj",
    )(x2, ocmp, oslc, owin, krb, ob, oc, z, w, gpost)


def _split_in(w):
    outs, o = [], 0
    for s in IN_SIZES:
        outs.append(w[..., o:o + s])
        o += s
    return outs


def _prep_w_in(w):
    a_q, a_kv, a_g, b_cq, b_ckv, b_kr, c_q, c_kv, z = _split_in(w)
    zeros = lambda n: jnp.zeros(w.shape[:-1] + (n,), w.dtype)
    krb = jnp.concatenate([zeros(HEAD_DIM), b_kr, a_g, zeros(LANES - HEAD_DIM - MLA_ROPE - 3 * NSA_HEADS)], axis=-1)
    return jnp.concatenate([a_q, a_kv, b_cq, b_ckv, krb, c_q, c_kv, z], axis=-1).astype(BF16)


def _prep_mla(w_uq, w_ukv):
    L = w_uq.shape[0]
    dq = MLA_NOPE + MLA_ROPE
    uq = w_uq.reshape(L, MLA_Q_RANK, MLA_HEADS, dq)
    uq = jnp.concatenate([uq, jnp.zeros((L, MLA_Q_RANK, MLA_HEADS, LANES - dq), uq.dtype)], axis=-1)
    ukv = w_ukv.reshape(L, MLA_KV_RANK, MLA_HEADS, MLA_NOPE + MLA_V)
    uk = jnp.concatenate([ukv[..., :MLA_NOPE], jnp.zeros((L, MLA_KV_RANK, MLA_HEADS, LANES - MLA_NOPE), ukv.dtype)], axis=-1)
    uv = ukv[..., MLA_NOPE:]
    return (uq.reshape(L, MLA_Q_RANK, MLA_HEADS * LANES).astype(BF16),
            uk.reshape(L, MLA_KV_RANK, MLA_HEADS * LANES).astype(BF16),
            uv.reshape(L, MLA_KV_RANK, MLA_HEADS * MLA_V).astype(BF16))


def _prep_compress(cmp_pos, cmp_w1, cmp_w2):
    L = cmp_pos.shape[0]
    pe = jnp.concatenate([cmp_pos[:, 0], cmp_pos[:, 1]], axis=-1)
    pet = pe[:, :CMP_STRIDE].reshape(L, 1, CMP_STRIDE * LANES)
    peb = pe[:, CMP_STRIDE:].reshape(L, 1, CMP_STRIDE * LANES)
    w1 = cmp_w1.reshape(L, 2, CMP_LEN, HEAD_DIM, CMP_HIDDEN)
    zero = jnp.zeros_like(w1[:, 0])
    w1k = jnp.concatenate([w1[:, 0], zero], axis=2)
    w1v = jnp.concatenate([zero, w1[:, 1]], axis=2)
    w1e = jnp.stack([w1k, w1v], axis=1)
    w1t = w1e[:, :, :CMP_STRIDE].reshape(L, 2, CMP_STRIDE * LANES, CMP_HIDDEN).astype(BF16)
    w1b = w1e[:, :, CMP_STRIDE:].reshape(L, 2, CMP_STRIDE * LANES, CMP_HIDDEN).astype(BF16)
    z2 = jnp.zeros_like(cmp_w2[:, 0])
    w2 = jnp.stack([jnp.concatenate([cmp_w2[:, 0], z2], axis=-1),
                    jnp.concatenate([z2, cmp_w2[:, 1]], axis=-1)], axis=1).astype(BF16)
    return pet, peb, w1t, w1b, w2


def _rope_tables(T):
    half = MLA_ROPE // 2
    inv = ROPE_THETA ** (-jnp.arange(half, dtype=F32) / half)
    ang = jnp.arange(T).astype(F32)[:, None] * inv[None, :]
    cos, sin = jnp.cos(ang), jnp.sin(ang)
    z = lambda n: jnp.zeros((T, n), F32)
    rc = jnp.concatenate([jnp.ones((T, MLA_NOPE), F32), cos, cos, z(LANES - MLA_NOPE - MLA_ROPE)], axis=1)
    rsa = jnp.concatenate([z(MLA_NOPE + half), sin, z(LANES - MLA_NOPE - MLA_ROPE)], axis=1)
    rsb = jnp.concatenate([z(MLA_NOPE), -sin, z(LANES - MLA_NOPE - half)], axis=1)
    return rc, rsa, rsb


def _static_tables(T):
    ns = T // SEL_BLOCK
    nc = T // CMP_STRIDE - 1
    ncp = T // CMP_STRIDE
    sstart = np.arange(ns) * SEL_BLOCK
    cstart = np.arange(nc) * CMP_STRIDE
    overlap = (np.clip(np.minimum(cstart[:, None] + CMP_LEN, sstart[None, :] + SEL_BLOCK)
                       - np.maximum(cstart[:, None], sstart[None, :]), 0, None) / CMP_STRIDE).astype(np.float32)
    ovt = np.zeros((ns, ncp), np.float32)
    ovt[:, :nc] = overlap.T
    eaug = np.zeros((T, LANES), np.float32)
    eaug[np.arange(T), HEAD_DIM + np.arange(T) // SEL_BLOCK] = 1.0
    return jnp.asarray(ovt, BF16), jnp.asarray(eaug, BF16)


def kernel(x, w_in, w_out, norm_pre, norm_post, cmp_pos, cmp_w1, cmp_w2, mla_q_norm, mla_w_uq, mla_kv_norm,
           mla_w_ukv, swa_sinks, rel_bias):
    B, T, D = x.shape
    depth = w_in.shape[0]
    assert D == D_MODEL and T % (4 * FLASH_TILE) == 0 and T // SEL_BLOCK <= HEAD_DIM
    ncp = T // CMP_STRIDE
    tm = 256

    w_in_p = _prep_w_in(w_in)
    wuq, wk, wv = _prep_mla(mla_w_uq, mla_w_ukv)
    pet, peb, w1t, w1b, w2 = _prep_compress(cmp_pos, cmp_w1, cmp_w2)
    w_out_b = w_out.astype(BF16)
    rope_c, rope_sa, rope_sb = _rope_tables(T)
    ovt, eaug = _static_tables(T)

    bc = _bias_table(rel_bias, 0, NSA_HEADS, T, ncp, -CMP_STRIDE, -(CMP_LEN - 1), 0, 1 << 30, 256)
    tab_near = _bias_table(rel_bias, 0, NSA_HEADS, 3 * FLASH_TILE, FLASH_TILE, -1, 0, 0, 1 << 30, 256)
    tab_win = _bias_table(rel_bias, 0, NSA_HEADS, Q_TILE, NSA_WINDOW + Q_TILE, -1, NSA_WINDOW, 0, NSA_WINDOW, Q_TILE)
    tab_swa = _bias_table(rel_bias, NSA_HEADS, SWA_HEADS, Q_TILE, SWA_WINDOW + Q_TILE, -1, SWA_WINDOW, 0, SWA_WINDOW, Q_TILE)
    b31 = rel_bias[:NSA_HEADS, REL_BUCKETS - 1]

    x2 = x.reshape(B * T, D)
    for l in range(depth):
        (qa, cmpkv, slck, slcv, win, qm, km, vm, krb, cq, ckv, z) = _project(
            x2, norm_pre[l][None], w_in_p[l], mla_q_norm[l][None], wuq[l], mla_kv_norm[l][None], wk[l], wv[l],
            rope_c, rope_sa, rope_sb, eaug, T, tm)
        r3 = lambda a: a.reshape(B, T, a.shape[-1])
        kvc = _compress(cmpkv.reshape(B, ncp, CMP_STRIDE * LANES), pet[l], peb[l], w1t[l], w1b[l], w2[l])
        qa3 = r3(qa)
        ocmp, selb = _cmp_attention(qa3, kvc, bc, ovt, T)
        oslc = _slc_attention(b31, qa3, selb, r3(slck), r3(slcv), tab_near, T)
        win_pad = jnp.pad(r3(win), ((0, 0), (NSA_WINDOW, 0), (0, 0)))
        owin = _band_attention(qa3, win_pad, tab_win, None, NSA_HEADS, NSA_WINDOW, T)
        ckv_pad = jnp.pad(r3(ckv), ((0, 0), (SWA_WINDOW, 0), (0, 0)))
        oc = _band_attention(r3(cq), ckv_pad, tab_swa, swa_sinks[l], SWA_HEADS, SWA_WINDOW, T)
        ob = _mla_attention(r3(qm), r3(km), r3(vm), T)
        flat = lambda a: a.reshape(B * T, a.shape[-1])
        x2 = _out_project(x2, flat(ocmp), flat(oslc), flat(owin), krb, flat(ob), flat(oc), z,
                          w_out_b[l], norm_post[l][None], tm)
    return x2.reshape(B, T, D)
```

```python
import functools
import math

import numpy as np
import jax
import jax.numpy as jnp
from jax import lax
from jax.experimental import pallas as pl
from jax.experimental.pallas import tpu as pltpu

F32 = jnp.float32
BF16 = jnp.bfloat16

D_MODEL = 1024
HEAD_DIM = 64
NSA_HEADS = 4
CMP_LEN = 32
CMP_STRIDE = 16
CMP_HIDDEN = 128
SEL_BLOCK = 64
SEL_TOP = 16
NSA_WINDOW = 512
MLA_HEADS = 4
MLA_Q_RANK = 256
MLA_KV_RANK = 128
MLA_NOPE = 64
MLA_ROPE = 32
MLA_V = 64
ROPE_THETA = 10000.0
SWA_HEADS = 8
SWA_WINDOW = 128
REL_BUCKETS = 32
REL_MAX_DIST = 512
NORM_EPS = 1e-6
NEG = -1e30
BIG = 1e9
LOG2E = math.log2(math.e)
IN_SIZES = (256, 384, 12, 256, 128, 32, 512, 128, 1024)

LANES = 128
Q_TILE = 128
FLASH_TILE = 512
GATE_LANE0 = 96
VMEM_LIMIT = 56 * 1024 * 1024

C_AQ, C_CMP, C_SLC, C_WIN, C_BCQ, C_BCKV, C_KRB, C_CQ, C_CKV, C_Z, C_END = (
    0, 256, 384, 512, 640, 896, 1024, 1152, 1664, 1792, 2816)


def _dot(a, b):
    return jnp.dot(a, b, preferred_element_type=F32)


def _dot_nt(a, b):
    return lax.dot_general(a, b, (((1,), (1,)), ((), ())), preferred_element_type=F32)


def _bucket_thresholds():
    d = np.arange(0, 4 * REL_MAX_DIST)
    exact = REL_BUCKETS // 2
    large = exact + (np.log(np.maximum(d, 1).astype(np.float32) / np.float32(exact))
                     / np.float32(math.log(REL_MAX_DIST / exact)) * np.float32(REL_BUCKETS - exact)).astype(np.int32)
    b = np.where(d < exact, d, np.minimum(large, REL_BUCKETS - 1))
    assert np.all(np.diff(b) >= 0) and b[-1] == REL_BUCKETS - 1
    return [int(np.argmax(b >= k)) for k in range(REL_BUCKETS)]


_THRESH = _bucket_thresholds()


def _cparams(sem):
    return pltpu.CompilerParams(dimension_semantics=sem, vmem_limit_bytes=VMEM_LIMIT)


def _table_kernel(h0, a_col, d0, lo, hi, mult, rb_ref, out_ref):
    h = pl.program_id(0) + h0
    rt, cc = out_ref.shape[1], out_ref.shape[2]
    r = pl.program_id(1) * rt + lax.broadcasted_iota(jnp.int32, (rt, cc), 0)
    c = lax.broadcasted_iota(jnp.int32, (rt, cc), 1)
    dist = r + a_col * c + d0
    acc = jnp.full((rt, cc), rb_ref[h, 0], F32)
    for k in range(1, REL_BUCKETS):
        acc = jnp.where(dist >= _THRESH[k], rb_ref[h, k], acc)
    ok = (dist >= lo) & (dist < hi)
    out_ref[0] = jnp.where(ok, acc * mult, NEG)


def _bias_table(rel_bias, h0, nh, rows, cols, a_col, d0, lo, hi, rt, mult=1.0):
    return pl.pallas_call(
        functools.partial(_table_kernel, h0, a_col, d0, lo, hi, mult),
        grid=(nh, rows // rt),
        in_specs=[pl.BlockSpec(memory_space=pltpu.SMEM)],
        out_specs=pl.BlockSpec((1, rt, cols), lambda h, r: (h, r, 0)),
        out_shape=jax.ShapeDtypeStruct((nh, rows, cols), F32),
        compiler_params=_cparams(("arbitrary", "arbitrary")),
        name="bias_table",
    )(rel_bias)


def _rms(v, g):
    return v * lax.rsqrt(jnp.mean(v * v, axis=-1, keepdims=True) + NORM_EPS) * g


def _proj_kernel(x_ref, gpre_ref, w_ref, qn_ref, wuq_ref, kvn_ref, wk_ref, wv_ref,
                 rc_ref, rsa_ref, rsb_ref, eaug_ref,
                 qa_ref, cmp_ref, slck_ref, slcv_ref, win_ref, qm_ref, km_ref, vm_ref,
                 krb_ref, cq_ref, ckv_ref, z_ref):
    hb = _rms(x_ref[...], gpre_ref[...]).astype(BF16)

    def proj(c0, c1):
        return _dot(hb, w_ref[:, c0:c1])

    tm = hb.shape[0]
    lane = lax.broadcasted_iota(jnp.int32, (tm, LANES), 1)
    lo = lane < HEAD_DIM

    qa_ref[...] = (proj(C_AQ, C_CMP) * 0.125).astype(BF16)
    cmp_ref[...] = proj(C_CMP, C_SLC)
    slc = proj(C_SLC, C_WIN)
    slcv_ref[...] = slc.astype(BF16)
    slck_ref[...] = jnp.where(lo, slc, eaug_ref[...].astype(F32)).astype(BF16)
    win_ref[...] = proj(C_WIN, C_BCQ).astype(BF16)
    cq_ref[...] = (proj(C_CQ, C_CKV) * 0.125).astype(BF16)
    ckv_ref[...] = proj(C_CKV, C_Z).astype(BF16)
    z_ref[...] = proj(C_Z, C_END)

    rc, rsa, rsb = rc_ref[...], rsa_ref[...], rsb_ref[...]

    def rope(v):
        return v * rc + pltpu.roll(v, 16, 1) * rsa + pltpu.roll(v, 112, 1) * rsb

    cqn = _rms(proj(C_BCQ, C_BCKV), qn_ref[...]).astype(BF16)
    qm = _dot(cqn, wuq_ref[...])
    for h in range(MLA_HEADS):
        qm_ref[:, LANES * h:LANES * (h + 1)] = rope(qm[:, LANES * h:LANES * (h + 1)]).astype(BF16)

    krb = proj(C_KRB, C_CQ)
    krb_ref[...] = krb
    krr = jnp.where(lo | (lane >= HEAD_DIM + MLA_ROPE), 0.0, rope(krb))

    ckvn = _rms(proj(C_BCKV, C_KRB), kvn_ref[...]).astype(BF16)
    kn = _dot(ckvn, wk_ref[...])
    for h in range(MLA_HEADS):
        km_ref[:, LANES * h:LANES * (h + 1)] = (kn[:, LANES * h:LANES * (h + 1)] + krr).astype(BF16)
    vm_ref[...] = _dot(ckvn, wv_ref[...]).astype(BF16)


def _project(x2, gpre, w, qn, wuq, kvn, wk, wv, rope_c, rope_sa, rope_sb, eaug, T, tm):
    BT = x2.shape[0]
    nt = T // tm
    row = lambda i: (i, 0)
    fix = lambda i: (0, 0)
    pos = lambda i: (i % nt, 0)
    widths = [(256, BF16), (128, F32), (128, BF16), (128, BF16), (128, BF16), (512, BF16), (512, BF16),
              (256, BF16), (128, F32), (512, BF16), (128, BF16), (1024, F32)]
    return pl.pallas_call(
        _proj_kernel,
        grid=(BT // tm,),
        in_specs=[pl.BlockSpec((tm, D_MODEL), row), pl.BlockSpec((1, D_MODEL), fix),
                  pl.BlockSpec(w.shape, fix), pl.BlockSpec(qn.shape, fix), pl.BlockSpec(wuq.shape, fix),
                  pl.BlockSpec(kvn.shape, fix), pl.BlockSpec(wk.shape, fix), pl.BlockSpec(wv.shape, fix),
                  pl.BlockSpec((tm, LANES), pos), pl.BlockSpec((tm, LANES), pos), pl.BlockSpec((tm, LANES), pos),
                  pl.BlockSpec((tm, LANES), pos)],
        out_specs=[pl.BlockSpec((tm, wd), row) for wd, _ in widths],
        out_shape=[jax.ShapeDtypeStruct((BT, wd), dt) for wd, dt in widths],
        compiler_params=_cparams(("arbitrary",)),
        name="proj",
    )(x2, gpre, w, qn, wuq, kvn, wk, wv, rope_c, rope_sa, rope_sb, eaug)


def _compress_kernel(c_ref, pet_ref, peb_ref, w1t_ref, w1b_ref, w2_ref, o_ref):
    c = c_ref[0]
    nch = c.shape[0]
    top = (c + pet_ref[...]).astype(BF16)
    bot = (c + peb_ref[...]).astype(BF16)
    out = jnp.zeros((nch, LANES), F32)
    for i in range(2):
        pre = _dot(top, w1t_ref[i]) + pltpu.roll(_dot(bot, w1b_ref[i]), nch - 1, 0)
        hid = pre * (1.0 / (1.0 + jnp.exp(-pre)))
        out = out + _dot(hid.astype(BF16), w2_ref[i])
    o_ref[0] = out.astype(BF16)


def _compress(chunks, pet, peb, w1t, w1b, w2):
    B, nch, width = chunks.shape
    full = lambda a: pl.BlockSpec(a.shape, lambda b: (0,) * a.ndim)
    return pl.pallas_call(
        _compress_kernel,
        grid=(B,),
        in_specs=[pl.BlockSpec((1, nch, width), lambda b: (b, 0, 0)),
                  full(pet), full(peb), full(w1t), full(w1b), full(w2)],
        out_specs=pl.BlockSpec((1, nch, LANES), lambda b: (b, 0, 0)),
        out_shape=jax.ShapeDtypeStruct((B, nch, LANES), BF16),
        compiler_params=_cparams(("arbitrary",)),
        name="compress",
    )(chunks, pet, peb, w1t, w1b, w2)


def _head_slabs(qblk, n_heads):
    q = qblk.astype(F32)
    lo = lax.broadcasted_iota(jnp.int32, (q.shape[0], LANES), 1) < HEAD_DIM
    out = []
    for j in range(n_heads // 2):
        slab = q[:, LANES * j:LANES * (j + 1)]
        out.append(jnp.where(lo, slab, 0.0))
        out.append(jnp.where(lo, pltpu.roll(slab, HEAD_DIM, 1), 0.0))
    return out


def _merge_upper(accs):
    lo = lax.broadcasted_iota(jnp.int32, accs[0].shape, 1) < HEAD_DIM
    slabs = [jnp.where(lo, pltpu.roll(accs[2 * j], HEAD_DIM, 1), accs[2 * j + 1]) for j in range(len(accs) // 2)]
    return jnp.concatenate(slabs, axis=1)


def _lane_fold(p):
    parts = [p[:, LANES * i:LANES * (i + 1)] for i in range(p.shape[1] // LANES)]
    return functools.reduce(lambda a, b: a + b, parts)


def _cmp_kernel(q_ref, kvc_ref, bc_ref, ovt_ref, ocmp_ref, selb_ref):
    tq = q_ref.shape[1]
    t0 = pl.program_id(1) * tq
    kvc = kvc_ref[0]
    qs = _head_slabs(q_ref[0], NSA_HEADS)
    s_all = _dot_nt(jnp.concatenate([q.astype(BF16) for q in qs], axis=0), kvc)
    valid = bc_ref[0] > 0.5 * NEG
    psum = jnp.zeros(valid.shape, F32)
    ps = []
    for h in range(NSA_HEADS):
        s = s_all[h * tq:(h + 1) * tq] + bc_ref[h]
        m = jnp.max(s, axis=-1, keepdims=True)
        e = jnp.where(valid, jnp.exp(s - m), 0.0)
        l = jnp.sum(e, axis=-1, keepdims=True)
        p = e * (1.0 / jnp.where(l > 0.0, l, 1.0))
        psum = psum + p
        ps.append(p.astype(BF16))
    o_all = _dot(jnp.concatenate(ps, axis=0), kvc)
    ocmp_ref[0] = _merge_upper([o_all[h * tq:(h + 1) * tq] for h in range(NSA_HEADS)])

    hi = psum.astype(BF16)
    lo = (psum - hi.astype(F32)).astype(BF16)
    ovt = ovt_ref[...]
    imp = _dot_nt(ovt, hi) + _dot_nt(ovt, lo)
    ns = imp.shape[0]
    srow = lax.broadcasted_iota(jnp.int32, (ns, tq), 0)
    cur = (t0 + lax.broadcasted_iota(jnp.int32, (ns, tq), 1)) // SEL_BLOCK
    forced = (srow == 0) | (srow == cur) | (srow == cur - 1)
    x = jnp.where(forced, BIG, jnp.where(srow <= cur, imp, -BIG))
    cnt = jnp.zeros((ns, tq), F32)
    for sp in range(ns):
        other = x[sp:sp + 1, :]
        beats = (other > x) | ((other == x) & (srow > sp))
        cnt = cnt + jnp.where(beats, 1.0, 0.0)
    sel = (cnt < float(min(SEL_TOP, ns))) & (srow <= cur)
    sb = jnp.where(sel, 0.0, NEG)
    if ns < HEAD_DIM:
        sb = jnp.concatenate([sb, jnp.full((HEAD_DIM - ns, tq), NEG, F32)], axis=0)
    full = jnp.concatenate([jnp.zeros((HEAD_DIM, tq), F32), sb], axis=0)
    selb_ref[0] = full.T.astype(BF16)


def _cmp_attention(qa, kvc, bc, ovt, T):
    B = qa.shape[0]
    ncp = kvc.shape[1]
    return pl.pallas_call(
        _cmp_kernel,
        grid=(B, T // Q_TILE),
        in_specs=[pl.BlockSpec((1, Q_TILE, 256), lambda b, n: (b, n, 0)),
                  pl.BlockSpec((1, ncp, LANES), lambda b, n: (b, 0, 0)),
                  pl.BlockSpec((NSA_HEADS, Q_TILE, ncp), lambda b, n: (0, n, 0)),
                  pl.BlockSpec(ovt.shape, lambda b, n: (0, 0))],
        out_specs=[pl.BlockSpec((1, Q_TILE, 256), lambda b, n: (b, n, 0)),
                   pl.BlockSpec((1, Q_TILE, LANES), lambda b, n: (b, n, 0))],
        out_shape=[jax.ShapeDtypeStruct((B, T, 256), F32), jax.ShapeDtypeStruct((B, T, LANES), BF16)],
        compiler_params=_cparams(("arbitrary", "arbitrary")),
        name="cmp_select",
    )(qa, kvc, bc, ovt)


def _softmax_tile(s2, rows, m_s, l_s, shift=None):
    m_t = jnp.max(s2, axis=-1, keepdims=True)
    if shift is not None:
        m_t = m_t + shift
    m_old = m_s[rows]
    m_new = jnp.maximum(m_old, m_t)
    p = jnp.exp2(s2 - (m_new if shift is None else m_new - shift))
    alpha = jnp.exp2(m_old - m_new)
    l_s[rows] = alpha * l_s[rows] + _lane_fold(p)
    m_s[rows] = m_new
    return p.astype(BF16), alpha


def _slc_kernel(b31_ref, q_ref, selb_ref, ka_ref, kv_ref, tab_ref, o_ref, m_s, l_s, acc_s):
    tq = q_ref.shape[1]
    H = NSA_HEADS
    n = pl.program_id(1)
    qs = _head_slabs(q_ref[0], H)
    selb = selb_ref[0].astype(F32)
    qst = jnp.concatenate([(qs[h] + selb).astype(BF16) for h in range(H)], axis=0)
    m_s[...] = jnp.full(m_s.shape, -jnp.inf, F32)
    l_s[...] = jnp.zeros(l_s.shape, F32)
    acc_s[...] = jnp.zeros(acc_s.shape, F32)

    def step(k0, near_block):
        ka = ka_ref[0, pl.ds(k0, tq), :]
        kv = kv_ref[0, pl.ds(k0, tq), :]
        s_all = _dot_nt(qst, ka)
        ps, alphas = [], []
        for h in range(H):
            rows = slice(h * tq, (h + 1) * tq)
            s2 = s_all[rows] * LOG2E
            if near_block is None:
                p, alpha = _softmax_tile(s2, rows, m_s, l_s, shift=b31_ref[h])
            else:
                s2 = s2 + tab_ref[h, near_block * tq:(near_block + 1) * tq, :]
                p, alpha = _softmax_tile(s2, rows, m_s, l_s)
            ps.append(p)
            alphas.append(alpha)
        pv = _dot(jnp.concatenate(ps, axis=0), kv)
        for h in range(H):
            rows = slice(h * tq, (h + 1) * tq)
            acc_s[rows] = alphas[h] * acc_s[rows] + pv[rows]

    def far(j, carry):
        step(pl.multiple_of(j * tq, tq), None)
        return carry

    lax.fori_loop(0, jnp.maximum(n - 1, 0), far, 0)

    @pl.when(n >= 1)
    def _():
        step(pl.multiple_of(jnp.maximum(n - 1, 0) * tq, tq), 1)

    step(pl.multiple_of(n * tq, tq), 0)
    outs = []
    for h in range(H):
        rows = slice(h * tq, (h + 1) * tq)
        outs.append(acc_s[rows] / jnp.sum(l_s[rows], axis=-1, keepdims=True))
    o_ref[0] = _merge_upper(outs)


def _slc_attention(b31, qa, selb, slck, slcv, tab, T):
    B = qa.shape[0]
    tq = FLASH_TILE
    H = NSA_HEADS
    return pl.pallas_call(
        _slc_kernel,
        grid=(B, T // tq),
        in_specs=[pl.BlockSpec(memory_space=pltpu.SMEM),
                  pl.BlockSpec((1, tq, 256), lambda b, n: (b, n, 0)),
                  pl.BlockSpec((1, tq, LANES), lambda b, n: (b, n, 0)),
                  pl.BlockSpec((1, T, LANES), lambda b, n: (b, 0, 0)),
                  pl.BlockSpec((1, T, LANES), lambda b, n: (b, 0, 0)),
                  pl.BlockSpec(tab.shape, lambda b, n: (0, 0, 0))],
        out_specs=pl.BlockSpec((1, tq, 256), lambda b, n: (b, n, 0)),
        out_shape=jax.ShapeDtypeStruct((B, T, 256), F32),
        scratch_shapes=[pltpu.VMEM((H * tq, 1), F32), pltpu.VMEM((H * tq, LANES), F32),
                        pltpu.VMEM((H * tq, LANES), F32)],
        compiler_params=_cparams(("arbitrary", "arbitrary")),
        name="slc_attention",
    )(b31, qa, selb, slck, slcv, tab)


def _band_kernel(n_heads, window, has_sink, *refs):
    if has_sink:
        q_ref, kv_ref, tab_ref, sink_ref, o_ref = refs
    else:
        q_ref, kv_ref, tab_ref, o_ref = refs
    tq = q_ref.shape[1]
    span = window + tq
    q0 = pl.multiple_of(pl.program_id(1) * tq, tq)
    kv = kv_ref[0, pl.ds(q0, span), :]
    qs = _head_slabs(q_ref[0], n_heads)
    qst = jnp.concatenate([q.astype(BF16) for q in qs], axis=0)
    kvalid = lax.broadcasted_iota(jnp.int32, (n_heads * tq, span), 1) >= window - q0
    s = jnp.where(kvalid, _dot_nt(qst, kv) + tab_ref[...], NEG)
    m = jnp.max(s, axis=-1, keepdims=True)
    if has_sink:
        m = jnp.maximum(m, sink_ref[...])
    e = jnp.exp(s - m)
    l = jnp.sum(e, axis=-1, keepdims=True)
    if has_sink:
        l = l + jnp.exp(sink_ref[...] - m)
    o_all = _dot((e * (1.0 / l)).astype(BF16), kv)
    o_ref[0] = _merge_upper([o_all[h * tq:(h + 1) * tq] for h in range(n_heads)])


def _band_attention(q, kv_pad, tab, sinks, n_heads, window, T):
    B = q.shape[0]
    width = n_heads * HEAD_DIM
    has_sink = sinks is not None
    tab2 = tab.reshape(n_heads * Q_TILE, window + Q_TILE)
    in_specs = [pl.BlockSpec((1, Q_TILE, width), lambda b, n: (b, n, 0)),
                pl.BlockSpec((1, T + window, LANES), lambda b, n: (b, 0, 0)),
                pl.BlockSpec(tab2.shape, lambda b, n: (0, 0))]
    args = (q, kv_pad, tab2)
    if has_sink:
        col = jnp.repeat(sinks, Q_TILE)[:, None]
        in_specs.append(pl.BlockSpec(col.shape, lambda b, n: (0, 0)))
        args = args + (col,)
    return pl.pallas_call(
        functools.partial(_band_kernel, n_heads, window, has_sink),
        grid=(B, T // Q_TILE),
        in_specs=in_specs,
        out_specs=pl.BlockSpec((1, Q_TILE, width), lambda b, n: (b, n, 0)),
        out_shape=jax.ShapeDtypeStruct((B, T, width), F32),
        compiler_params=_cparams(("arbitrary", "arbitrary")),
        name="band_sink" if has_sink else "band_window",
    )(*args)


def _mla_kernel(q_ref, k_ref, v_ref, o_ref, m_s, l_s, acc_s):
    tq = q_ref.shape[1]
    H = MLA_HEADS
    n = pl.program_id(1)
    c2 = float((MLA_NOPE + MLA_ROPE) ** -0.5) * LOG2E
    m_s[...] = jnp.full(m_s.shape, -jnp.inf, F32)
    l_s[...] = jnp.zeros(l_s.shape, F32)
    acc_s[...] = jnp.zeros(acc_s.shape, F32)

    def step(k0, diag):
        ss = [_dot_nt(q_ref[0, :, LANES * h:LANES * (h + 1)], k_ref[0, pl.ds(k0, tq), LANES * h:LANES * (h + 1)])
              for h in range(H)]
        ps, alphas = [], []
        for h in range(H):
            rows = slice(h * tq, (h + 1) * tq)
            s2 = ss[h] * c2
            if diag:
                row = lax.broadcasted_iota(jnp.int32, (tq, tq), 0)
                col = lax.broadcasted_iota(jnp.int32, (tq, tq), 1)
                s2 = jnp.where(col <= row, s2, NEG)
            p, alpha = _softmax_tile(s2, rows, m_s, l_s)
            ps.append(p)
            alphas.append(alpha)
        for h in range(H):
            rows = slice(h * tq, (h + 1) * tq)
            v = v_ref[0, pl.ds(k0, tq), LANES * (h // 2):LANES * (h // 2 + 1)]
            acc_s[rows] = alphas[h] * acc_s[rows] + _dot(ps[h], v)

    def far(j, carry):
        step(pl.multiple_of(j * tq, tq), False)
        return carry

    lax.fori_loop(0, n, far, 0)
    step(pl.multiple_of(n * tq, tq), True)
    lo = lax.broadcasted_iota(jnp.int32, (tq, LANES), 1) < HEAD_DIM
    outs = []
    for h in range(H):
        rows = slice(h * tq, (h + 1) * tq)
        outs.append(acc_s[rows] / jnp.sum(l_s[rows], axis=-1, keepdims=True))
    o_ref[0] = jnp.concatenate([jnp.where(lo, outs[2 * j], outs[2 * j + 1]) for j in range(H // 2)], axis=1)


def _mla_attention(qm, km, vm, T):
    B = qm.shape[0]
    tq = FLASH_TILE
    H = MLA_HEADS
    return pl.pallas_call(
        _mla_kernel,
        grid=(B, T // tq),
        in_specs=[pl.BlockSpec((1, tq, 512), lambda b, n: (b, n, 0)),
                  pl.BlockSpec((1, T, 512), lambda b, n: (b, 0, 0)),
                  pl.BlockSpec((1, T, 256), lambda b, n: (b, 0, 0))],
        out_specs=pl.BlockSpec((1, tq, 256), lambda b, n: (b, n, 0)),
        out_shape=jax.ShapeDtypeStruct((B, T, 256), F32),
        scratch_shapes=[pltpu.VMEM((H * tq, 1), F32), pltpu.VMEM((H * tq, LANES), F32),
                        pltpu.VMEM((H * tq, LANES), F32)],
        compiler_params=_cparams(("arbitrary", "arbitrary")),
        name="mla_attention",
    )(qm, km, vm)


def _out_kernel(x_ref, ocmp_ref, oslc_ref, owin_ref, krb_ref, ob_ref, oc_ref, z_ref, w_ref, gpost_ref, o_ref):
    tm = x_ref.shape[0]
    sig = 1.0 / (1.0 + jnp.exp(-krb_ref[...]))
    lo = lax.broadcasted_iota(jnp.int32, (tm, LANES), 1) < HEAD_DIM

    def gate(h, j):
        c = GATE_LANE0 + 3 * h + j
        return jnp.broadcast_to(sig[:, c:c + 1], (tm, LANES))

    branches = (ocmp_ref, oslc_ref, owin_ref)
    slabs = []
    for p in range(NSA_HEADS // 2):
        acc = jnp.zeros((tm, LANES), F32)
        for j in range(3):
            g = jnp.where(lo, gate(2 * p, j), gate(2 * p + 1, j))
            term = g * branches[j][:, LANES * p:LANES * (p + 1)]
            acc = term if j == 0 else acc + term
        slabs.append(acc)
    z = z_ref[...]
    mixed = jnp.concatenate(slabs + [ob_ref[...], oc_ref[...]], axis=1) * (z * (1.0 / (1.0 + jnp.exp(-z))))
    y = _dot(mixed.astype(BF16), w_ref[...])
    o_ref[...] = x_ref[...] + _rms(y, gpost_ref[...])


def _out_project(x2, ocmp, oslc, owin, krb, ob, oc, z, w, gpost, tm):
    BT = x2.shape[0]
    row = lambda i: (i, 0)
    fix = lambda i: (0, 0)
    spec = lambda a: pl.BlockSpec((tm, a.shape[1]), row)
    return pl.pallas_call(
        _out_kernel,
        grid=(BT // tm,),
        in_specs=[spec(x2), spec(ocmp), spec(oslc), spec(owin), spec(krb), spec(ob), spec(oc), spec(z),
                  pl.BlockSpec(w.shape, fix), pl.BlockSpec(gpost.shape, fix)],
        out_specs=pl.BlockSpec((tm, D_MODEL), row),
        out_shape=jax.ShapeDtypeStruct((BT, D_MODEL), F32),
        compiler_params=_cparams(("arbitrary",)),
        name="out_proj",
    )(x2, ocmp, oslc, owin, krb, ob, oc, z, w, gpost)


def _split_in(w):
    outs, o = [], 0
    for s in IN_SIZES:
        outs.append(w[..., o:o + s])
        o += s
    return outs


def _prep_w_in(w):
    a_q, a_kv, a_g, b_cq, b_ckv, b_kr, c_q, c_kv, z = _split_in(w)
    zeros = lambda n: jnp.zeros(w.shape[:-1] + (n,), w.dtype)
    krb = jnp.concatenate([zeros(HEAD_DIM), b_kr, a_g, zeros(LANES - HEAD_DIM - MLA_ROPE - 3 * NSA_HEADS)], axis=-1)
    return jnp.concatenate([a_q, a_kv, b_cq, b_ckv, krb, c_q, c_kv, z], axis=-1).astype(BF16)


def _prep_mla(w_uq, w_ukv):
    L = w_uq.shape[0]
    dq = MLA_NOPE + MLA_ROPE
    uq = w_uq.reshape(L, MLA_Q_RANK, MLA_HEADS, dq)
    uq = jnp.concatenate([uq, jnp.zeros((L, MLA_Q_RANK, MLA_HEADS, LANES - dq), uq.dtype)], axis=-1)
    ukv = w_ukv.reshape(L, MLA_KV_RANK, MLA_HEADS, MLA_NOPE + MLA_V)
    uk = jnp.concatenate([ukv[..., :MLA_NOPE], jnp.zeros((L, MLA_KV_RANK, MLA_HEADS, LANES - MLA_NOPE), ukv.dtype)], axis=-1)
    uv = ukv[..., MLA_NOPE:]
    return (uq.reshape(L, MLA_Q_RANK, MLA_HEADS * LANES).astype(BF16),
            uk.reshape(L, MLA_KV_RANK, MLA_HEADS * LANES).astype(BF16),
            uv.reshape(L, MLA_KV_RANK, MLA_HEADS * MLA_V).astype(BF16))


def _prep_compress(cmp_pos, cmp_w1, cmp_w2):
    L = cmp_pos.shape[0]
    pe = jnp.concatenate([cmp_pos[:, 0], cmp_pos[:, 1]], axis=-1)
    pet = pe[:, :CMP_STRIDE].reshape(L, 1, CMP_STRIDE * LANES)
    peb = pe[:, CMP_STRIDE:].reshape(L, 1, CMP_STRIDE * LANES)
    w1 = cmp_w1.reshape(L, 2, CMP_LEN, HEAD_DIM, CMP_HIDDEN)
    zero = jnp.zeros_like(w1[:, 0])
    w1k = jnp.concatenate([w1[:, 0], zero], axis=2)
    w1v = jnp.concatenate([zero, w1[:, 1]], axis=2)
    w1e = jnp.stack([w1k, w1v], axis=1)
    w1t = w1e[:, :, :CMP_STRIDE].reshape(L, 2, CMP_STRIDE * LANES, CMP_HIDDEN).astype(BF16)
    w1b = w1e[:, :, CMP_STRIDE:].reshape(L, 2, CMP_STRIDE * LANES, CMP_HIDDEN).astype(BF16)
    z2 = jnp.zeros_like(cmp_w2[:, 0])
    w2 = jnp.stack([jnp.concatenate([cmp_w2[:, 0], z2], axis=-1),
                    jnp.concatenate([z2, cmp_w2[:, 1]], axis=-1)], axis=1).astype(BF16)
    return pet, peb, w1t, w1b, w2


def _rope_tables(T):
    half = MLA_ROPE // 2
    inv = ROPE_THETA ** (-jnp.arange(half, dtype=F32) / half)
    ang = jnp.arange(T).astype(F32)[:, None] * inv[None, :]
    cos, sin = jnp.cos(ang), jnp.sin(ang)
    z = lambda n: jnp.zeros((T, n), F32)
    rc = jnp.concatenate([jnp.ones((T, MLA_NOPE), F32), cos, cos, z(LANES - MLA_NOPE - MLA_ROPE)], axis=1)
    rsa = jnp.concatenate([z(MLA_NOPE + half), sin, z(LANES - MLA_NOPE - MLA_ROPE)], axis=1)
    rsb = jnp.concatenate([z(MLA_NOPE), -sin, z(LANES - MLA_NOPE - half)], axis=1)
    return rc, rsa, rsb


def _static_tables(T):
    ns = T // SEL_BLOCK
    nc = T // CMP_STRIDE - 1
    ncp = T // CMP_STRIDE
    sstart = np.arange(ns) * SEL_BLOCK
    cstart = np.arange(nc) * CMP_STRIDE
    overlap = (np.clip(np.minimum(cstart[:, None] + CMP_LEN, sstart[None, :] + SEL_BLOCK)
                       - np.maximum(cstart[:, None], sstart[None, :]), 0, None) / CMP_STRIDE).astype(np.float32)
    ovt = np.zeros((ns, ncp), np.float32)
    ovt[:, :nc] = overlap.T
    eaug = np.zeros((T, LANES), np.float32)
    eaug[np.arange(T), HEAD_DIM + np.arange(T) // SEL_BLOCK] = 1.0
    return jnp.asarray(ovt, BF16), jnp.asarray(eaug, BF16)


def kernel(x, w_in, w_out, norm_pre, norm_post, cmp_pos, cmp_w1, cmp_w2, mla_q_norm, mla_w_uq, mla_kv_norm,
           mla_w_ukv, swa_sinks, rel_bias):
    B, T, D = x.shape
    depth = w_in.shape[0]
    assert D == D_MODEL and T % (2 * FLASH_TILE) == 0 and T // SEL_BLOCK <= HEAD_DIM
    ncp = T // CMP_STRIDE
    tm = 256

    w_in_p = _prep_w_in(w_in)
    wuq, wk, wv = _prep_mla(mla_w_uq, mla_w_ukv)
    pet, peb, w1t, w1b, w2 = _prep_compress(cmp_pos, cmp_w1, cmp_w2)
    w_out_b = w_out.astype(BF16)
    rope_c, rope_sa, rope_sb = _rope_tables(T)
    ovt, eaug = _static_tables(T)

    bc = _bias_table(rel_bias, 0, NSA_HEADS, T, ncp, -CMP_STRIDE, -(CMP_LEN - 1), 0, 1 << 30, 256)
    tab_near = _bias_table(rel_bias, 0, NSA_HEADS, 2 * FLASH_TILE, FLASH_TILE, -1, 0, 0, 1 << 30, 256, mult=LOG2E)
    tab_win = _bias_table(rel_bias, 0, NSA_HEADS, Q_TILE, NSA_WINDOW + Q_TILE, -1, NSA_WINDOW, 0, NSA_WINDOW, Q_TILE)
    tab_swa = _bias_table(rel_bias, NSA_HEADS, SWA_HEADS, Q_TILE, SWA_WINDOW + Q_TILE, -1, SWA_WINDOW, 0, SWA_WINDOW, Q_TILE)
    b31 = rel_bias[:NSA_HEADS, REL_BUCKETS - 1] * LOG2E

    x2 = x.reshape(B * T, D)
    for l in range(depth):
        (qa, cmpkv, slck, slcv, win, qm, km, vm, krb, cq, ckv, z) = _project(
            x2, norm_pre[l][None], w_in_p[l], mla_q_norm[l][None], wuq[l], mla_kv_norm[l][None], wk[l], wv[l],
            rope_c, rope_sa, rope_sb, eaug, T, tm)
        r3 = lambda a: a.reshape(B, T, a.shape[-1])
        kvc = _compress(cmpkv.reshape(B, ncp, CMP_STRIDE * LANES), pet[l], peb[l], w1t[l], w1b[l], w2[l])
        qa3 = r3(qa)
        ocmp, selb = _cmp_attention(qa3, kvc, bc, ovt, T)
        oslc = _slc_attention(b31, qa3, selb, r3(slck), r3(slcv), tab_near, T)
        win_pad = jnp.pad(r3(win), ((0, 0), (NSA_WINDOW, 0), (0, 0)))
        owin = _band_attention(qa3, win_pad, tab_win, None, NSA_HEADS, NSA_WINDOW, T)
        ckv_pad = jnp.pad(r3(ckv), ((0, 0), (SWA_WINDOW, 0), (0, 0)))
        oc = _band_attention(r3(cq), ckv_pad, tab_swa, swa_sinks[l], SWA_HEADS, SWA_WINDOW, T)
        ob = _mla_attention(r3(qm), r3(km), r3(vm), T)
        flat = lambda a: a.reshape(B * T, a.shape[-1])
        x2 = _out_project(x2, flat(ocmp), flat(oslc), flat(owin), krb, flat(ob), flat(oc), z,
                          w_out_b[l], norm_post[l][None], tm)
    return x2.reshape(B, T, D)
```

```python
import functools
import math

import numpy as np
import jax
import jax.numpy as jnp
from jax import lax
from jax.experimental import pallas as pl
from jax.experimental.pallas import tpu as pltpu

F32 = jnp.float32
BF16 = jnp.bfloat16

D_MODEL = 1024
HEAD_DIM = 64
NSA_HEADS = 4
CMP_LEN = 32
CMP_STRIDE = 16
CMP_HIDDEN = 128
SEL_BLOCK = 64
SEL_TOP = 16
NSA_WINDOW = 512
MLA_HEADS = 4
MLA_Q_RANK = 256
MLA_KV_RANK = 128
MLA_NOPE = 64
MLA_ROPE = 32
MLA_V = 64
ROPE_THETA = 10000.0
SWA_HEADS = 8
SWA_WINDOW = 128
REL_BUCKETS = 32
REL_MAX_DIST = 512
NORM_EPS = 1e-6
NEG = -1e30
BIG = 1e9
LOG2E = math.log2(math.e)
IN_SIZES = (256, 384, 12, 256, 128, 32, 512, 128, 1024)

LANES = 128
Q_TILE = 128
BAND_GROUP = 4
FLASH_TILE = 512
GATE_LANE0 = 96
VMEM_LIMIT = 56 * 1024 * 1024

C_AQ, C_CMP, C_SLC, C_WIN, C_BCQ, C_BCKV, C_KRB, C_CQ, C_CKV, C_Z, C_END = (
    0, 256, 384, 512, 640, 896, 1024, 1152, 1664, 1792, 2816)


def _dot(a, b):
    return jnp.dot(a, b, preferred_element_type=F32)


def _dot_nt(a, b):
    return lax.dot_general(a, b, (((1,), (1,)), ((), ())), preferred_element_type=F32)


def _bucket_thresholds():
    d = np.arange(0, 4 * REL_MAX_DIST)
    exact = REL_BUCKETS // 2
    large = exact + (np.log(np.maximum(d, 1).astype(np.float32) / np.float32(exact))
                     / np.float32(math.log(REL_MAX_DIST / exact)) * np.float32(REL_BUCKETS - exact)).astype(np.int32)
    b = np.where(d < exact, d, np.minimum(large, REL_BUCKETS - 1))
    assert np.all(np.diff(b) >= 0) and b[-1] == REL_BUCKETS - 1
    return [int(np.argmax(b >= k)) for k in range(REL_BUCKETS)]


_THRESH = _bucket_thresholds()


def _cparams(sem):
    return pltpu.CompilerParams(dimension_semantics=sem, vmem_limit_bytes=VMEM_LIMIT)


def _table_kernel(h0, a_col, d0, lo, hi, mult, rb_ref, out_ref):
    h = pl.program_id(0) + h0
    rt, cc = out_ref.shape[1], out_ref.shape[2]
    r = pl.program_id(1) * rt + lax.broadcasted_iota(jnp.int32, (rt, cc), 0)
    c = lax.broadcasted_iota(jnp.int32, (rt, cc), 1)
    dist = r + a_col * c + d0
    acc = jnp.full((rt, cc), rb_ref[h, 0], F32)
    for k in range(1, REL_BUCKETS):
        acc = jnp.where(dist >= _THRESH[k], rb_ref[h, k], acc)
    ok = (dist >= lo) & (dist < hi)
    out_ref[0] = jnp.where(ok, acc * mult, NEG)


def _bias_table(rel_bias, h0, nh, rows, cols, a_col, d0, lo, hi, rt, mult=1.0):
    return pl.pallas_call(
        functools.partial(_table_kernel, h0, a_col, d0, lo, hi, mult),
        grid=(nh, rows // rt),
        in_specs=[pl.BlockSpec(memory_space=pltpu.SMEM)],
        out_specs=pl.BlockSpec((1, rt, cols), lambda h, r: (h, r, 0)),
        out_shape=jax.ShapeDtypeStruct((nh, rows, cols), F32),
        compiler_params=_cparams(("arbitrary", "arbitrary")),
        name="bias_table",
    )(rel_bias)


def _rms(v, g):
    return v * lax.rsqrt(jnp.mean(v * v, axis=-1, keepdims=True) + NORM_EPS) * g


def _proj_kernel(x_ref, gpre_ref, w_ref, qn_ref, wuq_ref, kvn_ref, wk_ref, wv_ref,
                 rc_ref, rsa_ref, rsb_ref, eaug_ref,
                 qa_ref, cmp_ref, slck_ref, slcv_ref, win_ref, qm_ref, km_ref, vm_ref,
                 krb_ref, cq_ref, ckv_ref, z_ref):
    hb = _rms(x_ref[...], gpre_ref[...]).astype(BF16)

    def proj(c0, c1):
        return _dot(hb, w_ref[:, c0:c1])

    tm = hb.shape[0]
    lane = lax.broadcasted_iota(jnp.int32, (tm, LANES), 1)
    lo = lane < HEAD_DIM

    qa_ref[...] = (proj(C_AQ, C_CMP) * 0.125).astype(BF16)
    cmp_ref[...] = proj(C_CMP, C_SLC)
    slc = proj(C_SLC, C_WIN)
    slcv_ref[...] = jnp.where(lo, 1.0, slc).astype(BF16)
    slck_ref[...] = jnp.where(lo, slc, eaug_ref[...].astype(F32)).astype(BF16)
    win_ref[...] = proj(C_WIN, C_BCQ).astype(BF16)
    cq_ref[...] = (proj(C_CQ, C_CKV) * 0.125).astype(BF16)
    ckv_ref[...] = proj(C_CKV, C_Z).astype(BF16)
    z_ref[...] = proj(C_Z, C_END)

    rc, rsa, rsb = rc_ref[...], rsa_ref[...], rsb_ref[...]

    def rope(v):
        return v * rc + pltpu.roll(v, 16, 1) * rsa + pltpu.roll(v, 112, 1) * rsb

    cqn = _rms(proj(C_BCQ, C_BCKV), qn_ref[...]).astype(BF16)
    qm = _dot(cqn, wuq_ref[...])
    for h in range(MLA_HEADS):
        qm_ref[:, LANES * h:LANES * (h + 1)] = rope(qm[:, LANES * h:LANES * (h + 1)]).astype(BF16)

    krb = proj(C_KRB, C_CQ)
    krb_ref[...] = krb
    krr = jnp.where(lo | (lane >= HEAD_DIM + MLA_ROPE), 0.0, rope(krb))

    ckvn = _rms(proj(C_BCKV, C_KRB), kvn_ref[...]).astype(BF16)
    kn = _dot(ckvn, wk_ref[...])
    for h in range(MLA_HEADS):
        km_ref[:, LANES * h:LANES * (h + 1)] = (kn[:, LANES * h:LANES * (h + 1)] + krr).astype(BF16)
    vm = _dot(ckvn, wv_ref[...])
    for h in range(MLA_HEADS):
        vm_ref[:, LANES * h:LANES * (h + 1)] = jnp.where(lo, 1.0, vm[:, LANES * h:LANES * (h + 1)]).astype(BF16)


def _project(x2, gpre, w, qn, wuq, kvn, wk, wv, rope_c, rope_sa, rope_sb, eaug, T, tm):
    BT = x2.shape[0]
    nt = T // tm
    row = lambda i: (i, 0)
    fix = lambda i: (0, 0)
    pos = lambda i: (i % nt, 0)
    widths = [(256, BF16), (128, F32), (128, BF16), (128, BF16), (128, BF16), (512, BF16), (512, BF16),
              (512, BF16), (128, F32), (512, BF16), (128, BF16), (1024, F32)]
    return pl.pallas_call(
        _proj_kernel,
        grid=(BT // tm,),
        in_specs=[pl.BlockSpec((tm, D_MODEL), row), pl.BlockSpec((1, D_MODEL), fix),
                  pl.BlockSpec(w.shape, fix), pl.BlockSpec(qn.shape, fix), pl.BlockSpec(wuq.shape, fix),
                  pl.BlockSpec(kvn.shape, fix), pl.BlockSpec(wk.shape, fix), pl.BlockSpec(wv.shape, fix),
                  pl.BlockSpec((tm, LANES), pos), pl.BlockSpec((tm, LANES), pos), pl.BlockSpec((tm, LANES), pos),
                  pl.BlockSpec((tm, LANES), pos)],
        out_specs=[pl.BlockSpec((tm, wd), row) for wd, _ in widths],
        out_shape=[jax.ShapeDtypeStruct((BT, wd), dt) for wd, dt in widths],
        compiler_params=_cparams(("arbitrary",)),
        name="proj",
    )(x2, gpre, w, qn, wuq, kvn, wk, wv, rope_c, rope_sa, rope_sb, eaug)


def _compress_kernel(c_ref, pet_ref, peb_ref, w1t_ref, w1b_ref, w2_ref, o_ref):
    c = c_ref[0]
    nch = c.shape[0]
    top = (c + pet_ref[...]).astype(BF16)
    bot = (c + peb_ref[...]).astype(BF16)
    out = jnp.zeros((nch, LANES), F32)
    for i in range(2):
        pre = _dot(top, w1t_ref[i]) + pltpu.roll(_dot(bot, w1b_ref[i]), nch - 1, 0)
        hid = pre * (1.0 / (1.0 + jnp.exp(-pre)))
        out = out + _dot(hid.astype(BF16), w2_ref[i])
    o_ref[0] = out.astype(BF16)


def _compress(chunks, pet, peb, w1t, w1b, w2):
    B, nch, width = chunks.shape
    full = lambda a: pl.BlockSpec(a.shape, lambda b: (0,) * a.ndim)
    return pl.pallas_call(
        _compress_kernel,
        grid=(B,),
        in_specs=[pl.BlockSpec((1, nch, width), lambda b: (b, 0, 0)),
                  full(pet), full(peb), full(w1t), full(w1b), full(w2)],
        out_specs=pl.BlockSpec((1, nch, LANES), lambda b: (b, 0, 0)),
        out_shape=jax.ShapeDtypeStruct((B, nch, LANES), BF16),
        compiler_params=_cparams(("arbitrary",)),
        name="compress",
    )(chunks, pet, peb, w1t, w1b, w2)


def _head_slabs(qblk, n_heads):
    q = qblk.astype(F32)
    lo = lax.broadcasted_iota(jnp.int32, (q.shape[0], LANES), 1) < HEAD_DIM
    out = []
    for j in range(n_heads // 2):
        slab = q[:, LANES * j:LANES * (j + 1)]
        out.append(jnp.where(lo, slab, 0.0))
        out.append(jnp.where(lo, pltpu.roll(slab, HEAD_DIM, 1), 0.0))
    return out


def _merge_upper(accs):
    lo = lax.broadcasted_iota(jnp.int32, accs[0].shape, 1) < HEAD_DIM
    slabs = [jnp.where(lo, pltpu.roll(accs[2 * j], HEAD_DIM, 1), accs[2 * j + 1]) for j in range(len(accs) // 2)]
    return jnp.concatenate(slabs, axis=1)


def _merge_normalized(accs):
    lo = lax.broadcasted_iota(jnp.int32, accs[0].shape, 1) < HEAD_DIM
    slabs = []
    for j in range(len(accs) // 2):
        a, b = accs[2 * j], accs[2 * j + 1]
        slabs.append(jnp.where(lo, pltpu.roll(a, HEAD_DIM, 1) / a, b / pltpu.roll(b, HEAD_DIM, 1)))
    return jnp.concatenate(slabs, axis=1)


def _row_max_lanes(s):
    return jnp.broadcast_to(jnp.max(s, axis=-1, keepdims=True), (s.shape[0], LANES))


def _sub_lanes(s, ref):
    return jnp.concatenate([s[:, LANES * i:LANES * (i + 1)] - ref for i in range(s.shape[1] // LANES)], axis=1)


def _cmp_kernel(q_ref, kvc_ref, bc_ref, ovt_ref, ocmp_ref, selb_ref):
    tq = q_ref.shape[1]
    t0 = pl.program_id(1) * tq
    kvc = kvc_ref[0]
    qs = _head_slabs(q_ref[0], NSA_HEADS)
    s_all = _dot_nt(jnp.concatenate([q.astype(BF16) for q in qs], axis=0), kvc)
    valid = bc_ref[0] > 0.5 * NEG
    psum = jnp.zeros(valid.shape, F32)
    ps = []
    for h in range(NSA_HEADS):
        s = s_all[h * tq:(h + 1) * tq] + bc_ref[h]
        m = jnp.max(s, axis=-1, keepdims=True)
        e = jnp.where(valid, jnp.exp(s - m), 0.0)
        l = jnp.sum(e, axis=-1, keepdims=True)
        p = e * (1.0 / jnp.where(l > 0.0, l, 1.0))
        psum = psum + p
        ps.append(p.astype(BF16))
    o_all = _dot(jnp.concatenate(ps, axis=0), kvc)
    ocmp_ref[0] = _merge_upper([o_all[h * tq:(h + 1) * tq] for h in range(NSA_HEADS)])

    hi = psum.astype(BF16)
    lo = (psum - hi.astype(F32)).astype(BF16)
    ovt = ovt_ref[...]
    imp = _dot_nt(ovt, hi) + _dot_nt(ovt, lo)
    ns = imp.shape[0]
    srow = lax.broadcasted_iota(jnp.int32, (ns, tq), 0)
    cur = (t0 + lax.broadcasted_iota(jnp.int32, (ns, tq), 1)) // SEL_BLOCK
    forced = (srow == 0) | (srow == cur) | (srow == cur - 1)
    x = jnp.where(forced, BIG, jnp.where(srow <= cur, imp, -BIG))
    cnt = jnp.zeros((ns, tq), F32)
    for sp in range(ns):
        other = x[sp:sp + 1, :]
        beats = (other > x) | ((other == x) & (srow > sp))
        cnt = cnt + jnp.where(beats, 1.0, 0.0)
    sel = (cnt < float(min(SEL_TOP, ns))) & (srow <= cur)
    sb = jnp.where(sel, 0.0, NEG)
    if ns < HEAD_DIM:
        sb = jnp.concatenate([sb, jnp.full((HEAD_DIM - ns, tq), NEG, F32)], axis=0)
    full = jnp.concatenate([jnp.zeros((HEAD_DIM, tq), F32), sb], axis=0)
    selb_ref[0] = full.T.astype(BF16)


def _cmp_attention(qa, kvc, bc, ovt, T):
    B = qa.shape[0]
    ncp = kvc.shape[1]
    return pl.pallas_call(
        _cmp_kernel,
        grid=(B, T // Q_TILE),
        in_specs=[pl.BlockSpec((1, Q_TILE, 256), lambda b, n: (b, n, 0)),
                  pl.BlockSpec((1, ncp, LANES), lambda b, n: (b, 0, 0)),
                  pl.BlockSpec((NSA_HEADS, Q_TILE, ncp), lambda b, n: (0, n, 0)),
                  pl.BlockSpec(ovt.shape, lambda b, n: (0, 0))],
        out_specs=[pl.BlockSpec((1, Q_TILE, 256), lambda b, n: (b, n, 0)),
                   pl.BlockSpec((1, Q_TILE, LANES), lambda b, n: (b, n, 0))],
        out_shape=[jax.ShapeDtypeStruct((B, T, 256), F32), jax.ShapeDtypeStruct((B, T, LANES), BF16)],
        compiler_params=_cparams(("arbitrary", "arbitrary")),
        name="cmp_select",
    )(qa, kvc, bc, ovt)


def _softmax_tile(s2, rows, m_s, shift=None):
    m_t = _row_max_lanes(s2)
    if shift is not None:
        m_t = m_t + shift
    m_old = m_s[rows]
    m_new = jnp.maximum(m_old, m_t)
    p = jnp.exp2(_sub_lanes(s2, m_new if shift is None else m_new - shift))
    alpha = jnp.exp2(m_old - m_new)
    m_s[rows] = m_new
    return p.astype(BF16), alpha


def _slc_kernel(b31_ref, q_ref, selb_ref, ka_ref, kv_ref, tab_ref, o_ref, m_s, acc_s):
    tq = q_ref.shape[1]
    H = NSA_HEADS
    n = pl.program_id(1)
    qs = _head_slabs(q_ref[0], H)
    selb = selb_ref[0].astype(F32)
    qst = jnp.concatenate([(qs[h] + selb).astype(BF16) for h in range(H)], axis=0)
    m_s[...] = jnp.full(m_s.shape, -jnp.inf, F32)
    acc_s[...] = jnp.zeros(acc_s.shape, F32)

    def step(k0, near_block):
        ka = ka_ref[0, pl.ds(k0, tq), :]
        kv = kv_ref[0, pl.ds(k0, tq), :]
        s_all = _dot_nt(qst, ka)
        ps, alphas = [], []
        for h in range(H):
            rows = slice(h * tq, (h + 1) * tq)
            s2 = s_all[rows] * LOG2E
            if near_block is None:
                p, alpha = _softmax_tile(s2, rows, m_s, shift=b31_ref[h])
            else:
                s2 = s2 + tab_ref[h, near_block * tq:(near_block + 1) * tq, :]
                p, alpha = _softmax_tile(s2, rows, m_s)
            ps.append(p)
            alphas.append(alpha)
        pv = _dot(jnp.concatenate(ps, axis=0), kv)
        for h in range(H):
            rows = slice(h * tq, (h + 1) * tq)
            acc_s[rows] = alphas[h] * acc_s[rows] + pv[rows]

    def far(j, carry):
        step(pl.multiple_of(j * tq, tq), None)
        return carry

    lax.fori_loop(0, jnp.maximum(n - 1, 0), far, 0)

    @pl.when(n >= 1)
    def _():
        step(pl.multiple_of(jnp.maximum(n - 1, 0) * tq, tq), 1)

    step(pl.multiple_of(n * tq, tq), 0)
    o_ref[0] = _merge_normalized([acc_s[h * tq:(h + 1) * tq] for h in range(H)])


def _slc_attention(b31, qa, selb, slck, slcv, tab, T):
    B = qa.shape[0]
    tq = FLASH_TILE
    H = NSA_HEADS
    return pl.pallas_call(
        _slc_kernel,
        grid=(B, T // tq),
        in_specs=[pl.BlockSpec(memory_space=pltpu.SMEM),
                  pl.BlockSpec((1, tq, 256), lambda b, n: (b, n, 0)),
                  pl.BlockSpec((1, tq, LANES), lambda b, n: (b, n, 0)),
                  pl.BlockSpec((1, T, LANES), lambda b, n: (b, 0, 0)),
                  pl.BlockSpec((1, T, LANES), lambda b, n: (b, 0, 0)),
                  pl.BlockSpec(tab.shape, lambda b, n: (0, 0, 0))],
        out_specs=pl.BlockSpec((1, tq, 256), lambda b, n: (b, n, 0)),
        out_shape=jax.ShapeDtypeStruct((B, T, 256), F32),
        scratch_shapes=[pltpu.VMEM((H * tq, LANES), F32), pltpu.VMEM((H * tq, LANES), F32)],
        compiler_params=_cparams(("arbitrary", "arbitrary")),
        name="slc_attention",
    )(b31, qa, selb, slck, slcv, tab)


def _band_kernel(n_heads, window, has_sink, *refs):
    if has_sink:
        q_ref, kv_ref, tab_ref, sink_ref, o_ref = refs
    else:
        q_ref, kv_ref, tab_ref, o_ref = refs
    tq = Q_TILE
    span = window + tq
    lo_kv = lax.broadcasted_iota(jnp.int32, (span, LANES), 1) < HEAD_DIM
    lo_o = lax.broadcasted_iota(jnp.int32, (n_heads * tq, LANES), 1) < HEAD_DIM
    for g in range(BAND_GROUP):
        q0 = pl.multiple_of((pl.program_id(1) * BAND_GROUP + g) * tq, tq)
        kv = kv_ref[0, pl.ds(q0, span), :]
        qs = _head_slabs(q_ref[0, g * tq:(g + 1) * tq, :], n_heads)
        qst = jnp.concatenate([q.astype(BF16) for q in qs], axis=0)
        kvalid = lax.broadcasted_iota(jnp.int32, (n_heads * tq, span), 1) >= window - q0
        s = jnp.where(kvalid, _dot_nt(qst, kv) + tab_ref[...], NEG)
        m = _row_max_lanes(s)
        if has_sink:
            m = jnp.maximum(m, sink_ref[...])
        e = jnp.exp(_sub_lanes(s, m))
        acc = _dot(e.astype(BF16), jnp.where(lo_kv, 1.0, kv).astype(BF16))
        if has_sink:
            acc = acc + jnp.where(lo_o, jnp.exp(sink_ref[...] - m), 0.0)
        o_ref[0, g * tq:(g + 1) * tq, :] = _merge_normalized([acc[h * tq:(h + 1) * tq] for h in range(n_heads)])


def _band_attention(q, kv_pad, tab, sinks, n_heads, window, T):
    B = q.shape[0]
    width = n_heads * HEAD_DIM
    has_sink = sinks is not None
    tab2 = tab.reshape(n_heads * Q_TILE, window + Q_TILE)
    rows = BAND_GROUP * Q_TILE
    in_specs = [pl.BlockSpec((1, rows, width), lambda b, n: (b, n, 0)),
                pl.BlockSpec((1, T + window, LANES), lambda b, n: (b, 0, 0)),
                pl.BlockSpec(tab2.shape, lambda b, n: (0, 0))]
    args = (q, kv_pad, tab2)
    if has_sink:
        rep = jnp.broadcast_to(jnp.repeat(sinks, Q_TILE)[:, None], (n_heads * Q_TILE, LANES))
        in_specs.append(pl.BlockSpec(rep.shape, lambda b, n: (0, 0)))
        args = args + (rep,)
    return pl.pallas_call(
        functools.partial(_band_kernel, n_heads, window, has_sink),
        grid=(B, T // rows),
        in_specs=in_specs,
        out_specs=pl.BlockSpec((1, rows, width), lambda b, n: (b, n, 0)),
        out_shape=jax.ShapeDtypeStruct((B, T, width), F32),
        compiler_params=_cparams(("arbitrary", "arbitrary")),
        name="band_sink" if has_sink else "band_window",
    )(*args)


def _mla_kernel(q_ref, k_ref, v_ref, o_ref, m_s, acc_s):
    tq = q_ref.shape[1]
    H = MLA_HEADS
    n = pl.program_id(1)
    c2 = float((MLA_NOPE + MLA_ROPE) ** -0.5) * LOG2E
    m_s[...] = jnp.full(m_s.shape, -jnp.inf, F32)
    acc_s[...] = jnp.zeros(acc_s.shape, F32)

    def step(k0, diag):
        ss = [_dot_nt(q_ref[0, :, LANES * h:LANES * (h + 1)], k_ref[0, pl.ds(k0, tq), LANES * h:LANES * (h + 1)])
              for h in range(H)]
        ps, alphas = [], []
        for h in range(H):
            rows = slice(h * tq, (h + 1) * tq)
            s2 = ss[h] * c2
            if diag:
                row = lax.broadcasted_iota(jnp.int32, (tq, tq), 0)
                col = lax.broadcasted_iota(jnp.int32, (tq, tq), 1)
                s2 = jnp.where(col <= row, s2, NEG)
            p, alpha = _softmax_tile(s2, rows, m_s)
            ps.append(p)
            alphas.append(alpha)
        for h in range(H):
            rows = slice(h * tq, (h + 1) * tq)
            v = v_ref[0, pl.ds(k0, tq), LANES * h:LANES * (h + 1)]
            acc_s[rows] = alphas[h] * acc_s[rows] + _dot(ps[h], v)

    def far(j, carry):
        step(pl.multiple_of(j * tq, tq), False)
        return carry

    lax.fori_loop(0, n, far, 0)
    step(pl.multiple_of(n * tq, tq), True)
    o_ref[0] = _merge_normalized([acc_s[h * tq:(h + 1) * tq] for h in range(H)])


def _mla_attention(qm, km, vm, T):
    B = qm.shape[0]
    tq = FLASH_TILE
    H = MLA_HEADS
    return pl.pallas_call(
        _mla_kernel,
        grid=(B, T // tq),
        in_specs=[pl.BlockSpec((1, tq, 512), lambda b, n: (b, n, 0)),
                  pl.BlockSpec((1, T, 512), lambda b, n: (b, 0, 0)),
                  pl.BlockSpec((1, T, 512), lambda b, n: (b, 0, 0))],
        out_specs=pl.BlockSpec((1, tq, 256), lambda b, n: (b, n, 0)),
        out_shape=jax.ShapeDtypeStruct((B, T, 256), F32),
        scratch_shapes=[pltpu.VMEM((H * tq, LANES), F32), pltpu.VMEM((H * tq, LANES), F32)],
        compiler_params=_cparams(("arbitrary", "arbitrary")),
        name="mla_attention",
    )(qm, km, vm)


def _out_kernel(x_ref, ocmp_ref, oslc_ref, owin_ref, krb_ref, ob_ref, oc_ref, z_ref, w_ref, gpost_ref, o_ref):
    tm = x_ref.shape[0]
    sig = 1.0 / (1.0 + jnp.exp(-krb_ref[...]))
    lo = lax.broadcasted_iota(jnp.int32, (tm, LANES), 1) < HEAD_DIM

    def gate(h, j):
        c = GATE_LANE0 + 3 * h + j
        return jnp.broadcast_to(sig[:, c:c + 1], (tm, LANES))

    branches = (ocmp_ref, oslc_ref, owin_ref)
    slabs = []
    for p in range(NSA_HEADS // 2):
        acc = jnp.zeros((tm, LANES), F32)
        for j in range(3):
            g = jnp.where(lo, gate(2 * p, j), gate(2 * p + 1, j))
            term = g * branches[j][:, LANES * p:LANES * (p + 1)]
            acc = term if j == 0 else acc + term
        slabs.append(acc)
    z = z_ref[...]
    mixed = jnp.concatenate(slabs + [ob_ref[...], oc_ref[...]], axis=1) * (z * (1.0 / (1.0 + jnp.exp(-z))))
    y = _dot(mixed.astype(BF16), w_ref[...])
    o_ref[...] = x_ref[...] + _rms(y, gpost_ref[...])


def _out_project(x2, ocmp, oslc, owin, krb, ob, oc, z, w, gpost, tm):
    BT = x2.shape[0]
    row = lambda i: (i, 0)
    fix = lambda i: (0, 0)
    spec = lambda a: pl.BlockSpec((tm, a.shape[1]), row)
    return pl.pallas_call(
        _out_kernel,
        grid=(BT // tm,),
        in_specs=[spec(x2), spec(ocmp), spec(oslc), spec(owin), spec(krb), spec(ob), spec(oc), spec(z),
                  pl.BlockSpec(w.shape, fix), pl.BlockSpec(gpost.shape, fix)],
        out_specs=pl.BlockSpec((tm, D_MODEL), row),
        out_shape=jax.ShapeDtypeStruct((BT, D_MODEL), F32),
        compiler_params=_cparams(("arbitrary",)),
        name="out_proj",
    )(x2, ocmp, oslc, owin, krb, ob, oc, z, w, gpost)


def _split_in(w):
    outs, o = [], 0
    for s in IN_SIZES:
        outs.append(w[..., o:o + s])
        o += s
    return outs


def _prep_w_in(w):
    a_q, a_kv, a_g, b_cq, b_ckv, b_kr, c_q, c_kv, z = _split_in(w)
    zeros = lambda n: jnp.zeros(w.shape[:-1] + (n,), w.dtype)
    krb = jnp.concatenate([zeros(HEAD_DIM), b_kr, a_g, zeros(LANES - HEAD_DIM - MLA_ROPE - 3 * NSA_HEADS)], axis=-1)
    return jnp.concatenate([a_q, a_kv, b_cq, b_ckv, krb, c_q, c_kv, z], axis=-1).astype(BF16)


def _prep_mla(w_uq, w_ukv):
    L = w_uq.shape[0]
    dq = MLA_NOPE + MLA_ROPE
    uq = w_uq.reshape(L, MLA_Q_RANK, MLA_HEADS, dq)
    uq = jnp.concatenate([uq, jnp.zeros((L, MLA_Q_RANK, MLA_HEADS, LANES - dq), uq.dtype)], axis=-1)
    ukv = w_ukv.reshape(L, MLA_KV_RANK, MLA_HEADS, MLA_NOPE + MLA_V)
    uk = jnp.concatenate([ukv[..., :MLA_NOPE], jnp.zeros((L, MLA_KV_RANK, MLA_HEADS, LANES - MLA_NOPE), ukv.dtype)], axis=-1)
    uv = jnp.concatenate([jnp.zeros((L, MLA_KV_RANK, MLA_HEADS, LANES - MLA_V), ukv.dtype), ukv[..., MLA_NOPE:]], axis=-1)
    return (uq.reshape(L, MLA_Q_RANK, MLA_HEADS * LANES).astype(BF16),
            uk.reshape(L, MLA_KV_RANK, MLA_HEADS * LANES).astype(BF16),
            uv.reshape(L, MLA_KV_RANK, MLA_HEADS * LANES).astype(BF16))


def _prep_compress(cmp_pos, cmp_w1, cmp_w2):
    L = cmp_pos.shape[0]
    pe = jnp.concatenate([cmp_pos[:, 0], cmp_pos[:, 1]], axis=-1)
    pet = pe[:, :CMP_STRIDE].reshape(L, 1, CMP_STRIDE * LANES)
    peb = pe[:, CMP_STRIDE:].reshape(L, 1, CMP_STRIDE * LANES)
    w1 = cmp_w1.reshape(L, 2, CMP_LEN, HEAD_DIM, CMP_HIDDEN)
    zero = jnp.zeros_like(w1[:, 0])
    w1k = jnp.concatenate([w1[:, 0], zero], axis=2)
    w1v = jnp.concatenate([zero, w1[:, 1]], axis=2)
    w1e = jnp.stack([w1k, w1v], axis=1)
    w1t = w1e[:, :, :CMP_STRIDE].reshape(L, 2, CMP_STRIDE * LANES, CMP_HIDDEN).astype(BF16)
    w1b = w1e[:, :, CMP_STRIDE:].reshape(L, 2, CMP_STRIDE * LANES, CMP_HIDDEN).astype(BF16)
    z2 = jnp.zeros_like(cmp_w2[:, 0])
    w2 = jnp.stack([jnp.concatenate([cmp_w2[:, 0], z2], axis=-1),
                    jnp.concatenate([z2, cmp_w2[:, 1]], axis=-1)], axis=1).astype(BF16)
    return pet, peb, w1t, w1b, w2


def _rope_tables(T):
    half = MLA_ROPE // 2
    inv = ROPE_THETA ** (-jnp.arange(half, dtype=F32) / half)
    ang = jnp.arange(T).astype(F32)[:, None] * inv[None, :]
    cos, sin = jnp.cos(ang), jnp.sin(ang)
    z = lambda n: jnp.zeros((T, n), F32)
    rc = jnp.concatenate([jnp.ones((T, MLA_NOPE), F32), cos, cos, z(LANES - MLA_NOPE - MLA_ROPE)], axis=1)
    rsa = jnp.concatenate([z(MLA_NOPE + half), sin, z(LANES - MLA_NOPE - MLA_ROPE)], axis=1)
    rsb = jnp.concatenate([z(MLA_NOPE), -sin, z(LANES - MLA_NOPE - half)], axis=1)
    return rc, rsa, rsb


def _static_tables(T):
    ns = T // SEL_BLOCK
    nc = T // CMP_STRIDE - 1
    ncp = T // CMP_STRIDE
    sstart = np.arange(ns) * SEL_BLOCK
    cstart = np.arange(nc) * CMP_STRIDE
    overlap = (np.clip(np.minimum(cstart[:, None] + CMP_LEN, sstart[None, :] + SEL_BLOCK)
                       - np.maximum(cstart[:, None], sstart[None, :]), 0, None) / CMP_STRIDE).astype(np.float32)
    ovt = np.zeros((ns, ncp), np.float32)
    ovt[:, :nc] = overlap.T
    eaug = np.zeros((T, LANES), np.float32)
    eaug[np.arange(T), HEAD_DIM + np.arange(T) // SEL_BLOCK] = 1.0
    return jnp.asarray(ovt, BF16), jnp.asarray(eaug, BF16)


def kernel(x, w_in, w_out, norm_pre, norm_post, cmp_pos, cmp_w1, cmp_w2, mla_q_norm, mla_w_uq, mla_kv_norm,
           mla_w_ukv, swa_sinks, rel_bias):
    B, T, D = x.shape
    depth = w_in.shape[0]
    assert D == D_MODEL and T % (2 * FLASH_TILE) == 0 and T // SEL_BLOCK <= HEAD_DIM
    ncp = T // CMP_STRIDE
    tm = 512

    w_in_p = _prep_w_in(w_in)
    wuq, wk, wv = _prep_mla(mla_w_uq, mla_w_ukv)
    pet, peb, w1t, w1b, w2 = _prep_compress(cmp_pos, cmp_w1, cmp_w2)
    w_out_b = w_out.astype(BF16)
    rope_c, rope_sa, rope_sb = _rope_tables(T)
    ovt, eaug = _static_tables(T)

    bc = _bias_table(rel_bias, 0, NSA_HEADS, T, ncp, -CMP_STRIDE, -(CMP_LEN - 1), 0, 1 << 30, 256)
    tab_near = _bias_table(rel_bias, 0, NSA_HEADS, 2 * FLASH_TILE, FLASH_TILE, -1, 0, 0, 1 << 30, 256, mult=LOG2E)
    tab_win = _bias_table(rel_bias, 0, NSA_HEADS, Q_TILE, NSA_WINDOW + Q_TILE, -1, NSA_WINDOW, 0, NSA_WINDOW, Q_TILE)
    tab_swa = _bias_table(rel_bias, NSA_HEADS, SWA_HEADS, Q_TILE, SWA_WINDOW + Q_TILE, -1, SWA_WINDOW, 0, SWA_WINDOW, Q_TILE)
    b31 = rel_bias[:NSA_HEADS, REL_BUCKETS - 1] * LOG2E

    x2 = x.reshape(B * T, D)
    for l in range(depth):
        (qa, cmpkv, slck, slcv, win, qm, km, vm, krb, cq, ckv, z) = _project(
            x2, norm_pre[l][None], w_in_p[l], mla_q_norm[l][None], wuq[l], mla_kv_norm[l][None], wk[l], wv[l],
            rope_c, rope_sa, rope_sb, eaug, T, tm)
        r3 = lambda a: a.reshape(B, T, a.shape[-1])
        kvc = _compress(cmpkv.reshape(B, ncp, CMP_STRIDE * LANES), pet[l], peb[l], w1t[l], w1b[l], w2[l])
        qa3 = r3(qa)
        ocmp, selb = _cmp_attention(qa3, kvc, bc, ovt, T)
        oslc = _slc_attention(b31, qa3, selb, r3(slck), r3(slcv), tab_near, T)
        win_pad = jnp.pad(r3(win), ((0, 0), (NSA_WINDOW, 0), (0, 0)))
        owin = _band_attention(qa3, win_pad, tab_win, None, NSA_HEADS, NSA_WINDOW, T)
        ckv_pad = jnp.pad(r3(ckv), ((0, 0), (SWA_WINDOW, 0), (0, 0)))
        oc = _band_attention(r3(cq), ckv_pad, tab_swa, swa_sinks[l], SWA_HEADS, SWA_WINDOW, T)
        ob = _mla_attention(r3(qm), r3(km), r3(vm), T)
        flat = lambda a: a.reshape(B * T, a.shape[-1])
        x2 = _out_project(x2, flat(ocmp), flat(oslc), flat(owin), krb, flat(ob), flat(oc), z,
                          w_out_b[l], norm_post[l][None], tm)
    return x2.reshape(B, T, D)
```

```python
import functools
import math

import numpy as np
import jax
import jax.numpy as jnp
from jax import lax
from jax.experimental import pallas as pl
from jax.experimental.pallas import tpu as pltpu

F32 = jnp.float32
BF16 = jnp.bfloat16

D_MODEL = 1024
HEAD_DIM = 64
NSA_HEADS = 4
CMP_LEN = 32
CMP_STRIDE = 16
CMP_HIDDEN = 128
SEL_BLOCK = 64
SEL_TOP = 16
NSA_WINDOW = 512
MLA_HEADS = 4
MLA_Q_RANK = 256
MLA_KV_RANK = 128
MLA_NOPE = 64
MLA_ROPE = 32
MLA_V = 64
ROPE_THETA = 10000.0
SWA_HEADS = 8
SWA_WINDOW = 128
REL_BUCKETS = 32
REL_MAX_DIST = 512
NORM_EPS = 1e-6
NEG = -1e30
BIG = 1e9
LOG2E = math.log2(math.e)
IN_SIZES = (256, 384, 12, 256, 128, 32, 512, 128, 1024)

LANES = 128
Q_TILE = 128
BAND_GROUP = 4
CMP_GROUP = 4
FLASH_TILE = 512
GATE_LANE0 = 96
VMEM_LIMIT = 56 * 1024 * 1024

C_AQ, C_CMP, C_SLC, C_WIN, C_BCQ, C_BCKV, C_KRB, C_CQ, C_CKV, C_Z, C_END = (
    0, 256, 384, 512, 640, 896, 1024, 1152, 1664, 1792, 2816)


def _dot(a, b):
    return jnp.dot(a, b, preferred_element_type=F32)


def _dot_nt(a, b):
    return lax.dot_general(a, b, (((1,), (1,)), ((), ())), preferred_element_type=F32)


def _bucket_thresholds():
    d = np.arange(0, 4 * REL_MAX_DIST)
    exact = REL_BUCKETS // 2
    large = exact + (np.log(np.maximum(d, 1).astype(np.float32) / np.float32(exact))
                     / np.float32(math.log(REL_MAX_DIST / exact)) * np.float32(REL_BUCKETS - exact)).astype(np.int32)
    b = np.where(d < exact, d, np.minimum(large, REL_BUCKETS - 1))
    assert np.all(np.diff(b) >= 0) and b[-1] == REL_BUCKETS - 1
    return [int(np.argmax(b >= k)) for k in range(REL_BUCKETS)]


_THRESH = _bucket_thresholds()


def _cparams(sem):
    return pltpu.CompilerParams(dimension_semantics=sem, vmem_limit_bytes=VMEM_LIMIT)


def _table_kernel(h0, a_col, d0, blk_d0, lo, hi, cmin0, cmin_step, mult, rb_ref, out_ref):
    h = pl.program_id(0) + h0
    blk = pl.program_id(1)
    rt, cc = out_ref.shape[2], out_ref.shape[3]
    r = pl.program_id(2) * rt + lax.broadcasted_iota(jnp.int32, (rt, cc), 0)
    c = lax.broadcasted_iota(jnp.int32, (rt, cc), 1)
    dist = r + a_col * c + d0 + blk * blk_d0
    acc = jnp.full((rt, cc), rb_ref[h, 0], F32)
    for k in range(1, REL_BUCKETS):
        acc = jnp.where(dist >= _THRESH[k], rb_ref[h, k], acc)
    ok = (dist >= lo) & (dist < hi) & (c >= jnp.maximum(cmin0 - blk * cmin_step, 0))
    out_ref[0, 0] = jnp.where(ok, acc * mult, NEG)


def _bias_table(rel_bias, h0, nh, rows, cols, a_col, d0, lo, hi, rt, mult=1.0, nblk=1, blk_d0=0, cmin0=0, cmin_step=0):
    return pl.pallas_call(
        functools.partial(_table_kernel, h0, a_col, d0, blk_d0, lo, hi, cmin0, cmin_step, mult),
        grid=(nh, nblk, rows // rt),
        in_specs=[pl.BlockSpec(memory_space=pltpu.SMEM)],
        out_specs=pl.BlockSpec((1, 1, rt, cols), lambda h, b, r: (h, b, r, 0)),
        out_shape=jax.ShapeDtypeStruct((nh, nblk, rows, cols), F32),
        compiler_params=_cparams(("arbitrary", "arbitrary", "arbitrary")),
        name="bias_table",
    )(rel_bias)


def _rms(v, g):
    return v * lax.rsqrt(jnp.mean(v * v, axis=-1, keepdims=True) + NORM_EPS) * g


def _proj_kernel(x_ref, gpre_ref, w_ref, qn_ref, wuq_ref, kvn_ref, wk_ref, wv_ref,
                 rc_ref, rsa_ref, rsb_ref, eaug_ref,
                 qa_ref, cmp_ref, slck_ref, slcv_ref, win_ref, qm_ref, km_ref, vm_ref,
                 krb_ref, cq_ref, ckv_ref, z_ref):
    hb = _rms(x_ref[...], gpre_ref[...]).astype(BF16)

    chunks = {}

    def proj(c0, c1):
        for (a, b), y in chunks.items():
            if a <= c0 and c1 <= b:
                return y[:, c0 - a:c1 - a]
        raise AssertionError((c0, c1))

    for a, b in ((C_AQ, C_CMP), (C_CMP, C_WIN), (C_WIN, C_KRB), (C_KRB, C_Z), (C_Z, C_END)):
        chunks[(a, b)] = _dot(hb, w_ref[:, a:b])

    tm = hb.shape[0]
    lane = lax.broadcasted_iota(jnp.int32, (tm, LANES), 1)
    lo = lane < HEAD_DIM

    qa_ref[...] = (proj(C_AQ, C_CMP) * 0.125).astype(BF16)
    cmp_ref[...] = proj(C_CMP, C_SLC)
    slc = proj(C_SLC, C_WIN)
    slcv_ref[...] = jnp.where(lo, 1.0, slc).astype(BF16)
    slck_ref[...] = jnp.where(lo, slc, eaug_ref[...].astype(F32)).astype(BF16)
    win_ref[...] = proj(C_WIN, C_BCQ).astype(BF16)
    cq_ref[...] = (proj(C_CQ, C_CKV) * 0.125).astype(BF16)
    ckv_ref[...] = proj(C_CKV, C_Z).astype(BF16)
    z_ref[...] = proj(C_Z, C_END)

    rc, rsa, rsb = rc_ref[...], rsa_ref[...], rsb_ref[...]

    def rope(v):
        return v * rc + pltpu.roll(v, 16, 1) * rsa + pltpu.roll(v, 112, 1) * rsb

    cqn = _rms(proj(C_BCQ, C_BCKV), qn_ref[...]).astype(BF16)
    qm = _dot(cqn, wuq_ref[...])
    for h in range(MLA_HEADS):
        qm_ref[:, LANES * h:LANES * (h + 1)] = rope(qm[:, LANES * h:LANES * (h + 1)]).astype(BF16)

    krb = proj(C_KRB, C_CQ)
    krb_ref[...] = krb
    krr = jnp.where(lo | (lane >= HEAD_DIM + MLA_ROPE), 0.0, rope(krb))

    ckvn = _rms(proj(C_BCKV, C_KRB), kvn_ref[...]).astype(BF16)
    kn = _dot(ckvn, wk_ref[...])
    for h in range(MLA_HEADS):
        km_ref[:, LANES * h:LANES * (h + 1)] = (kn[:, LANES * h:LANES * (h + 1)] + krr).astype(BF16)
    vm = _dot(ckvn, wv_ref[...])
    for h in range(MLA_HEADS):
        vm_ref[:, LANES * h:LANES * (h + 1)] = jnp.where(lo, 1.0, vm[:, LANES * h:LANES * (h + 1)]).astype(BF16)


def _project(x2, gpre, w, qn, wuq, kvn, wk, wv, rope_c, rope_sa, rope_sb, eaug, T, tm):
    BT = x2.shape[0]
    nt = T // tm
    row = lambda i: (i, 0)
    fix = lambda i: (0, 0)
    pos = lambda i: (i % nt, 0)
    widths = [(256, BF16), (128, F32), (128, BF16), (128, BF16), (128, BF16), (512, BF16), (512, BF16),
              (512, BF16), (128, F32), (512, BF16), (128, BF16), (1024, F32)]
    return pl.pallas_call(
        _proj_kernel,
        grid=(BT // tm,),
        in_specs=[pl.BlockSpec((tm, D_MODEL), row), pl.BlockSpec((1, D_MODEL), fix),
                  pl.BlockSpec(w.shape, fix), pl.BlockSpec(qn.shape, fix), pl.BlockSpec(wuq.shape, fix),
                  pl.BlockSpec(kvn.shape, fix), pl.BlockSpec(wk.shape, fix), pl.BlockSpec(wv.shape, fix),
                  pl.BlockSpec((tm, LANES), pos), pl.BlockSpec((tm, LANES), pos), pl.BlockSpec((tm, LANES), pos),
                  pl.BlockSpec((tm, LANES), pos)],
        out_specs=[pl.BlockSpec((tm, wd), row) for wd, _ in widths],
        out_shape=[jax.ShapeDtypeStruct((BT, wd), dt) for wd, dt in widths],
        compiler_params=_cparams(("arbitrary",)),
        name="proj",
    )(x2, gpre, w, qn, wuq, kvn, wk, wv, rope_c, rope_sa, rope_sb, eaug)


def _compress_kernel(c_ref, pet_ref, peb_ref, w1t_ref, w1b_ref, w2_ref, o_ref):
    c = c_ref[0]
    nch = c.shape[0]
    top = (c + pet_ref[...]).astype(BF16)
    bot = (c + peb_ref[...]).astype(BF16)
    out = jnp.zeros((nch, LANES), F32)
    for i in range(2):
        pre = _dot(top, w1t_ref[i]) + pltpu.roll(_dot(bot, w1b_ref[i]), nch - 1, 0)
        hid = pre * (1.0 / (1.0 + jnp.exp(-pre)))
        out = out + _dot(hid.astype(BF16), w2_ref[i])
    o_ref[0] = out.astype(BF16)


def _compress(chunks, pet, peb, w1t, w1b, w2):
    B, nch, width = chunks.shape
    full = lambda a: pl.BlockSpec(a.shape, lambda b: (0,) * a.ndim)
    return pl.pallas_call(
        _compress_kernel,
        grid=(B,),
        in_specs=[pl.BlockSpec((1, nch, width), lambda b: (b, 0, 0)),
                  full(pet), full(peb), full(w1t), full(w1b), full(w2)],
        out_specs=pl.BlockSpec((1, nch, LANES), lambda b: (b, 0, 0)),
        out_shape=jax.ShapeDtypeStruct((B, nch, LANES), BF16),
        compiler_params=_cparams(("arbitrary",)),
        name="compress",
    )(chunks, pet, peb, w1t, w1b, w2)


def _head_slabs(qblk, n_heads):
    q = qblk.astype(F32)
    lo = lax.broadcasted_iota(jnp.int32, (q.shape[0], LANES), 1) < HEAD_DIM
    out = []
    for j in range(n_heads // 2):
        slab = q[:, LANES * j:LANES * (j + 1)]
        out.append(jnp.where(lo, slab, 0.0))
        out.append(jnp.where(lo, pltpu.roll(slab, HEAD_DIM, 1), 0.0))
    return out


def _merge_upper(accs):
    lo = lax.broadcasted_iota(jnp.int32, accs[0].shape, 1) < HEAD_DIM
    slabs = [jnp.where(lo, pltpu.roll(accs[2 * j], HEAD_DIM, 1), accs[2 * j + 1]) for j in range(len(accs) // 2)]
    return jnp.concatenate(slabs, axis=1)


def _merge_normalized(accs):
    lo = lax.broadcasted_iota(jnp.int32, accs[0].shape, 1) < HEAD_DIM
    slabs = []
    for j in range(len(accs) // 2):
        a, b = accs[2 * j], accs[2 * j + 1]
        out = jnp.where(lo, pltpu.roll(a, HEAD_DIM, 1), b)
        den = jnp.where(lo, a, pltpu.roll(b, HEAD_DIM, 1))
        slabs.append(out / den)
    return jnp.concatenate(slabs, axis=1)


def _row_max_lanes(s):
    return jnp.broadcast_to(jnp.max(s, axis=-1, keepdims=True), (s.shape[0], LANES))


def _sub_lanes(s, ref):
    return jnp.concatenate([s[:, LANES * i:LANES * (i + 1)] - ref for i in range(s.shape[1] // LANES)], axis=1)


def _topk_mask_t(imp, t0):
    ns, tq = imp.shape
    srow = lax.broadcasted_iota(jnp.int32, (ns, tq), 0)
    cur = (t0 + lax.broadcasted_iota(jnp.int32, (ns, tq), 1)) // SEL_BLOCK
    forced = (srow == 0) | (srow == cur) | (srow == cur - 1)
    x = jnp.where(forced, BIG, jnp.where(srow <= cur, imp, -BIG))
    sub = 8
    groups = [x[sub * v:sub * (v + 1)] for v in range(ns // sub)]
    rows_in = [srow[sub * v:sub * (v + 1)] for v in range(ns // sub)]
    cnts = [jnp.zeros((sub, tq), F32) for _ in groups]
    for sp in range(ns):
        other = jnp.broadcast_to(x[sp:sp + 1, :], (sub, tq))
        for v, xv in enumerate(groups):
            if sub * v > sp:
                beats = other >= xv
            elif sub * v + sub - 1 < sp:
                beats = other > xv
            else:
                beats = (other > xv) | ((other == xv) & (rows_in[v] > sp))
            cnts[v] = cnts[v] + jnp.where(beats, 1.0, 0.0)
    cnt = jnp.concatenate(cnts, axis=0)
    sel = (cnt < float(min(SEL_TOP, ns))) & (srow <= cur)
    return jnp.where(sel, 0.0, NEG)


def _cmp_kernel(q_ref, kvc_ref, bc_ref, ovt_ref, ocmp_ref, selb_ref):
    tq = Q_TILE
    kvc = kvc_ref[0]
    ovt = ovt_ref[...]
    ns = ovt.shape[0]
    groups = range(CMP_GROUP)
    sls = [slice(g * tq, (g + 1) * tq) for g in groups]
    s_alls = []
    for g in groups:
        qs = _head_slabs(q_ref[0, sls[g], :], NSA_HEADS)
        s_alls.append(_dot_nt(jnp.concatenate([q.astype(BF16) for q in qs], axis=0), kvc))
    psums, o_alls = [], []
    for g in groups:
        valid = bc_ref[0, sls[g], :] > 0.5 * NEG
        psum = jnp.zeros(valid.shape, F32)
        ps = []
        for h in range(NSA_HEADS):
            s = s_alls[g][h * tq:(h + 1) * tq] + bc_ref[h, sls[g], :]
            m = jnp.max(s, axis=-1, keepdims=True)
            e = jnp.where(valid, jnp.exp(s - m), 0.0)
            l = jnp.sum(e, axis=-1, keepdims=True)
            p = e * (1.0 / jnp.where(l > 0.0, l, 1.0))
            psum = psum + p
            ps.append(p.astype(BF16))
        psums.append(psum)
        o_alls.append(_dot(jnp.concatenate(ps, axis=0), kvc))
    imps = []
    for g in groups:
        hi = psums[g].astype(BF16)
        lo = (psums[g] - hi.astype(F32)).astype(BF16)
        imps.append(_dot_nt(ovt, hi) + _dot_nt(ovt, lo))
    for g in groups:
        ocmp_ref[0, sls[g], :] = _merge_upper([o_alls[g][h * tq:(h + 1) * tq] for h in range(NSA_HEADS)])
        sb = _topk_mask_t(imps[g], (pl.program_id(1) * CMP_GROUP + g) * tq)
        if ns < HEAD_DIM:
            sb = jnp.concatenate([sb, jnp.full((HEAD_DIM - ns, tq), NEG, F32)], axis=0)
        full = jnp.concatenate([jnp.zeros((HEAD_DIM, tq), F32), sb], axis=0)
        selb_ref[0, sls[g], :] = full.T.astype(BF16)


def _cmp_attention(qa, kvc, bc, ovt, T):
    B = qa.shape[0]
    ncp = kvc.shape[1]
    rows = CMP_GROUP * Q_TILE
    return pl.pallas_call(
        _cmp_kernel,
        grid=(B, T // rows),
        in_specs=[pl.BlockSpec((1, rows, 256), lambda b, n: (b, n, 0)),
                  pl.BlockSpec((1, ncp, LANES), lambda b, n: (b, 0, 0)),
                  pl.BlockSpec((NSA_HEADS, rows, ncp), lambda b, n: (0, n, 0)),
                  pl.BlockSpec(ovt.shape, lambda b, n: (0, 0))],
        out_specs=[pl.BlockSpec((1, rows, 256), lambda b, n: (b, n, 0)),
                   pl.BlockSpec((1, rows, LANES), lambda b, n: (b, n, 0))],
        out_shape=[jax.ShapeDtypeStruct((B, T, 256), F32), jax.ShapeDtypeStruct((B, T, LANES), BF16)],
        compiler_params=_cparams(("arbitrary", "arbitrary")),
        name="cmp_select",
    )(qa, kvc, bc, ovt)


def _softmax_tile(s2, rows, m_s, shift=None):
    m_t = _row_max_lanes(s2)
    if shift is not None:
        m_t = m_t + shift
    m_old = m_s[rows]
    m_new = jnp.maximum(m_old, m_t)
    p = jnp.exp2(_sub_lanes(s2, m_new if shift is None else m_new - shift))
    alpha = jnp.exp2(m_old - m_new)
    m_s[rows] = m_new
    return p.astype(BF16), alpha


def _slc_kernel(b31_ref, q_ref, selb_ref, ka_ref, kv_ref, tab_ref, o_ref, m_s, acc_s):
    tq = q_ref.shape[1]
    H = NSA_HEADS
    n = pl.program_id(1)
    qs = _head_slabs(q_ref[0], H)
    selb = selb_ref[0].astype(F32)
    qst = jnp.concatenate([(qs[h] + selb).astype(BF16) for h in range(H)], axis=0)
    m_s[...] = jnp.full(m_s.shape, -jnp.inf, F32)
    acc_s[...] = jnp.zeros(acc_s.shape, F32)

    def step(k0, near_block):
        ka = ka_ref[0, pl.ds(k0, tq), :]
        kv = kv_ref[0, pl.ds(k0, tq), :]
        s_all = _dot_nt(qst, ka)
        ps, alphas = [], []
        for h in range(H):
            rows = slice(h * tq, (h + 1) * tq)
            s2 = s_all[rows] * LOG2E
            if near_block is None:
                p, alpha = _softmax_tile(s2, rows, m_s, shift=b31_ref[h])
            else:
                s2 = s2 + tab_ref[h, near_block * tq:(near_block + 1) * tq, :]
                p, alpha = _softmax_tile(s2, rows, m_s)
            ps.append(p)
            alphas.append(alpha)
        pv = _dot(jnp.concatenate(ps, axis=0), kv)
        for h in range(H):
            rows = slice(h * tq, (h + 1) * tq)
            acc_s[rows] = alphas[h] * acc_s[rows] + pv[rows]

    def far(j, carry):
        step(pl.multiple_of(j * tq, tq), None)
        return carry

    lax.fori_loop(0, jnp.maximum(n - 1, 0), far, 0)

    @pl.when(n >= 1)
    def _():
        step(pl.multiple_of(jnp.maximum(n - 1, 0) * tq, tq), 1)

    step(pl.multiple_of(n * tq, tq), 0)
    o_ref[0] = _merge_normalized([acc_s[h * tq:(h + 1) * tq] for h in range(H)])


def _slc_attention(b31, qa, selb, slck, slcv, tab, T):
    B = qa.shape[0]
    tq = FLASH_TILE
    H = NSA_HEADS
    return pl.pallas_call(
        _slc_kernel,
        grid=(B, T // tq),
        in_specs=[pl.BlockSpec(memory_space=pltpu.SMEM),
                  pl.BlockSpec((1, tq, 256), lambda b, n: (b, n, 0)),
                  pl.BlockSpec((1, tq, LANES), lambda b, n: (b, n, 0)),
                  pl.BlockSpec((1, T, LANES), lambda b, n: (b, 0, 0)),
                  pl.BlockSpec((1, T, LANES), lambda b, n: (b, 0, 0)),
                  pl.BlockSpec(tab.shape, lambda b, n: (0, 0, 0))],
        out_specs=pl.BlockSpec((1, tq, 256), lambda b, n: (b, n, 0)),
        out_shape=jax.ShapeDtypeStruct((B, T, 256), F32),
        scratch_shapes=[pltpu.VMEM((H * tq, LANES), F32), pltpu.VMEM((H * tq, LANES), F32)],
        compiler_params=_cparams(("arbitrary", "arbitrary")),
        name="slc_attention",
    )(b31, qa, selb, slck, slcv, tab)


def _band_kernel(n_heads, window, has_sink, *refs):
    if has_sink:
        q_ref, kv_ref, tab_ref, sink_ref, o_ref = refs
    else:
        q_ref, kv_ref, tab_ref, o_ref = refs
    tq = Q_TILE
    span = window + tq
    lo_kv = lax.broadcasted_iota(jnp.int32, (span, LANES), 1) < HEAD_DIM
    lo_o = lax.broadcasted_iota(jnp.int32, (n_heads * tq, LANES), 1) < HEAD_DIM
    groups = range(BAND_GROUP)
    tiles = [pl.program_id(1) * BAND_GROUP + g for g in groups]
    kvs, dots = [], []
    for g in groups:
        q0 = pl.multiple_of(tiles[g] * tq, tq)
        kvs.append(kv_ref[0, pl.ds(q0, span), :])
        qs = _head_slabs(q_ref[0, g * tq:(g + 1) * tq, :], n_heads)
        dots.append(_dot_nt(jnp.concatenate([q.astype(BF16) for q in qs], axis=0), kvs[g]))
    accs, ms = [], []
    for g in groups:
        s = dots[g] + tab_ref[jnp.minimum(tiles[g], window // tq)]
        m = _row_max_lanes(s)
        if has_sink:
            m = jnp.maximum(m, sink_ref[...])
        e = jnp.exp(_sub_lanes(s, m))
        ms.append(m)
        accs.append(_dot(e.astype(BF16), jnp.where(lo_kv, 1.0, kvs[g]).astype(BF16)))
    for g in groups:
        acc = accs[g]
        if has_sink:
            acc = acc + jnp.where(lo_o, jnp.exp(sink_ref[...] - ms[g]), 0.0)
        o_ref[0, g * tq:(g + 1) * tq, :] = _merge_normalized([acc[h * tq:(h + 1) * tq] for h in range(n_heads)])


def _band_attention(q, kv_pad, tab, sinks, n_heads, window, T):
    B = q.shape[0]
    width = n_heads * HEAD_DIM
    has_sink = sinks is not None
    nvar = tab.shape[1]
    tab2 = jnp.transpose(tab, (1, 0, 2, 3)).reshape(nvar, n_heads * Q_TILE, window + Q_TILE)
    rows = BAND_GROUP * Q_TILE
    in_specs = [pl.BlockSpec((1, rows, width), lambda b, n: (b, n, 0)),
                pl.BlockSpec((1, T + window, LANES), lambda b, n: (b, 0, 0)),
                pl.BlockSpec(tab2.shape, lambda b, n: (0, 0, 0))]
    args = (q, kv_pad, tab2)
    if has_sink:
        rep = jnp.broadcast_to(jnp.repeat(sinks, Q_TILE)[:, None], (n_heads * Q_TILE, LANES))
        in_specs.append(pl.BlockSpec(rep.shape, lambda b, n: (0, 0)))
        args = args + (rep,)
    return pl.pallas_call(
        functools.partial(_band_kernel, n_heads, window, has_sink),
        grid=(B, T // rows),
        in_specs=in_specs,
        out_specs=pl.BlockSpec((1, rows, width), lambda b, n: (b, n, 0)),
        out_shape=jax.ShapeDtypeStruct((B, T, width), F32),
        compiler_params=_cparams(("arbitrary", "arbitrary")),
        name="band_sink" if has_sink else "band_window",
    )(*args)


def _mla_kernel(q_ref, k_ref, v_ref, o_ref, m_s, acc_s):
    tq = q_ref.shape[1]
    H = MLA_HEADS
    n = pl.program_id(1)
    c2 = float((MLA_NOPE + MLA_ROPE) ** -0.5) * LOG2E
    m_s[...] = jnp.full(m_s.shape, -jnp.inf, F32)
    acc_s[...] = jnp.zeros(acc_s.shape, F32)

    def step(k0, diag):
        ss = [_dot_nt(q_ref[0, :, LANES * h:LANES * (h + 1)], k_ref[0, pl.ds(k0, tq), LANES * h:LANES * (h + 1)])
              for h in range(H)]
        ps, alphas = [], []
        for h in range(H):
            rows = slice(h * tq, (h + 1) * tq)
            s2 = ss[h] * c2
            if diag:
                row = lax.broadcasted_iota(jnp.int32, (tq, tq), 0)
                col = lax.broadcasted_iota(jnp.int32, (tq, tq), 1)
                s2 = jnp.where(col <= row, s2, NEG)
            p, alpha = _softmax_tile(s2, rows, m_s)
            ps.append(p)
            alphas.append(alpha)
        for h in range(H):
            rows = slice(h * tq, (h + 1) * tq)
            v = v_ref[0, pl.ds(k0, tq), LANES * h:LANES * (h + 1)]
            acc_s[rows] = alphas[h] * acc_s[rows] + _dot(ps[h], v)

    def far(j, carry):
        step(pl.multiple_of(j * tq, tq), False)
        return carry

    lax.fori_loop(0, n, far, 0)
    step(pl.multiple_of(n * tq, tq), True)
    o_ref[0] = _merge_normalized([acc_s[h * tq:(h + 1) * tq] for h in range(H)])


def _mla_attention(qm, km, vm, T):
    B = qm.shape[0]
    tq = FLASH_TILE
    H = MLA_HEADS
    return pl.pallas_call(
        _mla_kernel,
        grid=(B, T // tq),
        in_specs=[pl.BlockSpec((1, tq, 512), lambda b, n: (b, n, 0)),
                  pl.BlockSpec((1, T, 512), lambda b, n: (b, 0, 0)),
                  pl.BlockSpec((1, T, 512), lambda b, n: (b, 0, 0))],
        out_specs=pl.BlockSpec((1, tq, 256), lambda b, n: (b, n, 0)),
        out_shape=jax.ShapeDtypeStruct((B, T, 256), F32),
        scratch_shapes=[pltpu.VMEM((H * tq, LANES), F32), pltpu.VMEM((H * tq, LANES), F32)],
        compiler_params=_cparams(("arbitrary", "arbitrary")),
        name="mla_attention",
    )(qm, km, vm)


def _out_kernel(x_ref, ocmp_ref, oslc_ref, owin_ref, krb_ref, ob_ref, oc_ref, z_ref, w_ref, gpost_ref, o_ref):
    tm = x_ref.shape[0]
    sig = 1.0 / (1.0 + jnp.exp(-krb_ref[...]))
    lo = lax.broadcasted_iota(jnp.int32, (tm, LANES), 1) < HEAD_DIM

    def gate(h, j):
        c = GATE_LANE0 + 3 * h + j
        return jnp.broadcast_to(sig[:, c:c + 1], (tm, LANES))

    branches = (ocmp_ref, oslc_ref, owin_ref)
    slabs = []
    for p in range(NSA_HEADS // 2):
        acc = jnp.zeros((tm, LANES), F32)
        for j in range(3):
            g = jnp.where(lo, gate(2 * p, j), gate(2 * p + 1, j))
            term = g * branches[j][:, LANES * p:LANES * (p + 1)]
            acc = term if j == 0 else acc + term
        slabs.append(acc)
    z = z_ref[...]
    mixed = jnp.concatenate(slabs + [ob_ref[...], oc_ref[...]], axis=1) * (z * (1.0 / (1.0 + jnp.exp(-z))))
    y = _dot(mixed.astype(BF16), w_ref[...])
    o_ref[...] = x_ref[...] + _rms(y, gpost_ref[...])


def _out_project(x2, ocmp, oslc, owin, krb, ob, oc, z, w, gpost, tm):
    BT = x2.shape[0]
    row = lambda i: (i, 0)
    fix = lambda i: (0, 0)
    spec = lambda a: pl.BlockSpec((tm, a.shape[1]), row)
    return pl.pallas_call(
        _out_kernel,
        grid=(BT // tm,),
        in_specs=[spec(x2), spec(ocmp), spec(oslc), spec(owin), spec(krb), spec(ob), spec(oc), spec(z),
                  pl.BlockSpec(w.shape, fix), pl.BlockSpec(gpost.shape, fix)],
        out_specs=pl.BlockSpec((tm, D_MODEL), row),
        out_shape=jax.ShapeDtypeStruct((BT, D_MODEL), F32),
        compiler_params=_cparams(("arbitrary",)),
        name="out_proj",
    )(x2, ocmp, oslc, owin, krb, ob, oc, z, w, gpost)


def _split_in(w):
    outs, o = [], 0
    for s in IN_SIZES:
        outs.append(w[..., o:o + s])
        o += s
    return outs


def _prep_w_in(w):
    a_q, a_kv, a_g, b_cq, b_ckv, b_kr, c_q, c_kv, z = _split_in(w)
    zeros = lambda n: jnp.zeros(w.shape[:-1] + (n,), w.dtype)
    krb = jnp.concatenate([zeros(HEAD_DIM), b_kr, a_g, zeros(LANES - HEAD_DIM - MLA_ROPE - 3 * NSA_HEADS)], axis=-1)
    return jnp.concatenate([a_q, a_kv, b_cq, b_ckv, krb, c_q, c_kv, z], axis=-1).astype(BF16)


def _prep_mla(w_uq, w_ukv):
    L = w_uq.shape[0]
    dq = MLA_NOPE + MLA_ROPE
    uq = w_uq.reshape(L, MLA_Q_RANK, MLA_HEADS, dq)
    uq = jnp.concatenate([uq, jnp.zeros((L, MLA_Q_RANK, MLA_HEADS, LANES - dq), uq.dtype)], axis=-1)
    ukv = w_ukv.reshape(L, MLA_KV_RANK, MLA_HEADS, MLA_NOPE + MLA_V)
    uk = jnp.concatenate([ukv[..., :MLA_NOPE], jnp.zeros((L, MLA_KV_RANK, MLA_HEADS, LANES - MLA_NOPE), ukv.dtype)], axis=-1)
    uv = jnp.concatenate([jnp.zeros((L, MLA_KV_RANK, MLA_HEADS, LANES - MLA_V), ukv.dtype), ukv[..., MLA_NOPE:]], axis=-1)
    return (uq.reshape(L, MLA_Q_RANK, MLA_HEADS * LANES).astype(BF16),
            uk.reshape(L, MLA_KV_RANK, MLA_HEADS * LANES).astype(BF16),
            uv.reshape(L, MLA_KV_RANK, MLA_HEADS * LANES).astype(BF16))


def _prep_compress(cmp_pos, cmp_w1, cmp_w2):
    L = cmp_pos.shape[0]
    pe = jnp.concatenate([cmp_pos[:, 0], cmp_pos[:, 1]], axis=-1)
    pet = pe[:, :CMP_STRIDE].reshape(L, 1, CMP_STRIDE * LANES)
    peb = pe[:, CMP_STRIDE:].reshape(L, 1, CMP_STRIDE * LANES)
    w1 = cmp_w1.reshape(L, 2, CMP_LEN, HEAD_DIM, CMP_HIDDEN)
    zero = jnp.zeros_like(w1[:, 0])
    w1k = jnp.concatenate([w1[:, 0], zero], axis=2)
    w1v = jnp.concatenate([zero, w1[:, 1]], axis=2)
    w1e = jnp.stack([w1k, w1v], axis=1)
    w1t = w1e[:, :, :CMP_STRIDE].reshape(L, 2, CMP_STRIDE * LANES, CMP_HIDDEN).astype(BF16)
    w1b = w1e[:, :, CMP_STRIDE:].reshape(L, 2, CMP_STRIDE * LANES, CMP_HIDDEN).astype(BF16)
    z2 = jnp.zeros_like(cmp_w2[:, 0])
    w2 = jnp.stack([jnp.concatenate([cmp_w2[:, 0], z2], axis=-1),
                    jnp.concatenate([z2, cmp_w2[:, 1]], axis=-1)], axis=1).astype(BF16)
    return pet, peb, w1t, w1b, w2


def _rope_tables(T):
    half = MLA_ROPE // 2
    inv = ROPE_THETA ** (-jnp.arange(half, dtype=F32) / half)
    ang = jnp.arange(T).astype(F32)[:, None] * inv[None, :]
    cos, sin = jnp.cos(ang), jnp.sin(ang)
    z = lambda n: jnp.zeros((T, n), F32)
    rc = jnp.concatenate([jnp.ones((T, MLA_NOPE), F32), cos, cos, z(LANES - MLA_NOPE - MLA_ROPE)], axis=1)
    rsa = jnp.concatenate([z(MLA_NOPE + half), sin, z(LANES - MLA_NOPE - MLA_ROPE)], axis=1)
    rsb = jnp.concatenate([z(MLA_NOPE), -sin, z(LANES - MLA_NOPE - half)], axis=1)
    return rc, rsa, rsb


def _static_tables(T):
    ns = T // SEL_BLOCK
    nc = T // CMP_STRIDE - 1
    ncp = T // CMP_STRIDE
    sstart = np.arange(ns) * SEL_BLOCK
    cstart = np.arange(nc) * CMP_STRIDE
    overlap = (np.clip(np.minimum(cstart[:, None] + CMP_LEN, sstart[None, :] + SEL_BLOCK)
                       - np.maximum(cstart[:, None], sstart[None, :]), 0, None) / CMP_STRIDE).astype(np.float32)
    ovt = np.zeros((ns, ncp), np.float32)
    ovt[:, :nc] = overlap.T
    eaug = np.zeros((T, LANES), np.float32)
    eaug[np.arange(T), HEAD_DIM + np.arange(T) // SEL_BLOCK] = 1.0
    return jnp.asarray(ovt, BF16), jnp.asarray(eaug, BF16)


def kernel(x, w_in, w_out, norm_pre, norm_post, cmp_pos, cmp_w1, cmp_w2, mla_q_norm, mla_w_uq, mla_kv_norm,
           mla_w_ukv, swa_sinks, rel_bias):
    B, T, D = x.shape
    depth = w_in.shape[0]
    assert D == D_MODEL and T % (2 * FLASH_TILE) == 0 and T // SEL_BLOCK <= HEAD_DIM
    ncp = T // CMP_STRIDE
    tm = 512

    w_in_p = _prep_w_in(w_in)
    wuq, wk, wv = _prep_mla(mla_w_uq, mla_w_ukv)
    pet, peb, w1t, w1b, w2 = _prep_compress(cmp_pos, cmp_w1, cmp_w2)
    w_out_b = w_out.astype(BF16)
    rope_c, rope_sa, rope_sb = _rope_tables(T)
    ovt, eaug = _static_tables(T)

    bc = _bias_table(rel_bias, 0, NSA_HEADS, T, ncp, -CMP_STRIDE, -(CMP_LEN - 1), 0, 1 << 30, 256).reshape(NSA_HEADS, T, ncp)
    tab_near = _bias_table(rel_bias, 0, NSA_HEADS, FLASH_TILE, FLASH_TILE, -1, 0, 0, 1 << 30, 256, mult=LOG2E,
                           nblk=2, blk_d0=FLASH_TILE).reshape(NSA_HEADS, 2 * FLASH_TILE, FLASH_TILE)
    tab_win = _bias_table(rel_bias, 0, NSA_HEADS, Q_TILE, NSA_WINDOW + Q_TILE, -1, NSA_WINDOW, 0, NSA_WINDOW, Q_TILE,
                          nblk=NSA_WINDOW // Q_TILE + 1, cmin0=NSA_WINDOW, cmin_step=Q_TILE)
    tab_swa = _bias_table(rel_bias, NSA_HEADS, SWA_HEADS, Q_TILE, SWA_WINDOW + Q_TILE, -1, SWA_WINDOW, 0, SWA_WINDOW, Q_TILE,
                          nblk=SWA_WINDOW // Q_TILE + 1, cmin0=SWA_WINDOW, cmin_step=Q_TILE)
    b31 = rel_bias[:NSA_HEADS, REL_BUCKETS - 1] * LOG2E

    x2 = x.reshape(B * T, D)
    for l in range(depth):
        (qa, cmpkv, slck, slcv, win, qm, km, vm, krb, cq, ckv, z) = _project(
            x2, norm_pre[l][None], w_in_p[l], mla_q_norm[l][None], wuq[l], mla_kv_norm[l][None], wk[l], wv[l],
            rope_c, rope_sa, rope_sb, eaug, T, tm)
        r3 = lambda a: a.reshape(B, T, a.shape[-1])
        kvc = _compress(cmpkv.reshape(B, ncp, CMP_STRIDE * LANES), pet[l], peb[l], w1t[l], w1b[l], w2[l])
        qa3 = r3(qa)
        ocmp, selb = _cmp_attention(qa3, kvc, bc, ovt, T)
        oslc = _slc_attention(b31, qa3, selb, r3(slck), r3(slcv), tab_near, T)
        win_pad = jnp.pad(r3(win), ((0, 0), (NSA_WINDOW, 0), (0, 0)))
        owin = _band_attention(qa3, win_pad, tab_win, None, NSA_HEADS, NSA_WINDOW, T)
        ckv_pad = jnp.pad(r3(ckv), ((0, 0), (SWA_WINDOW, 0), (0, 0)))
        oc = _band_attention(r3(cq), ckv_pad, tab_swa, swa_sinks[l], SWA_HEADS, SWA_WINDOW, T)
        ob = _mla_attention(r3(qm), r3(km), r3(vm), T)
        flat = lambda a: a.reshape(B * T, a.shape[-1])
        x2 = _out_project(x2, flat(ocmp), flat(oslc), flat(owin), krb, flat(ob), flat(oc), z,
                          w_out_b[l], norm_post[l][None], tm)
    return x2.reshape(B, T, D)
```

```python
import functools
import math

import numpy as np
import jax
import jax.numpy as jnp
from jax import lax
from jax.experimental import pallas as pl
from jax.experimental.pallas import tpu as pltpu

F32 = jnp.float32
BF16 = jnp.bfloat16

D_MODEL = 1024
HEAD_DIM = 64
NSA_HEADS = 4
CMP_LEN = 32
CMP_STRIDE = 16
CMP_HIDDEN = 128
SEL_BLOCK = 64
SEL_TOP = 16
NSA_WINDOW = 512
MLA_HEADS = 4
MLA_Q_RANK = 256
MLA_KV_RANK = 128
MLA_NOPE = 64
MLA_ROPE = 32
MLA_V = 64
ROPE_THETA = 10000.0
SWA_HEADS = 8
SWA_WINDOW = 128
REL_BUCKETS = 32
REL_MAX_DIST = 512
NORM_EPS = 1e-6
NEG = -1e30
BIG = 1e9
LOG2E = math.log2(math.e)
IN_SIZES = (256, 384, 12, 256, 128, 32, 512, 128, 1024)

LANES = 128
Q_TILE = 128
BAND_ROWS = 4096
CMP_GROUP = 4
FLASH_TILE = 512
GATE_LANE0 = 96
VMEM_LIMIT = 56 * 1024 * 1024

C_AQ, C_CMP, C_SLC, C_WIN, C_BCQ, C_BCKV, C_KRB, C_CQ, C_CKV, C_Z, C_END = (
    0, 256, 384, 512, 640, 896, 1024, 1152, 1664, 1792, 2816)


def _dot(a, b):
    return jnp.dot(a, b, preferred_element_type=F32)


def _dot_nt(a, b):
    return lax.dot_general(a, b, (((1,), (1,)), ((), ())), preferred_element_type=F32)


def _bucket_thresholds():
    d = np.arange(0, 4 * REL_MAX_DIST)
    exact = REL_BUCKETS // 2
    large = exact + (np.log(np.maximum(d, 1).astype(np.float32) / np.float32(exact))
                     / np.float32(math.log(REL_MAX_DIST / exact)) * np.float32(REL_BUCKETS - exact)).astype(np.int32)
    b = np.where(d < exact, d, np.minimum(large, REL_BUCKETS - 1))
    assert np.all(np.diff(b) >= 0) and b[-1] == REL_BUCKETS - 1
    return [int(np.argmax(b >= k)) for k in range(REL_BUCKETS)]


_THRESH = _bucket_thresholds()


def _cparams(sem):
    return pltpu.CompilerParams(dimension_semantics=sem, vmem_limit_bytes=VMEM_LIMIT)


def _table_kernel(h0, a_col, d0, blk_d0, lo, hi, cmin0, cmin_step, mult, rb_ref, out_ref):
    h = pl.program_id(0) + h0
    blk = pl.program_id(1)
    rt, cc = out_ref.shape[2], out_ref.shape[3]
    r = pl.program_id(2) * rt + lax.broadcasted_iota(jnp.int32, (rt, cc), 0)
    c = lax.broadcasted_iota(jnp.int32, (rt, cc), 1)
    dist = r + a_col * c + d0 + blk * blk_d0
    acc = jnp.full((rt, cc), rb_ref[h, 0], F32)
    for k in range(1, REL_BUCKETS):
        acc = jnp.where(dist >= _THRESH[k], rb_ref[h, k], acc)
    ok = (dist >= lo) & (dist < hi) & (c >= jnp.maximum(cmin0 - blk * cmin_step, 0))
    out_ref[0, 0] = jnp.where(ok, acc * mult, NEG)


def _bias_table(rel_bias, h0, nh, rows, cols, a_col, d0, lo, hi, rt, mult=1.0, nblk=1, blk_d0=0, cmin0=0, cmin_step=0):
    return pl.pallas_call(
        functools.partial(_table_kernel, h0, a_col, d0, blk_d0, lo, hi, cmin0, cmin_step, mult),
        grid=(nh, nblk, rows // rt),
        in_specs=[pl.BlockSpec(memory_space=pltpu.SMEM)],
        out_specs=pl.BlockSpec((1, 1, rt, cols), lambda h, b, r: (h, b, r, 0)),
        out_shape=jax.ShapeDtypeStruct((nh, nblk, rows, cols), F32),
        compiler_params=_cparams(("arbitrary", "arbitrary", "arbitrary")),
        name="bias_table",
    )(rel_bias)


def _rms(v, g):
    return v * lax.rsqrt(jnp.mean(v * v, axis=-1, keepdims=True) + NORM_EPS) * g


def _proj_kernel(x_ref, gpre_ref, w_ref, qn_ref, wuq_ref, kvn_ref, wk_ref, wv_ref,
                 rc_ref, rsa_ref, rsb_ref, eaug_ref,
                 qa_ref, cmp_ref, slck_ref, slcv_ref, win_ref, qm_ref, km_ref, vm_ref,
                 krb_ref, cq_ref, ckv_ref, z_ref):
    hb = _rms(x_ref[...], gpre_ref[...]).astype(BF16)

    chunks = {}

    def proj(c0, c1):
        for (a, b), y in chunks.items():
            if a <= c0 and c1 <= b:
                return y[:, c0 - a:c1 - a]
        raise AssertionError((c0, c1))

    for a, b in ((C_AQ, C_CMP), (C_CMP, C_WIN), (C_WIN, C_KRB), (C_KRB, C_Z), (C_Z, C_END)):
        chunks[(a, b)] = _dot(hb, w_ref[:, a:b])

    tm = hb.shape[0]
    lane = lax.broadcasted_iota(jnp.int32, (tm, LANES), 1)
    lo = lane < HEAD_DIM

    qa_ref[...] = (proj(C_AQ, C_CMP) * 0.125).astype(BF16)
    cmp_ref[...] = proj(C_CMP, C_SLC)
    slc = proj(C_SLC, C_WIN)
    slcv_ref[...] = jnp.where(lo, 1.0, slc).astype(BF16)
    slck_ref[...] = jnp.where(lo, slc, eaug_ref[...].astype(F32)).astype(BF16)
    win_ref[...] = proj(C_WIN, C_BCQ).astype(BF16)
    cq_ref[...] = (proj(C_CQ, C_CKV) * 0.125).astype(BF16)
    ckv_ref[...] = proj(C_CKV, C_Z).astype(BF16)
    z_ref[...] = proj(C_Z, C_END)

    rc, rsa, rsb = rc_ref[...], rsa_ref[...], rsb_ref[...]

    def rope(v):
        return v * rc + pltpu.roll(v, 16, 1) * rsa + pltpu.roll(v, 112, 1) * rsb

    cqn = _rms(proj(C_BCQ, C_BCKV), qn_ref[...]).astype(BF16)
    qm = _dot(cqn, wuq_ref[...])
    for h in range(MLA_HEADS):
        qm_ref[:, LANES * h:LANES * (h + 1)] = rope(qm[:, LANES * h:LANES * (h + 1)]).astype(BF16)

    krb = proj(C_KRB, C_CQ)
    krb_ref[...] = krb
    krr = jnp.where(lo | (lane >= HEAD_DIM + MLA_ROPE), 0.0, rope(krb))

    ckvn = _rms(proj(C_BCKV, C_KRB), kvn_ref[...]).astype(BF16)
    kn = _dot(ckvn, wk_ref[...])
    for h in range(MLA_HEADS):
        km_ref[:, LANES * h:LANES * (h + 1)] = (kn[:, LANES * h:LANES * (h + 1)] + krr).astype(BF16)
    vm = _dot(ckvn, wv_ref[...])
    for h in range(MLA_HEADS):
        vm_ref[:, LANES * h:LANES * (h + 1)] = jnp.where(lo, 1.0, vm[:, LANES * h:LANES * (h + 1)]).astype(BF16)


def _project(x2, gpre, w, qn, wuq, kvn, wk, wv, rope_c, rope_sa, rope_sb, eaug, T, tm):
    BT = x2.shape[0]
    nt = T // tm
    row = lambda i: (i, 0)
    fix = lambda i: (0, 0)
    pos = lambda i: (i % nt, 0)
    widths = [(256, BF16), (128, F32), (128, BF16), (128, BF16), (128, BF16), (512, BF16), (512, BF16),
              (512, BF16), (128, F32), (512, BF16), (128, BF16), (1024, F32)]
    return pl.pallas_call(
        _proj_kernel,
        grid=(BT // tm,),
        in_specs=[pl.BlockSpec((tm, D_MODEL), row), pl.BlockSpec((1, D_MODEL), fix),
                  pl.BlockSpec(w.shape, fix), pl.BlockSpec(qn.shape, fix), pl.BlockSpec(wuq.shape, fix),
                  pl.BlockSpec(kvn.shape, fix), pl.BlockSpec(wk.shape, fix), pl.BlockSpec(wv.shape, fix),
                  pl.BlockSpec((tm, LANES), pos), pl.BlockSpec((tm, LANES), pos), pl.BlockSpec((tm, LANES), pos),
                  pl.BlockSpec((tm, LANES), pos)],
        out_specs=[pl.BlockSpec((tm, wd), row) for wd, _ in widths],
        out_shape=[jax.ShapeDtypeStruct((BT, wd), dt) for wd, dt in widths],
        compiler_params=_cparams(("arbitrary",)),
        name="proj",
    )(x2, gpre, w, qn, wuq, kvn, wk, wv, rope_c, rope_sa, rope_sb, eaug)


def _compress_kernel(c_ref, pet_ref, peb_ref, w1t_ref, w1b_ref, w2_ref, o_ref):
    c = c_ref[0]
    nch = c.shape[0]
    top = (c + pet_ref[...]).astype(BF16)
    bot = (c + peb_ref[...]).astype(BF16)
    out = jnp.zeros((nch, LANES), F32)
    for i in range(2):
        pre = _dot(top, w1t_ref[i]) + pltpu.roll(_dot(bot, w1b_ref[i]), nch - 1, 0)
        hid = pre * (1.0 / (1.0 + jnp.exp(-pre)))
        out = out + _dot(hid.astype(BF16), w2_ref[i])
    o_ref[0] = out.astype(BF16)


def _compress(chunks, pet, peb, w1t, w1b, w2):
    B, nch, width = chunks.shape
    full = lambda a: pl.BlockSpec(a.shape, lambda b: (0,) * a.ndim)
    return pl.pallas_call(
        _compress_kernel,
        grid=(B,),
        in_specs=[pl.BlockSpec((1, nch, width), lambda b: (b, 0, 0)),
                  full(pet), full(peb), full(w1t), full(w1b), full(w2)],
        out_specs=pl.BlockSpec((1, nch, LANES), lambda b: (b, 0, 0)),
        out_shape=jax.ShapeDtypeStruct((B, nch, LANES), BF16),
        compiler_params=_cparams(("arbitrary",)),
        name="compress",
    )(chunks, pet, peb, w1t, w1b, w2)


def _head_slabs(qblk, n_heads):
    q = qblk.astype(F32)
    lo = lax.broadcasted_iota(jnp.int32, (q.shape[0], LANES), 1) < HEAD_DIM
    out = []
    for j in range(n_heads // 2):
        slab = q[:, LANES * j:LANES * (j + 1)]
        out.append(jnp.where(lo, slab, 0.0))
        out.append(jnp.where(lo, pltpu.roll(slab, HEAD_DIM, 1), 0.0))
    return out


def _merge_upper(accs):
    lo = lax.broadcasted_iota(jnp.int32, accs[0].shape, 1) < HEAD_DIM
    slabs = [jnp.where(lo, pltpu.roll(accs[2 * j], HEAD_DIM, 1), accs[2 * j + 1]) for j in range(len(accs) // 2)]
    return jnp.concatenate(slabs, axis=1)


def _merge_normalized(accs):
    lo = lax.broadcasted_iota(jnp.int32, accs[0].shape, 1) < HEAD_DIM
    slabs = []
    for j in range(len(accs) // 2):
        a, b = accs[2 * j], accs[2 * j + 1]
        out = jnp.where(lo, pltpu.roll(a, HEAD_DIM, 1), b)
        den = jnp.where(lo, a, pltpu.roll(b, HEAD_DIM, 1))
        slabs.append(out / den)
    return jnp.concatenate(slabs, axis=1)


def _row_max_lanes(s):
    return jnp.broadcast_to(jnp.max(s, axis=-1, keepdims=True), (s.shape[0], LANES))


def _sub_lanes(s, ref):
    return jnp.concatenate([s[:, LANES * i:LANES * (i + 1)] - ref for i in range(s.shape[1] // LANES)], axis=1)


def _topk_mask_t(imp, t0):
    ns, tq = imp.shape
    srow = lax.broadcasted_iota(jnp.int32, (ns, tq), 0)
    cur = (t0 + lax.broadcasted_iota(jnp.int32, (ns, tq), 1)) // SEL_BLOCK
    forced = (srow == 0) | (srow == cur) | (srow == cur - 1)
    x = jnp.where(forced, BIG, jnp.where(srow <= cur, imp, -BIG))
    sub = 8
    groups = [x[sub * v:sub * (v + 1)] for v in range(ns // sub)]
    rows_in = [srow[sub * v:sub * (v + 1)] for v in range(ns // sub)]
    cnts = [jnp.zeros((sub, tq), F32) for _ in groups]
    for sp in range(ns):
        other = jnp.broadcast_to(x[sp:sp + 1, :], (sub, tq))
        for v, xv in enumerate(groups):
            if sub * v > sp:
                beats = other >= xv
            elif sub * v + sub - 1 < sp:
                beats = other > xv
            else:
                beats = (other > xv) | ((other == xv) & (rows_in[v] > sp))
            cnts[v] = cnts[v] + jnp.where(beats, 1.0, 0.0)
    cnt = jnp.concatenate(cnts, axis=0)
    sel = (cnt < float(min(SEL_TOP, ns))) & (srow <= cur)
    return jnp.where(sel, 0.0, NEG)


def _cmp_kernel(q_ref, kvc_ref, bc_ref, ovt_ref, ocmp_ref, selb_ref):
    tq = Q_TILE
    kvc = kvc_ref[0]
    ovt = ovt_ref[...]
    ns = ovt.shape[0]
    groups = range(CMP_GROUP)
    sls = [slice(g * tq, (g + 1) * tq) for g in groups]
    s_alls = []
    for g in groups:
        qs = _head_slabs(q_ref[0, sls[g], :], NSA_HEADS)
        s_alls.append(_dot_nt(jnp.concatenate([q.astype(BF16) for q in qs], axis=0), kvc))
    psums, o_alls = [], []
    for g in groups:
        valid = bc_ref[0, sls[g], :] > 0.5 * NEG
        psum = jnp.zeros(valid.shape, F32)
        ps = []
        for h in range(NSA_HEADS):
            s = s_alls[g][h * tq:(h + 1) * tq] + bc_ref[h, sls[g], :]
            m = jnp.max(s, axis=-1, keepdims=True)
            e = jnp.where(valid, jnp.exp(s - m), 0.0)
            l = jnp.sum(e, axis=-1, keepdims=True)
            p = e * (1.0 / jnp.where(l > 0.0, l, 1.0))
            psum = psum + p
            ps.append(p.astype(BF16))
        psums.append(psum)
        o_alls.append(_dot(jnp.concatenate(ps, axis=0), kvc))
    imps = []
    for g in groups:
        hi = psums[g].astype(BF16)
        lo = (psums[g] - hi.astype(F32)).astype(BF16)
        imps.append(_dot_nt(ovt, hi) + _dot_nt(ovt, lo))
    for g in groups:
        ocmp = _merge_upper([o_alls[g][h * tq:(h + 1) * tq] for h in range(NSA_HEADS)])
        ocmp_ref[0, sls[g], :] = ocmp.astype(ocmp_ref.dtype)
        sb = _topk_mask_t(imps[g], (pl.program_id(1) * CMP_GROUP + g) * tq)
        if ns < HEAD_DIM:
            sb = jnp.concatenate([sb, jnp.full((HEAD_DIM - ns, tq), NEG, F32)], axis=0)
        full = jnp.concatenate([jnp.zeros((HEAD_DIM, tq), F32), sb], axis=0)
        selb_ref[0, sls[g], :] = full.T.astype(BF16)


def _cmp_attention(qa, kvc, bc, ovt, T):
    B = qa.shape[0]
    ncp = kvc.shape[1]
    rows = CMP_GROUP * Q_TILE
    return pl.pallas_call(
        _cmp_kernel,
        grid=(B, T // rows),
        in_specs=[pl.BlockSpec((1, rows, 256), lambda b, n: (b, n, 0)),
                  pl.BlockSpec((1, ncp, LANES), lambda b, n: (b, 0, 0)),
                  pl.BlockSpec((NSA_HEADS, rows, ncp), lambda b, n: (0, n, 0)),
                  pl.BlockSpec(ovt.shape, lambda b, n: (0, 0))],
        out_specs=[pl.BlockSpec((1, rows, 256), lambda b, n: (b, n, 0)),
                   pl.BlockSpec((1, rows, LANES), lambda b, n: (b, n, 0))],
        out_shape=[jax.ShapeDtypeStruct((B, T, 256), BF16), jax.ShapeDtypeStruct((B, T, LANES), BF16)],
        compiler_params=_cparams(("arbitrary", "arbitrary")),
        name="cmp_select",
    )(qa, kvc, bc, ovt)


def _softmax_tile(s2, rows, m_s, shift=None):
    m_t = _row_max_lanes(s2)
    if shift is not None:
        m_t = m_t + shift
    m_old = m_s[rows]
    m_new = jnp.maximum(m_old, m_t)
    p = jnp.exp2(_sub_lanes(s2, m_new if shift is None else m_new - shift))
    alpha = jnp.exp2(m_old - m_new)
    m_s[rows] = m_new
    return p.astype(BF16), alpha


def _slc_kernel(b31_ref, q_ref, selb_ref, ka_ref, kv_ref, tab_ref, o_ref, m_s, acc_s):
    tq = q_ref.shape[1]
    H = NSA_HEADS
    n = pl.program_id(1)
    qs = _head_slabs(q_ref[0], H)
    selb = selb_ref[0].astype(F32)
    qst = jnp.concatenate([(qs[h] + selb).astype(BF16) for h in range(H)], axis=0)
    m_s[...] = jnp.full(m_s.shape, -jnp.inf, F32)
    acc_s[...] = jnp.zeros(acc_s.shape, F32)

    def step(k0, near_block):
        ka = ka_ref[0, pl.ds(k0, tq), :]
        kv = kv_ref[0, pl.ds(k0, tq), :]
        s_all = _dot_nt(qst, ka)
        ps, alphas = [], []
        for h in range(H):
            rows = slice(h * tq, (h + 1) * tq)
            s2 = s_all[rows] * LOG2E
            if near_block is None:
                p, alpha = _softmax_tile(s2, rows, m_s, shift=b31_ref[h])
            else:
                s2 = s2 + tab_ref[h, near_block * tq:(near_block + 1) * tq, :]
                p, alpha = _softmax_tile(s2, rows, m_s)
            ps.append(p)
            alphas.append(alpha)
        pv = _dot(jnp.concatenate(ps, axis=0), kv)
        for h in range(H):
            rows = slice(h * tq, (h + 1) * tq)
            acc_s[rows] = alphas[h] * acc_s[rows] + pv[rows]

    def far(j, carry):
        step(pl.multiple_of(j * tq, tq), None)
        return carry

    lax.fori_loop(0, jnp.maximum(n - 1, 0), far, 0)

    @pl.when(n >= 1)
    def _():
        step(pl.multiple_of(jnp.maximum(n - 1, 0) * tq, tq), 1)

    step(pl.multiple_of(n * tq, tq), 0)
    o_ref[0] = _merge_normalized([acc_s[h * tq:(h + 1) * tq] for h in range(H)]).astype(o_ref.dtype)


def _slc_attention(b31, qa, selb, slck, slcv, tab, T):
    B = qa.shape[0]
    tq = FLASH_TILE
    H = NSA_HEADS
    tile = lambda w: pl.BlockSpec((1, tq, w), lambda b, n: (b, n, 0))
    return pl.pallas_call(
        _slc_kernel,
        grid=(B, T // tq),
        in_specs=[pl.BlockSpec(memory_space=pltpu.SMEM),
                  tile(256), tile(LANES),
                  pl.BlockSpec((1, T, LANES), lambda b, n: (b, 0, 0)),
                  pl.BlockSpec((1, T, LANES), lambda b, n: (b, 0, 0)),
                  pl.BlockSpec(tab.shape, lambda b, n: (0, 0, 0))],
        out_specs=tile(256),
        out_shape=jax.ShapeDtypeStruct((B, T, 256), BF16),
        scratch_shapes=[pltpu.VMEM((H * tq, LANES), F32), pltpu.VMEM((H * tq, LANES), F32)],
        compiler_params=_cparams(("arbitrary", "arbitrary")),
        name="slc_attention",
    )(b31, qa, selb, slck, slcv, tab)


def _band_kernel(n_heads, window, has_sink, n_sub, *refs):
    if has_sink:
        q_ref, kv_ref, tab_ref, sink_ref, o_ref = refs
    else:
        q_ref, kv_ref, tab_ref, o_ref = refs
    tq = Q_TILE
    span = window + tq
    lo_kv = lax.broadcasted_iota(jnp.int32, (span, LANES), 1) < HEAD_DIM
    lo_o = lax.broadcasted_iota(jnp.int32, (n_heads * tq, LANES), 1) < HEAD_DIM
    groups = range(n_sub)
    tiles = [pl.program_id(1) * n_sub + g for g in groups]
    kvs, dots = [], []
    for g in groups:
        q0 = pl.multiple_of(tiles[g] * tq, tq)
        kvs.append(kv_ref[0, pl.ds(q0, span), :])
        qs = _head_slabs(q_ref[0, g * tq:(g + 1) * tq, :], n_heads)
        dots.append(_dot_nt(jnp.concatenate([q.astype(BF16) for q in qs], axis=0), kvs[g]))
    accs, ms = [], []
    for g in groups:
        s = dots[g] + tab_ref[jnp.minimum(tiles[g], window // tq)]
        m = _row_max_lanes(s)
        if has_sink:
            m = jnp.maximum(m, sink_ref[...])
        e = jnp.exp(_sub_lanes(s, m))
        ms.append(m)
        accs.append(_dot(e.astype(BF16), jnp.where(lo_kv, 1.0, kvs[g]).astype(BF16)))
    for g in groups:
        acc = accs[g]
        if has_sink:
            acc = acc + jnp.where(lo_o, jnp.exp(sink_ref[...] - ms[g]), 0.0)
        out = _merge_normalized([acc[h * tq:(h + 1) * tq] for h in range(n_heads)])
        o_ref[0, g * tq:(g + 1) * tq, :] = out.astype(o_ref.dtype)


def _band_attention(q, kv_pad, tab, sinks, n_heads, window, T, out_dtype):
    B = q.shape[0]
    width = n_heads * HEAD_DIM
    has_sink = sinks is not None
    nvar = tab.shape[1]
    tab2 = jnp.transpose(tab, (1, 0, 2, 3)).reshape(nvar, n_heads * Q_TILE, window + Q_TILE)
    n_sub = BAND_ROWS // (n_heads * Q_TILE)
    rows = n_sub * Q_TILE
    in_specs = [pl.BlockSpec((1, rows, width), lambda b, n: (b, n, 0)),
                pl.BlockSpec((1, T + window, LANES), lambda b, n: (b, 0, 0)),
                pl.BlockSpec(tab2.shape, lambda b, n: (0, 0, 0))]
    args = (q, kv_pad, tab2)
    if has_sink:
        rep = jnp.broadcast_to(jnp.repeat(sinks, Q_TILE)[:, None], (n_heads * Q_TILE, LANES))
        in_specs.append(pl.BlockSpec(rep.shape, lambda b, n: (0, 0)))
        args = args + (rep,)
    return pl.pallas_call(
        functools.partial(_band_kernel, n_heads, window, has_sink, n_sub),
        grid=(B, T // rows),
        in_specs=in_specs,
        out_specs=pl.BlockSpec((1, rows, width), lambda b, n: (b, n, 0)),
        out_shape=jax.ShapeDtypeStruct((B, T, width), out_dtype),
        compiler_params=_cparams(("arbitrary", "arbitrary")),
        name="band_sink" if has_sink else "band_window",
    )(*args)


def _mla_kernel(q_ref, k_ref, v_ref, o_ref, m_s, acc_s):
    tq = q_ref.shape[1]
    H = MLA_HEADS
    n = pl.program_id(1)
    c2 = float((MLA_NOPE + MLA_ROPE) ** -0.5) * LOG2E
    m_s[...] = jnp.full(m_s.shape, -jnp.inf, F32)
    acc_s[...] = jnp.zeros(acc_s.shape, F32)

    def step(k0, diag):
        ss = [_dot_nt(q_ref[0, :, LANES * h:LANES * (h + 1)], k_ref[0, pl.ds(k0, tq), LANES * h:LANES * (h + 1)])
              for h in range(H)]
        ps, alphas = [], []
        for h in range(H):
            rows = slice(h * tq, (h + 1) * tq)
            s2 = ss[h] * c2
            if diag:
                row = lax.broadcasted_iota(jnp.int32, (tq, tq), 0)
                col = lax.broadcasted_iota(jnp.int32, (tq, tq), 1)
                s2 = jnp.where(col <= row, s2, NEG)
            p, alpha = _softmax_tile(s2, rows, m_s)
            ps.append(p)
            alphas.append(alpha)
        for h in range(H):
            rows = slice(h * tq, (h + 1) * tq)
            v = v_ref[0, pl.ds(k0, tq), LANES * h:LANES * (h + 1)]
            acc_s[rows] = alphas[h] * acc_s[rows] + _dot(ps[h], v)

    def far(j, carry):
        step(pl.multiple_of(j * tq, tq), False)
        return carry

    lax.fori_loop(0, n, far, 0)
    step(pl.multiple_of(n * tq, tq), True)
    o_ref[0] = _merge_normalized([acc_s[h * tq:(h + 1) * tq] for h in range(H)]).astype(o_ref.dtype)


def _mla_attention(qm, km, vm, T):
    B = qm.shape[0]
    tq = FLASH_TILE
    H = MLA_HEADS
    return pl.pallas_call(
        _mla_kernel,
        grid=(B, T // tq),
        in_specs=[pl.BlockSpec((1, tq, 512), lambda b, n: (b, n, 0)),
                  pl.BlockSpec((1, T, 512), lambda b, n: (b, 0, 0)),
                  pl.BlockSpec((1, T, 512), lambda b, n: (b, 0, 0))],
        out_specs=pl.BlockSpec((1, tq, 256), lambda b, n: (b, n, 0)),
        out_shape=jax.ShapeDtypeStruct((B, T, 256), BF16),
        scratch_shapes=[pltpu.VMEM((H * tq, LANES), F32), pltpu.VMEM((H * tq, LANES), F32)],
        compiler_params=_cparams(("arbitrary", "arbitrary")),
        name="mla_attention",
    )(qm, km, vm)


def _out_kernel(x_ref, ocmp_ref, oslc_ref, owin_ref, krb_ref, gexp_ref, ob_ref, oc_ref, z_ref, w_ref, gpost_ref, o_ref):
    sig = 1.0 / (1.0 + jnp.exp(-krb_ref[...]))
    hi = sig.astype(BF16)
    lo = (sig - hi.astype(F32)).astype(BF16)
    gates = _dot(hi, gexp_ref[...]) + _dot(lo, gexp_ref[...])
    w_a = NSA_HEADS * HEAD_DIM
    oa = None
    for j, br in enumerate((ocmp_ref, oslc_ref, owin_ref)):
        term = gates[:, w_a * j:w_a * (j + 1)] * br[...].astype(F32)
        oa = term if oa is None else oa + term
    z = z_ref[...]
    heads = jnp.concatenate([oa, ob_ref[...].astype(F32), oc_ref[...].astype(F32)], axis=1)
    mixed = heads * (z * (1.0 / (1.0 + jnp.exp(-z))))
    y = _dot(mixed.astype(BF16), w_ref[...])
    o_ref[...] = x_ref[...] + _rms(y, gpost_ref[...])


def _out_project(x2, ocmp, oslc, owin, krb, gexp, ob, oc, z, w, gpost, tm):
    BT = x2.shape[0]
    row = lambda i: (i, 0)
    fix = lambda i: (0, 0)
    spec = lambda a: pl.BlockSpec((tm, a.shape[1]), row)
    return pl.pallas_call(
        _out_kernel,
        grid=(BT // tm,),
        in_specs=[spec(x2), spec(ocmp), spec(oslc), spec(owin), spec(krb), pl.BlockSpec(gexp.shape, fix),
                  spec(ob), spec(oc), spec(z), pl.BlockSpec(w.shape, fix), pl.BlockSpec(gpost.shape, fix)],
        out_specs=pl.BlockSpec((tm, D_MODEL), row),
        out_shape=jax.ShapeDtypeStruct((BT, D_MODEL), F32),
        compiler_params=_cparams(("arbitrary",)),
        name="out_proj",
    )(x2, ocmp, oslc, owin, krb, gexp, ob, oc, z, w, gpost)


def _split_in(w):
    outs, o = [], 0
    for s in IN_SIZES:
        outs.append(w[..., o:o + s])
        o += s
    return outs


def _prep_w_in(w):
    a_q, a_kv, a_g, b_cq, b_ckv, b_kr, c_q, c_kv, z = _split_in(w)
    zeros = lambda n: jnp.zeros(w.shape[:-1] + (n,), w.dtype)
    krb = jnp.concatenate([zeros(HEAD_DIM), b_kr, a_g, zeros(LANES - HEAD_DIM - MLA_ROPE - 3 * NSA_HEADS)], axis=-1)
    return jnp.concatenate([a_q, a_kv, b_cq, b_ckv, krb, c_q, c_kv, z], axis=-1).astype(BF16)


def _prep_mla(w_uq, w_ukv):
    L = w_uq.shape[0]
    dq = MLA_NOPE + MLA_ROPE
    uq = w_uq.reshape(L, MLA_Q_RANK, MLA_HEADS, dq)
    uq = jnp.concatenate([uq, jnp.zeros((L, MLA_Q_RANK, MLA_HEADS, LANES - dq), uq.dtype)], axis=-1)
    ukv = w_ukv.reshape(L, MLA_KV_RANK, MLA_HEADS, MLA_NOPE + MLA_V)
    uk = jnp.concatenate([ukv[..., :MLA_NOPE], jnp.zeros((L, MLA_KV_RANK, MLA_HEADS, LANES - MLA_NOPE), ukv.dtype)], axis=-1)
    uv = jnp.concatenate([jnp.zeros((L, MLA_KV_RANK, MLA_HEADS, LANES - MLA_V), ukv.dtype), ukv[..., MLA_NOPE:]], axis=-1)
    return (uq.reshape(L, MLA_Q_RANK, MLA_HEADS * LANES).astype(BF16),
            uk.reshape(L, MLA_KV_RANK, MLA_HEADS * LANES).astype(BF16),
            uv.reshape(L, MLA_KV_RANK, MLA_HEADS * LANES).astype(BF16))


def _prep_compress(cmp_pos, cmp_w1, cmp_w2):
    L = cmp_pos.shape[0]
    pe = jnp.concatenate([cmp_pos[:, 0], cmp_pos[:, 1]], axis=-1)
    pet = pe[:, :CMP_STRIDE].reshape(L, 1, CMP_STRIDE * LANES)
    peb = pe[:, CMP_STRIDE:].reshape(L, 1, CMP_STRIDE * LANES)
    w1 = cmp_w1.reshape(L, 2, CMP_LEN, HEAD_DIM, CMP_HIDDEN)
    zero = jnp.zeros_like(w1[:, 0])
    w1k = jnp.concatenate([w1[:, 0], zero], axis=2)
    w1v = jnp.concatenate([zero, w1[:, 1]], axis=2)
    w1e = jnp.stack([w1k, w1v], axis=1)
    w1t = w1e[:, :, :CMP_STRIDE].reshape(L, 2, CMP_STRIDE * LANES, CMP_HIDDEN).astype(BF16)
    w1b = w1e[:, :, CMP_STRIDE:].reshape(L, 2, CMP_STRIDE * LANES, CMP_HIDDEN).astype(BF16)
    z2 = jnp.zeros_like(cmp_w2[:, 0])
    w2 = jnp.stack([jnp.concatenate([cmp_w2[:, 0], z2], axis=-1),
                    jnp.concatenate([z2, cmp_w2[:, 1]], axis=-1)], axis=1).astype(BF16)
    return pet, peb, w1t, w1b, w2


def _rope_tables(T):
    half = MLA_ROPE // 2
    inv = ROPE_THETA ** (-jnp.arange(half, dtype=F32) / half)
    ang = jnp.arange(T).astype(F32)[:, None] * inv[None, :]
    cos, sin = jnp.cos(ang), jnp.sin(ang)
    z = lambda n: jnp.zeros((T, n), F32)
    rc = jnp.concatenate([jnp.ones((T, MLA_NOPE), F32), cos, cos, z(LANES - MLA_NOPE - MLA_ROPE)], axis=1)
    rsa = jnp.concatenate([z(MLA_NOPE + half), sin, z(LANES - MLA_NOPE - MLA_ROPE)], axis=1)
    rsb = jnp.concatenate([z(MLA_NOPE), -sin, z(LANES - MLA_NOPE - half)], axis=1)
    return rc, rsa, rsb


def _static_tables(T):
    ns = T // SEL_BLOCK
    nc = T // CMP_STRIDE - 1
    ncp = T // CMP_STRIDE
    sstart = np.arange(ns) * SEL_BLOCK
    cstart = np.arange(nc) * CMP_STRIDE
    overlap = (np.clip(np.minimum(cstart[:, None] + CMP_LEN, sstart[None, :] + SEL_BLOCK)
                       - np.maximum(cstart[:, None], sstart[None, :]), 0, None) / CMP_STRIDE).astype(np.float32)
    ovt = np.zeros((ns, ncp), np.float32)
    ovt[:, :nc] = overlap.T
    eaug = np.zeros((T, LANES), np.float32)
    eaug[np.arange(T), HEAD_DIM + np.arange(T) // SEL_BLOCK] = 1.0
    gexp = np.zeros((LANES, 3 * NSA_HEADS * HEAD_DIM), np.float32)
    for h in range(NSA_HEADS):
        for j in range(3):
            c0 = NSA_HEADS * HEAD_DIM * j + HEAD_DIM * h
            gexp[GATE_LANE0 + 3 * h + j, c0:c0 + HEAD_DIM] = 1.0
    return jnp.asarray(ovt, BF16), jnp.asarray(eaug, BF16), jnp.asarray(gexp, BF16)


def kernel(x, w_in, w_out, norm_pre, norm_post, cmp_pos, cmp_w1, cmp_w2, mla_q_norm, mla_w_uq, mla_kv_norm,
           mla_w_ukv, swa_sinks, rel_bias):
    B, T, D = x.shape
    depth = w_in.shape[0]
    assert D == D_MODEL and T % (2 * FLASH_TILE) == 0 and T // SEL_BLOCK <= HEAD_DIM
    ncp = T // CMP_STRIDE
    tm = 512

    w_in_p = _prep_w_in(w_in)
    wuq, wk, wv = _prep_mla(mla_w_uq, mla_w_ukv)
    pet, peb, w1t, w1b, w2 = _prep_compress(cmp_pos, cmp_w1, cmp_w2)
    w_out_b = w_out.astype(BF16)
    rope_c, rope_sa, rope_sb = _rope_tables(T)
    ovt, eaug, gexp = _static_tables(T)

    bc = _bias_table(rel_bias, 0, NSA_HEADS, T, ncp, -CMP_STRIDE, -(CMP_LEN - 1), 0, 1 << 30, 256).reshape(NSA_HEADS, T, ncp)
    tab_near = _bias_table(rel_bias, 0, NSA_HEADS, FLASH_TILE, FLASH_TILE, -1, 0, 0, 1 << 30, 256, mult=LOG2E,
                           nblk=2, blk_d0=FLASH_TILE).reshape(NSA_HEADS, 2 * FLASH_TILE, FLASH_TILE)
    tab_win = _bias_table(rel_bias, 0, NSA_HEADS, Q_TILE, NSA_WINDOW + Q_TILE, -1, NSA_WINDOW, 0, NSA_WINDOW, Q_TILE,
                          nblk=NSA_WINDOW // Q_TILE + 1, cmin0=NSA_WINDOW, cmin_step=Q_TILE)
    tab_swa = _bias_table(rel_bias, NSA_HEADS, SWA_HEADS, Q_TILE, SWA_WINDOW + Q_TILE, -1, SWA_WINDOW, 0, SWA_WINDOW, Q_TILE,
                          nblk=SWA_WINDOW // Q_TILE + 1, cmin0=SWA_WINDOW, cmin_step=Q_TILE)
    b31 = rel_bias[:NSA_HEADS, REL_BUCKETS - 1] * LOG2E

    x2 = x.reshape(B * T, D)
    for l in range(depth):
        (qa, cmpkv, slck, slcv, win, qm, km, vm, krb, cq, ckv, z) = _project(
            x2, norm_pre[l][None], w_in_p[l], mla_q_norm[l][None], wuq[l], mla_kv_norm[l][None], wk[l], wv[l],
            rope_c, rope_sa, rope_sb, eaug, T, tm)
        r3 = lambda a: a.reshape(B, T, a.shape[-1])
        kvc = _compress(cmpkv.reshape(B, ncp, CMP_STRIDE * LANES), pet[l], peb[l], w1t[l], w1b[l], w2[l])
        qa3 = r3(qa)
        ocmp, selb = _cmp_attention(qa3, kvc, bc, ovt, T)
        win_pad = jnp.pad(r3(win), ((0, 0), (NSA_WINDOW, 0), (0, 0)))
        owin = _band_attention(qa3, win_pad, tab_win, None, NSA_HEADS, NSA_WINDOW, T, BF16)
        oslc = _slc_attention(b31, qa3, selb, r3(slck), r3(slcv), tab_near, T)
        ckv_pad = jnp.pad(r3(ckv), ((0, 0), (SWA_WINDOW, 0), (0, 0)))
        oc = _band_attention(r3(cq), ckv_pad, tab_swa, swa_sinks[l], SWA_HEADS, SWA_WINDOW, T, BF16)
        ob = _mla_attention(r3(qm), r3(km), r3(vm), T)
        flat = lambda a: a.reshape(B * T, a.shape[-1])
        x2 = _out_project(x2, flat(ocmp), flat(oslc), flat(owin), krb, gexp, flat(ob), flat(oc), z,
                          w_out_b[l], norm_post[l][None], tm)
    return x2.reshape(B, T, D)
```

```python
import functools
import math

import numpy as np
import jax
import jax.numpy as jnp
from jax import lax
from jax.experimental import pallas as pl
from jax.experimental.pallas import tpu as pltpu

F32 = jnp.float32
BF16 = jnp.bfloat16

D_MODEL = 1024
HEAD_DIM = 64
NSA_HEADS = 4
CMP_LEN = 32
CMP_STRIDE = 16
CMP_HIDDEN = 128
SEL_BLOCK = 64
SEL_TOP = 16
NSA_WINDOW = 512
MLA_HEADS = 4
MLA_Q_RANK = 256
MLA_KV_RANK = 128
MLA_NOPE = 64
MLA_ROPE = 32
MLA_V = 64
ROPE_THETA = 10000.0
SWA_HEADS = 8
SWA_WINDOW = 128
REL_BUCKETS = 32
REL_MAX_DIST = 512
NORM_EPS = 1e-6
NEG = -1e30
BIG = 1e9
LOG2E = math.log2(math.e)
IN_SIZES = (256, 384, 12, 256, 128, 32, 512, 128, 1024)

LANES = 128
Q_TILE = 128
BAND_ROWS = 4096
CMP_GROUP = 4
FLASH_TILE = 512
GATE_LANE0 = 96
VMEM_LIMIT = 56 * 1024 * 1024

C_AQ, C_CMP, C_SLC, C_WIN, C_BCQ, C_BCKV, C_KRB, C_CQ, C_CKV, C_Z = (
    0, 256, 384, 512, 640, 896, 1024, 1152, 1664, 1792)


def _dot(a, b):
    return jnp.dot(a, b, preferred_element_type=F32)


def _dot_nt(a, b):
    return lax.dot_general(a, b, (((1,), (1,)), ((), ())), preferred_element_type=F32)


def _bucket_thresholds():
    d = np.arange(0, 4 * REL_MAX_DIST)
    exact = REL_BUCKETS // 2
    large = exact + (np.log(np.maximum(d, 1).astype(np.float32) / np.float32(exact))
                     / np.float32(math.log(REL_MAX_DIST / exact)) * np.float32(REL_BUCKETS - exact)).astype(np.int32)
    b = np.where(d < exact, d, np.minimum(large, REL_BUCKETS - 1))
    assert np.all(np.diff(b) >= 0) and b[-1] == REL_BUCKETS - 1
    return [int(np.argmax(b >= k)) for k in range(REL_BUCKETS)]


_THRESH = _bucket_thresholds()


def _cparams(sem):
    return pltpu.CompilerParams(dimension_semantics=sem, vmem_limit_bytes=VMEM_LIMIT)


def _table_kernel(h0, a_col, d0, blk_d0, lo, hi, cmin0, cmin_step, mult, rb_ref, out_ref):
    h = pl.program_id(0) + h0
    blk = pl.program_id(1)
    rt, cc = out_ref.shape[2], out_ref.shape[3]
    r = pl.program_id(2) * rt + lax.broadcasted_iota(jnp.int32, (rt, cc), 0)
    c = lax.broadcasted_iota(jnp.int32, (rt, cc), 1)
    dist = r + a_col * c + d0 + blk * blk_d0
    acc = jnp.full((rt, cc), rb_ref[h, 0], F32)
    for k in range(1, REL_BUCKETS):
        acc = jnp.where(dist >= _THRESH[k], rb_ref[h, k], acc)
    ok = (dist >= lo) & (dist < hi) & (c >= jnp.maximum(cmin0 - blk * cmin_step, 0))
    out_ref[0, 0] = jnp.where(ok, acc * mult, NEG)


def _bias_table(rel_bias, h0, nh, rows, cols, a_col, d0, lo, hi, rt, mult=1.0, nblk=1, blk_d0=0, cmin0=0, cmin_step=0):
    return pl.pallas_call(
        functools.partial(_table_kernel, h0, a_col, d0, blk_d0, lo, hi, cmin0, cmin_step, mult),
        grid=(nh, nblk, rows // rt),
        in_specs=[pl.BlockSpec(memory_space=pltpu.SMEM)],
        out_specs=pl.BlockSpec((1, 1, rt, cols), lambda h, b, r: (h, b, r, 0)),
        out_shape=jax.ShapeDtypeStruct((nh, nblk, rows, cols), F32),
        compiler_params=_cparams(("arbitrary", "arbitrary", "arbitrary")),
        name="bias_table",
    )(rel_bias)


def _rms(v, g):
    return v * lax.rsqrt(jnp.mean(v * v, axis=-1, keepdims=True) + NORM_EPS) * g


def _proj_kernel(x_ref, gpre_ref, w_ref, qn_ref, wuq_ref, kvn_ref, wk_ref, wv_ref,
                 rc_ref, rsa_ref, rsb_ref, eaug_ref,
                 qa_ref, cmp_ref, slck_ref, slcv_ref, win_ref, qm_ref, km_ref, vm_ref,
                 krb_ref, cq_ref, ckv_ref):
    hb = _rms(x_ref[...], gpre_ref[...]).astype(BF16)

    chunks = {}

    def proj(c0, c1):
        for (a, b), y in chunks.items():
            if a <= c0 and c1 <= b:
                return y[:, c0 - a:c1 - a]
        raise AssertionError((c0, c1))

    for a, b in ((C_AQ, C_CMP), (C_CMP, C_WIN), (C_WIN, C_KRB), (C_KRB, C_Z)):
        chunks[(a, b)] = _dot(hb, w_ref[:, a:b])

    tm = hb.shape[0]
    lane = lax.broadcasted_iota(jnp.int32, (tm, LANES), 1)
    lo = lane < HEAD_DIM

    qa_ref[...] = (proj(C_AQ, C_CMP) * 0.125).astype(BF16)
    cmp_ref[...] = proj(C_CMP, C_SLC)
    slc = proj(C_SLC, C_WIN)
    slcv_ref[...] = jnp.where(lo, 1.0, slc).astype(BF16)
    slck_ref[...] = jnp.where(lo, slc, eaug_ref[...].astype(F32)).astype(BF16)
    win_ref[...] = proj(C_WIN, C_BCQ).astype(BF16)
    cq_ref[...] = (proj(C_CQ, C_CKV) * 0.125).astype(BF16)
    ckv_ref[...] = proj(C_CKV, C_Z).astype(BF16)

    rc, rsa, rsb = rc_ref[...], rsa_ref[...], rsb_ref[...]

    def rope(v):
        return v * rc + pltpu.roll(v, 16, 1) * rsa + pltpu.roll(v, 112, 1) * rsb

    cqn = _rms(proj(C_BCQ, C_BCKV), qn_ref[...]).astype(BF16)
    qm = _dot(cqn, wuq_ref[...])
    for h in range(MLA_HEADS):
        qm_ref[:, LANES * h:LANES * (h + 1)] = rope(qm[:, LANES * h:LANES * (h + 1)]).astype(BF16)

    krb = proj(C_KRB, C_CQ)
    krb_ref[...] = krb
    krr = jnp.where(lo | (lane >= HEAD_DIM + MLA_ROPE), 0.0, rope(krb))

    ckvn = _rms(proj(C_BCKV, C_KRB), kvn_ref[...]).astype(BF16)
    kn = _dot(ckvn, wk_ref[...])
    for h in range(MLA_HEADS):
        km_ref[:, LANES * h:LANES * (h + 1)] = (kn[:, LANES * h:LANES * (h + 1)] + krr).astype(BF16)
    vm = _dot(ckvn, wv_ref[...])
    for h in range(MLA_HEADS):
        vm_ref[:, LANES * h:LANES * (h + 1)] = jnp.where(lo, 1.0, vm[:, LANES * h:LANES * (h + 1)]).astype(BF16)


def _project(x2, gpre, w, qn, wuq, kvn, wk, wv, rope_c, rope_sa, rope_sb, eaug, T, tm):
    BT = x2.shape[0]
    nt = T // tm
    row = lambda i: (i, 0)
    fix = lambda i: (0, 0)
    pos = lambda i: (i % nt, 0)
    widths = [(256, BF16), (128, F32), (128, BF16), (128, BF16), (128, BF16), (512, BF16), (512, BF16),
              (512, BF16), (128, F32), (512, BF16), (128, BF16)]
    return pl.pallas_call(
        _proj_kernel,
        grid=(BT // tm,),
        in_specs=[pl.BlockSpec((tm, D_MODEL), row), pl.BlockSpec((1, D_MODEL), fix),
                  pl.BlockSpec(w.shape, fix), pl.BlockSpec(qn.shape, fix), pl.BlockSpec(wuq.shape, fix),
                  pl.BlockSpec(kvn.shape, fix), pl.BlockSpec(wk.shape, fix), pl.BlockSpec(wv.shape, fix),
                  pl.BlockSpec((tm, LANES), pos), pl.BlockSpec((tm, LANES), pos), pl.BlockSpec((tm, LANES), pos),
                  pl.BlockSpec((tm, LANES), pos)],
        out_specs=[pl.BlockSpec((tm, wd), row) for wd, _ in widths],
        out_shape=[jax.ShapeDtypeStruct((BT, wd), dt) for wd, dt in widths],
        compiler_params=_cparams(("arbitrary",)),
        name="proj",
    )(x2, gpre, w, qn, wuq, kvn, wk, wv, rope_c, rope_sa, rope_sb, eaug)


def _compress_kernel(c_ref, pet_ref, peb_ref, w1t_ref, w1b_ref, w2_ref, o_ref):
    c = c_ref[0]
    nch = c.shape[0]
    top = (c + pet_ref[...]).astype(BF16)
    bot = (c + peb_ref[...]).astype(BF16)
    out = jnp.zeros((nch, LANES), F32)
    for i in range(2):
        pre = _dot(top, w1t_ref[i]) + pltpu.roll(_dot(bot, w1b_ref[i]), nch - 1, 0)
        hid = pre * (1.0 / (1.0 + jnp.exp(-pre)))
        out = out + _dot(hid.astype(BF16), w2_ref[i])
    o_ref[0] = out.astype(BF16)


def _compress(chunks, pet, peb, w1t, w1b, w2):
    B, nch, width = chunks.shape
    full = lambda a: pl.BlockSpec(a.shape, lambda b: (0,) * a.ndim)
    return pl.pallas_call(
        _compress_kernel,
        grid=(B,),
        in_specs=[pl.BlockSpec((1, nch, width), lambda b: (b, 0, 0)),
                  full(pet), full(peb), full(w1t), full(w1b), full(w2)],
        out_specs=pl.BlockSpec((1, nch, LANES), lambda b: (b, 0, 0)),
        out_shape=jax.ShapeDtypeStruct((B, nch, LANES), BF16),
        compiler_params=_cparams(("arbitrary",)),
        name="compress",
    )(chunks, pet, peb, w1t, w1b, w2)


def _head_slabs(qblk, n_heads):
    q = qblk.astype(F32)
    lo = lax.broadcasted_iota(jnp.int32, (q.shape[0], LANES), 1) < HEAD_DIM
    out = []
    for j in range(n_heads // 2):
        slab = q[:, LANES * j:LANES * (j + 1)]
        out.append(jnp.where(lo, slab, 0.0))
        out.append(jnp.where(lo, pltpu.roll(slab, HEAD_DIM, 1), 0.0))
    return out


def _merge_upper(accs):
    lo = lax.broadcasted_iota(jnp.int32, accs[0].shape, 1) < HEAD_DIM
    slabs = [jnp.where(lo, pltpu.roll(accs[2 * j], HEAD_DIM, 1), accs[2 * j + 1]) for j in range(len(accs) // 2)]
    return jnp.concatenate(slabs, axis=1)


def _merge_normalized(accs):
    lo = lax.broadcasted_iota(jnp.int32, accs[0].shape, 1) < HEAD_DIM
    slabs = []
    for j in range(len(accs) // 2):
        a, b = accs[2 * j], accs[2 * j + 1]
        out = jnp.where(lo, pltpu.roll(a, HEAD_DIM, 1), b)
        den = jnp.where(lo, a, pltpu.roll(b, HEAD_DIM, 1))
        slabs.append(out / den)
    return jnp.concatenate(slabs, axis=1)


def _row_max_lanes(s):
    return jnp.broadcast_to(jnp.max(s, axis=-1, keepdims=True), (s.shape[0], LANES))


def _sub_lanes(s, ref):
    return jnp.concatenate([s[:, LANES * i:LANES * (i + 1)] - ref for i in range(s.shape[1] // LANES)], axis=1)


def _topk_mask_t(imp, t0):
    ns, tq = imp.shape
    srow = lax.broadcasted_iota(jnp.int32, (ns, tq), 0)
    cur = (t0 + lax.broadcasted_iota(jnp.int32, (ns, tq), 1)) // SEL_BLOCK
    forced = (srow == 0) | (srow == cur) | (srow == cur - 1)
    x = jnp.where(forced, BIG, jnp.where(srow <= cur, imp, -BIG))
    sub = 8
    groups = [x[sub * v:sub * (v + 1)] for v in range(ns // sub)]
    rows_in = [srow[sub * v:sub * (v + 1)] for v in range(ns // sub)]
    cnts = [jnp.zeros((sub, tq), F32) for _ in groups]
    for sp in range(ns):
        other = jnp.broadcast_to(x[sp:sp + 1, :], (sub, tq))
        for v, xv in enumerate(groups):
            if sub * v > sp:
                beats = other >= xv
            elif sub * v + sub - 1 < sp:
                beats = other > xv
            else:
                beats = (other > xv) | ((other == xv) & (rows_in[v] > sp))
            cnts[v] = cnts[v] + jnp.where(beats, 1.0, 0.0)
    cnt = jnp.concatenate(cnts, axis=0)
    sel = (cnt < float(min(SEL_TOP, ns))) & (srow <= cur)
    return jnp.where(sel, 0.0, NEG)


def _cmp_kernel(q_ref, kvc_ref, bc_ref, ovt_ref, ocmp_ref, selb_ref):
    tq = Q_TILE
    kvc = kvc_ref[0]
    ovt = ovt_ref[...]
    ns = ovt.shape[0]
    groups = range(CMP_GROUP)
    sls = [slice(g * tq, (g + 1) * tq) for g in groups]
    s_alls = []
    for g in groups:
        qs = _head_slabs(q_ref[0, sls[g], :], NSA_HEADS)
        s_alls.append(_dot_nt(jnp.concatenate([q.astype(BF16) for q in qs], axis=0), kvc))
    psums, o_alls = [], []
    for g in groups:
        valid = bc_ref[0, sls[g], :] > 0.5 * NEG
        psum = jnp.zeros(valid.shape, F32)
        ps = []
        for h in range(NSA_HEADS):
            s = s_alls[g][h * tq:(h + 1) * tq] + bc_ref[h, sls[g], :]
            m = jnp.max(s, axis=-1, keepdims=True)
            e = jnp.where(valid, jnp.exp(s - m), 0.0)
            l = jnp.sum(e, axis=-1, keepdims=True)
            p = e * (1.0 / jnp.where(l > 0.0, l, 1.0))
            psum = psum + p
            ps.append(p.astype(BF16))
        psums.append(psum)
        o_alls.append(_dot(jnp.concatenate(ps, axis=0), kvc))
    imps = []
    for g in groups:
        hi = psums[g].astype(BF16)
        lo = (psums[g] - hi.astype(F32)).astype(BF16)
        imps.append(_dot_nt(ovt, hi) + _dot_nt(ovt, lo))
    for g in groups:
        ocmp = _merge_upper([o_alls[g][h * tq:(h + 1) * tq] for h in range(NSA_HEADS)])
        ocmp_ref[0, sls[g], :] = ocmp.astype(ocmp_ref.dtype)
        sb = _topk_mask_t(imps[g], (pl.program_id(1) * CMP_GROUP + g) * tq)
        if ns < HEAD_DIM:
            sb = jnp.concatenate([sb, jnp.full((HEAD_DIM - ns, tq), NEG, F32)], axis=0)
        full = jnp.concatenate([jnp.zeros((HEAD_DIM, tq), F32), sb], axis=0)
        selb_ref[0, sls[g], :] = full.T.astype(BF16)


def _cmp_attention(qa, kvc, bc, ovt, T):
    B = qa.shape[0]
    ncp = kvc.shape[1]
    rows = CMP_GROUP * Q_TILE
    return pl.pallas_call(
        _cmp_kernel,
        grid=(B, T // rows),
        in_specs=[pl.BlockSpec((1, rows, 256), lambda b, n: (b, n, 0)),
                  pl.BlockSpec((1, ncp, LANES), lambda b, n: (b, 0, 0)),
                  pl.BlockSpec((NSA_HEADS, rows, ncp), lambda b, n: (0, n, 0)),
                  pl.BlockSpec(ovt.shape, lambda b, n: (0, 0))],
        out_specs=[pl.BlockSpec((1, rows, 256), lambda b, n: (b, n, 0)),
                   pl.BlockSpec((1, rows, LANES), lambda b, n: (b, n, 0))],
        out_shape=[jax.ShapeDtypeStruct((B, T, 256), BF16), jax.ShapeDtypeStruct((B, T, LANES), BF16)],
        compiler_params=_cparams(("arbitrary", "arbitrary")),
        name="cmp_select",
    )(qa, kvc, bc, ovt)


def _softmax_tile(s2, rows, m_s, shift=None):
    m_t = _row_max_lanes(s2)
    if shift is not None:
        m_t = m_t + shift
    m_old = m_s[rows]
    m_new = jnp.maximum(m_old, m_t)
    p = jnp.exp2(_sub_lanes(s2, m_new if shift is None else m_new - shift))
    alpha = jnp.exp2(m_old - m_new)
    m_s[rows] = m_new
    return p.astype(BF16), alpha


def _slc_kernel(b31_ref, q_ref, selb_ref, ka_ref, kv_ref, tab_ref, o_ref, m_s, acc_s):
    tq = q_ref.shape[1]
    H = NSA_HEADS
    n = pl.program_id(1)
    qs = _head_slabs(q_ref[0], H)
    selb = selb_ref[0].astype(F32)
    qst = jnp.concatenate([(qs[h] + selb).astype(BF16) for h in range(H)], axis=0)
    m_s[...] = jnp.full(m_s.shape, -jnp.inf, F32)
    acc_s[...] = jnp.zeros(acc_s.shape, F32)

    def step(k0, near_block):
        ka = ka_ref[0, pl.ds(k0, tq), :]
        kv = kv_ref[0, pl.ds(k0, tq), :]
        s_all = _dot_nt(qst, ka)
        ps, alphas = [], []
        for h in range(H):
            rows = slice(h * tq, (h + 1) * tq)
            s2 = s_all[rows] * LOG2E
            if near_block is None:
                p, alpha = _softmax_tile(s2, rows, m_s, shift=b31_ref[h])
            else:
                s2 = s2 + tab_ref[h, near_block * tq:(near_block + 1) * tq, :]
                p, alpha = _softmax_tile(s2, rows, m_s)
            ps.append(p)
            alphas.append(alpha)
        pv = _dot(jnp.concatenate(ps, axis=0), kv)
        for h in range(H):
            rows = slice(h * tq, (h + 1) * tq)
            acc_s[rows] = alphas[h] * acc_s[rows] + pv[rows]

    def far(j, carry):
        step(pl.multiple_of(j * tq, tq), None)
        return carry

    lax.fori_loop(0, jnp.maximum(n - 1, 0), far, 0)

    @pl.when(n >= 1)
    def _():
        step(pl.multiple_of(jnp.maximum(n - 1, 0) * tq, tq), 1)

    step(pl.multiple_of(n * tq, tq), 0)
    o_ref[0] = _merge_normalized([acc_s[h * tq:(h + 1) * tq] for h in range(H)]).astype(o_ref.dtype)


def _slc_attention(b31, qa, selb, slck, slcv, tab, T):
    B = qa.shape[0]
    tq = FLASH_TILE
    H = NSA_HEADS
    tile = lambda w: pl.BlockSpec((1, tq, w), lambda b, n: (b, n, 0))
    return pl.pallas_call(
        _slc_kernel,
        grid=(B, T // tq),
        in_specs=[pl.BlockSpec(memory_space=pltpu.SMEM),
                  tile(256), tile(LANES),
                  pl.BlockSpec((1, T, LANES), lambda b, n: (b, 0, 0)),
                  pl.BlockSpec((1, T, LANES), lambda b, n: (b, 0, 0)),
                  pl.BlockSpec(tab.shape, lambda b, n: (0, 0, 0))],
        out_specs=tile(256),
        out_shape=jax.ShapeDtypeStruct((B, T, 256), BF16),
        scratch_shapes=[pltpu.VMEM((H * tq, LANES), F32), pltpu.VMEM((H * tq, LANES), F32)],
        compiler_params=_cparams(("arbitrary", "arbitrary")),
        name="slc_attention",
    )(b31, qa, selb, slck, slcv, tab)


def _band_kernel(n_heads, window, has_sink, n_sub, *refs):
    if has_sink:
        q_ref, kv_ref, tab_ref, sink_ref, o_ref = refs
    else:
        q_ref, kv_ref, tab_ref, o_ref = refs
    tq = Q_TILE
    span = window + tq
    lo_kv = lax.broadcasted_iota(jnp.int32, (span, LANES), 1) < HEAD_DIM
    lo_o = lax.broadcasted_iota(jnp.int32, (n_heads * tq, LANES), 1) < HEAD_DIM
    groups = range(n_sub)
    tiles = [pl.program_id(1) * n_sub + g for g in groups]
    kvs, dots = [], []
    for g in groups:
        q0 = pl.multiple_of(tiles[g] * tq, tq)
        kvs.append(kv_ref[0, pl.ds(q0, span), :])
        qs = _head_slabs(q_ref[0, g * tq:(g + 1) * tq, :], n_heads)
        dots.append(_dot_nt(jnp.concatenate([q.astype(BF16) for q in qs], axis=0), kvs[g]))
    accs, ms = [], []
    for g in groups:
        s = dots[g] + tab_ref[jnp.minimum(tiles[g], window // tq)]
        m = _row_max_lanes(s)
        if has_sink:
            m = jnp.maximum(m, sink_ref[...])
        e = jnp.exp(_sub_lanes(s, m))
        ms.append(m)
        accs.append(_dot(e.astype(BF16), jnp.where(lo_kv, 1.0, kvs[g]).astype(BF16)))
    for g in groups:
        acc = accs[g]
        if has_sink:
            acc = acc + jnp.where(lo_o, jnp.exp(sink_ref[...] - ms[g]), 0.0)
        out = _merge_normalized([acc[h * tq:(h + 1) * tq] for h in range(n_heads)])
        o_ref[0, g * tq:(g + 1) * tq, :] = out.astype(o_ref.dtype)


def _band_attention(q, kv_pad, tab, sinks, n_heads, window, T, out_dtype):
    B = q.shape[0]
    width = n_heads * HEAD_DIM
    has_sink = sinks is not None
    nvar = tab.shape[1]
    tab2 = jnp.transpose(tab, (1, 0, 2, 3)).reshape(nvar, n_heads * Q_TILE, window + Q_TILE)
    n_sub = BAND_ROWS // (n_heads * Q_TILE)
    rows = n_sub * Q_TILE
    in_specs = [pl.BlockSpec((1, rows, width), lambda b, n: (b, n, 0)),
                pl.BlockSpec((1, T + window, LANES), lambda b, n: (b, 0, 0)),
                pl.BlockSpec(tab2.shape, lambda b, n: (0, 0, 0))]
    args = (q, kv_pad, tab2)
    if has_sink:
        rep = jnp.broadcast_to(jnp.repeat(sinks, Q_TILE)[:, None], (n_heads * Q_TILE, LANES))
        in_specs.append(pl.BlockSpec(rep.shape, lambda b, n: (0, 0)))
        args = args + (rep,)
    return pl.pallas_call(
        functools.partial(_band_kernel, n_heads, window, has_sink, n_sub),
        grid=(B, T // rows),
        in_specs=in_specs,
        out_specs=pl.BlockSpec((1, rows, width), lambda b, n: (b, n, 0)),
        out_shape=jax.ShapeDtypeStruct((B, T, width), out_dtype),
        compiler_params=_cparams(("arbitrary", "arbitrary")),
        name="band_sink" if has_sink else "band_window",
    )(*args)


def _mla_kernel(q_ref, k_ref, v_ref, o_ref, m_s, acc_s):
    tq = q_ref.shape[1]
    H = MLA_HEADS
    n = pl.program_id(1)
    c2 = float((MLA_NOPE + MLA_ROPE) ** -0.5) * LOG2E
    m_s[...] = jnp.full(m_s.shape, -jnp.inf, F32)
    acc_s[...] = jnp.zeros(acc_s.shape, F32)

    def step(k0, diag):
        ss = [_dot_nt(q_ref[0, :, LANES * h:LANES * (h + 1)], k_ref[0, pl.ds(k0, tq), LANES * h:LANES * (h + 1)])
              for h in range(H)]
        ps, alphas = [], []
        for h in range(H):
            rows = slice(h * tq, (h + 1) * tq)
            s2 = ss[h] * c2
            if diag:
                row = lax.broadcasted_iota(jnp.int32, (tq, tq), 0)
                col = lax.broadcasted_iota(jnp.int32, (tq, tq), 1)
                s2 = jnp.where(col <= row, s2, NEG)
            p, alpha = _softmax_tile(s2, rows, m_s)
            ps.append(p)
            alphas.append(alpha)
        for h in range(H):
            rows = slice(h * tq, (h + 1) * tq)
            v = v_ref[0, pl.ds(k0, tq), LANES * h:LANES * (h + 1)]
            acc_s[rows] = alphas[h] * acc_s[rows] + _dot(ps[h], v)

    def far(j, carry):
        step(pl.multiple_of(j * tq, tq), False)
        return carry

    lax.fori_loop(0, n, far, 0)
    step(pl.multiple_of(n * tq, tq), True)
    o_ref[0] = _merge_normalized([acc_s[h * tq:(h + 1) * tq] for h in range(H)]).astype(o_ref.dtype)


def _mla_attention(qm, km, vm, T):
    B = qm.shape[0]
    tq = FLASH_TILE
    H = MLA_HEADS
    return pl.pallas_call(
        _mla_kernel,
        grid=(B, T // tq),
        in_specs=[pl.BlockSpec((1, tq, 512), lambda b, n: (b, n, 0)),
                  pl.BlockSpec((1, T, 512), lambda b, n: (b, 0, 0)),
                  pl.BlockSpec((1, T, 512), lambda b, n: (b, 0, 0))],
        out_specs=pl.BlockSpec((1, tq, 256), lambda b, n: (b, n, 0)),
        out_shape=jax.ShapeDtypeStruct((B, T, 256), BF16),
        scratch_shapes=[pltpu.VMEM((H * tq, LANES), F32), pltpu.VMEM((H * tq, LANES), F32)],
        compiler_params=_cparams(("arbitrary", "arbitrary")),
        name="mla_attention",
    )(qm, km, vm)


def _out_kernel(x_ref, gpre_ref, wz_ref, ocmp_ref, oslc_ref, owin_ref, krb_ref, gexp_ref, ob_ref, oc_ref, w_ref,
                gpost_ref, o_ref):
    z = _dot(_rms(x_ref[...], gpre_ref[...]).astype(BF16), wz_ref[...])
    sig = 1.0 / (1.0 + jnp.exp(-krb_ref[...]))
    hi = sig.astype(BF16)
    lo = (sig - hi.astype(F32)).astype(BF16)
    gates = _dot(hi, gexp_ref[...]) + _dot(lo, gexp_ref[...])
    w_a = NSA_HEADS * HEAD_DIM
    oa = None
    for j, br in enumerate((ocmp_ref, oslc_ref, owin_ref)):
        term = gates[:, w_a * j:w_a * (j + 1)] * br[...].astype(F32)
        oa = term if oa is None else oa + term
    heads = jnp.concatenate([oa, ob_ref[...].astype(F32), oc_ref[...].astype(F32)], axis=1)
    mixed = heads * (z * (1.0 / (1.0 + jnp.exp(-z))))
    y = _dot(mixed.astype(BF16), w_ref[...])
    o_ref[...] = x_ref[...] + _rms(y, gpost_ref[...])


def _out_project(x2, gpre, wz, ocmp, oslc, owin, krb, gexp, ob, oc, w, gpost, tm):
    BT = x2.shape[0]
    row = lambda i: (i, 0)
    fix = lambda i: (0, 0)
    spec = lambda a: pl.BlockSpec((tm, a.shape[1]), row)
    full = lambda a: pl.BlockSpec(a.shape, fix)
    return pl.pallas_call(
        _out_kernel,
        grid=(BT // tm,),
        in_specs=[spec(x2), full(gpre), full(wz), spec(ocmp), spec(oslc), spec(owin), spec(krb), full(gexp),
                  spec(ob), spec(oc), full(w), full(gpost)],
        out_specs=pl.BlockSpec((tm, D_MODEL), row),
        out_shape=jax.ShapeDtypeStruct((BT, D_MODEL), F32),
        compiler_params=_cparams(("arbitrary",)),
        name="out_proj",
    )(x2, gpre, wz, ocmp, oslc, owin, krb, gexp, ob, oc, w, gpost)


def _split_in(w):
    outs, o = [], 0
    for s in IN_SIZES:
        outs.append(w[..., o:o + s])
        o += s
    return outs


def _prep_w_in(w):
    a_q, a_kv, a_g, b_cq, b_ckv, b_kr, c_q, c_kv, z = _split_in(w.astype(BF16))
    zeros = lambda n: jnp.zeros(w.shape[:-1] + (n,), BF16)
    krb = jnp.concatenate([zeros(HEAD_DIM), b_kr, a_g, zeros(LANES - HEAD_DIM - MLA_ROPE - 3 * NSA_HEADS)], axis=-1)
    return jnp.concatenate([a_q, a_kv, b_cq, b_ckv, krb, c_q, c_kv], axis=-1), z


def _prep_mla(w_uq, w_ukv):
    L = w_uq.shape[0]
    dq = MLA_NOPE + MLA_ROPE
    uq = w_uq.reshape(L, MLA_Q_RANK, MLA_HEADS, dq)
    uq = jnp.concatenate([uq, jnp.zeros((L, MLA_Q_RANK, MLA_HEADS, LANES - dq), uq.dtype)], axis=-1)
    ukv = w_ukv.reshape(L, MLA_KV_RANK, MLA_HEADS, MLA_NOPE + MLA_V)
    uk = jnp.concatenate([ukv[..., :MLA_NOPE], jnp.zeros((L, MLA_KV_RANK, MLA_HEADS, LANES - MLA_NOPE), ukv.dtype)], axis=-1)
    uv = jnp.concatenate([jnp.zeros((L, MLA_KV_RANK, MLA_HEADS, LANES - MLA_V), ukv.dtype), ukv[..., MLA_NOPE:]], axis=-1)
    return (uq.reshape(L, MLA_Q_RANK, MLA_HEADS * LANES).astype(BF16),
            uk.reshape(L, MLA_KV_RANK, MLA_HEADS * LANES).astype(BF16),
            uv.reshape(L, MLA_KV_RANK, MLA_HEADS * LANES).astype(BF16))


def _prep_compress(cmp_pos, cmp_w1, cmp_w2):
    L = cmp_pos.shape[0]
    pe = jnp.concatenate([cmp_pos[:, 0], cmp_pos[:, 1]], axis=-1)
    pet = pe[:, :CMP_STRIDE].reshape(L, 1, CMP_STRIDE * LANES)
    peb = pe[:, CMP_STRIDE:].reshape(L, 1, CMP_STRIDE * LANES)
    w1 = cmp_w1.reshape(L, 2, CMP_LEN, HEAD_DIM, CMP_HIDDEN)
    zero = jnp.zeros_like(w1[:, 0])
    w1k = jnp.concatenate([w1[:, 0], zero], axis=2)
    w1v = jnp.concatenate([zero, w1[:, 1]], axis=2)
    w1e = jnp.stack([w1k, w1v], axis=1)
    w1t = w1e[:, :, :CMP_STRIDE].reshape(L, 2, CMP_STRIDE * LANES, CMP_HIDDEN).astype(BF16)
    w1b = w1e[:, :, CMP_STRIDE:].reshape(L, 2, CMP_STRIDE * LANES, CMP_HIDDEN).astype(BF16)
    z2 = jnp.zeros_like(cmp_w2[:, 0])
    w2 = jnp.stack([jnp.concatenate([cmp_w2[:, 0], z2], axis=-1),
                    jnp.concatenate([z2, cmp_w2[:, 1]], axis=-1)], axis=1).astype(BF16)
    return pet, peb, w1t, w1b, w2


def _rope_tables(T):
    half = MLA_ROPE // 2
    inv = ROPE_THETA ** (-jnp.arange(half, dtype=F32) / half)
    ang = jnp.arange(T).astype(F32)[:, None] * inv[None, :]
    cos, sin = jnp.cos(ang), jnp.sin(ang)
    z = lambda n: jnp.zeros((T, n), F32)
    rc = jnp.concatenate([jnp.ones((T, MLA_NOPE), F32), cos, cos, z(LANES - MLA_NOPE - MLA_ROPE)], axis=1)
    rsa = jnp.concatenate([z(MLA_NOPE + half), sin, z(LANES - MLA_NOPE - MLA_ROPE)], axis=1)
    rsb = jnp.concatenate([z(MLA_NOPE), -sin, z(LANES - MLA_NOPE - half)], axis=1)
    return rc, rsa, rsb


def _static_tables(T):
    ns = T // SEL_BLOCK
    nc = T // CMP_STRIDE - 1
    ncp = T // CMP_STRIDE
    sstart = np.arange(ns) * SEL_BLOCK
    cstart = np.arange(nc) * CMP_STRIDE
    overlap = (np.clip(np.minimum(cstart[:, None] + CMP_LEN, sstart[None, :] + SEL_BLOCK)
                       - np.maximum(cstart[:, None], sstart[None, :]), 0, None) / CMP_STRIDE).astype(np.float32)
    ovt = np.zeros((ns, ncp), np.float32)
    ovt[:, :nc] = overlap.T
    eaug = np.zeros((T, LANES), np.float32)
    eaug[np.arange(T), HEAD_DIM + np.arange(T) // SEL_BLOCK] = 1.0
    gexp = np.zeros((LANES, 3 * NSA_HEADS * HEAD_DIM), np.float32)
    for h in range(NSA_HEADS):
        for j in range(3):
            c0 = NSA_HEADS * HEAD_DIM * j + HEAD_DIM * h
            gexp[GATE_LANE0 + 3 * h + j, c0:c0 + HEAD_DIM] = 1.0
    return jnp.asarray(ovt, BF16), jnp.asarray(eaug, BF16), jnp.asarray(gexp, BF16)


def kernel(x, w_in, w_out, norm_pre, norm_post, cmp_pos, cmp_w1, cmp_w2, mla_q_norm, mla_w_uq, mla_kv_norm,
           mla_w_ukv, swa_sinks, rel_bias):
    B, T, D = x.shape
    depth = w_in.shape[0]
    assert D == D_MODEL and T % (2 * FLASH_TILE) == 0 and T // SEL_BLOCK <= HEAD_DIM
    ncp = T // CMP_STRIDE
    tm = 512

    w_in_p, w_z = _prep_w_in(w_in)
    wuq, wk, wv = _prep_mla(mla_w_uq, mla_w_ukv)
    pet, peb, w1t, w1b, w2 = _prep_compress(cmp_pos, cmp_w1, cmp_w2)
    w_out_b = w_out.astype(BF16)
    rope_c, rope_sa, rope_sb = _rope_tables(T)
    ovt, eaug, gexp = _static_tables(T)

    bc = _bias_table(rel_bias, 0, NSA_HEADS, T, ncp, -CMP_STRIDE, -(CMP_LEN - 1), 0, 1 << 30, 256).reshape(NSA_HEADS, T, ncp)
    tab_near = _bias_table(rel_bias, 0, NSA_HEADS, FLASH_TILE, FLASH_TILE, -1, 0, 0, 1 << 30, 256, mult=LOG2E,
                           nblk=2, blk_d0=FLASH_TILE).reshape(NSA_HEADS, 2 * FLASH_TILE, FLASH_TILE)
    tab_win = _bias_table(rel_bias, 0, NSA_HEADS, Q_TILE, NSA_WINDOW + Q_TILE, -1, NSA_WINDOW, 0, NSA_WINDOW, Q_TILE,
                          nblk=NSA_WINDOW // Q_TILE + 1, cmin0=NSA_WINDOW, cmin_step=Q_TILE)
    tab_swa = _bias_table(rel_bias, NSA_HEADS, SWA_HEADS, Q_TILE, SWA_WINDOW + Q_TILE, -1, SWA_WINDOW, 0, SWA_WINDOW, Q_TILE,
                          nblk=SWA_WINDOW // Q_TILE + 1, cmin0=SWA_WINDOW, cmin_step=Q_TILE)
    b31 = rel_bias[:NSA_HEADS, REL_BUCKETS - 1] * LOG2E

    x2 = x.reshape(B * T, D)
    for l in range(depth):
        (qa, cmpkv, slck, slcv, win, qm, km, vm, krb, cq, ckv) = _project(
            x2, norm_pre[l][None], w_in_p[l], mla_q_norm[l][None], wuq[l], mla_kv_norm[l][None], wk[l], wv[l],
            rope_c, rope_sa, rope_sb, eaug, T, tm)
        r3 = lambda a: a.reshape(B, T, a.shape[-1])
        kvc = _compress(cmpkv.reshape(B, ncp, CMP_STRIDE * LANES), pet[l], peb[l], w1t[l], w1b[l], w2[l])
        qa3 = r3(qa)
        ocmp, selb = _cmp_attention(qa3, kvc, bc, ovt, T)
        win_pad = jnp.pad(r3(win), ((0, 0), (NSA_WINDOW, 0), (0, 0)))
        owin = _band_attention(qa3, win_pad, tab_win, None, NSA_HEADS, NSA_WINDOW, T, BF16)
        oslc = _slc_attention(b31, qa3, selb, r3(slck), r3(slcv), tab_near, T)
        ckv_pad = jnp.pad(r3(ckv), ((0, 0), (SWA_WINDOW, 0), (0, 0)))
        oc = _band_attention(r3(cq), ckv_pad, tab_swa, swa_sinks[l], SWA_HEADS, SWA_WINDOW, T, BF16)
        ob = _mla_attention(r3(qm), r3(km), r3(vm), T)
        flat = lambda a: a.reshape(B * T, a.shape[-1])
        x2 = _out_project(x2, norm_pre[l][None], w_z[l], flat(ocmp), flat(oslc), flat(owin), krb, gexp,
                          flat(ob), flat(oc), w_out_b[l], norm_post[l][None], tm)
    return x2.reshape(B, T, D)
```

```python
import functools
import math

import numpy as np
import jax
import jax.numpy as jnp
from jax import lax
from jax.experimental import pallas as pl
from jax.experimental.pallas import tpu as pltpu

F32 = jnp.float32
BF16 = jnp.bfloat16

D_MODEL = 1024
HEAD_DIM = 64
NSA_HEADS = 4
CMP_LEN = 32
CMP_STRIDE = 16
CMP_HIDDEN = 128
SEL_BLOCK = 64
SEL_TOP = 16
NSA_WINDOW = 512
MLA_HEADS = 4
MLA_Q_RANK = 256
MLA_KV_RANK = 128
MLA_NOPE = 64
MLA_ROPE = 32
MLA_V = 64
ROPE_THETA = 10000.0
SWA_HEADS = 8
SWA_WINDOW = 128
REL_BUCKETS = 32
REL_MAX_DIST = 512
NORM_EPS = 1e-6
NEG = -1e30
BIG = 1e9
LOG2E = math.log2(math.e)
MLA_SCALE2 = (MLA_NOPE + MLA_ROPE) ** -0.5 * LOG2E
IN_SIZES = (256, 384, 12, 256, 128, 32, 512, 128, 1024)

LANES = 128
Q_TILE = 128
BAND_ROWS = 4096
CMP_GROUP = 4
FLASH_TILE = 512
GATE_LANE0 = 96
VMEM_LIMIT = 56 * 1024 * 1024

C_AQ, C_CMP, C_SLC, C_WIN, C_BCQ, C_BCKV, C_KRB, C_CQ, C_CKV, C_Z = (
    0, 256, 384, 512, 640, 896, 1024, 1152, 1664, 1792)


def _dot(a, b):
    return jnp.dot(a, b, preferred_element_type=F32)


def _dot_nt(a, b):
    return lax.dot_general(a, b, (((1,), (1,)), ((), ())), preferred_element_type=F32)


def _bucket_thresholds():
    d = np.arange(0, 4 * REL_MAX_DIST)
    exact = REL_BUCKETS // 2
    large = exact + (np.log(np.maximum(d, 1).astype(np.float32) / np.float32(exact))
                     / np.float32(math.log(REL_MAX_DIST / exact)) * np.float32(REL_BUCKETS - exact)).astype(np.int32)
    b = np.where(d < exact, d, np.minimum(large, REL_BUCKETS - 1))
    assert np.all(np.diff(b) >= 0) and b[-1] == REL_BUCKETS - 1
    return [int(np.argmax(b >= k)) for k in range(REL_BUCKETS)]


_THRESH = _bucket_thresholds()


def _cparams(sem):
    return pltpu.CompilerParams(dimension_semantics=sem, vmem_limit_bytes=VMEM_LIMIT)


def _table_kernel(h0, a_col, d0, blk_d0, lo, hi, cmin0, cmin_step, mult, rb_ref, out_ref):
    h = pl.program_id(0) + h0
    blk = pl.program_id(1)
    rt, cc = out_ref.shape[2], out_ref.shape[3]
    r = pl.program_id(2) * rt + lax.broadcasted_iota(jnp.int32, (rt, cc), 0)
    c = lax.broadcasted_iota(jnp.int32, (rt, cc), 1)
    dist = r + a_col * c + d0 + blk * blk_d0
    acc = jnp.full((rt, cc), rb_ref[h, 0], F32)
    for k in range(1, REL_BUCKETS):
        acc = jnp.where(dist >= _THRESH[k], rb_ref[h, k], acc)
    ok = (dist >= lo) & (dist < hi) & (c >= jnp.maximum(cmin0 - blk * cmin_step, 0))
    out_ref[0, 0] = jnp.where(ok, acc * mult, NEG)


def _bias_table(rel_bias, h0, nh, rows, cols, a_col, d0, lo, hi, rt, mult=1.0, nblk=1, blk_d0=0, cmin0=0, cmin_step=0):
    return pl.pallas_call(
        functools.partial(_table_kernel, h0, a_col, d0, blk_d0, lo, hi, cmin0, cmin_step, mult),
        grid=(nh, nblk, rows // rt),
        in_specs=[pl.BlockSpec(memory_space=pltpu.SMEM)],
        out_specs=pl.BlockSpec((1, 1, rt, cols), lambda h, b, r: (h, b, r, 0)),
        out_shape=jax.ShapeDtypeStruct((nh, nblk, rows, cols), F32),
        compiler_params=_cparams(("arbitrary", "arbitrary", "arbitrary")),
        name="bias_table",
    )(rel_bias)


def _rms(v, g):
    return v * lax.rsqrt(jnp.mean(v * v, axis=-1, keepdims=True) + NORM_EPS) * g


def _proj_kernel(x_ref, gpre_ref, w_ref, qn_ref, wuq_ref, kvn_ref, wk_ref, wv_ref,
                 rc_ref, rsa_ref, rsb_ref, eaug_ref,
                 qa_ref, qa2_ref, cmp_ref, slck_ref, slcv_ref, win_ref, qm_ref, km_ref, vm_ref,
                 krb_ref, cq_ref, ckv_ref):
    hb = _rms(x_ref[...], gpre_ref[...]).astype(BF16)

    chunks = {}

    def proj(c0, c1):
        for (a, b), y in chunks.items():
            if a <= c0 and c1 <= b:
                return y[:, c0 - a:c1 - a]
        raise AssertionError((c0, c1))

    for a, b in ((C_AQ, C_CMP), (C_CMP, C_WIN), (C_WIN, C_KRB), (C_KRB, C_Z)):
        chunks[(a, b)] = _dot(hb, w_ref[:, a:b])

    tm = hb.shape[0]
    lane = lax.broadcasted_iota(jnp.int32, (tm, LANES), 1)
    lo = lane < HEAD_DIM

    qa_ref[...] = (proj(C_AQ, C_CMP) * 0.125).astype(BF16)
    qa2_ref[...] = (proj(C_AQ, C_CMP) * (0.125 * LOG2E)).astype(BF16)
    cmp_ref[...] = proj(C_CMP, C_SLC)
    slc = proj(C_SLC, C_WIN)
    slcv_ref[...] = jnp.where(lo, 1.0, slc).astype(BF16)
    slck_ref[...] = jnp.where(lo, slc, eaug_ref[...].astype(F32)).astype(BF16)
    win_ref[...] = proj(C_WIN, C_BCQ).astype(BF16)
    cq_ref[...] = (proj(C_CQ, C_CKV) * 0.125).astype(BF16)
    ckv_ref[...] = proj(C_CKV, C_Z).astype(BF16)

    rc, rsa, rsb = rc_ref[...], rsa_ref[...], rsb_ref[...]

    def rope(v):
        return v * rc + pltpu.roll(v, 16, 1) * rsa + pltpu.roll(v, 112, 1) * rsb

    cqn = _rms(proj(C_BCQ, C_BCKV), qn_ref[...]).astype(BF16)
    qm = _dot(cqn, wuq_ref[...])
    for h in range(MLA_HEADS):
        qh = rope(qm[:, LANES * h:LANES * (h + 1)]) * MLA_SCALE2
        qm_ref[:, LANES * h:LANES * (h + 1)] = qh.astype(BF16)

    krb = proj(C_KRB, C_CQ)
    krb_ref[...] = krb
    krr = jnp.where(lo | (lane >= HEAD_DIM + MLA_ROPE), 0.0, rope(krb))

    ckvn = _rms(proj(C_BCKV, C_KRB), kvn_ref[...]).astype(BF16)
    kn = _dot(ckvn, wk_ref[...])
    for h in range(MLA_HEADS):
        km_ref[:, LANES * h:LANES * (h + 1)] = (kn[:, LANES * h:LANES * (h + 1)] + krr).astype(BF16)
    vm = _dot(ckvn, wv_ref[...])
    for h in range(MLA_HEADS):
        vm_ref[:, LANES * h:LANES * (h + 1)] = jnp.where(lo, 1.0, vm[:, LANES * h:LANES * (h + 1)]).astype(BF16)


def _project(x2, gpre, w, qn, wuq, kvn, wk, wv, rope_c, rope_sa, rope_sb, eaug, T, tm):
    BT = x2.shape[0]
    nt = T // tm
    row = lambda i: (i, 0)
    fix = lambda i: (0, 0)
    pos = lambda i: (i % nt, 0)
    widths = [(256, BF16), (256, BF16), (128, F32), (128, BF16), (128, BF16), (128, BF16), (512, BF16), (512, BF16),
              (512, BF16), (128, F32), (512, BF16), (128, BF16)]
    return pl.pallas_call(
        _proj_kernel,
        grid=(BT // tm,),
        in_specs=[pl.BlockSpec((tm, D_MODEL), row), pl.BlockSpec((1, D_MODEL), fix),
                  pl.BlockSpec(w.shape, fix), pl.BlockSpec(qn.shape, fix), pl.BlockSpec(wuq.shape, fix),
                  pl.BlockSpec(kvn.shape, fix), pl.BlockSpec(wk.shape, fix), pl.BlockSpec(wv.shape, fix),
                  pl.BlockSpec((tm, LANES), pos), pl.BlockSpec((tm, LANES), pos), pl.BlockSpec((tm, LANES), pos),
                  pl.BlockSpec((tm, LANES), pos)],
        out_specs=[pl.BlockSpec((tm, wd), row) for wd, _ in widths],
        out_shape=[jax.ShapeDtypeStruct((BT, wd), dt) for wd, dt in widths],
        compiler_params=_cparams(("arbitrary",)),
        name="proj",
    )(x2, gpre, w, qn, wuq, kvn, wk, wv, rope_c, rope_sa, rope_sb, eaug)


def _compress_kernel(c_ref, pet_ref, peb_ref, w1t_ref, w1b_ref, w2_ref, o_ref):
    c = c_ref[0]
    nch = c.shape[0]
    top = (c + pet_ref[...]).astype(BF16)
    bot = (c + peb_ref[...]).astype(BF16)
    out = jnp.zeros((nch, LANES), F32)
    for i in range(2):
        pre = _dot(top, w1t_ref[i]) + pltpu.roll(_dot(bot, w1b_ref[i]), nch - 1, 0)
        hid = pre * (1.0 / (1.0 + jnp.exp(-pre)))
        out = out + _dot(hid.astype(BF16), w2_ref[i])
    o_ref[0] = out.astype(BF16)


def _compress(chunks, pet, peb, w1t, w1b, w2):
    B, nch, width = chunks.shape
    full = lambda a: pl.BlockSpec(a.shape, lambda b: (0,) * a.ndim)
    return pl.pallas_call(
        _compress_kernel,
        grid=(B,),
        in_specs=[pl.BlockSpec((1, nch, width), lambda b: (b, 0, 0)),
                  full(pet), full(peb), full(w1t), full(w1b), full(w2)],
        out_specs=pl.BlockSpec((1, nch, LANES), lambda b: (b, 0, 0)),
        out_shape=jax.ShapeDtypeStruct((B, nch, LANES), BF16),
        compiler_params=_cparams(("arbitrary",)),
        name="compress",
    )(chunks, pet, peb, w1t, w1b, w2)


def _head_slabs(qblk, n_heads):
    q = qblk.astype(F32)
    lo = lax.broadcasted_iota(jnp.int32, (q.shape[0], LANES), 1) < HEAD_DIM
    out = []
    for j in range(n_heads // 2):
        slab = q[:, LANES * j:LANES * (j + 1)]
        out.append(jnp.where(lo, slab, 0.0))
        out.append(jnp.where(lo, pltpu.roll(slab, HEAD_DIM, 1), 0.0))
    return out


def _merge_upper(accs):
    lo = lax.broadcasted_iota(jnp.int32, accs[0].shape, 1) < HEAD_DIM
    slabs = [jnp.where(lo, pltpu.roll(accs[2 * j], HEAD_DIM, 1), accs[2 * j + 1]) for j in range(len(accs) // 2)]
    return jnp.concatenate(slabs, axis=1)


def _merge_normalized(accs):
    lo = lax.broadcasted_iota(jnp.int32, accs[0].shape, 1) < HEAD_DIM
    slabs = []
    for j in range(len(accs) // 2):
        a, b = accs[2 * j], accs[2 * j + 1]
        out = jnp.where(lo, pltpu.roll(a, HEAD_DIM, 1), b)
        den = jnp.where(lo, a, pltpu.roll(b, HEAD_DIM, 1))
        slabs.append(out / den)
    return jnp.concatenate(slabs, axis=1)


def _row_max_lanes(s):
    return jnp.broadcast_to(jnp.max(s, axis=-1, keepdims=True), (s.shape[0], LANES))


def _sub_lanes(s, ref):
    return jnp.concatenate([s[:, LANES * i:LANES * (i + 1)] - ref for i in range(s.shape[1] // LANES)], axis=1)


def _topk_mask_t(imp, t0):
    ns, tq = imp.shape
    srow = lax.broadcasted_iota(jnp.int32, (ns, tq), 0)
    cur = (t0 + lax.broadcasted_iota(jnp.int32, (ns, tq), 1)) // SEL_BLOCK
    forced = (srow == 0) | (srow == cur) | (srow == cur - 1)
    x = jnp.where(forced, BIG, jnp.where(srow <= cur, imp, -BIG))
    sub = 8
    groups = [x[sub * v:sub * (v + 1)] for v in range(ns // sub)]
    rows_in = [srow[sub * v:sub * (v + 1)] for v in range(ns // sub)]
    cnts = [jnp.zeros((sub, tq), F32) for _ in groups]
    for sp in range(ns):
        other = jnp.broadcast_to(x[sp:sp + 1, :], (sub, tq))
        for v, xv in enumerate(groups):
            if sub * v > sp:
                beats = other >= xv
            elif sub * v + sub - 1 < sp:
                beats = other > xv
            else:
                beats = (other > xv) | ((other == xv) & (rows_in[v] > sp))
            cnts[v] = cnts[v] + jnp.where(beats, 1.0, 0.0)
    cnt = jnp.concatenate(cnts, axis=0)
    sel = (cnt < float(min(SEL_TOP, ns))) & (srow <= cur)
    return jnp.where(sel, 0.0, NEG)


def _cmp_kernel(q_ref, kvc_ref, bc_ref, ovt_ref, ocmp_ref, selb_ref):
    tq = Q_TILE
    kvc = kvc_ref[0]
    ovt = ovt_ref[...]
    ns = ovt.shape[0]
    groups = range(CMP_GROUP)
    sls = [slice(g * tq, (g + 1) * tq) for g in groups]
    s_alls = []
    for g in groups:
        qs = _head_slabs(q_ref[0, sls[g], :], NSA_HEADS)
        s_alls.append(_dot_nt(jnp.concatenate([q.astype(BF16) for q in qs], axis=0), kvc))
    psums, o_alls = [], []
    for g in groups:
        valid = bc_ref[0, sls[g], :] > 0.5 * NEG
        psum = jnp.zeros(valid.shape, F32)
        ps = []
        for h in range(NSA_HEADS):
            s = s_alls[g][h * tq:(h + 1) * tq] + bc_ref[h, sls[g], :]
            m = jnp.max(s, axis=-1, keepdims=True)
            e = jnp.where(valid, jnp.exp(s - m), 0.0)
            l = jnp.sum(e, axis=-1, keepdims=True)
            p = e * (1.0 / jnp.where(l > 0.0, l, 1.0))
            psum = psum + p
            ps.append(p.astype(BF16))
        psums.append(psum)
        o_alls.append(_dot(jnp.concatenate(ps, axis=0), kvc))
    imps = []
    for g in groups:
        hi = psums[g].astype(BF16)
        lo = (psums[g] - hi.astype(F32)).astype(BF16)
        imps.append(_dot_nt(ovt, hi) + _dot_nt(ovt, lo))
    for g in groups:
        ocmp = _merge_upper([o_alls[g][h * tq:(h + 1) * tq] for h in range(NSA_HEADS)])
        ocmp_ref[0, sls[g], :] = ocmp.astype(ocmp_ref.dtype)
        sb = _topk_mask_t(imps[g], (pl.program_id(1) * CMP_GROUP + g) * tq)
        if ns < HEAD_DIM:
            sb = jnp.concatenate([sb, jnp.full((HEAD_DIM - ns, tq), NEG, F32)], axis=0)
        full = jnp.concatenate([jnp.zeros((HEAD_DIM, tq), F32), sb], axis=0)
        selb_ref[0, sls[g], :] = full.T.astype(BF16)


def _cmp_attention(qa, kvc, bc, ovt, T):
    B = qa.shape[0]
    ncp = kvc.shape[1]
    rows = CMP_GROUP * Q_TILE
    return pl.pallas_call(
        _cmp_kernel,
        grid=(B, T // rows),
        in_specs=[pl.BlockSpec((1, rows, 256), lambda b, n: (b, n, 0)),
                  pl.BlockSpec((1, ncp, LANES), lambda b, n: (b, 0, 0)),
                  pl.BlockSpec((NSA_HEADS, rows, ncp), lambda b, n: (0, n, 0)),
                  pl.BlockSpec(ovt.shape, lambda b, n: (0, 0))],
        out_specs=[pl.BlockSpec((1, rows, 256), lambda b, n: (b, n, 0)),
                   pl.BlockSpec((1, rows, LANES), lambda b, n: (b, n, 0))],
        out_shape=[jax.ShapeDtypeStruct((B, T, 256), BF16), jax.ShapeDtypeStruct((B, T, LANES), BF16)],
        compiler_params=_cparams(("arbitrary", "arbitrary")),
        name="cmp_select",
    )(qa, kvc, bc, ovt)


def _softmax_tile(s2, rows, m_s, shift=None):
    m_t = _row_max_lanes(s2)
    if shift is not None:
        m_t = m_t + shift
    m_old = m_s[rows]
    m_new = jnp.maximum(m_old, m_t)
    p = jnp.exp2(_sub_lanes(s2, m_new if shift is None else m_new - shift))
    alpha = jnp.exp2(m_old - m_new)
    m_s[rows] = m_new
    return p.astype(BF16), alpha


def _slc_kernel(b31_ref, q_ref, selb_ref, ka_ref, kv_ref, tab_ref, o_ref, m_s, acc_s):
    tq = q_ref.shape[1]
    H = NSA_HEADS
    n = pl.program_id(1)
    qs = _head_slabs(q_ref[0], H)
    selb = selb_ref[0].astype(F32)
    qst = jnp.concatenate([(qs[h] + selb).astype(BF16) for h in range(H)], axis=0)
    m_s[...] = jnp.full(m_s.shape, -jnp.inf, F32)
    acc_s[...] = jnp.zeros(acc_s.shape, F32)

    def scores(k0):
        k0 = pl.multiple_of(k0, tq)
        return _dot_nt(qst, ka_ref[0, pl.ds(k0, tq), :]), kv_ref[0, pl.ds(k0, tq), :]

    def update(s_all, kv, near_block):
        ps, alphas = [], []
        for h in range(H):
            rows = slice(h * tq, (h + 1) * tq)
            s2 = s_all[rows]
            if near_block is None:
                p, alpha = _softmax_tile(s2, rows, m_s, shift=b31_ref[h])
            else:
                s2 = s2 + tab_ref[h, near_block * tq:(near_block + 1) * tq, :]
                p, alpha = _softmax_tile(s2, rows, m_s)
            ps.append(p)
            alphas.append(alpha)
        pv = _dot(jnp.concatenate(ps, axis=0), kv)
        for h in range(H):
            rows = slice(h * tq, (h + 1) * tq)
            acc_s[rows] = alphas[h] * acc_s[rows] + pv[rows]

    def far_pair(i, carry):
        first, second = scores(2 * i * tq), scores((2 * i + 1) * tq)
        update(*first, None)
        update(*second, None)
        return carry

    n_far = jnp.maximum(n - 1, 0)
    lax.fori_loop(0, n_far // 2, far_pair, 0)

    @pl.when(n_far % 2 == 1)
    def _():
        update(*scores((n_far - 1) * tq), None)

    @pl.when(n >= 1)
    def _():
        update(*scores(jnp.maximum(n - 1, 0) * tq), 1)

    update(*scores(n * tq), 0)
    o_ref[0] = _merge_normalized([acc_s[h * tq:(h + 1) * tq] for h in range(H)]).astype(o_ref.dtype)


def _slc_attention(b31, qa, selb, slck, slcv, tab, T):
    B = qa.shape[0]
    tq = FLASH_TILE
    H = NSA_HEADS
    tile = lambda w: pl.BlockSpec((1, tq, w), lambda b, n: (b, n, 0))
    return pl.pallas_call(
        _slc_kernel,
        grid=(B, T // tq),
        in_specs=[pl.BlockSpec(memory_space=pltpu.SMEM),
                  tile(256), tile(LANES),
                  pl.BlockSpec((1, T, LANES), lambda b, n: (b, 0, 0)),
                  pl.BlockSpec((1, T, LANES), lambda b, n: (b, 0, 0)),
                  pl.BlockSpec(tab.shape, lambda b, n: (0, 0, 0))],
        out_specs=tile(256),
        out_shape=jax.ShapeDtypeStruct((B, T, 256), BF16),
        scratch_shapes=[pltpu.VMEM((H * tq, LANES), F32), pltpu.VMEM((H * tq, LANES), F32)],
        compiler_params=_cparams(("arbitrary", "arbitrary")),
        name="slc_attention",
    )(b31, qa, selb, slck, slcv, tab)


def _band_kernel(n_heads, window, has_sink, n_sub, *refs):
    if has_sink:
        q_ref, kv_ref, tab_ref, sink_ref, o_ref = refs
    else:
        q_ref, kv_ref, tab_ref, o_ref = refs
    tq = Q_TILE
    span = window + tq
    lo_kv = lax.broadcasted_iota(jnp.int32, (span, LANES), 1) < HEAD_DIM
    lo_o = lax.broadcasted_iota(jnp.int32, (n_heads * tq, LANES), 1) < HEAD_DIM
    groups = range(n_sub)
    tiles = [pl.program_id(1) * n_sub + g for g in groups]
    kvs, dots = [], []
    for g in groups:
        q0 = pl.multiple_of(tiles[g] * tq, tq)
        kvs.append(kv_ref[0, pl.ds(q0, span), :])
        qs = _head_slabs(q_ref[0, g * tq:(g + 1) * tq, :], n_heads)
        dots.append(_dot_nt(jnp.concatenate([q.astype(BF16) for q in qs], axis=0), kvs[g]))
    accs, ms = [], []
    for g in groups:
        s = dots[g] + tab_ref[jnp.minimum(tiles[g], window // tq)]
        m = _row_max_lanes(s)
        if has_sink:
            m = jnp.maximum(m, sink_ref[...])
        e = jnp.exp(_sub_lanes(s, m))
        ms.append(m)
        accs.append(_dot(e.astype(BF16), jnp.where(lo_kv, 1.0, kvs[g]).astype(BF16)))
    for g in groups:
        acc = accs[g]
        if has_sink:
            acc = acc + jnp.where(lo_o, jnp.exp(sink_ref[...] - ms[g]), 0.0)
        out = _merge_normalized([acc[h * tq:(h + 1) * tq] for h in range(n_heads)])
        o_ref[0, g * tq:(g + 1) * tq, :] = out.astype(o_ref.dtype)


def _band_attention(q, kv_pad, tab, sinks, n_heads, window, T, out_dtype):
    B = q.shape[0]
    width = n_heads * HEAD_DIM
    has_sink = sinks is not None
    nvar = tab.shape[1]
    tab2 = jnp.transpose(tab, (1, 0, 2, 3)).reshape(nvar, n_heads * Q_TILE, window + Q_TILE)
    n_sub = BAND_ROWS // (n_heads * Q_TILE)
    rows = n_sub * Q_TILE
    in_specs = [pl.BlockSpec((1, rows, width), lambda b, n: (b, n, 0)),
                pl.BlockSpec((1, T + window, LANES), lambda b, n: (b, 0, 0)),
                pl.BlockSpec(tab2.shape, lambda b, n: (0, 0, 0))]
    args = (q, kv_pad, tab2)
    if has_sink:
        rep = jnp.broadcast_to(jnp.repeat(sinks, Q_TILE)[:, None], (n_heads * Q_TILE, LANES))
        in_specs.append(pl.BlockSpec(rep.shape, lambda b, n: (0, 0)))
        args = args + (rep,)
    return pl.pallas_call(
        functools.partial(_band_kernel, n_heads, window, has_sink, n_sub),
        grid=(B, T // rows),
        in_specs=in_specs,
        out_specs=pl.BlockSpec((1, rows, width), lambda b, n: (b, n, 0)),
        out_shape=jax.ShapeDtypeStruct((B, T, width), out_dtype),
        compiler_params=_cparams(("arbitrary", "arbitrary")),
        name="band_sink" if has_sink else "band_window",
    )(*args)


def _mla_kernel(q_ref, k_ref, v_ref, o_ref, m_s, acc_s):
    tq = q_ref.shape[1]
    H = MLA_HEADS
    n = pl.program_id(1)
    m_s[...] = jnp.full(m_s.shape, -jnp.inf, F32)
    acc_s[...] = jnp.zeros(acc_s.shape, F32)

    def scores(k0):
        k0 = pl.multiple_of(k0, tq)
        return k0, [_dot_nt(q_ref[0, :, LANES * h:LANES * (h + 1)], k_ref[0, pl.ds(k0, tq), LANES * h:LANES * (h + 1)])
                    for h in range(H)]

    def update(k0, ss, diag):
        ps, alphas = [], []
        for h in range(H):
            rows = slice(h * tq, (h + 1) * tq)
            s2 = ss[h]
            if diag:
                row = lax.broadcasted_iota(jnp.int32, (tq, tq), 0)
                col = lax.broadcasted_iota(jnp.int32, (tq, tq), 1)
                s2 = jnp.where(col <= row, s2, NEG)
            p, alpha = _softmax_tile(s2, rows, m_s)
            ps.append(p)
            alphas.append(alpha)
        for h in range(H):
            rows = slice(h * tq, (h + 1) * tq)
            v = v_ref[0, pl.ds(k0, tq), LANES * h:LANES * (h + 1)]
            acc_s[rows] = alphas[h] * acc_s[rows] + _dot(ps[h], v)

    def far_pair(i, carry):
        first, second = scores(2 * i * tq), scores((2 * i + 1) * tq)
        update(*first, False)
        update(*second, False)
        return carry

    lax.fori_loop(0, n // 2, far_pair, 0)

    @pl.when(n % 2 == 1)
    def _():
        update(*scores((n - 1) * tq), False)

    update(*scores(n * tq), True)
    o_ref[0] = _merge_normalized([acc_s[h * tq:(h + 1) * tq] for h in range(H)]).astype(o_ref.dtype)


def _mla_attention(qm, km, vm, T):
    B = qm.shape[0]
    tq = FLASH_TILE
    H = MLA_HEADS
    return pl.pallas_call(
        _mla_kernel,
        grid=(B, T // tq),
        in_specs=[pl.BlockSpec((1, tq, 512), lambda b, n: (b, n, 0)),
                  pl.BlockSpec((1, T, 512), lambda b, n: (b, 0, 0)),
                  pl.BlockSpec((1, T, 512), lambda b, n: (b, 0, 0))],
        out_specs=pl.BlockSpec((1, tq, 256), lambda b, n: (b, n, 0)),
        out_shape=jax.ShapeDtypeStruct((B, T, 256), BF16),
        scratch_shapes=[pltpu.VMEM((H * tq, LANES), F32), pltpu.VMEM((H * tq, LANES), F32)],
        compiler_params=_cparams(("arbitrary", "arbitrary")),
        name="mla_attention",
    )(qm, km, vm)


def _out_kernel(x_ref, gpre_ref, wz_ref, ocmp_ref, oslc_ref, owin_ref, krb_ref, gexp_ref, ob_ref, oc_ref, w_ref,
                gpost_ref, o_ref):
    z = _dot(_rms(x_ref[...], gpre_ref[...]).astype(BF16), wz_ref[...])
    sig = 1.0 / (1.0 + jnp.exp(-krb_ref[...]))
    hi = sig.astype(BF16)
    lo = (sig - hi.astype(F32)).astype(BF16)
    gates = _dot(hi, gexp_ref[...]) + _dot(lo, gexp_ref[...])
    w_a = NSA_HEADS * HEAD_DIM
    oa = None
    for j, br in enumerate((ocmp_ref, oslc_ref, owin_ref)):
        term = gates[:, w_a * j:w_a * (j + 1)] * br[...].astype(F32)
        oa = term if oa is None else oa + term
    heads = jnp.concatenate([oa, ob_ref[...].astype(F32), oc_ref[...].astype(F32)], axis=1)
    mixed = heads * (z * (1.0 / (1.0 + jnp.exp(-z))))
    y = _dot(mixed.astype(BF16), w_ref[...])
    o_ref[...] = x_ref[...] + _rms(y, gpost_ref[...])


def _out_project(x2, gpre, wz, ocmp, oslc, owin, krb, gexp, ob, oc, w, gpost, tm):
    BT = x2.shape[0]
    row = lambda i: (i, 0)
    fix = lambda i: (0, 0)
    spec = lambda a: pl.BlockSpec((tm, a.shape[1]), row)
    full = lambda a: pl.BlockSpec(a.shape, fix)
    return pl.pallas_call(
        _out_kernel,
        grid=(BT // tm,),
        in_specs=[spec(x2), full(gpre), full(wz), spec(ocmp), spec(oslc), spec(owin), spec(krb), full(gexp),
                  spec(ob), spec(oc), full(w), full(gpost)],
        out_specs=pl.BlockSpec((tm, D_MODEL), row),
        out_shape=jax.ShapeDtypeStruct((BT, D_MODEL), F32),
        compiler_params=_cparams(("arbitrary",)),
        name="out_proj",
    )(x2, gpre, wz, ocmp, oslc, owin, krb, gexp, ob, oc, w, gpost)


def _split_in(w):
    outs, o = [], 0
    for s in IN_SIZES:
        outs.append(w[..., o:o + s])
        o += s
    return outs


def _prep_w_in(w):
    a_q, a_kv, a_g, b_cq, b_ckv, b_kr, c_q, c_kv, z = _split_in(w.astype(BF16))
    zeros = lambda n: jnp.zeros(w.shape[:-1] + (n,), BF16)
    krb = jnp.concatenate([zeros(HEAD_DIM), b_kr, a_g, zeros(LANES - HEAD_DIM - MLA_ROPE - 3 * NSA_HEADS)], axis=-1)
    return jnp.concatenate([a_q, a_kv, b_cq, b_ckv, krb, c_q, c_kv], axis=-1), z


def _prep_mla(w_uq, w_ukv):
    L = w_uq.shape[0]
    dq = MLA_NOPE + MLA_ROPE
    uq = w_uq.reshape(L, MLA_Q_RANK, MLA_HEADS, dq)
    uq = jnp.concatenate([uq, jnp.zeros((L, MLA_Q_RANK, MLA_HEADS, LANES - dq), uq.dtype)], axis=-1)
    ukv = w_ukv.reshape(L, MLA_KV_RANK, MLA_HEADS, MLA_NOPE + MLA_V)
    uk = jnp.concatenate([ukv[..., :MLA_NOPE], jnp.zeros((L, MLA_KV_RANK, MLA_HEADS, LANES - MLA_NOPE), ukv.dtype)], axis=-1)
    uv = jnp.concatenate([jnp.zeros((L, MLA_KV_RANK, MLA_HEADS, LANES - MLA_V), ukv.dtype), ukv[..., MLA_NOPE:]], axis=-1)
    return (uq.reshape(L, MLA_Q_RANK, MLA_HEADS * LANES).astype(BF16),
            uk.reshape(L, MLA_KV_RANK, MLA_HEADS * LANES).astype(BF16),
            uv.reshape(L, MLA_KV_RANK, MLA_HEADS * LANES).astype(BF16))


def _prep_compress(cmp_pos, cmp_w1, cmp_w2):
    L = cmp_pos.shape[0]
    pe = jnp.concatenate([cmp_pos[:, 0], cmp_pos[:, 1]], axis=-1)
    pet = pe[:, :CMP_STRIDE].reshape(L, 1, CMP_STRIDE * LANES)
    peb = pe[:, CMP_STRIDE:].reshape(L, 1, CMP_STRIDE * LANES)
    w1 = cmp_w1.reshape(L, 2, CMP_LEN, HEAD_DIM, CMP_HIDDEN)
    zero = jnp.zeros_like(w1[:, 0])
    w1k = jnp.concatenate([w1[:, 0], zero], axis=2)
    w1v = jnp.concatenate([zero, w1[:, 1]], axis=2)
    w1e = jnp.stack([w1k, w1v], axis=1)
    w1t = w1e[:, :, :CMP_STRIDE].reshape(L, 2, CMP_STRIDE * LANES, CMP_HIDDEN).astype(BF16)
    w1b = w1e[:, :, CMP_STRIDE:].reshape(L, 2, CMP_STRIDE * LANES, CMP_HIDDEN).astype(BF16)
    z2 = jnp.zeros_like(cmp_w2[:, 0])
    w2 = jnp.stack([jnp.concatenate([cmp_w2[:, 0], z2], axis=-1),
                    jnp.concatenate([z2, cmp_w2[:, 1]], axis=-1)], axis=1).astype(BF16)
    return pet, peb, w1t, w1b, w2


def _rope_tables(T):
    half = MLA_ROPE // 2
    inv = ROPE_THETA ** (-jnp.arange(half, dtype=F32) / half)
    ang = jnp.arange(T).astype(F32)[:, None] * inv[None, :]
    cos, sin = jnp.cos(ang), jnp.sin(ang)
    z = lambda n: jnp.zeros((T, n), F32)
    rc = jnp.concatenate([jnp.ones((T, MLA_NOPE), F32), cos, cos, z(LANES - MLA_NOPE - MLA_ROPE)], axis=1)
    rsa = jnp.concatenate([z(MLA_NOPE + half), sin, z(LANES - MLA_NOPE - MLA_ROPE)], axis=1)
    rsb = jnp.concatenate([z(MLA_NOPE), -sin, z(LANES - MLA_NOPE - half)], axis=1)
    return rc, rsa, rsb


def _static_tables(T):
    ns = T // SEL_BLOCK
    nc = T // CMP_STRIDE - 1
    ncp = T // CMP_STRIDE
    sstart = np.arange(ns) * SEL_BLOCK
    cstart = np.arange(nc) * CMP_STRIDE
    overlap = (np.clip(np.minimum(cstart[:, None] + CMP_LEN, sstart[None, :] + SEL_BLOCK)
                       - np.maximum(cstart[:, None], sstart[None, :]), 0, None) / CMP_STRIDE).astype(np.float32)
    ovt = np.zeros((ns, ncp), np.float32)
    ovt[:, :nc] = overlap.T
    eaug = np.zeros((T, LANES), np.float32)
    eaug[np.arange(T), HEAD_DIM + np.arange(T) // SEL_BLOCK] = 1.0
    gexp = np.zeros((LANES, 3 * NSA_HEADS * HEAD_DIM), np.float32)
    for h in range(NSA_HEADS):
        for j in range(3):
            c0 = NSA_HEADS * HEAD_DIM * j + HEAD_DIM * h
            gexp[GATE_LANE0 + 3 * h + j, c0:c0 + HEAD_DIM] = 1.0
    return jnp.asarray(ovt, BF16), jnp.asarray(eaug, BF16), jnp.asarray(gexp, BF16)


def kernel(x, w_in, w_out, norm_pre, norm_post, cmp_pos, cmp_w1, cmp_w2, mla_q_norm, mla_w_uq, mla_kv_norm,
           mla_w_ukv, swa_sinks, rel_bias):
    B, T, D = x.shape
    depth = w_in.shape[0]
    assert D == D_MODEL and T % (2 * FLASH_TILE) == 0 and T // SEL_BLOCK <= HEAD_DIM
    ncp = T // CMP_STRIDE
    tm = 512

    w_in_p, w_z = _prep_w_in(w_in)
    wuq, wk, wv = _prep_mla(mla_w_uq, mla_w_ukv)
    pet, peb, w1t, w1b, w2 = _prep_compress(cmp_pos, cmp_w1, cmp_w2)
    w_out_b = w_out.astype(BF16)
    rope_c, rope_sa, rope_sb = _rope_tables(T)
    ovt, eaug, gexp = _static_tables(T)

    bc = _bias_table(rel_bias, 0, NSA_HEADS, T, ncp, -CMP_STRIDE, -(CMP_LEN - 1), 0, 1 << 30, 256).reshape(NSA_HEADS, T, ncp)
    tab_near = _bias_table(rel_bias, 0, NSA_HEADS, FLASH_TILE, FLASH_TILE, -1, 0, 0, 1 << 30, 256, mult=LOG2E,
                           nblk=2, blk_d0=FLASH_TILE).reshape(NSA_HEADS, 2 * FLASH_TILE, FLASH_TILE)
    tab_win = _bias_table(rel_bias, 0, NSA_HEADS, Q_TILE, NSA_WINDOW + Q_TILE, -1, NSA_WINDOW, 0, NSA_WINDOW, Q_TILE,
                          nblk=NSA_WINDOW // Q_TILE + 1, cmin0=NSA_WINDOW, cmin_step=Q_TILE)
    tab_swa = _bias_table(rel_bias, NSA_HEADS, SWA_HEADS, Q_TILE, SWA_WINDOW + Q_TILE, -1, SWA_WINDOW, 0, SWA_WINDOW, Q_TILE,
                          nblk=SWA_WINDOW // Q_TILE + 1, cmin0=SWA_WINDOW, cmin_step=Q_TILE)
    b31 = rel_bias[:NSA_HEADS, REL_BUCKETS - 1] * LOG2E

    x2 = x.reshape(B * T, D)
    for l in range(depth):
        (qa, qa2, cmpkv, slck, slcv, win, qm, km, vm, krb, cq, ckv) = _project(
            x2, norm_pre[l][None], w_in_p[l], mla_q_norm[l][None], wuq[l], mla_kv_norm[l][None], wk[l], wv[l],
            rope_c, rope_sa, rope_sb, eaug, T, tm)
        r3 = lambda a: a.reshape(B, T, a.shape[-1])
        kvc = _compress(cmpkv.reshape(B, ncp, CMP_STRIDE * LANES), pet[l], peb[l], w1t[l], w1b[l], w2[l])
        qa3 = r3(qa)
        ocmp, selb = _cmp_attention(qa3, kvc, bc, ovt, T)
        win_pad = jnp.pad(r3(win), ((0, 0), (NSA_WINDOW, 0), (0, 0)))
        owin = _band_attention(qa3, win_pad, tab_win, None, NSA_HEADS, NSA_WINDOW, T, BF16)
        oslc = _slc_attention(b31, r3(qa2), selb, r3(slck), r3(slcv), tab_near, T)
        ckv_pad = jnp.pad(r3(ckv), ((0, 0), (SWA_WINDOW, 0), (0, 0)))
        oc = _band_attention(r3(cq), ckv_pad, tab_swa, swa_sinks[l], SWA_HEADS, SWA_WINDOW, T, BF16)
        ob = _mla_attention(r3(qm), r3(km), r3(vm), T)
        flat = lambda a: a.reshape(B * T, a.shape[-1])
        x2 = _out_project(x2, norm_pre[l][None], w_z[l], flat(ocmp), flat(oslc), flat(owin), krb, gexp,
                          flat(ob), flat(oc), w_out_b[l], norm_post[l][None], tm)
    return x2.reshape(B, T, D)
```

```python
import functools
import math

import numpy as np
import jax
import jax.numpy as jnp
from jax import lax
from jax.experimental import pallas as pl
from jax.experimental.pallas import tpu as pltpu

F32 = jnp.float32
BF16 = jnp.bfloat16

D_MODEL = 1024
HEAD_DIM = 64
NSA_HEADS = 4
CMP_LEN = 32
CMP_STRIDE = 16
CMP_HIDDEN = 128
SEL_BLOCK = 64
SEL_TOP = 16
NSA_WINDOW = 512
MLA_HEADS = 4
MLA_Q_RANK = 256
MLA_KV_RANK = 128
MLA_NOPE = 64
MLA_ROPE = 32
MLA_V = 64
ROPE_THETA = 10000.0
SWA_HEADS = 8
SWA_WINDOW = 128
REL_BUCKETS = 32
REL_MAX_DIST = 512
NORM_EPS = 1e-6
NEG = -1e30
BIG = 1e9
LOG2E = math.log2(math.e)
MLA_SCALE2 = (MLA_NOPE + MLA_ROPE) ** -0.5 * LOG2E
IN_SIZES = (256, 384, 12, 256, 128, 32, 512, 128, 1024)

LANES = 128
Q_TILE = 128
BAND_ROWS = 4096
PROJ_SUBTILES = 1
OUT_SUBTILES = 4
OUT_ROWS = 1024
CMP_GROUP = 4
FLASH_TILE = 512
GATE_LANE0 = 96
VMEM_LIMIT = 56 * 1024 * 1024

C_AQ, C_CMP, C_SLC, C_WIN, C_BCQ, C_BCKV, C_KRB, C_CQ, C_CKV, C_Z = (
    0, 256, 384, 512, 640, 896, 1024, 1152, 1664, 1792)


def _dot(a, b):
    return jnp.dot(a, b, preferred_element_type=F32)


def _dot_nt(a, b):
    return lax.dot_general(a, b, (((1,), (1,)), ((), ())), preferred_element_type=F32)


def _bucket_thresholds():
    d = np.arange(0, 4 * REL_MAX_DIST)
    exact = REL_BUCKETS // 2
    large = exact + (np.log(np.maximum(d, 1).astype(np.float32) / np.float32(exact))
                     / np.float32(math.log(REL_MAX_DIST / exact)) * np.float32(REL_BUCKETS - exact)).astype(np.int32)
    b = np.where(d < exact, d, np.minimum(large, REL_BUCKETS - 1))
    assert np.all(np.diff(b) >= 0) and b[-1] == REL_BUCKETS - 1
    return [int(np.argmax(b >= k)) for k in range(REL_BUCKETS)]


_THRESH = _bucket_thresholds()


def _cparams(sem):
    return pltpu.CompilerParams(dimension_semantics=sem, vmem_limit_bytes=VMEM_LIMIT)


def _table_kernel(h0, a_col, d0, blk_d0, lo, hi, cmin0, cmin_step, mult, rb_ref, out_ref):
    h = pl.program_id(0) + h0
    blk = pl.program_id(1)
    rt, cc = out_ref.shape[2], out_ref.shape[3]
    r = pl.program_id(2) * rt + lax.broadcasted_iota(jnp.int32, (rt, cc), 0)
    c = lax.broadcasted_iota(jnp.int32, (rt, cc), 1)
    dist = r + a_col * c + d0 + blk * blk_d0
    acc = jnp.full((rt, cc), rb_ref[h, 0], F32)
    for k in range(1, REL_BUCKETS):
        acc = jnp.where(dist >= _THRESH[k], rb_ref[h, k], acc)
    ok = (dist >= lo) & (dist < hi) & (c >= jnp.maximum(cmin0 - blk * cmin_step, 0))
    out_ref[0, 0] = jnp.where(ok, acc * mult, NEG)


def _bias_table(rel_bias, h0, nh, rows, cols, a_col, d0, lo, hi, rt, mult=1.0, nblk=1, blk_d0=0, cmin0=0, cmin_step=0):
    return pl.pallas_call(
        functools.partial(_table_kernel, h0, a_col, d0, blk_d0, lo, hi, cmin0, cmin_step, mult),
        grid=(nh, nblk, rows // rt),
        in_specs=[pl.BlockSpec(memory_space=pltpu.SMEM)],
        out_specs=pl.BlockSpec((1, 1, rt, cols), lambda h, b, r: (h, b, r, 0)),
        out_shape=jax.ShapeDtypeStruct((nh, nblk, rows, cols), F32),
        compiler_params=_cparams(("arbitrary", "arbitrary", "arbitrary")),
        name="bias_table",
    )(rel_bias)


def _rms(v, g):
    return v * lax.rsqrt(jnp.mean(v * v, axis=-1, keepdims=True) + NORM_EPS) * g


def _proj_kernel(x_ref, gpre_ref, w_ref, qn_ref, wuq_ref, kvn_ref, wk_ref, wv_ref,
                 rc_ref, rsa_ref, rsb_ref, eaug_ref,
                 qa_ref, qa2_ref, cmp_ref, slck_ref, slcv_ref, win_ref, qm_ref, km_ref, vm_ref,
                 krb_ref, cq_ref, ckv_ref):
    sub = x_ref.shape[0] // PROJ_SUBTILES
    rs = [slice(t * sub, (t + 1) * sub) for t in range(PROJ_SUBTILES)]
    hbs = [_rms(x_ref[r, :], gpre_ref[...]).astype(BF16) for r in rs]
    bounds = ((C_AQ, C_CMP), (C_CMP, C_WIN), (C_WIN, C_KRB), (C_KRB, C_Z))
    chunk_sets = [{ab: _dot(hb, w_ref[:, ab[0]:ab[1]]) for ab in bounds} for hb in hbs]
    lane = lax.broadcasted_iota(jnp.int32, (sub, LANES), 1)
    lo = lane < HEAD_DIM

    for r, chunks in zip(rs, chunk_sets):
        def proj(c0, c1):
            for (a, b), y in chunks.items():
                if a <= c0 and c1 <= b:
                    return y[:, c0 - a:c1 - a]
            raise AssertionError((c0, c1))

        qa_ref[r, :] = (proj(C_AQ, C_CMP) * 0.125).astype(BF16)
        qa2_ref[r, :] = (proj(C_AQ, C_CMP) * (0.125 * LOG2E)).astype(BF16)
        cmp_ref[r, :] = proj(C_CMP, C_SLC)
        slc = proj(C_SLC, C_WIN)
        slcv_ref[r, :] = jnp.where(lo, 1.0, slc).astype(BF16)
        slck_ref[r, :] = jnp.where(lo, slc, eaug_ref[r, :].astype(F32)).astype(BF16)
        win_ref[r, :] = proj(C_WIN, C_BCQ).astype(BF16)
        cq_ref[r, :] = (proj(C_CQ, C_CKV) * 0.125).astype(BF16)
        ckv_ref[r, :] = proj(C_CKV, C_Z).astype(BF16)

        rc, rsa, rsb = rc_ref[r, :], rsa_ref[r, :], rsb_ref[r, :]

        def rope(v):
            return v * rc + pltpu.roll(v, 16, 1) * rsa + pltpu.roll(v, 112, 1) * rsb

        cqn = _rms(proj(C_BCQ, C_BCKV), qn_ref[...]).astype(BF16)
        qm = _dot(cqn, wuq_ref[...])
        for h in range(MLA_HEADS):
            qh = rope(qm[:, LANES * h:LANES * (h + 1)]) * MLA_SCALE2
            qm_ref[r, LANES * h:LANES * (h + 1)] = qh.astype(BF16)

        krb = proj(C_KRB, C_CQ)
        krb_ref[r, :] = krb
        krr = jnp.where(lo | (lane >= HEAD_DIM + MLA_ROPE), 0.0, rope(krb))

        ckvn = _rms(proj(C_BCKV, C_KRB), kvn_ref[...]).astype(BF16)
        kn = _dot(ckvn, wk_ref[...])
        for h in range(MLA_HEADS):
            km_ref[r, LANES * h:LANES * (h + 1)] = (kn[:, LANES * h:LANES * (h + 1)] + krr).astype(BF16)
        vm = _dot(ckvn, wv_ref[...])
        for h in range(MLA_HEADS):
            vm_ref[r, LANES * h:LANES * (h + 1)] = jnp.where(lo, 1.0, vm[:, LANES * h:LANES * (h + 1)]).astype(BF16)


def _project(x2, gpre, w, qn, wuq, kvn, wk, wv, rope_c, rope_sa, rope_sb, eaug, T, tm):
    BT = x2.shape[0]
    nt = T // tm
    row = lambda i: (i, 0)
    fix = lambda i: (0, 0)
    pos = lambda i: (i % nt, 0)
    widths = [(256, BF16), (256, BF16), (128, F32), (128, BF16), (128, BF16), (128, BF16), (512, BF16), (512, BF16),
              (512, BF16), (128, F32), (512, BF16), (128, BF16)]
    return pl.pallas_call(
        _proj_kernel,
        grid=(BT // tm,),
        in_specs=[pl.BlockSpec((tm, D_MODEL), row), pl.BlockSpec((1, D_MODEL), fix),
                  pl.BlockSpec(w.shape, fix), pl.BlockSpec(qn.shape, fix), pl.BlockSpec(wuq.shape, fix),
                  pl.BlockSpec(kvn.shape, fix), pl.BlockSpec(wk.shape, fix), pl.BlockSpec(wv.shape, fix),
                  pl.BlockSpec((tm, LANES), pos), pl.BlockSpec((tm, LANES), pos), pl.BlockSpec((tm, LANES), pos),
                  pl.BlockSpec((tm, LANES), pos)],
        out_specs=[pl.BlockSpec((tm, wd), row) for wd, _ in widths],
        out_shape=[jax.ShapeDtypeStruct((BT, wd), dt) for wd, dt in widths],
        compiler_params=_cparams(("arbitrary",)),
        name="proj",
    )(x2, gpre, w, qn, wuq, kvn, wk, wv, rope_c, rope_sa, rope_sb, eaug)


def _compress_kernel(c_ref, pet_ref, peb_ref, w1t_ref, w1b_ref, w2_ref, o_ref):
    c = c_ref[0]
    nch = c.shape[0]
    top = (c + pet_ref[...]).astype(BF16)
    bot = (c + peb_ref[...]).astype(BF16)
    out = jnp.zeros((nch, LANES), F32)
    for i in range(2):
        pre = _dot(top, w1t_ref[i]) + pltpu.roll(_dot(bot, w1b_ref[i]), nch - 1, 0)
        hid = pre * (1.0 / (1.0 + jnp.exp(-pre)))
        out = out + _dot(hid.astype(BF16), w2_ref[i])
    o_ref[0] = out.astype(BF16)


def _compress(chunks, pet, peb, w1t, w1b, w2):
    B, nch, width = chunks.shape
    full = lambda a: pl.BlockSpec(a.shape, lambda b: (0,) * a.ndim)
    return pl.pallas_call(
        _compress_kernel,
        grid=(B,),
        in_specs=[pl.BlockSpec((1, nch, width), lambda b: (b, 0, 0)),
                  full(pet), full(peb), full(w1t), full(w1b), full(w2)],
        out_specs=pl.BlockSpec((1, nch, LANES), lambda b: (b, 0, 0)),
        out_shape=jax.ShapeDtypeStruct((B, nch, LANES), BF16),
        compiler_params=_cparams(("arbitrary",)),
        name="compress",
    )(chunks, pet, peb, w1t, w1b, w2)


def _head_slabs(qblk, n_heads):
    q = qblk.astype(F32)
    lo = lax.broadcasted_iota(jnp.int32, (q.shape[0], LANES), 1) < HEAD_DIM
    out = []
    for j in range(n_heads // 2):
        slab = q[:, LANES * j:LANES * (j + 1)]
        out.append(jnp.where(lo, slab, 0.0))
        out.append(jnp.where(lo, pltpu.roll(slab, HEAD_DIM, 1), 0.0))
    return out


def _merge_upper(accs):
    lo = lax.broadcasted_iota(jnp.int32, accs[0].shape, 1) < HEAD_DIM
    slabs = [jnp.where(lo, pltpu.roll(accs[2 * j], HEAD_DIM, 1), accs[2 * j + 1]) for j in range(len(accs) // 2)]
    return jnp.concatenate(slabs, axis=1)


def _merge_normalized(accs):
    lo = lax.broadcasted_iota(jnp.int32, accs[0].shape, 1) < HEAD_DIM
    slabs = []
    for j in range(len(accs) // 2):
        a, b = accs[2 * j], accs[2 * j + 1]
        out = jnp.where(lo, pltpu.roll(a, HEAD_DIM, 1), b)
        den = jnp.where(lo, a, pltpu.roll(b, HEAD_DIM, 1))
        slabs.append(out / den)
    return jnp.concatenate(slabs, axis=1)


def _row_max_lanes(s):
    return jnp.broadcast_to(jnp.max(s, axis=-1, keepdims=True), (s.shape[0], LANES))


def _sub_lanes(s, ref):
    return jnp.concatenate([s[:, LANES * i:LANES * (i + 1)] - ref for i in range(s.shape[1] // LANES)], axis=1)


def _topk_mask_t(imp, t0, x_s, cnt_s):
    ns, tq = imp.shape
    n_top = min(SEL_TOP, ns)
    srow = lax.broadcasted_iota(jnp.int32, (ns, tq), 0)
    cur = (t0 + lax.broadcasted_iota(jnp.int32, (ns, tq), 1)) // SEL_BLOCK
    forced = (srow == 0) | (srow == cur) | (srow == cur - 1)
    x_s[...] = jnp.where(forced, BIG, jnp.where(srow <= cur, imp, -BIG))
    cnt_s[...] = jnp.zeros((ns, tq), F32)
    sub = 8
    last = (t0 + tq - 1) // SEL_BLOCK
    for c in range(ns // sub):
        @pl.when((last >= n_top) & (sub * c <= last))
        def _():
            groups = [x_s[sub * v:sub * (v + 1), :] for v in range(ns // sub)]
            cnts = [cnt_s[sub * v:sub * (v + 1), :] for v in range(ns // sub)]
            for sp in range(sub * c, sub * (c + 1)):
                other = jnp.broadcast_to(x_s[sp:sp + 1, :], (sub, tq))
                for v, xv in enumerate(groups):
                    if sub * v > sp:
                        beats = other >= xv
                    elif sub * v + sub - 1 < sp:
                        beats = other > xv
                    else:
                        beats = (other > xv) | ((other == xv) & (srow[sub * v:sub * (v + 1)] > sp))
                    cnts[v] = cnts[v] + jnp.where(beats, 1.0, 0.0)
            for v in range(ns // sub):
                cnt_s[sub * v:sub * (v + 1), :] = cnts[v]
    sel = (cnt_s[...] < float(n_top)) & (srow <= cur)
    return jnp.where(sel, 0.0, NEG)


def _cmp_kernel(q_ref, kvc_ref, bc_ref, ovt_ref, ocmp_ref, selb_ref, x_s, cnt_s):
    tq = Q_TILE
    kvc = kvc_ref[0]
    ovt = ovt_ref[...]
    ns = ovt.shape[0]
    groups = range(CMP_GROUP)
    sls = [slice(g * tq, (g + 1) * tq) for g in groups]
    s_alls = []
    for g in groups:
        qs = _head_slabs(q_ref[0, sls[g], :], NSA_HEADS)
        s_alls.append(_dot_nt(jnp.concatenate([q.astype(BF16) for q in qs], axis=0), kvc))
    psums, o_alls = [], []
    for g in groups:
        valid = bc_ref[0, sls[g], :] > 0.5 * NEG
        psum = jnp.zeros(valid.shape, F32)
        ps = []
        for h in range(NSA_HEADS):
            s = s_alls[g][h * tq:(h + 1) * tq] + bc_ref[h, sls[g], :]
            m = jnp.max(s, axis=-1, keepdims=True)
            e = jnp.where(valid, jnp.exp(s - m), 0.0)
            l = jnp.sum(e, axis=-1, keepdims=True)
            p = e * (1.0 / jnp.where(l > 0.0, l, 1.0))
            psum = psum + p
            ps.append(p.astype(BF16))
        psums.append(psum)
        o_alls.append(_dot(jnp.concatenate(ps, axis=0), kvc))
    imps = []
    for g in groups:
        hi = psums[g].astype(BF16)
        lo = (psums[g] - hi.astype(F32)).astype(BF16)
        imps.append(_dot_nt(ovt, hi) + _dot_nt(ovt, lo))
    for g in groups:
        ocmp = _merge_upper([o_alls[g][h * tq:(h + 1) * tq] for h in range(NSA_HEADS)])
        ocmp_ref[0, sls[g], :] = ocmp.astype(ocmp_ref.dtype)
        sb = _topk_mask_t(imps[g], (pl.program_id(1) * CMP_GROUP + g) * tq, x_s.at[g], cnt_s.at[g])
        if ns < HEAD_DIM:
            sb = jnp.concatenate([sb, jnp.full((HEAD_DIM - ns, tq), NEG, F32)], axis=0)
        full = jnp.concatenate([jnp.zeros((HEAD_DIM, tq), F32), sb], axis=0)
        selb_ref[0, sls[g], :] = full.T.astype(BF16)


def _cmp_attention(qa, kvc, bc, ovt, T):
    B = qa.shape[0]
    ncp = kvc.shape[1]
    rows = CMP_GROUP * Q_TILE
    return pl.pallas_call(
        _cmp_kernel,
        grid=(B, T // rows),
        in_specs=[pl.BlockSpec((1, rows, 256), lambda b, n: (b, n, 0)),
                  pl.BlockSpec((1, ncp, LANES), lambda b, n: (b, 0, 0)),
                  pl.BlockSpec((NSA_HEADS, rows, ncp), lambda b, n: (0, n, 0)),
                  pl.BlockSpec(ovt.shape, lambda b, n: (0, 0))],
        out_specs=[pl.BlockSpec((1, rows, 256), lambda b, n: (b, n, 0)),
                   pl.BlockSpec((1, rows, LANES), lambda b, n: (b, n, 0))],
        out_shape=[jax.ShapeDtypeStruct((B, T, 256), BF16), jax.ShapeDtypeStruct((B, T, LANES), BF16)],
        scratch_shapes=[pltpu.VMEM((CMP_GROUP, ovt.shape[0], Q_TILE), F32),
                        pltpu.VMEM((CMP_GROUP, ovt.shape[0], Q_TILE), F32)],
        compiler_params=_cparams(("arbitrary", "arbitrary")),
        name="cmp_select",
    )(qa, kvc, bc, ovt)


def _softmax_tile(s2, rows, m_s, shift=None):
    m_t = _row_max_lanes(s2)
    if shift is not None:
        m_t = m_t + shift
    m_old = m_s[rows]
    m_new = jnp.maximum(m_old, m_t)
    p = jnp.exp2(_sub_lanes(s2, m_new if shift is None else m_new - shift))
    alpha = jnp.exp2(m_old - m_new)
    m_s[rows] = m_new
    return p.astype(BF16), alpha


def _slc_kernel(b31_ref, q_ref, selb_ref, ka_ref, kv_ref, tab_ref, o_ref, m_s, acc_s):
    tq = q_ref.shape[1]
    H = NSA_HEADS
    n = pl.program_id(1)
    qs = _head_slabs(q_ref[0], H)
    selb = selb_ref[0].astype(F32)
    qst = jnp.concatenate([(qs[h] + selb).astype(BF16) for h in range(H)], axis=0)
    m_s[...] = jnp.full(m_s.shape, -jnp.inf, F32)
    acc_s[...] = jnp.zeros(acc_s.shape, F32)

    def scores(k0):
        k0 = pl.multiple_of(k0, tq)
        return _dot_nt(qst, ka_ref[0, pl.ds(k0, tq), :]), kv_ref[0, pl.ds(k0, tq), :]

    def update(s_all, kv, near_block):
        ps, alphas = [], []
        for h in range(H):
            rows = slice(h * tq, (h + 1) * tq)
            s2 = s_all[rows]
            if near_block is None:
                p, alpha = _softmax_tile(s2, rows, m_s, shift=b31_ref[h])
            else:
                s2 = s2 + tab_ref[h, near_block * tq:(near_block + 1) * tq, :]
                p, alpha = _softmax_tile(s2, rows, m_s)
            ps.append(p)
            alphas.append(alpha)
        pv = _dot(jnp.concatenate(ps, axis=0), kv)
        for h in range(H):
            rows = slice(h * tq, (h + 1) * tq)
            acc_s[rows] = alphas[h] * acc_s[rows] + pv[rows]

    def far_pair(i, carry):
        first, second = scores(2 * i * tq), scores((2 * i + 1) * tq)
        update(*first, None)
        update(*second, None)
        return carry

    n_far = jnp.maximum(n - 1, 0)
    lax.fori_loop(0, n_far // 2, far_pair, 0)

    @pl.when(n_far % 2 == 1)
    def _():
        update(*scores((n_far - 1) * tq), None)

    @pl.when(n >= 1)
    def _():
        update(*scores(jnp.maximum(n - 1, 0) * tq), 1)

    update(*scores(n * tq), 0)
    o_ref[0] = _merge_normalized([acc_s[h * tq:(h + 1) * tq] for h in range(H)]).astype(o_ref.dtype)


def _slc_attention(b31, qa, selb, slck, slcv, tab, T):
    B = qa.shape[0]
    tq = FLASH_TILE
    H = NSA_HEADS
    tile = lambda w: pl.BlockSpec((1, tq, w), lambda b, n: (b, n, 0))
    return pl.pallas_call(
        _slc_kernel,
        grid=(B, T // tq),
        in_specs=[pl.BlockSpec(memory_space=pltpu.SMEM),
                  tile(256), tile(LANES),
                  pl.BlockSpec((1, T, LANES), lambda b, n: (b, 0, 0)),
                  pl.BlockSpec((1, T, LANES), lambda b, n: (b, 0, 0)),
                  pl.BlockSpec(tab.shape, lambda b, n: (0, 0, 0))],
        out_specs=tile(256),
        out_shape=jax.ShapeDtypeStruct((B, T, 256), BF16),
        scratch_shapes=[pltpu.VMEM((H * tq, LANES), F32), pltpu.VMEM((H * tq, LANES), F32)],
        compiler_params=_cparams(("arbitrary", "arbitrary")),
        name="slc_attention",
    )(b31, qa, selb, slck, slcv, tab)


def _band_kernel(n_heads, window, has_sink, n_sub, *refs):
    if has_sink:
        q_ref, kv_ref, tab_ref, sink_ref, o_ref = refs
    else:
        q_ref, kv_ref, tab_ref, o_ref = refs
    tq = Q_TILE
    span = window + tq
    lo_kv = lax.broadcasted_iota(jnp.int32, (span, LANES), 1) < HEAD_DIM
    lo_o = lax.broadcasted_iota(jnp.int32, (n_heads * tq, LANES), 1) < HEAD_DIM
    groups = range(n_sub)
    tiles = [pl.program_id(1) * n_sub + g for g in groups]
    kvs, dots = [], []
    for g in groups:
        q0 = pl.multiple_of(tiles[g] * tq, tq)
        kvs.append(kv_ref[0, pl.ds(q0, span), :])
        qs = _head_slabs(q_ref[0, g * tq:(g + 1) * tq, :], n_heads)
        dots.append(_dot_nt(jnp.concatenate([q.astype(BF16) for q in qs], axis=0), kvs[g]))
    accs, ms = [], []
    for g in groups:
        s = dots[g] + tab_ref[jnp.minimum(tiles[g], window // tq)]
        m = _row_max_lanes(s)
        if has_sink:
            m = jnp.maximum(m, sink_ref[...])
        e = jnp.exp(_sub_lanes(s, m))
        ms.append(m)
        accs.append(_dot(e.astype(BF16), jnp.where(lo_kv, 1.0, kvs[g]).astype(BF16)))
    for g in groups:
        acc = accs[g]
        if has_sink:
            acc = acc + jnp.where(lo_o, jnp.exp(sink_ref[...] - ms[g]), 0.0)
        out = _merge_normalized([acc[h * tq:(h + 1) * tq] for h in range(n_heads)])
        o_ref[0, g * tq:(g + 1) * tq, :] = out.astype(o_ref.dtype)


def _band_attention(q, kv_pad, tab, sinks, n_heads, window, T, out_dtype):
    B = q.shape[0]
    width = n_heads * HEAD_DIM
    has_sink = sinks is not None
    nvar = tab.shape[1]
    tab2 = jnp.transpose(tab, (1, 0, 2, 3)).reshape(nvar, n_heads * Q_TILE, window + Q_TILE)
    n_sub = BAND_ROWS // (n_heads * Q_TILE)
    rows = n_sub * Q_TILE
    in_specs = [pl.BlockSpec((1, rows, width), lambda b, n: (b, n, 0)),
                pl.BlockSpec((1, T + window, LANES), lambda b, n: (b, 0, 0)),
                pl.BlockSpec(tab2.shape, lambda b, n: (0, 0, 0))]
    args = (q, kv_pad, tab2)
    if has_sink:
        rep = jnp.broadcast_to(jnp.repeat(sinks, Q_TILE)[:, None], (n_heads * Q_TILE, LANES))
        in_specs.append(pl.BlockSpec(rep.shape, lambda b, n: (0, 0)))
        args = args + (rep,)
    return pl.pallas_call(
        functools.partial(_band_kernel, n_heads, window, has_sink, n_sub),
        grid=(B, T // rows),
        in_specs=in_specs,
        out_specs=pl.BlockSpec((1, rows, width), lambda b, n: (b, n, 0)),
        out_shape=jax.ShapeDtypeStruct((B, T, width), out_dtype),
        compiler_params=_cparams(("arbitrary", "arbitrary")),
        name="band_sink" if has_sink else "band_window",
    )(*args)


def _mla_kernel(q_ref, k_ref, v_ref, o_ref, m_s, acc_s):
    tq = q_ref.shape[1]
    H = MLA_HEADS
    n = pl.program_id(1)
    m_s[...] = jnp.full(m_s.shape, -jnp.inf, F32)
    acc_s[...] = jnp.zeros(acc_s.shape, F32)

    def scores(k0):
        k0 = pl.multiple_of(k0, tq)
        return k0, [_dot_nt(q_ref[0, :, LANES * h:LANES * (h + 1)], k_ref[0, pl.ds(k0, tq), LANES * h:LANES * (h + 1)])
                    for h in range(H)]

    def update(k0, ss, diag):
        ps, alphas = [], []
        for h in range(H):
            rows = slice(h * tq, (h + 1) * tq)
            s2 = ss[h]
            if diag:
                row = lax.broadcasted_iota(jnp.int32, (tq, tq), 0)
                col = lax.broadcasted_iota(jnp.int32, (tq, tq), 1)
                s2 = jnp.where(col <= row, s2, NEG)
            p, alpha = _softmax_tile(s2, rows, m_s)
            ps.append(p)
            alphas.append(alpha)
        for h in range(H):
            rows = slice(h * tq, (h + 1) * tq)
            v = v_ref[0, pl.ds(k0, tq), LANES * h:LANES * (h + 1)]
            acc_s[rows] = alphas[h] * acc_s[rows] + _dot(ps[h], v)

    def far_pair(i, carry):
        first, second = scores(2 * i * tq), scores((2 * i + 1) * tq)
        update(*first, False)
        update(*second, False)
        return carry

    lax.fori_loop(0, n // 2, far_pair, 0)

    @pl.when(n % 2 == 1)
    def _():
        update(*scores((n - 1) * tq), False)

    update(*scores(n * tq), True)
    o_ref[0] = _merge_normalized([acc_s[h * tq:(h + 1) * tq] for h in range(H)]).astype(o_ref.dtype)


def _mla_attention(qm, km, vm, T):
    B = qm.shape[0]
    tq = FLASH_TILE
    H = MLA_HEADS
    return pl.pallas_call(
        _mla_kernel,
        grid=(B, T // tq),
        in_specs=[pl.BlockSpec((1, tq, 512), lambda b, n: (b, n, 0)),
                  pl.BlockSpec((1, T, 512), lambda b, n: (b, 0, 0)),
                  pl.BlockSpec((1, T, 512), lambda b, n: (b, 0, 0))],
        out_specs=pl.BlockSpec((1, tq, 256), lambda b, n: (b, n, 0)),
        out_shape=jax.ShapeDtypeStruct((B, T, 256), BF16),
        scratch_shapes=[pltpu.VMEM((H * tq, LANES), F32), pltpu.VMEM((H * tq, LANES), F32)],
        compiler_params=_cparams(("arbitrary", "arbitrary")),
        name="mla_attention",
    )(qm, km, vm)


def _out_kernel(x_ref, gpre_ref, wz_ref, ocmp_ref, oslc_ref, owin_ref, krb_ref, gexp_ref, ob_ref, oc_ref, w_ref,
                gpost_ref, o_ref):
    sub = x_ref.shape[0] // OUT_SUBTILES
    rs = [slice(t * sub, (t + 1) * sub) for t in range(OUT_SUBTILES)]
    w_a = NSA_HEADS * HEAD_DIM
    hs = [_rms(x_ref[r, :], gpre_ref[...]).astype(BF16) for r in rs]
    zs = [_dot(h, wz_ref[...]) for h in hs]
    mixeds = []
    for r, z in zip(rs, zs):
        sig = 1.0 / (1.0 + jnp.exp(-krb_ref[r, :]))
        hi = sig.astype(BF16)
        lo = (sig - hi.astype(F32)).astype(BF16)
        gates = _dot(hi, gexp_ref[...]) + _dot(lo, gexp_ref[...])
        oa = None
        for j, br in enumerate((ocmp_ref, oslc_ref, owin_ref)):
            term = gates[:, w_a * j:w_a * (j + 1)] * br[r, :].astype(F32)
            oa = term if oa is None else oa + term
        heads = jnp.concatenate([oa, ob_ref[r, :].astype(F32), oc_ref[r, :].astype(F32)], axis=1)
        mixeds.append((heads * (z * (1.0 / (1.0 + jnp.exp(-z))))).astype(BF16))
    ys = [_dot(m, w_ref[...]) for m in mixeds]
    for r, y in zip(rs, ys):
        o_ref[r, :] = x_ref[r, :] + _rms(y, gpost_ref[...])


def _out_project(x2, gpre, wz, ocmp, oslc, owin, krb, gexp, ob, oc, w, gpost, tm):
    BT = x2.shape[0]
    row = lambda i: (i, 0)
    fix = lambda i: (0, 0)
    spec = lambda a: pl.BlockSpec((tm, a.shape[1]), row)
    full = lambda a: pl.BlockSpec(a.shape, fix)
    return pl.pallas_call(
        _out_kernel,
        grid=(BT // tm,),
        in_specs=[spec(x2), full(gpre), full(wz), spec(ocmp), spec(oslc), spec(owin), spec(krb), full(gexp),
                  spec(ob), spec(oc), full(w), full(gpost)],
        out_specs=pl.BlockSpec((tm, D_MODEL), row),
        out_shape=jax.ShapeDtypeStruct((BT, D_MODEL), F32),
        compiler_params=_cparams(("arbitrary",)),
        name="out_proj",
    )(x2, gpre, wz, ocmp, oslc, owin, krb, gexp, ob, oc, w, gpost)


def _split_in(w):
    outs, o = [], 0
    for s in IN_SIZES:
        outs.append(w[..., o:o + s])
        o += s
    return outs


def _prep_w_in(w):
    a_q, a_kv, a_g, b_cq, b_ckv, b_kr, c_q, c_kv, z = _split_in(w.astype(BF16))
    zeros = lambda n: jnp.zeros(w.shape[:-1] + (n,), BF16)
    krb = jnp.concatenate([zeros(HEAD_DIM), b_kr, a_g, zeros(LANES - HEAD_DIM - MLA_ROPE - 3 * NSA_HEADS)], axis=-1)
    return jnp.concatenate([a_q, a_kv, b_cq, b_ckv, krb, c_q, c_kv], axis=-1), z


def _prep_mla(w_uq, w_ukv):
    L = w_uq.shape[0]
    dq = MLA_NOPE + MLA_ROPE
    uq = w_uq.reshape(L, MLA_Q_RANK, MLA_HEADS, dq)
    uq = jnp.concatenate([uq, jnp.zeros((L, MLA_Q_RANK, MLA_HEADS, LANES - dq), uq.dtype)], axis=-1)
    ukv = w_ukv.reshape(L, MLA_KV_RANK, MLA_HEADS, MLA_NOPE + MLA_V)
    uk = jnp.concatenate([ukv[..., :MLA_NOPE], jnp.zeros((L, MLA_KV_RANK, MLA_HEADS, LANES - MLA_NOPE), ukv.dtype)], axis=-1)
    uv = jnp.concatenate([jnp.zeros((L, MLA_KV_RANK, MLA_HEADS, LANES - MLA_V), ukv.dtype), ukv[..., MLA_NOPE:]], axis=-1)
    return (uq.reshape(L, MLA_Q_RANK, MLA_HEADS * LANES).astype(BF16),
            uk.reshape(L, MLA_KV_RANK, MLA_HEADS * LANES).astype(BF16),
            uv.reshape(L, MLA_KV_RANK, MLA_HEADS * LANES).astype(BF16))


def _prep_compress(cmp_pos, cmp_w1, cmp_w2):
    L = cmp_pos.shape[0]
    pe = jnp.concatenate([cmp_pos[:, 0], cmp_pos[:, 1]], axis=-1)
    pet = pe[:, :CMP_STRIDE].reshape(L, 1, CMP_STRIDE * LANES)
    peb = pe[:, CMP_STRIDE:].reshape(L, 1, CMP_STRIDE * LANES)
    w1 = cmp_w1.reshape(L, 2, CMP_LEN, HEAD_DIM, CMP_HIDDEN)
    zero = jnp.zeros_like(w1[:, 0])
    w1k = jnp.concatenate([w1[:, 0], zero], axis=2)
    w1v = jnp.concatenate([zero, w1[:, 1]], axis=2)
    w1e = jnp.stack([w1k, w1v], axis=1)
    w1t = w1e[:, :, :CMP_STRIDE].reshape(L, 2, CMP_STRIDE * LANES, CMP_HIDDEN).astype(BF16)
    w1b = w1e[:, :, CMP_STRIDE:].reshape(L, 2, CMP_STRIDE * LANES, CMP_HIDDEN).astype(BF16)
    z2 = jnp.zeros_like(cmp_w2[:, 0])
    w2 = jnp.stack([jnp.concatenate([cmp_w2[:, 0], z2], axis=-1),
                    jnp.concatenate([z2, cmp_w2[:, 1]], axis=-1)], axis=1).astype(BF16)
    return pet, peb, w1t, w1b, w2


def _rope_tables(T):
    half = MLA_ROPE // 2
    inv = ROPE_THETA ** (-jnp.arange(half, dtype=F32) / half)
    ang = jnp.arange(T).astype(F32)[:, None] * inv[None, :]
    cos, sin = jnp.cos(ang), jnp.sin(ang)
    z = lambda n: jnp.zeros((T, n), F32)
    rc = jnp.concatenate([jnp.ones((T, MLA_NOPE), F32), cos, cos, z(LANES - MLA_NOPE - MLA_ROPE)], axis=1)
    rsa = jnp.concatenate([z(MLA_NOPE + half), sin, z(LANES - MLA_NOPE - MLA_ROPE)], axis=1)
    rsb = jnp.concatenate([z(MLA_NOPE), -sin, z(LANES - MLA_NOPE - half)], axis=1)
    return rc, rsa, rsb


def _static_tables(T):
    ns = T // SEL_BLOCK
    nc = T // CMP_STRIDE - 1
    ncp = T // CMP_STRIDE
    sstart = np.arange(ns) * SEL_BLOCK
    cstart = np.arange(nc) * CMP_STRIDE
    overlap = (np.clip(np.minimum(cstart[:, None] + CMP_LEN, sstart[None, :] + SEL_BLOCK)
                       - np.maximum(cstart[:, None], sstart[None, :]), 0, None) / CMP_STRIDE).astype(np.float32)
    ovt = np.zeros((ns, ncp), np.float32)
    ovt[:, :nc] = overlap.T
    eaug = np.zeros((T, LANES), np.float32)
    eaug[np.arange(T), HEAD_DIM + np.arange(T) // SEL_BLOCK] = 1.0
    gexp = np.zeros((LANES, 3 * NSA_HEADS * HEAD_DIM), np.float32)
    for h in range(NSA_HEADS):
        for j in range(3):
            c0 = NSA_HEADS * HEAD_DIM * j + HEAD_DIM * h
            gexp[GATE_LANE0 + 3 * h + j, c0:c0 + HEAD_DIM] = 1.0
    return jnp.asarray(ovt, BF16), jnp.asarray(eaug, BF16), jnp.asarray(gexp, BF16)


def kernel(x, w_in, w_out, norm_pre, norm_post, cmp_pos, cmp_w1, cmp_w2, mla_q_norm, mla_w_uq, mla_kv_norm,
           mla_w_ukv, swa_sinks, rel_bias):
    B, T, D = x.shape
    depth = w_in.shape[0]
    assert D == D_MODEL and T % (2 * FLASH_TILE) == 0 and T // SEL_BLOCK <= HEAD_DIM
    ncp = T // CMP_STRIDE
    tm = 512

    w_in_p, w_z = _prep_w_in(w_in)
    wuq, wk, wv = _prep_mla(mla_w_uq, mla_w_ukv)
    pet, peb, w1t, w1b, w2 = _prep_compress(cmp_pos, cmp_w1, cmp_w2)
    w_out_b = w_out.astype(BF16)
    rope_c, rope_sa, rope_sb = _rope_tables(T)
    ovt, eaug, gexp = _static_tables(T)

    bc = _bias_table(rel_bias, 0, NSA_HEADS, T, ncp, -CMP_STRIDE, -(CMP_LEN - 1), 0, 1 << 30, 256).reshape(NSA_HEADS, T, ncp)
    tab_near = _bias_table(rel_bias, 0, NSA_HEADS, FLASH_TILE, FLASH_TILE, -1, 0, 0, 1 << 30, 256, mult=LOG2E,
                           nblk=2, blk_d0=FLASH_TILE).reshape(NSA_HEADS, 2 * FLASH_TILE, FLASH_TILE)
    tab_win = _bias_table(rel_bias, 0, NSA_HEADS, Q_TILE, NSA_WINDOW + Q_TILE, -1, NSA_WINDOW, 0, NSA_WINDOW, Q_TILE,
                          nblk=NSA_WINDOW // Q_TILE + 1, cmin0=NSA_WINDOW, cmin_step=Q_TILE)
    tab_swa = _bias_table(rel_bias, NSA_HEADS, SWA_HEADS, Q_TILE, SWA_WINDOW + Q_TILE, -1, SWA_WINDOW, 0, SWA_WINDOW, Q_TILE,
                          nblk=SWA_WINDOW // Q_TILE + 1, cmin0=SWA_WINDOW, cmin_step=Q_TILE)
    b31 = rel_bias[:NSA_HEADS, REL_BUCKETS - 1] * LOG2E

    x2 = x.reshape(B * T, D)
    for l in range(depth):
        (qa, qa2, cmpkv, slck, slcv, win, qm, km, vm, krb, cq, ckv) = _project(
            x2, norm_pre[l][None], w_in_p[l], mla_q_norm[l][None], wuq[l], mla_kv_norm[l][None], wk[l], wv[l],
            rope_c, rope_sa, rope_sb, eaug, T, PROJ_SUBTILES * tm)
        r3 = lambda a: a.reshape(B, T, a.shape[-1])
        kvc = _compress(cmpkv.reshape(B, ncp, CMP_STRIDE * LANES), pet[l], peb[l], w1t[l], w1b[l], w2[l])
        qa3 = r3(qa)
        ocmp, selb = _cmp_attention(qa3, kvc, bc, ovt, T)
        win_pad = jnp.pad(r3(win), ((0, 0), (NSA_WINDOW, 0), (0, 0)))
        owin = _band_attention(qa3, win_pad, tab_win, None, NSA_HEADS, NSA_WINDOW, T, BF16)
        oslc = _slc_attention(b31, r3(qa2), selb, r3(slck), r3(slcv), tab_near, T)
        ckv_pad = jnp.pad(r3(ckv), ((0, 0), (SWA_WINDOW, 0), (0, 0)))
        oc = _band_attention(r3(cq), ckv_pad, tab_swa, swa_sinks[l], SWA_HEADS, SWA_WINDOW, T, BF16)
        ob = _mla_attention(r3(qm), r3(km), r3(vm), T)
        flat = lambda a: a.reshape(B * T, a.shape[-1])
        x2 = _out_project(x2, norm_pre[l][None], w_z[l], flat(ocmp), flat(oslc), flat(owin), krb, gexp,
                          flat(ob), flat(oc), w_out_b[l], norm_post[l][None], OUT_ROWS)
    return x2.reshape(B, T, D)
```

```python
import functools
import math

import numpy as np
import jax
import jax.numpy as jnp
from jax import lax
from jax.experimental import pallas as pl
from jax.experimental.pallas import tpu as pltpu

F32 = jnp.float32
BF16 = jnp.bfloat16

D_MODEL = 1024
HEAD_DIM = 64
NSA_HEADS = 4
CMP_LEN = 32
CMP_STRIDE = 16
CMP_HIDDEN = 128
SEL_BLOCK = 64
SEL_TOP = 16
NSA_WINDOW = 512
MLA_HEADS = 4
MLA_Q_RANK = 256
MLA_KV_RANK = 128
MLA_NOPE = 64
MLA_ROPE = 32
MLA_V = 64
ROPE_THETA = 10000.0
SWA_HEADS = 8
SWA_WINDOW = 128
REL_BUCKETS = 32
REL_MAX_DIST = 512
NORM_EPS = 1e-6
NEG = -1e30
BIG = 1e9
LOG2E = math.log2(math.e)
MLA_SCALE2 = (MLA_NOPE + MLA_ROPE) ** -0.5 * LOG2E
IN_SIZES = (256, 384, 12, 256, 128, 32, 512, 128, 1024)

LANES = 128
Q_TILE = 128
BAND_ROWS = 4096
PROJ_SUBTILES = 1
OUT_SUBTILES = 4
OUT_ROWS = 1024
CMP_GROUP = 4
FLASH_TILE = 512
GATE_LANE0 = 96
VMEM_LIMIT = 56 * 1024 * 1024

C_AQ, C_CMP, C_SLC, C_WIN, C_BCQ, C_BCKV, C_KRB, C_CQ, C_CKV, C_Z = (
    0, 256, 384, 512, 640, 896, 1024, 1152, 1664, 1792)


def _dot(a, b):
    return jnp.dot(a, b, preferred_element_type=F32)


def _dot_nt(a, b):
    return lax.dot_general(a, b, (((1,), (1,)), ((), ())), preferred_element_type=F32)


def _bucket_thresholds():
    d = np.arange(0, 4 * REL_MAX_DIST)
    exact = REL_BUCKETS // 2
    large = exact + (np.log(np.maximum(d, 1).astype(np.float32) / np.float32(exact))
                     / np.float32(math.log(REL_MAX_DIST / exact)) * np.float32(REL_BUCKETS - exact)).astype(np.int32)
    b = np.where(d < exact, d, np.minimum(large, REL_BUCKETS - 1))
    assert np.all(np.diff(b) >= 0) and b[-1] == REL_BUCKETS - 1
    return [int(np.argmax(b >= k)) for k in range(REL_BUCKETS)]


_THRESH = _bucket_thresholds()


def _cparams(sem):
    return pltpu.CompilerParams(dimension_semantics=sem, vmem_limit_bytes=VMEM_LIMIT)


def _table_kernel(h0, a_col, d0, blk_d0, lo, hi, mult, rb_ref, out_ref):
    h = pl.program_id(0) + h0
    blk = pl.program_id(1)
    rt, cc = out_ref.shape[2], out_ref.shape[3]
    r = pl.program_id(2) * rt + lax.broadcasted_iota(jnp.int32, (rt, cc), 0)
    c = lax.broadcasted_iota(jnp.int32, (rt, cc), 1)
    dist = r + a_col * c + d0 + blk * blk_d0
    acc = jnp.full((rt, cc), rb_ref[h, 0], F32)
    for k in range(1, REL_BUCKETS):
        acc = jnp.where(dist >= _THRESH[k], rb_ref[h, k], acc)
    ok = (dist >= lo) & (dist < hi)
    out_ref[0, 0] = jnp.where(ok, acc * mult, NEG)


def _bias_table(rel_bias, h0, nh, rows, cols, a_col, d0, lo, hi, rt, mult=1.0, nblk=1, blk_d0=0):
    return pl.pallas_call(
        functools.partial(_table_kernel, h0, a_col, d0, blk_d0, lo, hi, mult),
        grid=(nh, nblk, rows // rt),
        in_specs=[pl.BlockSpec(memory_space=pltpu.SMEM)],
        out_specs=pl.BlockSpec((1, 1, rt, cols), lambda h, b, r: (h, b, r, 0)),
        out_shape=jax.ShapeDtypeStruct((nh, nblk, rows, cols), F32),
        compiler_params=_cparams(("arbitrary", "arbitrary", "arbitrary")),
        name="bias_table",
    )(rel_bias)


def _rms(v, g):
    return v * lax.rsqrt(jnp.mean(v * v, axis=-1, keepdims=True) + NORM_EPS) * g


def _proj_kernel(x_ref, gpre_ref, w_ref, qn_ref, wuq_ref, kvn_ref, wk_ref, wv_ref,
                 rc_ref, rsa_ref, rsb_ref, eaug_ref,
                 qa_ref, qa2_ref, cmp_ref, slck_ref, slcv_ref, win_ref, qm_ref, km_ref, vm_ref,
                 krb_ref, cq_ref, ckv_ref):
    sub = x_ref.shape[0] // PROJ_SUBTILES
    rs = [slice(t * sub, (t + 1) * sub) for t in range(PROJ_SUBTILES)]
    hbs = [_rms(x_ref[r, :], gpre_ref[...]).astype(BF16) for r in rs]
    bounds = ((C_AQ, C_CMP), (C_CMP, C_WIN), (C_WIN, C_KRB), (C_KRB, C_Z))
    chunk_sets = [{ab: _dot(hb, w_ref[:, ab[0]:ab[1]]) for ab in bounds} for hb in hbs]
    lane = lax.broadcasted_iota(jnp.int32, (sub, LANES), 1)
    lo = lane < HEAD_DIM

    for r, chunks in zip(rs, chunk_sets):
        def proj(c0, c1):
            for (a, b), y in chunks.items():
                if a <= c0 and c1 <= b:
                    return y[:, c0 - a:c1 - a]
            raise AssertionError((c0, c1))

        qa_ref[r, :] = (proj(C_AQ, C_CMP) * 0.125).astype(BF16)
        qa2_ref[r, :] = (proj(C_AQ, C_CMP) * (0.125 * LOG2E)).astype(BF16)
        cmp_ref[r, :] = proj(C_CMP, C_SLC)
        slc = proj(C_SLC, C_WIN)
        slcv_ref[r, :] = jnp.where(lo, 1.0, slc).astype(BF16)
        slck_ref[r, :] = jnp.where(lo, slc, eaug_ref[r, :].astype(F32)).astype(BF16)
        win_ref[r, :] = proj(C_WIN, C_BCQ).astype(BF16)
        cq_ref[r, :] = (proj(C_CQ, C_CKV) * 0.125).astype(BF16)
        ckv_ref[r, :] = proj(C_CKV, C_Z).astype(BF16)

        rc, rsa, rsb = rc_ref[r, :], rsa_ref[r, :], rsb_ref[r, :]

        def rope(v):
            return v * rc + pltpu.roll(v, 16, 1) * rsa + pltpu.roll(v, 112, 1) * rsb

        cqn = _rms(proj(C_BCQ, C_BCKV), qn_ref[...]).astype(BF16)
        qm = _dot(cqn, wuq_ref[...])
        for h in range(MLA_HEADS):
            qh = rope(qm[:, LANES * h:LANES * (h + 1)]) * MLA_SCALE2
            qm_ref[r, LANES * h:LANES * (h + 1)] = qh.astype(BF16)

        krb = proj(C_KRB, C_CQ)
        krb_ref[r, :] = krb
        krr = jnp.where(lo | (lane >= HEAD_DIM + MLA_ROPE), 0.0, rope(krb))

        ckvn = _rms(proj(C_BCKV, C_KRB), kvn_ref[...]).astype(BF16)
        kn = _dot(ckvn, wk_ref[...])
        for h in range(MLA_HEADS):
            km_ref[r, LANES * h:LANES * (h + 1)] = (kn[:, LANES * h:LANES * (h + 1)] + krr).astype(BF16)
        vm = _dot(ckvn, wv_ref[...])
        for h in range(MLA_HEADS):
            vm_ref[r, LANES * h:LANES * (h + 1)] = jnp.where(lo, 1.0, vm[:, LANES * h:LANES * (h + 1)]).astype(BF16)


def _project(x2, gpre, w, qn, wuq, kvn, wk, wv, rope_c, rope_sa, rope_sb, eaug, T, tm):
    BT = x2.shape[0]
    nt = T // tm
    row = lambda i: (i, 0)
    fix = lambda i: (0, 0)
    pos = lambda i: (i % nt, 0)
    widths = [(256, BF16), (256, BF16), (128, F32), (128, BF16), (128, BF16), (128, BF16), (512, BF16), (512, BF16),
              (512, BF16), (128, F32), (512, BF16), (128, BF16)]
    return pl.pallas_call(
        _proj_kernel,
        grid=(BT // tm,),
        in_specs=[pl.BlockSpec((tm, D_MODEL), row), pl.BlockSpec((1, D_MODEL), fix),
                  pl.BlockSpec(w.shape, fix), pl.BlockSpec(qn.shape, fix), pl.BlockSpec(wuq.shape, fix),
                  pl.BlockSpec(kvn.shape, fix), pl.BlockSpec(wk.shape, fix), pl.BlockSpec(wv.shape, fix),
                  pl.BlockSpec((tm, LANES), pos), pl.BlockSpec((tm, LANES), pos), pl.BlockSpec((tm, LANES), pos),
                  pl.BlockSpec((tm, LANES), pos)],
        out_specs=[pl.BlockSpec((tm, wd), row) for wd, _ in widths],
        out_shape=[jax.ShapeDtypeStruct((BT, wd), dt) for wd, dt in widths],
        compiler_params=_cparams(("arbitrary",)),
        name="proj",
    )(x2, gpre, w, qn, wuq, kvn, wk, wv, rope_c, rope_sa, rope_sb, eaug)


def _compress_kernel(c_ref, pet_ref, peb_ref, w1t_ref, w1b_ref, w2_ref, o_ref):
    nch = c_ref.shape[1] // CMP_STRIDE
    c = jnp.concatenate([c_ref[0, pl.ds(j, nch, stride=CMP_STRIDE), :] for j in range(CMP_STRIDE)], axis=1)
    top = (c + pet_ref[...]).astype(BF16)
    bot = (c + peb_ref[...]).astype(BF16)
    out = jnp.zeros((nch, LANES), F32)
    for i in range(2):
        pre = _dot(top, w1t_ref[i]) + pltpu.roll(_dot(bot, w1b_ref[i]), nch - 1, 0)
        hid = pre * (1.0 / (1.0 + jnp.exp(-pre)))
        out = out + _dot(hid.astype(BF16), w2_ref[i])
    o_ref[0] = out.astype(BF16)


def _compress(tokens, pet, peb, w1t, w1b, w2):
    B, T, width = tokens.shape
    nch = T // CMP_STRIDE
    full = lambda a: pl.BlockSpec(a.shape, lambda b: (0,) * a.ndim)
    return pl.pallas_call(
        _compress_kernel,
        grid=(B,),
        in_specs=[pl.BlockSpec((1, T, width), lambda b: (b, 0, 0)),
                  full(pet), full(peb), full(w1t), full(w1b), full(w2)],
        out_specs=pl.BlockSpec((1, nch, LANES), lambda b: (b, 0, 0)),
        out_shape=jax.ShapeDtypeStruct((B, nch, LANES), BF16),
        compiler_params=_cparams(("arbitrary",)),
        name="compress",
    )(tokens, pet, peb, w1t, w1b, w2)


def _head_slabs(qblk, n_heads):
    q = qblk.astype(F32)
    lo = lax.broadcasted_iota(jnp.int32, (q.shape[0], LANES), 1) < HEAD_DIM
    out = []
    for j in range(n_heads // 2):
        slab = q[:, LANES * j:LANES * (j + 1)]
        out.append(jnp.where(lo, slab, 0.0))
        out.append(jnp.where(lo, pltpu.roll(slab, HEAD_DIM, 1), 0.0))
    return out


def _merge_upper(accs):
    lo = lax.broadcasted_iota(jnp.int32, accs[0].shape, 1) < HEAD_DIM
    slabs = [jnp.where(lo, pltpu.roll(accs[2 * j], HEAD_DIM, 1), accs[2 * j + 1]) for j in range(len(accs) // 2)]
    return jnp.concatenate(slabs, axis=1)


def _merge_normalized(accs):
    lo = lax.broadcasted_iota(jnp.int32, accs[0].shape, 1) < HEAD_DIM
    slabs = []
    for j in range(len(accs) // 2):
        a, b = accs[2 * j], accs[2 * j + 1]
        out = jnp.where(lo, pltpu.roll(a, HEAD_DIM, 1), b)
        den = jnp.where(lo, a, pltpu.roll(b, HEAD_DIM, 1))
        slabs.append(out / den)
    return jnp.concatenate(slabs, axis=1)


def _row_max_lanes(s):
    return jnp.broadcast_to(jnp.max(s, axis=-1, keepdims=True), (s.shape[0], LANES))


def _sub_lanes(s, ref):
    return jnp.concatenate([s[:, LANES * i:LANES * (i + 1)] - ref for i in range(s.shape[1] // LANES)], axis=1)


def _topk_mask_t(imp, t0):
    ns, tq = imp.shape
    srow = lax.broadcasted_iota(jnp.int32, (ns, tq), 0)
    cur = (t0 + lax.broadcasted_iota(jnp.int32, (ns, tq), 1)) // SEL_BLOCK
    forced = (srow == 0) | (srow == cur) | (srow == cur - 1)
    x = jnp.where(forced, BIG, jnp.where(srow <= cur, imp, -BIG))
    sub = 8
    groups = [x[sub * v:sub * (v + 1)] for v in range(ns // sub)]
    rows_in = [srow[sub * v:sub * (v + 1)] for v in range(ns // sub)]
    cnts = [jnp.zeros((sub, tq), F32) for _ in groups]
    for sp in range(ns):
        other = jnp.broadcast_to(x[sp:sp + 1, :], (sub, tq))
        for v, xv in enumerate(groups):
            if sub * v > sp:
                beats = other >= xv
            elif sub * v + sub - 1 < sp:
                beats = other > xv
            else:
                beats = (other > xv) | ((other == xv) & (rows_in[v] > sp))
            cnts[v] = cnts[v] + jnp.where(beats, 1.0, 0.0)
    cnt = jnp.concatenate(cnts, axis=0)
    sel = (cnt < float(min(SEL_TOP, ns))) & (srow <= cur)
    return jnp.where(sel, 0.0, NEG)


def _cmp_kernel(q_ref, kvc_ref, bc_ref, ovt_ref, ocmp_ref, selb_ref):
    tq = Q_TILE
    kvc = kvc_ref[0]
    ovt = ovt_ref[...]
    ns = ovt.shape[0]
    groups = range(CMP_GROUP)
    sls = [slice(g * tq, (g + 1) * tq) for g in groups]
    s_alls = []
    for g in groups:
        qs = _head_slabs(q_ref[0, sls[g], :], NSA_HEADS)
        s_alls.append(_dot_nt(jnp.concatenate([q.astype(BF16) for q in qs], axis=0), kvc))
    psums, o_alls = [], []
    for g in groups:
        valid = bc_ref[0, sls[g], :] > 0.5 * NEG
        psum = jnp.zeros(valid.shape, F32)
        ps = []
        for h in range(NSA_HEADS):
            s = s_alls[g][h * tq:(h + 1) * tq] + bc_ref[h, sls[g], :]
            m = jnp.max(s, axis=-1, keepdims=True)
            e = jnp.where(valid, jnp.exp(s - m), 0.0)
            l = jnp.sum(e, axis=-1, keepdims=True)
            p = e * (1.0 / jnp.where(l > 0.0, l, 1.0))
            psum = psum + p
            ps.append(p.astype(BF16))
        psums.append(psum)
        o_alls.append(_dot(jnp.concatenate(ps, axis=0), kvc))
    imps = []
    for g in groups:
        hi = psums[g].astype(BF16)
        lo = (psums[g] - hi.astype(F32)).astype(BF16)
        imps.append(_dot_nt(ovt, hi) + _dot_nt(ovt, lo))
    for g in groups:
        ocmp = _merge_upper([o_alls[g][h * tq:(h + 1) * tq] for h in range(NSA_HEADS)])
        ocmp_ref[0, sls[g], :] = ocmp.astype(ocmp_ref.dtype)
        sb = _topk_mask_t(imps[g], (pl.program_id(1) * CMP_GROUP + g) * tq)
        if ns < HEAD_DIM:
            sb = jnp.concatenate([sb, jnp.full((HEAD_DIM - ns, tq), NEG, F32)], axis=0)
        full = jnp.concatenate([jnp.zeros((HEAD_DIM, tq), F32), sb], axis=0)
        selb_ref[0, sls[g], :] = full.T.astype(BF16)


def _cmp_attention(qa, kvc, bc, ovt, T):
    B = qa.shape[0]
    ncp = kvc.shape[1]
    rows = CMP_GROUP * Q_TILE
    return pl.pallas_call(
        _cmp_kernel,
        grid=(B, T // rows),
        in_specs=[pl.BlockSpec((1, rows, 256), lambda b, n: (b, n, 0)),
                  pl.BlockSpec((1, ncp, LANES), lambda b, n: (b, 0, 0)),
                  pl.BlockSpec((NSA_HEADS, rows, ncp), lambda b, n: (0, n, 0)),
                  pl.BlockSpec(ovt.shape, lambda b, n: (0, 0))],
        out_specs=[pl.BlockSpec((1, rows, 256), lambda b, n: (b, n, 0)),
                   pl.BlockSpec((1, rows, LANES), lambda b, n: (b, n, 0))],
        out_shape=[jax.ShapeDtypeStruct((B, T, 256), BF16), jax.ShapeDtypeStruct((B, T, LANES), BF16)],
        compiler_params=_cparams(("arbitrary", "arbitrary")),
        name="cmp_select",
    )(qa, kvc, bc, ovt)


def _softmax_tile(s2, rows, m_s, shift=None):
    m_t = _row_max_lanes(s2)
    if shift is not None:
        m_t = m_t + shift
    m_old = m_s[rows]
    m_new = jnp.maximum(m_old, m_t)
    p = jnp.exp2(_sub_lanes(s2, m_new if shift is None else m_new - shift))
    alpha = jnp.exp2(m_old - m_new)
    m_s[rows] = m_new
    return p.astype(BF16), alpha


def _slc_kernel(b31_ref, q_ref, selb_ref, ka_ref, kv_ref, tab_ref, o_ref, m_s, acc_s):
    tq = q_ref.shape[1]
    H = NSA_HEADS
    n = pl.program_id(1)
    qs = _head_slabs(q_ref[0], H)
    selb = selb_ref[0].astype(F32)
    qst = jnp.concatenate([(qs[h] + selb).astype(BF16) for h in range(H)], axis=0)
    m_s[...] = jnp.full(m_s.shape, -jnp.inf, F32)
    acc_s[...] = jnp.zeros(acc_s.shape, F32)

    def scores(k0):
        k0 = pl.multiple_of(k0, tq)
        return _dot_nt(qst, ka_ref[0, pl.ds(k0, tq), :]), kv_ref[0, pl.ds(k0, tq), :]

    def update(s_all, kv, near_block):
        ps, alphas = [], []
        for h in range(H):
            rows = slice(h * tq, (h + 1) * tq)
            s2 = s_all[rows]
            if near_block is None:
                p, alpha = _softmax_tile(s2, rows, m_s, shift=b31_ref[h])
            else:
                s2 = s2 + tab_ref[h, near_block * tq:(near_block + 1) * tq, :]
                p, alpha = _softmax_tile(s2, rows, m_s)
            ps.append(p)
            alphas.append(alpha)
        pv = _dot(jnp.concatenate(ps, axis=0), kv)
        for h in range(H):
            rows = slice(h * tq, (h + 1) * tq)
            acc_s[rows] = alphas[h] * acc_s[rows] + pv[rows]

    def far_pair(i, carry):
        first, second = scores(2 * i * tq), scores((2 * i + 1) * tq)
        update(*first, None)
        update(*second, None)
        return carry

    n_far = jnp.maximum(n - 1, 0)
    lax.fori_loop(0, n_far // 2, far_pair, 0)

    @pl.when(n_far % 2 == 1)
    def _():
        update(*scores((n_far - 1) * tq), None)

    @pl.when(n >= 1)
    def _():
        update(*scores(jnp.maximum(n - 1, 0) * tq), 1)

    update(*scores(n * tq), 0)
    o_ref[0] = _merge_normalized([acc_s[h * tq:(h + 1) * tq] for h in range(H)]).astype(o_ref.dtype)


def _slc_attention(b31, qa, selb, slck, slcv, tab, T):
    B = qa.shape[0]
    tq = FLASH_TILE
    H = NSA_HEADS
    tile = lambda w: pl.BlockSpec((1, tq, w), lambda b, n: (b, n, 0))
    return pl.pallas_call(
        _slc_kernel,
        grid=(B, T // tq),
        in_specs=[pl.BlockSpec(memory_space=pltpu.SMEM),
                  tile(256), tile(LANES),
                  pl.BlockSpec((1, T, LANES), lambda b, n: (b, 0, 0)),
                  pl.BlockSpec((1, T, LANES), lambda b, n: (b, 0, 0)),
                  pl.BlockSpec(tab.shape, lambda b, n: (0, 0, 0))],
        out_specs=tile(256),
        out_shape=jax.ShapeDtypeStruct((B, T, 256), BF16),
        scratch_shapes=[pltpu.VMEM((H * tq, LANES), F32), pltpu.VMEM((H * tq, LANES), F32)],
        compiler_params=_cparams(("arbitrary", "arbitrary")),
        name="slc_attention",
    )(b31, qa, selb, slck, slcv, tab)


def _band_kernel(n_heads, window, has_sink, n_sub, *refs):
    if has_sink:
        q_ref, kv_ref, tab_ref, sink_ref, o_ref = refs
    else:
        q_ref, kv_ref, tab_ref, o_ref = refs
    tq = Q_TILE
    span = window + tq
    lo_kv = lax.broadcasted_iota(jnp.int32, (span, LANES), 1) < HEAD_DIM
    lo_o = lax.broadcasted_iota(jnp.int32, (n_heads * tq, LANES), 1) < HEAD_DIM
    groups = range(n_sub)
    tiles = [pl.program_id(1) * n_sub + g for g in groups]
    kvs, dots = [], []
    for g in groups:
        k0 = pl.multiple_of(jnp.maximum(tiles[g] * tq - window, 0), tq)
        kvs.append(kv_ref[0, pl.ds(k0, span), :])
        qs = _head_slabs(q_ref[0, g * tq:(g + 1) * tq, :], n_heads)
        dots.append(_dot_nt(jnp.concatenate([q.astype(BF16) for q in qs], axis=0), kvs[g]))
    accs, ms = [], []
    for g in groups:
        s = dots[g] + tab_ref[jnp.minimum(tiles[g], window // tq)]
        m = _row_max_lanes(s)
        if has_sink:
            m = jnp.maximum(m, sink_ref[...])
        e = jnp.exp(_sub_lanes(s, m))
        ms.append(m)
        accs.append(_dot(e.astype(BF16), jnp.where(lo_kv, 1.0, kvs[g]).astype(BF16)))
    for g in groups:
        acc = accs[g]
        if has_sink:
            acc = acc + jnp.where(lo_o, jnp.exp(sink_ref[...] - ms[g]), 0.0)
        out = _merge_normalized([acc[h * tq:(h + 1) * tq] for h in range(n_heads)])
        o_ref[0, g * tq:(g + 1) * tq, :] = out.astype(o_ref.dtype)


def _band_attention(q, kv_pad, tab, sinks, n_heads, window, T, out_dtype):
    B = q.shape[0]
    width = n_heads * HEAD_DIM
    has_sink = sinks is not None
    nvar = tab.shape[1]
    tab2 = jnp.transpose(tab, (1, 0, 2, 3)).reshape(nvar, n_heads * Q_TILE, window + Q_TILE)
    n_sub = BAND_ROWS // (n_heads * Q_TILE)
    rows = n_sub * Q_TILE
    in_specs = [pl.BlockSpec((1, rows, width), lambda b, n: (b, n, 0)),
                pl.BlockSpec((1, T, LANES), lambda b, n: (b, 0, 0)),
                pl.BlockSpec(tab2.shape, lambda b, n: (0, 0, 0))]
    args = (q, kv_pad, tab2)
    if has_sink:
        rep = jnp.broadcast_to(jnp.repeat(sinks, Q_TILE)[:, None], (n_heads * Q_TILE, LANES))
        in_specs.append(pl.BlockSpec(rep.shape, lambda b, n: (0, 0)))
        args = args + (rep,)
    return pl.pallas_call(
        functools.partial(_band_kernel, n_heads, window, has_sink, n_sub),
        grid=(B, T // rows),
        in_specs=in_specs,
        out_specs=pl.BlockSpec((1, rows, width), lambda b, n: (b, n, 0)),
        out_shape=jax.ShapeDtypeStruct((B, T, width), out_dtype),
        compiler_params=_cparams(("arbitrary", "arbitrary")),
        name="band_sink" if has_sink else "band_window",
    )(*args)


def _mla_kernel(q_ref, k_ref, v_ref, o_ref, m_s, acc_s):
    tq = q_ref.shape[1]
    H = MLA_HEADS
    n = pl.program_id(1)
    m_s[...] = jnp.full(m_s.shape, -jnp.inf, F32)
    acc_s[...] = jnp.zeros(acc_s.shape, F32)

    def scores(k0):
        k0 = pl.multiple_of(k0, tq)
        return k0, [_dot_nt(q_ref[0, :, LANES * h:LANES * (h + 1)], k_ref[0, pl.ds(k0, tq), LANES * h:LANES * (h + 1)])
                    for h in range(H)]

    def update(k0, ss, diag):
        ps, alphas = [], []
        for h in range(H):
            rows = slice(h * tq, (h + 1) * tq)
            s2 = ss[h]
            if diag:
                row = lax.broadcasted_iota(jnp.int32, (tq, tq), 0)
                col = lax.broadcasted_iota(jnp.int32, (tq, tq), 1)
                s2 = jnp.where(col <= row, s2, NEG)
            p, alpha = _softmax_tile(s2, rows, m_s)
            ps.append(p)
            alphas.append(alpha)
        for h in range(H):
            rows = slice(h * tq, (h + 1) * tq)
            v = v_ref[0, pl.ds(k0, tq), LANES * h:LANES * (h + 1)]
            acc_s[rows] = alphas[h] * acc_s[rows] + _dot(ps[h], v)

    def far_pair(i, carry):
        first, second = scores(2 * i * tq), scores((2 * i + 1) * tq)
        update(*first, False)
        update(*second, False)
        return carry

    lax.fori_loop(0, n // 2, far_pair, 0)

    @pl.when(n % 2 == 1)
    def _():
        update(*scores((n - 1) * tq), False)

    update(*scores(n * tq), True)
    o_ref[0] = _merge_normalized([acc_s[h * tq:(h + 1) * tq] for h in range(H)]).astype(o_ref.dtype)


def _mla_attention(qm, km, vm, T):
    B = qm.shape[0]
    tq = FLASH_TILE
    H = MLA_HEADS
    return pl.pallas_call(
        _mla_kernel,
        grid=(B, T // tq),
        in_specs=[pl.BlockSpec((1, tq, 512), lambda b, n: (b, n, 0)),
                  pl.BlockSpec((1, T, 512), lambda b, n: (b, 0, 0)),
                  pl.BlockSpec((1, T, 512), lambda b, n: (b, 0, 0))],
        out_specs=pl.BlockSpec((1, tq, 256), lambda b, n: (b, n, 0)),
        out_shape=jax.ShapeDtypeStruct((B, T, 256), BF16),
        scratch_shapes=[pltpu.VMEM((H * tq, LANES), F32), pltpu.VMEM((H * tq, LANES), F32)],
        compiler_params=_cparams(("arbitrary", "arbitrary")),
        name="mla_attention",
    )(qm, km, vm)


def _out_kernel(x_ref, gpre_ref, wz_ref, ocmp_ref, oslc_ref, owin_ref, krb_ref, gexp_ref, ob_ref, oc_ref, w_ref,
                gpost_ref, o_ref):
    sub = x_ref.shape[0] // OUT_SUBTILES
    rs = [slice(t * sub, (t + 1) * sub) for t in range(OUT_SUBTILES)]
    w_a = NSA_HEADS * HEAD_DIM
    hs = [_rms(x_ref[r, :], gpre_ref[...]).astype(BF16) for r in rs]
    zs = [_dot(h, wz_ref[...]) for h in hs]
    mixeds = []
    for r, z in zip(rs, zs):
        sig = 1.0 / (1.0 + jnp.exp(-krb_ref[r, :]))
        hi = sig.astype(BF16)
        lo = (sig - hi.astype(F32)).astype(BF16)
        gates = _dot(hi, gexp_ref[...]) + _dot(lo, gexp_ref[...])
        oa = None
        for j, br in enumerate((ocmp_ref, oslc_ref, owin_ref)):
            term = gates[:, w_a * j:w_a * (j + 1)] * br[r, :].astype(F32)
            oa = term if oa is None else oa + term
        heads = jnp.concatenate([oa, ob_ref[r, :].astype(F32), oc_ref[r, :].astype(F32)], axis=1)
        mixeds.append((heads * (z * (1.0 / (1.0 + jnp.exp(-z))))).astype(BF16))
    ys = [_dot(m, w_ref[...]) for m in mixeds]
    for r, y in zip(rs, ys):
        o_ref[r, :] = x_ref[r, :] + _rms(y, gpost_ref[...])


def _out_project(x2, gpre, wz, ocmp, oslc, owin, krb, gexp, ob, oc, w, gpost, tm):
    BT = x2.shape[0]
    row = lambda i: (i, 0)
    fix = lambda i: (0, 0)
    spec = lambda a: pl.BlockSpec((tm, a.shape[1]), row)
    full = lambda a: pl.BlockSpec(a.shape, fix)
    return pl.pallas_call(
        _out_kernel,
        grid=(BT // tm,),
        in_specs=[spec(x2), full(gpre), full(wz), spec(ocmp), spec(oslc), spec(owin), spec(krb), full(gexp),
                  spec(ob), spec(oc), full(w), full(gpost)],
        out_specs=pl.BlockSpec((tm, D_MODEL), row),
        out_shape=jax.ShapeDtypeStruct((BT, D_MODEL), F32),
        compiler_params=_cparams(("arbitrary",)),
        name="out_proj",
    )(x2, gpre, wz, ocmp, oslc, owin, krb, gexp, ob, oc, w, gpost)


def _split_in(w):
    outs, o = [], 0
    for s in IN_SIZES:
        outs.append(w[..., o:o + s])
        o += s
    return outs


def _relayout_kernel(w_ref, wp_ref, wz_ref):
    a_q, a_kv, a_g, b_cq, b_ckv, b_kr, c_q, c_kv, z = _split_in(w_ref[0])
    zeros = lambda n: jnp.zeros((a_q.shape[0], n), F32)
    krb = jnp.concatenate([zeros(HEAD_DIM), b_kr, a_g, zeros(LANES - HEAD_DIM - MLA_ROPE - 3 * NSA_HEADS)], axis=-1)
    wp_ref[0] = jnp.concatenate([a_q, a_kv, b_cq, b_ckv, krb, c_q, c_kv], axis=-1).astype(BF16)
    wz_ref[0] = z.astype(BF16)


def _prep_w_in(w):
    L, d, n = w.shape
    rt = 256
    return pl.pallas_call(
        _relayout_kernel,
        grid=(L, d // rt),
        in_specs=[pl.BlockSpec((1, rt, n), lambda l, r: (l, r, 0))],
        out_specs=[pl.BlockSpec((1, rt, C_Z), lambda l, r: (l, r, 0)),
                   pl.BlockSpec((1, rt, IN_SIZES[-1]), lambda l, r: (l, r, 0))],
        out_shape=[jax.ShapeDtypeStruct((L, d, C_Z), BF16), jax.ShapeDtypeStruct((L, d, IN_SIZES[-1]), BF16)],
        compiler_params=_cparams(("arbitrary", "arbitrary")),
        name="w_in_relayout",
    )(w)


def _prep_mla(w_uq, w_ukv):
    L = w_uq.shape[0]
    dq = MLA_NOPE + MLA_ROPE
    uq = w_uq.reshape(L, MLA_Q_RANK, MLA_HEADS, dq)
    uq = jnp.concatenate([uq, jnp.zeros((L, MLA_Q_RANK, MLA_HEADS, LANES - dq), uq.dtype)], axis=-1)
    ukv = w_ukv.reshape(L, MLA_KV_RANK, MLA_HEADS, MLA_NOPE + MLA_V)
    uk = jnp.concatenate([ukv[..., :MLA_NOPE], jnp.zeros((L, MLA_KV_RANK, MLA_HEADS, LANES - MLA_NOPE), ukv.dtype)], axis=-1)
    uv = jnp.concatenate([jnp.zeros((L, MLA_KV_RANK, MLA_HEADS, LANES - MLA_V), ukv.dtype), ukv[..., MLA_NOPE:]], axis=-1)
    return (uq.reshape(L, MLA_Q_RANK, MLA_HEADS * LANES).astype(BF16),
            uk.reshape(L, MLA_KV_RANK, MLA_HEADS * LANES).astype(BF16),
            uv.reshape(L, MLA_KV_RANK, MLA_HEADS * LANES).astype(BF16))


def _prep_compress(cmp_pos, cmp_w1, cmp_w2):
    L = cmp_pos.shape[0]
    pe = jnp.concatenate([cmp_pos[:, 0], cmp_pos[:, 1]], axis=-1)
    pet = pe[:, :CMP_STRIDE].reshape(L, 1, CMP_STRIDE * LANES)
    peb = pe[:, CMP_STRIDE:].reshape(L, 1, CMP_STRIDE * LANES)
    w1 = cmp_w1.reshape(L, 2, CMP_LEN, HEAD_DIM, CMP_HIDDEN)
    zero = jnp.zeros_like(w1[:, 0])
    w1k = jnp.concatenate([w1[:, 0], zero], axis=2)
    w1v = jnp.concatenate([zero, w1[:, 1]], axis=2)
    w1e = jnp.stack([w1k, w1v], axis=1)
    w1t = w1e[:, :, :CMP_STRIDE].reshape(L, 2, CMP_STRIDE * LANES, CMP_HIDDEN).astype(BF16)
    w1b = w1e[:, :, CMP_STRIDE:].reshape(L, 2, CMP_STRIDE * LANES, CMP_HIDDEN).astype(BF16)
    z2 = jnp.zeros_like(cmp_w2[:, 0])
    w2 = jnp.stack([jnp.concatenate([cmp_w2[:, 0], z2], axis=-1),
                    jnp.concatenate([z2, cmp_w2[:, 1]], axis=-1)], axis=1).astype(BF16)
    return pet, peb, w1t, w1b, w2


def _rope_tables(T):
    half = MLA_ROPE // 2
    inv = ROPE_THETA ** (-jnp.arange(half, dtype=F32) / half)
    ang = jnp.arange(T).astype(F32)[:, None] * inv[None, :]
    cos, sin = jnp.cos(ang), jnp.sin(ang)
    z = lambda n: jnp.zeros((T, n), F32)
    rc = jnp.concatenate([jnp.ones((T, MLA_NOPE), F32), cos, cos, z(LANES - MLA_NOPE - MLA_ROPE)], axis=1)
    rsa = jnp.concatenate([z(MLA_NOPE + half), sin, z(LANES - MLA_NOPE - MLA_ROPE)], axis=1)
    rsb = jnp.concatenate([z(MLA_NOPE), -sin, z(LANES - MLA_NOPE - half)], axis=1)
    return rc, rsa, rsb


def _static_tables(T):
    ns = T // SEL_BLOCK
    nc = T // CMP_STRIDE - 1
    ncp = T // CMP_STRIDE
    sstart = np.arange(ns) * SEL_BLOCK
    cstart = np.arange(nc) * CMP_STRIDE
    overlap = (np.clip(np.minimum(cstart[:, None] + CMP_LEN, sstart[None, :] + SEL_BLOCK)
                       - np.maximum(cstart[:, None], sstart[None, :]), 0, None) / CMP_STRIDE).astype(np.float32)
    ovt = np.zeros((ns, ncp), np.float32)
    ovt[:, :nc] = overlap.T
    eaug = np.zeros((T, LANES), np.float32)
    eaug[np.arange(T), HEAD_DIM + np.arange(T) // SEL_BLOCK] = 1.0
    gexp = np.zeros((LANES, 3 * NSA_HEADS * HEAD_DIM), np.float32)
    for h in range(NSA_HEADS):
        for j in range(3):
            c0 = NSA_HEADS * HEAD_DIM * j + HEAD_DIM * h
            gexp[GATE_LANE0 + 3 * h + j, c0:c0 + HEAD_DIM] = 1.0
    return jnp.asarray(ovt, BF16), jnp.asarray(eaug, BF16), jnp.asarray(gexp, BF16)


def kernel(x, w_in, w_out, norm_pre, norm_post, cmp_pos, cmp_w1, cmp_w2, mla_q_norm, mla_w_uq, mla_kv_norm,
           mla_w_ukv, swa_sinks, rel_bias):
    B, T, D = x.shape
    depth = w_in.shape[0]
    assert D == D_MODEL and T % (2 * FLASH_TILE) == 0 and T // SEL_BLOCK <= HEAD_DIM
    ncp = T // CMP_STRIDE
    tm = 512

    w_in_p, w_z = _prep_w_in(w_in)
    wuq, wk, wv = _prep_mla(mla_w_uq, mla_w_ukv)
    pet, peb, w1t, w1b, w2 = _prep_compress(cmp_pos, cmp_w1, cmp_w2)
    w_out_b = w_out.astype(BF16)
    rope_c, rope_sa, rope_sb = _rope_tables(T)
    ovt, eaug, gexp = _static_tables(T)

    bc = _bias_table(rel_bias, 0, NSA_HEADS, T, ncp, -CMP_STRIDE, -(CMP_LEN - 1), 0, 1 << 30, 256).reshape(NSA_HEADS, T, ncp)
    tab_near = _bias_table(rel_bias, 0, NSA_HEADS, FLASH_TILE, FLASH_TILE, -1, 0, 0, 1 << 30, 256, mult=LOG2E,
                           nblk=2, blk_d0=FLASH_TILE).reshape(NSA_HEADS, 2 * FLASH_TILE, FLASH_TILE)
    tab_win = _bias_table(rel_bias, 0, NSA_HEADS, Q_TILE, NSA_WINDOW + Q_TILE, -1, 0, 0, NSA_WINDOW, Q_TILE,
                          nblk=NSA_WINDOW // Q_TILE + 1, blk_d0=Q_TILE)
    tab_swa = _bias_table(rel_bias, NSA_HEADS, SWA_HEADS, Q_TILE, SWA_WINDOW + Q_TILE, -1, 0, 0, SWA_WINDOW, Q_TILE,
                          nblk=SWA_WINDOW // Q_TILE + 1, blk_d0=Q_TILE)
    b31 = rel_bias[:NSA_HEADS, REL_BUCKETS - 1] * LOG2E

    x2 = x.reshape(B * T, D)
    for l in range(depth):
        (qa, qa2, cmpkv, slck, slcv, win, qm, km, vm, krb, cq, ckv) = _project(
            x2, norm_pre[l][None], w_in_p[l], mla_q_norm[l][None], wuq[l], mla_kv_norm[l][None], wk[l], wv[l],
            rope_c, rope_sa, rope_sb, eaug, T, PROJ_SUBTILES * tm)
        r3 = lambda a: a.reshape(B, T, a.shape[-1])
        kvc = _compress(r3(cmpkv), pet[l], peb[l], w1t[l], w1b[l], w2[l])
        qa3 = r3(qa)
        ocmp, selb = _cmp_attention(qa3, kvc, bc, ovt, T)
        owin = _band_attention(qa3, r3(win), tab_win, None, NSA_HEADS, NSA_WINDOW, T, BF16)
        oslc = _slc_attention(b31, r3(qa2), selb, r3(slck), r3(slcv), tab_near, T)
        oc = _band_attention(r3(cq), r3(ckv), tab_swa, swa_sinks[l], SWA_HEADS, SWA_WINDOW, T, BF16)
        ob = _mla_attention(r3(qm), r3(km), r3(vm), T)
        flat = lambda a: a.reshape(B * T, a.shape[-1])
        x2 = _out_project(x2, norm_pre[l][None], w_z[l], flat(ocmp), flat(oslc), flat(owin), krb, gexp,
                          flat(ob), flat(oc), w_out_b[l], norm_post[l][None], OUT_ROWS)
    return x2.reshape(B, T, D)
```

```python
import functools
import math

import numpy as np
import jax
import jax.numpy as jnp
from jax import lax
from jax.experimental import pallas as pl
from jax.experimental.pallas import tpu as pltpu

F32 = jnp.float32
BF16 = jnp.bfloat16

D_MODEL = 1024
HEAD_DIM = 64
NSA_HEADS = 4
CMP_LEN = 32
CMP_STRIDE = 16
CMP_HIDDEN = 128
SEL_BLOCK = 64
SEL_TOP = 16
NSA_WINDOW = 512
MLA_HEADS = 4
MLA_Q_RANK = 256
MLA_KV_RANK = 128
MLA_NOPE = 64
MLA_ROPE = 32
MLA_V = 64
ROPE_THETA = 10000.0
SWA_HEADS = 8
SWA_WINDOW = 128
REL_BUCKETS = 32
REL_MAX_DIST = 512
NORM_EPS = 1e-6
NEG = -1e30
BIG = 1e9
LOG2E = math.log2(math.e)
MLA_SCALE2 = (MLA_NOPE + MLA_ROPE) ** -0.5 * LOG2E
IN_SIZES = (256, 384, 12, 256, 128, 32, 512, 128, 1024)

LANES = 128
Q_TILE = 128
BAND_ROWS = 4096
PROJ_SUBTILES = 1
OUT_SUBTILES = 4
OUT_ROWS = 1024
CMP_GROUP = 4
FLASH_TILE = 512
GATE_LANE0 = 96
VMEM_LIMIT = 56 * 1024 * 1024

C_AQ, C_CMP, C_SLC, C_WIN, C_BCQ, C_BCKV, C_KRB, C_CQ, C_CKV, C_Z = (
    0, 256, 384, 512, 640, 896, 1024, 1152, 1664, 1792)


def _dot(a, b):
    return jnp.dot(a, b, preferred_element_type=F32)


def _dot_nt(a, b):
    return lax.dot_general(a, b, (((1,), (1,)), ((), ())), preferred_element_type=F32)


def _bucket_thresholds():
    d = np.arange(0, 4 * REL_MAX_DIST)
    exact = REL_BUCKETS // 2
    large = exact + (np.log(np.maximum(d, 1).astype(np.float32) / np.float32(exact))
                     / np.float32(math.log(REL_MAX_DIST / exact)) * np.float32(REL_BUCKETS - exact)).astype(np.int32)
    b = np.where(d < exact, d, np.minimum(large, REL_BUCKETS - 1))
    assert np.all(np.diff(b) >= 0) and b[-1] == REL_BUCKETS - 1
    return [int(np.argmax(b >= k)) for k in range(REL_BUCKETS)]


_THRESH = _bucket_thresholds()


def _cparams(sem):
    return pltpu.CompilerParams(dimension_semantics=sem, vmem_limit_bytes=VMEM_LIMIT)


def _table_kernel(h0, a_col, d0, blk_d0, lo, hi, mult, rb_ref, out_ref):
    h = pl.program_id(0) + h0
    blk = pl.program_id(1)
    rt, cc = out_ref.shape[2], out_ref.shape[3]
    r = pl.program_id(2) * rt + lax.broadcasted_iota(jnp.int32, (rt, cc), 0)
    c = lax.broadcasted_iota(jnp.int32, (rt, cc), 1)
    dist = r + a_col * c + d0 + blk * blk_d0
    acc = jnp.full((rt, cc), rb_ref[h, 0], F32)
    for k in range(1, REL_BUCKETS):
        acc = jnp.where(dist >= _THRESH[k], rb_ref[h, k], acc)
    ok = (dist >= lo) & (dist < hi)
    out_ref[0, 0] = jnp.where(ok, acc * mult, NEG)


def _bias_table(rel_bias, h0, nh, rows, cols, a_col, d0, lo, hi, rt, mult=1.0, nblk=1, blk_d0=0):
    return pl.pallas_call(
        functools.partial(_table_kernel, h0, a_col, d0, blk_d0, lo, hi, mult),
        grid=(nh, nblk, rows // rt),
        in_specs=[pl.BlockSpec(memory_space=pltpu.SMEM)],
        out_specs=pl.BlockSpec((1, 1, rt, cols), lambda h, b, r: (h, b, r, 0)),
        out_shape=jax.ShapeDtypeStruct((nh, nblk, rows, cols), F32),
        compiler_params=_cparams(("arbitrary", "arbitrary", "arbitrary")),
        name="bias_table",
    )(rel_bias)


def _rms(v, g):
    return v * lax.rsqrt(jnp.mean(v * v, axis=-1, keepdims=True) + NORM_EPS) * g


def _proj_kernel(x_ref, gpre_ref, w_ref, qn_ref, wuq_ref, kvn_ref, wk_ref, wv_ref,
                 rc_ref, rsa_ref, rsb_ref, eaug_ref,
                 qa_ref, qa2_ref, cmp_ref, slck_ref, slcv_ref, win_ref, qm_ref, km_ref, vm_ref,
                 krb_ref, cq_ref, ckv_ref):
    sub = x_ref.shape[0] // PROJ_SUBTILES
    rs = [slice(t * sub, (t + 1) * sub) for t in range(PROJ_SUBTILES)]
    hbs = [_rms(x_ref[r, :], gpre_ref[...]).astype(BF16) for r in rs]
    bounds = ((C_AQ, C_CMP), (C_CMP, C_WIN), (C_WIN, C_KRB), (C_KRB, C_Z))
    chunk_sets = [{ab: _dot(hb, w_ref[:, ab[0]:ab[1]]) for ab in bounds} for hb in hbs]
    lane = lax.broadcasted_iota(jnp.int32, (sub, LANES), 1)
    lo = lane < HEAD_DIM

    for r, chunks in zip(rs, chunk_sets):
        def proj(c0, c1):
            for (a, b), y in chunks.items():
                if a <= c0 and c1 <= b:
                    return y[:, c0 - a:c1 - a]
            raise AssertionError((c0, c1))

        qa_ref[r, :] = (proj(C_AQ, C_CMP) * 0.125).astype(BF16)
        qa2_ref[r, :] = (proj(C_AQ, C_CMP) * (0.125 * LOG2E)).astype(BF16)
        cmp_ref[r, :] = proj(C_CMP, C_SLC)
        slc = proj(C_SLC, C_WIN)
        slcv_ref[r, :] = jnp.where(lo, 1.0, slc).astype(BF16)
        slck_ref[r, :] = jnp.where(lo, slc, eaug_ref[r, :].astype(F32)).astype(BF16)
        win_ref[r, :] = proj(C_WIN, C_BCQ).astype(BF16)
        cq_ref[r, :] = (proj(C_CQ, C_CKV) * 0.125).astype(BF16)
        ckv_ref[r, :] = proj(C_CKV, C_Z).astype(BF16)

        rc, rsa, rsb = rc_ref[r, :], rsa_ref[r, :], rsb_ref[r, :]

        def rope(v):
            return v * rc + pltpu.roll(v, 16, 1) * rsa + pltpu.roll(v, 112, 1) * rsb

        cqn = _rms(proj(C_BCQ, C_BCKV), qn_ref[...]).astype(BF16)
        qm = _dot(cqn, wuq_ref[...])
        for h in range(MLA_HEADS):
            qh = rope(qm[:, LANES * h:LANES * (h + 1)]) * MLA_SCALE2
            qm_ref[r, LANES * h:LANES * (h + 1)] = qh.astype(BF16)

        krb = proj(C_KRB, C_CQ)
        krb_ref[r, :] = krb
        krr = jnp.where(lo | (lane >= HEAD_DIM + MLA_ROPE), 0.0, rope(krb))

        ckvn = _rms(proj(C_BCKV, C_KRB), kvn_ref[...]).astype(BF16)
        kn = _dot(ckvn, wk_ref[...])
        for h in range(MLA_HEADS):
            km_ref[r, LANES * h:LANES * (h + 1)] = (kn[:, LANES * h:LANES * (h + 1)] + krr).astype(BF16)
        vm = _dot(ckvn, wv_ref[...])
        for h in range(MLA_HEADS):
            vm_ref[r, LANES * h:LANES * (h + 1)] = jnp.where(lo, 1.0, vm[:, LANES * h:LANES * (h + 1)]).astype(BF16)


def _project(x2, gpre, w, qn, wuq, kvn, wk, wv, rope_c, rope_sa, rope_sb, eaug, T, tm):
    BT = x2.shape[0]
    nt = T // tm
    row = lambda i: (i, 0)
    fix = lambda i: (0, 0)
    pos = lambda i: (i % nt, 0)
    widths = [(256, BF16), (256, BF16), (128, F32), (128, BF16), (128, BF16), (128, BF16), (512, BF16), (512, BF16),
              (512, BF16), (128, F32), (512, BF16), (128, BF16)]
    return pl.pallas_call(
        _proj_kernel,
        grid=(BT // tm,),
        in_specs=[pl.BlockSpec((tm, D_MODEL), row), pl.BlockSpec((1, D_MODEL), fix),
                  pl.BlockSpec(w.shape, fix), pl.BlockSpec(qn.shape, fix), pl.BlockSpec(wuq.shape, fix),
                  pl.BlockSpec(kvn.shape, fix), pl.BlockSpec(wk.shape, fix), pl.BlockSpec(wv.shape, fix),
                  pl.BlockSpec((tm, LANES), pos), pl.BlockSpec((tm, LANES), pos), pl.BlockSpec((tm, LANES), pos),
                  pl.BlockSpec((tm, LANES), pos)],
        out_specs=[pl.BlockSpec((tm, wd), row) for wd, _ in widths],
        out_shape=[jax.ShapeDtypeStruct((BT, wd), dt) for wd, dt in widths],
        compiler_params=_cparams(("arbitrary",)),
        name="proj",
    )(x2, gpre, w, qn, wuq, kvn, wk, wv, rope_c, rope_sa, rope_sb, eaug)


def _compress_kernel(c_ref, pet_ref, peb_ref, w1t_ref, w1b_ref, w2_ref, o_ref):
    nch = c_ref.shape[1] // CMP_STRIDE
    c = jnp.concatenate([c_ref[0, pl.ds(j, nch, stride=CMP_STRIDE), :] for j in range(CMP_STRIDE)], axis=1)
    top = (c + pet_ref[...]).astype(BF16)
    bot = (c + peb_ref[...]).astype(BF16)
    out = jnp.zeros((nch, LANES), F32)
    for i in range(2):
        pre = _dot(top, w1t_ref[i]) + pltpu.roll(_dot(bot, w1b_ref[i]), nch - 1, 0)
        hid = pre * (1.0 / (1.0 + jnp.exp(-pre)))
        out = out + _dot(hid.astype(BF16), w2_ref[i])
    o_ref[0] = out.astype(BF16)


def _compress(tokens, pet, peb, w1t, w1b, w2):
    B, T, width = tokens.shape
    nch = T // CMP_STRIDE
    full = lambda a: pl.BlockSpec(a.shape, lambda b: (0,) * a.ndim)
    return pl.pallas_call(
        _compress_kernel,
        grid=(B,),
        in_specs=[pl.BlockSpec((1, T, width), lambda b: (b, 0, 0)),
                  full(pet), full(peb), full(w1t), full(w1b), full(w2)],
        out_specs=pl.BlockSpec((1, nch, LANES), lambda b: (b, 0, 0)),
        out_shape=jax.ShapeDtypeStruct((B, nch, LANES), BF16),
        compiler_params=_cparams(("arbitrary",)),
        name="compress",
    )(tokens, pet, peb, w1t, w1b, w2)


def _head_slabs(qblk, n_heads):
    q = qblk.astype(F32)
    lo = lax.broadcasted_iota(jnp.int32, (q.shape[0], LANES), 1) < HEAD_DIM
    out = []
    for j in range(n_heads // 2):
        slab = q[:, LANES * j:LANES * (j + 1)]
        out.append(jnp.where(lo, slab, 0.0))
        out.append(jnp.where(lo, pltpu.roll(slab, HEAD_DIM, 1), 0.0))
    return out


def _merge_upper(accs):
    lo = lax.broadcasted_iota(jnp.int32, accs[0].shape, 1) < HEAD_DIM
    slabs = [jnp.where(lo, pltpu.roll(accs[2 * j], HEAD_DIM, 1), accs[2 * j + 1]) for j in range(len(accs) // 2)]
    return jnp.concatenate(slabs, axis=1)


def _merge_normalized(accs):
    lo = lax.broadcasted_iota(jnp.int32, accs[0].shape, 1) < HEAD_DIM
    slabs = []
    for j in range(len(accs) // 2):
        a, b = accs[2 * j], accs[2 * j + 1]
        out = jnp.where(lo, pltpu.roll(a, HEAD_DIM, 1), b)
        den = jnp.where(lo, a, pltpu.roll(b, HEAD_DIM, 1))
        slabs.append(out / den)
    return jnp.concatenate(slabs, axis=1)


def _row_max_lanes(s):
    return jnp.broadcast_to(jnp.max(s, axis=-1, keepdims=True), (s.shape[0], LANES))


def _sub_lanes(s, ref):
    return jnp.concatenate([s[:, LANES * i:LANES * (i + 1)] - ref for i in range(s.shape[1] // LANES)], axis=1)


def _topk_mask_t(imp, t0):
    ns, tq = imp.shape
    srow = lax.broadcasted_iota(jnp.int32, (ns, tq), 0)
    cur = (t0 + lax.broadcasted_iota(jnp.int32, (ns, tq), 1)) // SEL_BLOCK
    forced = (srow == 0) | (srow == cur) | (srow == cur - 1)
    x = jnp.where(forced, BIG, jnp.where(srow <= cur, imp, -BIG))
    sub = 8
    groups = [x[sub * v:sub * (v + 1)] for v in range(ns // sub)]
    rows_in = [srow[sub * v:sub * (v + 1)] for v in range(ns // sub)]
    cnts = [jnp.zeros((sub, tq), F32) for _ in groups]
    for sp in range(ns):
        other = jnp.broadcast_to(x[sp:sp + 1, :], (sub, tq))
        for v, xv in enumerate(groups):
            if sub * v > sp:
                beats = other >= xv
            elif sub * v + sub - 1 < sp:
                beats = other > xv
            else:
                beats = (other > xv) | ((other == xv) & (rows_in[v] > sp))
            cnts[v] = cnts[v] + jnp.where(beats, 1.0, 0.0)
    cnt = jnp.concatenate(cnts, axis=0)
    sel = (cnt < float(min(SEL_TOP, ns))) & (srow <= cur)
    return jnp.where(sel, 0.0, NEG)


def _cmp_kernel(q_ref, kvc_ref, bc_ref, ovt_ref, ocmp_ref, selb_ref):
    tq = Q_TILE
    kvc = kvc_ref[0]
    ovt = ovt_ref[...]
    ns = ovt.shape[0]
    groups = range(CMP_GROUP)
    sls = [slice(g * tq, (g + 1) * tq) for g in groups]
    s_alls = []
    for g in groups:
        qs = _head_slabs(q_ref[0, sls[g], :], NSA_HEADS)
        s_alls.append(_dot_nt(jnp.concatenate([q.astype(BF16) for q in qs], axis=0), kvc))
    psums, o_alls = [], []
    for g in groups:
        valid = bc_ref[0, sls[g], :] > 0.5 * NEG
        psum = jnp.zeros(valid.shape, F32)
        ps = []
        for h in range(NSA_HEADS):
            s = s_alls[g][h * tq:(h + 1) * tq] + bc_ref[h, sls[g], :]
            m = jnp.max(s, axis=-1, keepdims=True)
            e = jnp.where(valid, jnp.exp(s - m), 0.0)
            l = jnp.sum(e, axis=-1, keepdims=True)
            p = e * (1.0 / jnp.where(l > 0.0, l, 1.0))
            psum = psum + p
            ps.append(p.astype(BF16))
        psums.append(psum)
        o_alls.append(_dot(jnp.concatenate(ps, axis=0), kvc))
    imps = []
    for g in groups:
        hi = psums[g].astype(BF16)
        lo = (psums[g] - hi.astype(F32)).astype(BF16)
        imps.append(_dot_nt(ovt, hi) + _dot_nt(ovt, lo))
    for g in groups:
        ocmp = _merge_upper([o_alls[g][h * tq:(h + 1) * tq] for h in range(NSA_HEADS)])
        ocmp_ref[0, sls[g], :] = ocmp.astype(ocmp_ref.dtype)
        sb = _topk_mask_t(imps[g], (pl.program_id(1) * CMP_GROUP + g) * tq)
        if ns < HEAD_DIM:
            sb = jnp.concatenate([sb, jnp.full((HEAD_DIM - ns, tq), NEG, F32)], axis=0)
        full = jnp.concatenate([jnp.zeros((HEAD_DIM, tq), F32), sb], axis=0)
        selb_ref[0, sls[g], :] = full.T.astype(BF16)


def _cmp_attention(qa, kvc, bc, ovt, T):
    B = qa.shape[0]
    ncp = kvc.shape[1]
    rows = CMP_GROUP * Q_TILE
    return pl.pallas_call(
        _cmp_kernel,
        grid=(B, T // rows),
        in_specs=[pl.BlockSpec((1, rows, 256), lambda b, n: (b, n, 0)),
                  pl.BlockSpec((1, ncp, LANES), lambda b, n: (b, 0, 0)),
                  pl.BlockSpec((NSA_HEADS, rows, ncp), lambda b, n: (0, n, 0)),
                  pl.BlockSpec(ovt.shape, lambda b, n: (0, 0))],
        out_specs=[pl.BlockSpec((1, rows, 256), lambda b, n: (b, n, 0)),
                   pl.BlockSpec((1, rows, LANES), lambda b, n: (b, n, 0))],
        out_shape=[jax.ShapeDtypeStruct((B, T, 256), BF16), jax.ShapeDtypeStruct((B, T, LANES), BF16)],
        compiler_params=_cparams(("arbitrary", "arbitrary")),
        name="cmp_select",
    )(qa, kvc, bc, ovt)


def _softmax_tile(s2, rows, m_s, shift=None):
    m_t = _row_max_lanes(s2)
    if shift is not None:
        m_t = m_t + shift
    m_old = m_s[rows]
    m_new = jnp.maximum(m_old, m_t)
    p = jnp.exp2(_sub_lanes(s2, m_new if shift is None else m_new - shift))
    alpha = jnp.exp2(m_old - m_new)
    m_s[rows] = m_new
    return p.astype(BF16), alpha


def _slc_kernel(b31_ref, q_ref, selb_ref, ka_ref, kv_ref, tab_ref, o_ref, m_s, acc_s):
    tq = q_ref.shape[1]
    H = NSA_HEADS
    n = pl.program_id(1)
    qs = _head_slabs(q_ref[0], H)
    selb = selb_ref[0].astype(F32)
    qst = jnp.concatenate([(qs[h] + selb).astype(BF16) for h in range(H)], axis=0)
    m_s[...] = jnp.full(m_s.shape, -jnp.inf, F32)
    acc_s[...] = jnp.zeros(acc_s.shape, F32)

    def scores(k0, width=tq):
        k0 = pl.multiple_of(k0, tq)
        return _dot_nt(qst, ka_ref[0, pl.ds(k0, width), :]), kv_ref[0, pl.ds(k0, width), :]

    def update(s_all, kv, near_block):
        ps, alphas = [], []
        for h in range(H):
            rows = slice(h * tq, (h + 1) * tq)
            s2 = s_all[rows]
            if near_block is None:
                p, alpha = _softmax_tile(s2, rows, m_s, shift=b31_ref[h])
            else:
                s2 = s2 + (tab_ref[h] if near_block == 2 else tab_ref[h, :, tq:])
                p, alpha = _softmax_tile(s2, rows, m_s)
            ps.append(p)
            alphas.append(alpha)
        pv = _dot(jnp.concatenate(ps, axis=0), kv)
        for h in range(H):
            rows = slice(h * tq, (h + 1) * tq)
            acc_s[rows] = alphas[h] * acc_s[rows] + pv[rows]

    def far_pair(i, carry):
        update(*scores(2 * i * tq, 2 * tq), None)
        return carry

    n_far = jnp.maximum(n - 1, 0)
    lax.fori_loop(0, n_far // 2, far_pair, 0)

    @pl.when(n_far % 2 == 1)
    def _():
        update(*scores((n_far - 1) * tq), None)

    @pl.when(n >= 1)
    def _():
        update(*scores(jnp.maximum(n - 1, 0) * tq, 2 * tq), 2)

    @pl.when(n == 0)
    def _():
        update(*scores(0), 1)

    o_ref[0] = _merge_normalized([acc_s[h * tq:(h + 1) * tq] for h in range(H)]).astype(o_ref.dtype)


def _slc_attention(b31, qa, selb, slck, slcv, tab, T):
    B = qa.shape[0]
    tq = FLASH_TILE
    H = NSA_HEADS
    tile = lambda w: pl.BlockSpec((1, tq, w), lambda b, n: (b, n, 0))
    return pl.pallas_call(
        _slc_kernel,
        grid=(B, T // tq),
        in_specs=[pl.BlockSpec(memory_space=pltpu.SMEM),
                  tile(256), tile(LANES),
                  pl.BlockSpec((1, T, LANES), lambda b, n: (b, 0, 0)),
                  pl.BlockSpec((1, T, LANES), lambda b, n: (b, 0, 0)),
                  pl.BlockSpec(tab.shape, lambda b, n: (0, 0, 0))],
        out_specs=tile(256),
        out_shape=jax.ShapeDtypeStruct((B, T, 256), BF16),
        scratch_shapes=[pltpu.VMEM((H * tq, LANES), F32), pltpu.VMEM((H * tq, LANES), F32)],
        compiler_params=_cparams(("arbitrary", "arbitrary")),
        name="slc_attention",
    )(b31, qa, selb, slck, slcv, tab)


def _band_kernel(n_heads, window, has_sink, n_sub, *refs):
    if has_sink:
        q_ref, kv_ref, tab_ref, sink_ref, o_ref = refs
    else:
        q_ref, kv_ref, tab_ref, o_ref = refs
    tq = Q_TILE
    span = window + tq
    lo_kv = lax.broadcasted_iota(jnp.int32, (span, LANES), 1) < HEAD_DIM
    lo_o = lax.broadcasted_iota(jnp.int32, (n_heads * tq, LANES), 1) < HEAD_DIM
    groups = range(n_sub)
    tiles = [pl.program_id(1) * n_sub + g for g in groups]
    kvs, dots = [], []
    for g in groups:
        k0 = pl.multiple_of(jnp.maximum(tiles[g] * tq - window, 0), tq)
        kvs.append(kv_ref[0, pl.ds(k0, span), :])
        qs = _head_slabs(q_ref[0, g * tq:(g + 1) * tq, :], n_heads)
        dots.append(_dot_nt(jnp.concatenate([q.astype(BF16) for q in qs], axis=0), kvs[g]))
    accs, ms = [], []
    for g in groups:
        s = dots[g] + tab_ref[jnp.minimum(tiles[g], window // tq)]
        m = _row_max_lanes(s)
        if has_sink:
            m = jnp.maximum(m, sink_ref[...])
        e = jnp.exp(_sub_lanes(s, m))
        ms.append(m)
        accs.append(_dot(e.astype(BF16), jnp.where(lo_kv, 1.0, kvs[g]).astype(BF16)))
    for g in groups:
        acc = accs[g]
        if has_sink:
            acc = acc + jnp.where(lo_o, jnp.exp(sink_ref[...] - ms[g]), 0.0)
        out = _merge_normalized([acc[h * tq:(h + 1) * tq] for h in range(n_heads)])
        o_ref[0, g * tq:(g + 1) * tq, :] = out.astype(o_ref.dtype)


def _band_attention(q, kv_pad, tab, sinks, n_heads, window, T, out_dtype):
    B = q.shape[0]
    width = n_heads * HEAD_DIM
    has_sink = sinks is not None
    nvar = tab.shape[1]
    tab2 = jnp.transpose(tab, (1, 0, 2, 3)).reshape(nvar, n_heads * Q_TILE, window + Q_TILE)
    n_sub = BAND_ROWS // (n_heads * Q_TILE)
    rows = n_sub * Q_TILE
    in_specs = [pl.BlockSpec((1, rows, width), lambda b, n: (b, n, 0)),
                pl.BlockSpec((1, T, LANES), lambda b, n: (b, 0, 0)),
                pl.BlockSpec(tab2.shape, lambda b, n: (0, 0, 0))]
    args = (q, kv_pad, tab2)
    if has_sink:
        rep = jnp.broadcast_to(jnp.repeat(sinks, Q_TILE)[:, None], (n_heads * Q_TILE, LANES))
        in_specs.append(pl.BlockSpec(rep.shape, lambda b, n: (0, 0)))
        args = args + (rep,)
    return pl.pallas_call(
        functools.partial(_band_kernel, n_heads, window, has_sink, n_sub),
        grid=(B, T // rows),
        in_specs=in_specs,
        out_specs=pl.BlockSpec((1, rows, width), lambda b, n: (b, n, 0)),
        out_shape=jax.ShapeDtypeStruct((B, T, width), out_dtype),
        compiler_params=_cparams(("arbitrary", "arbitrary")),
        name="band_sink" if has_sink else "band_window",
    )(*args)


def _mla_kernel(q_ref, k_ref, v_ref, o_ref, m_s, acc_s):
    tq = q_ref.shape[1]
    H = MLA_HEADS
    n = pl.program_id(1)
    m_s[...] = jnp.full(m_s.shape, -jnp.inf, F32)
    acc_s[...] = jnp.zeros(acc_s.shape, F32)

    def scores(k0, width=tq):
        k0 = pl.multiple_of(k0, tq)
        return k0, width, [_dot_nt(q_ref[0, :, LANES * h:LANES * (h + 1)],
                                   k_ref[0, pl.ds(k0, width), LANES * h:LANES * (h + 1)]) for h in range(H)]

    def update(k0, width, ss, diag):
        ps, alphas = [], []
        for h in range(H):
            rows = slice(h * tq, (h + 1) * tq)
            s2 = ss[h]
            if diag:
                row = lax.broadcasted_iota(jnp.int32, (tq, width), 0)
                col = lax.broadcasted_iota(jnp.int32, (tq, width), 1)
                s2 = jnp.where(col <= row + (width - tq), s2, NEG)
            p, alpha = _softmax_tile(s2, rows, m_s)
            ps.append(p)
            alphas.append(alpha)
        for h in range(H):
            rows = slice(h * tq, (h + 1) * tq)
            v = v_ref[0, pl.ds(k0, width), LANES * h:LANES * (h + 1)]
            acc_s[rows] = alphas[h] * acc_s[rows] + _dot(ps[h], v)

    def far_pair(i, carry):
        update(*scores(2 * i * tq, 2 * tq), False)
        return carry

    n_far = jnp.maximum(n - 1, 0)
    lax.fori_loop(0, n_far // 2, far_pair, 0)

    @pl.when(n_far % 2 == 1)
    def _():
        update(*scores((n_far - 1) * tq), False)

    @pl.when(n >= 1)
    def _():
        update(*scores(jnp.maximum(n - 1, 0) * tq, 2 * tq), True)

    @pl.when(n == 0)
    def _():
        update(*scores(0), True)

    o_ref[0] = _merge_normalized([acc_s[h * tq:(h + 1) * tq] for h in range(H)]).astype(o_ref.dtype)


def _mla_attention(qm, km, vm, T):
    B = qm.shape[0]
    tq = FLASH_TILE
    H = MLA_HEADS
    return pl.pallas_call(
        _mla_kernel,
        grid=(B, T // tq),
        in_specs=[pl.BlockSpec((1, tq, 512), lambda b, n: (b, n, 0)),
                  pl.BlockSpec((1, T, 512), lambda b, n: (b, 0, 0)),
                  pl.BlockSpec((1, T, 512), lambda b, n: (b, 0, 0))],
        out_specs=pl.BlockSpec((1, tq, 256), lambda b, n: (b, n, 0)),
        out_shape=jax.ShapeDtypeStruct((B, T, 256), BF16),
        scratch_shapes=[pltpu.VMEM((H * tq, LANES), F32), pltpu.VMEM((H * tq, LANES), F32)],
        compiler_params=_cparams(("arbitrary", "arbitrary")),
        name="mla_attention",
    )(qm, km, vm)


def _out_kernel(x_ref, gpre_ref, wz_ref, ocmp_ref, oslc_ref, owin_ref, krb_ref, gexp_ref, ob_ref, oc_ref, w_ref,
                gpost_ref, o_ref):
    sub = x_ref.shape[0] // OUT_SUBTILES
    rs = [slice(t * sub, (t + 1) * sub) for t in range(OUT_SUBTILES)]
    w_a = NSA_HEADS * HEAD_DIM
    hs = [_rms(x_ref[r, :], gpre_ref[...]).astype(BF16) for r in rs]
    zs = [_dot(h, wz_ref[...]) for h in hs]
    mixeds = []
    for r, z in zip(rs, zs):
        sig = 1.0 / (1.0 + jnp.exp(-krb_ref[r, :]))
        hi = sig.astype(BF16)
        lo = (sig - hi.astype(F32)).astype(BF16)
        gates = _dot(hi, gexp_ref[...]) + _dot(lo, gexp_ref[...])
        oa = None
        for j, br in enumerate((ocmp_ref, oslc_ref, owin_ref)):
            term = gates[:, w_a * j:w_a * (j + 1)] * br[r, :].astype(F32)
            oa = term if oa is None else oa + term
        heads = jnp.concatenate([oa, ob_ref[r, :].astype(F32), oc_ref[r, :].astype(F32)], axis=1)
        mixeds.append((heads * (z * (1.0 / (1.0 + jnp.exp(-z))))).astype(BF16))
    ys = [_dot(m, w_ref[...]) for m in mixeds]
    for r, y in zip(rs, ys):
        o_ref[r, :] = x_ref[r, :] + _rms(y, gpost_ref[...])


def _out_project(x2, gpre, wz, ocmp, oslc, owin, krb, gexp, ob, oc, w, gpost, tm):
    BT = x2.shape[0]
    row = lambda i: (i, 0)
    fix = lambda i: (0, 0)
    spec = lambda a: pl.BlockSpec((tm, a.shape[1]), row)
    full = lambda a: pl.BlockSpec(a.shape, fix)
    return pl.pallas_call(
        _out_kernel,
        grid=(BT // tm,),
        in_specs=[spec(x2), full(gpre), full(wz), spec(ocmp), spec(oslc), spec(owin), spec(krb), full(gexp),
                  spec(ob), spec(oc), full(w), full(gpost)],
        out_specs=pl.BlockSpec((tm, D_MODEL), row),
        out_shape=jax.ShapeDtypeStruct((BT, D_MODEL), F32),
        compiler_params=_cparams(("arbitrary",)),
        name="out_proj",
    )(x2, gpre, wz, ocmp, oslc, owin, krb, gexp, ob, oc, w, gpost)


def _split_in(w):
    outs, o = [], 0
    for s in IN_SIZES:
        outs.append(w[..., o:o + s])
        o += s
    return outs


def _relayout_kernel(w_ref, wp_ref, wz_ref):
    a_q, a_kv, a_g, b_cq, b_ckv, b_kr, c_q, c_kv, z = _split_in(w_ref[0])
    zeros = lambda n: jnp.zeros((a_q.shape[0], n), F32)
    krb = jnp.concatenate([zeros(HEAD_DIM), b_kr, a_g, zeros(LANES - HEAD_DIM - MLA_ROPE - 3 * NSA_HEADS)], axis=-1)
    wp_ref[0] = jnp.concatenate([a_q, a_kv, b_cq, b_ckv, krb, c_q, c_kv], axis=-1).astype(BF16)
    wz_ref[0] = z.astype(BF16)


def _prep_w_in(w):
    L, d, n = w.shape
    rt = 256
    return pl.pallas_call(
        _relayout_kernel,
        grid=(L, d // rt),
        in_specs=[pl.BlockSpec((1, rt, n), lambda l, r: (l, r, 0))],
        out_specs=[pl.BlockSpec((1, rt, C_Z), lambda l, r: (l, r, 0)),
                   pl.BlockSpec((1, rt, IN_SIZES[-1]), lambda l, r: (l, r, 0))],
        out_shape=[jax.ShapeDtypeStruct((L, d, C_Z), BF16), jax.ShapeDtypeStruct((L, d, IN_SIZES[-1]), BF16)],
        compiler_params=_cparams(("arbitrary", "arbitrary")),
        name="w_in_relayout",
    )(w)


def _prep_mla(w_uq, w_ukv):
    L = w_uq.shape[0]
    dq = MLA_NOPE + MLA_ROPE
    uq = w_uq.reshape(L, MLA_Q_RANK, MLA_HEADS, dq)
    uq = jnp.concatenate([uq, jnp.zeros((L, MLA_Q_RANK, MLA_HEADS, LANES - dq), uq.dtype)], axis=-1)
    ukv = w_ukv.reshape(L, MLA_KV_RANK, MLA_HEADS, MLA_NOPE + MLA_V)
    uk = jnp.concatenate([ukv[..., :MLA_NOPE], jnp.zeros((L, MLA_KV_RANK, MLA_HEADS, LANES - MLA_NOPE), ukv.dtype)], axis=-1)
    uv = jnp.concatenate([jnp.zeros((L, MLA_KV_RANK, MLA_HEADS, LANES - MLA_V), ukv.dtype), ukv[..., MLA_NOPE:]], axis=-1)
    return (uq.reshape(L, MLA_Q_RANK, MLA_HEADS * LANES).astype(BF16),
            uk.reshape(L, MLA_KV_RANK, MLA_HEADS * LANES).astype(BF16),
            uv.reshape(L, MLA_KV_RANK, MLA_HEADS * LANES).astype(BF16))


def _prep_compress(cmp_pos, cmp_w1, cmp_w2):
    L = cmp_pos.shape[0]
    pe = jnp.concatenate([cmp_pos[:, 0], cmp_pos[:, 1]], axis=-1)
    pet = pe[:, :CMP_STRIDE].reshape(L, 1, CMP_STRIDE * LANES)
    peb = pe[:, CMP_STRIDE:].reshape(L, 1, CMP_STRIDE * LANES)
    w1 = cmp_w1.reshape(L, 2, CMP_LEN, HEAD_DIM, CMP_HIDDEN)
    zero = jnp.zeros_like(w1[:, 0])
    w1k = jnp.concatenate([w1[:, 0], zero], axis=2)
    w1v = jnp.concatenate([zero, w1[:, 1]], axis=2)
    w1e = jnp.stack([w1k, w1v], axis=1)
    w1t = w1e[:, :, :CMP_STRIDE].reshape(L, 2, CMP_STRIDE * LANES, CMP_HIDDEN).astype(BF16)
    w1b = w1e[:, :, CMP_STRIDE:].reshape(L, 2, CMP_STRIDE * LANES, CMP_HIDDEN).astype(BF16)
    z2 = jnp.zeros_like(cmp_w2[:, 0])
    w2 = jnp.stack([jnp.concatenate([cmp_w2[:, 0], z2], axis=-1),
                    jnp.concatenate([z2, cmp_w2[:, 1]], axis=-1)], axis=1).astype(BF16)
    return pet, peb, w1t, w1b, w2


def _rope_tables(T):
    half = MLA_ROPE // 2
    inv = ROPE_THETA ** (-jnp.arange(half, dtype=F32) / half)
    ang = jnp.arange(T).astype(F32)[:, None] * inv[None, :]
    cos, sin = jnp.cos(ang), jnp.sin(ang)
    z = lambda n: jnp.zeros((T, n), F32)
    rc = jnp.concatenate([jnp.ones((T, MLA_NOPE), F32), cos, cos, z(LANES - MLA_NOPE - MLA_ROPE)], axis=1)
    rsa = jnp.concatenate([z(MLA_NOPE + half), sin, z(LANES - MLA_NOPE - MLA_ROPE)], axis=1)
    rsb = jnp.concatenate([z(MLA_NOPE), -sin, z(LANES - MLA_NOPE - half)], axis=1)
    return rc, rsa, rsb


def _static_tables(T):
    ns = T // SEL_BLOCK
    nc = T // CMP_STRIDE - 1
    ncp = T // CMP_STRIDE
    sstart = np.arange(ns) * SEL_BLOCK
    cstart = np.arange(nc) * CMP_STRIDE
    overlap = (np.clip(np.minimum(cstart[:, None] + CMP_LEN, sstart[None, :] + SEL_BLOCK)
                       - np.maximum(cstart[:, None], sstart[None, :]), 0, None) / CMP_STRIDE).astype(np.float32)
    ovt = np.zeros((ns, ncp), np.float32)
    ovt[:, :nc] = overlap.T
    eaug = np.zeros((T, LANES), np.float32)
    eaug[np.arange(T), HEAD_DIM + np.arange(T) // SEL_BLOCK] = 1.0
    gexp = np.zeros((LANES, 3 * NSA_HEADS * HEAD_DIM), np.float32)
    for h in range(NSA_HEADS):
        for j in range(3):
            c0 = NSA_HEADS * HEAD_DIM * j + HEAD_DIM * h
            gexp[GATE_LANE0 + 3 * h + j, c0:c0 + HEAD_DIM] = 1.0
    return jnp.asarray(ovt, BF16), jnp.asarray(eaug, BF16), jnp.asarray(gexp, BF16)


def kernel(x, w_in, w_out, norm_pre, norm_post, cmp_pos, cmp_w1, cmp_w2, mla_q_norm, mla_w_uq, mla_kv_norm,
           mla_w_ukv, swa_sinks, rel_bias):
    B, T, D = x.shape
    depth = w_in.shape[0]
    assert D == D_MODEL and T % (2 * FLASH_TILE) == 0 and T // SEL_BLOCK <= HEAD_DIM
    ncp = T // CMP_STRIDE
    tm = 512

    w_in_p, w_z = _prep_w_in(w_in)
    wuq, wk, wv = _prep_mla(mla_w_uq, mla_w_ukv)
    pet, peb, w1t, w1b, w2 = _prep_compress(cmp_pos, cmp_w1, cmp_w2)
    w_out_b = w_out.astype(BF16)
    rope_c, rope_sa, rope_sb = _rope_tables(T)
    ovt, eaug, gexp = _static_tables(T)

    bc = _bias_table(rel_bias, 0, NSA_HEADS, T, ncp, -CMP_STRIDE, -(CMP_LEN - 1), 0, 1 << 30, 256).reshape(NSA_HEADS, T, ncp)
    tab_near = _bias_table(rel_bias, 0, NSA_HEADS, FLASH_TILE, 2 * FLASH_TILE, -1, FLASH_TILE, 0, 1 << 30, 256,
                           mult=LOG2E).reshape(NSA_HEADS, FLASH_TILE, 2 * FLASH_TILE)
    tab_win = _bias_table(rel_bias, 0, NSA_HEADS, Q_TILE, NSA_WINDOW + Q_TILE, -1, 0, 0, NSA_WINDOW, Q_TILE,
                          nblk=NSA_WINDOW // Q_TILE + 1, blk_d0=Q_TILE)
    tab_swa = _bias_table(rel_bias, NSA_HEADS, SWA_HEADS, Q_TILE, SWA_WINDOW + Q_TILE, -1, 0, 0, SWA_WINDOW, Q_TILE,
                          nblk=SWA_WINDOW // Q_TILE + 1, blk_d0=Q_TILE)
    b31 = rel_bias[:NSA_HEADS, REL_BUCKETS - 1] * LOG2E

    x2 = x.reshape(B * T, D)
    for l in range(depth):
        (qa, qa2, cmpkv, slck, slcv, win, qm, km, vm, krb, cq, ckv) = _project(
            x2, norm_pre[l][None], w_in_p[l], mla_q_norm[l][None], wuq[l], mla_kv_norm[l][None], wk[l], wv[l],
            rope_c, rope_sa, rope_sb, eaug, T, PROJ_SUBTILES * tm)
        r3 = lambda a: a.reshape(B, T, a.shape[-1])
        kvc = _compress(r3(cmpkv), pet[l], peb[l], w1t[l], w1b[l], w2[l])
        qa3 = r3(qa)
        ocmp, selb = _cmp_attention(qa3, kvc, bc, ovt, T)
        owin = _band_attention(qa3, r3(win), tab_win, None, NSA_HEADS, NSA_WINDOW, T, BF16)
        oslc = _slc_attention(b31, r3(qa2), selb, r3(slck), r3(slcv), tab_near, T)
        oc = _band_attention(r3(cq), r3(ckv), tab_swa, swa_sinks[l], SWA_HEADS, SWA_WINDOW, T, BF16)
        ob = _mla_attention(r3(qm), r3(km), r3(vm), T)
        flat = lambda a: a.reshape(B * T, a.shape[-1])
        x2 = _out_project(x2, norm_pre[l][None], w_z[l], flat(ocmp), flat(oslc), flat(owin), krb, gexp,
                          flat(ob), flat(oc), w_out_b[l], norm_post[l][None], OUT_ROWS)
    return x2.reshape(B, T, D)
```

```python
import functools
import math

import numpy as np
import jax
import jax.numpy as jnp
from jax import lax
from jax.experimental import pallas as pl
from jax.experimental.pallas import tpu as pltpu

F32 = jnp.float32
BF16 = jnp.bfloat16

D_MODEL = 1024
HEAD_DIM = 64
NSA_HEADS = 4
CMP_LEN = 32
CMP_STRIDE = 16
CMP_HIDDEN = 128
SEL_BLOCK = 64
SEL_TOP = 16
NSA_WINDOW = 512
MLA_HEADS = 4
MLA_Q_RANK = 256
MLA_KV_RANK = 128
MLA_NOPE = 64
MLA_ROPE = 32
MLA_V = 64
ROPE_THETA = 10000.0
SWA_HEADS = 8
SWA_WINDOW = 128
REL_BUCKETS = 32
REL_MAX_DIST = 512
NORM_EPS = 1e-6
NEG = -1e30
BIG = 1e9
LOG2E = math.log2(math.e)
MLA_SCALE2 = (MLA_NOPE + MLA_ROPE) ** -0.5 * LOG2E
IN_SIZES = (256, 384, 12, 256, 128, 32, 512, 128, 1024)

LANES = 128
Q_TILE = 128
WIN_TILE = 128
BAND_ROWS = 4096
PROJ_SUBTILES = 1
OUT_SUBTILES = 4
OUT_ROWS = 1024
CMP_GROUP = 4
FLASH_TILE = 512
GATE_LANE0 = 96
VMEM_LIMIT = 56 * 1024 * 1024

C_AQ, C_CMP, C_SLC, C_WIN, C_BCQ, C_BCKV, C_KRB, C_CQ, C_CKV, C_Z = (
    0, 256, 384, 512, 640, 896, 1024, 1152, 1664, 1792)


def _dot(a, b):
    return jnp.dot(a, b, preferred_element_type=F32)


def _dot_nt(a, b):
    return lax.dot_general(a, b, (((1,), (1,)), ((), ())), preferred_element_type=F32)


def _bucket_thresholds():
    d = np.arange(0, 4 * REL_MAX_DIST)
    exact = REL_BUCKETS // 2
    large = exact + (np.log(np.maximum(d, 1).astype(np.float32) / np.float32(exact))
                     / np.float32(math.log(REL_MAX_DIST / exact)) * np.float32(REL_BUCKETS - exact)).astype(np.int32)
    b = np.where(d < exact, d, np.minimum(large, REL_BUCKETS - 1))
    assert np.all(np.diff(b) >= 0) and b[-1] == REL_BUCKETS - 1
    return [int(np.argmax(b >= k)) for k in range(REL_BUCKETS)]


_THRESH = _bucket_thresholds()


def _cparams(sem):
    return pltpu.CompilerParams(dimension_semantics=sem, vmem_limit_bytes=VMEM_LIMIT)


def _table_kernel(h0, a_col, d0, blk_d0, lo, hi, mult, rb_ref, out_ref):
    h = pl.program_id(0) + h0
    blk = pl.program_id(1)
    rt, cc = out_ref.shape[2], out_ref.shape[3]
    r = pl.program_id(2) * rt + lax.broadcasted_iota(jnp.int32, (rt, cc), 0)
    c = lax.broadcasted_iota(jnp.int32, (rt, cc), 1)
    dist = r + a_col * c + d0 + blk * blk_d0
    acc = jnp.full((rt, cc), rb_ref[h, 0], F32)
    for k in range(1, REL_BUCKETS):
        acc = jnp.where(dist >= _THRESH[k], rb_ref[h, k], acc)
    ok = (dist >= lo) & (dist < hi)
    out_ref[0, 0] = jnp.where(ok, acc * mult, NEG)


def _bias_table(rel_bias, h0, nh, rows, cols, a_col, d0, lo, hi, rt, mult=1.0, nblk=1, blk_d0=0):
    return pl.pallas_call(
        functools.partial(_table_kernel, h0, a_col, d0, blk_d0, lo, hi, mult),
        grid=(nh, nblk, rows // rt),
        in_specs=[pl.BlockSpec(memory_space=pltpu.SMEM)],
        out_specs=pl.BlockSpec((1, 1, rt, cols), lambda h, b, r: (h, b, r, 0)),
        out_shape=jax.ShapeDtypeStruct((nh, nblk, rows, cols), F32),
        compiler_params=_cparams(("arbitrary", "arbitrary", "arbitrary")),
        name="bias_table",
    )(rel_bias)


def _rms(v, g):
    return v * lax.rsqrt(jnp.mean(v * v, axis=-1, keepdims=True) + NORM_EPS) * g


def _proj_kernel(x_ref, gpre_ref, w_ref, qn_ref, wuq_ref, kvn_ref, wk_ref, wv_ref,
                 rc_ref, rsa_ref, rsb_ref, eaug_ref,
                 qa_ref, qa2_ref, cmp_ref, slck_ref, slcv_ref, win_ref, qm_ref, km_ref, vm_ref,
                 krb_ref, cq_ref, ckv_ref):
    sub = x_ref.shape[0] // PROJ_SUBTILES
    rs = [slice(t * sub, (t + 1) * sub) for t in range(PROJ_SUBTILES)]
    hbs = [_rms(x_ref[r, :], gpre_ref[...]).astype(BF16) for r in rs]
    bounds = ((C_AQ, C_CMP), (C_CMP, C_WIN), (C_WIN, C_KRB), (C_KRB, C_Z))
    chunk_sets = [{ab: _dot(hb, w_ref[:, ab[0]:ab[1]]) for ab in bounds} for hb in hbs]
    lane = lax.broadcasted_iota(jnp.int32, (sub, LANES), 1)
    lo = lane < HEAD_DIM

    for r, chunks in zip(rs, chunk_sets):
        def proj(c0, c1):
            for (a, b), y in chunks.items():
                if a <= c0 and c1 <= b:
                    return y[:, c0 - a:c1 - a]
            raise AssertionError((c0, c1))

        qa_ref[r, :] = (proj(C_AQ, C_CMP) * 0.125).astype(BF16)
        qa2_ref[r, :] = (proj(C_AQ, C_CMP) * (0.125 * LOG2E)).astype(BF16)
        cmp_ref[r, :] = proj(C_CMP, C_SLC)
        slc = proj(C_SLC, C_WIN)
        slcv_ref[r, :] = jnp.where(lo, 1.0, slc).astype(BF16)
        slck_ref[r, :] = jnp.where(lo, slc, eaug_ref[r, :].astype(F32)).astype(BF16)
        win_ref[r, :] = proj(C_WIN, C_BCQ).astype(BF16)
        cq_ref[r, :] = (proj(C_CQ, C_CKV) * 0.125).astype(BF16)
        ckv_ref[r, :] = proj(C_CKV, C_Z).astype(BF16)

        rc, rsa, rsb = rc_ref[r, :], rsa_ref[r, :], rsb_ref[r, :]

        def rope(v):
            return v * rc + pltpu.roll(v, 16, 1) * rsa + pltpu.roll(v, 112, 1) * rsb

        cqn = _rms(proj(C_BCQ, C_BCKV), qn_ref[...]).astype(BF16)
        qm = _dot(cqn, wuq_ref[...])
        for h in range(MLA_HEADS):
            qh = rope(qm[:, LANES * h:LANES * (h + 1)]) * MLA_SCALE2
            qm_ref[r, LANES * h:LANES * (h + 1)] = qh.astype(BF16)

        krb = proj(C_KRB, C_CQ)
        krb_ref[r, :] = krb
        krr = jnp.where(lo | (lane >= HEAD_DIM + MLA_ROPE), 0.0, rope(krb))

        ckvn = _rms(proj(C_BCKV, C_KRB), kvn_ref[...]).astype(BF16)
        kn = _dot(ckvn, wk_ref[...])
        for h in range(MLA_HEADS):
            km_ref[r, LANES * h:LANES * (h + 1)] = (kn[:, LANES * h:LANES * (h + 1)] + krr).astype(BF16)
        vm = _dot(ckvn, wv_ref[...])
        for h in range(MLA_HEADS):
            vm_ref[r, LANES * h:LANES * (h + 1)] = jnp.where(lo, 1.0, vm[:, LANES * h:LANES * (h + 1)]).astype(BF16)


def _project(x2, gpre, w, qn, wuq, kvn, wk, wv, rope_c, rope_sa, rope_sb, eaug, T, tm):
    BT = x2.shape[0]
    nt = T // tm
    row = lambda i: (i, 0)
    fix = lambda i: (0, 0)
    pos = lambda i: (i % nt, 0)
    widths = [(256, BF16), (256, BF16), (128, F32), (128, BF16), (128, BF16), (128, BF16), (512, BF16), (512, BF16),
              (512, BF16), (128, F32), (512, BF16), (128, BF16)]
    return pl.pallas_call(
        _proj_kernel,
        grid=(BT // tm,),
        in_specs=[pl.BlockSpec((tm, D_MODEL), row), pl.BlockSpec((1, D_MODEL), fix),
                  pl.BlockSpec(w.shape, fix), pl.BlockSpec(qn.shape, fix), pl.BlockSpec(wuq.shape, fix),
                  pl.BlockSpec(kvn.shape, fix), pl.BlockSpec(wk.shape, fix), pl.BlockSpec(wv.shape, fix),
                  pl.BlockSpec((tm, LANES), pos), pl.BlockSpec((tm, LANES), pos), pl.BlockSpec((tm, LANES), pos),
                  pl.BlockSpec((tm, LANES), pos)],
        out_specs=[pl.BlockSpec((tm, wd), row) for wd, _ in widths],
        out_shape=[jax.ShapeDtypeStruct((BT, wd), dt) for wd, dt in widths],
        compiler_params=_cparams(("arbitrary",)),
        name="proj",
    )(x2, gpre, w, qn, wuq, kvn, wk, wv, rope_c, rope_sa, rope_sb, eaug)


def _compress_kernel(c_ref, pet_ref, peb_ref, w1t_ref, w1b_ref, w2_ref, o_ref):
    nch = c_ref.shape[1] // CMP_STRIDE
    c = jnp.concatenate([c_ref[0, pl.ds(j, nch, stride=CMP_STRIDE), :] for j in range(CMP_STRIDE)], axis=1)
    top = (c + pet_ref[...]).astype(BF16)
    bot = (c + peb_ref[...]).astype(BF16)
    out = jnp.zeros((nch, LANES), F32)
    for i in range(2):
        pre = _dot(top, w1t_ref[i]) + pltpu.roll(_dot(bot, w1b_ref[i]), nch - 1, 0)
        hid = pre * (1.0 / (1.0 + jnp.exp(-pre)))
        out = out + _dot(hid.astype(BF16), w2_ref[i])
    o_ref[0] = out.astype(BF16)


def _compress(tokens, pet, peb, w1t, w1b, w2):
    B, T, width = tokens.shape
    nch = T // CMP_STRIDE
    full = lambda a: pl.BlockSpec(a.shape, lambda b: (0,) * a.ndim)
    return pl.pallas_call(
        _compress_kernel,
        grid=(B,),
        in_specs=[pl.BlockSpec((1, T, width), lambda b: (b, 0, 0)),
                  full(pet), full(peb), full(w1t), full(w1b), full(w2)],
        out_specs=pl.BlockSpec((1, nch, LANES), lambda b: (b, 0, 0)),
        out_shape=jax.ShapeDtypeStruct((B, nch, LANES), BF16),
        compiler_params=_cparams(("arbitrary",)),
        name="compress",
    )(tokens, pet, peb, w1t, w1b, w2)


def _head_slabs(qblk, n_heads):
    q = qblk.astype(F32)
    lo = lax.broadcasted_iota(jnp.int32, (q.shape[0], LANES), 1) < HEAD_DIM
    out = []
    for j in range(n_heads // 2):
        slab = q[:, LANES * j:LANES * (j + 1)]
        out.append(jnp.where(lo, slab, 0.0))
        out.append(jnp.where(lo, pltpu.roll(slab, HEAD_DIM, 1), 0.0))
    return out


def _merge_upper(accs):
    lo = lax.broadcasted_iota(jnp.int32, accs[0].shape, 1) < HEAD_DIM
    slabs = [jnp.where(lo, pltpu.roll(accs[2 * j], HEAD_DIM, 1), accs[2 * j + 1]) for j in range(len(accs) // 2)]
    return jnp.concatenate(slabs, axis=1)


def _merge_normalized(accs):
    lo = lax.broadcasted_iota(jnp.int32, accs[0].shape, 1) < HEAD_DIM
    slabs = []
    for j in range(len(accs) // 2):
        a, b = accs[2 * j], accs[2 * j + 1]
        out = jnp.where(lo, pltpu.roll(a, HEAD_DIM, 1), b)
        den = jnp.where(lo, a, pltpu.roll(b, HEAD_DIM, 1))
        slabs.append(out / den)
    return jnp.concatenate(slabs, axis=1)


def _row_max_lanes(s):
    return jnp.broadcast_to(jnp.max(s, axis=-1, keepdims=True), (s.shape[0], LANES))


def _sub_lanes(s, ref):
    return jnp.concatenate([s[:, LANES * i:LANES * (i + 1)] - ref for i in range(s.shape[1] // LANES)], axis=1)


def _topk_mask_t(imp, t0):
    ns, tq = imp.shape
    srow = lax.broadcasted_iota(jnp.int32, (ns, tq), 0)
    cur = (t0 + lax.broadcasted_iota(jnp.int32, (ns, tq), 1)) // SEL_BLOCK
    forced = (srow == 0) | (srow == cur) | (srow == cur - 1)
    x = jnp.where(forced, BIG, jnp.where(srow <= cur, imp, -BIG))
    sub = 8
    groups = [x[sub * v:sub * (v + 1)] for v in range(ns // sub)]
    rows_in = [srow[sub * v:sub * (v + 1)] for v in range(ns // sub)]
    cnts = [jnp.zeros((sub, tq), F32) for _ in groups]
    for sp in range(ns):
        other = jnp.broadcast_to(x[sp:sp + 1, :], (sub, tq))
        for v, xv in enumerate(groups):
            if sub * v > sp:
                beats = other >= xv
            elif sub * v + sub - 1 < sp:
                beats = other > xv
            else:
                beats = (other > xv) | ((other == xv) & (rows_in[v] > sp))
            cnts[v] = cnts[v] + jnp.where(beats, 1.0, 0.0)
    cnt = jnp.concatenate(cnts, axis=0)
    sel = (cnt < float(min(SEL_TOP, ns))) & (srow <= cur)
    return jnp.where(sel, 0.0, NEG)


def _cmp_kernel(q_ref, kvc_ref, bc_ref, ovt_ref, ocmp_ref, selb_ref):
    tq = Q_TILE
    kvc = kvc_ref[0]
    ovt = ovt_ref[...]
    ns = ovt.shape[0]
    groups = range(CMP_GROUP)
    sls = [slice(g * tq, (g + 1) * tq) for g in groups]
    s_alls = []
    for g in groups:
        qs = _head_slabs(q_ref[0, sls[g], :], NSA_HEADS)
        s_alls.append(_dot_nt(jnp.concatenate([q.astype(BF16) for q in qs], axis=0), kvc))
    psums, o_alls = [], []
    for g in groups:
        valid = bc_ref[0, sls[g], :] > 0.5 * NEG
        psum = jnp.zeros(valid.shape, F32)
        ps = []
        for h in range(NSA_HEADS):
            s = s_alls[g][h * tq:(h + 1) * tq] + bc_ref[h, sls[g], :]
            m = jnp.max(s, axis=-1, keepdims=True)
            e = jnp.where(valid, jnp.exp(s - m), 0.0)
            l = jnp.sum(e, axis=-1, keepdims=True)
            p = e * (1.0 / jnp.where(l > 0.0, l, 1.0))
            psum = psum + p
            ps.append(p.astype(BF16))
        psums.append(psum)
        o_alls.append(_dot(jnp.concatenate(ps, axis=0), kvc))
    imps = []
    for g in groups:
        hi = psums[g].astype(BF16)
        lo = (psums[g] - hi.astype(F32)).astype(BF16)
        imps.append(_dot_nt(ovt, hi) + _dot_nt(ovt, lo))
    for g in groups:
        ocmp = _merge_upper([o_alls[g][h * tq:(h + 1) * tq] for h in range(NSA_HEADS)])
        ocmp_ref[0, sls[g], :] = ocmp.astype(ocmp_ref.dtype)
        sb = _topk_mask_t(imps[g], (pl.program_id(1) * CMP_GROUP + g) * tq)
        if ns < HEAD_DIM:
            sb = jnp.concatenate([sb, jnp.full((HEAD_DIM - ns, tq), NEG, F32)], axis=0)
        full = jnp.concatenate([jnp.zeros((HEAD_DIM, tq), F32), sb], axis=0)
        selb_ref[0, sls[g], :] = full.T.astype(BF16)


def _cmp_attention(qa, kvc, bc, ovt, T):
    B = qa.shape[0]
    ncp = kvc.shape[1]
    rows = CMP_GROUP * Q_TILE
    return pl.pallas_call(
        _cmp_kernel,
        grid=(B, T // rows),
        in_specs=[pl.BlockSpec((1, rows, 256), lambda b, n: (b, n, 0)),
                  pl.BlockSpec((1, ncp, LANES), lambda b, n: (b, 0, 0)),
                  pl.BlockSpec((NSA_HEADS, rows, ncp), lambda b, n: (0, n, 0)),
                  pl.BlockSpec(ovt.shape, lambda b, n: (0, 0))],
        out_specs=[pl.BlockSpec((1, rows, 256), lambda b, n: (b, n, 0)),
                   pl.BlockSpec((1, rows, LANES), lambda b, n: (b, n, 0))],
        out_shape=[jax.ShapeDtypeStruct((B, T, 256), BF16), jax.ShapeDtypeStruct((B, T, LANES), BF16)],
        compiler_params=_cparams(("arbitrary", "arbitrary")),
        name="cmp_select",
    )(qa, kvc, bc, ovt)


def _softmax_tile(s2, rows, m_s, shift=None):
    m_t = _row_max_lanes(s2)
    if shift is not None:
        m_t = m_t + shift
    m_old = m_s[rows]
    m_new = jnp.maximum(m_old, m_t)
    p = jnp.exp2(_sub_lanes(s2, m_new if shift is None else m_new - shift))
    alpha = jnp.exp2(m_old - m_new)
    m_s[rows] = m_new
    return p.astype(BF16), alpha


def _slc_kernel(b31_ref, q_ref, selb_ref, ka_ref, kv_ref, tab_ref, o_ref, m_s, acc_s):
    tq = q_ref.shape[1]
    H = NSA_HEADS
    n = pl.program_id(1)
    qs = _head_slabs(q_ref[0], H)
    selb = selb_ref[0].astype(F32)
    qst = jnp.concatenate([(qs[h] + selb).astype(BF16) for h in range(H)], axis=0)
    m_s[...] = jnp.full(m_s.shape, -jnp.inf, F32)
    acc_s[...] = jnp.zeros(acc_s.shape, F32)

    def scores(k0, width=tq):
        k0 = pl.multiple_of(k0, tq)
        return _dot_nt(qst, ka_ref[0, pl.ds(k0, width), :]), kv_ref[0, pl.ds(k0, width), :]

    def update(s_all, kv):
        ps, alphas = [], []
        for h in range(H):
            rows = slice(h * tq, (h + 1) * tq)
            p, alpha = _softmax_tile(s_all[rows], rows, m_s, shift=b31_ref[h])
            ps.append(p)
            alphas.append(alpha)
        pv = _dot(jnp.concatenate(ps, axis=0), kv)
        for h in range(H):
            rows = slice(h * tq, (h + 1) * tq)
            acc_s[rows] = alphas[h] * acc_s[rows] + pv[rows]

    def far_pair(i, carry):
        update(*scores(2 * i * tq, 2 * tq))
        return carry

    n_far = jnp.maximum(n - 1, 0)
    lax.fori_loop(0, n_far // 2, far_pair, 0)

    @pl.when(n_far % 2 == 1)
    def _():
        update(*scores((n_far - 1) * tq))

    def near_diagonal(k0, before):
        k0 = pl.multiple_of(k0, tq)
        half = tq // 2
        parts = []
        for part in range(2):
            kw = before + half * (part + 1)
            q_part = jnp.concatenate([qst[h * tq + part * half:h * tq + (part + 1) * half] for h in range(H)], axis=0)
            parts.append((_dot_nt(q_part, ka_ref[0, pl.ds(k0, kw), :]), kv_ref[0, pl.ds(k0, kw), :], kw))
        for part, (s_all, kv, kw) in enumerate(parts):
            ps, alphas = [], []
            for h in range(H):
                rows = slice(h * tq + part * half, h * tq + (part + 1) * half)
                bias = tab_ref[h, part * half:(part + 1) * half, tq - before:tq - before + kw]
                p, alpha = _softmax_tile(s_all[h * half:(h + 1) * half] + bias, rows, m_s)
                ps.append(p)
                alphas.append(alpha)
            pv = _dot(jnp.concatenate(ps, axis=0), kv)
            for h in range(H):
                rows = slice(h * tq + part * half, h * tq + (part + 1) * half)
                acc_s[rows] = alphas[h] * acc_s[rows] + pv[h * half:(h + 1) * half]

    @pl.when(n >= 1)
    def _():
        near_diagonal(jnp.maximum(n - 1, 0) * tq, tq)

    @pl.when(n == 0)
    def _():
        near_diagonal(0, 0)

    o_ref[0] = _merge_normalized([acc_s[h * tq:(h + 1) * tq] for h in range(H)]).astype(o_ref.dtype)


def _slc_attention(b31, qa, selb, slck, slcv, tab, T):
    B = qa.shape[0]
    tq = FLASH_TILE
    H = NSA_HEADS
    tile = lambda w: pl.BlockSpec((1, tq, w), lambda b, n: (b, n, 0))
    return pl.pallas_call(
        _slc_kernel,
        grid=(B, T // tq),
        in_specs=[pl.BlockSpec(memory_space=pltpu.SMEM),
                  tile(256), tile(LANES),
                  pl.BlockSpec((1, T, LANES), lambda b, n: (b, 0, 0)),
                  pl.BlockSpec((1, T, LANES), lambda b, n: (b, 0, 0)),
                  pl.BlockSpec(tab.shape, lambda b, n: (0, 0, 0))],
        out_specs=tile(256),
        out_shape=jax.ShapeDtypeStruct((B, T, 256), BF16),
        scratch_shapes=[pltpu.VMEM((H * tq, LANES), F32), pltpu.VMEM((H * tq, LANES), F32)],
        compiler_params=_cparams(("arbitrary", "arbitrary")),
        name="slc_attention",
    )(b31, qa, selb, slck, slcv, tab)


def _band_kernel(n_heads, window, has_sink, n_sub, tq, *refs):
    if has_sink:
        q_ref, kv_ref, tab_ref, sink_ref, o_ref = refs
    else:
        q_ref, kv_ref, tab_ref, o_ref = refs
    span = window + tq
    lo_kv = lax.broadcasted_iota(jnp.int32, (span, LANES), 1) < HEAD_DIM
    lo_o = lax.broadcasted_iota(jnp.int32, (n_heads * tq, LANES), 1) < HEAD_DIM
    groups = range(n_sub)
    tiles = [pl.program_id(1) * n_sub + g for g in groups]
    kvs, dots = [], []
    for g in groups:
        k0 = pl.multiple_of(jnp.maximum(tiles[g] * tq - window, 0), tq)
        kvs.append(kv_ref[0, pl.ds(k0, span), :])
        qs = _head_slabs(q_ref[0, g * tq:(g + 1) * tq, :], n_heads)
        dots.append(_dot_nt(jnp.concatenate([q.astype(BF16) for q in qs], axis=0), kvs[g]))
    accs, ms = [], []
    for g in groups:
        s = dots[g] + tab_ref[jnp.minimum(tiles[g], window // tq)]
        m = _row_max_lanes(s)
        if has_sink:
            m = jnp.maximum(m, sink_ref[...])
        e = jnp.exp(_sub_lanes(s, m))
        ms.append(m)
        accs.append(_dot(e.astype(BF16), jnp.where(lo_kv, 1.0, kvs[g]).astype(BF16)))
    for g in groups:
        acc = accs[g]
        if has_sink:
            acc = acc + jnp.where(lo_o, jnp.exp(sink_ref[...] - ms[g]), 0.0)
        out = _merge_normalized([acc[h * tq:(h + 1) * tq] for h in range(n_heads)])
        o_ref[0, g * tq:(g + 1) * tq, :] = out.astype(o_ref.dtype)


def _band_attention(q, kv_pad, tab, sinks, n_heads, window, T, out_dtype):
    B = q.shape[0]
    width = n_heads * HEAD_DIM
    has_sink = sinks is not None
    nvar, tq = tab.shape[1], tab.shape[2]
    tab2 = jnp.transpose(tab, (1, 0, 2, 3)).reshape(nvar, n_heads * tq, window + tq)
    n_sub = BAND_ROWS // (n_heads * tq)
    rows = n_sub * tq
    in_specs = [pl.BlockSpec((1, rows, width), lambda b, n: (b, n, 0)),
                pl.BlockSpec((1, T, LANES), lambda b, n: (b, 0, 0)),
                pl.BlockSpec(tab2.shape, lambda b, n: (0, 0, 0))]
    args = (q, kv_pad, tab2)
    if has_sink:
        rep = jnp.broadcast_to(jnp.repeat(sinks, tq)[:, None], (n_heads * tq, LANES))
        in_specs.append(pl.BlockSpec(rep.shape, lambda b, n: (0, 0)))
        args = args + (rep,)
    return pl.pallas_call(
        functools.partial(_band_kernel, n_heads, window, has_sink, n_sub, tq),
        grid=(B, T // rows),
        in_specs=in_specs,
        out_specs=pl.BlockSpec((1, rows, width), lambda b, n: (b, n, 0)),
        out_shape=jax.ShapeDtypeStruct((B, T, width), out_dtype),
        compiler_params=_cparams(("arbitrary", "arbitrary")),
        name="band_sink" if has_sink else "band_window",
    )(*args)


def _mla_kernel(q_ref, k_ref, v_ref, o_ref, m_s, acc_s):
    tq = q_ref.shape[1]
    H = MLA_HEADS
    n = pl.program_id(1)
    m_s[...] = jnp.full(m_s.shape, -jnp.inf, F32)
    acc_s[...] = jnp.zeros(acc_s.shape, F32)

    def scores(k0, width=tq):
        k0 = pl.multiple_of(k0, tq)
        return k0, width, [_dot_nt(q_ref[0, :, LANES * h:LANES * (h + 1)],
                                   k_ref[0, pl.ds(k0, width), LANES * h:LANES * (h + 1)]) for h in range(H)]

    def update(k0, width, ss):
        ps, alphas = [], []
        for h in range(H):
            p, alpha = _softmax_tile(ss[h], slice(h * tq, (h + 1) * tq), m_s)
            ps.append(p)
            alphas.append(alpha)
        for h in range(H):
            rows = slice(h * tq, (h + 1) * tq)
            v = v_ref[0, pl.ds(k0, width), LANES * h:LANES * (h + 1)]
            acc_s[rows] = alphas[h] * acc_s[rows] + _dot(ps[h], v)

    def far_pair(i, carry):
        update(*scores(2 * i * tq, 2 * tq))
        return carry

    n_far = jnp.maximum(n - 1, 0)
    lax.fori_loop(0, n_far // 2, far_pair, 0)

    @pl.when(n_far % 2 == 1)
    def _():
        update(*scores((n_far - 1) * tq))

    def near_diagonal(k0, before):
        k0 = pl.multiple_of(k0, tq)
        half = tq // 2
        parts = []
        for part in range(2):
            kw = before + half * (part + 1)
            r = slice(part * half, (part + 1) * half)
            parts.append((kw, [_dot_nt(q_ref[0, r, LANES * h:LANES * (h + 1)],
                                       k_ref[0, pl.ds(k0, kw), LANES * h:LANES * (h + 1)]) for h in range(H)]))
        for part, (kw, ss) in enumerate(parts):
            row = lax.broadcasted_iota(jnp.int32, (half, kw), 0)
            col = lax.broadcasted_iota(jnp.int32, (half, kw), 1)
            causal = col <= row + (before + part * half)
            ps, alphas = [], []
            for h in range(H):
                rows = slice(h * tq + part * half, h * tq + (part + 1) * half)
                p, alpha = _softmax_tile(jnp.where(causal, ss[h], NEG), rows, m_s)
                ps.append(p)
                alphas.append(alpha)
            for h in range(H):
                rows = slice(h * tq + part * half, h * tq + (part + 1) * half)
                v = v_ref[0, pl.ds(k0, kw), LANES * h:LANES * (h + 1)]
                acc_s[rows] = alphas[h] * acc_s[rows] + _dot(ps[h], v)

    @pl.when(n >= 1)
    def _():
        near_diagonal(jnp.maximum(n - 1, 0) * tq, tq)

    @pl.when(n == 0)
    def _():
        near_diagonal(0, 0)

    o_ref[0] = _merge_normalized([acc_s[h * tq:(h + 1) * tq] for h in range(H)]).astype(o_ref.dtype)


def _mla_attention(qm, km, vm, T):
    B = qm.shape[0]
    tq = FLASH_TILE
    H = MLA_HEADS
    return pl.pallas_call(
        _mla_kernel,
        grid=(B, T // tq),
        in_specs=[pl.BlockSpec((1, tq, 512), lambda b, n: (b, n, 0)),
                  pl.BlockSpec((1, T, 512), lambda b, n: (b, 0, 0)),
                  pl.BlockSpec((1, T, 512), lambda b, n: (b, 0, 0))],
        out_specs=pl.BlockSpec((1, tq, 256), lambda b, n: (b, n, 0)),
        out_shape=jax.ShapeDtypeStruct((B, T, 256), BF16),
        scratch_shapes=[pltpu.VMEM((H * tq, LANES), F32), pltpu.VMEM((H * tq, LANES), F32)],
        compiler_params=_cparams(("arbitrary", "arbitrary")),
        name="mla_attention",
    )(qm, km, vm)


def _out_kernel(x_ref, gpre_ref, wz_ref, ocmp_ref, oslc_ref, owin_ref, krb_ref, gexp_ref, ob_ref, oc_ref, w_ref,
                gpost_ref, o_ref):
    sub = x_ref.shape[0] // OUT_SUBTILES
    rs = [slice(t * sub, (t + 1) * sub) for t in range(OUT_SUBTILES)]
    w_a = NSA_HEADS * HEAD_DIM
    hs = [_rms(x_ref[r, :], gpre_ref[...]).astype(BF16) for r in rs]
    zs = [_dot(h, wz_ref[...]) for h in hs]
    mixeds = []
    for r, z in zip(rs, zs):
        sig = 1.0 / (1.0 + jnp.exp(-krb_ref[r, :]))
        hi = sig.astype(BF16)
        lo = (sig - hi.astype(F32)).astype(BF16)
        gates = _dot(hi, gexp_ref[...]) + _dot(lo, gexp_ref[...])
        oa = None
        for j, br in enumerate((ocmp_ref, oslc_ref, owin_ref)):
            term = gates[:, w_a * j:w_a * (j + 1)] * br[r, :].astype(F32)
            oa = term if oa is None else oa + term
        heads = jnp.concatenate([oa, ob_ref[r, :].astype(F32), oc_ref[r, :].astype(F32)], axis=1)
        mixeds.append((heads * (z * (1.0 / (1.0 + jnp.exp(-z))))).astype(BF16))
    ys = [_dot(m, w_ref[...]) for m in mixeds]
    for r, y in zip(rs, ys):
        o_ref[r, :] = x_ref[r, :] + _rms(y, gpost_ref[...])


def _out_project(x2, gpre, wz, ocmp, oslc, owin, krb, gexp, ob, oc, w, gpost, tm):
    BT = x2.shape[0]
    row = lambda i: (i, 0)
    fix = lambda i: (0, 0)
    spec = lambda a: pl.BlockSpec((tm, a.shape[1]), row)
    full = lambda a: pl.BlockSpec(a.shape, fix)
    return pl.pallas_call(
        _out_kernel,
        grid=(BT // tm,),
        in_specs=[spec(x2), full(gpre), full(wz), spec(ocmp), spec(oslc), spec(owin), spec(krb), full(gexp),
                  spec(ob), spec(oc), full(w), full(gpost)],
        out_specs=pl.BlockSpec((tm, D_MODEL), row),
        out_shape=jax.ShapeDtypeStruct((BT, D_MODEL), F32),
        compiler_params=_cparams(("arbitrary",)),
        name="out_proj",
    )(x2, gpre, wz, ocmp, oslc, owin, krb, gexp, ob, oc, w, gpost)


def _split_in(w):
    outs, o = [], 0
    for s in IN_SIZES:
        outs.append(w[..., o:o + s])
        o += s
    return outs


def _relayout_kernel(w_ref, wp_ref, wz_ref):
    a_q, a_kv, a_g, b_cq, b_ckv, b_kr, c_q, c_kv, z = _split_in(w_ref[0])
    zeros = lambda n: jnp.zeros((a_q.shape[0], n), F32)
    krb = jnp.concatenate([zeros(HEAD_DIM), b_kr, a_g, zeros(LANES - HEAD_DIM - MLA_ROPE - 3 * NSA_HEADS)], axis=-1)
    wp_ref[0] = jnp.concatenate([a_q, a_kv, b_cq, b_ckv, krb, c_q, c_kv], axis=-1).astype(BF16)
    wz_ref[0] = z.astype(BF16)


def _prep_w_in(w):
    L, d, n = w.shape
    rt = 256
    return pl.pallas_call(
        _relayout_kernel,
        grid=(L, d // rt),
        in_specs=[pl.BlockSpec((1, rt, n), lambda l, r: (l, r, 0))],
        out_specs=[pl.BlockSpec((1, rt, C_Z), lambda l, r: (l, r, 0)),
                   pl.BlockSpec((1, rt, IN_SIZES[-1]), lambda l, r: (l, r, 0))],
        out_shape=[jax.ShapeDtypeStruct((L, d, C_Z), BF16), jax.ShapeDtypeStruct((L, d, IN_SIZES[-1]), BF16)],
        compiler_params=_cparams(("arbitrary", "arbitrary")),
        name="w_in_relayout",
    )(w)


def _prep_mla(w_uq, w_ukv):
    L = w_uq.shape[0]
    dq = MLA_NOPE + MLA_ROPE
    uq = w_uq.reshape(L, MLA_Q_RANK, MLA_HEADS, dq)
    uq = jnp.concatenate([uq, jnp.zeros((L, MLA_Q_RANK, MLA_HEADS, LANES - dq), uq.dtype)], axis=-1)
    ukv = w_ukv.reshape(L, MLA_KV_RANK, MLA_HEADS, MLA_NOPE + MLA_V)
    uk = jnp.concatenate([ukv[..., :MLA_NOPE], jnp.zeros((L, MLA_KV_RANK, MLA_HEADS, LANES - MLA_NOPE), ukv.dtype)], axis=-1)
    uv = jnp.concatenate([jnp.zeros((L, MLA_KV_RANK, MLA_HEADS, LANES - MLA_V), ukv.dtype), ukv[..., MLA_NOPE:]], axis=-1)
    return (uq.reshape(L, MLA_Q_RANK, MLA_HEADS * LANES).astype(BF16),
            uk.reshape(L, MLA_KV_RANK, MLA_HEADS * LANES).astype(BF16),
            uv.reshape(L, MLA_KV_RANK, MLA_HEADS * LANES).astype(BF16))


def _prep_compress(cmp_pos, cmp_w1, cmp_w2):
    L = cmp_pos.shape[0]
    pe = jnp.concatenate([cmp_pos[:, 0], cmp_pos[:, 1]], axis=-1)
    pet = pe[:, :CMP_STRIDE].reshape(L, 1, CMP_STRIDE * LANES)
    peb = pe[:, CMP_STRIDE:].reshape(L, 1, CMP_STRIDE * LANES)
    w1 = cmp_w1.reshape(L, 2, CMP_LEN, HEAD_DIM, CMP_HIDDEN)
    zero = jnp.zeros_like(w1[:, 0])
    w1k = jnp.concatenate([w1[:, 0], zero], axis=2)
    w1v = jnp.concatenate([zero, w1[:, 1]], axis=2)
    w1e = jnp.stack([w1k, w1v], axis=1)
    w1t = w1e[:, :, :CMP_STRIDE].reshape(L, 2, CMP_STRIDE * LANES, CMP_HIDDEN).astype(BF16)
    w1b = w1e[:, :, CMP_STRIDE:].reshape(L, 2, CMP_STRIDE * LANES, CMP_HIDDEN).astype(BF16)
    z2 = jnp.zeros_like(cmp_w2[:, 0])
    w2 = jnp.stack([jnp.concatenate([cmp_w2[:, 0], z2], axis=-1),
                    jnp.concatenate([z2, cmp_w2[:, 1]], axis=-1)], axis=1).astype(BF16)
    return pet, peb, w1t, w1b, w2


def _rope_tables(T):
    half = MLA_ROPE // 2
    inv = ROPE_THETA ** (-jnp.arange(half, dtype=F32) / half)
    ang = jnp.arange(T).astype(F32)[:, None] * inv[None, :]
    cos, sin = jnp.cos(ang), jnp.sin(ang)
    z = lambda n: jnp.zeros((T, n), F32)
    rc = jnp.concatenate([jnp.ones((T, MLA_NOPE), F32), cos, cos, z(LANES - MLA_NOPE - MLA_ROPE)], axis=1)
    rsa = jnp.concatenate([z(MLA_NOPE + half), sin, z(LANES - MLA_NOPE - MLA_ROPE)], axis=1)
    rsb = jnp.concatenate([z(MLA_NOPE), -sin, z(LANES - MLA_NOPE - half)], axis=1)
    return rc, rsa, rsb


def _static_tables(T):
    ns = T // SEL_BLOCK
    nc = T // CMP_STRIDE - 1
    ncp = T // CMP_STRIDE
    sstart = np.arange(ns) * SEL_BLOCK
    cstart = np.arange(nc) * CMP_STRIDE
    overlap = (np.clip(np.minimum(cstart[:, None] + CMP_LEN, sstart[None, :] + SEL_BLOCK)
                       - np.maximum(cstart[:, None], sstart[None, :]), 0, None) / CMP_STRIDE).astype(np.float32)
    ovt = np.zeros((ns, ncp), np.float32)
    ovt[:, :nc] = overlap.T
    eaug = np.zeros((T, LANES), np.float32)
    eaug[np.arange(T), HEAD_DIM + np.arange(T) // SEL_BLOCK] = 1.0
    gexp = np.zeros((LANES, 3 * NSA_HEADS * HEAD_DIM), np.float32)
    for h in range(NSA_HEADS):
        for j in range(3):
            c0 = NSA_HEADS * HEAD_DIM * j + HEAD_DIM * h
            gexp[GATE_LANE0 + 3 * h + j, c0:c0 + HEAD_DIM] = 1.0
    return jnp.asarray(ovt, BF16), jnp.asarray(eaug, BF16), jnp.asarray(gexp, BF16)


def kernel(x, w_in, w_out, norm_pre, norm_post, cmp_pos, cmp_w1, cmp_w2, mla_q_norm, mla_w_uq, mla_kv_norm,
           mla_w_ukv, swa_sinks, rel_bias):
    B, T, D = x.shape
    depth = w_in.shape[0]
    assert D == D_MODEL and T % (2 * FLASH_TILE) == 0 and T // SEL_BLOCK <= HEAD_DIM
    ncp = T // CMP_STRIDE
    tm = 512

    w_in_p, w_z = _prep_w_in(w_in)
    wuq, wk, wv = _prep_mla(mla_w_uq, mla_w_ukv)
    pet, peb, w1t, w1b, w2 = _prep_compress(cmp_pos, cmp_w1, cmp_w2)
    w_out_b = w_out.astype(BF16)
    rope_c, rope_sa, rope_sb = _rope_tables(T)
    ovt, eaug, gexp = _static_tables(T)

    bc = _bias_table(rel_bias, 0, NSA_HEADS, T, ncp, -CMP_STRIDE, -(CMP_LEN - 1), 0, 1 << 30, 256).reshape(NSA_HEADS, T, ncp)
    tab_near = _bias_table(rel_bias, 0, NSA_HEADS, FLASH_TILE, 2 * FLASH_TILE, -1, FLASH_TILE, 0, 1 << 30, 256,
                           mult=LOG2E).reshape(NSA_HEADS, FLASH_TILE, 2 * FLASH_TILE)
    tab_win = _bias_table(rel_bias, 0, NSA_HEADS, WIN_TILE, NSA_WINDOW + WIN_TILE, -1, 0, 0, NSA_WINDOW, WIN_TILE,
                          nblk=NSA_WINDOW // WIN_TILE + 1, blk_d0=WIN_TILE)
    tab_swa = _bias_table(rel_bias, NSA_HEADS, SWA_HEADS, Q_TILE, SWA_WINDOW + Q_TILE, -1, 0, 0, SWA_WINDOW, Q_TILE,
                          nblk=SWA_WINDOW // Q_TILE + 1, blk_d0=Q_TILE)
    b31 = rel_bias[:NSA_HEADS, REL_BUCKETS - 1] * LOG2E

    x2 = x.reshape(B * T, D)
    for l in range(depth):
        (qa, qa2, cmpkv, slck, slcv, win, qm, km, vm, krb, cq, ckv) = _project(
            x2, norm_pre[l][None], w_in_p[l], mla_q_norm[l][None], wuq[l], mla_kv_norm[l][None], wk[l], wv[l],
            rope_c, rope_sa, rope_sb, eaug, T, PROJ_SUBTILES * tm)
        r3 = lambda a: a.reshape(B, T, a.shape[-1])
        kvc = _compress(r3(cmpkv), pet[l], peb[l], w1t[l], w1b[l], w2[l])
        qa3 = r3(qa)
        ocmp, selb = _cmp_attention(qa3, kvc, bc, ovt, T)
        owin = _band_attention(qa3, r3(win), tab_win, None, NSA_HEADS, NSA_WINDOW, T, BF16)
        oslc = _slc_attention(b31, r3(qa2), selb, r3(slck), r3(slcv), tab_near, T)
        oc = _band_attention(r3(cq), r3(ckv), tab_swa, swa_sinks[l], SWA_HEADS, SWA_WINDOW, T, BF16)
        ob = _mla_attention(r3(qm), r3(km), r3(vm), T)
        flat = lambda a: a.reshape(B * T, a.shape[-1])
        x2 = _out_project(x2, norm_pre[l][None], w_z[l], flat(ocmp), flat(oslc), flat(owin), krb, gexp,
                          flat(ob), flat(oc), w_out_b[l], norm_post[l][None], OUT_ROWS)
    return x2.reshape(B, T, D)
```

```python
import functools
import math

import numpy as np
import jax
import jax.numpy as jnp
from jax import lax
from jax.experimental import pallas as pl
from jax.experimental.pallas import tpu as pltpu

F32 = jnp.float32
BF16 = jnp.bfloat16

D_MODEL = 1024
HEAD_DIM = 64
NSA_HEADS = 4
CMP_LEN = 32
CMP_STRIDE = 16
CMP_HIDDEN = 128
SEL_BLOCK = 64
SEL_TOP = 16
NSA_WINDOW = 512
MLA_HEADS = 4
MLA_Q_RANK = 256
MLA_KV_RANK = 128
MLA_NOPE = 64
MLA_ROPE = 32
MLA_V = 64
ROPE_THETA = 10000.0
SWA_HEADS = 8
SWA_WINDOW = 128
REL_BUCKETS = 32
REL_MAX_DIST = 512
NORM_EPS = 1e-6
NEG = -1e30
BIG = 1e9
LOG2E = math.log2(math.e)
MLA_SCALE2 = (MLA_NOPE + MLA_ROPE) ** -0.5 * LOG2E
IN_SIZES = (256, 384, 12, 256, 128, 32, 512, 128, 1024)

LANES = 128
Q_TILE = 128
WIN_TILE = 128
BAND_ROWS = 4096
PROJ_SUBTILES = 1
OUT_SUBTILES = 4
OUT_ROWS = 1024
CMP_GROUP = 4
FLASH_TILE = 512
SLC_DIAG_PARTS = 4
GATE_LANE0 = 96
VMEM_LIMIT = 56 * 1024 * 1024

C_AQ, C_CMP, C_SLC, C_WIN, C_BCQ, C_BCKV, C_KRB, C_CQ, C_CKV, C_Z = (
    0, 256, 384, 512, 640, 896, 1024, 1152, 1664, 1792)


def _dot(a, b):
    return jnp.dot(a, b, preferred_element_type=F32)


def _dot_nt(a, b):
    return lax.dot_general(a, b, (((1,), (1,)), ((), ())), preferred_element_type=F32)


def _bucket_thresholds():
    d = np.arange(0, 4 * REL_MAX_DIST)
    exact = REL_BUCKETS // 2
    large = exact + (np.log(np.maximum(d, 1).astype(np.float32) / np.float32(exact))
                     / np.float32(math.log(REL_MAX_DIST / exact)) * np.float32(REL_BUCKETS - exact)).astype(np.int32)
    b = np.where(d < exact, d, np.minimum(large, REL_BUCKETS - 1))
    assert np.all(np.diff(b) >= 0) and b[-1] == REL_BUCKETS - 1
    return [int(np.argmax(b >= k)) for k in range(REL_BUCKETS)]


_THRESH = _bucket_thresholds()


def _cparams(sem):
    return pltpu.CompilerParams(dimension_semantics=sem, vmem_limit_bytes=VMEM_LIMIT)


def _table_kernel(h0, a_col, d0, blk_d0, lo, hi, mult, rb_ref, out_ref):
    h = pl.program_id(0) + h0
    blk = pl.program_id(1)
    rt, cc = out_ref.shape[2], out_ref.shape[3]
    r = pl.program_id(2) * rt + lax.broadcasted_iota(jnp.int32, (rt, cc), 0)
    c = lax.broadcasted_iota(jnp.int32, (rt, cc), 1)
    dist = r + a_col * c + d0 + blk * blk_d0
    acc = jnp.full((rt, cc), rb_ref[h, 0], F32)
    for k in range(1, REL_BUCKETS):
        acc = jnp.where(dist >= _THRESH[k], rb_ref[h, k], acc)
    ok = (dist >= lo) & (dist < hi)
    out_ref[0, 0] = jnp.where(ok, acc * mult, NEG)


def _bias_table(rel_bias, h0, nh, rows, cols, a_col, d0, lo, hi, rt, mult=1.0, nblk=1, blk_d0=0):
    return pl.pallas_call(
        functools.partial(_table_kernel, h0, a_col, d0, blk_d0, lo, hi, mult),
        grid=(nh, nblk, rows // rt),
        in_specs=[pl.BlockSpec(memory_space=pltpu.SMEM)],
        out_specs=pl.BlockSpec((1, 1, rt, cols), lambda h, b, r: (h, b, r, 0)),
        out_shape=jax.ShapeDtypeStruct((nh, nblk, rows, cols), F32),
        compiler_params=_cparams(("arbitrary", "arbitrary", "arbitrary")),
        name="bias_table",
    )(rel_bias)


def _rms(v, g):
    return v * lax.rsqrt(jnp.mean(v * v, axis=-1, keepdims=True) + NORM_EPS) * g


def _proj_kernel(x_ref, gpre_ref, w_ref, qn_ref, wuq_ref, kvn_ref, wk_ref, wv_ref,
                 rc_ref, rsa_ref, rsb_ref, eaug_ref,
                 qa_ref, qa2_ref, cmp_ref, slck_ref, slcv_ref, win_ref, qm_ref, km_ref, vm_ref,
                 krb_ref, cq_ref, ckv_ref):
    sub = x_ref.shape[0] // PROJ_SUBTILES
    rs = [slice(t * sub, (t + 1) * sub) for t in range(PROJ_SUBTILES)]
    hbs = [_rms(x_ref[r, :], gpre_ref[...]).astype(BF16) for r in rs]
    bounds = ((C_AQ, C_CMP), (C_CMP, C_WIN), (C_WIN, C_KRB), (C_KRB, C_Z))
    chunk_sets = [{ab: _dot(hb, w_ref[:, ab[0]:ab[1]]) for ab in bounds} for hb in hbs]
    lane = lax.broadcasted_iota(jnp.int32, (sub, LANES), 1)
    lo = lane < HEAD_DIM

    for r, chunks in zip(rs, chunk_sets):
        def proj(c0, c1):
            for (a, b), y in chunks.items():
                if a <= c0 and c1 <= b:
                    return y[:, c0 - a:c1 - a]
            raise AssertionError((c0, c1))

        qa_ref[r, :] = (proj(C_AQ, C_CMP) * 0.125).astype(BF16)
        qa2_ref[r, :] = (proj(C_AQ, C_CMP) * (0.125 * LOG2E)).astype(BF16)
        cmp_ref[r, :] = proj(C_CMP, C_SLC)
        slc = proj(C_SLC, C_WIN)
        slcv_ref[r, :] = jnp.where(lo, 1.0, slc).astype(BF16)
        slck_ref[r, :] = jnp.where(lo, slc, eaug_ref[r, :].astype(F32)).astype(BF16)
        win_ref[r, :] = proj(C_WIN, C_BCQ).astype(BF16)
        cq_ref[r, :] = (proj(C_CQ, C_CKV) * 0.125).astype(BF16)
        ckv_ref[r, :] = proj(C_CKV, C_Z).astype(BF16)

        rc, rsa, rsb = rc_ref[r, :], rsa_ref[r, :], rsb_ref[r, :]

        def rope(v):
            return v * rc + pltpu.roll(v, 16, 1) * rsa + pltpu.roll(v, 112, 1) * rsb

        cqn = _rms(proj(C_BCQ, C_BCKV), qn_ref[...]).astype(BF16)
        qm = _dot(cqn, wuq_ref[...])
        for h in range(MLA_HEADS):
            qh = rope(qm[:, LANES * h:LANES * (h + 1)]) * MLA_SCALE2
            qm_ref[r, LANES * h:LANES * (h + 1)] = qh.astype(BF16)

        krb = proj(C_KRB, C_CQ)
        krb_ref[r, :] = krb
        krr = jnp.where(lo | (lane >= HEAD_DIM + MLA_ROPE), 0.0, rope(krb))

        ckvn = _rms(proj(C_BCKV, C_KRB), kvn_ref[...]).astype(BF16)
        kn = _dot(ckvn, wk_ref[...])
        for h in range(MLA_HEADS):
            km_ref[r, LANES * h:LANES * (h + 1)] = (kn[:, LANES * h:LANES * (h + 1)] + krr).astype(BF16)
        vm = _dot(ckvn, wv_ref[...])
        for h in range(MLA_HEADS):
            vm_ref[r, LANES * h:LANES * (h + 1)] = jnp.where(lo, 1.0, vm[:, LANES * h:LANES * (h + 1)]).astype(BF16)


def _project(x2, gpre, w, qn, wuq, kvn, wk, wv, rope_c, rope_sa, rope_sb, eaug, T, tm):
    BT = x2.shape[0]
    nt = T // tm
    row = lambda i: (i, 0)
    fix = lambda i: (0, 0)
    pos = lambda i: (i % nt, 0)
    widths = [(256, BF16), (256, BF16), (128, F32), (128, BF16), (128, BF16), (128, BF16), (512, BF16), (512, BF16),
              (512, BF16), (128, F32), (512, BF16), (128, BF16)]
    return pl.pallas_call(
        _proj_kernel,
        grid=(BT // tm,),
        in_specs=[pl.BlockSpec((tm, D_MODEL), row), pl.BlockSpec((1, D_MODEL), fix),
                  pl.BlockSpec(w.shape, fix), pl.BlockSpec(qn.shape, fix), pl.BlockSpec(wuq.shape, fix),
                  pl.BlockSpec(kvn.shape, fix), pl.BlockSpec(wk.shape, fix), pl.BlockSpec(wv.shape, fix),
                  pl.BlockSpec((tm, LANES), pos), pl.BlockSpec((tm, LANES), pos), pl.BlockSpec((tm, LANES), pos),
                  pl.BlockSpec((tm, LANES), pos)],
        out_specs=[pl.BlockSpec((tm, wd), row) for wd, _ in widths],
        out_shape=[jax.ShapeDtypeStruct((BT, wd), dt) for wd, dt in widths],
        compiler_params=_cparams(("arbitrary",)),
        name="proj",
    )(x2, gpre, w, qn, wuq, kvn, wk, wv, rope_c, rope_sa, rope_sb, eaug)


def _compress_kernel(c_ref, pet_ref, peb_ref, w1t_ref, w1b_ref, w2_ref, o_ref):
    nch = c_ref.shape[1] // CMP_STRIDE
    c = jnp.concatenate([c_ref[0, pl.ds(j, nch, stride=CMP_STRIDE), :] for j in range(CMP_STRIDE)], axis=1)
    top = (c + pet_ref[...]).astype(BF16)
    bot = (c + peb_ref[...]).astype(BF16)
    out = jnp.zeros((nch, LANES), F32)
    for i in range(2):
        pre = _dot(top, w1t_ref[i]) + pltpu.roll(_dot(bot, w1b_ref[i]), nch - 1, 0)
        hid = pre * (1.0 / (1.0 + jnp.exp(-pre)))
        out = out + _dot(hid.astype(BF16), w2_ref[i])
    o_ref[0] = out.astype(BF16)


def _compress(tokens, pet, peb, w1t, w1b, w2):
    B, T, width = tokens.shape
    nch = T // CMP_STRIDE
    full = lambda a: pl.BlockSpec(a.shape, lambda b: (0,) * a.ndim)
    return pl.pallas_call(
        _compress_kernel,
        grid=(B,),
        in_specs=[pl.BlockSpec((1, T, width), lambda b: (b, 0, 0)),
                  full(pet), full(peb), full(w1t), full(w1b), full(w2)],
        out_specs=pl.BlockSpec((1, nch, LANES), lambda b: (b, 0, 0)),
        out_shape=jax.ShapeDtypeStruct((B, nch, LANES), BF16),
        compiler_params=_cparams(("arbitrary",)),
        name="compress",
    )(tokens, pet, peb, w1t, w1b, w2)


def _head_slabs(qblk, n_heads):
    q = qblk.astype(F32)
    lo = lax.broadcasted_iota(jnp.int32, (q.shape[0], LANES), 1) < HEAD_DIM
    out = []
    for j in range(n_heads // 2):
        slab = q[:, LANES * j:LANES * (j + 1)]
        out.append(jnp.where(lo, slab, 0.0))
        out.append(jnp.where(lo, pltpu.roll(slab, HEAD_DIM, 1), 0.0))
    return out


def _merge_upper(accs):
    lo = lax.broadcasted_iota(jnp.int32, accs[0].shape, 1) < HEAD_DIM
    slabs = [jnp.where(lo, pltpu.roll(accs[2 * j], HEAD_DIM, 1), accs[2 * j + 1]) for j in range(len(accs) // 2)]
    return jnp.concatenate(slabs, axis=1)


def _merge_normalized(accs):
    lo = lax.broadcasted_iota(jnp.int32, accs[0].shape, 1) < HEAD_DIM
    slabs = []
    for j in range(len(accs) // 2):
        a, b = accs[2 * j], accs[2 * j + 1]
        out = jnp.where(lo, pltpu.roll(a, HEAD_DIM, 1), b)
        den = jnp.where(lo, a, pltpu.roll(b, HEAD_DIM, 1))
        slabs.append(out / den)
    return jnp.concatenate(slabs, axis=1)


def _row_max_lanes(s):
    return jnp.broadcast_to(jnp.max(s, axis=-1, keepdims=True), (s.shape[0], LANES))


def _sub_lanes(s, ref):
    return jnp.concatenate([s[:, LANES * i:LANES * (i + 1)] - ref for i in range(s.shape[1] // LANES)], axis=1)


def _topk_mask_t(imp, t0):
    ns, tq = imp.shape
    srow = lax.broadcasted_iota(jnp.int32, (ns, tq), 0)
    cur = (t0 + lax.broadcasted_iota(jnp.int32, (ns, tq), 1)) // SEL_BLOCK
    forced = (srow == 0) | (srow == cur) | (srow == cur - 1)
    x = jnp.where(forced, BIG, jnp.where(srow <= cur, imp, -BIG))
    sub = 8
    groups = [x[sub * v:sub * (v + 1)] for v in range(ns // sub)]
    rows_in = [srow[sub * v:sub * (v + 1)] for v in range(ns // sub)]
    cnts = [jnp.zeros((sub, tq), F32) for _ in groups]
    for sp in range(ns):
        other = jnp.broadcast_to(x[sp:sp + 1, :], (sub, tq))
        for v, xv in enumerate(groups):
            if sub * v > sp:
                beats = other >= xv
            elif sub * v + sub - 1 < sp:
                beats = other > xv
            else:
                beats = (other > xv) | ((other == xv) & (rows_in[v] > sp))
            cnts[v] = cnts[v] + jnp.where(beats, 1.0, 0.0)
    cnt = jnp.concatenate(cnts, axis=0)
    sel = (cnt < float(min(SEL_TOP, ns))) & (srow <= cur)
    return jnp.where(sel, 0.0, NEG)


def _cmp_kernel(q_ref, kvc_ref, bc_ref, ovt_ref, ocmp_ref, selb_ref):
    tq = Q_TILE
    kvc = kvc_ref[0]
    ovt = ovt_ref[...]
    ns = ovt.shape[0]
    groups = range(CMP_GROUP)
    sls = [slice(g * tq, (g + 1) * tq) for g in groups]
    s_alls = []
    for g in groups:
        qs = _head_slabs(q_ref[0, sls[g], :], NSA_HEADS)
        s_alls.append(_dot_nt(jnp.concatenate([q.astype(BF16) for q in qs], axis=0), kvc))
    psums, o_alls = [], []
    for g in groups:
        psum = None
        ps = []
        for h in range(NSA_HEADS):
            s = s_alls[g][h * tq:(h + 1) * tq] + bc_ref[h, sls[g], :]
            m = jnp.max(s, axis=-1, keepdims=True)
            e = jnp.exp(s - m)
            l = jnp.sum(e, axis=-1, keepdims=True)
            p = e * jnp.where(m > 0.5 * NEG, 1.0 / l, 0.0)
            psum = p if psum is None else psum + p
            ps.append(p.astype(BF16))
        psums.append(psum)
        o_alls.append(_dot(jnp.concatenate(ps, axis=0), kvc))
    imps = []
    for g in groups:
        hi = psums[g].astype(BF16)
        lo = (psums[g] - hi.astype(F32)).astype(BF16)
        imps.append(_dot_nt(ovt, hi) + _dot_nt(ovt, lo))
    for g in groups:
        ocmp = _merge_upper([o_alls[g][h * tq:(h + 1) * tq] for h in range(NSA_HEADS)])
        ocmp_ref[0, sls[g], :] = ocmp.astype(ocmp_ref.dtype)
        sb = _topk_mask_t(imps[g], (pl.program_id(1) * CMP_GROUP + g) * tq)
        if ns < HEAD_DIM:
            sb = jnp.concatenate([sb, jnp.full((HEAD_DIM - ns, tq), NEG, F32)], axis=0)
        full = jnp.concatenate([jnp.zeros((HEAD_DIM, tq), F32), sb], axis=0)
        selb_ref[0, sls[g], :] = full.T.astype(BF16)


def _cmp_attention(qa, kvc, bc, ovt, T):
    B = qa.shape[0]
    ncp = kvc.shape[1]
    rows = CMP_GROUP * Q_TILE
    return pl.pallas_call(
        _cmp_kernel,
        grid=(B, T // rows),
        in_specs=[pl.BlockSpec((1, rows, 256), lambda b, n: (b, n, 0)),
                  pl.BlockSpec((1, ncp, LANES), lambda b, n: (b, 0, 0)),
                  pl.BlockSpec((NSA_HEADS, rows, ncp), lambda b, n: (0, n, 0)),
                  pl.BlockSpec(ovt.shape, lambda b, n: (0, 0))],
        out_specs=[pl.BlockSpec((1, rows, 256), lambda b, n: (b, n, 0)),
                   pl.BlockSpec((1, rows, LANES), lambda b, n: (b, n, 0))],
        out_shape=[jax.ShapeDtypeStruct((B, T, 256), BF16), jax.ShapeDtypeStruct((B, T, LANES), BF16)],
        compiler_params=_cparams(("arbitrary", "arbitrary")),
        name="cmp_select",
    )(qa, kvc, bc, ovt)


def _softmax_tile(s2, rows, m_s, shift=None):
    m_t = _row_max_lanes(s2)
    if shift is not None:
        m_t = m_t + shift
    m_old = m_s[rows]
    m_new = jnp.maximum(m_old, m_t)
    p = jnp.exp2(_sub_lanes(s2, m_new if shift is None else m_new - shift))
    alpha = jnp.exp2(m_old - m_new)
    m_s[rows] = m_new
    return p.astype(BF16), alpha


def _slc_kernel(b31_ref, q_ref, selb_ref, ka_ref, kv_ref, tab_ref, o_ref, m_s, acc_s):
    tq = q_ref.shape[1]
    H = NSA_HEADS
    n = pl.program_id(1)
    qs = _head_slabs(q_ref[0], H)
    selb = selb_ref[0].astype(F32)
    qst = jnp.concatenate([(qs[h] + selb).astype(BF16) for h in range(H)], axis=0)
    m_s[...] = jnp.full(m_s.shape, -jnp.inf, F32)
    acc_s[...] = jnp.zeros(acc_s.shape, F32)

    def scores(k0, width=tq):
        k0 = pl.multiple_of(k0, tq)
        return _dot_nt(qst, ka_ref[0, pl.ds(k0, width), :]), kv_ref[0, pl.ds(k0, width), :]

    def update(s_all, kv):
        ps, alphas = [], []
        for h in range(H):
            rows = slice(h * tq, (h + 1) * tq)
            p, alpha = _softmax_tile(s_all[rows], rows, m_s, shift=b31_ref[h])
            ps.append(p)
            alphas.append(alpha)
        pv = _dot(jnp.concatenate(ps, axis=0), kv)
        for h in range(H):
            rows = slice(h * tq, (h + 1) * tq)
            acc_s[rows] = alphas[h] * acc_s[rows] + pv[rows]

    def far_pair(i, carry):
        update(*scores(2 * i * tq, 2 * tq))
        return carry

    n_far = jnp.maximum(n - 1, 0)
    lax.fori_loop(0, n_far // 2, far_pair, 0)

    @pl.when(n_far % 2 == 1)
    def _():
        update(*scores((n_far - 1) * tq))

    def near_diagonal(k0, before):
        k0 = pl.multiple_of(k0, tq)
        half = tq // SLC_DIAG_PARTS
        parts = []
        for part in range(SLC_DIAG_PARTS):
            kw = before + half * (part + 1)
            q_part = jnp.concatenate([qst[h * tq + part * half:h * tq + (part + 1) * half] for h in range(H)], axis=0)
            parts.append((_dot_nt(q_part, ka_ref[0, pl.ds(k0, kw), :]), kv_ref[0, pl.ds(k0, kw), :], kw))
        for part, (s_all, kv, kw) in enumerate(parts):
            ps, alphas = [], []
            for h in range(H):
                rows = slice(h * tq + part * half, h * tq + (part + 1) * half)
                bias = tab_ref[h, part * half:(part + 1) * half, tq - before:tq - before + kw]
                p, alpha = _softmax_tile(s_all[h * half:(h + 1) * half] + bias, rows, m_s)
                ps.append(p)
                alphas.append(alpha)
            pv = _dot(jnp.concatenate(ps, axis=0), kv)
            for h in range(H):
                rows = slice(h * tq + part * half, h * tq + (part + 1) * half)
                acc_s[rows] = alphas[h] * acc_s[rows] + pv[h * half:(h + 1) * half]

    @pl.when(n >= 1)
    def _():
        near_diagonal(jnp.maximum(n - 1, 0) * tq, tq)

    @pl.when(n == 0)
    def _():
        near_diagonal(0, 0)

    o_ref[0] = _merge_normalized([acc_s[h * tq:(h + 1) * tq] for h in range(H)]).astype(o_ref.dtype)


def _slc_attention(b31, qa, selb, slck, slcv, tab, T):
    B = qa.shape[0]
    tq = FLASH_TILE
    H = NSA_HEADS
    tile = lambda w: pl.BlockSpec((1, tq, w), lambda b, n: (b, n, 0))
    return pl.pallas_call(
        _slc_kernel,
        grid=(B, T // tq),
        in_specs=[pl.BlockSpec(memory_space=pltpu.SMEM),
                  tile(256), tile(LANES),
                  pl.BlockSpec((1, T, LANES), lambda b, n: (b, 0, 0)),
                  pl.BlockSpec((1, T, LANES), lambda b, n: (b, 0, 0)),
                  pl.BlockSpec(tab.shape, lambda b, n: (0, 0, 0))],
        out_specs=tile(256),
        out_shape=jax.ShapeDtypeStruct((B, T, 256), BF16),
        scratch_shapes=[pltpu.VMEM((H * tq, LANES), F32), pltpu.VMEM((H * tq, LANES), F32)],
        compiler_params=_cparams(("arbitrary", "arbitrary")),
        name="slc_attention",
    )(b31, qa, selb, slck, slcv, tab)


def _band_kernel(n_heads, window, has_sink, n_sub, tq, *refs):
    if has_sink:
        q_ref, kv_ref, tab_ref, sink_ref, o_ref = refs
    else:
        q_ref, kv_ref, tab_ref, o_ref = refs
    span = window + tq
    lo_kv = lax.broadcasted_iota(jnp.int32, (span, LANES), 1) < HEAD_DIM
    lo_o = lax.broadcasted_iota(jnp.int32, (n_heads * tq, LANES), 1) < HEAD_DIM
    groups = range(n_sub)
    tiles = [pl.program_id(1) * n_sub + g for g in groups]
    kvs, dots = [], []
    for g in groups:
        k0 = pl.multiple_of(jnp.maximum(tiles[g] * tq - window, 0), tq)
        kvs.append(kv_ref[0, pl.ds(k0, span), :])
        qs = _head_slabs(q_ref[0, g * tq:(g + 1) * tq, :], n_heads)
        dots.append(_dot_nt(jnp.concatenate([q.astype(BF16) for q in qs], axis=0), kvs[g]))
    accs, ms = [], []
    for g in groups:
        s = dots[g] + tab_ref[jnp.minimum(tiles[g], window // tq)]
        m = _row_max_lanes(s)
        if has_sink:
            m = jnp.maximum(m, sink_ref[...])
        e = jnp.exp(_sub_lanes(s, m))
        ms.append(m)
        accs.append(_dot(e.astype(BF16), jnp.where(lo_kv, 1.0, kvs[g]).astype(BF16)))
    for g in groups:
        acc = accs[g]
        if has_sink:
            acc = acc + jnp.where(lo_o, jnp.exp(sink_ref[...] - ms[g]), 0.0)
        out = _merge_normalized([acc[h * tq:(h + 1) * tq] for h in range(n_heads)])
        o_ref[0, g * tq:(g + 1) * tq, :] = out.astype(o_ref.dtype)


def _band_attention(q, kv_pad, tab, sinks, n_heads, window, T, out_dtype):
    B = q.shape[0]
    width = n_heads * HEAD_DIM
    has_sink = sinks is not None
    nvar, tq = tab.shape[1], tab.shape[2]
    tab2 = jnp.transpose(tab, (1, 0, 2, 3)).reshape(nvar, n_heads * tq, window + tq)
    n_sub = BAND_ROWS // (n_heads * tq)
    rows = n_sub * tq
    in_specs = [pl.BlockSpec((1, rows, width), lambda b, n: (b, n, 0)),
                pl.BlockSpec((1, T, LANES), lambda b, n: (b, 0, 0)),
                pl.BlockSpec(tab2.shape, lambda b, n: (0, 0, 0))]
    args = (q, kv_pad, tab2)
    if has_sink:
        rep = jnp.broadcast_to(jnp.repeat(sinks, tq)[:, None], (n_heads * tq, LANES))
        in_specs.append(pl.BlockSpec(rep.shape, lambda b, n: (0, 0)))
        args = args + (rep,)
    return pl.pallas_call(
        functools.partial(_band_kernel, n_heads, window, has_sink, n_sub, tq),
        grid=(B, T // rows),
        in_specs=in_specs,
        out_specs=pl.BlockSpec((1, rows, width), lambda b, n: (b, n, 0)),
        out_shape=jax.ShapeDtypeStruct((B, T, width), out_dtype),
        compiler_params=_cparams(("arbitrary", "arbitrary")),
        name="band_sink" if has_sink else "band_window",
    )(*args)


def _mla_kernel(q_ref, k_ref, v_ref, o_ref, m_s, acc_s):
    tq = q_ref.shape[1]
    H = MLA_HEADS
    n = pl.program_id(1)
    m_s[...] = jnp.full(m_s.shape, -jnp.inf, F32)
    acc_s[...] = jnp.zeros(acc_s.shape, F32)

    def scores(k0, width=tq):
        k0 = pl.multiple_of(k0, tq)
        return k0, width, [_dot_nt(q_ref[0, :, LANES * h:LANES * (h + 1)],
                                   k_ref[0, pl.ds(k0, width), LANES * h:LANES * (h + 1)]) for h in range(H)]

    def update(k0, width, ss):
        ps, alphas = [], []
        for h in range(H):
            p, alpha = _softmax_tile(ss[h], slice(h * tq, (h + 1) * tq), m_s)
            ps.append(p)
            alphas.append(alpha)
        for h in range(H):
            rows = slice(h * tq, (h + 1) * tq)
            v = v_ref[0, pl.ds(k0, width), LANES * h:LANES * (h + 1)]
            acc_s[rows] = alphas[h] * acc_s[rows] + _dot(ps[h], v)

    def far_pair(i, carry):
        update(*scores(2 * i * tq, 2 * tq))
        return carry

    n_far = jnp.maximum(n - 1, 0)
    lax.fori_loop(0, n_far // 2, far_pair, 0)

    @pl.when(n_far % 2 == 1)
    def _():
        update(*scores((n_far - 1) * tq))

    def near_diagonal(k0, before):
        k0 = pl.multiple_of(k0, tq)
        half = tq // 2
        parts = []
        for part in range(2):
            kw = before + half * (part + 1)
            r = slice(part * half, (part + 1) * half)
            parts.append((kw, [_dot_nt(q_ref[0, r, LANES * h:LANES * (h + 1)],
                                       k_ref[0, pl.ds(k0, kw), LANES * h:LANES * (h + 1)]) for h in range(H)]))
        for part, (kw, ss) in enumerate(parts):
            row = lax.broadcasted_iota(jnp.int32, (half, kw), 0)
            col = lax.broadcasted_iota(jnp.int32, (half, kw), 1)
            causal = col <= row + (before + part * half)
            ps, alphas = [], []
            for h in range(H):
                rows = slice(h * tq + part * half, h * tq + (part + 1) * half)
                p, alpha = _softmax_tile(jnp.where(causal, ss[h], NEG), rows, m_s)
                ps.append(p)
                alphas.append(alpha)
            for h in range(H):
                rows = slice(h * tq + part * half, h * tq + (part + 1) * half)
                v = v_ref[0, pl.ds(k0, kw), LANES * h:LANES * (h + 1)]
                acc_s[rows] = alphas[h] * acc_s[rows] + _dot(ps[h], v)

    @pl.when(n >= 1)
    def _():
        near_diagonal(jnp.maximum(n - 1, 0) * tq, tq)

    @pl.when(n == 0)
    def _():
        near_diagonal(0, 0)

    o_ref[0] = _merge_normalized([acc_s[h * tq:(h + 1) * tq] for h in range(H)]).astype(o_ref.dtype)


def _mla_attention(qm, km, vm, T):
    B = qm.shape[0]
    tq = FLASH_TILE
    H = MLA_HEADS
    return pl.pallas_call(
        _mla_kernel,
        grid=(B, T // tq),
        in_specs=[pl.BlockSpec((1, tq, 512), lambda b, n: (b, n, 0)),
                  pl.BlockSpec((1, T, 512), lambda b, n: (b, 0, 0)),
                  pl.BlockSpec((1, T, 512), lambda b, n: (b, 0, 0))],
        out_specs=pl.BlockSpec((1, tq, 256), lambda b, n: (b, n, 0)),
        out_shape=jax.ShapeDtypeStruct((B, T, 256), BF16),
        scratch_shapes=[pltpu.VMEM((H * tq, LANES), F32), pltpu.VMEM((H * tq, LANES), F32)],
        compiler_params=_cparams(("arbitrary", "arbitrary")),
        name="mla_attention",
    )(qm, km, vm)


def _out_kernel(x_ref, gpre_ref, wz_ref, ocmp_ref, oslc_ref, owin_ref, krb_ref, gexp_ref, ob_ref, oc_ref, w_ref,
                gpost_ref, o_ref):
    sub = x_ref.shape[0] // OUT_SUBTILES
    rs = [slice(t * sub, (t + 1) * sub) for t in range(OUT_SUBTILES)]
    w_a = NSA_HEADS * HEAD_DIM
    hs = [_rms(x_ref[r, :], gpre_ref[...]).astype(BF16) for r in rs]
    zs = [_dot(h, wz_ref[...]) for h in hs]
    mixeds = []
    for r, z in zip(rs, zs):
        sig = 1.0 / (1.0 + jnp.exp(-krb_ref[r, :]))
        hi = sig.astype(BF16)
        lo = (sig - hi.astype(F32)).astype(BF16)
        gates = _dot(hi, gexp_ref[...]) + _dot(lo, gexp_ref[...])
        oa = None
        for j, br in enumerate((ocmp_ref, oslc_ref, owin_ref)):
            term = gates[:, w_a * j:w_a * (j + 1)] * br[r, :].astype(F32)
            oa = term if oa is None else oa + term
        heads = jnp.concatenate([oa, ob_ref[r, :].astype(F32), oc_ref[r, :].astype(F32)], axis=1)
        mixeds.append((heads * (z * (1.0 / (1.0 + jnp.exp(-z))))).astype(BF16))
    ys = [_dot(m, w_ref[...]) for m in mixeds]
    for r, y in zip(rs, ys):
        o_ref[r, :] = x_ref[r, :] + _rms(y, gpost_ref[...])


def _out_project(x2, gpre, wz, ocmp, oslc, owin, krb, gexp, ob, oc, w, gpost, tm):
    BT = x2.shape[0]
    row = lambda i: (i, 0)
    fix = lambda i: (0, 0)
    spec = lambda a: pl.BlockSpec((tm, a.shape[1]), row)
    full = lambda a: pl.BlockSpec(a.shape, fix)
    return pl.pallas_call(
        _out_kernel,
        grid=(BT // tm,),
        in_specs=[spec(x2), full(gpre), full(wz), spec(ocmp), spec(oslc), spec(owin), spec(krb), full(gexp),
                  spec(ob), spec(oc), full(w), full(gpost)],
        out_specs=pl.BlockSpec((tm, D_MODEL), row),
        out_shape=jax.ShapeDtypeStruct((BT, D_MODEL), F32),
        compiler_params=_cparams(("arbitrary",)),
        name="out_proj",
    )(x2, gpre, wz, ocmp, oslc, owin, krb, gexp, ob, oc, w, gpost)


def _split_in(w):
    outs, o = [], 0
    for s in IN_SIZES:
        outs.append(w[..., o:o + s])
        o += s
    return outs


def _relayout_kernel(w_ref, wp_ref, wz_ref):
    a_q, a_kv, a_g, b_cq, b_ckv, b_kr, c_q, c_kv, z = _split_in(w_ref[0])
    zeros = lambda n: jnp.zeros((a_q.shape[0], n), F32)
    krb = jnp.concatenate([zeros(HEAD_DIM), b_kr, a_g, zeros(LANES - HEAD_DIM - MLA_ROPE - 3 * NSA_HEADS)], axis=-1)
    wp_ref[0] = jnp.concatenate([a_q, a_kv, b_cq, b_ckv, krb, c_q, c_kv], axis=-1).astype(BF16)
    wz_ref[0] = z.astype(BF16)


def _prep_w_in(w):
    L, d, n = w.shape
    rt = 256
    return pl.pallas_call(
        _relayout_kernel,
        grid=(L, d // rt),
        in_specs=[pl.BlockSpec((1, rt, n), lambda l, r: (l, r, 0))],
        out_specs=[pl.BlockSpec((1, rt, C_Z), lambda l, r: (l, r, 0)),
                   pl.BlockSpec((1, rt, IN_SIZES[-1]), lambda l, r: (l, r, 0))],
        out_shape=[jax.ShapeDtypeStruct((L, d, C_Z), BF16), jax.ShapeDtypeStruct((L, d, IN_SIZES[-1]), BF16)],
        compiler_params=_cparams(("arbitrary", "arbitrary")),
        name="w_in_relayout",
    )(w)


def _prep_mla(w_uq, w_ukv):
    L = w_uq.shape[0]
    dq = MLA_NOPE + MLA_ROPE
    uq = w_uq.reshape(L, MLA_Q_RANK, MLA_HEADS, dq)
    uq = jnp.concatenate([uq, jnp.zeros((L, MLA_Q_RANK, MLA_HEADS, LANES - dq), uq.dtype)], axis=-1)
    ukv = w_ukv.reshape(L, MLA_KV_RANK, MLA_HEADS, MLA_NOPE + MLA_V)
    uk = jnp.concatenate([ukv[..., :MLA_NOPE], jnp.zeros((L, MLA_KV_RANK, MLA_HEADS, LANES - MLA_NOPE), ukv.dtype)], axis=-1)
    uv = jnp.concatenate([jnp.zeros((L, MLA_KV_RANK, MLA_HEADS, LANES - MLA_V), ukv.dtype), ukv[..., MLA_NOPE:]], axis=-1)
    return (uq.reshape(L, MLA_Q_RANK, MLA_HEADS * LANES).astype(BF16),
            uk.reshape(L, MLA_KV_RANK, MLA_HEADS * LANES).astype(BF16),
            uv.reshape(L, MLA_KV_RANK, MLA_HEADS * LANES).astype(BF16))


def _prep_compress(cmp_pos, cmp_w1, cmp_w2):
    L = cmp_pos.shape[0]
    pe = jnp.concatenate([cmp_pos[:, 0], cmp_pos[:, 1]], axis=-1)
    pet = pe[:, :CMP_STRIDE].reshape(L, 1, CMP_STRIDE * LANES)
    peb = pe[:, CMP_STRIDE:].reshape(L, 1, CMP_STRIDE * LANES)
    w1 = cmp_w1.reshape(L, 2, CMP_LEN, HEAD_DIM, CMP_HIDDEN)
    zero = jnp.zeros_like(w1[:, 0])
    w1k = jnp.concatenate([w1[:, 0], zero], axis=2)
    w1v = jnp.concatenate([zero, w1[:, 1]], axis=2)
    w1e = jnp.stack([w1k, w1v], axis=1)
    w1t = w1e[:, :, :CMP_STRIDE].reshape(L, 2, CMP_STRIDE * LANES, CMP_HIDDEN).astype(BF16)
    w1b = w1e[:, :, CMP_STRIDE:].reshape(L, 2, CMP_STRIDE * LANES, CMP_HIDDEN).astype(BF16)
    z2 = jnp.zeros_like(cmp_w2[:, 0])
    w2 = jnp.stack([jnp.concatenate([cmp_w2[:, 0], z2], axis=-1),
                    jnp.concatenate([z2, cmp_w2[:, 1]], axis=-1)], axis=1).astype(BF16)
    return pet, peb, w1t, w1b, w2


def _rope_tables(T):
    half = MLA_ROPE // 2
    inv = ROPE_THETA ** (-jnp.arange(half, dtype=F32) / half)
    ang = jnp.arange(T).astype(F32)[:, None] * inv[None, :]
    cos, sin = jnp.cos(ang), jnp.sin(ang)
    z = lambda n: jnp.zeros((T, n), F32)
    rc = jnp.concatenate([jnp.ones((T, MLA_NOPE), F32), cos, cos, z(LANES - MLA_NOPE - MLA_ROPE)], axis=1)
    rsa = jnp.concatenate([z(MLA_NOPE + half), sin, z(LANES - MLA_NOPE - MLA_ROPE)], axis=1)
    rsb = jnp.concatenate([z(MLA_NOPE), -sin, z(LANES - MLA_NOPE - half)], axis=1)
    return rc, rsa, rsb


def _static_tables(T):
    ns = T // SEL_BLOCK
    nc = T // CMP_STRIDE - 1
    ncp = T // CMP_STRIDE
    sstart = np.arange(ns) * SEL_BLOCK
    cstart = np.arange(nc) * CMP_STRIDE
    overlap = (np.clip(np.minimum(cstart[:, None] + CMP_LEN, sstart[None, :] + SEL_BLOCK)
                       - np.maximum(cstart[:, None], sstart[None, :]), 0, None) / CMP_STRIDE).astype(np.float32)
    ovt = np.zeros((ns, ncp), np.float32)
    ovt[:, :nc] = overlap.T
    eaug = np.zeros((T, LANES), np.float32)
    eaug[np.arange(T), HEAD_DIM + np.arange(T) // SEL_BLOCK] = 1.0
    gexp = np.zeros((LANES, 3 * NSA_HEADS * HEAD_DIM), np.float32)
    for h in range(NSA_HEADS):
        for j in range(3):
            c0 = NSA_HEADS * HEAD_DIM * j + HEAD_DIM * h
            gexp[GATE_LANE0 + 3 * h + j, c0:c0 + HEAD_DIM] = 1.0
    return jnp.asarray(ovt, BF16), jnp.asarray(eaug, BF16), jnp.asarray(gexp, BF16)


def kernel(x, w_in, w_out, norm_pre, norm_post, cmp_pos, cmp_w1, cmp_w2, mla_q_norm, mla_w_uq, mla_kv_norm,
           mla_w_ukv, swa_sinks, rel_bias):
    B, T, D = x.shape
    depth = w_in.shape[0]
    assert D == D_MODEL and T % (2 * FLASH_TILE) == 0 and T // SEL_BLOCK <= HEAD_DIM
    ncp = T // CMP_STRIDE
    tm = 512

    w_in_p, w_z = _prep_w_in(w_in)
    wuq, wk, wv = _prep_mla(mla_w_uq, mla_w_ukv)
    pet, peb, w1t, w1b, w2 = _prep_compress(cmp_pos, cmp_w1, cmp_w2)
    w_out_b = w_out.astype(BF16)
    rope_c, rope_sa, rope_sb = _rope_tables(T)
    ovt, eaug, gexp = _static_tables(T)

    bc = _bias_table(rel_bias, 0, NSA_HEADS, T, ncp, -CMP_STRIDE, -(CMP_LEN - 1), 0, 1 << 30, 256).reshape(NSA_HEADS, T, ncp)
    tab_near = _bias_table(rel_bias, 0, NSA_HEADS, FLASH_TILE, 2 * FLASH_TILE, -1, FLASH_TILE, 0, 1 << 30, 256,
                           mult=LOG2E).reshape(NSA_HEADS, FLASH_TILE, 2 * FLASH_TILE)
    tab_win = _bias_table(rel_bias, 0, NSA_HEADS, WIN_TILE, NSA_WINDOW + WIN_TILE, -1, 0, 0, NSA_WINDOW, WIN_TILE,
                          nblk=NSA_WINDOW // WIN_TILE + 1, blk_d0=WIN_TILE)
    tab_swa = _bias_table(rel_bias, NSA_HEADS, SWA_HEADS, Q_TILE, SWA_WINDOW + Q_TILE, -1, 0, 0, SWA_WINDOW, Q_TILE,
                          nblk=SWA_WINDOW // Q_TILE + 1, blk_d0=Q_TILE)
    b31 = rel_bias[:NSA_HEADS, REL_BUCKETS - 1] * LOG2E

    x2 = x.reshape(B * T, D)
    for l in range(depth):
        (qa, qa2, cmpkv, slck, slcv, win, qm, km, vm, krb, cq, ckv) = _project(
            x2, norm_pre[l][None], w_in_p[l], mla_q_norm[l][None], wuq[l], mla_kv_norm[l][None], wk[l], wv[l],
            rope_c, rope_sa, rope_sb, eaug, T, PROJ_SUBTILES * tm)
        r3 = lambda a: a.reshape(B, T, a.shape[-1])
        kvc = _compress(r3(cmpkv), pet[l], peb[l], w1t[l], w1b[l], w2[l])
        qa3 = r3(qa)
        ocmp, selb = _cmp_attention(qa3, kvc, bc, ovt, T)
        owin = _band_attention(qa3, r3(win), tab_win, None, NSA_HEADS, NSA_WINDOW, T, BF16)
        oslc = _slc_attention(b31, r3(qa2), selb, r3(slck), r3(slcv), tab_near, T)
        oc = _band_attention(r3(cq), r3(ckv), tab_swa, swa_sinks[l], SWA_HEADS, SWA_WINDOW, T, BF16)
        ob = _mla_attention(r3(qm), r3(km), r3(vm), T)
        flat = lambda a: a.reshape(B * T, a.shape[-1])
        x2 = _out_project(x2, norm_pre[l][None], w_z[l], flat(ocmp), flat(oslc), flat(owin), krb, gexp,
                          flat(ob), flat(oc), w_out_b[l], norm_post[l][None], OUT_ROWS)
    return x2.reshape(B, T, D)
```

```python
import functools
import math

import numpy as np
import jax
import jax.numpy as jnp
from jax import lax
from jax.experimental import pallas as pl
from jax.experimental.pallas import tpu as pltpu

F32 = jnp.float32
BF16 = jnp.bfloat16

D_MODEL = 1024
HEAD_DIM = 64
NSA_HEADS = 4
CMP_LEN = 32
CMP_STRIDE = 16
CMP_HIDDEN = 128
SEL_BLOCK = 64
SEL_TOP = 16
NSA_WINDOW = 512
MLA_HEADS = 4
MLA_Q_RANK = 256
MLA_KV_RANK = 128
MLA_NOPE = 64
MLA_ROPE = 32
MLA_V = 64
ROPE_THETA = 10000.0
SWA_HEADS = 8
SWA_WINDOW = 128
REL_BUCKETS = 32
REL_MAX_DIST = 512
NORM_EPS = 1e-6
NEG = -1e30
BIG = 1e9
LOG2E = math.log2(math.e)
MLA_SCALE2 = (MLA_NOPE + MLA_ROPE) ** -0.5 * LOG2E
IN_SIZES = (256, 384, 12, 256, 128, 32, 512, 128, 1024)

LANES = 128
Q_TILE = 128
WIN_TILE = 128
BAND_ROWS = 4096
PROJ_SUBTILES = 1
OUT_SUBTILES = 4
OUT_ROWS = 1024
CMP_GROUP = 4
FLASH_TILE = 512
SLC_DIAG_PARTS = 4
GATE_LANE0 = 96
VMEM_LIMIT = 56 * 1024 * 1024

C_AQ, C_CMP, C_SLC, C_WIN, C_BCQ, C_BCKV, C_KRB, C_CQ, C_CKV, C_Z = (
    0, 256, 384, 512, 640, 896, 1024, 1152, 1664, 1792)


def _dot(a, b):
    return jnp.dot(a, b, preferred_element_type=F32)


def _dot_nt(a, b):
    return lax.dot_general(a, b, (((1,), (1,)), ((), ())), preferred_element_type=F32)


def _bucket_thresholds():
    d = np.arange(0, 4 * REL_MAX_DIST)
    exact = REL_BUCKETS // 2
    large = exact + (np.log(np.maximum(d, 1).astype(np.float32) / np.float32(exact))
                     / np.float32(math.log(REL_MAX_DIST / exact)) * np.float32(REL_BUCKETS - exact)).astype(np.int32)
    b = np.where(d < exact, d, np.minimum(large, REL_BUCKETS - 1))
    assert np.all(np.diff(b) >= 0) and b[-1] == REL_BUCKETS - 1
    return [int(np.argmax(b >= k)) for k in range(REL_BUCKETS)]


_THRESH = _bucket_thresholds()


def _cparams(sem):
    return pltpu.CompilerParams(dimension_semantics=sem, vmem_limit_bytes=VMEM_LIMIT)


def _table_kernel(h0, a_col, d0, blk_d0, lo, hi, mult, rb_ref, out_ref):
    h = pl.program_id(0) + h0
    blk = pl.program_id(1)
    rt, cc = out_ref.shape[2], out_ref.shape[3]
    r = pl.program_id(2) * rt + lax.broadcasted_iota(jnp.int32, (rt, cc), 0)
    c = lax.broadcasted_iota(jnp.int32, (rt, cc), 1)
    dist = r + a_col * c + d0 + blk * blk_d0
    acc = jnp.full((rt, cc), rb_ref[h, 0], F32)
    for k in range(1, REL_BUCKETS):
        acc = jnp.where(dist >= _THRESH[k], rb_ref[h, k], acc)
    ok = (dist >= lo) & (dist < hi)
    out_ref[0, 0] = jnp.where(ok, acc * mult, NEG)


def _bias_table(rel_bias, h0, nh, rows, cols, a_col, d0, lo, hi, rt, mult=1.0, nblk=1, blk_d0=0):
    return pl.pallas_call(
        functools.partial(_table_kernel, h0, a_col, d0, blk_d0, lo, hi, mult),
        grid=(nh, nblk, rows // rt),
        in_specs=[pl.BlockSpec(memory_space=pltpu.SMEM)],
        out_specs=pl.BlockSpec((1, 1, rt, cols), lambda h, b, r: (h, b, r, 0)),
        out_shape=jax.ShapeDtypeStruct((nh, nblk, rows, cols), F32),
        compiler_params=_cparams(("arbitrary", "arbitrary", "arbitrary")),
        name="bias_table",
    )(rel_bias)


def _rms(v, g):
    return v * lax.rsqrt(jnp.mean(v * v, axis=-1, keepdims=True) + NORM_EPS) * g


def _proj_kernel(x_ref, gpre_ref, w_ref, qn_ref, wuq_ref, kvn_ref, wk_ref, wv_ref,
                 rc_ref, rsa_ref, rsb_ref, eaug_ref,
                 qa_ref, qa2_ref, cmp_ref, slck_ref, slcv_ref, win_ref, qm_ref, km_ref, vm_ref,
                 krb_ref, cq_ref, ckv_ref):
    sub = x_ref.shape[0] // PROJ_SUBTILES
    rs = [slice(t * sub, (t + 1) * sub) for t in range(PROJ_SUBTILES)]
    hbs = [_rms(x_ref[r, :], gpre_ref[...]).astype(BF16) for r in rs]
    bounds = ((C_AQ, C_CMP), (C_CMP, C_WIN), (C_WIN, C_KRB), (C_KRB, C_Z))
    chunk_sets = [{ab: _dot(hb, w_ref[:, ab[0]:ab[1]]) for ab in bounds} for hb in hbs]
    lane = lax.broadcasted_iota(jnp.int32, (sub, LANES), 1)
    lo = lane < HEAD_DIM

    for r, chunks in zip(rs, chunk_sets):
        def proj(c0, c1):
            for (a, b), y in chunks.items():
                if a <= c0 and c1 <= b:
                    return y[:, c0 - a:c1 - a]
            raise AssertionError((c0, c1))

        qa_ref[r, :] = (proj(C_AQ, C_CMP) * 0.125).astype(BF16)
        qa2_ref[r, :] = (proj(C_AQ, C_CMP) * (0.125 * LOG2E)).astype(BF16)
        cmp_ref[r, :] = proj(C_CMP, C_SLC)
        slc = proj(C_SLC, C_WIN)
        slcv_ref[r, :] = jnp.where(lo, 1.0, slc).astype(BF16)
        slck_ref[r, :] = jnp.where(lo, slc, eaug_ref[r, :].astype(F32)).astype(BF16)
        win_ref[r, :] = proj(C_WIN, C_BCQ).astype(BF16)
        cq_ref[r, :] = (proj(C_CQ, C_CKV) * 0.125).astype(BF16)
        ckv_ref[r, :] = proj(C_CKV, C_Z).astype(BF16)

        rc, rsa, rsb = rc_ref[r, :], rsa_ref[r, :], rsb_ref[r, :]

        def rope(v):
            return v * rc + pltpu.roll(v, 16, 1) * rsa + pltpu.roll(v, 112, 1) * rsb

        cqn = _rms(proj(C_BCQ, C_BCKV), qn_ref[...]).astype(BF16)
        qm = _dot(cqn, wuq_ref[...])
        for h in range(MLA_HEADS):
            qh = rope(qm[:, LANES * h:LANES * (h + 1)]) * MLA_SCALE2
            qm_ref[r, LANES * h:LANES * (h + 1)] = qh.astype(BF16)

        krb = proj(C_KRB, C_CQ)
        krb_ref[r, :] = krb
        krr = jnp.where(lo | (lane >= HEAD_DIM + MLA_ROPE), 0.0, rope(krb))

        ckvn = _rms(proj(C_BCKV, C_KRB), kvn_ref[...]).astype(BF16)
        kn = _dot(ckvn, wk_ref[...])
        for h in range(MLA_HEADS):
            km_ref[r, LANES * h:LANES * (h + 1)] = (kn[:, LANES * h:LANES * (h + 1)] + krr).astype(BF16)
        vm = _dot(ckvn, wv_ref[...])
        for h in range(MLA_HEADS):
            vm_ref[r, LANES * h:LANES * (h + 1)] = jnp.where(lo, 1.0, vm[:, LANES * h:LANES * (h + 1)]).astype(BF16)


def _project(x2, gpre, w, qn, wuq, kvn, wk, wv, rope_c, rope_sa, rope_sb, eaug, T, tm):
    BT = x2.shape[0]
    nt = T // tm
    row = lambda i: (i, 0)
    fix = lambda i: (0, 0)
    pos = lambda i: (i % nt, 0)
    widths = [(256, BF16), (256, BF16), (128, F32), (128, BF16), (128, BF16), (128, BF16), (512, BF16), (512, BF16),
              (512, BF16), (128, F32), (512, BF16), (128, BF16)]
    return pl.pallas_call(
        _proj_kernel,
        grid=(BT // tm,),
        in_specs=[pl.BlockSpec((tm, D_MODEL), row), pl.BlockSpec((1, D_MODEL), fix),
                  pl.BlockSpec(w.shape, fix), pl.BlockSpec(qn.shape, fix), pl.BlockSpec(wuq.shape, fix),
                  pl.BlockSpec(kvn.shape, fix), pl.BlockSpec(wk.shape, fix), pl.BlockSpec(wv.shape, fix),
                  pl.BlockSpec((tm, LANES), pos), pl.BlockSpec((tm, LANES), pos), pl.BlockSpec((tm, LANES), pos),
                  pl.BlockSpec((tm, LANES), pos)],
        out_specs=[pl.BlockSpec((tm, wd), row) for wd, _ in widths],
        out_shape=[jax.ShapeDtypeStruct((BT, wd), dt) for wd, dt in widths],
        compiler_params=_cparams(("arbitrary",)),
        name="proj",
    )(x2, gpre, w, qn, wuq, kvn, wk, wv, rope_c, rope_sa, rope_sb, eaug)


def _compress_kernel(c_ref, pet_ref, peb_ref, w1t_ref, w1b_ref, w2_ref, o_ref):
    nch = c_ref.shape[1] // CMP_STRIDE
    c = jnp.concatenate([c_ref[0, pl.ds(j, nch, stride=CMP_STRIDE), :] for j in range(CMP_STRIDE)], axis=1)
    top = (c + pet_ref[...]).astype(BF16)
    bot = (c + peb_ref[...]).astype(BF16)
    out = jnp.zeros((nch, LANES), F32)
    for i in range(2):
        pre = _dot(top, w1t_ref[i]) + pltpu.roll(_dot(bot, w1b_ref[i]), nch - 1, 0)
        hid = pre * (1.0 / (1.0 + jnp.exp(-pre)))
        out = out + _dot(hid.astype(BF16), w2_ref[i])
    o_ref[0] = out.astype(BF16)


def _compress(tokens, pet, peb, w1t, w1b, w2):
    B, T, width = tokens.shape
    nch = T // CMP_STRIDE
    full = lambda a: pl.BlockSpec(a.shape, lambda b: (0,) * a.ndim)
    return pl.pallas_call(
        _compress_kernel,
        grid=(B,),
        in_specs=[pl.BlockSpec((1, T, width), lambda b: (b, 0, 0)),
                  full(pet), full(peb), full(w1t), full(w1b), full(w2)],
        out_specs=pl.BlockSpec((1, nch, LANES), lambda b: (b, 0, 0)),
        out_shape=jax.ShapeDtypeStruct((B, nch, LANES), BF16),
        compiler_params=_cparams(("arbitrary",)),
        name="compress",
    )(tokens, pet, peb, w1t, w1b, w2)


def _head_slabs(qblk, n_heads):
    q = qblk.astype(F32)
    lo = lax.broadcasted_iota(jnp.int32, (q.shape[0], LANES), 1) < HEAD_DIM
    out = []
    for j in range(n_heads // 2):
        slab = q[:, LANES * j:LANES * (j + 1)]
        out.append(jnp.where(lo, slab, 0.0))
        out.append(jnp.where(lo, pltpu.roll(slab, HEAD_DIM, 1), 0.0))
    return out


def _merge_upper(accs):
    lo = lax.broadcasted_iota(jnp.int32, accs[0].shape, 1) < HEAD_DIM
    slabs = [jnp.where(lo, pltpu.roll(accs[2 * j], HEAD_DIM, 1), accs[2 * j + 1]) for j in range(len(accs) // 2)]
    return jnp.concatenate(slabs, axis=1)


def _merge_normalized(accs):
    lo = lax.broadcasted_iota(jnp.int32, accs[0].shape, 1) < HEAD_DIM
    slabs = []
    for j in range(len(accs) // 2):
        a, b = accs[2 * j], accs[2 * j + 1]
        out = jnp.where(lo, pltpu.roll(a, HEAD_DIM, 1), b)
        den = jnp.where(lo, a, pltpu.roll(b, HEAD_DIM, 1))
        slabs.append(out / den)
    return jnp.concatenate(slabs, axis=1)


def _row_max_lanes(s):
    return jnp.broadcast_to(jnp.max(s, axis=-1, keepdims=True), (s.shape[0], LANES))


def _sub_lanes(s, ref):
    return jnp.concatenate([s[:, LANES * i:LANES * (i + 1)] - ref for i in range(s.shape[1] // LANES)], axis=1)


def _topk_mask_t(imp, t0):
    ns, tq = imp.shape
    srow = lax.broadcasted_iota(jnp.int32, (ns, tq), 0)
    cur = (t0 + lax.broadcasted_iota(jnp.int32, (ns, tq), 1)) // SEL_BLOCK
    forced = (srow == 0) | (srow == cur) | (srow == cur - 1)
    x = jnp.where(forced, BIG, jnp.where(srow <= cur, imp, -BIG))
    sub = 8
    groups = [x[sub * v:sub * (v + 1)] for v in range(ns // sub)]
    rows_in = [srow[sub * v:sub * (v + 1)] for v in range(ns // sub)]
    cnts = [jnp.zeros((sub, tq), F32) for _ in groups]
    for sp in range(ns):
        other = jnp.broadcast_to(x[sp:sp + 1, :], (sub, tq))
        for v, xv in enumerate(groups):
            if sub * v > sp:
                beats = other >= xv
            elif sub * v + sub - 1 < sp:
                beats = other > xv
            else:
                beats = (other > xv) | ((other == xv) & (rows_in[v] > sp))
            cnts[v] = cnts[v] + jnp.where(beats, 1.0, 0.0)
    cnt = jnp.concatenate(cnts, axis=0)
    sel = (cnt < float(min(SEL_TOP, ns))) & (srow <= cur)
    return jnp.where(sel, 0.0, NEG)


def _cmp_kernel(q_ref, kvc_ref, bc_ref, ovt_ref, ocmp_ref, selb_ref):
    rows = CMP_GROUP * Q_TILE
    ncp, ns = kvc_ref.shape[1], ovt_ref.shape[0]
    step = pl.program_id(1)
    lo_step, nse = 0, SEL_TOP
    while lo_step * rows < ns * SEL_BLOCK:
        nse = min(nse, ns)
        hi_step = nse * SEL_BLOCK // rows
        ncw = min(ncp, max(LANES, nse * (SEL_BLOCK // CMP_STRIDE)))
        if hi_step > lo_step:
            pl.when((step >= lo_step) & (step < hi_step))(
                functools.partial(_cmp_body, ncw, nse, q_ref, kvc_ref, bc_ref, ovt_ref, ocmp_ref, selb_ref))
            lo_step = hi_step
        nse *= 2


def _cmp_body(ncw, ns, q_ref, kvc_ref, bc_ref, ovt_ref, ocmp_ref, selb_ref):
    tq = Q_TILE
    kvc = kvc_ref[0, :ncw, :]
    ovt = ovt_ref[:ns, :ncw]
    groups = range(CMP_GROUP)
    sls = [slice(g * tq, (g + 1) * tq) for g in groups]
    s_alls = []
    for g in groups:
        qs = _head_slabs(q_ref[0, sls[g], :], NSA_HEADS)
        s_alls.append(_dot_nt(jnp.concatenate([q.astype(BF16) for q in qs], axis=0), kvc))
    psums, o_alls = [], []
    for g in groups:
        psum = None
        ps = []
        for h in range(NSA_HEADS):
            s = s_alls[g][h * tq:(h + 1) * tq] + bc_ref[h, sls[g], :ncw]
            m = jnp.max(s, axis=-1, keepdims=True)
            e = jnp.exp(s - m)
            l = jnp.sum(e, axis=-1, keepdims=True)
            p = e * jnp.where(m > 0.5 * NEG, 1.0 / l, 0.0)
            psum = p if psum is None else psum + p
            ps.append(p.astype(BF16))
        psums.append(psum)
        o_alls.append(_dot(jnp.concatenate(ps, axis=0), kvc))
    imps = []
    for g in groups:
        hi = psums[g].astype(BF16)
        lo = (psums[g] - hi.astype(F32)).astype(BF16)
        imps.append(_dot_nt(ovt, hi) + _dot_nt(ovt, lo))
    for g in groups:
        ocmp = _merge_upper([o_alls[g][h * tq:(h + 1) * tq] for h in range(NSA_HEADS)])
        ocmp_ref[0, sls[g], :] = ocmp.astype(ocmp_ref.dtype)
        sb = _topk_mask_t(imps[g], (pl.program_id(1) * CMP_GROUP + g) * tq)
        if ns < HEAD_DIM:
            sb = jnp.concatenate([sb, jnp.full((HEAD_DIM - ns, tq), NEG, F32)], axis=0)
        full = jnp.concatenate([jnp.zeros((HEAD_DIM, tq), F32), sb], axis=0)
        selb_ref[0, sls[g], :] = full.T.astype(BF16)


def _cmp_attention(qa, kvc, bc, ovt, T):
    B = qa.shape[0]
    ncp = kvc.shape[1]
    rows = CMP_GROUP * Q_TILE
    return pl.pallas_call(
        _cmp_kernel,
        grid=(B, T // rows),
        in_specs=[pl.BlockSpec((1, rows, 256), lambda b, n: (b, n, 0)),
                  pl.BlockSpec((1, ncp, LANES), lambda b, n: (b, 0, 0)),
                  pl.BlockSpec((NSA_HEADS, rows, ncp), lambda b, n: (0, n, 0)),
                  pl.BlockSpec(ovt.shape, lambda b, n: (0, 0))],
        out_specs=[pl.BlockSpec((1, rows, 256), lambda b, n: (b, n, 0)),
                   pl.BlockSpec((1, rows, LANES), lambda b, n: (b, n, 0))],
        out_shape=[jax.ShapeDtypeStruct((B, T, 256), BF16), jax.ShapeDtypeStruct((B, T, LANES), BF16)],
        compiler_params=_cparams(("arbitrary", "arbitrary")),
        name="cmp_select",
    )(qa, kvc, bc, ovt)


def _softmax_tile(s2, rows, m_s, shift=None):
    m_t = _row_max_lanes(s2)
    if shift is not None:
        m_t = m_t + shift
    m_old = m_s[rows]
    m_new = jnp.maximum(m_old, m_t)
    p = jnp.exp2(_sub_lanes(s2, m_new if shift is None else m_new - shift))
    alpha = jnp.exp2(m_old - m_new)
    m_s[rows] = m_new
    return p.astype(BF16), alpha


def _slc_kernel(b31_ref, q_ref, selb_ref, ka_ref, kv_ref, tab_ref, o_ref, m_s, acc_s):
    tq = q_ref.shape[1]
    H = NSA_HEADS
    n = pl.program_id(1)
    qs = _head_slabs(q_ref[0], H)
    selb = selb_ref[0].astype(F32)
    qst = jnp.concatenate([(qs[h] + selb).astype(BF16) for h in range(H)], axis=0)
    m_s[...] = jnp.full(m_s.shape, -jnp.inf, F32)
    acc_s[...] = jnp.zeros(acc_s.shape, F32)

    def scores(k0, width=tq):
        k0 = pl.multiple_of(k0, tq)
        return _dot_nt(qst, ka_ref[0, pl.ds(k0, width), :]), kv_ref[0, pl.ds(k0, width), :]

    def update(s_all, kv):
        ps, alphas = [], []
        for h in range(H):
            rows = slice(h * tq, (h + 1) * tq)
            p, alpha = _softmax_tile(s_all[rows], rows, m_s, shift=b31_ref[h])
            ps.append(p)
            alphas.append(alpha)
        pv = _dot(jnp.concatenate(ps, axis=0), kv)
        for h in range(H):
            rows = slice(h * tq, (h + 1) * tq)
            acc_s[rows] = alphas[h] * acc_s[rows] + pv[rows]

    def far_pair(i, carry):
        update(*scores(2 * i * tq, 2 * tq))
        return carry

    n_far = jnp.maximum(n - 1, 0)
    lax.fori_loop(0, n_far // 2, far_pair, 0)

    @pl.when(n_far % 2 == 1)
    def _():
        update(*scores((n_far - 1) * tq))

    def near_diagonal(k0, before):
        k0 = pl.multiple_of(k0, tq)
        half = tq // SLC_DIAG_PARTS
        parts = []
        for part in range(SLC_DIAG_PARTS):
            kw = before + half * (part + 1)
            q_part = jnp.concatenate([qst[h * tq + part * half:h * tq + (part + 1) * half] for h in range(H)], axis=0)
            parts.append((_dot_nt(q_part, ka_ref[0, pl.ds(k0, kw), :]), kv_ref[0, pl.ds(k0, kw), :], kw))
        for part, (s_all, kv, kw) in enumerate(parts):
            ps, alphas = [], []
            for h in range(H):
                rows = slice(h * tq + part * half, h * tq + (part + 1) * half)
                bias = tab_ref[h, part * half:(part + 1) * half, tq - before:tq - before + kw]
                p, alpha = _softmax_tile(s_all[h * half:(h + 1) * half] + bias, rows, m_s)
                ps.append(p)
                alphas.append(alpha)
            pv = _dot(jnp.concatenate(ps, axis=0), kv)
            for h in range(H):
                rows = slice(h * tq + part * half, h * tq + (part + 1) * half)
                acc_s[rows] = alphas[h] * acc_s[rows] + pv[h * half:(h + 1) * half]

    @pl.when(n >= 1)
    def _():
        near_diagonal(jnp.maximum(n - 1, 0) * tq, tq)

    @pl.when(n == 0)
    def _():
        near_diagonal(0, 0)

    o_ref[0] = _merge_normalized([acc_s[h * tq:(h + 1) * tq] for h in range(H)]).astype(o_ref.dtype)


def _slc_attention(b31, qa, selb, slck, slcv, tab, T):
    B = qa.shape[0]
    tq = FLASH_TILE
    H = NSA_HEADS
    tile = lambda w: pl.BlockSpec((1, tq, w), lambda b, n: (b, n, 0))
    return pl.pallas_call(
        _slc_kernel,
        grid=(B, T // tq),
        in_specs=[pl.BlockSpec(memory_space=pltpu.SMEM),
                  tile(256), tile(LANES),
                  pl.BlockSpec((1, T, LANES), lambda b, n: (b, 0, 0)),
                  pl.BlockSpec((1, T, LANES), lambda b, n: (b, 0, 0)),
                  pl.BlockSpec(tab.shape, lambda b, n: (0, 0, 0))],
        out_specs=tile(256),
        out_shape=jax.ShapeDtypeStruct((B, T, 256), BF16),
        scratch_shapes=[pltpu.VMEM((H * tq, LANES), F32), pltpu.VMEM((H * tq, LANES), F32)],
        compiler_params=_cparams(("arbitrary", "arbitrary")),
        name="slc_attention",
    )(b31, qa, selb, slck, slcv, tab)


def _band_kernel(n_heads, window, has_sink, n_sub, tq, *refs):
    if has_sink:
        q_ref, kv_ref, tab_ref, sink_ref, o_ref = refs
    else:
        q_ref, kv_ref, tab_ref, o_ref = refs
    span = window + tq
    lo_kv = lax.broadcasted_iota(jnp.int32, (span, LANES), 1) < HEAD_DIM
    lo_o = lax.broadcasted_iota(jnp.int32, (n_heads * tq, LANES), 1) < HEAD_DIM
    groups = range(n_sub)
    tiles = [pl.program_id(1) * n_sub + g for g in groups]
    kvs, dots = [], []
    for g in groups:
        k0 = pl.multiple_of(jnp.maximum(tiles[g] * tq - window, 0), tq)
        kvs.append(kv_ref[0, pl.ds(k0, span), :])
        qs = _head_slabs(q_ref[0, g * tq:(g + 1) * tq, :], n_heads)
        dots.append(_dot_nt(jnp.concatenate([q.astype(BF16) for q in qs], axis=0), kvs[g]))
    accs, ms = [], []
    for g in groups:
        s = dots[g] + tab_ref[jnp.minimum(tiles[g], window // tq)]
        m = _row_max_lanes(s)
        if has_sink:
            m = jnp.maximum(m, sink_ref[...])
        e = jnp.exp(_sub_lanes(s, m))
        ms.append(m)
        accs.append(_dot(e.astype(BF16), jnp.where(lo_kv, 1.0, kvs[g]).astype(BF16)))
    for g in groups:
        acc = accs[g]
        if has_sink:
            acc = acc + jnp.where(lo_o, jnp.exp(sink_ref[...] - ms[g]), 0.0)
        out = _merge_normalized([acc[h * tq:(h + 1) * tq] for h in range(n_heads)])
        o_ref[0, g * tq:(g + 1) * tq, :] = out.astype(o_ref.dtype)


def _band_attention(q, kv_pad, tab, sinks, n_heads, window, T, out_dtype):
    B = q.shape[0]
    width = n_heads * HEAD_DIM
    has_sink = sinks is not None
    nvar, tq = tab.shape[1], tab.shape[2]
    tab2 = jnp.transpose(tab, (1, 0, 2, 3)).reshape(nvar, n_heads * tq, window + tq)
    n_sub = BAND_ROWS // (n_heads * tq)
    rows = n_sub * tq
    in_specs = [pl.BlockSpec((1, rows, width), lambda b, n: (b, n, 0)),
                pl.BlockSpec((1, T, LANES), lambda b, n: (b, 0, 0)),
                pl.BlockSpec(tab2.shape, lambda b, n: (0, 0, 0))]
    args = (q, kv_pad, tab2)
    if has_sink:
        rep = jnp.broadcast_to(jnp.repeat(sinks, tq)[:, None], (n_heads * tq, LANES))
        in_specs.append(pl.BlockSpec(rep.shape, lambda b, n: (0, 0)))
        args = args + (rep,)
    return pl.pallas_call(
        functools.partial(_band_kernel, n_heads, window, has_sink, n_sub, tq),
        grid=(B, T // rows),
        in_specs=in_specs,
        out_specs=pl.BlockSpec((1, rows, width), lambda b, n: (b, n, 0)),
        out_shape=jax.ShapeDtypeStruct((B, T, width), out_dtype),
        compiler_params=_cparams(("arbitrary", "arbitrary")),
        name="band_sink" if has_sink else "band_window",
    )(*args)


def _mla_kernel(q_ref, k_ref, v_ref, o_ref, m_s, acc_s):
    tq = q_ref.shape[1]
    H = MLA_HEADS
    n = pl.program_id(1)
    m_s[...] = jnp.full(m_s.shape, -jnp.inf, F32)
    acc_s[...] = jnp.zeros(acc_s.shape, F32)

    def scores(k0, width=tq):
        k0 = pl.multiple_of(k0, tq)
        return k0, width, [_dot_nt(q_ref[0, :, LANES * h:LANES * (h + 1)],
                                   k_ref[0, pl.ds(k0, width), LANES * h:LANES * (h + 1)]) for h in range(H)]

    def update(k0, width, ss):
        ps, alphas = [], []
        for h in range(H):
            p, alpha = _softmax_tile(ss[h], slice(h * tq, (h + 1) * tq), m_s)
            ps.append(p)
            alphas.append(alpha)
        for h in range(H):
            rows = slice(h * tq, (h + 1) * tq)
            v = v_ref[0, pl.ds(k0, width), LANES * h:LANES * (h + 1)]
            acc_s[rows] = alphas[h] * acc_s[rows] + _dot(ps[h], v)

    def far_pair(i, carry):
        update(*scores(2 * i * tq, 2 * tq))
        return carry

    n_far = jnp.maximum(n - 1, 0)
    lax.fori_loop(0, n_far // 2, far_pair, 0)

    @pl.when(n_far % 2 == 1)
    def _():
        update(*scores((n_far - 1) * tq))

    def near_diagonal(k0, before):
        k0 = pl.multiple_of(k0, tq)
        half = tq // 2
        parts = []
        for part in range(2):
            kw = before + half * (part + 1)
            r = slice(part * half, (part + 1) * half)
            parts.append((kw, [_dot_nt(q_ref[0, r, LANES * h:LANES * (h + 1)],
                                       k_ref[0, pl.ds(k0, kw), LANES * h:LANES * (h + 1)]) for h in range(H)]))
        for part, (kw, ss) in enumerate(parts):
            row = lax.broadcasted_iota(jnp.int32, (half, kw), 0)
            col = lax.broadcasted_iota(jnp.int32, (half, kw), 1)
            causal = col <= row + (before + part * half)
            ps, alphas = [], []
            for h in range(H):
                rows = slice(h * tq + part * half, h * tq + (part + 1) * half)
                p, alpha = _softmax_tile(jnp.where(causal, ss[h], NEG), rows, m_s)
                ps.append(p)
                alphas.append(alpha)
            for h in range(H):
                rows = slice(h * tq + part * half, h * tq + (part + 1) * half)
                v = v_ref[0, pl.ds(k0, kw), LANES * h:LANES * (h + 1)]
                acc_s[rows] = alphas[h] * acc_s[rows] + _dot(ps[h], v)

    @pl.when(n >= 1)
    def _():
        near_diagonal(jnp.maximum(n - 1, 0) * tq, tq)

    @pl.when(n == 0)
    def _():
        near_diagonal(0, 0)

    o_ref[0] = _merge_normalized([acc_s[h * tq:(h + 1) * tq] for h in range(H)]).astype(o_ref.dtype)


def _mla_attention(qm, km, vm, T):
    B = qm.shape[0]
    tq = FLASH_TILE
    H = MLA_HEADS
    return pl.pallas_call(
        _mla_kernel,
        grid=(B, T // tq),
        in_specs=[pl.BlockSpec((1, tq, 512), lambda b, n: (b, n, 0)),
                  pl.BlockSpec((1, T, 512), lambda b, n: (b, 0, 0)),
                  pl.BlockSpec((1, T, 512), lambda b, n: (b, 0, 0))],
        out_specs=pl.BlockSpec((1, tq, 256), lambda b, n: (b, n, 0)),
        out_shape=jax.ShapeDtypeStruct((B, T, 256), BF16),
        scratch_shapes=[pltpu.VMEM((H * tq, LANES), F32), pltpu.VMEM((H * tq, LANES), F32)],
        compiler_params=_cparams(("arbitrary", "arbitrary")),
        name="mla_attention",
    )(qm, km, vm)


def _out_kernel(x_ref, gpre_ref, wz_ref, ocmp_ref, oslc_ref, owin_ref, krb_ref, gexp_ref, ob_ref, oc_ref, w_ref,
                gpost_ref, o_ref):
    sub = x_ref.shape[0] // OUT_SUBTILES
    rs = [slice(t * sub, (t + 1) * sub) for t in range(OUT_SUBTILES)]
    w_a = NSA_HEADS * HEAD_DIM
    hs = [_rms(x_ref[r, :], gpre_ref[...]).astype(BF16) for r in rs]
    zs = [_dot(h, wz_ref[...]) for h in hs]
    mixeds = []
    for r, z in zip(rs, zs):
        sig = 1.0 / (1.0 + jnp.exp(-krb_ref[r, :]))
        hi = sig.astype(BF16)
        lo = (sig - hi.astype(F32)).astype(BF16)
        gates = _dot(hi, gexp_ref[...]) + _dot(lo, gexp_ref[...])
        oa = None
        for j, br in enumerate((ocmp_ref, oslc_ref, owin_ref)):
            term = gates[:, w_a * j:w_a * (j + 1)] * br[r, :].astype(F32)
            oa = term if oa is None else oa + term
        heads = jnp.concatenate([oa, ob_ref[r, :].astype(F32), oc_ref[r, :].astype(F32)], axis=1)
        mixeds.append((heads * (z * (1.0 / (1.0 + jnp.exp(-z))))).astype(BF16))
    ys = [_dot(m, w_ref[...]) for m in mixeds]
    for r, y in zip(rs, ys):
        o_ref[r, :] = x_ref[r, :] + _rms(y, gpost_ref[...])


def _out_project(x2, gpre, wz, ocmp, oslc, owin, krb, gexp, ob, oc, w, gpost, tm):
    BT = x2.shape[0]
    row = lambda i: (i, 0)
    fix = lambda i: (0, 0)
    spec = lambda a: pl.BlockSpec((tm, a.shape[1]), row)
    full = lambda a: pl.BlockSpec(a.shape, fix)
    return pl.pallas_call(
        _out_kernel,
        grid=(BT // tm,),
        in_specs=[spec(x2), full(gpre), full(wz), spec(ocmp), spec(oslc), spec(owin), spec(krb), full(gexp),
                  spec(ob), spec(oc), full(w), full(gpost)],
        out_specs=pl.BlockSpec((tm, D_MODEL), row),
        out_shape=jax.ShapeDtypeStruct((BT, D_MODEL), F32),
        compiler_params=_cparams(("arbitrary",)),
        name="out_proj",
    )(x2, gpre, wz, ocmp, oslc, owin, krb, gexp, ob, oc, w, gpost)


def _split_in(w):
    outs, o = [], 0
    for s in IN_SIZES:
        outs.append(w[..., o:o + s])
        o += s
    return outs


def _relayout_kernel(w_ref, wp_ref, wz_ref):
    a_q, a_kv, a_g, b_cq, b_ckv, b_kr, c_q, c_kv, z = _split_in(w_ref[0])
    zeros = lambda n: jnp.zeros((a_q.shape[0], n), F32)
    krb = jnp.concatenate([zeros(HEAD_DIM), b_kr, a_g, zeros(LANES - HEAD_DIM - MLA_ROPE - 3 * NSA_HEADS)], axis=-1)
    wp_ref[0] = jnp.concatenate([a_q, a_kv, b_cq, b_ckv, krb, c_q, c_kv], axis=-1).astype(BF16)
    wz_ref[0] = z.astype(BF16)


def _prep_w_in(w):
    L, d, n = w.shape
    rt = 256
    return pl.pallas_call(
        _relayout_kernel,
        grid=(L, d // rt),
        in_specs=[pl.BlockSpec((1, rt, n), lambda l, r: (l, r, 0))],
        out_specs=[pl.BlockSpec((1, rt, C_Z), lambda l, r: (l, r, 0)),
                   pl.BlockSpec((1, rt, IN_SIZES[-1]), lambda l, r: (l, r, 0))],
        out_shape=[jax.ShapeDtypeStruct((L, d, C_Z), BF16), jax.ShapeDtypeStruct((L, d, IN_SIZES[-1]), BF16)],
        compiler_params=_cparams(("arbitrary", "arbitrary")),
        name="w_in_relayout",
    )(w)


def _prep_mla(w_uq, w_ukv):
    L = w_uq.shape[0]
    dq = MLA_NOPE + MLA_ROPE
    uq = w_uq.reshape(L, MLA_Q_RANK, MLA_HEADS, dq)
    uq = jnp.concatenate([uq, jnp.zeros((L, MLA_Q_RANK, MLA_HEADS, LANES - dq), uq.dtype)], axis=-1)
    ukv = w_ukv.reshape(L, MLA_KV_RANK, MLA_HEADS, MLA_NOPE + MLA_V)
    uk = jnp.concatenate([ukv[..., :MLA_NOPE], jnp.zeros((L, MLA_KV_RANK, MLA_HEADS, LANES - MLA_NOPE), ukv.dtype)], axis=-1)
    uv = jnp.concatenate([jnp.zeros((L, MLA_KV_RANK, MLA_HEADS, LANES - MLA_V), ukv.dtype), ukv[..., MLA_NOPE:]], axis=-1)
    return (uq.reshape(L, MLA_Q_RANK, MLA_HEADS * LANES).astype(BF16),
            uk.reshape(L, MLA_KV_RANK, MLA_HEADS * LANES).astype(BF16),
            uv.reshape(L, MLA_KV_RANK, MLA_HEADS * LANES).astype(BF16))


def _prep_compress(cmp_pos, cmp_w1, cmp_w2):
    L = cmp_pos.shape[0]
    pe = jnp.concatenate([cmp_pos[:, 0], cmp_pos[:, 1]], axis=-1)
    pet = pe[:, :CMP_STRIDE].reshape(L, 1, CMP_STRIDE * LANES)
    peb = pe[:, CMP_STRIDE:].reshape(L, 1, CMP_STRIDE * LANES)
    w1 = cmp_w1.reshape(L, 2, CMP_LEN, HEAD_DIM, CMP_HIDDEN)
    zero = jnp.zeros_like(w1[:, 0])
    w1k = jnp.concatenate([w1[:, 0], zero], axis=2)
    w1v = jnp.concatenate([zero, w1[:, 1]], axis=2)
    w1e = jnp.stack([w1k, w1v], axis=1)
    w1t = w1e[:, :, :CMP_STRIDE].reshape(L, 2, CMP_STRIDE * LANES, CMP_HIDDEN).astype(BF16)
    w1b = w1e[:, :, CMP_STRIDE:].reshape(L, 2, CMP_STRIDE * LANES, CMP_HIDDEN).astype(BF16)
    z2 = jnp.zeros_like(cmp_w2[:, 0])
    w2 = jnp.stack([jnp.concatenate([cmp_w2[:, 0], z2], axis=-1),
                    jnp.concatenate([z2, cmp_w2[:, 1]], axis=-1)], axis=1).astype(BF16)
    return pet, peb, w1t, w1b, w2


def _rope_tables(T):
    half = MLA_ROPE // 2
    inv = ROPE_THETA ** (-jnp.arange(half, dtype=F32) / half)
    ang = jnp.arange(T).astype(F32)[:, None] * inv[None, :]
    cos, sin = jnp.cos(ang), jnp.sin(ang)
    z = lambda n: jnp.zeros((T, n), F32)
    rc = jnp.concatenate([jnp.ones((T, MLA_NOPE), F32), cos, cos, z(LANES - MLA_NOPE - MLA_ROPE)], axis=1)
    rsa = jnp.concatenate([z(MLA_NOPE + half), sin, z(LANES - MLA_NOPE - MLA_ROPE)], axis=1)
    rsb = jnp.concatenate([z(MLA_NOPE), -sin, z(LANES - MLA_NOPE - half)], axis=1)
    return rc, rsa, rsb


def _static_tables(T):
    ns = T // SEL_BLOCK
    nc = T // CMP_STRIDE - 1
    ncp = T // CMP_STRIDE
    sstart = np.arange(ns) * SEL_BLOCK
    cstart = np.arange(nc) * CMP_STRIDE
    overlap = (np.clip(np.minimum(cstart[:, None] + CMP_LEN, sstart[None, :] + SEL_BLOCK)
                       - np.maximum(cstart[:, None], sstart[None, :]), 0, None) / CMP_STRIDE).astype(np.float32)
    ovt = np.zeros((ns, ncp), np.float32)
    ovt[:, :nc] = overlap.T
    eaug = np.zeros((T, LANES), np.float32)
    eaug[np.arange(T), HEAD_DIM + np.arange(T) // SEL_BLOCK] = 1.0
    gexp = np.zeros((LANES, 3 * NSA_HEADS * HEAD_DIM), np.float32)
    for h in range(NSA_HEADS):
        for j in range(3):
            c0 = NSA_HEADS * HEAD_DIM * j + HEAD_DIM * h
            gexp[GATE_LANE0 + 3 * h + j, c0:c0 + HEAD_DIM] = 1.0
    return jnp.asarray(ovt, BF16), jnp.asarray(eaug, BF16), jnp.asarray(gexp, BF16)


def kernel(x, w_in, w_out, norm_pre, norm_post, cmp_pos, cmp_w1, cmp_w2, mla_q_norm, mla_w_uq, mla_kv_norm,
           mla_w_ukv, swa_sinks, rel_bias):
    B, T, D = x.shape
    depth = w_in.shape[0]
    assert D == D_MODEL and T % (2 * FLASH_TILE) == 0 and T // SEL_BLOCK <= HEAD_DIM
    ncp = T // CMP_STRIDE
    tm = 512

    w_in_p, w_z = _prep_w_in(w_in)
    wuq, wk, wv = _prep_mla(mla_w_uq, mla_w_ukv)
    pet, peb, w1t, w1b, w2 = _prep_compress(cmp_pos, cmp_w1, cmp_w2)
    w_out_b = w_out.astype(BF16)
    rope_c, rope_sa, rope_sb = _rope_tables(T)
    ovt, eaug, gexp = _static_tables(T)

    bc = _bias_table(rel_bias, 0, NSA_HEADS, T, ncp, -CMP_STRIDE, -(CMP_LEN - 1), 0, 1 << 30, 256).reshape(NSA_HEADS, T, ncp)
    tab_near = _bias_table(rel_bias, 0, NSA_HEADS, FLASH_TILE, 2 * FLASH_TILE, -1, FLASH_TILE, 0, 1 << 30, 256,
                           mult=LOG2E).reshape(NSA_HEADS, FLASH_TILE, 2 * FLASH_TILE)
    tab_win = _bias_table(rel_bias, 0, NSA_HEADS, WIN_TILE, NSA_WINDOW + WIN_TILE, -1, 0, 0, NSA_WINDOW, WIN_TILE,
                          nblk=NSA_WINDOW // WIN_TILE + 1, blk_d0=WIN_TILE)
    tab_swa = _bias_table(rel_bias, NSA_HEADS, SWA_HEADS, Q_TILE, SWA_WINDOW + Q_TILE, -1, 0, 0, SWA_WINDOW, Q_TILE,
                          nblk=SWA_WINDOW // Q_TILE + 1, blk_d0=Q_TILE)
    b31 = rel_bias[:NSA_HEADS, REL_BUCKETS - 1] * LOG2E

    x2 = x.reshape(B * T, D)
    for l in range(depth):
        (qa, qa2, cmpkv, slck, slcv, win, qm, km, vm, krb, cq, ckv) = _project(
            x2, norm_pre[l][None], w_in_p[l], mla_q_norm[l][None], wuq[l], mla_kv_norm[l][None], wk[l], wv[l],
            rope_c, rope_sa, rope_sb, eaug, T, PROJ_SUBTILES * tm)
        r3 = lambda a: a.reshape(B, T, a.shape[-1])
        kvc = _compress(r3(cmpkv), pet[l], peb[l], w1t[l], w1b[l], w2[l])
        qa3 = r3(qa)
        ocmp, selb = _cmp_attention(qa3, kvc, bc, ovt, T)
        owin = _band_attention(qa3, r3(win), tab_win, None, NSA_HEADS, NSA_WINDOW, T, BF16)
        oslc = _slc_attention(b31, r3(qa2), selb, r3(slck), r3(slcv), tab_near, T)
        oc = _band_attention(r3(cq), r3(ckv), tab_swa, swa_sinks[l], SWA_HEADS, SWA_WINDOW, T, BF16)
        ob = _mla_attention(r3(qm), r3(km), r3(vm), T)
        flat = lambda a: a.reshape(B * T, a.shape[-1])
        x2 = _out_project(x2, norm_pre[l][None], w_z[l], flat(ocmp), flat(oslc), flat(owin), krb, gexp,
                          flat(ob), flat(oc), w_out_b[l], norm_post[l][None], OUT_ROWS)
    return x2.reshape(B, T, D)
```

```python
import functools
import math

import numpy as np
import jax
import jax.numpy as jnp
from jax import lax
from jax.experimental import pallas as pl
from jax.experimental.pallas import tpu as pltpu

F32 = jnp.float32
BF16 = jnp.bfloat16

D_MODEL = 1024
HEAD_DIM = 64
NSA_HEADS = 4
CMP_LEN = 32
CMP_STRIDE = 16
CMP_HIDDEN = 128
SEL_BLOCK = 64
SEL_TOP = 16
NSA_WINDOW = 512
MLA_HEADS = 4
MLA_Q_RANK = 256
MLA_KV_RANK = 128
MLA_NOPE = 64
MLA_ROPE = 32
MLA_V = 64
ROPE_THETA = 10000.0
SWA_HEADS = 8
SWA_WINDOW = 128
REL_BUCKETS = 32
REL_MAX_DIST = 512
NORM_EPS = 1e-6
NEG = -1e30
BIG = 1e9
LOG2E = math.log2(math.e)
MLA_SCALE2 = (MLA_NOPE + MLA_ROPE) ** -0.5 * LOG2E
IN_SIZES = (256, 384, 12, 256, 128, 32, 512, 128, 1024)

LANES = 128
Q_TILE = 128
WIN_TILE = 128
BAND_ROWS = 4096
PROJ_SUBTILES = 1
OUT_SUBTILES = 4
OUT_ROWS = 1024
CMP_GROUP = 4
FLASH_TILE = 512
SLC_DIAG_PARTS = 4
MLA_DIAG_PARTS = 4
GATE_LANE0 = 96
VMEM_LIMIT = 56 * 1024 * 1024

C_AQ, C_CMP, C_SLC, C_WIN, C_BCQ, C_BCKV, C_KRB, C_CQ, C_CKV, C_Z = (
    0, 256, 384, 512, 640, 896, 1024, 1152, 1664, 1792)


def _dot(a, b):
    return jnp.dot(a, b, preferred_element_type=F32)


def _dot_nt(a, b):
    return lax.dot_general(a, b, (((1,), (1,)), ((), ())), preferred_element_type=F32)


def _bucket_thresholds():
    d = np.arange(0, 4 * REL_MAX_DIST)
    exact = REL_BUCKETS // 2
    large = exact + (np.log(np.maximum(d, 1).astype(np.float32) / np.float32(exact))
                     / np.float32(math.log(REL_MAX_DIST / exact)) * np.float32(REL_BUCKETS - exact)).astype(np.int32)
    b = np.where(d < exact, d, np.minimum(large, REL_BUCKETS - 1))
    assert np.all(np.diff(b) >= 0) and b[-1] == REL_BUCKETS - 1
    return [int(np.argmax(b >= k)) for k in range(REL_BUCKETS)]


_THRESH = _bucket_thresholds()


def _cparams(sem):
    return pltpu.CompilerParams(dimension_semantics=sem, vmem_limit_bytes=VMEM_LIMIT)


def _table_kernel(h0, row_step, sign, d0, blk_d0, lo, hi, mult, rb_ref, out_ref):
    h = pl.program_id(0) + h0
    blk = pl.program_id(1)
    rt, cc = out_ref.shape[2], out_ref.shape[3]
    base = d0 + blk * blk_d0 - sign * row_step * rt * pl.program_id(2)
    width = -(-(row_step * rt + cc) // LANES) * LANES
    w = lax.broadcasted_iota(jnp.int32, (8, width), 1)
    dist = base + sign * jnp.where(w < cc, w, w - width)
    acc = jnp.full(dist.shape, rb_ref[h, 0], F32)
    for k in range(1, REL_BUCKETS):
        acc = jnp.where(dist >= _THRESH[k], rb_ref[h, k], acc)
    row = jnp.where((dist >= lo) & (dist < hi), acc * mult, NEG)
    rolled = pltpu.roll(jnp.broadcast_to(row[:1], (rt, width)), 0, 1, stride=row_step, stride_axis=0)
    out_ref[0, 0] = rolled[:, :cc]


def _bias_table(rel_bias, h0, nh, rows, cols, row_step, sign, d0, lo, hi, rt, mult=1.0, nblk=1, blk_d0=0):
    return pl.pallas_call(
        functools.partial(_table_kernel, h0, row_step, sign, d0, blk_d0, lo, hi, mult),
        grid=(nh, nblk, rows // rt),
        in_specs=[pl.BlockSpec(memory_space=pltpu.SMEM)],
        out_specs=pl.BlockSpec((1, 1, rt, cols), lambda h, b, r: (h, b, r, 0)),
        out_shape=jax.ShapeDtypeStruct((nh, nblk, rows, cols), F32),
        compiler_params=_cparams(("arbitrary", "arbitrary", "arbitrary")),
        name="bias_table",
    )(rel_bias)


def _rms(v, g):
    return v * lax.rsqrt(jnp.mean(v * v, axis=-1, keepdims=True) + NORM_EPS) * g


def _proj_kernel(x_ref, gpre_ref, w_ref, qn_ref, wuq_ref, kvn_ref, wk_ref, wv_ref,
                 rc_ref, rsa_ref, rsb_ref, eaug_ref,
                 qa_ref, qa2_ref, cmp_ref, slck_ref, slcv_ref, win_ref, qm_ref, km_ref, vm_ref,
                 krb_ref, cq_ref, ckv_ref):
    sub = x_ref.shape[0] // PROJ_SUBTILES
    rs = [slice(t * sub, (t + 1) * sub) for t in range(PROJ_SUBTILES)]
    hbs = [_rms(x_ref[r, :], gpre_ref[...]).astype(BF16) for r in rs]
    bounds = ((C_AQ, C_CMP), (C_CMP, C_WIN), (C_WIN, C_KRB), (C_KRB, C_Z))
    chunk_sets = [{ab: _dot(hb, w_ref[:, ab[0]:ab[1]]) for ab in bounds} for hb in hbs]
    lane = lax.broadcasted_iota(jnp.int32, (sub, LANES), 1)
    lo = lane < HEAD_DIM

    for r, chunks in zip(rs, chunk_sets):
        def proj(c0, c1):
            for (a, b), y in chunks.items():
                if a <= c0 and c1 <= b:
                    return y[:, c0 - a:c1 - a]
            raise AssertionError((c0, c1))

        qa_ref[r, :] = (proj(C_AQ, C_CMP) * 0.125).astype(BF16)
        qa2_ref[r, :] = (proj(C_AQ, C_CMP) * (0.125 * LOG2E)).astype(BF16)
        cmp_ref[r, :] = proj(C_CMP, C_SLC)
        slc = proj(C_SLC, C_WIN)
        slcv_ref[r, :] = jnp.where(lo, 1.0, slc).astype(BF16)
        slck_ref[r, :] = jnp.where(lo, slc, eaug_ref[r, :].astype(F32)).astype(BF16)
        win_ref[r, :] = proj(C_WIN, C_BCQ).astype(BF16)
        cq_ref[r, :] = (proj(C_CQ, C_CKV) * 0.125).astype(BF16)
        ckv_ref[r, :] = proj(C_CKV, C_Z).astype(BF16)

        rc, rsa, rsb = rc_ref[r, :], rsa_ref[r, :], rsb_ref[r, :]

        def rope(v):
            return v * rc + pltpu.roll(v, 16, 1) * rsa + pltpu.roll(v, 112, 1) * rsb

        cqn = _rms(proj(C_BCQ, C_BCKV), qn_ref[...]).astype(BF16)
        qm = _dot(cqn, wuq_ref[...])
        for h in range(MLA_HEADS):
            qh = rope(qm[:, LANES * h:LANES * (h + 1)]) * MLA_SCALE2
            qm_ref[r, LANES * h:LANES * (h + 1)] = qh.astype(BF16)

        krb = proj(C_KRB, C_CQ)
        krb_ref[r, :] = krb
        krr = jnp.where(lo | (lane >= HEAD_DIM + MLA_ROPE), 0.0, rope(krb))

        ckvn = _rms(proj(C_BCKV, C_KRB), kvn_ref[...]).astype(BF16)
        kn = _dot(ckvn, wk_ref[...])
        for h in range(MLA_HEADS):
            km_ref[r, LANES * h:LANES * (h + 1)] = (kn[:, LANES * h:LANES * (h + 1)] + krr).astype(BF16)
        vm = _dot(ckvn, wv_ref[...])
        for h in range(MLA_HEADS):
            vm_ref[r, LANES * h:LANES * (h + 1)] = jnp.where(lo, 1.0, vm[:, LANES * h:LANES * (h + 1)]).astype(BF16)


def _project(x2, gpre, w, qn, wuq, kvn, wk, wv, rope_c, rope_sa, rope_sb, eaug, T, tm):
    BT = x2.shape[0]
    nt = T // tm
    row = lambda i: (i, 0)
    fix = lambda i: (0, 0)
    pos = lambda i: (i % nt, 0)
    widths = [(256, BF16), (256, BF16), (128, F32), (128, BF16), (128, BF16), (128, BF16), (512, BF16), (512, BF16),
              (512, BF16), (128, F32), (512, BF16), (128, BF16)]
    return pl.pallas_call(
        _proj_kernel,
        grid=(BT // tm,),
        in_specs=[pl.BlockSpec((tm, D_MODEL), row), pl.BlockSpec((1, D_MODEL), fix),
                  pl.BlockSpec(w.shape, fix), pl.BlockSpec(qn.shape, fix), pl.BlockSpec(wuq.shape, fix),
                  pl.BlockSpec(kvn.shape, fix), pl.BlockSpec(wk.shape, fix), pl.BlockSpec(wv.shape, fix),
                  pl.BlockSpec((tm, LANES), pos), pl.BlockSpec((tm, LANES), pos), pl.BlockSpec((tm, LANES), pos),
                  pl.BlockSpec((tm, LANES), pos)],
        out_specs=[pl.BlockSpec((tm, wd), row) for wd, _ in widths],
        out_shape=[jax.ShapeDtypeStruct((BT, wd), dt) for wd, dt in widths],
        compiler_params=_cparams(("arbitrary",)),
        name="proj",
    )(x2, gpre, w, qn, wuq, kvn, wk, wv, rope_c, rope_sa, rope_sb, eaug)


def _compress_kernel(c_ref, pet_ref, peb_ref, w1t_ref, w1b_ref, w2_ref, o_ref):
    nch = c_ref.shape[1] // CMP_STRIDE
    c = jnp.concatenate([c_ref[0, pl.ds(j, nch, stride=CMP_STRIDE), :] for j in range(CMP_STRIDE)], axis=1)
    top = (c + pet_ref[...]).astype(BF16)
    bot = (c + peb_ref[...]).astype(BF16)
    out = jnp.zeros((nch, LANES), F32)
    for i in range(2):
        pre = _dot(top, w1t_ref[i]) + pltpu.roll(_dot(bot, w1b_ref[i]), nch - 1, 0)
        hid = pre * (1.0 / (1.0 + jnp.exp(-pre)))
        out = out + _dot(hid.astype(BF16), w2_ref[i])
    o_ref[0] = out.astype(BF16)


def _compress(tokens, pet, peb, w1t, w1b, w2):
    B, T, width = tokens.shape
    nch = T // CMP_STRIDE
    full = lambda a: pl.BlockSpec(a.shape, lambda b: (0,) * a.ndim)
    return pl.pallas_call(
        _compress_kernel,
        grid=(B,),
        in_specs=[pl.BlockSpec((1, T, width), lambda b: (b, 0, 0)),
                  full(pet), full(peb), full(w1t), full(w1b), full(w2)],
        out_specs=pl.BlockSpec((1, nch, LANES), lambda b: (b, 0, 0)),
        out_shape=jax.ShapeDtypeStruct((B, nch, LANES), BF16),
        compiler_params=_cparams(("arbitrary",)),
        name="compress",
    )(tokens, pet, peb, w1t, w1b, w2)


def _head_slabs(qblk, n_heads):
    q = qblk.astype(F32)
    lo = lax.broadcasted_iota(jnp.int32, (q.shape[0], LANES), 1) < HEAD_DIM
    out = []
    for j in range(n_heads // 2):
        slab = q[:, LANES * j:LANES * (j + 1)]
        out.append(jnp.where(lo, slab, 0.0))
        out.append(jnp.where(lo, pltpu.roll(slab, HEAD_DIM, 1), 0.0))
    return out


def _merge_upper(accs):
    lo = lax.broadcasted_iota(jnp.int32, accs[0].shape, 1) < HEAD_DIM
    slabs = [jnp.where(lo, pltpu.roll(accs[2 * j], HEAD_DIM, 1), accs[2 * j + 1]) for j in range(len(accs) // 2)]
    return jnp.concatenate(slabs, axis=1)


def _merge_normalized(accs):
    lo = lax.broadcasted_iota(jnp.int32, accs[0].shape, 1) < HEAD_DIM
    slabs = []
    for j in range(len(accs) // 2):
        a, b = accs[2 * j], accs[2 * j + 1]
        out = jnp.where(lo, pltpu.roll(a, HEAD_DIM, 1), b)
        den = jnp.where(lo, a, pltpu.roll(b, HEAD_DIM, 1))
        slabs.append(out / den)
    return jnp.concatenate(slabs, axis=1)


def _row_max_lanes(s):
    return jnp.broadcast_to(jnp.max(s, axis=-1, keepdims=True), (s.shape[0], LANES))


def _sub_lanes(s, ref):
    return jnp.concatenate([s[:, LANES * i:LANES * (i + 1)] - ref for i in range(s.shape[1] // LANES)], axis=1)


def _topk_mask_t(imp, t0):
    ns, tq = imp.shape
    srow = lax.broadcasted_iota(jnp.int32, (ns, tq), 0)
    cur = (t0 + lax.broadcasted_iota(jnp.int32, (ns, tq), 1)) // SEL_BLOCK
    forced = (srow == 0) | (srow == cur) | (srow == cur - 1)
    x = jnp.where(forced, BIG, jnp.where(srow <= cur, imp, -BIG))
    sub = 8
    groups = [x[sub * v:sub * (v + 1)] for v in range(ns // sub)]
    rows_in = [srow[sub * v:sub * (v + 1)] for v in range(ns // sub)]
    cnts = [jnp.zeros((sub, tq), F32) for _ in groups]
    for sp in range(ns):
        other = jnp.broadcast_to(x[sp:sp + 1, :], (sub, tq))
        for v, xv in enumerate(groups):
            if sub * v > sp:
                beats = other >= xv
            elif sub * v + sub - 1 < sp:
                beats = other > xv
            else:
                beats = (other > xv) | ((other == xv) & (rows_in[v] > sp))
            cnts[v] = cnts[v] + jnp.where(beats, 1.0, 0.0)
    cnt = jnp.concatenate(cnts, axis=0)
    sel = (cnt < float(min(SEL_TOP, ns))) & (srow <= cur)
    return jnp.where(sel, 0.0, NEG)


def _cmp_kernel(q_ref, kvc_ref, bc_ref, ovt_ref, ocmp_ref, selb_ref):
    rows = CMP_GROUP * Q_TILE
    ncp, ns = kvc_ref.shape[1], ovt_ref.shape[0]
    step = pl.program_id(1)
    lo_step, nse = 0, SEL_TOP
    while lo_step * rows < ns * SEL_BLOCK:
        nse = min(nse, ns)
        hi_step = nse * SEL_BLOCK // rows
        ncw = min(ncp, max(LANES, nse * (SEL_BLOCK // CMP_STRIDE)))
        if hi_step > lo_step:
            pl.when((step >= lo_step) & (step < hi_step))(
                functools.partial(_cmp_body, ncw, nse, q_ref, kvc_ref, bc_ref, ovt_ref, ocmp_ref, selb_ref))
            lo_step = hi_step
        nse *= 2


def _cmp_body(ncw, ns, q_ref, kvc_ref, bc_ref, ovt_ref, ocmp_ref, selb_ref):
    tq = Q_TILE
    kvc = kvc_ref[0, :ncw, :]
    ovt = ovt_ref[:ns, :ncw]
    groups = range(CMP_GROUP)
    sls = [slice(g * tq, (g + 1) * tq) for g in groups]
    s_alls = []
    for g in groups:
        qs = _head_slabs(q_ref[0, sls[g], :], NSA_HEADS)
        s_alls.append(_dot_nt(jnp.concatenate([q.astype(BF16) for q in qs], axis=0), kvc))
    psums, o_alls = [], []
    for g in groups:
        psum = None
        ps = []
        for h in range(NSA_HEADS):
            s = s_alls[g][h * tq:(h + 1) * tq] + bc_ref[h, sls[g], :ncw]
            m = jnp.max(s, axis=-1, keepdims=True)
            e = jnp.exp(s - m)
            l = jnp.sum(e, axis=-1, keepdims=True)
            p = e * jnp.where(m > 0.5 * NEG, 1.0 / l, 0.0)
            psum = p if psum is None else psum + p
            ps.append(p.astype(BF16))
        psums.append(psum)
        o_alls.append(_dot(jnp.concatenate(ps, axis=0), kvc))
    imps = []
    for g in groups:
        hi = psums[g].astype(BF16)
        lo = (psums[g] - hi.astype(F32)).astype(BF16)
        imps.append(_dot_nt(ovt, hi) + _dot_nt(ovt, lo))
    for g in groups:
        ocmp = _merge_upper([o_alls[g][h * tq:(h + 1) * tq] for h in range(NSA_HEADS)])
        ocmp_ref[0, sls[g], :] = ocmp.astype(ocmp_ref.dtype)
        sb = _topk_mask_t(imps[g], (pl.program_id(1) * CMP_GROUP + g) * tq)
        if ns < HEAD_DIM:
            sb = jnp.concatenate([sb, jnp.full((HEAD_DIM - ns, tq), NEG, F32)], axis=0)
        full = jnp.concatenate([jnp.zeros((HEAD_DIM, tq), F32), sb], axis=0)
        selb_ref[0, sls[g], :] = full.T.astype(BF16)


def _cmp_attention(qa, kvc, bc, ovt, T):
    B = qa.shape[0]
    ncp = kvc.shape[1]
    rows = CMP_GROUP * Q_TILE
    return pl.pallas_call(
        _cmp_kernel,
        grid=(B, T // rows),
        in_specs=[pl.BlockSpec((1, rows, 256), lambda b, n: (b, n, 0)),
                  pl.BlockSpec((1, ncp, LANES), lambda b, n: (b, 0, 0)),
                  pl.BlockSpec((NSA_HEADS, rows, ncp), lambda b, n: (0, n, 0)),
                  pl.BlockSpec(ovt.shape, lambda b, n: (0, 0))],
        out_specs=[pl.BlockSpec((1, rows, 256), lambda b, n: (b, n, 0)),
                   pl.BlockSpec((1, rows, LANES), lambda b, n: (b, n, 0))],
        out_shape=[jax.ShapeDtypeStruct((B, T, 256), BF16), jax.ShapeDtypeStruct((B, T, LANES), BF16)],
        compiler_params=_cparams(("arbitrary", "arbitrary")),
        name="cmp_select",
    )(qa, kvc, bc, ovt)


def _softmax_tile(s2, rows, m_s, shift=None):
    m_t = _row_max_lanes(s2)
    if shift is not None:
        m_t = m_t + shift
    m_old = m_s[rows]
    m_new = jnp.maximum(m_old, m_t)
    p = jnp.exp2(_sub_lanes(s2, m_new if shift is None else m_new - shift))
    alpha = jnp.exp2(m_old - m_new)
    m_s[rows] = m_new
    return p.astype(BF16), alpha


def _slc_kernel(b31_ref, q_ref, selb_ref, ka_ref, kv_ref, tab_ref, o_ref, m_s, acc_s):
    tq = q_ref.shape[1]
    H = NSA_HEADS
    n = pl.program_id(1)
    qs = _head_slabs(q_ref[0], H)
    selb = selb_ref[0].astype(F32)
    qst = jnp.concatenate([(qs[h] + selb).astype(BF16) for h in range(H)], axis=0)
    m_s[...] = jnp.full(m_s.shape, -jnp.inf, F32)
    acc_s[...] = jnp.zeros(acc_s.shape, F32)

    def scores(k0, width=tq):
        k0 = pl.multiple_of(k0, tq)
        return _dot_nt(qst, ka_ref[0, pl.ds(k0, width), :]), kv_ref[0, pl.ds(k0, width), :]

    def update(s_all, kv):
        ps, alphas = [], []
        for h in range(H):
            rows = slice(h * tq, (h + 1) * tq)
            p, alpha = _softmax_tile(s_all[rows], rows, m_s, shift=b31_ref[h])
            ps.append(p)
            alphas.append(alpha)
        pv = _dot(jnp.concatenate(ps, axis=0), kv)
        for h in range(H):
            rows = slice(h * tq, (h + 1) * tq)
            acc_s[rows] = alphas[h] * acc_s[rows] + pv[rows]

    def far_pair(i, carry):
        update(*scores(2 * i * tq, 2 * tq))
        return carry

    n_far = jnp.maximum(n - 1, 0)
    lax.fori_loop(0, n_far // 2, far_pair, 0)

    @pl.when(n_far % 2 == 1)
    def _():
        update(*scores((n_far - 1) * tq))

    def near_diagonal(k0, before):
        k0 = pl.multiple_of(k0, tq)
        half = tq // SLC_DIAG_PARTS
        parts = []
        for part in range(SLC_DIAG_PARTS):
            kw = before + half * (part + 1)
            q_part = jnp.concatenate([qst[h * tq + part * half:h * tq + (part + 1) * half] for h in range(H)], axis=0)
            parts.append((_dot_nt(q_part, ka_ref[0, pl.ds(k0, kw), :]), kv_ref[0, pl.ds(k0, kw), :], kw))
        for part, (s_all, kv, kw) in enumerate(parts):
            ps, alphas = [], []
            for h in range(H):
                rows = slice(h * tq + part * half, h * tq + (part + 1) * half)
                bias = tab_ref[h, part * half:(part + 1) * half, tq - before:tq - before + kw]
                p, alpha = _softmax_tile(s_all[h * half:(h + 1) * half] + bias, rows, m_s)
                ps.append(p)
                alphas.append(alpha)
            pv = _dot(jnp.concatenate(ps, axis=0), kv)
            for h in range(H):
                rows = slice(h * tq + part * half, h * tq + (part + 1) * half)
                acc_s[rows] = alphas[h] * acc_s[rows] + pv[h * half:(h + 1) * half]

    @pl.when(n >= 1)
    def _():
        near_diagonal(jnp.maximum(n - 1, 0) * tq, tq)

    @pl.when(n == 0)
    def _():
        near_diagonal(0, 0)

    o_ref[0] = _merge_normalized([acc_s[h * tq:(h + 1) * tq] for h in range(H)]).astype(o_ref.dtype)


def _slc_attention(b31, qa, selb, slck, slcv, tab, T):
    B = qa.shape[0]
    tq = FLASH_TILE
    H = NSA_HEADS
    tile = lambda w: pl.BlockSpec((1, tq, w), lambda b, n: (b, n, 0))
    return pl.pallas_call(
        _slc_kernel,
        grid=(B, T // tq),
        in_specs=[pl.BlockSpec(memory_space=pltpu.SMEM),
                  tile(256), tile(LANES),
                  pl.BlockSpec((1, T, LANES), lambda b, n: (b, 0, 0)),
                  pl.BlockSpec((1, T, LANES), lambda b, n: (b, 0, 0)),
                  pl.BlockSpec(tab.shape, lambda b, n: (0, 0, 0))],
        out_specs=tile(256),
        out_shape=jax.ShapeDtypeStruct((B, T, 256), BF16),
        scratch_shapes=[pltpu.VMEM((H * tq, LANES), F32), pltpu.VMEM((H * tq, LANES), F32)],
        compiler_params=_cparams(("arbitrary", "arbitrary")),
        name="slc_attention",
    )(b31, qa, selb, slck, slcv, tab)


def _band_kernel(n_heads, window, has_sink, n_sub, tq, *refs):
    if has_sink:
        q_ref, kv_ref, tab_ref, sink_ref, o_ref = refs
    else:
        q_ref, kv_ref, tab_ref, o_ref = refs
    span = window + tq
    lo_kv = lax.broadcasted_iota(jnp.int32, (span, LANES), 1) < HEAD_DIM
    lo_o = lax.broadcasted_iota(jnp.int32, (n_heads * tq, LANES), 1) < HEAD_DIM
    groups = range(n_sub)
    tiles = [pl.program_id(1) * n_sub + g for g in groups]
    kvs, dots = [], []
    for g in groups:
        k0 = pl.multiple_of(jnp.maximum(tiles[g] * tq - window, 0), tq)
        kvs.append(kv_ref[0, pl.ds(k0, span), :])
        qs = _head_slabs(q_ref[0, g * tq:(g + 1) * tq, :], n_heads)
        dots.append(_dot_nt(jnp.concatenate([q.astype(BF16) for q in qs], axis=0), kvs[g]))
    accs, ms = [], []
    for g in groups:
        s = dots[g] + tab_ref[jnp.minimum(tiles[g], window // tq)]
        m = _row_max_lanes(s)
        if has_sink:
            m = jnp.maximum(m, sink_ref[...])
        e = jnp.exp(_sub_lanes(s, m))
        ms.append(m)
        accs.append(_dot(e.astype(BF16), jnp.where(lo_kv, 1.0, kvs[g]).astype(BF16)))
    for g in groups:
        acc = accs[g]
        if has_sink:
            acc = acc + jnp.where(lo_o, jnp.exp(sink_ref[...] - ms[g]), 0.0)
        out = _merge_normalized([acc[h * tq:(h + 1) * tq] for h in range(n_heads)])
        o_ref[0, g * tq:(g + 1) * tq, :] = out.astype(o_ref.dtype)


def _band_attention(q, kv_pad, tab, sinks, n_heads, window, T, out_dtype):
    B = q.shape[0]
    width = n_heads * HEAD_DIM
    has_sink = sinks is not None
    nvar, tq = tab.shape[1], tab.shape[2]
    tab2 = jnp.transpose(tab, (1, 0, 2, 3)).reshape(nvar, n_heads * tq, window + tq)
    n_sub = BAND_ROWS // (n_heads * tq)
    rows = n_sub * tq
    in_specs = [pl.BlockSpec((1, rows, width), lambda b, n: (b, n, 0)),
                pl.BlockSpec((1, T, LANES), lambda b, n: (b, 0, 0)),
                pl.BlockSpec(tab2.shape, lambda b, n: (0, 0, 0))]
    args = (q, kv_pad, tab2)
    if has_sink:
        rep = jnp.broadcast_to(jnp.repeat(sinks, tq)[:, None], (n_heads * tq, LANES))
        in_specs.append(pl.BlockSpec(rep.shape, lambda b, n: (0, 0)))
        args = args + (rep,)
    return pl.pallas_call(
        functools.partial(_band_kernel, n_heads, window, has_sink, n_sub, tq),
        grid=(B, T // rows),
        in_specs=in_specs,
        out_specs=pl.BlockSpec((1, rows, width), lambda b, n: (b, n, 0)),
        out_shape=jax.ShapeDtypeStruct((B, T, width), out_dtype),
        compiler_params=_cparams(("arbitrary", "arbitrary")),
        name="band_sink" if has_sink else "band_window",
    )(*args)


def _mla_kernel(q_ref, k_ref, v_ref, o_ref, m_s, acc_s):
    tq = q_ref.shape[1]
    H = MLA_HEADS
    n = pl.program_id(1)
    m_s[...] = jnp.full(m_s.shape, -jnp.inf, F32)
    acc_s[...] = jnp.zeros(acc_s.shape, F32)

    def scores(k0, width=tq):
        k0 = pl.multiple_of(k0, tq)
        return k0, width, [_dot_nt(q_ref[0, :, LANES * h:LANES * (h + 1)],
                                   k_ref[0, pl.ds(k0, width), LANES * h:LANES * (h + 1)]) for h in range(H)]

    def update(k0, width, ss):
        ps, alphas = [], []
        for h in range(H):
            p, alpha = _softmax_tile(ss[h], slice(h * tq, (h + 1) * tq), m_s)
            ps.append(p)
            alphas.append(alpha)
        for h in range(H):
            rows = slice(h * tq, (h + 1) * tq)
            v = v_ref[0, pl.ds(k0, width), LANES * h:LANES * (h + 1)]
            acc_s[rows] = alphas[h] * acc_s[rows] + _dot(ps[h], v)

    def far_pair(i, carry):
        update(*scores(2 * i * tq, 2 * tq))
        return carry

    n_far = jnp.maximum(n - 1, 0)
    lax.fori_loop(0, n_far // 2, far_pair, 0)

    @pl.when(n_far % 2 == 1)
    def _():
        update(*scores((n_far - 1) * tq))

    def near_diagonal(k0, before):
        k0 = pl.multiple_of(k0, tq)
        half = tq // MLA_DIAG_PARTS
        parts = []
        for part in range(MLA_DIAG_PARTS):
            kw = before + half * (part + 1)
            r = slice(part * half, (part + 1) * half)
            parts.append((kw, [_dot_nt(q_ref[0, r, LANES * h:LANES * (h + 1)],
                                       k_ref[0, pl.ds(k0, kw), LANES * h:LANES * (h + 1)]) for h in range(H)]))
        for part, (kw, ss) in enumerate(parts):
            row = lax.broadcasted_iota(jnp.int32, (half, kw), 0)
            col = lax.broadcasted_iota(jnp.int32, (half, kw), 1)
            causal = col <= row + (before + part * half)
            ps, alphas = [], []
            for h in range(H):
                rows = slice(h * tq + part * half, h * tq + (part + 1) * half)
                p, alpha = _softmax_tile(jnp.where(causal, ss[h], NEG), rows, m_s)
                ps.append(p)
                alphas.append(alpha)
            for h in range(H):
                rows = slice(h * tq + part * half, h * tq + (part + 1) * half)
                v = v_ref[0, pl.ds(k0, kw), LANES * h:LANES * (h + 1)]
                acc_s[rows] = alphas[h] * acc_s[rows] + _dot(ps[h], v)

    @pl.when(n >= 1)
    def _():
        near_diagonal(jnp.maximum(n - 1, 0) * tq, tq)

    @pl.when(n == 0)
    def _():
        near_diagonal(0, 0)

    o_ref[0] = _merge_normalized([acc_s[h * tq:(h + 1) * tq] for h in range(H)]).astype(o_ref.dtype)


def _mla_attention(qm, km, vm, T):
    B = qm.shape[0]
    tq = FLASH_TILE
    H = MLA_HEADS
    return pl.pallas_call(
        _mla_kernel,
        grid=(B, T // tq),
        in_specs=[pl.BlockSpec((1, tq, 512), lambda b, n: (b, n, 0)),
                  pl.BlockSpec((1, T, 512), lambda b, n: (b, 0, 0)),
                  pl.BlockSpec((1, T, 512), lambda b, n: (b, 0, 0))],
        out_specs=pl.BlockSpec((1, tq, 256), lambda b, n: (b, n, 0)),
        out_shape=jax.ShapeDtypeStruct((B, T, 256), BF16),
        scratch_shapes=[pltpu.VMEM((H * tq, LANES), F32), pltpu.VMEM((H * tq, LANES), F32)],
        compiler_params=_cparams(("arbitrary", "arbitrary")),
        name="mla_attention",
    )(qm, km, vm)


def _out_kernel(x_ref, gpre_ref, wz_ref, ocmp_ref, oslc_ref, owin_ref, krb_ref, gexp_ref, ob_ref, oc_ref, w_ref,
                gpost_ref, o_ref):
    sub = x_ref.shape[0] // OUT_SUBTILES
    rs = [slice(t * sub, (t + 1) * sub) for t in range(OUT_SUBTILES)]
    w_a = NSA_HEADS * HEAD_DIM
    hs = [_rms(x_ref[r, :], gpre_ref[...]).astype(BF16) for r in rs]
    zs = [_dot(h, wz_ref[...]) for h in hs]
    mixeds = []
    for r, z in zip(rs, zs):
        sig = 1.0 / (1.0 + jnp.exp(-krb_ref[r, :]))
        hi = sig.astype(BF16)
        lo = (sig - hi.astype(F32)).astype(BF16)
        gates = _dot(hi, gexp_ref[...]) + _dot(lo, gexp_ref[...])
        oa = None
        for j, br in enumerate((ocmp_ref, oslc_ref, owin_ref)):
            term = gates[:, w_a * j:w_a * (j + 1)] * br[r, :].astype(F32)
            oa = term if oa is None else oa + term
        heads = jnp.concatenate([oa, ob_ref[r, :].astype(F32), oc_ref[r, :].astype(F32)], axis=1)
        mixeds.append((heads * (z * (1.0 / (1.0 + jnp.exp(-z))))).astype(BF16))
    ys = [_dot(m, w_ref[...]) for m in mixeds]
    for r, y in zip(rs, ys):
        o_ref[r, :] = x_ref[r, :] + _rms(y, gpost_ref[...])


def _out_project(x2, gpre, wz, ocmp, oslc, owin, krb, gexp, ob, oc, w, gpost, tm):
    BT = x2.shape[0]
    row = lambda i: (i, 0)
    fix = lambda i: (0, 0)
    spec = lambda a: pl.BlockSpec((tm, a.shape[1]), row)
    full = lambda a: pl.BlockSpec(a.shape, fix)
    return pl.pallas_call(
        _out_kernel,
        grid=(BT // tm,),
        in_specs=[spec(x2), full(gpre), full(wz), spec(ocmp), spec(oslc), spec(owin), spec(krb), full(gexp),
                  spec(ob), spec(oc), full(w), full(gpost)],
        out_specs=pl.BlockSpec((tm, D_MODEL), row),
        out_shape=jax.ShapeDtypeStruct((BT, D_MODEL), F32),
        compiler_params=_cparams(("arbitrary",)),
        name="out_proj",
    )(x2, gpre, wz, ocmp, oslc, owin, krb, gexp, ob, oc, w, gpost)


def _split_in(w):
    outs, o = [], 0
    for s in IN_SIZES:
        outs.append(w[..., o:o + s])
        o += s
    return outs


def _relayout_kernel(w_ref, wp_ref, wz_ref):
    a_q, a_kv, a_g, b_cq, b_ckv, b_kr, c_q, c_kv, z = _split_in(w_ref[0])
    zeros = lambda n: jnp.zeros((a_q.shape[0], n), F32)
    krb = jnp.concatenate([zeros(HEAD_DIM), b_kr, a_g, zeros(LANES - HEAD_DIM - MLA_ROPE - 3 * NSA_HEADS)], axis=-1)
    wp_ref[0] = jnp.concatenate([a_q, a_kv, b_cq, b_ckv, krb, c_q, c_kv], axis=-1).astype(BF16)
    wz_ref[0] = z.astype(BF16)


def _prep_w_in(w):
    L, d, n = w.shape
    rt = 256
    return pl.pallas_call(
        _relayout_kernel,
        grid=(L, d // rt),
        in_specs=[pl.BlockSpec((1, rt, n), lambda l, r: (l, r, 0))],
        out_specs=[pl.BlockSpec((1, rt, C_Z), lambda l, r: (l, r, 0)),
                   pl.BlockSpec((1, rt, IN_SIZES[-1]), lambda l, r: (l, r, 0))],
        out_shape=[jax.ShapeDtypeStruct((L, d, C_Z), BF16), jax.ShapeDtypeStruct((L, d, IN_SIZES[-1]), BF16)],
        compiler_params=_cparams(("arbitrary", "arbitrary")),
        name="w_in_relayout",
    )(w)


def _prep_mla(w_uq, w_ukv):
    L = w_uq.shape[0]
    dq = MLA_NOPE + MLA_ROPE
    uq = w_uq.reshape(L, MLA_Q_RANK, MLA_HEADS, dq)
    uq = jnp.concatenate([uq, jnp.zeros((L, MLA_Q_RANK, MLA_HEADS, LANES - dq), uq.dtype)], axis=-1)
    ukv = w_ukv.reshape(L, MLA_KV_RANK, MLA_HEADS, MLA_NOPE + MLA_V)
    uk = jnp.concatenate([ukv[..., :MLA_NOPE], jnp.zeros((L, MLA_KV_RANK, MLA_HEADS, LANES - MLA_NOPE), ukv.dtype)], axis=-1)
    uv = jnp.concatenate([jnp.zeros((L, MLA_KV_RANK, MLA_HEADS, LANES - MLA_V), ukv.dtype), ukv[..., MLA_NOPE:]], axis=-1)
    return (uq.reshape(L, MLA_Q_RANK, MLA_HEADS * LANES).astype(BF16),
            uk.reshape(L, MLA_KV_RANK, MLA_HEADS * LANES).astype(BF16),
            uv.reshape(L, MLA_KV_RANK, MLA_HEADS * LANES).astype(BF16))


def _prep_compress(cmp_pos, cmp_w1, cmp_w2):
    L = cmp_pos.shape[0]
    pe = jnp.concatenate([cmp_pos[:, 0], cmp_pos[:, 1]], axis=-1)
    pet = pe[:, :CMP_STRIDE].reshape(L, 1, CMP_STRIDE * LANES)
    peb = pe[:, CMP_STRIDE:].reshape(L, 1, CMP_STRIDE * LANES)
    w1 = cmp_w1.reshape(L, 2, CMP_LEN, HEAD_DIM, CMP_HIDDEN)
    zero = jnp.zeros_like(w1[:, 0])
    w1k = jnp.concatenate([w1[:, 0], zero], axis=2)
    w1v = jnp.concatenate([zero, w1[:, 1]], axis=2)
    w1e = jnp.stack([w1k, w1v], axis=1)
    w1t = w1e[:, :, :CMP_STRIDE].reshape(L, 2, CMP_STRIDE * LANES, CMP_HIDDEN).astype(BF16)
    w1b = w1e[:, :, CMP_STRIDE:].reshape(L, 2, CMP_STRIDE * LANES, CMP_HIDDEN).astype(BF16)
    z2 = jnp.zeros_like(cmp_w2[:, 0])
    w2 = jnp.stack([jnp.concatenate([cmp_w2[:, 0], z2], axis=-1),
                    jnp.concatenate([z2, cmp_w2[:, 1]], axis=-1)], axis=1).astype(BF16)
    return pet, peb, w1t, w1b, w2


def _rope_tables(T):
    half = MLA_ROPE // 2
    inv = ROPE_THETA ** (-jnp.arange(half, dtype=F32) / half)
    ang = jnp.arange(T).astype(F32)[:, None] * inv[None, :]
    cos, sin = jnp.cos(ang), jnp.sin(ang)
    z = lambda n: jnp.zeros((T, n), F32)
    rc = jnp.concatenate([jnp.ones((T, MLA_NOPE), F32), cos, cos, z(LANES - MLA_NOPE - MLA_ROPE)], axis=1)
    rsa = jnp.concatenate([z(MLA_NOPE + half), sin, z(LANES - MLA_NOPE - MLA_ROPE)], axis=1)
    rsb = jnp.concatenate([z(MLA_NOPE), -sin, z(LANES - MLA_NOPE - half)], axis=1)
    return rc, rsa, rsb


def _static_tables(T):
    ns = T // SEL_BLOCK
    nc = T // CMP_STRIDE - 1
    ncp = T // CMP_STRIDE
    sstart = np.arange(ns) * SEL_BLOCK
    cstart = np.arange(nc) * CMP_STRIDE
    overlap = (np.clip(np.minimum(cstart[:, None] + CMP_LEN, sstart[None, :] + SEL_BLOCK)
                       - np.maximum(cstart[:, None], sstart[None, :]), 0, None) / CMP_STRIDE).astype(np.float32)
    ovt = np.zeros((ns, ncp), np.float32)
    ovt[:, :nc] = overlap.T
    eaug = np.zeros((T, LANES), np.float32)
    eaug[np.arange(T), HEAD_DIM + np.arange(T) // SEL_BLOCK] = 1.0
    gexp = np.zeros((LANES, 3 * NSA_HEADS * HEAD_DIM), np.float32)
    for h in range(NSA_HEADS):
        for j in range(3):
            c0 = NSA_HEADS * HEAD_DIM * j + HEAD_DIM * h
            gexp[GATE_LANE0 + 3 * h + j, c0:c0 + HEAD_DIM] = 1.0
    return jnp.asarray(ovt, BF16), jnp.asarray(eaug, BF16), jnp.asarray(gexp, BF16)


def kernel(x, w_in, w_out, norm_pre, norm_post, cmp_pos, cmp_w1, cmp_w2, mla_q_norm, mla_w_uq, mla_kv_norm,
           mla_w_ukv, swa_sinks, rel_bias):
    B, T, D = x.shape
    depth = w_in.shape[0]
    assert D == D_MODEL and T % (2 * FLASH_TILE) == 0 and T // SEL_BLOCK <= HEAD_DIM
    ncp = T // CMP_STRIDE
    tm = 512

    w_in_p, w_z = _prep_w_in(w_in)
    wuq, wk, wv = _prep_mla(mla_w_uq, mla_w_ukv)
    pet, peb, w1t, w1b, w2 = _prep_compress(cmp_pos, cmp_w1, cmp_w2)
    w_out_b = w_out.astype(BF16)
    rope_c, rope_sa, rope_sb = _rope_tables(T)
    ovt, eaug, gexp = _static_tables(T)

    bc = _bias_table(rel_bias, 0, NSA_HEADS, ncp, T, CMP_STRIDE, 1, -(CMP_LEN - 1), 0, 1 << 30, 32)
    bc = jnp.transpose(bc.reshape(NSA_HEADS, ncp, T), (0, 2, 1))
    tab_near = _bias_table(rel_bias, 0, NSA_HEADS, FLASH_TILE, 2 * FLASH_TILE, 1, -1, FLASH_TILE, 0, 1 << 30, 256,
                           mult=LOG2E).reshape(NSA_HEADS, FLASH_TILE, 2 * FLASH_TILE)
    tab_win = _bias_table(rel_bias, 0, NSA_HEADS, WIN_TILE, NSA_WINDOW + WIN_TILE, 1, -1, 0, 0, NSA_WINDOW, WIN_TILE,
                          nblk=NSA_WINDOW // WIN_TILE + 1, blk_d0=WIN_TILE)
    tab_swa = _bias_table(rel_bias, NSA_HEADS, SWA_HEADS, Q_TILE, SWA_WINDOW + Q_TILE, 1, -1, 0, 0, SWA_WINDOW, Q_TILE,
                          nblk=SWA_WINDOW // Q_TILE + 1, blk_d0=Q_TILE)
    b31 = rel_bias[:NSA_HEADS, REL_BUCKETS - 1] * LOG2E

    x2 = x.reshape(B * T, D)
    for l in range(depth):
        (qa, qa2, cmpkv, slck, slcv, win, qm, km, vm, krb, cq, ckv) = _project(
            x2, norm_pre[l][None], w_in_p[l], mla_q_norm[l][None], wuq[l], mla_kv_norm[l][None], wk[l], wv[l],
            rope_c, rope_sa, rope_sb, eaug, T, PROJ_SUBTILES * tm)
        r3 = lambda a: a.reshape(B, T, a.shape[-1])
        kvc = _compress(r3(cmpkv), pet[l], peb[l], w1t[l], w1b[l], w2[l])
        qa3 = r3(qa)
        ocmp, selb = _cmp_attention(qa3, kvc, bc, ovt, T)
        owin = _band_attention(qa3, r3(win), tab_win, None, NSA_HEADS, NSA_WINDOW, T, BF16)
        oslc = _slc_attention(b31, r3(qa2), selb, r3(slck), r3(slcv), tab_near, T)
        oc = _band_attention(r3(cq), r3(ckv), tab_swa, swa_sinks[l], SWA_HEADS, SWA_WINDOW, T, BF16)
        ob = _mla_attention(r3(qm), r3(km), r3(vm), T)
        flat = lambda a: a.reshape(B * T, a.shape[-1])
        x2 = _out_project(x2, norm_pre[l][None], w_z[l], flat(ocmp), flat(oslc), flat(owin), krb, gexp,
                          flat(ob), flat(oc), w_out_b[l], norm_post[l][None], OUT_ROWS)
    return x2.reshape(B, T, D)
```

```python
import functools
import math

import numpy as np
import jax
import jax.numpy as jnp
from jax import lax
from jax.experimental import pallas as pl
from jax.experimental.pallas import tpu as pltpu

F32 = jnp.float32
BF16 = jnp.bfloat16

D_MODEL = 1024
HEAD_DIM = 64
NSA_HEADS = 4
CMP_LEN = 32
CMP_STRIDE = 16
CMP_HIDDEN = 128
SEL_BLOCK = 64
SEL_TOP = 16
NSA_WINDOW = 512
MLA_HEADS = 4
MLA_Q_RANK = 256
MLA_KV_RANK = 128
MLA_NOPE = 64
MLA_ROPE = 32
MLA_V = 64
ROPE_THETA = 10000.0
SWA_HEADS = 8
SWA_WINDOW = 128
REL_BUCKETS = 32
REL_MAX_DIST = 512
NORM_EPS = 1e-6
NEG = -1e30
BIG = 1e9
LOG2E = math.log2(math.e)
MLA_SCALE2 = (MLA_NOPE + MLA_ROPE) ** -0.5 * LOG2E
IN_SIZES = (256, 384, 12, 256, 128, 32, 512, 128, 1024)

LANES = 128
Q_TILE = 128
WIN_TILE = 128
BAND_ROWS = 4096
PROJ_SUBTILES = 1
OUT_SUBTILES = 4
OUT_ROWS = 1024
CMP_GROUP = 4
FLASH_TILE = 512
SLC_DIAG_PARTS = 4
MLA_DIAG_PARTS = 4
GATE_LANE0 = 96
VMEM_LIMIT = 56 * 1024 * 1024

C_AQ, C_CMP, C_SLC, C_WIN, C_BCQ, C_BCKV, C_KRB, C_CQ, C_CKV, C_Z = (
    0, 256, 384, 512, 640, 896, 1024, 1152, 1664, 1792)


def _dot(a, b):
    return jnp.dot(a, b, preferred_element_type=F32)


def _dot_nt(a, b):
    return lax.dot_general(a, b, (((1,), (1,)), ((), ())), preferred_element_type=F32)


def _bucket_thresholds():
    d = np.arange(0, 4 * REL_MAX_DIST)
    exact = REL_BUCKETS // 2
    large = exact + (np.log(np.maximum(d, 1).astype(np.float32) / np.float32(exact))
                     / np.float32(math.log(REL_MAX_DIST / exact)) * np.float32(REL_BUCKETS - exact)).astype(np.int32)
    b = np.where(d < exact, d, np.minimum(large, REL_BUCKETS - 1))
    assert np.all(np.diff(b) >= 0) and b[-1] == REL_BUCKETS - 1
    return [int(np.argmax(b >= k)) for k in range(REL_BUCKETS)]


_THRESH = _bucket_thresholds()


def _cparams(sem):
    return pltpu.CompilerParams(dimension_semantics=sem, vmem_limit_bytes=VMEM_LIMIT)


def _table_kernel(h0, row_step, sign, d0, blk_d0, lo, hi, mult, rb_ref, out_ref):
    h = pl.program_id(0) + h0
    blk = pl.program_id(1)
    rt, cc = out_ref.shape[2], out_ref.shape[3]
    base = d0 + blk * blk_d0 - sign * row_step * rt * pl.program_id(2)
    width = -(-(row_step * rt + cc) // LANES) * LANES
    w = lax.broadcasted_iota(jnp.int32, (8, width), 1)
    dist = base + sign * jnp.where(w < cc, w, w - width)
    acc = jnp.full(dist.shape, rb_ref[h, 0], F32)
    for k in range(1, REL_BUCKETS):
        acc = jnp.where(dist >= _THRESH[k], rb_ref[h, k], acc)
    row = jnp.where((dist >= lo) & (dist < hi), acc * mult, NEG)
    rolled = pltpu.roll(jnp.broadcast_to(row[:1], (rt, width)), 0, 1, stride=row_step, stride_axis=0)
    out_ref[0, 0] = rolled[:, :cc]


def _bias_table(rel_bias, h0, nh, rows, cols, row_step, sign, d0, lo, hi, rt, mult=1.0, nblk=1, blk_d0=0):
    return pl.pallas_call(
        functools.partial(_table_kernel, h0, row_step, sign, d0, blk_d0, lo, hi, mult),
        grid=(nh, nblk, rows // rt),
        in_specs=[pl.BlockSpec(memory_space=pltpu.SMEM)],
        out_specs=pl.BlockSpec((1, 1, rt, cols), lambda h, b, r: (h, b, r, 0)),
        out_shape=jax.ShapeDtypeStruct((nh, nblk, rows, cols), F32),
        compiler_params=_cparams(("arbitrary", "arbitrary", "arbitrary")),
        name="bias_table",
    )(rel_bias)


def _rms(v, g):
    return v * lax.rsqrt(jnp.mean(v * v, axis=-1, keepdims=True) + NORM_EPS) * g


def _proj_kernel(x_ref, gpre_ref, w_ref, qn_ref, wuq_ref, kvn_ref, wk_ref, wv_ref,
                 rc_ref, rsa_ref, rsb_ref, eaug_ref,
                 qa_ref, qa2_ref, cmp_ref, slck_ref, slcv_ref, win_ref, qm_ref, km_ref, vm_ref,
                 krb_ref, cq_ref, ckv_ref):
    sub = x_ref.shape[0] // PROJ_SUBTILES
    rs = [slice(t * sub, (t + 1) * sub) for t in range(PROJ_SUBTILES)]
    hbs = [_rms(x_ref[r, :], gpre_ref[...]).astype(BF16) for r in rs]
    bounds = ((C_AQ, C_CMP), (C_CMP, C_WIN), (C_WIN, C_KRB), (C_KRB, C_Z))
    chunk_sets = [{ab: _dot(hb, w_ref[:, ab[0]:ab[1]]) for ab in bounds} for hb in hbs]
    lane = lax.broadcasted_iota(jnp.int32, (sub, LANES), 1)
    lo = lane < HEAD_DIM

    for r, chunks in zip(rs, chunk_sets):
        def proj(c0, c1):
            for (a, b), y in chunks.items():
                if a <= c0 and c1 <= b:
                    return y[:, c0 - a:c1 - a]
            raise AssertionError((c0, c1))

        qa_ref[r, :] = (proj(C_AQ, C_CMP) * 0.125).astype(BF16)
        qa2_ref[r, :] = (proj(C_AQ, C_CMP) * (0.125 * LOG2E)).astype(BF16)
        cmp_ref[r, :] = proj(C_CMP, C_SLC)
        slc = proj(C_SLC, C_WIN)
        slcv_ref[r, :] = jnp.where(lo, 1.0, slc).astype(BF16)
        slck_ref[r, :] = jnp.where(lo, slc, eaug_ref[r, :].astype(F32)).astype(BF16)
        win_ref[r, :] = proj(C_WIN, C_BCQ).astype(BF16)
        cq_ref[r, :] = (proj(C_CQ, C_CKV) * 0.125).astype(BF16)
        ckv_ref[r, :] = proj(C_CKV, C_Z).astype(BF16)

        rc, rsa, rsb = rc_ref[r, :], rsa_ref[r, :], rsb_ref[r, :]

        def rope(v):
            return v * rc + pltpu.roll(v, 16, 1) * rsa + pltpu.roll(v, 112, 1) * rsb

        cqn = _rms(proj(C_BCQ, C_BCKV), qn_ref[...]).astype(BF16)
        qm = _dot(cqn, wuq_ref[...])
        for h in range(MLA_HEADS):
            qh = rope(qm[:, LANES * h:LANES * (h + 1)]) * MLA_SCALE2
            qm_ref[r, LANES * h:LANES * (h + 1)] = qh.astype(BF16)

        krb = proj(C_KRB, C_CQ)
        krb_ref[r, :] = krb
        krr = jnp.where(lo | (lane >= HEAD_DIM + MLA_ROPE), 0.0, rope(krb))

        ckvn = _rms(proj(C_BCKV, C_KRB), kvn_ref[...]).astype(BF16)
        kn = _dot(ckvn, wk_ref[...])
        for h in range(MLA_HEADS):
            km_ref[r, LANES * h:LANES * (h + 1)] = (kn[:, LANES * h:LANES * (h + 1)] + krr).astype(BF16)
        vm = _dot(ckvn, wv_ref[...])
        for h in range(MLA_HEADS):
            vm_ref[r, LANES * h:LANES * (h + 1)] = jnp.where(lo, 1.0, vm[:, LANES * h:LANES * (h + 1)]).astype(BF16)


def _project(x2, gpre, w, qn, wuq, kvn, wk, wv, rope_c, rope_sa, rope_sb, eaug, T, tm):
    BT = x2.shape[0]
    nt = T // tm
    row = lambda i: (i, 0)
    fix = lambda i: (0, 0)
    pos = lambda i: (i % nt, 0)
    widths = [(256, BF16), (256, BF16), (128, F32), (128, BF16), (128, BF16), (128, BF16), (512, BF16), (512, BF16),
              (512, BF16), (128, F32), (512, BF16), (128, BF16)]
    return pl.pallas_call(
        _proj_kernel,
        grid=(BT // tm,),
        in_specs=[pl.BlockSpec((tm, D_MODEL), row), pl.BlockSpec((1, D_MODEL), fix),
                  pl.BlockSpec(w.shape, fix), pl.BlockSpec(qn.shape, fix), pl.BlockSpec(wuq.shape, fix),
                  pl.BlockSpec(kvn.shape, fix), pl.BlockSpec(wk.shape, fix), pl.BlockSpec(wv.shape, fix),
                  pl.BlockSpec((tm, LANES), pos), pl.BlockSpec((tm, LANES), pos), pl.BlockSpec((tm, LANES), pos),
                  pl.BlockSpec((tm, LANES), pos)],
        out_specs=[pl.BlockSpec((tm, wd), row) for wd, _ in widths],
        out_shape=[jax.ShapeDtypeStruct((BT, wd), dt) for wd, dt in widths],
        compiler_params=_cparams(("arbitrary",)),
        name="proj",
    )(x2, gpre, w, qn, wuq, kvn, wk, wv, rope_c, rope_sa, rope_sb, eaug)


def _compress_kernel(c_ref, pet_ref, peb_ref, w1t_ref, w1b_ref, w2_ref, o_ref):
    nch = c_ref.shape[1] // CMP_STRIDE
    c = jnp.concatenate([c_ref[0, pl.ds(j, nch, stride=CMP_STRIDE), :] for j in range(CMP_STRIDE)], axis=1)
    top = (c + pet_ref[...]).astype(BF16)
    bot = (c + peb_ref[...]).astype(BF16)
    out = jnp.zeros((nch, LANES), F32)
    for i in range(2):
        pre = _dot(top, w1t_ref[i]) + pltpu.roll(_dot(bot, w1b_ref[i]), nch - 1, 0)
        hid = pre * (1.0 / (1.0 + jnp.exp(-pre)))
        out = out + _dot(hid.astype(BF16), w2_ref[i])
    o_ref[0] = out.astype(BF16)


def _compress(tokens, pet, peb, w1t, w1b, w2):
    B, T, width = tokens.shape
    nch = T // CMP_STRIDE
    full = lambda a: pl.BlockSpec(a.shape, lambda b: (0,) * a.ndim)
    return pl.pallas_call(
        _compress_kernel,
        grid=(B,),
        in_specs=[pl.BlockSpec((1, T, width), lambda b: (b, 0, 0)),
                  full(pet), full(peb), full(w1t), full(w1b), full(w2)],
        out_specs=pl.BlockSpec((1, nch, LANES), lambda b: (b, 0, 0)),
        out_shape=jax.ShapeDtypeStruct((B, nch, LANES), BF16),
        compiler_params=_cparams(("arbitrary",)),
        name="compress",
    )(tokens, pet, peb, w1t, w1b, w2)


def _head_slabs(qblk, n_heads):
    q = qblk.astype(F32)
    lo = lax.broadcasted_iota(jnp.int32, (q.shape[0], LANES), 1) < HEAD_DIM
    out = []
    for j in range(n_heads // 2):
        slab = q[:, LANES * j:LANES * (j + 1)]
        out.append(jnp.where(lo, slab, 0.0))
        out.append(jnp.where(lo, pltpu.roll(slab, HEAD_DIM, 1), 0.0))
    return out


def _merge_upper(accs):
    lo = lax.broadcasted_iota(jnp.int32, accs[0].shape, 1) < HEAD_DIM
    slabs = [jnp.where(lo, pltpu.roll(accs[2 * j], HEAD_DIM, 1), accs[2 * j + 1]) for j in range(len(accs) // 2)]
    return jnp.concatenate(slabs, axis=1)


def _merge_normalized(accs):
    lo = lax.broadcasted_iota(jnp.int32, accs[0].shape, 1) < HEAD_DIM
    slabs = []
    for j in range(len(accs) // 2):
        a, b = accs[2 * j], accs[2 * j + 1]
        out = jnp.where(lo, pltpu.roll(a, HEAD_DIM, 1), b)
        den = jnp.where(lo, a, pltpu.roll(b, HEAD_DIM, 1))
        slabs.append(out / den)
    return jnp.concatenate(slabs, axis=1)


def _row_max_lanes(s):
    return jnp.broadcast_to(jnp.max(s, axis=-1, keepdims=True), (s.shape[0], LANES))


def _sub_lanes(s, ref):
    return jnp.concatenate([s[:, LANES * i:LANES * (i + 1)] - ref for i in range(s.shape[1] // LANES)], axis=1)


def _topk_mask_t(imp, t0):
    ns, tq = imp.shape
    srow = lax.broadcasted_iota(jnp.int32, (ns, tq), 0)
    cur = (t0 + lax.broadcasted_iota(jnp.int32, (ns, tq), 1)) // SEL_BLOCK
    forced = (srow == 0) | (srow == cur) | (srow == cur - 1)
    x = jnp.where(forced, BIG, jnp.where(srow <= cur, imp, -BIG))
    sub = 8
    groups = [x[sub * v:sub * (v + 1)] for v in range(ns // sub)]
    rows_in = [srow[sub * v:sub * (v + 1)] for v in range(ns // sub)]
    cnts = [jnp.zeros((sub, tq), F32) for _ in groups]
    for sp in range(ns):
        other = jnp.broadcast_to(x[sp:sp + 1, :], (sub, tq))
        for v, xv in enumerate(groups):
            if sub * v > sp:
                beats = other >= xv
            elif sub * v + sub - 1 < sp:
                beats = other > xv
            else:
                beats = (other > xv) | ((other == xv) & (rows_in[v] > sp))
            cnts[v] = cnts[v] + jnp.where(beats, 1.0, 0.0)
    cnt = jnp.concatenate(cnts, axis=0)
    sel = (cnt < float(min(SEL_TOP, ns))) & (srow <= cur)
    return jnp.where(sel, 0.0, NEG)


def _cmp_kernel(q_ref, kvc_ref, bc_ref, ovt_ref, ocmp_ref, selb_ref):
    rows = CMP_GROUP * Q_TILE
    ncp, ns = kvc_ref.shape[1], ovt_ref.shape[0]
    step = pl.program_id(1)
    lo_step, nse = 0, SEL_TOP
    while lo_step * rows < ns * SEL_BLOCK:
        nse = min(nse, ns)
        hi_step = nse * SEL_BLOCK // rows
        ncw = min(ncp, max(LANES, nse * (SEL_BLOCK // CMP_STRIDE)))
        if hi_step > lo_step:
            pl.when((step >= lo_step) & (step < hi_step))(
                functools.partial(_cmp_body, ncw, nse, q_ref, kvc_ref, bc_ref, ovt_ref, ocmp_ref, selb_ref))
            lo_step = hi_step
        nse *= 2


def _cmp_body(ncw, ns, q_ref, kvc_ref, bc_ref, ovt_ref, ocmp_ref, selb_ref):
    tq = Q_TILE
    kvc = kvc_ref[0, :ncw, :]
    ovt = ovt_ref[:ns, :ncw]
    groups = range(CMP_GROUP)
    sls = [slice(g * tq, (g + 1) * tq) for g in groups]
    s_alls = []
    for g in groups:
        qs = _head_slabs(q_ref[0, sls[g], :], NSA_HEADS)
        s_alls.append(_dot_nt(jnp.concatenate([q.astype(BF16) for q in qs], axis=0), kvc))
    psums, o_alls = [], []
    for g in groups:
        psum = None
        ps = []
        for h in range(NSA_HEADS):
            s = s_alls[g][h * tq:(h + 1) * tq] + bc_ref[h, sls[g], :ncw]
            m = jnp.max(s, axis=-1, keepdims=True)
            e = jnp.exp(s - m)
            l = jnp.sum(e, axis=-1, keepdims=True)
            p = e * jnp.where(m > 0.5 * NEG, 1.0 / l, 0.0)
            psum = p if psum is None else psum + p
            ps.append(p.astype(BF16))
        psums.append(psum)
        o_alls.append(_dot(jnp.concatenate(ps, axis=0), kvc))
    imps = []
    for g in groups:
        hi = psums[g].astype(BF16)
        lo = (psums[g] - hi.astype(F32)).astype(BF16)
        imps.append(_dot_nt(ovt, hi) + _dot_nt(ovt, lo))
    for g in groups:
        ocmp = _merge_upper([o_alls[g][h * tq:(h + 1) * tq] for h in range(NSA_HEADS)])
        ocmp_ref[0, sls[g], :] = ocmp.astype(ocmp_ref.dtype)
        sb = _topk_mask_t(imps[g], (pl.program_id(1) * CMP_GROUP + g) * tq)
        if ns < HEAD_DIM:
            sb = jnp.concatenate([sb, jnp.full((HEAD_DIM - ns, tq), NEG, F32)], axis=0)
        full = jnp.concatenate([jnp.zeros((HEAD_DIM, tq), F32), sb], axis=0)
        selb_ref[0, sls[g], :] = full.T.astype(BF16)


def _cmp_attention(qa, kvc, bc, ovt, T):
    B = qa.shape[0]
    ncp = kvc.shape[1]
    rows = CMP_GROUP * Q_TILE
    return pl.pallas_call(
        _cmp_kernel,
        grid=(B, T // rows),
        in_specs=[pl.BlockSpec((1, rows, 256), lambda b, n: (b, n, 0)),
                  pl.BlockSpec((1, ncp, LANES), lambda b, n: (b, 0, 0)),
                  pl.BlockSpec((NSA_HEADS, rows, ncp), lambda b, n: (0, n, 0)),
                  pl.BlockSpec(ovt.shape, lambda b, n: (0, 0))],
        out_specs=[pl.BlockSpec((1, rows, 256), lambda b, n: (b, n, 0)),
                   pl.BlockSpec((1, rows, LANES), lambda b, n: (b, n, 0))],
        out_shape=[jax.ShapeDtypeStruct((B, T, 256), BF16), jax.ShapeDtypeStruct((B, T, LANES), BF16)],
        compiler_params=_cparams(("arbitrary", "arbitrary")),
        name="cmp_select",
    )(qa, kvc, bc, ovt)


def _softmax_tile(s2, rows, m_s, shift=None):
    m_t = _row_max_lanes(s2)
    if shift is not None:
        m_t = m_t + shift
    m_old = m_s[rows]
    m_new = jnp.maximum(m_old, m_t)
    p = jnp.exp2(_sub_lanes(s2, m_new if shift is None else m_new - shift))
    alpha = jnp.exp2(m_old - m_new)
    m_s[rows] = m_new
    return p.astype(BF16), alpha


def _slc_kernel(b31_ref, q_ref, selb_ref, ka_ref, kv_ref, tab_ref, o_ref, m_s, acc_s):
    tq = q_ref.shape[1]
    H = NSA_HEADS
    n = pl.program_id(1)
    qs = _head_slabs(q_ref[0], H)
    selb = selb_ref[0].astype(F32)
    qst = jnp.concatenate([(qs[h] + selb).astype(BF16) for h in range(H)], axis=0)
    m_s[...] = jnp.full(m_s.shape, -jnp.inf, F32)
    acc_s[...] = jnp.zeros(acc_s.shape, F32)

    def scores(k0, width=tq):
        k0 = pl.multiple_of(k0, tq)
        return _dot_nt(qst, ka_ref[0, pl.ds(k0, width), :]), kv_ref[0, pl.ds(k0, width), :]

    def update(s_all, kv):
        ps, alphas = [], []
        for h in range(H):
            rows = slice(h * tq, (h + 1) * tq)
            p, alpha = _softmax_tile(s_all[rows], rows, m_s, shift=b31_ref[h])
            ps.append(p)
            alphas.append(alpha)
        pv = _dot(jnp.concatenate(ps, axis=0), kv)
        for h in range(H):
            rows = slice(h * tq, (h + 1) * tq)
            acc_s[rows] = alphas[h] * acc_s[rows] + pv[rows]

    def far_pair(i, carry):
        update(*scores(2 * i * tq, 2 * tq))
        return carry

    n_far = jnp.maximum(n - 1, 0)
    lax.fori_loop(0, n_far // 2, far_pair, 0)

    @pl.when(n_far % 2 == 1)
    def _():
        update(*scores((n_far - 1) * tq))

    def near_diagonal(k0, before):
        k0 = pl.multiple_of(k0, tq)
        half = tq // SLC_DIAG_PARTS
        parts = []
        for part in range(SLC_DIAG_PARTS):
            kw = before + half * (part + 1)
            q_part = jnp.concatenate([qst[h * tq + part * half:h * tq + (part + 1) * half] for h in range(H)], axis=0)
            parts.append((_dot_nt(q_part, ka_ref[0, pl.ds(k0, kw), :]), kv_ref[0, pl.ds(k0, kw), :], kw))
        for part, (s_all, kv, kw) in enumerate(parts):
            ps, alphas = [], []
            for h in range(H):
                rows = slice(h * tq + part * half, h * tq + (part + 1) * half)
                bias = tab_ref[h, part * half:(part + 1) * half, tq - before:tq - before + kw]
                p, alpha = _softmax_tile(s_all[h * half:(h + 1) * half] + bias, rows, m_s)
                ps.append(p)
                alphas.append(alpha)
            pv = _dot(jnp.concatenate(ps, axis=0), kv)
            for h in range(H):
                rows = slice(h * tq + part * half, h * tq + (part + 1) * half)
                acc_s[rows] = alphas[h] * acc_s[rows] + pv[h * half:(h + 1) * half]

    @pl.when(n >= 1)
    def _():
        near_diagonal(jnp.maximum(n - 1, 0) * tq, tq)

    @pl.when(n == 0)
    def _():
        near_diagonal(0, 0)

    o_ref[0] = _merge_normalized([acc_s[h * tq:(h + 1) * tq] for h in range(H)]).astype(o_ref.dtype)


def _slc_attention(b31, qa, selb, slck, slcv, tab, T):
    B = qa.shape[0]
    tq = FLASH_TILE
    H = NSA_HEADS
    tile = lambda w: pl.BlockSpec((1, tq, w), lambda b, n: (b, n, 0))
    return pl.pallas_call(
        _slc_kernel,
        grid=(B, T // tq),
        in_specs=[pl.BlockSpec(memory_space=pltpu.SMEM),
                  tile(256), tile(LANES),
                  pl.BlockSpec((1, T, LANES), lambda b, n: (b, 0, 0)),
                  pl.BlockSpec((1, T, LANES), lambda b, n: (b, 0, 0)),
                  pl.BlockSpec(tab.shape, lambda b, n: (0, 0, 0))],
        out_specs=tile(256),
        out_shape=jax.ShapeDtypeStruct((B, T, 256), BF16),
        scratch_shapes=[pltpu.VMEM((H * tq, LANES), F32), pltpu.VMEM((H * tq, LANES), F32)],
        compiler_params=_cparams(("arbitrary", "arbitrary")),
        name="slc_attention",
    )(b31, qa, selb, slck, slcv, tab)


def _band_kernel(n_heads, window, has_sink, n_sub, tq, *refs):
    if has_sink:
        q_ref, kv_ref, tab_ref, sink_ref, o_ref = refs
    else:
        q_ref, kv_ref, tab_ref, o_ref = refs
    span = window + tq
    lo_kv = lax.broadcasted_iota(jnp.int32, (span, LANES), 1) < HEAD_DIM
    lo_o = lax.broadcasted_iota(jnp.int32, (n_heads * tq, LANES), 1) < HEAD_DIM
    groups = range(n_sub)
    tiles = [pl.program_id(1) * n_sub + g for g in groups]
    kvs, dots = [], []
    for g in groups:
        k0 = pl.multiple_of(jnp.maximum(tiles[g] * tq - window, 0), tq)
        kvs.append(kv_ref[0, pl.ds(k0, span), :])
        qs = _head_slabs(q_ref[0, g * tq:(g + 1) * tq, :], n_heads)
        dots.append(_dot_nt(jnp.concatenate([q.astype(BF16) for q in qs], axis=0), kvs[g]))
    accs, ms = [], []
    for g in groups:
        s = dots[g] + tab_ref[jnp.minimum(tiles[g], window // tq)]
        m = _row_max_lanes(s)
        if has_sink:
            m = jnp.maximum(m, sink_ref[...])
        e = jnp.exp(_sub_lanes(s, m))
        ms.append(m)
        accs.append(_dot(e.astype(BF16), jnp.where(lo_kv, 1.0, kvs[g]).astype(BF16)))
    for g in groups:
        acc = accs[g]
        if has_sink:
            acc = acc + jnp.where(lo_o, jnp.exp(sink_ref[...] - ms[g]), 0.0)
        out = _merge_normalized([acc[h * tq:(h + 1) * tq] for h in range(n_heads)])
        o_ref[0, g * tq:(g + 1) * tq, :] = out.astype(o_ref.dtype)


def _band_attention(q, kv_pad, tab, sinks, n_heads, window, T, out_dtype):
    B = q.shape[0]
    width = n_heads * HEAD_DIM
    has_sink = sinks is not None
    nvar, tq = tab.shape[1], tab.shape[2]
    tab2 = jnp.transpose(tab, (1, 0, 2, 3)).reshape(nvar, n_heads * tq, window + tq)
    n_sub = BAND_ROWS // (n_heads * tq)
    rows = n_sub * tq
    in_specs = [pl.BlockSpec((1, rows, width), lambda b, n: (b, n, 0)),
                pl.BlockSpec((1, T, LANES), lambda b, n: (b, 0, 0)),
                pl.BlockSpec(tab2.shape, lambda b, n: (0, 0, 0))]
    args = (q, kv_pad, tab2)
    if has_sink:
        rep = jnp.broadcast_to(jnp.repeat(sinks, tq)[:, None], (n_heads * tq, LANES))
        in_specs.append(pl.BlockSpec(rep.shape, lambda b, n: (0, 0)))
        args = args + (rep,)
    return pl.pallas_call(
        functools.partial(_band_kernel, n_heads, window, has_sink, n_sub, tq),
        grid=(B, T // rows),
        in_specs=in_specs,
        out_specs=pl.BlockSpec((1, rows, width), lambda b, n: (b, n, 0)),
        out_shape=jax.ShapeDtypeStruct((B, T, width), out_dtype),
        compiler_params=_cparams(("arbitrary", "arbitrary")),
        name="band_sink" if has_sink else "band_window",
    )(*args)


def _mla_kernel(q_ref, k_ref, v_ref, o_ref, m_s, acc_s):
    tq = q_ref.shape[1]
    H = MLA_HEADS
    n = pl.program_id(1)
    m_s[...] = jnp.full(m_s.shape, -jnp.inf, F32)
    acc_s[...] = jnp.zeros(acc_s.shape, F32)

    def scores(k0, width=tq):
        k0 = pl.multiple_of(k0, tq)
        return k0, width, [_dot_nt(q_ref[0, :, LANES * h:LANES * (h + 1)],
                                   k_ref[0, pl.ds(k0, width), LANES * h:LANES * (h + 1)]) for h in range(H)]

    def update(k0, width, ss):
        ps, alphas = [], []
        for h in range(H):
            p, alpha = _softmax_tile(ss[h], slice(h * tq, (h + 1) * tq), m_s)
            ps.append(p)
            alphas.append(alpha)
        for h in range(H):
            rows = slice(h * tq, (h + 1) * tq)
            v = v_ref[0, pl.ds(k0, width), LANES * h:LANES * (h + 1)]
            acc_s[rows] = alphas[h] * acc_s[rows] + _dot(ps[h], v)

    def far_pair(i, carry):
        update(*scores(2 * i * tq, 2 * tq))
        return carry

    n_far = jnp.maximum(n - 1, 0)
    lax.fori_loop(0, n_far // 2, far_pair, 0)

    @pl.when(n_far % 2 == 1)
    def _():
        update(*scores((n_far - 1) * tq))

    def near_diagonal(k0, before):
        k0 = pl.multiple_of(k0, tq)
        half = tq // MLA_DIAG_PARTS
        parts = []
        for part in range(MLA_DIAG_PARTS):
            kw = before + half * (part + 1)
            r = slice(part * half, (part + 1) * half)
            parts.append((kw, [_dot_nt(q_ref[0, r, LANES * h:LANES * (h + 1)],
                                       k_ref[0, pl.ds(k0, kw), LANES * h:LANES * (h + 1)]) for h in range(H)]))
        for part, (kw, ss) in enumerate(parts):
            row = lax.broadcasted_iota(jnp.int32, (half, kw), 0)
            col = lax.broadcasted_iota(jnp.int32, (half, kw), 1)
            causal = col <= row + (before + part * half)
            ps, alphas = [], []
            for h in range(H):
                rows = slice(h * tq + part * half, h * tq + (part + 1) * half)
                p, alpha = _softmax_tile(jnp.where(causal, ss[h], NEG), rows, m_s)
                ps.append(p)
                alphas.append(alpha)
            for h in range(H):
                rows = slice(h * tq + part * half, h * tq + (part + 1) * half)
                v = v_ref[0, pl.ds(k0, kw), LANES * h:LANES * (h + 1)]
                acc_s[rows] = alphas[h] * acc_s[rows] + _dot(ps[h], v)

    @pl.when(n >= 1)
    def _():
        near_diagonal(jnp.maximum(n - 1, 0) * tq, tq)

    @pl.when(n == 0)
    def _():
        near_diagonal(0, 0)

    o_ref[0] = _merge_normalized([acc_s[h * tq:(h + 1) * tq] for h in range(H)]).astype(o_ref.dtype)


def _mla_attention(qm, km, vm, T):
    B = qm.shape[0]
    tq = FLASH_TILE
    H = MLA_HEADS
    return pl.pallas_call(
        _mla_kernel,
        grid=(B, T // tq),
        in_specs=[pl.BlockSpec((1, tq, 512), lambda b, n: (b, n, 0)),
                  pl.BlockSpec((1, T, 512), lambda b, n: (b, 0, 0)),
                  pl.BlockSpec((1, T, 512), lambda b, n: (b, 0, 0))],
        out_specs=pl.BlockSpec((1, tq, 256), lambda b, n: (b, n, 0)),
        out_shape=jax.ShapeDtypeStruct((B, T, 256), BF16),
        scratch_shapes=[pltpu.VMEM((H * tq, LANES), F32), pltpu.VMEM((H * tq, LANES), F32)],
        compiler_params=_cparams(("arbitrary", "arbitrary")),
        name="mla_attention",
    )(qm, km, vm)


def _out_kernel(x_ref, gpre_ref, wz_ref, ocmp_ref, oslc_ref, owin_ref, krb_ref, gexp_ref, ob_ref, oc_ref, w_ref,
                gpost_ref, o_ref):
    sub = x_ref.shape[0] // OUT_SUBTILES
    rs = [slice(t * sub, (t + 1) * sub) for t in range(OUT_SUBTILES)]
    w_a = NSA_HEADS * HEAD_DIM
    hs = [_rms(x_ref[r, :], gpre_ref[...]).astype(BF16) for r in rs]
    zs = [_dot(h, wz_ref[...]) for h in hs]
    mixeds = []
    for r, z in zip(rs, zs):
        sig = 1.0 / (1.0 + jnp.exp(-krb_ref[r, :]))
        hi = sig.astype(BF16)
        lo = (sig - hi.astype(F32)).astype(BF16)
        gates = _dot(jnp.concatenate([hi, lo], axis=1), gexp_ref[...])
        oa = None
        for j, br in enumerate((ocmp_ref, oslc_ref, owin_ref)):
            term = gates[:, w_a * j:w_a * (j + 1)] * br[r, :].astype(F32)
            oa = term if oa is None else oa + term
        heads = jnp.concatenate([oa, ob_ref[r, :].astype(F32), oc_ref[r, :].astype(F32)], axis=1)
        mixeds.append((heads * (z * (1.0 / (1.0 + jnp.exp(-z))))).astype(BF16))
    ys = [_dot(m, w_ref[...]) for m in mixeds]
    for r, y in zip(rs, ys):
        o_ref[r, :] = x_ref[r, :] + _rms(y, gpost_ref[...])


def _out_project(x2, gpre, wz, ocmp, oslc, owin, krb, gexp, ob, oc, w, gpost, tm):
    BT = x2.shape[0]
    row = lambda i: (i, 0)
    fix = lambda i: (0, 0)
    spec = lambda a: pl.BlockSpec((tm, a.shape[1]), row)
    full = lambda a: pl.BlockSpec(a.shape, fix)
    return pl.pallas_call(
        _out_kernel,
        grid=(BT // tm,),
        in_specs=[spec(x2), full(gpre), full(wz), spec(ocmp), spec(oslc), spec(owin), spec(krb), full(gexp),
                  spec(ob), spec(oc), full(w), full(gpost)],
        out_specs=pl.BlockSpec((tm, D_MODEL), row),
        out_shape=jax.ShapeDtypeStruct((BT, D_MODEL), F32),
        compiler_params=_cparams(("arbitrary",)),
        name="out_proj",
    )(x2, gpre, wz, ocmp, oslc, owin, krb, gexp, ob, oc, w, gpost)


def _split_in(w):
    outs, o = [], 0
    for s in IN_SIZES:
        outs.append(w[..., o:o + s])
        o += s
    return outs


def _relayout_kernel(w_ref, wp_ref, wz_ref):
    a_q, a_kv, a_g, b_cq, b_ckv, b_kr, c_q, c_kv, z = _split_in(w_ref[0])
    zeros = lambda n: jnp.zeros((a_q.shape[0], n), F32)
    krb = jnp.concatenate([zeros(HEAD_DIM), b_kr, a_g, zeros(LANES - HEAD_DIM - MLA_ROPE - 3 * NSA_HEADS)], axis=-1)
    wp_ref[0] = jnp.concatenate([a_q, a_kv, b_cq, b_ckv, krb, c_q, c_kv], axis=-1).astype(BF16)
    wz_ref[0] = z.astype(BF16)


def _prep_w_in(w):
    L, d, n = w.shape
    rt = 256
    return pl.pallas_call(
        _relayout_kernel,
        grid=(L, d // rt),
        in_specs=[pl.BlockSpec((1, rt, n), lambda l, r: (l, r, 0))],
        out_specs=[pl.BlockSpec((1, rt, C_Z), lambda l, r: (l, r, 0)),
                   pl.BlockSpec((1, rt, IN_SIZES[-1]), lambda l, r: (l, r, 0))],
        out_shape=[jax.ShapeDtypeStruct((L, d, C_Z), BF16), jax.ShapeDtypeStruct((L, d, IN_SIZES[-1]), BF16)],
        compiler_params=_cparams(("arbitrary", "arbitrary")),
        name="w_in_relayout",
    )(w)


def _prep_mla(w_uq, w_ukv):
    L = w_uq.shape[0]
    dq = MLA_NOPE + MLA_ROPE
    uq = w_uq.reshape(L, MLA_Q_RANK, MLA_HEADS, dq)
    uq = jnp.concatenate([uq, jnp.zeros((L, MLA_Q_RANK, MLA_HEADS, LANES - dq), uq.dtype)], axis=-1)
    ukv = w_ukv.reshape(L, MLA_KV_RANK, MLA_HEADS, MLA_NOPE + MLA_V)
    uk = jnp.concatenate([ukv[..., :MLA_NOPE], jnp.zeros((L, MLA_KV_RANK, MLA_HEADS, LANES - MLA_NOPE), ukv.dtype)], axis=-1)
    uv = jnp.concatenate([jnp.zeros((L, MLA_KV_RANK, MLA_HEADS, LANES - MLA_V), ukv.dtype), ukv[..., MLA_NOPE:]], axis=-1)
    return (uq.reshape(L, MLA_Q_RANK, MLA_HEADS * LANES).astype(BF16),
            uk.reshape(L, MLA_KV_RANK, MLA_HEADS * LANES).astype(BF16),
            uv.reshape(L, MLA_KV_RANK, MLA_HEADS * LANES).astype(BF16))


def _prep_compress(cmp_pos, cmp_w1, cmp_w2):
    L = cmp_pos.shape[0]
    pe = jnp.concatenate([cmp_pos[:, 0], cmp_pos[:, 1]], axis=-1)
    pet = pe[:, :CMP_STRIDE].reshape(L, 1, CMP_STRIDE * LANES)
    peb = pe[:, CMP_STRIDE:].reshape(L, 1, CMP_STRIDE * LANES)
    w1 = cmp_w1.reshape(L, 2, CMP_LEN, HEAD_DIM, CMP_HIDDEN)
    zero = jnp.zeros_like(w1[:, 0])
    w1k = jnp.concatenate([w1[:, 0], zero], axis=2)
    w1v = jnp.concatenate([zero, w1[:, 1]], axis=2)
    w1e = jnp.stack([w1k, w1v], axis=1)
    w1t = w1e[:, :, :CMP_STRIDE].reshape(L, 2, CMP_STRIDE * LANES, CMP_HIDDEN).astype(BF16)
    w1b = w1e[:, :, CMP_STRIDE:].reshape(L, 2, CMP_STRIDE * LANES, CMP_HIDDEN).astype(BF16)
    z2 = jnp.zeros_like(cmp_w2[:, 0])
    w2 = jnp.stack([jnp.concatenate([cmp_w2[:, 0], z2], axis=-1),
                    jnp.concatenate([z2, cmp_w2[:, 1]], axis=-1)], axis=1).astype(BF16)
    return pet, peb, w1t, w1b, w2


def _rope_tables(T):
    half = MLA_ROPE // 2
    inv = ROPE_THETA ** (-jnp.arange(half, dtype=F32) / half)
    ang = jnp.arange(T).astype(F32)[:, None] * inv[None, :]
    cos, sin = jnp.cos(ang), jnp.sin(ang)
    z = lambda n: jnp.zeros((T, n), F32)
    rc = jnp.concatenate([jnp.ones((T, MLA_NOPE), F32), cos, cos, z(LANES - MLA_NOPE - MLA_ROPE)], axis=1)
    rsa = jnp.concatenate([z(MLA_NOPE + half), sin, z(LANES - MLA_NOPE - MLA_ROPE)], axis=1)
    rsb = jnp.concatenate([z(MLA_NOPE), -sin, z(LANES - MLA_NOPE - half)], axis=1)
    return rc, rsa, rsb


def _static_tables(T):
    ns = T // SEL_BLOCK
    nc = T // CMP_STRIDE - 1
    ncp = T // CMP_STRIDE
    sstart = np.arange(ns) * SEL_BLOCK
    cstart = np.arange(nc) * CMP_STRIDE
    overlap = (np.clip(np.minimum(cstart[:, None] + CMP_LEN, sstart[None, :] + SEL_BLOCK)
                       - np.maximum(cstart[:, None], sstart[None, :]), 0, None) / CMP_STRIDE).astype(np.float32)
    ovt = np.zeros((ns, ncp), np.float32)
    ovt[:, :nc] = overlap.T
    eaug = np.zeros((T, LANES), np.float32)
    eaug[np.arange(T), HEAD_DIM + np.arange(T) // SEL_BLOCK] = 1.0
    gexp = np.zeros((2, LANES, 3 * NSA_HEADS * HEAD_DIM), np.float32)
    for h in range(NSA_HEADS):
        for j in range(3):
            c0 = NSA_HEADS * HEAD_DIM * j + HEAD_DIM * h
            gexp[:, GATE_LANE0 + 3 * h + j, c0:c0 + HEAD_DIM] = 1.0
    gexp = gexp.reshape(2 * LANES, -1)
    return jnp.asarray(ovt, BF16), jnp.asarray(eaug, BF16), jnp.asarray(gexp, BF16)


def kernel(x, w_in, w_out, norm_pre, norm_post, cmp_pos, cmp_w1, cmp_w2, mla_q_norm, mla_w_uq, mla_kv_norm,
           mla_w_ukv, swa_sinks, rel_bias):
    B, T, D = x.shape
    depth = w_in.shape[0]
    assert D == D_MODEL and T % (2 * FLASH_TILE) == 0 and T // SEL_BLOCK <= HEAD_DIM
    ncp = T // CMP_STRIDE
    tm = 512

    w_in_p, w_z = _prep_w_in(w_in)
    wuq, wk, wv = _prep_mla(mla_w_uq, mla_w_ukv)
    pet, peb, w1t, w1b, w2 = _prep_compress(cmp_pos, cmp_w1, cmp_w2)
    w_out_b = w_out.astype(BF16)
    rope_c, rope_sa, rope_sb = _rope_tables(T)
    ovt, eaug, gexp = _static_tables(T)

    bc = _bias_table(rel_bias, 0, NSA_HEADS, ncp, T, CMP_STRIDE, 1, -(CMP_LEN - 1), 0, 1 << 30, 32)
    bc = jnp.transpose(bc.reshape(NSA_HEADS, ncp, T), (0, 2, 1))
    tab_near = _bias_table(rel_bias, 0, NSA_HEADS, FLASH_TILE, 2 * FLASH_TILE, 1, -1, FLASH_TILE, 0, 1 << 30, 256,
                           mult=LOG2E).reshape(NSA_HEADS, FLASH_TILE, 2 * FLASH_TILE)
    tab_win = _bias_table(rel_bias, 0, NSA_HEADS, WIN_TILE, NSA_WINDOW + WIN_TILE, 1, -1, 0, 0, NSA_WINDOW, WIN_TILE,
                          nblk=NSA_WINDOW // WIN_TILE + 1, blk_d0=WIN_TILE)
    tab_swa = _bias_table(rel_bias, NSA_HEADS, SWA_HEADS, Q_TILE, SWA_WINDOW + Q_TILE, 1, -1, 0, 0, SWA_WINDOW, Q_TILE,
                          nblk=SWA_WINDOW // Q_TILE + 1, blk_d0=Q_TILE)
    b31 = rel_bias[:NSA_HEADS, REL_BUCKETS - 1] * LOG2E

    x2 = x.reshape(B * T, D)
    for l in range(depth):
        (qa, qa2, cmpkv, slck, slcv, win, qm, km, vm, krb, cq, ckv) = _project(
            x2, norm_pre[l][None], w_in_p[l], mla_q_norm[l][None], wuq[l], mla_kv_norm[l][None], wk[l], wv[l],
            rope_c, rope_sa, rope_sb, eaug, T, PROJ_SUBTILES * tm)
        r3 = lambda a: a.reshape(B, T, a.shape[-1])
        kvc = _compress(r3(cmpkv), pet[l], peb[l], w1t[l], w1b[l], w2[l])
        qa3 = r3(qa)
        ocmp, selb = _cmp_attention(qa3, kvc, bc, ovt, T)
        owin = _band_attention(qa3, r3(win), tab_win, None, NSA_HEADS, NSA_WINDOW, T, BF16)
        oslc = _slc_attention(b31, r3(qa2), selb, r3(slck), r3(slcv), tab_near, T)
        oc = _band_attention(r3(cq), r3(ckv), tab_swa, swa_sinks[l], SWA_HEADS, SWA_WINDOW, T, BF16)
        ob = _mla_attention(r3(qm), r3(km), r3(vm), T)
        flat = lambda a: a.reshape(B * T, a.shape[-1])
        x2 = _out_project(x2, norm_pre[l][None], w_z[l], flat(ocmp), flat(oslc), flat(owin), krb, gexp,
                          flat(ob), flat(oc), w_out_b[l], norm_post[l][None], OUT_ROWS)
    return x2.reshape(B, T, D)
```

```python
import functools
import math

import numpy as np
import jax
import jax.numpy as jnp
from jax import lax
from jax.experimental import pallas as pl
from jax.experimental.pallas import tpu as pltpu

F32 = jnp.float32
BF16 = jnp.bfloat16

D_MODEL = 1024
HEAD_DIM = 64
NSA_HEADS = 4
CMP_LEN = 32
CMP_STRIDE = 16
CMP_HIDDEN = 128
SEL_BLOCK = 64
SEL_TOP = 16
NSA_WINDOW = 512
MLA_HEADS = 4
MLA_Q_RANK = 256
MLA_KV_RANK = 128
MLA_NOPE = 64
MLA_ROPE = 32
MLA_V = 64
ROPE_THETA = 10000.0
SWA_HEADS = 8
SWA_WINDOW = 128
REL_BUCKETS = 32
REL_MAX_DIST = 512
NORM_EPS = 1e-6
NEG = -1e30
BIG = 1e9
LOG2E = math.log2(math.e)
MLA_SCALE2 = (MLA_NOPE + MLA_ROPE) ** -0.5 * LOG2E
IN_SIZES = (256, 384, 12, 256, 128, 32, 512, 128, 1024)

LANES = 128
Q_TILE = 128
WIN_TILE = 128
BAND_ROWS = 4096
PROJ_SUBTILES = 1
OUT_SUBTILES = 4
OUT_ROWS = 1024
CMP_GROUP = 4
FLASH_TILE = 512
SLC_DIAG_PARTS = 4
MLA_DIAG_PARTS = 4
GATE_LANE0 = 96
VMEM_LIMIT = 56 * 1024 * 1024

C_AQ, C_CMP, C_SLC, C_WIN, C_BCQ, C_BCKV, C_KRB, C_CQ, C_CKV, C_Z = (
    0, 256, 384, 512, 640, 896, 1024, 1152, 1664, 1792)


def _dot(a, b):
    return jnp.dot(a, b, preferred_element_type=F32)


def _dot_nt(a, b):
    return lax.dot_general(a, b, (((1,), (1,)), ((), ())), preferred_element_type=F32)


def _bucket_thresholds():
    d = np.arange(0, 4 * REL_MAX_DIST)
    exact = REL_BUCKETS // 2
    large = exact + (np.log(np.maximum(d, 1).astype(np.float32) / np.float32(exact))
                     / np.float32(math.log(REL_MAX_DIST / exact)) * np.float32(REL_BUCKETS - exact)).astype(np.int32)
    b = np.where(d < exact, d, np.minimum(large, REL_BUCKETS - 1))
    assert np.all(np.diff(b) >= 0) and b[-1] == REL_BUCKETS - 1
    return [int(np.argmax(b >= k)) for k in range(REL_BUCKETS)]


_THRESH = _bucket_thresholds()


def _cparams(sem):
    return pltpu.CompilerParams(dimension_semantics=sem, vmem_limit_bytes=VMEM_LIMIT)


def _table_kernel(h0, row_step, sign, d0, blk_d0, lo, hi, mult, rb_ref, out_ref):
    h = pl.program_id(0) + h0
    blk = pl.program_id(1)
    rt, cc = out_ref.shape[2], out_ref.shape[3]
    base = d0 + blk * blk_d0 - sign * row_step * rt * pl.program_id(2)
    width = -(-(row_step * rt + cc) // LANES) * LANES
    w = lax.broadcasted_iota(jnp.int32, (8, width), 1)
    dist = base + sign * jnp.where(w < cc, w, w - width)
    acc = jnp.full(dist.shape, rb_ref[h, 0], F32)
    for k in range(1, REL_BUCKETS):
        acc = jnp.where(dist >= _THRESH[k], rb_ref[h, k], acc)
    row = jnp.where((dist >= lo) & (dist < hi), acc * mult, NEG)
    rolled = pltpu.roll(jnp.broadcast_to(row[:1], (rt, width)), 0, 1, stride=row_step, stride_axis=0)
    out_ref[0, 0] = rolled[:, :cc]


def _bias_table(rel_bias, h0, nh, rows, cols, row_step, sign, d0, lo, hi, rt, mult=1.0, nblk=1, blk_d0=0):
    return pl.pallas_call(
        functools.partial(_table_kernel, h0, row_step, sign, d0, blk_d0, lo, hi, mult),
        grid=(nh, nblk, rows // rt),
        in_specs=[pl.BlockSpec(memory_space=pltpu.SMEM)],
        out_specs=pl.BlockSpec((1, 1, rt, cols), lambda h, b, r: (h, b, r, 0)),
        out_shape=jax.ShapeDtypeStruct((nh, nblk, rows, cols), F32),
        compiler_params=_cparams(("arbitrary", "arbitrary", "arbitrary")),
        name="bias_table",
    )(rel_bias)


def _rms(v, g):
    return v * lax.rsqrt(jnp.mean(v * v, axis=-1, keepdims=True) + NORM_EPS) * g


def _proj_kernel(x_ref, gpre_ref, w_ref, qn_ref, wuq_ref, kvn_ref, wk_ref, wv_ref,
                 rc_ref, rsa_ref, rsb_ref, eaug_ref,
                 qa_ref, qa2_ref, cmp_ref, slck_ref, slcv_ref, win_ref, qm_ref, km_ref, vm_ref,
                 krb_ref, cq_ref, ckv_ref):
    sub = x_ref.shape[0] // PROJ_SUBTILES
    rs = [slice(t * sub, (t + 1) * sub) for t in range(PROJ_SUBTILES)]
    hbs = [_rms(x_ref[r, :], gpre_ref[...]).astype(BF16) for r in rs]
    bounds = ((C_AQ, C_CMP), (C_CMP, C_WIN), (C_WIN, C_KRB), (C_KRB, C_Z))
    chunk_sets = [{ab: _dot(hb, w_ref[:, ab[0]:ab[1]]) for ab in bounds} for hb in hbs]
    lane = lax.broadcasted_iota(jnp.int32, (sub, LANES), 1)
    lo = lane < HEAD_DIM

    for r, chunks in zip(rs, chunk_sets):
        def proj(c0, c1):
            for (a, b), y in chunks.items():
                if a <= c0 and c1 <= b:
                    return y[:, c0 - a:c1 - a]
            raise AssertionError((c0, c1))

        qa_ref[r, :] = (proj(C_AQ, C_CMP) * 0.125).astype(BF16)
        qa2_ref[r, :] = (proj(C_AQ, C_CMP) * (0.125 * LOG2E)).astype(BF16)
        cmp_ref[r, :] = proj(C_CMP, C_SLC)
        slc = proj(C_SLC, C_WIN)
        slcv_ref[r, :] = jnp.where(lo, 1.0, slc).astype(BF16)
        slck_ref[r, :] = jnp.where(lo, slc, eaug_ref[r, :].astype(F32)).astype(BF16)
        win_ref[r, :] = proj(C_WIN, C_BCQ).astype(BF16)
        cq_ref[r, :] = (proj(C_CQ, C_CKV) * 0.125).astype(BF16)
        ckv_ref[r, :] = proj(C_CKV, C_Z).astype(BF16)

        rc, rsa, rsb = rc_ref[r, :], rsa_ref[r, :], rsb_ref[r, :]

        def rope(v):
            return v * rc + pltpu.roll(v, 16, 1) * rsa + pltpu.roll(v, 112, 1) * rsb

        cqn = _rms(proj(C_BCQ, C_BCKV), qn_ref[...]).astype(BF16)
        qm = _dot(cqn, wuq_ref[...])
        for h in range(MLA_HEADS):
            qh = rope(qm[:, LANES * h:LANES * (h + 1)]) * MLA_SCALE2
            qm_ref[r, LANES * h:LANES * (h + 1)] = qh.astype(BF16)

        krb = proj(C_KRB, C_CQ)
        krb_ref[r, :] = krb
        krr = jnp.where(lo | (lane >= HEAD_DIM + MLA_ROPE), 0.0, rope(krb))

        ckvn = _rms(proj(C_BCKV, C_KRB), kvn_ref[...]).astype(BF16)
        kn = _dot(ckvn, wk_ref[...])
        for h in range(MLA_HEADS):
            km_ref[r, LANES * h:LANES * (h + 1)] = (kn[:, LANES * h:LANES * (h + 1)] + krr).astype(BF16)
        vm = _dot(ckvn, wv_ref[...])
        for h in range(MLA_HEADS):
            vm_ref[r, LANES * h:LANES * (h + 1)] = jnp.where(lo, 1.0, vm[:, LANES * h:LANES * (h + 1)]).astype(BF16)


def _project(x2, gpre, w, qn, wuq, kvn, wk, wv, rope_c, rope_sa, rope_sb, eaug, T, tm):
    BT = x2.shape[0]
    nt = T // tm
    row = lambda i: (i, 0)
    fix = lambda i: (0, 0)
    pos = lambda i: (i % nt, 0)
    widths = [(256, BF16), (256, BF16), (128, F32), (128, BF16), (128, BF16), (128, BF16), (512, BF16), (512, BF16),
              (512, BF16), (128, F32), (512, BF16), (128, BF16)]
    return pl.pallas_call(
        _proj_kernel,
        grid=(BT // tm,),
        in_specs=[pl.BlockSpec((tm, D_MODEL), row), pl.BlockSpec((1, D_MODEL), fix),
                  pl.BlockSpec(w.shape, fix), pl.BlockSpec(qn.shape, fix), pl.BlockSpec(wuq.shape, fix),
                  pl.BlockSpec(kvn.shape, fix), pl.BlockSpec(wk.shape, fix), pl.BlockSpec(wv.shape, fix),
                  pl.BlockSpec((tm, LANES), pos), pl.BlockSpec((tm, LANES), pos), pl.BlockSpec((tm, LANES), pos),
                  pl.BlockSpec((tm, LANES), pos)],
        out_specs=[pl.BlockSpec((tm, wd), row) for wd, _ in widths],
        out_shape=[jax.ShapeDtypeStruct((BT, wd), dt) for wd, dt in widths],
        compiler_params=_cparams(("arbitrary",)),
        name="proj",
    )(x2, gpre, w, qn, wuq, kvn, wk, wv, rope_c, rope_sa, rope_sb, eaug)


def _compress_kernel(c_ref, pet_ref, peb_ref, w1t_ref, w1b_ref, w2_ref, o_ref):
    nch = c_ref.shape[1] // CMP_STRIDE
    c = jnp.concatenate([c_ref[0, pl.ds(j, nch, stride=CMP_STRIDE), :] for j in range(CMP_STRIDE)], axis=1)
    top = (c + pet_ref[...]).astype(BF16)
    bot = (c + peb_ref[...]).astype(BF16)
    out = jnp.zeros((nch, LANES), F32)
    for i in range(2):
        pre = _dot(top, w1t_ref[i]) + pltpu.roll(_dot(bot, w1b_ref[i]), nch - 1, 0)
        hid = pre * (1.0 / (1.0 + jnp.exp(-pre)))
        out = out + _dot(hid.astype(BF16), w2_ref[i])
    o_ref[0] = out.astype(BF16)


def _compress(tokens, pet, peb, w1t, w1b, w2):
    B, T, width = tokens.shape
    nch = T // CMP_STRIDE
    full = lambda a: pl.BlockSpec(a.shape, lambda b: (0,) * a.ndim)
    return pl.pallas_call(
        _compress_kernel,
        grid=(B,),
        in_specs=[pl.BlockSpec((1, T, width), lambda b: (b, 0, 0)),
                  full(pet), full(peb), full(w1t), full(w1b), full(w2)],
        out_specs=pl.BlockSpec((1, nch, LANES), lambda b: (b, 0, 0)),
        out_shape=jax.ShapeDtypeStruct((B, nch, LANES), BF16),
        compiler_params=_cparams(("arbitrary",)),
        name="compress",
    )(tokens, pet, peb, w1t, w1b, w2)


def _head_slabs(qblk, n_heads):
    q = qblk.astype(F32)
    lo = lax.broadcasted_iota(jnp.int32, (q.shape[0], LANES), 1) < HEAD_DIM
    out = []
    for j in range(n_heads // 2):
        slab = q[:, LANES * j:LANES * (j + 1)]
        out.append(jnp.where(lo, slab, 0.0))
        out.append(jnp.where(lo, pltpu.roll(slab, HEAD_DIM, 1), 0.0))
    return out


def _merge_upper(accs):
    lo = lax.broadcasted_iota(jnp.int32, accs[0].shape, 1) < HEAD_DIM
    slabs = [jnp.where(lo, pltpu.roll(accs[2 * j], HEAD_DIM, 1), accs[2 * j + 1]) for j in range(len(accs) // 2)]
    return jnp.concatenate(slabs, axis=1)


def _merge_normalized(accs):
    lo = lax.broadcasted_iota(jnp.int32, accs[0].shape, 1) < HEAD_DIM
    slabs = []
    for j in range(len(accs) // 2):
        a, b = accs[2 * j], accs[2 * j + 1]
        out = jnp.where(lo, pltpu.roll(a, HEAD_DIM, 1), b)
        den = jnp.where(lo, a, pltpu.roll(b, HEAD_DIM, 1))
        slabs.append(out / den)
    return jnp.concatenate(slabs, axis=1)


def _row_max_lanes(s):
    return jnp.broadcast_to(jnp.max(s, axis=-1, keepdims=True), (s.shape[0], LANES))


def _sub_lanes(s, ref):
    return jnp.concatenate([s[:, LANES * i:LANES * (i + 1)] - ref for i in range(s.shape[1] // LANES)], axis=1)


def _topk_mask_t(imp, t0):
    ns, tq = imp.shape
    srow = lax.broadcasted_iota(jnp.int32, (ns, tq), 0)
    cur = (t0 + lax.broadcasted_iota(jnp.int32, (ns, tq), 1)) // SEL_BLOCK
    forced = (srow == 0) | (srow == cur) | (srow == cur - 1)
    x = jnp.where(forced, BIG, jnp.where(srow <= cur, imp, -BIG))
    sub = 8
    groups = [x[sub * v:sub * (v + 1)] for v in range(ns // sub)]
    rows_in = [srow[sub * v:sub * (v + 1)] for v in range(ns // sub)]
    cnts = [jnp.zeros((sub, tq), F32) for _ in groups]
    for sp in range(ns):
        other = jnp.broadcast_to(x[sp:sp + 1, :], (sub, tq))
        for v, xv in enumerate(groups):
            if sub * v > sp:
                beats = other >= xv
            elif sub * v + sub - 1 < sp:
                beats = other > xv
            else:
                beats = (other > xv) | ((other == xv) & (rows_in[v] > sp))
            cnts[v] = cnts[v] + jnp.where(beats, 1.0, 0.0)
    cnt = jnp.concatenate(cnts, axis=0)
    sel = (cnt < float(min(SEL_TOP, ns))) & (srow <= cur)
    return jnp.where(sel, 0.0, NEG)


def _cmp_kernel(q_ref, kvc_ref, bc_ref, ovt_ref, ocmp_ref, selb_ref):
    rows = CMP_GROUP * Q_TILE
    ncp, ns = kvc_ref.shape[1], ovt_ref.shape[0]
    step = pl.program_id(1)
    lo_step, nse = 0, SEL_TOP
    while lo_step * rows < ns * SEL_BLOCK:
        nse = min(nse, ns)
        hi_step = nse * SEL_BLOCK // rows
        ncw = min(ncp, max(LANES, nse * (SEL_BLOCK // CMP_STRIDE)))
        if hi_step > lo_step:
            pl.when((step >= lo_step) & (step < hi_step))(
                functools.partial(_cmp_body, ncw, nse, q_ref, kvc_ref, bc_ref, ovt_ref, ocmp_ref, selb_ref))
            lo_step = hi_step
        nse *= 2


def _cmp_body(ncw, ns, q_ref, kvc_ref, bc_ref, ovt_ref, ocmp_ref, selb_ref):
    tq = Q_TILE
    kvc = kvc_ref[0, :ncw, :]
    ovt = ovt_ref[:ns, :ncw]
    groups = range(CMP_GROUP)
    sls = [slice(g * tq, (g + 1) * tq) for g in groups]
    s_alls = []
    for g in groups:
        qs = _head_slabs(q_ref[0, sls[g], :], NSA_HEADS)
        s_alls.append(_dot_nt(jnp.concatenate([q.astype(BF16) for q in qs], axis=0), kvc))
    psums, o_alls = [], []
    for g in groups:
        psum = None
        ps = []
        for h in range(NSA_HEADS):
            s = s_alls[g][h * tq:(h + 1) * tq] + bc_ref[h, sls[g], :ncw]
            m = jnp.max(s, axis=-1, keepdims=True)
            e = jnp.exp(s - m)
            l = jnp.sum(e, axis=-1, keepdims=True)
            p = e * jnp.where(m > 0.5 * NEG, 1.0 / l, 0.0)
            psum = p if psum is None else psum + p
            ps.append(p.astype(BF16))
        psums.append(psum)
        o_alls.append(_dot(jnp.concatenate(ps, axis=0), kvc))
    imps = []
    for g in groups:
        hi = psums[g].astype(BF16)
        lo = (psums[g] - hi.astype(F32)).astype(BF16)
        imps.append(_dot_nt(ovt, hi) + _dot_nt(ovt, lo))
    for g in groups:
        ocmp = _merge_upper([o_alls[g][h * tq:(h + 1) * tq] for h in range(NSA_HEADS)])
        ocmp_ref[0, sls[g], :] = ocmp.astype(ocmp_ref.dtype)
        sb = _topk_mask_t(imps[g], (pl.program_id(1) * CMP_GROUP + g) * tq)
        if ns < HEAD_DIM:
            sb = jnp.concatenate([sb, jnp.full((HEAD_DIM - ns, tq), NEG, F32)], axis=0)
        full = jnp.concatenate([jnp.zeros((HEAD_DIM, tq), F32), sb], axis=0)
        selb_ref[0, sls[g], :] = full.T.astype(BF16)


def _cmp_attention(qa, kvc, bc, ovt, T):
    B = qa.shape[0]
    ncp = kvc.shape[1]
    rows = CMP_GROUP * Q_TILE
    return pl.pallas_call(
        _cmp_kernel,
        grid=(B, T // rows),
        in_specs=[pl.BlockSpec((1, rows, 256), lambda b, n: (b, n, 0)),
                  pl.BlockSpec((1, ncp, LANES), lambda b, n: (b, 0, 0)),
                  pl.BlockSpec((NSA_HEADS, rows, ncp), lambda b, n: (0, n, 0)),
                  pl.BlockSpec(ovt.shape, lambda b, n: (0, 0))],
        out_specs=[pl.BlockSpec((1, rows, 256), lambda b, n: (b, n, 0)),
                   pl.BlockSpec((1, rows, LANES), lambda b, n: (b, n, 0))],
        out_shape=[jax.ShapeDtypeStruct((B, T, 256), BF16), jax.ShapeDtypeStruct((B, T, LANES), BF16)],
        compiler_params=_cparams(("arbitrary", "arbitrary")),
        name="cmp_select",
    )(qa, kvc, bc, ovt)


def _softmax_tile(s2, rows, m_s, shift=None):
    m_t = _row_max_lanes(s2)
    if shift is not None:
        m_t = m_t + shift
    m_old = m_s[rows]
    m_new = jnp.maximum(m_old, m_t)
    p = jnp.exp2(_sub_lanes(s2, m_new if shift is None else m_new - shift))
    alpha = jnp.exp2(m_old - m_new)
    m_s[rows] = m_new
    return p.astype(BF16), alpha


def _slc_kernel(b31_ref, q_ref, selb_ref, ka_ref, kv_ref, tab_ref, o_ref, m_s, acc_s):
    tq = q_ref.shape[1]
    H = NSA_HEADS
    n = pl.program_id(1)
    qs = _head_slabs(q_ref[0], H)
    selb = selb_ref[0].astype(F32)
    qst = jnp.concatenate([(qs[h] + selb).astype(BF16) for h in range(H)], axis=0)
    m_s[...] = jnp.full(m_s.shape, -jnp.inf, F32)
    acc_s[...] = jnp.zeros(acc_s.shape, F32)

    def scores(k0, width=tq):
        k0 = pl.multiple_of(k0, tq)
        return _dot_nt(qst, ka_ref[0, pl.ds(k0, width), :]), kv_ref[0, pl.ds(k0, width), :]

    def update(s_all, kv):
        ps, alphas = [], []
        for h in range(H):
            rows = slice(h * tq, (h + 1) * tq)
            p, alpha = _softmax_tile(s_all[rows], rows, m_s, shift=b31_ref[h])
            ps.append(p)
            alphas.append(alpha)
        pv = _dot(jnp.concatenate(ps, axis=0), kv)
        for h in range(H):
            rows = slice(h * tq, (h + 1) * tq)
            acc_s[rows] = alphas[h] * acc_s[rows] + pv[rows]

    def far_pair(i, carry):
        update(*scores(2 * i * tq, 2 * tq))
        return carry

    n_far = jnp.maximum(n - 1, 0)
    lax.fori_loop(0, n_far // 2, far_pair, 0)

    @pl.when(n_far % 2 == 1)
    def _():
        update(*scores((n_far - 1) * tq))

    def near_diagonal(k0, before):
        k0 = pl.multiple_of(k0, tq)
        half = tq // SLC_DIAG_PARTS
        parts = []
        for part in range(SLC_DIAG_PARTS):
            kw = before + half * (part + 1)
            q_part = jnp.concatenate([qst[h * tq + part * half:h * tq + (part + 1) * half] for h in range(H)], axis=0)
            parts.append((_dot_nt(q_part, ka_ref[0, pl.ds(k0, kw), :]), kv_ref[0, pl.ds(k0, kw), :], kw))
        for part, (s_all, kv, kw) in enumerate(parts):
            ps, alphas = [], []
            for h in range(H):
                rows = slice(h * tq + part * half, h * tq + (part + 1) * half)
                bias = tab_ref[h, part * half:(part + 1) * half, tq - before:tq - before + kw]
                p, alpha = _softmax_tile(s_all[h * half:(h + 1) * half] + bias, rows, m_s)
                ps.append(p)
                alphas.append(alpha)
            pv = _dot(jnp.concatenate(ps, axis=0), kv)
            for h in range(H):
                rows = slice(h * tq + part * half, h * tq + (part + 1) * half)
                acc_s[rows] = alphas[h] * acc_s[rows] + pv[h * half:(h + 1) * half]

    @pl.when(n >= 1)
    def _():
        near_diagonal(jnp.maximum(n - 1, 0) * tq, tq)

    @pl.when(n == 0)
    def _():
        near_diagonal(0, 0)

    o_ref[0] = _merge_normalized([acc_s[h * tq:(h + 1) * tq] for h in range(H)]).astype(o_ref.dtype)


def _slc_attention(b31, qa, selb, slck, slcv, tab, T):
    B = qa.shape[0]
    tq = FLASH_TILE
    H = NSA_HEADS
    tile = lambda w: pl.BlockSpec((1, tq, w), lambda b, n: (b, n, 0))
    return pl.pallas_call(
        _slc_kernel,
        grid=(B, T // tq),
        in_specs=[pl.BlockSpec(memory_space=pltpu.SMEM),
                  tile(256), tile(LANES),
                  pl.BlockSpec((1, T, LANES), lambda b, n: (b, 0, 0)),
                  pl.BlockSpec((1, T, LANES), lambda b, n: (b, 0, 0)),
                  pl.BlockSpec(tab.shape, lambda b, n: (0, 0, 0))],
        out_specs=tile(256),
        out_shape=jax.ShapeDtypeStruct((B, T, 256), BF16),
        scratch_shapes=[pltpu.VMEM((H * tq, LANES), F32), pltpu.VMEM((H * tq, LANES), F32)],
        compiler_params=_cparams(("arbitrary", "arbitrary")),
        name="slc_attention",
    )(b31, qa, selb, slck, slcv, tab)


def _band_kernel(n_heads, window, has_sink, n_sub, tq, *refs):
    if has_sink:
        q_ref, kv_ref, tab_ref, sink_ref, o_ref = refs
    else:
        q_ref, kv_ref, tab_ref, o_ref = refs
    span = window + tq
    lo_kv = lax.broadcasted_iota(jnp.int32, (span, LANES), 1) < HEAD_DIM
    lo_o = lax.broadcasted_iota(jnp.int32, (n_heads * tq, LANES), 1) < HEAD_DIM
    groups = range(n_sub)
    tiles = [pl.program_id(1) * n_sub + g for g in groups]
    kvs, dots = [], []
    for g in groups:
        k0 = pl.multiple_of(jnp.maximum(tiles[g] * tq - window, 0), tq)
        kvs.append(kv_ref[0, pl.ds(k0, span), :])
        qs = _head_slabs(q_ref[0, g * tq:(g + 1) * tq, :], n_heads)
        dots.append(_dot_nt(jnp.concatenate([q.astype(BF16) for q in qs], axis=0), kvs[g]))
    accs, ms = [], []
    for g in groups:
        s = dots[g] + tab_ref[jnp.minimum(tiles[g], window // tq)]
        m = _row_max_lanes(s)
        if has_sink:
            m = jnp.maximum(m, sink_ref[...])
        e = jnp.exp(_sub_lanes(s, m))
        ms.append(m)
        accs.append(_dot(e.astype(BF16), jnp.where(lo_kv, 1.0, kvs[g]).astype(BF16)))
    for g in groups:
        acc = accs[g]
        if has_sink:
            acc = acc + jnp.where(lo_o, jnp.exp(sink_ref[...] - ms[g]), 0.0)
        out = _merge_normalized([acc[h * tq:(h + 1) * tq] for h in range(n_heads)])
        o_ref[0, g * tq:(g + 1) * tq, :] = out.astype(o_ref.dtype)


def _band_attention(q, kv_pad, tab, sinks, n_heads, window, T, out_dtype):
    B = q.shape[0]
    width = n_heads * HEAD_DIM
    has_sink = sinks is not None
    nvar, tq = tab.shape[1], tab.shape[2]
    tab2 = jnp.transpose(tab, (1, 0, 2, 3)).reshape(nvar, n_heads * tq, window + tq)
    n_sub = (BAND_ROWS if has_sink else 2 * BAND_ROWS) // (n_heads * tq)
    rows = n_sub * tq
    in_specs = [pl.BlockSpec((1, rows, width), lambda b, n: (b, n, 0)),
                pl.BlockSpec((1, T, LANES), lambda b, n: (b, 0, 0)),
                pl.BlockSpec(tab2.shape, lambda b, n: (0, 0, 0))]
    args = (q, kv_pad, tab2)
    if has_sink:
        rep = jnp.broadcast_to(jnp.repeat(sinks, tq)[:, None], (n_heads * tq, LANES))
        in_specs.append(pl.BlockSpec(rep.shape, lambda b, n: (0, 0)))
        args = args + (rep,)
    return pl.pallas_call(
        functools.partial(_band_kernel, n_heads, window, has_sink, n_sub, tq),
        grid=(B, T // rows),
        in_specs=in_specs,
        out_specs=pl.BlockSpec((1, rows, width), lambda b, n: (b, n, 0)),
        out_shape=jax.ShapeDtypeStruct((B, T, width), out_dtype),
        compiler_params=_cparams(("arbitrary", "arbitrary")),
        name="band_sink" if has_sink else "band_window",
    )(*args)


def _mla_kernel(q_ref, k_ref, v_ref, o_ref, m_s, acc_s):
    tq = q_ref.shape[1]
    H = MLA_HEADS
    n = pl.program_id(1)
    m_s[...] = jnp.full(m_s.shape, -jnp.inf, F32)
    acc_s[...] = jnp.zeros(acc_s.shape, F32)

    def scores(k0, width=tq):
        k0 = pl.multiple_of(k0, tq)
        return k0, width, [_dot_nt(q_ref[0, :, LANES * h:LANES * (h + 1)],
                                   k_ref[0, pl.ds(k0, width), LANES * h:LANES * (h + 1)]) for h in range(H)]

    def update(k0, width, ss):
        ps, alphas = [], []
        for h in range(H):
            p, alpha = _softmax_tile(ss[h], slice(h * tq, (h + 1) * tq), m_s)
            ps.append(p)
            alphas.append(alpha)
        for h in range(H):
            rows = slice(h * tq, (h + 1) * tq)
            v = v_ref[0, pl.ds(k0, width), LANES * h:LANES * (h + 1)]
            acc_s[rows] = alphas[h] * acc_s[rows] + _dot(ps[h], v)

    def far_pair(i, carry):
        update(*scores(2 * i * tq, 2 * tq))
        return carry

    n_far = jnp.maximum(n - 1, 0)
    lax.fori_loop(0, n_far // 2, far_pair, 0)

    @pl.when(n_far % 2 == 1)
    def _():
        update(*scores((n_far - 1) * tq))

    def near_diagonal(k0, before):
        k0 = pl.multiple_of(k0, tq)
        half = tq // MLA_DIAG_PARTS
        parts = []
        for part in range(MLA_DIAG_PARTS):
            kw = before + half * (part + 1)
            r = slice(part * half, (part + 1) * half)
            parts.append((kw, [_dot_nt(q_ref[0, r, LANES * h:LANES * (h + 1)],
                                       k_ref[0, pl.ds(k0, kw), LANES * h:LANES * (h + 1)]) for h in range(H)]))
        for part, (kw, ss) in enumerate(parts):
            row = lax.broadcasted_iota(jnp.int32, (half, kw), 0)
            col = lax.broadcasted_iota(jnp.int32, (half, kw), 1)
            causal = col <= row + (before + part * half)
            ps, alphas = [], []
            for h in range(H):
                rows = slice(h * tq + part * half, h * tq + (part + 1) * half)
                p, alpha = _softmax_tile(jnp.where(causal, ss[h], NEG), rows, m_s)
                ps.append(p)
                alphas.append(alpha)
            for h in range(H):
                rows = slice(h * tq + part * half, h * tq + (part + 1) * half)
                v = v_ref[0, pl.ds(k0, kw), LANES * h:LANES * (h + 1)]
                acc_s[rows] = alphas[h] * acc_s[rows] + _dot(ps[h], v)

    @pl.when(n >= 1)
    def _():
        near_diagonal(jnp.maximum(n - 1, 0) * tq, tq)

    @pl.when(n == 0)
    def _():
        near_diagonal(0, 0)

    o_ref[0] = _merge_normalized([acc_s[h * tq:(h + 1) * tq] for h in range(H)]).astype(o_ref.dtype)


def _mla_attention(qm, km, vm, T):
    B = qm.shape[0]
    tq = FLASH_TILE
    H = MLA_HEADS
    return pl.pallas_call(
        _mla_kernel,
        grid=(B, T // tq),
        in_specs=[pl.BlockSpec((1, tq, 512), lambda b, n: (b, n, 0)),
                  pl.BlockSpec((1, T, 512), lambda b, n: (b, 0, 0)),
                  pl.BlockSpec((1, T, 512), lambda b, n: (b, 0, 0))],
        out_specs=pl.BlockSpec((1, tq, 256), lambda b, n: (b, n, 0)),
        out_shape=jax.ShapeDtypeStruct((B, T, 256), BF16),
        scratch_shapes=[pltpu.VMEM((H * tq, LANES), F32), pltpu.VMEM((H * tq, LANES), F32)],
        compiler_params=_cparams(("arbitrary", "arbitrary")),
        name="mla_attention",
    )(qm, km, vm)


def _out_kernel(x_ref, gpre_ref, wz_ref, ocmp_ref, oslc_ref, owin_ref, krb_ref, gexp_ref, ob_ref, oc_ref, w_ref,
                gpost_ref, o_ref):
    sub = x_ref.shape[0] // OUT_SUBTILES
    rs = [slice(t * sub, (t + 1) * sub) for t in range(OUT_SUBTILES)]
    w_a = NSA_HEADS * HEAD_DIM
    hs = [_rms(x_ref[r, :], gpre_ref[...]).astype(BF16) for r in rs]
    zs = [_dot(h, wz_ref[...]) for h in hs]
    mixeds = []
    for r, z in zip(rs, zs):
        sig = 1.0 / (1.0 + jnp.exp(-krb_ref[r, :]))
        hi = sig.astype(BF16)
        lo = (sig - hi.astype(F32)).astype(BF16)
        gates = _dot(jnp.concatenate([hi, lo], axis=1), gexp_ref[...])
        oa = None
        for j, br in enumerate((ocmp_ref, oslc_ref, owin_ref)):
            term = gates[:, w_a * j:w_a * (j + 1)] * br[r, :].astype(F32)
            oa = term if oa is None else oa + term
        heads = jnp.concatenate([oa, ob_ref[r, :].astype(F32), oc_ref[r, :].astype(F32)], axis=1)
        mixeds.append((heads * (z * (1.0 / (1.0 + jnp.exp(-z))))).astype(BF16))
    ys = [_dot(m, w_ref[...]) for m in mixeds]
    for r, y in zip(rs, ys):
        o_ref[r, :] = x_ref[r, :] + _rms(y, gpost_ref[...])


def _out_project(x2, gpre, wz, ocmp, oslc, owin, krb, gexp, ob, oc, w, gpost, tm):
    BT = x2.shape[0]
    row = lambda i: (i, 0)
    fix = lambda i: (0, 0)
    spec = lambda a: pl.BlockSpec((tm, a.shape[1]), row)
    full = lambda a: pl.BlockSpec(a.shape, fix)
    return pl.pallas_call(
        _out_kernel,
        grid=(BT // tm,),
        in_specs=[spec(x2), full(gpre), full(wz), spec(ocmp), spec(oslc), spec(owin), spec(krb), full(gexp),
                  spec(ob), spec(oc), full(w), full(gpost)],
        out_specs=pl.BlockSpec((tm, D_MODEL), row),
        out_shape=jax.ShapeDtypeStruct((BT, D_MODEL), F32),
        compiler_params=_cparams(("arbitrary",)),
        name="out_proj",
    )(x2, gpre, wz, ocmp, oslc, owin, krb, gexp, ob, oc, w, gpost)


def _split_in(w):
    outs, o = [], 0
    for s in IN_SIZES:
        outs.append(w[..., o:o + s])
        o += s
    return outs


def _relayout_kernel(w_ref, wp_ref, wz_ref):
    a_q, a_kv, a_g, b_cq, b_ckv, b_kr, c_q, c_kv, z = _split_in(w_ref[0])
    zeros = lambda n: jnp.zeros((a_q.shape[0], n), F32)
    krb = jnp.concatenate([zeros(HEAD_DIM), b_kr, a_g, zeros(LANES - HEAD_DIM - MLA_ROPE - 3 * NSA_HEADS)], axis=-1)
    wp_ref[0] = jnp.concatenate([a_q, a_kv, b_cq, b_ckv, krb, c_q, c_kv], axis=-1).astype(BF16)
    wz_ref[0] = z.astype(BF16)


def _prep_w_in(w):
    L, d, n = w.shape
    rt = 256
    return pl.pallas_call(
        _relayout_kernel,
        grid=(L, d // rt),
        in_specs=[pl.BlockSpec((1, rt, n), lambda l, r: (l, r, 0))],
        out_specs=[pl.BlockSpec((1, rt, C_Z), lambda l, r: (l, r, 0)),
                   pl.BlockSpec((1, rt, IN_SIZES[-1]), lambda l, r: (l, r, 0))],
        out_shape=[jax.ShapeDtypeStruct((L, d, C_Z), BF16), jax.ShapeDtypeStruct((L, d, IN_SIZES[-1]), BF16)],
        compiler_params=_cparams(("arbitrary", "arbitrary")),
        name="w_in_relayout",
    )(w)


def _prep_mla(w_uq, w_ukv):
    L = w_uq.shape[0]
    dq = MLA_NOPE + MLA_ROPE
    uq = w_uq.reshape(L, MLA_Q_RANK, MLA_HEADS, dq)
    uq = jnp.concatenate([uq, jnp.zeros((L, MLA_Q_RANK, MLA_HEADS, LANES - dq), uq.dtype)], axis=-1)
    ukv = w_ukv.reshape(L, MLA_KV_RANK, MLA_HEADS, MLA_NOPE + MLA_V)
    uk = jnp.concatenate([ukv[..., :MLA_NOPE], jnp.zeros((L, MLA_KV_RANK, MLA_HEADS, LANES - MLA_NOPE), ukv.dtype)], axis=-1)
    uv = jnp.concatenate([jnp.zeros((L, MLA_KV_RANK, MLA_HEADS, LANES - MLA_V), ukv.dtype), ukv[..., MLA_NOPE:]], axis=-1)
    return (uq.reshape(L, MLA_Q_RANK, MLA_HEADS * LANES).astype(BF16),
            uk.reshape(L, MLA_KV_RANK, MLA_HEADS * LANES).astype(BF16),
            uv.reshape(L, MLA_KV_RANK, MLA_HEADS * LANES).astype(BF16))


def _prep_compress(cmp_pos, cmp_w1, cmp_w2):
    L = cmp_pos.shape[0]
    pe = jnp.concatenate([cmp_pos[:, 0], cmp_pos[:, 1]], axis=-1)
    pet = pe[:, :CMP_STRIDE].reshape(L, 1, CMP_STRIDE * LANES)
    peb = pe[:, CMP_STRIDE:].reshape(L, 1, CMP_STRIDE * LANES)
    w1 = cmp_w1.reshape(L, 2, CMP_LEN, HEAD_DIM, CMP_HIDDEN)
    zero = jnp.zeros_like(w1[:, 0])
    w1k = jnp.concatenate([w1[:, 0], zero], axis=2)
    w1v = jnp.concatenate([zero, w1[:, 1]], axis=2)
    w1e = jnp.stack([w1k, w1v], axis=1)
    w1t = w1e[:, :, :CMP_STRIDE].reshape(L, 2, CMP_STRIDE * LANES, CMP_HIDDEN).astype(BF16)
    w1b = w1e[:, :, CMP_STRIDE:].reshape(L, 2, CMP_STRIDE * LANES, CMP_HIDDEN).astype(BF16)
    z2 = jnp.zeros_like(cmp_w2[:, 0])
    w2 = jnp.stack([jnp.concatenate([cmp_w2[:, 0], z2], axis=-1),
                    jnp.concatenate([z2, cmp_w2[:, 1]], axis=-1)], axis=1).astype(BF16)
    return pet, peb, w1t, w1b, w2


def _rope_tables(T):
    half = MLA_ROPE // 2
    inv = ROPE_THETA ** (-jnp.arange(half, dtype=F32) / half)
    ang = jnp.arange(T).astype(F32)[:, None] * inv[None, :]
    cos, sin = jnp.cos(ang), jnp.sin(ang)
    z = lambda n: jnp.zeros((T, n), F32)
    rc = jnp.concatenate([jnp.ones((T, MLA_NOPE), F32), cos, cos, z(LANES - MLA_NOPE - MLA_ROPE)], axis=1)
    rsa = jnp.concatenate([z(MLA_NOPE + half), sin, z(LANES - MLA_NOPE - MLA_ROPE)], axis=1)
    rsb = jnp.concatenate([z(MLA_NOPE), -sin, z(LANES - MLA_NOPE - half)], axis=1)
    return rc, rsa, rsb


def _static_tables(T):
    ns = T // SEL_BLOCK
    nc = T // CMP_STRIDE - 1
    ncp = T // CMP_STRIDE
    sstart = np.arange(ns) * SEL_BLOCK
    cstart = np.arange(nc) * CMP_STRIDE
    overlap = (np.clip(np.minimum(cstart[:, None] + CMP_LEN, sstart[None, :] + SEL_BLOCK)
                       - np.maximum(cstart[:, None], sstart[None, :]), 0, None) / CMP_STRIDE).astype(np.float32)
    ovt = np.zeros((ns, ncp), np.float32)
    ovt[:, :nc] = overlap.T
    eaug = np.zeros((T, LANES), np.float32)
    eaug[np.arange(T), HEAD_DIM + np.arange(T) // SEL_BLOCK] = 1.0
    gexp = np.zeros((2, LANES, 3 * NSA_HEADS * HEAD_DIM), np.float32)
    for h in range(NSA_HEADS):
        for j in range(3):
            c0 = NSA_HEADS * HEAD_DIM * j + HEAD_DIM * h
            gexp[:, GATE_LANE0 + 3 * h + j, c0:c0 + HEAD_DIM] = 1.0
    gexp = gexp.reshape(2 * LANES, -1)
    return jnp.asarray(ovt, BF16), jnp.asarray(eaug, BF16), jnp.asarray(gexp, BF16)


def kernel(x, w_in, w_out, norm_pre, norm_post, cmp_pos, cmp_w1, cmp_w2, mla_q_norm, mla_w_uq, mla_kv_norm,
           mla_w_ukv, swa_sinks, rel_bias):
    B, T, D = x.shape
    depth = w_in.shape[0]
    assert D == D_MODEL and T % (2 * FLASH_TILE) == 0 and T // SEL_BLOCK <= HEAD_DIM
    ncp = T // CMP_STRIDE
    tm = 512

    w_in_p, w_z = _prep_w_in(w_in)
    wuq, wk, wv = _prep_mla(mla_w_uq, mla_w_ukv)
    pet, peb, w1t, w1b, w2 = _prep_compress(cmp_pos, cmp_w1, cmp_w2)
    w_out_b = w_out.astype(BF16)
    rope_c, rope_sa, rope_sb = _rope_tables(T)
    ovt, eaug, gexp = _static_tables(T)

    bc = _bias_table(rel_bias, 0, NSA_HEADS, ncp, T, CMP_STRIDE, 1, -(CMP_LEN - 1), 0, 1 << 30, 32)
    bc = jnp.transpose(bc.reshape(NSA_HEADS, ncp, T), (0, 2, 1))
    tab_near = _bias_table(rel_bias, 0, NSA_HEADS, FLASH_TILE, 2 * FLASH_TILE, 1, -1, FLASH_TILE, 0, 1 << 30, 256,
                           mult=LOG2E).reshape(NSA_HEADS, FLASH_TILE, 2 * FLASH_TILE)
    tab_win = _bias_table(rel_bias, 0, NSA_HEADS, WIN_TILE, NSA_WINDOW + WIN_TILE, 1, -1, 0, 0, NSA_WINDOW, WIN_TILE,
                          nblk=NSA_WINDOW // WIN_TILE + 1, blk_d0=WIN_TILE)
    tab_swa = _bias_table(rel_bias, NSA_HEADS, SWA_HEADS, Q_TILE, SWA_WINDOW + Q_TILE, 1, -1, 0, 0, SWA_WINDOW, Q_TILE,
                          nblk=SWA_WINDOW // Q_TILE + 1, blk_d0=Q_TILE)
    b31 = rel_bias[:NSA_HEADS, REL_BUCKETS - 1] * LOG2E

    x2 = x.reshape(B * T, D)
    for l in range(depth):
        (qa, qa2, cmpkv, slck, slcv, win, qm, km, vm, krb, cq, ckv) = _project(
            x2, norm_pre[l][None], w_in_p[l], mla_q_norm[l][None], wuq[l], mla_kv_norm[l][None], wk[l], wv[l],
            rope_c, rope_sa, rope_sb, eaug, T, 2 * tm)
        r3 = lambda a: a.reshape(B, T, a.shape[-1])
        kvc = _compress(r3(cmpkv), pet[l], peb[l], w1t[l], w1b[l], w2[l])
        qa3 = r3(qa)
        ocmp, selb = _cmp_attention(qa3, kvc, bc, ovt, T)
        owin = _band_attention(qa3, r3(win), tab_win, None, NSA_HEADS, NSA_WINDOW, T, BF16)
        oslc = _slc_attention(b31, r3(qa2), selb, r3(slck), r3(slcv), tab_near, T)
        oc = _band_attention(r3(cq), r3(ckv), tab_swa, swa_sinks[l], SWA_HEADS, SWA_WINDOW, T, BF16)
        ob = _mla_attention(r3(qm), r3(km), r3(vm), T)
        flat = lambda a: a.reshape(B * T, a.shape[-1])
        x2 = _out_project(x2, norm_pre[l][None], w_z[l], flat(ocmp), flat(oslc), flat(owin), krb, gexp,
                          flat(ob), flat(oc), w_out_b[l], norm_post[l][None], OUT_ROWS)
    return x2.reshape(B, T, D)
```

```python
import functools
import math

import numpy as np
import jax
import jax.numpy as jnp
from jax import lax
from jax.experimental import pallas as pl
from jax.experimental.pallas import tpu as pltpu

F32 = jnp.float32
BF16 = jnp.bfloat16

D_MODEL = 1024
HEAD_DIM = 64
NSA_HEADS = 4
CMP_LEN = 32
CMP_STRIDE = 16
CMP_HIDDEN = 128
SEL_BLOCK = 64
SEL_TOP = 16
NSA_WINDOW = 512
MLA_HEADS = 4
MLA_Q_RANK = 256
MLA_KV_RANK = 128
MLA_NOPE = 64
MLA_ROPE = 32
MLA_V = 64
ROPE_THETA = 10000.0
SWA_HEADS = 8
SWA_WINDOW = 128
REL_BUCKETS = 32
REL_MAX_DIST = 512
NORM_EPS = 1e-6
NEG = -1e30
BIG = 1e9
LOG2E = math.log2(math.e)
MLA_SCALE2 = (MLA_NOPE + MLA_ROPE) ** -0.5 * LOG2E
IN_SIZES = (256, 384, 12, 256, 128, 32, 512, 128, 1024)

LANES = 128
Q_TILE = 128
WIN_TILE = 128
BAND_ROWS = 4096
PROJ_ROWS = 1024
OUT_SUBTILES = 4
OUT_ROWS = 1024
CMP_GROUP = 4
FLASH_TILE = 512
SLC_DIAG_PARTS = 4
MLA_DIAG_PARTS = 4
GATE_LANE0 = 96
VMEM_LIMIT = 56 * 1024 * 1024

C_AQ, C_CMP, C_SLC, C_WIN, C_BCQ, C_BCKV, C_KRB, C_CQ, C_CKV, C_Z = (
    0, 256, 384, 512, 640, 896, 1024, 1152, 1664, 1792)


def _dot(a, b):
    return jnp.dot(a, b, preferred_element_type=F32)


def _dot_nt(a, b):
    return lax.dot_general(a, b, (((1,), (1,)), ((), ())), preferred_element_type=F32)


def _bucket_thresholds():
    d = np.arange(0, 4 * REL_MAX_DIST)
    exact = REL_BUCKETS // 2
    large = exact + (np.log(np.maximum(d, 1).astype(np.float32) / np.float32(exact))
                     / np.float32(math.log(REL_MAX_DIST / exact)) * np.float32(REL_BUCKETS - exact)).astype(np.int32)
    b = np.where(d < exact, d, np.minimum(large, REL_BUCKETS - 1))
    assert np.all(np.diff(b) >= 0) and b[-1] == REL_BUCKETS - 1
    return [int(np.argmax(b >= k)) for k in range(REL_BUCKETS)]


_THRESH = _bucket_thresholds()


def _cparams(sem):
    return pltpu.CompilerParams(dimension_semantics=sem, vmem_limit_bytes=VMEM_LIMIT)


def _table_kernel(h0, row_step, sign, d0, blk_d0, lo, hi, mult, rb_ref, out_ref):
    h = pl.program_id(0) + h0
    blk = pl.program_id(1)
    rt, cc = out_ref.shape[2], out_ref.shape[3]
    base = d0 + blk * blk_d0 - sign * row_step * rt * pl.program_id(2)
    width = -(-(row_step * rt + cc) // LANES) * LANES
    w = lax.broadcasted_iota(jnp.int32, (8, width), 1)
    dist = base + sign * jnp.where(w < cc, w, w - width)
    acc = jnp.full(dist.shape, rb_ref[h, 0], F32)
    for k in range(1, REL_BUCKETS):
        acc = jnp.where(dist >= _THRESH[k], rb_ref[h, k], acc)
    row = jnp.where((dist >= lo) & (dist < hi), acc * mult, NEG)
    rolled = pltpu.roll(jnp.broadcast_to(row[:1], (rt, width)), 0, 1, stride=row_step, stride_axis=0)
    out_ref[0, 0] = rolled[:, :cc]


def _bias_table(rel_bias, h0, nh, rows, cols, row_step, sign, d0, lo, hi, rt, mult=1.0, nblk=1, blk_d0=0):
    return pl.pallas_call(
        functools.partial(_table_kernel, h0, row_step, sign, d0, blk_d0, lo, hi, mult),
        grid=(nh, nblk, rows // rt),
        in_specs=[pl.BlockSpec(memory_space=pltpu.SMEM)],
        out_specs=pl.BlockSpec((1, 1, rt, cols), lambda h, b, r: (h, b, r, 0)),
        out_shape=jax.ShapeDtypeStruct((nh, nblk, rows, cols), F32),
        compiler_params=_cparams(("arbitrary", "arbitrary", "arbitrary")),
        name="bias_table",
    )(rel_bias)


def _rms(v, g):
    return v * lax.rsqrt(jnp.mean(v * v, axis=-1, keepdims=True) + NORM_EPS) * g


def _proj_kernel(x_ref, gpre_ref, w_ref, qn_ref, wuq_ref, kvn_ref, wk_ref, wv_ref,
                 rc_ref, rsa_ref, rsb_ref, eaug_ref,
                 qa_ref, qa2_ref, cmp_ref, slck_ref, slcv_ref, win_ref, qm_ref, km_ref, vm_ref,
                 krb_ref, cq_ref, ckv_ref):
    hb = _rms(x_ref[...], gpre_ref[...]).astype(BF16)
    bounds = ((C_AQ, C_CMP), (C_CMP, C_WIN), (C_WIN, C_KRB), (C_KRB, C_Z))
    chunks = {ab: _dot(hb, w_ref[:, ab[0]:ab[1]]) for ab in bounds}
    lane = lax.broadcasted_iota(jnp.int32, (hb.shape[0], LANES), 1)
    lo = lane < HEAD_DIM

    def proj(c0, c1):
        for (a, b), y in chunks.items():
            if a <= c0 and c1 <= b:
                return y[:, c0 - a:c1 - a]
        raise AssertionError((c0, c1))

    qa_ref[...] = (proj(C_AQ, C_CMP) * 0.125).astype(BF16)
    qa2_ref[...] = (proj(C_AQ, C_CMP) * (0.125 * LOG2E)).astype(BF16)
    cmp_ref[...] = proj(C_CMP, C_SLC)
    slc = proj(C_SLC, C_WIN)
    slcv_ref[...] = jnp.where(lo, 1.0, slc).astype(BF16)
    slck_ref[...] = jnp.where(lo, slc, eaug_ref[...].astype(F32)).astype(BF16)
    win_ref[...] = proj(C_WIN, C_BCQ).astype(BF16)
    cq_ref[...] = (proj(C_CQ, C_CKV) * 0.125).astype(BF16)
    ckv_ref[...] = proj(C_CKV, C_Z).astype(BF16)

    rc, rsa, rsb = rc_ref[...], rsa_ref[...], rsb_ref[...]

    def rope(v):
        return v * rc + pltpu.roll(v, 16, 1) * rsa + pltpu.roll(v, 112, 1) * rsb

    cqn = _rms(proj(C_BCQ, C_BCKV), qn_ref[...]).astype(BF16)
    qm = _dot(cqn, wuq_ref[...])
    for h in range(MLA_HEADS):
        qh = rope(qm[:, LANES * h:LANES * (h + 1)]) * MLA_SCALE2
        qm_ref[:, LANES * h:LANES * (h + 1)] = qh.astype(BF16)

    krb = proj(C_KRB, C_CQ)
    krb_ref[...] = krb
    krr = jnp.where(lo | (lane >= HEAD_DIM + MLA_ROPE), 0.0, rope(krb))

    ckvn = _rms(proj(C_BCKV, C_KRB), kvn_ref[...]).astype(BF16)
    kn = _dot(ckvn, wk_ref[...])
    for h in range(MLA_HEADS):
        km_ref[:, LANES * h:LANES * (h + 1)] = (kn[:, LANES * h:LANES * (h + 1)] + krr).astype(BF16)
    vm = _dot(ckvn, wv_ref[...])
    for h in range(MLA_HEADS):
        vm_ref[:, LANES * h:LANES * (h + 1)] = jnp.where(lo, 1.0, vm[:, LANES * h:LANES * (h + 1)]).astype(BF16)


def _project(x2, gpre, w, qn, wuq, kvn, wk, wv, rope_c, rope_sa, rope_sb, eaug, T, tm):
    BT = x2.shape[0]
    nt = T // tm
    row = lambda i: (i, 0)
    fix = lambda i: (0, 0)
    pos = lambda i: (i % nt, 0)
    widths = [(256, BF16), (256, BF16), (128, F32), (128, BF16), (128, BF16), (128, BF16), (512, BF16), (512, BF16),
              (512, BF16), (128, F32), (512, BF16), (128, BF16)]
    return pl.pallas_call(
        _proj_kernel,
        grid=(BT // tm,),
        in_specs=[pl.BlockSpec((tm, D_MODEL), row), pl.BlockSpec((1, D_MODEL), fix),
                  pl.BlockSpec(w.shape, fix), pl.BlockSpec(qn.shape, fix), pl.BlockSpec(wuq.shape, fix),
                  pl.BlockSpec(kvn.shape, fix), pl.BlockSpec(wk.shape, fix), pl.BlockSpec(wv.shape, fix),
                  pl.BlockSpec((tm, LANES), pos), pl.BlockSpec((tm, LANES), pos), pl.BlockSpec((tm, LANES), pos),
                  pl.BlockSpec((tm, LANES), pos)],
        out_specs=[pl.BlockSpec((tm, wd), row) for wd, _ in widths],
        out_shape=[jax.ShapeDtypeStruct((BT, wd), dt) for wd, dt in widths],
        compiler_params=_cparams(("arbitrary",)),
        name="proj",
    )(x2, gpre, w, qn, wuq, kvn, wk, wv, rope_c, rope_sa, rope_sb, eaug)


def _compress_kernel(c_ref, pet_ref, peb_ref, w1t_ref, w1b_ref, w2_ref, o_ref):
    nch = c_ref.shape[1] // CMP_STRIDE
    c = jnp.concatenate([c_ref[0, pl.ds(j, nch, stride=CMP_STRIDE), :] for j in range(CMP_STRIDE)], axis=1)
    top = (c + pet_ref[...]).astype(BF16)
    bot = (c + peb_ref[...]).astype(BF16)
    out = jnp.zeros((nch, LANES), F32)
    for i in range(2):
        pre = _dot(top, w1t_ref[i]) + pltpu.roll(_dot(bot, w1b_ref[i]), nch - 1, 0)
        hid = pre * (1.0 / (1.0 + jnp.exp(-pre)))
        out = out + _dot(hid.astype(BF16), w2_ref[i])
    o_ref[0] = out.astype(BF16)


def _compress(tokens, pet, peb, w1t, w1b, w2):
    B, T, width = tokens.shape
    nch = T // CMP_STRIDE
    full = lambda a: pl.BlockSpec(a.shape, lambda b: (0,) * a.ndim)
    return pl.pallas_call(
        _compress_kernel,
        grid=(B,),
        in_specs=[pl.BlockSpec((1, T, width), lambda b: (b, 0, 0)),
                  full(pet), full(peb), full(w1t), full(w1b), full(w2)],
        out_specs=pl.BlockSpec((1, nch, LANES), lambda b: (b, 0, 0)),
        out_shape=jax.ShapeDtypeStruct((B, nch, LANES), BF16),
        compiler_params=_cparams(("arbitrary",)),
        name="compress",
    )(tokens, pet, peb, w1t, w1b, w2)


def _head_slabs(qblk, n_heads):
    q = qblk.astype(F32)
    lo = lax.broadcasted_iota(jnp.int32, (q.shape[0], LANES), 1) < HEAD_DIM
    out = []
    for j in range(n_heads // 2):
        slab = q[:, LANES * j:LANES * (j + 1)]
        out.append(jnp.where(lo, slab, 0.0))
        out.append(jnp.where(lo, pltpu.roll(slab, HEAD_DIM, 1), 0.0))
    return out


def _merge_upper(accs):
    lo = lax.broadcasted_iota(jnp.int32, accs[0].shape, 1) < HEAD_DIM
    slabs = [jnp.where(lo, pltpu.roll(accs[2 * j], HEAD_DIM, 1), accs[2 * j + 1]) for j in range(len(accs) // 2)]
    return jnp.concatenate(slabs, axis=1)


def _merge_normalized(accs):
    lo = lax.broadcasted_iota(jnp.int32, accs[0].shape, 1) < HEAD_DIM
    slabs = []
    for j in range(len(accs) // 2):
        a, b = accs[2 * j], accs[2 * j + 1]
        out = jnp.where(lo, pltpu.roll(a, HEAD_DIM, 1), b)
        den = jnp.where(lo, a, pltpu.roll(b, HEAD_DIM, 1))
        slabs.append(out / den)
    return jnp.concatenate(slabs, axis=1)


def _row_max_lanes(s):
    return jnp.broadcast_to(jnp.max(s, axis=-1, keepdims=True), (s.shape[0], LANES))


def _sub_lanes(s, ref):
    return jnp.concatenate([s[:, LANES * i:LANES * (i + 1)] - ref for i in range(s.shape[1] // LANES)], axis=1)


def _topk_mask_t(imp, t0):
    ns, tq = imp.shape
    srow = lax.broadcasted_iota(jnp.int32, (ns, tq), 0)
    cur = (t0 + lax.broadcasted_iota(jnp.int32, (ns, tq), 1)) // SEL_BLOCK
    forced = (srow == 0) | (srow == cur) | (srow == cur - 1)
    x = jnp.where(forced, BIG, jnp.where(srow <= cur, imp, -BIG))
    sub = 8
    groups = [x[sub * v:sub * (v + 1)] for v in range(ns // sub)]
    rows_in = [srow[sub * v:sub * (v + 1)] for v in range(ns // sub)]
    cnts = [jnp.zeros((sub, tq), F32) for _ in groups]
    for sp in range(ns):
        other = jnp.broadcast_to(x[sp:sp + 1, :], (sub, tq))
        for v, xv in enumerate(groups):
            if sub * v > sp:
                beats = other >= xv
            elif sub * v + sub - 1 < sp:
                beats = other > xv
            else:
                beats = (other > xv) | ((other == xv) & (rows_in[v] > sp))
            cnts[v] = cnts[v] + jnp.where(beats, 1.0, 0.0)
    cnt = jnp.concatenate(cnts, axis=0)
    sel = (cnt < float(min(SEL_TOP, ns))) & (srow <= cur)
    return jnp.where(sel, 0.0, NEG)


def _cmp_kernel(q_ref, kvc_ref, bc_ref, ovt_ref, ocmp_ref, selb_ref):
    rows = CMP_GROUP * Q_TILE
    ncp, ns = kvc_ref.shape[1], ovt_ref.shape[0]
    step = pl.program_id(1)
    lo_step, nse = 0, SEL_TOP
    while lo_step * rows < ns * SEL_BLOCK:
        nse = min(nse, ns)
        hi_step = nse * SEL_BLOCK // rows
        ncw = min(ncp, max(LANES, nse * (SEL_BLOCK // CMP_STRIDE)))
        if hi_step > lo_step:
            pl.when((step >= lo_step) & (step < hi_step))(
                functools.partial(_cmp_body, ncw, nse, q_ref, kvc_ref, bc_ref, ovt_ref, ocmp_ref, selb_ref))
            lo_step = hi_step
        nse *= 2


def _cmp_body(ncw, ns, q_ref, kvc_ref, bc_ref, ovt_ref, ocmp_ref, selb_ref):
    tq = Q_TILE
    kvc = kvc_ref[0, :ncw, :]
    ovt = ovt_ref[:ns, :ncw]
    groups = range(CMP_GROUP)
    sls = [slice(g * tq, (g + 1) * tq) for g in groups]
    s_alls = []
    for g in groups:
        qs = _head_slabs(q_ref[0, sls[g], :], NSA_HEADS)
        s_alls.append(_dot_nt(jnp.concatenate([q.astype(BF16) for q in qs], axis=0), kvc))
    psums, o_alls = [], []
    for g in groups:
        psum = None
        ps = []
        for h in range(NSA_HEADS):
            s = s_alls[g][h * tq:(h + 1) * tq] + bc_ref[h, sls[g], :ncw]
            m = jnp.max(s, axis=-1, keepdims=True)
            e = jnp.exp(s - m)
            l = jnp.sum(e, axis=-1, keepdims=True)
            p = e * jnp.where(m > 0.5 * NEG, 1.0 / l, 0.0)
            psum = p if psum is None else psum + p
            ps.append(p.astype(BF16))
        psums.append(psum)
        o_alls.append(_dot(jnp.concatenate(ps, axis=0), kvc))
    imps = []
    for g in groups:
        hi = psums[g].astype(BF16)
        lo = (psums[g] - hi.astype(F32)).astype(BF16)
        imps.append(_dot_nt(ovt, hi) + _dot_nt(ovt, lo))
    for g in groups:
        ocmp = _merge_upper([o_alls[g][h * tq:(h + 1) * tq] for h in range(NSA_HEADS)])
        ocmp_ref[0, sls[g], :] = ocmp.astype(ocmp_ref.dtype)
        sb = _topk_mask_t(imps[g], (pl.program_id(1) * CMP_GROUP + g) * tq)
        if ns < HEAD_DIM:
            sb = jnp.concatenate([sb, jnp.full((HEAD_DIM - ns, tq), NEG, F32)], axis=0)
        full = jnp.concatenate([jnp.zeros((HEAD_DIM, tq), F32), sb], axis=0)
        selb_ref[0, sls[g], :] = full.T.astype(BF16)


def _cmp_attention(qa, kvc, bc, ovt, T):
    B = qa.shape[0]
    ncp = kvc.shape[1]
    rows = CMP_GROUP * Q_TILE
    return pl.pallas_call(
        _cmp_kernel,
        grid=(B, T // rows),
        in_specs=[pl.BlockSpec((1, rows, 256), lambda b, n: (b, n, 0)),
                  pl.BlockSpec((1, ncp, LANES), lambda b, n: (b, 0, 0)),
                  pl.BlockSpec((NSA_HEADS, rows, ncp), lambda b, n: (0, n, 0)),
                  pl.BlockSpec(ovt.shape, lambda b, n: (0, 0))],
        out_specs=[pl.BlockSpec((1, rows, 256), lambda b, n: (b, n, 0)),
                   pl.BlockSpec((1, rows, LANES), lambda b, n: (b, n, 0))],
        out_shape=[jax.ShapeDtypeStruct((B, T, 256), BF16), jax.ShapeDtypeStruct((B, T, LANES), BF16)],
        compiler_params=_cparams(("arbitrary", "arbitrary")),
        name="cmp_select",
    )(qa, kvc, bc, ovt)


def _softmax_tile(s2, rows, m_s, shift=None):
    m_t = _row_max_lanes(s2)
    if shift is not None:
        m_t = m_t + shift
    m_old = m_s[rows]
    m_new = jnp.maximum(m_old, m_t)
    p = jnp.exp2(_sub_lanes(s2, m_new if shift is None else m_new - shift))
    alpha = jnp.exp2(m_old - m_new)
    m_s[rows] = m_new
    return p.astype(BF16), alpha


def _slc_kernel(b31_ref, q_ref, selb_ref, ka_ref, kv_ref, tab_ref, o_ref, m_s, acc_s):
    tq = q_ref.shape[1]
    H = NSA_HEADS
    n = pl.program_id(1)
    qs = _head_slabs(q_ref[0], H)
    selb = selb_ref[0].astype(F32)
    qst = jnp.concatenate([(qs[h] + selb).astype(BF16) for h in range(H)], axis=0)
    m_s[...] = jnp.full(m_s.shape, -jnp.inf, F32)
    acc_s[...] = jnp.zeros(acc_s.shape, F32)

    def scores(k0, width=tq):
        k0 = pl.multiple_of(k0, tq)
        return _dot_nt(qst, ka_ref[0, pl.ds(k0, width), :]), kv_ref[0, pl.ds(k0, width), :]

    def update(s_all, kv):
        ps, alphas = [], []
        for h in range(H):
            rows = slice(h * tq, (h + 1) * tq)
            p, alpha = _softmax_tile(s_all[rows], rows, m_s, shift=b31_ref[h])
            ps.append(p)
            alphas.append(alpha)
        pv = _dot(jnp.concatenate(ps, axis=0), kv)
        for h in range(H):
            rows = slice(h * tq, (h + 1) * tq)
            acc_s[rows] = alphas[h] * acc_s[rows] + pv[rows]

    def far_pair(i, carry):
        update(*scores(2 * i * tq, 2 * tq))
        return carry

    n_far = jnp.maximum(n - 1, 0)
    lax.fori_loop(0, n_far // 2, far_pair, 0)

    @pl.when(n_far % 2 == 1)
    def _():
        update(*scores((n_far - 1) * tq))

    def near_diagonal(k0, before):
        k0 = pl.multiple_of(k0, tq)
        half = tq // SLC_DIAG_PARTS
        parts = []
        for part in range(SLC_DIAG_PARTS):
            kw = before + half * (part + 1)
            q_part = jnp.concatenate([qst[h * tq + part * half:h * tq + (part + 1) * half] for h in range(H)], axis=0)
            parts.append((_dot_nt(q_part, ka_ref[0, pl.ds(k0, kw), :]), kv_ref[0, pl.ds(k0, kw), :], kw))
        for part, (s_all, kv, kw) in enumerate(parts):
            ps, alphas = [], []
            for h in range(H):
                rows = slice(h * tq + part * half, h * tq + (part + 1) * half)
                bias = tab_ref[h, part * half:(part + 1) * half, tq - before:tq - before + kw]
                p, alpha = _softmax_tile(s_all[h * half:(h + 1) * half] + bias, rows, m_s)
                ps.append(p)
                alphas.append(alpha)
            pv = _dot(jnp.concatenate(ps, axis=0), kv)
            for h in range(H):
                rows = slice(h * tq + part * half, h * tq + (part + 1) * half)
                acc_s[rows] = alphas[h] * acc_s[rows] + pv[h * half:(h + 1) * half]

    @pl.when(n >= 1)
    def _():
        near_diagonal(jnp.maximum(n - 1, 0) * tq, tq)

    @pl.when(n == 0)
    def _():
        near_diagonal(0, 0)

    o_ref[0] = _merge_normalized([acc_s[h * tq:(h + 1) * tq] for h in range(H)]).astype(o_ref.dtype)


def _slc_attention(b31, qa, selb, slck, slcv, tab, T):
    B = qa.shape[0]
    tq = FLASH_TILE
    H = NSA_HEADS
    tile = lambda w: pl.BlockSpec((1, tq, w), lambda b, n: (b, n, 0))
    return pl.pallas_call(
        _slc_kernel,
        grid=(B, T // tq),
        in_specs=[pl.BlockSpec(memory_space=pltpu.SMEM),
                  tile(256), tile(LANES),
                  pl.BlockSpec((1, T, LANES), lambda b, n: (b, 0, 0)),
                  pl.BlockSpec((1, T, LANES), lambda b, n: (b, 0, 0)),
                  pl.BlockSpec(tab.shape, lambda b, n: (0, 0, 0))],
        out_specs=tile(256),
        out_shape=jax.ShapeDtypeStruct((B, T, 256), BF16),
        scratch_shapes=[pltpu.VMEM((H * tq, LANES), F32), pltpu.VMEM((H * tq, LANES), F32)],
        compiler_params=_cparams(("arbitrary", "arbitrary")),
        name="slc_attention",
    )(b31, qa, selb, slck, slcv, tab)


def _band_kernel(n_heads, window, has_sink, n_sub, tq, *refs):
    if has_sink:
        q_ref, kv_ref, tab_ref, sink_ref, o_ref = refs
    else:
        q_ref, kv_ref, tab_ref, o_ref = refs
    span = window + tq
    lo_kv = lax.broadcasted_iota(jnp.int32, (span, LANES), 1) < HEAD_DIM
    lo_o = lax.broadcasted_iota(jnp.int32, (n_heads * tq, LANES), 1) < HEAD_DIM
    groups = range(n_sub)
    tiles = [pl.program_id(1) * n_sub + g for g in groups]
    kvs, dots = [], []
    for g in groups:
        k0 = pl.multiple_of(jnp.maximum(tiles[g] * tq - window, 0), tq)
        kvs.append(kv_ref[0, pl.ds(k0, span), :])
        qs = _head_slabs(q_ref[0, g * tq:(g + 1) * tq, :], n_heads)
        dots.append(_dot_nt(jnp.concatenate([q.astype(BF16) for q in qs], axis=0), kvs[g]))
    accs, ms = [], []
    for g in groups:
        s = dots[g] + tab_ref[jnp.minimum(tiles[g], window // tq)]
        m = _row_max_lanes(s)
        if has_sink:
            m = jnp.maximum(m, sink_ref[...])
        e = jnp.exp(_sub_lanes(s, m))
        ms.append(m)
        accs.append(_dot(e.astype(BF16), jnp.where(lo_kv, 1.0, kvs[g]).astype(BF16)))
    for g in groups:
        acc = accs[g]
        if has_sink:
            acc = acc + jnp.where(lo_o, jnp.exp(sink_ref[...] - ms[g]), 0.0)
        out = _merge_normalized([acc[h * tq:(h + 1) * tq] for h in range(n_heads)])
        o_ref[0, g * tq:(g + 1) * tq, :] = out.astype(o_ref.dtype)


def _band_attention(q, kv_pad, tab, sinks, n_heads, window, T, out_dtype):
    B = q.shape[0]
    width = n_heads * HEAD_DIM
    has_sink = sinks is not None
    nvar, tq = tab.shape[1], tab.shape[2]
    tab2 = jnp.transpose(tab, (1, 0, 2, 3)).reshape(nvar, n_heads * tq, window + tq)
    n_sub = (BAND_ROWS if has_sink else 2 * BAND_ROWS) // (n_heads * tq)
    rows = n_sub * tq
    in_specs = [pl.BlockSpec((1, rows, width), lambda b, n: (b, n, 0)),
                pl.BlockSpec((1, T, LANES), lambda b, n: (b, 0, 0)),
                pl.BlockSpec(tab2.shape, lambda b, n: (0, 0, 0))]
    args = (q, kv_pad, tab2)
    if has_sink:
        rep = jnp.broadcast_to(jnp.repeat(sinks, tq)[:, None], (n_heads * tq, LANES))
        in_specs.append(pl.BlockSpec(rep.shape, lambda b, n: (0, 0)))
        args = args + (rep,)
    return pl.pallas_call(
        functools.partial(_band_kernel, n_heads, window, has_sink, n_sub, tq),
        grid=(B, T // rows),
        in_specs=in_specs,
        out_specs=pl.BlockSpec((1, rows, width), lambda b, n: (b, n, 0)),
        out_shape=jax.ShapeDtypeStruct((B, T, width), out_dtype),
        compiler_params=_cparams(("arbitrary", "arbitrary")),
        name="band_sink" if has_sink else "band_window",
    )(*args)


def _mla_kernel(q_ref, k_ref, v_ref, o_ref, m_s, acc_s):
    tq = q_ref.shape[1]
    H = MLA_HEADS
    n = pl.program_id(1)
    m_s[...] = jnp.full(m_s.shape, -jnp.inf, F32)
    acc_s[...] = jnp.zeros(acc_s.shape, F32)

    def scores(k0, width=tq):
        k0 = pl.multiple_of(k0, tq)
        return k0, width, [_dot_nt(q_ref[0, :, LANES * h:LANES * (h + 1)],
                                   k_ref[0, pl.ds(k0, width), LANES * h:LANES * (h + 1)]) for h in range(H)]

    def update(k0, width, ss):
        ps, alphas = [], []
        for h in range(H):
            p, alpha = _softmax_tile(ss[h], slice(h * tq, (h + 1) * tq), m_s)
            ps.append(p)
            alphas.append(alpha)
        for h in range(H):
            rows = slice(h * tq, (h + 1) * tq)
            v = v_ref[0, pl.ds(k0, width), LANES * h:LANES * (h + 1)]
            acc_s[rows] = alphas[h] * acc_s[rows] + _dot(ps[h], v)

    def far_pair(i, carry):
        update(*scores(2 * i * tq, 2 * tq))
        return carry

    n_far = jnp.maximum(n - 1, 0)
    lax.fori_loop(0, n_far // 2, far_pair, 0)

    @pl.when(n_far % 2 == 1)
    def _():
        update(*scores((n_far - 1) * tq))

    def near_diagonal(k0, before):
        k0 = pl.multiple_of(k0, tq)
        half = tq // MLA_DIAG_PARTS
        parts = []
        for part in range(MLA_DIAG_PARTS):
            kw = before + half * (part + 1)
            r = slice(part * half, (part + 1) * half)
            parts.append((kw, [_dot_nt(q_ref[0, r, LANES * h:LANES * (h + 1)],
                                       k_ref[0, pl.ds(k0, kw), LANES * h:LANES * (h + 1)]) for h in range(H)]))
        for part, (kw, ss) in enumerate(parts):
            row = lax.broadcasted_iota(jnp.int32, (half, kw), 0)
            col = lax.broadcasted_iota(jnp.int32, (half, kw), 1)
            causal = col <= row + (before + part * half)
            ps, alphas = [], []
            for h in range(H):
                rows = slice(h * tq + part * half, h * tq + (part + 1) * half)
                p, alpha = _softmax_tile(jnp.where(causal, ss[h], NEG), rows, m_s)
                ps.append(p)
                alphas.append(alpha)
            for h in range(H):
                rows = slice(h * tq + part * half, h * tq + (part + 1) * half)
                v = v_ref[0, pl.ds(k0, kw), LANES * h:LANES * (h + 1)]
                acc_s[rows] = alphas[h] * acc_s[rows] + _dot(ps[h], v)

    @pl.when(n >= 1)
    def _():
        near_diagonal(jnp.maximum(n - 1, 0) * tq, tq)

    @pl.when(n == 0)
    def _():
        near_diagonal(0, 0)

    o_ref[0] = _merge_normalized([acc_s[h * tq:(h + 1) * tq] for h in range(H)]).astype(o_ref.dtype)


def _mla_attention(qm, km, vm, T):
    B = qm.shape[0]
    tq = FLASH_TILE
    H = MLA_HEADS
    return pl.pallas_call(
        _mla_kernel,
        grid=(B, T // tq),
        in_specs=[pl.BlockSpec((1, tq, 512), lambda b, n: (b, n, 0)),
                  pl.BlockSpec((1, T, 512), lambda b, n: (b, 0, 0)),
                  pl.BlockSpec((1, T, 512), lambda b, n: (b, 0, 0))],
        out_specs=pl.BlockSpec((1, tq, 256), lambda b, n: (b, n, 0)),
        out_shape=jax.ShapeDtypeStruct((B, T, 256), BF16),
        scratch_shapes=[pltpu.VMEM((H * tq, LANES), F32), pltpu.VMEM((H * tq, LANES), F32)],
        compiler_params=_cparams(("arbitrary", "arbitrary")),
        name="mla_attention",
    )(qm, km, vm)


def _out_kernel(x_ref, gpre_ref, wz_ref, ocmp_ref, oslc_ref, owin_ref, krb_ref, gexp_ref, ob_ref, oc_ref, w_ref,
                gpost_ref, o_ref):
    sub = x_ref.shape[0] // OUT_SUBTILES
    rs = [slice(t * sub, (t + 1) * sub) for t in range(OUT_SUBTILES)]
    w_a = NSA_HEADS * HEAD_DIM
    hs = [_rms(x_ref[r, :], gpre_ref[...]).astype(BF16) for r in rs]
    zs = [_dot(h, wz_ref[...]) for h in hs]
    mixeds = []
    for r, z in zip(rs, zs):
        sig = 1.0 / (1.0 + jnp.exp(-krb_ref[r, :]))
        hi = sig.astype(BF16)
        lo = (sig - hi.astype(F32)).astype(BF16)
        gates = _dot(jnp.concatenate([hi, lo], axis=1), gexp_ref[...])
        oa = None
        for j, br in enumerate((ocmp_ref, oslc_ref, owin_ref)):
            term = gates[:, w_a * j:w_a * (j + 1)] * br[r, :].astype(F32)
            oa = term if oa is None else oa + term
        heads = jnp.concatenate([oa, ob_ref[r, :].astype(F32), oc_ref[r, :].astype(F32)], axis=1)
        mixeds.append((heads * (z * (1.0 / (1.0 + jnp.exp(-z))))).astype(BF16))
    ys = [_dot(m, w_ref[...]) for m in mixeds]
    for r, y in zip(rs, ys):
        o_ref[r, :] = x_ref[r, :] + _rms(y, gpost_ref[...])


def _out_project(x2, gpre, wz, ocmp, oslc, owin, krb, gexp, ob, oc, w, gpost, tm):
    BT = x2.shape[0]
    row = lambda i: (i, 0)
    fix = lambda i: (0, 0)
    spec = lambda a: pl.BlockSpec((tm, a.shape[1]), row)
    full = lambda a: pl.BlockSpec(a.shape, fix)
    return pl.pallas_call(
        _out_kernel,
        grid=(BT // tm,),
        in_specs=[spec(x2), full(gpre), full(wz), spec(ocmp), spec(oslc), spec(owin), spec(krb), full(gexp),
                  spec(ob), spec(oc), full(w), full(gpost)],
        out_specs=pl.BlockSpec((tm, D_MODEL), row),
        out_shape=jax.ShapeDtypeStruct((BT, D_MODEL), F32),
        compiler_params=_cparams(("arbitrary",)),
        name="out_proj",
    )(x2, gpre, wz, ocmp, oslc, owin, krb, gexp, ob, oc, w, gpost)


def _split_in(w):
    outs, o = [], 0
    for s in IN_SIZES:
        outs.append(w[..., o:o + s])
        o += s
    return outs


def _relayout_kernel(w_ref, wp_ref, wz_ref):
    a_q, a_kv, a_g, b_cq, b_ckv, b_kr, c_q, c_kv, z = _split_in(w_ref[0])
    zeros = lambda n: jnp.zeros((a_q.shape[0], n), F32)
    krb = jnp.concatenate([zeros(HEAD_DIM), b_kr, a_g, zeros(LANES - HEAD_DIM - MLA_ROPE - 3 * NSA_HEADS)], axis=-1)
    wp_ref[0] = jnp.concatenate([a_q, a_kv, b_cq, b_ckv, krb, c_q, c_kv], axis=-1).astype(BF16)
    wz_ref[0] = z.astype(BF16)


def _prep_w_in(w):
    L, d, n = w.shape
    rt = 256
    return pl.pallas_call(
        _relayout_kernel,
        grid=(L, d // rt),
        in_specs=[pl.BlockSpec((1, rt, n), lambda l, r: (l, r, 0))],
        out_specs=[pl.BlockSpec((1, rt, C_Z), lambda l, r: (l, r, 0)),
                   pl.BlockSpec((1, rt, IN_SIZES[-1]), lambda l, r: (l, r, 0))],
        out_shape=[jax.ShapeDtypeStruct((L, d, C_Z), BF16), jax.ShapeDtypeStruct((L, d, IN_SIZES[-1]), BF16)],
        compiler_params=_cparams(("arbitrary", "arbitrary")),
        name="w_in_relayout",
    )(w)


def _prep_mla(w_uq, w_ukv):
    L = w_uq.shape[0]
    dq = MLA_NOPE + MLA_ROPE
    uq = w_uq.reshape(L, MLA_Q_RANK, MLA_HEADS, dq)
    uq = jnp.concatenate([uq, jnp.zeros((L, MLA_Q_RANK, MLA_HEADS, LANES - dq), uq.dtype)], axis=-1)
    ukv = w_ukv.reshape(L, MLA_KV_RANK, MLA_HEADS, MLA_NOPE + MLA_V)
    uk = jnp.concatenate([ukv[..., :MLA_NOPE], jnp.zeros((L, MLA_KV_RANK, MLA_HEADS, LANES - MLA_NOPE), ukv.dtype)], axis=-1)
    uv = jnp.concatenate([jnp.zeros((L, MLA_KV_RANK, MLA_HEADS, LANES - MLA_V), ukv.dtype), ukv[..., MLA_NOPE:]], axis=-1)
    return (uq.reshape(L, MLA_Q_RANK, MLA_HEADS * LANES).astype(BF16),
            uk.reshape(L, MLA_KV_RANK, MLA_HEADS * LANES).astype(BF16),
            uv.reshape(L, MLA_KV_RANK, MLA_HEADS * LANES).astype(BF16))


def _prep_compress(cmp_pos, cmp_w1, cmp_w2):
    L = cmp_pos.shape[0]
    pe = jnp.concatenate([cmp_pos[:, 0], cmp_pos[:, 1]], axis=-1)
    pet = pe[:, :CMP_STRIDE].reshape(L, 1, CMP_STRIDE * LANES)
    peb = pe[:, CMP_STRIDE:].reshape(L, 1, CMP_STRIDE * LANES)
    w1 = cmp_w1.reshape(L, 2, CMP_LEN, HEAD_DIM, CMP_HIDDEN)
    zero = jnp.zeros_like(w1[:, 0])
    w1k = jnp.concatenate([w1[:, 0], zero], axis=2)
    w1v = jnp.concatenate([zero, w1[:, 1]], axis=2)
    w1e = jnp.stack([w1k, w1v], axis=1)
    w1t = w1e[:, :, :CMP_STRIDE].reshape(L, 2, CMP_STRIDE * LANES, CMP_HIDDEN).astype(BF16)
    w1b = w1e[:, :, CMP_STRIDE:].reshape(L, 2, CMP_STRIDE * LANES, CMP_HIDDEN).astype(BF16)
    z2 = jnp.zeros_like(cmp_w2[:, 0])
    w2 = jnp.stack([jnp.concatenate([cmp_w2[:, 0], z2], axis=-1),
                    jnp.concatenate([z2, cmp_w2[:, 1]], axis=-1)], axis=1).astype(BF16)
    return pet, peb, w1t, w1b, w2


def _rope_tables(T):
    half = MLA_ROPE // 2
    inv = ROPE_THETA ** (-jnp.arange(half, dtype=F32) / half)
    ang = jnp.arange(T).astype(F32)[:, None] * inv[None, :]
    cos, sin = jnp.cos(ang), jnp.sin(ang)
    z = lambda n: jnp.zeros((T, n), F32)
    rc = jnp.concatenate([jnp.ones((T, MLA_NOPE), F32), cos, cos, z(LANES - MLA_NOPE - MLA_ROPE)], axis=1)
    rsa = jnp.concatenate([z(MLA_NOPE + half), sin, z(LANES - MLA_NOPE - MLA_ROPE)], axis=1)
    rsb = jnp.concatenate([z(MLA_NOPE), -sin, z(LANES - MLA_NOPE - half)], axis=1)
    return rc, rsa, rsb


def _static_tables(T):
    ns = T // SEL_BLOCK
    nc = T // CMP_STRIDE - 1
    ncp = T // CMP_STRIDE
    sstart = np.arange(ns) * SEL_BLOCK
    cstart = np.arange(nc) * CMP_STRIDE
    overlap = (np.clip(np.minimum(cstart[:, None] + CMP_LEN, sstart[None, :] + SEL_BLOCK)
                       - np.maximum(cstart[:, None], sstart[None, :]), 0, None) / CMP_STRIDE).astype(np.float32)
    ovt = np.zeros((ns, ncp), np.float32)
    ovt[:, :nc] = overlap.T
    eaug = np.zeros((T, LANES), np.float32)
    eaug[np.arange(T), HEAD_DIM + np.arange(T) // SEL_BLOCK] = 1.0
    gexp = np.zeros((2, LANES, 3 * NSA_HEADS * HEAD_DIM), np.float32)
    for h in range(NSA_HEADS):
        for j in range(3):
            c0 = NSA_HEADS * HEAD_DIM * j + HEAD_DIM * h
            gexp[:, GATE_LANE0 + 3 * h + j, c0:c0 + HEAD_DIM] = 1.0
    gexp = gexp.reshape(2 * LANES, -1)
    return jnp.asarray(ovt, BF16), jnp.asarray(eaug, BF16), jnp.asarray(gexp, BF16)


def kernel(x, w_in, w_out, norm_pre, norm_post, cmp_pos, cmp_w1, cmp_w2, mla_q_norm, mla_w_uq, mla_kv_norm,
           mla_w_ukv, swa_sinks, rel_bias):
    B, T, D = x.shape
    depth = w_in.shape[0]
    assert D == D_MODEL and T % (2 * FLASH_TILE) == 0 and T // SEL_BLOCK <= HEAD_DIM
    ncp = T // CMP_STRIDE

    w_in_p, w_z = _prep_w_in(w_in)
    wuq, wk, wv = _prep_mla(mla_w_uq, mla_w_ukv)
    pet, peb, w1t, w1b, w2 = _prep_compress(cmp_pos, cmp_w1, cmp_w2)
    w_out_b = w_out.astype(BF16)
    rope_c, rope_sa, rope_sb = _rope_tables(T)
    ovt, eaug, gexp = _static_tables(T)

    bc = _bias_table(rel_bias, 0, NSA_HEADS, ncp, T, CMP_STRIDE, 1, -(CMP_LEN - 1), 0, 1 << 30, 32)
    bc = jnp.transpose(bc.reshape(NSA_HEADS, ncp, T), (0, 2, 1))
    tab_near = _bias_table(rel_bias, 0, NSA_HEADS, FLASH_TILE, 2 * FLASH_TILE, 1, -1, FLASH_TILE, 0, 1 << 30, 256,
                           mult=LOG2E).reshape(NSA_HEADS, FLASH_TILE, 2 * FLASH_TILE)
    tab_win = _bias_table(rel_bias, 0, NSA_HEADS, WIN_TILE, NSA_WINDOW + WIN_TILE, 1, -1, 0, 0, NSA_WINDOW, WIN_TILE,
                          nblk=NSA_WINDOW // WIN_TILE + 1, blk_d0=WIN_TILE)
    tab_swa = _bias_table(rel_bias, NSA_HEADS, SWA_HEADS, Q_TILE, SWA_WINDOW + Q_TILE, 1, -1, 0, 0, SWA_WINDOW, Q_TILE,
                          nblk=SWA_WINDOW // Q_TILE + 1, blk_d0=Q_TILE)
    b31 = rel_bias[:NSA_HEADS, REL_BUCKETS - 1] * LOG2E

    x2 = x.reshape(B * T, D)
    for l in range(depth):
        (qa, qa2, cmpkv, slck, slcv, win, qm, km, vm, krb, cq, ckv) = _project(
            x2, norm_pre[l][None], w_in_p[l], mla_q_norm[l][None], wuq[l], mla_kv_norm[l][None], wk[l], wv[l],
            rope_c, rope_sa, rope_sb, eaug, T, PROJ_ROWS)
        r3 = lambda a: a.reshape(B, T, a.shape[-1])
        kvc = _compress(r3(cmpkv), pet[l], peb[l], w1t[l], w1b[l], w2[l])
        qa3 = r3(qa)
        ocmp, selb = _cmp_attention(qa3, kvc, bc, ovt, T)
        owin = _band_attention(qa3, r3(win), tab_win, None, NSA_HEADS, NSA_WINDOW, T, BF16)
        oslc = _slc_attention(b31, r3(qa2), selb, r3(slck), r3(slcv), tab_near, T)
        oc = _band_attention(r3(cq), r3(ckv), tab_swa, swa_sinks[l], SWA_HEADS, SWA_WINDOW, T, BF16)
        ob = _mla_attention(r3(qm), r3(km), r3(vm), T)
        flat = lambda a: a.reshape(B * T, a.shape[-1])
        x2 = _out_project(x2, norm_pre[l][None], w_z[l], flat(ocmp), flat(oslc), flat(owin), krb, gexp,
                          flat(ob), flat(oc), w_out_b[l], norm_post[l][None], OUT_ROWS)
    return x2.reshape(B, T, D)
```

```python
import functools
import math

import numpy as np
import jax
import jax.numpy as jnp
from jax import lax
from jax.experimental import pallas as pl
from jax.experimental.pallas import tpu as pltpu

F32 = jnp.float32
BF16 = jnp.bfloat16

D_MODEL = 1024
HEAD_DIM = 64
NSA_HEADS = 4
CMP_LEN = 32
CMP_STRIDE = 16
CMP_HIDDEN = 128
SEL_BLOCK = 64
SEL_TOP = 16
NSA_WINDOW = 512
MLA_HEADS = 4
MLA_Q_RANK = 256
MLA_KV_RANK = 128
MLA_NOPE = 64
MLA_ROPE = 32
MLA_V = 64
ROPE_THETA = 10000.0
SWA_HEADS = 8
SWA_WINDOW = 128
REL_BUCKETS = 32
REL_MAX_DIST = 512
NORM_EPS = 1e-6
NEG = -1e30
BIG = 1e9
LOG2E = math.log2(math.e)
MLA_SCALE2 = (MLA_NOPE + MLA_ROPE) ** -0.5 * LOG2E
IN_SIZES = (256, 384, 12, 256, 128, 32, 512, 128, 1024)

LANES = 128
Q_TILE = 128
WIN_TILE = 128
BAND_ROWS = 4096
PROJ_ROWS = 1024
OUT_SUBTILES = 4
OUT_ROWS = 1024
CMP_GROUP = 4
FLASH_TILE = 512
SLC_DIAG_PARTS = 4
MLA_DIAG_PARTS = 4
GATE_LANE0 = 96
VMEM_LIMIT = 56 * 1024 * 1024

C_AQ, C_CMP, C_SLC, C_WIN, C_BCQ, C_BCKV, C_KRB, C_CQ, C_CKV, C_Z = (
    0, 256, 384, 512, 640, 896, 1024, 1152, 1664, 1792)


def _dot(a, b):
    return jnp.dot(a, b, preferred_element_type=F32)


def _dot_nt(a, b):
    return lax.dot_general(a, b, (((1,), (1,)), ((), ())), preferred_element_type=F32)


def _bucket_thresholds():
    d = np.arange(0, 4 * REL_MAX_DIST)
    exact = REL_BUCKETS // 2
    large = exact + (np.log(np.maximum(d, 1).astype(np.float32) / np.float32(exact))
                     / np.float32(math.log(REL_MAX_DIST / exact)) * np.float32(REL_BUCKETS - exact)).astype(np.int32)
    b = np.where(d < exact, d, np.minimum(large, REL_BUCKETS - 1))
    assert np.all(np.diff(b) >= 0) and b[-1] == REL_BUCKETS - 1
    return [int(np.argmax(b >= k)) for k in range(REL_BUCKETS)]


_THRESH = _bucket_thresholds()


def _cparams(sem):
    return pltpu.CompilerParams(dimension_semantics=sem, vmem_limit_bytes=VMEM_LIMIT)


def _table_kernel(h0, row_step, sign, d0, blk_d0, lo, hi, mult, rb_ref, out_ref):
    h = pl.program_id(0) + h0
    blk = pl.program_id(1)
    rt, cc = out_ref.shape[2], out_ref.shape[3]
    base = d0 + blk * blk_d0 - sign * row_step * rt * pl.program_id(2)
    width = -(-(row_step * rt + cc) // LANES) * LANES
    w = lax.broadcasted_iota(jnp.int32, (8, width), 1)
    dist = base + sign * jnp.where(w < cc, w, w - width)
    acc = jnp.full(dist.shape, rb_ref[h, 0], F32)
    for k in range(1, REL_BUCKETS):
        acc = jnp.where(dist >= _THRESH[k], rb_ref[h, k], acc)
    row = jnp.where((dist >= lo) & (dist < hi), acc * mult, NEG)
    rolled = pltpu.roll(jnp.broadcast_to(row[:1], (rt, width)), 0, 1, stride=row_step, stride_axis=0)
    out_ref[0, 0] = rolled[:, :cc]


def _bias_table(rel_bias, h0, nh, rows, cols, row_step, sign, d0, lo, hi, rt, mult=1.0, nblk=1, blk_d0=0):
    return pl.pallas_call(
        functools.partial(_table_kernel, h0, row_step, sign, d0, blk_d0, lo, hi, mult),
        grid=(nh, nblk, rows // rt),
        in_specs=[pl.BlockSpec(memory_space=pltpu.SMEM)],
        out_specs=pl.BlockSpec((1, 1, rt, cols), lambda h, b, r: (h, b, r, 0)),
        out_shape=jax.ShapeDtypeStruct((nh, nblk, rows, cols), F32),
        compiler_params=_cparams(("arbitrary", "arbitrary", "arbitrary")),
        name="bias_table",
    )(rel_bias)


def _rms(v, g):
    return v * lax.rsqrt(jnp.mean(v * v, axis=-1, keepdims=True) + NORM_EPS) * g


def _proj_kernel(x_ref, gpre_ref, w_ref, qn_ref, wuq_ref, kvn_ref, wk_ref, wv_ref,
                 rc_ref, rsa_ref, rsb_ref, eaug_ref,
                 qa_ref, qa2_ref, cmp_ref, slck_ref, slcv_ref, win_ref, qm_ref, km_ref, vm_ref,
                 krb_ref, cq_ref, ckv_ref):
    hb = _rms(x_ref[...], gpre_ref[...]).astype(BF16)
    bounds = ((C_AQ, C_CMP), (C_CMP, C_WIN), (C_WIN, C_KRB), (C_KRB, C_Z))
    chunks = {ab: _dot(hb, w_ref[:, ab[0]:ab[1]]) for ab in bounds}
    lane = lax.broadcasted_iota(jnp.int32, (hb.shape[0], LANES), 1)
    lo = lane < HEAD_DIM

    def proj(c0, c1):
        for (a, b), y in chunks.items():
            if a <= c0 and c1 <= b:
                return y[:, c0 - a:c1 - a]
        raise AssertionError((c0, c1))

    qa_ref[...] = (proj(C_AQ, C_CMP) * 0.125).astype(BF16)
    qa2_ref[...] = (proj(C_AQ, C_CMP) * (0.125 * LOG2E)).astype(BF16)
    cmp_ref[...] = proj(C_CMP, C_SLC)
    slc = proj(C_SLC, C_WIN)
    slcv_ref[...] = jnp.where(lo, 1.0, slc).astype(BF16)
    slck_ref[...] = jnp.where(lo, slc, eaug_ref[...].astype(F32)).astype(BF16)
    win_ref[...] = proj(C_WIN, C_BCQ).astype(BF16)
    cq_ref[...] = (proj(C_CQ, C_CKV) * 0.125).astype(BF16)
    ckv_ref[...] = proj(C_CKV, C_Z).astype(BF16)

    rc, rsa, rsb = rc_ref[...], rsa_ref[...], rsb_ref[...]

    def rope(v):
        return v * rc + pltpu.roll(v, 16, 1) * rsa + pltpu.roll(v, 112, 1) * rsb

    cqn = _rms(proj(C_BCQ, C_BCKV), qn_ref[...]).astype(BF16)
    qm = _dot(cqn, wuq_ref[...])
    for h in range(MLA_HEADS):
        qh = rope(qm[:, LANES * h:LANES * (h + 1)]) * MLA_SCALE2
        qm_ref[:, LANES * h:LANES * (h + 1)] = qh.astype(BF16)

    krb = proj(C_KRB, C_CQ)
    krb_ref[...] = krb
    krr = jnp.where(lo | (lane >= HEAD_DIM + MLA_ROPE), 0.0, rope(krb))

    ckvn = _rms(proj(C_BCKV, C_KRB), kvn_ref[...]).astype(BF16)
    kn = _dot(ckvn, wk_ref[...])
    for h in range(MLA_HEADS):
        km_ref[:, LANES * h:LANES * (h + 1)] = (kn[:, LANES * h:LANES * (h + 1)] + krr).astype(BF16)
    vm = _dot(ckvn, wv_ref[...])
    for h in range(MLA_HEADS):
        vm_ref[:, LANES * h:LANES * (h + 1)] = jnp.where(lo, 1.0, vm[:, LANES * h:LANES * (h + 1)]).astype(BF16)


def _project(x2, gpre, w, qn, wuq, kvn, wk, wv, rope_c, rope_sa, rope_sb, eaug, T, tm):
    BT = x2.shape[0]
    nt = T // tm
    row = lambda i: (i, 0)
    fix = lambda i: (0, 0)
    pos = lambda i: (i % nt, 0)
    widths = [(256, BF16), (256, BF16), (128, F32), (128, BF16), (128, BF16), (128, BF16), (512, BF16), (512, BF16),
              (512, BF16), (128, F32), (512, BF16), (128, BF16)]
    return pl.pallas_call(
        _proj_kernel,
        grid=(BT // tm,),
        in_specs=[pl.BlockSpec((tm, D_MODEL), row), pl.BlockSpec((1, D_MODEL), fix),
                  pl.BlockSpec(w.shape, fix), pl.BlockSpec(qn.shape, fix), pl.BlockSpec(wuq.shape, fix),
                  pl.BlockSpec(kvn.shape, fix), pl.BlockSpec(wk.shape, fix), pl.BlockSpec(wv.shape, fix),
                  pl.BlockSpec((tm, LANES), pos), pl.BlockSpec((tm, LANES), pos), pl.BlockSpec((tm, LANES), pos),
                  pl.BlockSpec((tm, LANES), pos)],
        out_specs=[pl.BlockSpec((tm, wd), row) for wd, _ in widths],
        out_shape=[jax.ShapeDtypeStruct((BT, wd), dt) for wd, dt in widths],
        compiler_params=_cparams(("arbitrary",)),
        name="proj",
    )(x2, gpre, w, qn, wuq, kvn, wk, wv, rope_c, rope_sa, rope_sb, eaug)


def _compress_kernel(c_ref, pet_ref, peb_ref, w1t_ref, w1b_ref, w2_ref, o_ref):
    nch = c_ref.shape[1] // CMP_STRIDE
    c = jnp.concatenate([c_ref[0, pl.ds(j, nch, stride=CMP_STRIDE), :] for j in range(CMP_STRIDE)], axis=1)
    top = (c + pet_ref[...]).astype(BF16)
    bot = (c + peb_ref[...]).astype(BF16)
    out = jnp.zeros((nch, LANES), F32)
    for i in range(2):
        pre = _dot(top, w1t_ref[i]) + pltpu.roll(_dot(bot, w1b_ref[i]), nch - 1, 0)
        hid = pre * (1.0 / (1.0 + jnp.exp(-pre)))
        out = out + _dot(hid.astype(BF16), w2_ref[i])
    o_ref[0] = out.astype(BF16)


def _compress(tokens, pet, peb, w1t, w1b, w2):
    B, T, width = tokens.shape
    nch = T // CMP_STRIDE
    full = lambda a: pl.BlockSpec(a.shape, lambda b: (0,) * a.ndim)
    return pl.pallas_call(
        _compress_kernel,
        grid=(B,),
        in_specs=[pl.BlockSpec((1, T, width), lambda b: (b, 0, 0)),
                  full(pet), full(peb), full(w1t), full(w1b), full(w2)],
        out_specs=pl.BlockSpec((1, nch, LANES), lambda b: (b, 0, 0)),
        out_shape=jax.ShapeDtypeStruct((B, nch, LANES), BF16),
        compiler_params=_cparams(("arbitrary",)),
        name="compress",
    )(tokens, pet, peb, w1t, w1b, w2)


def _head_slabs(qblk, n_heads):
    q = qblk.astype(F32)
    lo = lax.broadcasted_iota(jnp.int32, (q.shape[0], LANES), 1) < HEAD_DIM
    out = []
    for j in range(n_heads // 2):
        slab = q[:, LANES * j:LANES * (j + 1)]
        out.append(jnp.where(lo, slab, 0.0))
        out.append(jnp.where(lo, pltpu.roll(slab, HEAD_DIM, 1), 0.0))
    return out


def _merge_upper(accs):
    lo = lax.broadcasted_iota(jnp.int32, accs[0].shape, 1) < HEAD_DIM
    slabs = [jnp.where(lo, pltpu.roll(accs[2 * j], HEAD_DIM, 1), accs[2 * j + 1]) for j in range(len(accs) // 2)]
    return jnp.concatenate(slabs, axis=1)


def _merge_normalized(accs):
    lo = lax.broadcasted_iota(jnp.int32, accs[0].shape, 1) < HEAD_DIM
    slabs = []
    for j in range(len(accs) // 2):
        a, b = accs[2 * j], accs[2 * j + 1]
        out = jnp.where(lo, pltpu.roll(a, HEAD_DIM, 1), b)
        den = jnp.where(lo, a, pltpu.roll(b, HEAD_DIM, 1))
        slabs.append(out / den)
    return jnp.concatenate(slabs, axis=1)


def _row_max_lanes(s):
    return jnp.broadcast_to(jnp.max(s, axis=-1, keepdims=True), (s.shape[0], LANES))


def _sub_lanes(s, ref):
    return jnp.concatenate([s[:, LANES * i:LANES * (i + 1)] - ref for i in range(s.shape[1] // LANES)], axis=1)


def _topk_mask_t(imp, t0):
    ns, tq = imp.shape
    srow = lax.broadcasted_iota(jnp.int32, (ns, tq), 0)
    cur = (t0 + lax.broadcasted_iota(jnp.int32, (ns, tq), 1)) // SEL_BLOCK
    forced = (srow == 0) | (srow == cur) | (srow == cur - 1)
    x = jnp.where(forced, BIG, jnp.where(srow <= cur, imp, -BIG))
    sub = 8
    groups = [x[sub * v:sub * (v + 1)] for v in range(ns // sub)]
    rows_in = [srow[sub * v:sub * (v + 1)] for v in range(ns // sub)]
    cnts = [jnp.zeros((sub, tq), F32) for _ in groups]
    for sp in range(ns):
        other = jnp.broadcast_to(x[sp:sp + 1, :], (sub, tq))
        for v, xv in enumerate(groups):
            if sub * v > sp:
                beats = other >= xv
            elif sub * v + sub - 1 < sp:
                beats = other > xv
            else:
                beats = (other > xv) | ((other == xv) & (rows_in[v] > sp))
            cnts[v] = cnts[v] + jnp.where(beats, 1.0, 0.0)
    cnt = jnp.concatenate(cnts, axis=0)
    sel = (cnt < float(min(SEL_TOP, ns))) & (srow <= cur)
    return jnp.where(sel, 0.0, NEG)


def _cmp_kernel(q_ref, kvc_ref, bc_ref, ovt_ref, ocmp_ref, selb_ref):
    rows = CMP_GROUP * Q_TILE
    ncp, ns = kvc_ref.shape[1], ovt_ref.shape[0]
    step = pl.program_id(1)
    lo_step, nse = 0, SEL_TOP
    while lo_step * rows < ns * SEL_BLOCK:
        nse = min(nse, ns)
        hi_step = nse * SEL_BLOCK // rows
        ncw = min(ncp, -(-nse * (SEL_BLOCK // CMP_STRIDE) // LANES) * LANES)
        if hi_step > lo_step:
            pl.when((step >= lo_step) & (step < hi_step))(
                functools.partial(_cmp_body, ncw, nse, q_ref, kvc_ref, bc_ref, ovt_ref, ocmp_ref, selb_ref))
            lo_step = hi_step
        nse += SEL_TOP


def _cmp_body(ncw, ns, q_ref, kvc_ref, bc_ref, ovt_ref, ocmp_ref, selb_ref):
    tq = Q_TILE
    kvc = kvc_ref[0, :ncw, :]
    ovt = ovt_ref[:ns, :ncw]
    groups = range(CMP_GROUP)
    sls = [slice(g * tq, (g + 1) * tq) for g in groups]
    s_alls = []
    for g in groups:
        qs = _head_slabs(q_ref[0, sls[g], :], NSA_HEADS)
        s_alls.append(_dot_nt(jnp.concatenate([q.astype(BF16) for q in qs], axis=0), kvc))
    psums, o_alls = [], []
    for g in groups:
        psum = None
        ps = []
        for h in range(NSA_HEADS):
            s = s_alls[g][h * tq:(h + 1) * tq] + bc_ref[h, sls[g], :ncw]
            m = jnp.max(s, axis=-1, keepdims=True)
            e = jnp.exp(s - m)
            l = jnp.sum(e, axis=-1, keepdims=True)
            p = e * jnp.where(m > 0.5 * NEG, 1.0 / l, 0.0)
            psum = p if psum is None else psum + p
            ps.append(p.astype(BF16))
        psums.append(psum)
        o_alls.append(_dot(jnp.concatenate(ps, axis=0), kvc))
    imps = []
    for g in groups:
        hi = psums[g].astype(BF16)
        lo = (psums[g] - hi.astype(F32)).astype(BF16)
        imps.append(_dot_nt(ovt, hi) + _dot_nt(ovt, lo))
    for g in groups:
        ocmp = _merge_upper([o_alls[g][h * tq:(h + 1) * tq] for h in range(NSA_HEADS)])
        ocmp_ref[0, sls[g], :] = ocmp.astype(ocmp_ref.dtype)
        sb = _topk_mask_t(imps[g], (pl.program_id(1) * CMP_GROUP + g) * tq)
        if ns < HEAD_DIM:
            sb = jnp.concatenate([sb, jnp.full((HEAD_DIM - ns, tq), NEG, F32)], axis=0)
        full = jnp.concatenate([jnp.zeros((HEAD_DIM, tq), F32), sb], axis=0)
        selb_ref[0, sls[g], :] = full.T.astype(BF16)


def _cmp_attention(qa, kvc, bc, ovt, T):
    B = qa.shape[0]
    ncp = kvc.shape[1]
    rows = CMP_GROUP * Q_TILE
    return pl.pallas_call(
        _cmp_kernel,
        grid=(B, T // rows),
        in_specs=[pl.BlockSpec((1, rows, 256), lambda b, n: (b, n, 0)),
                  pl.BlockSpec((1, ncp, LANES), lambda b, n: (b, 0, 0)),
                  pl.BlockSpec((NSA_HEADS, rows, ncp), lambda b, n: (0, n, 0)),
                  pl.BlockSpec(ovt.shape, lambda b, n: (0, 0))],
        out_specs=[pl.BlockSpec((1, rows, 256), lambda b, n: (b, n, 0)),
                   pl.BlockSpec((1, rows, LANES), lambda b, n: (b, n, 0))],
        out_shape=[jax.ShapeDtypeStruct((B, T, 256), BF16), jax.ShapeDtypeStruct((B, T, LANES), BF16)],
        compiler_params=_cparams(("arbitrary", "arbitrary")),
        name="cmp_select",
    )(qa, kvc, bc, ovt)


def _softmax_tile(s2, rows, m_s, shift=None):
    m_t = _row_max_lanes(s2)
    if shift is not None:
        m_t = m_t + shift
    m_old = m_s[rows]
    m_new = jnp.maximum(m_old, m_t)
    p = jnp.exp2(_sub_lanes(s2, m_new if shift is None else m_new - shift))
    alpha = jnp.exp2(m_old - m_new)
    m_s[rows] = m_new
    return p.astype(BF16), alpha


def _slc_kernel(b31_ref, q_ref, selb_ref, ka_ref, kv_ref, tab_ref, o_ref, m_s, acc_s):
    tq = q_ref.shape[1]
    H = NSA_HEADS
    n = pl.program_id(1)
    qs = _head_slabs(q_ref[0], H)
    selb = selb_ref[0].astype(F32)
    qst = jnp.concatenate([(qs[h] + selb).astype(BF16) for h in range(H)], axis=0)
    m_s[...] = jnp.full(m_s.shape, -jnp.inf, F32)
    acc_s[...] = jnp.zeros(acc_s.shape, F32)

    def scores(k0, width=tq):
        k0 = pl.multiple_of(k0, tq)
        return _dot_nt(qst, ka_ref[0, pl.ds(k0, width), :]), kv_ref[0, pl.ds(k0, width), :]

    def update(s_all, kv):
        ps, alphas = [], []
        for h in range(H):
            rows = slice(h * tq, (h + 1) * tq)
            p, alpha = _softmax_tile(s_all[rows], rows, m_s, shift=b31_ref[h])
            ps.append(p)
            alphas.append(alpha)
        pv = _dot(jnp.concatenate(ps, axis=0), kv)
        for h in range(H):
            rows = slice(h * tq, (h + 1) * tq)
            acc_s[rows] = alphas[h] * acc_s[rows] + pv[rows]

    def far_triple(i, carry):
        update(*scores(3 * i * tq, 3 * tq))
        return carry

    n_far = jnp.maximum(n - 1, 0)
    lax.fori_loop(0, n_far // 3, far_triple, 0)

    @pl.when(n_far % 3 == 1)
    def _():
        update(*scores((n_far - 1) * tq))

    @pl.when(n_far % 3 == 2)
    def _():
        update(*scores((n_far - 2) * tq, 2 * tq))

    def near_diagonal(k0, before):
        k0 = pl.multiple_of(k0, tq)
        half = tq // SLC_DIAG_PARTS
        parts = []
        for part in range(SLC_DIAG_PARTS):
            kw = before + half * (part + 1)
            q_part = jnp.concatenate([qst[h * tq + part * half:h * tq + (part + 1) * half] for h in range(H)], axis=0)
            parts.append((_dot_nt(q_part, ka_ref[0, pl.ds(k0, kw), :]), kv_ref[0, pl.ds(k0, kw), :], kw))
        for part, (s_all, kv, kw) in enumerate(parts):
            ps, alphas = [], []
            for h in range(H):
                rows = slice(h * tq + part * half, h * tq + (part + 1) * half)
                bias = tab_ref[h, part * half:(part + 1) * half, tq - before:tq - before + kw]
                p, alpha = _softmax_tile(s_all[h * half:(h + 1) * half] + bias, rows, m_s)
                ps.append(p)
                alphas.append(alpha)
            pv = _dot(jnp.concatenate(ps, axis=0), kv)
            for h in range(H):
                rows = slice(h * tq + part * half, h * tq + (part + 1) * half)
                acc_s[rows] = alphas[h] * acc_s[rows] + pv[h * half:(h + 1) * half]

    @pl.when(n >= 1)
    def _():
        near_diagonal(jnp.maximum(n - 1, 0) * tq, tq)

    @pl.when(n == 0)
    def _():
        near_diagonal(0, 0)

    o_ref[0] = _merge_normalized([acc_s[h * tq:(h + 1) * tq] for h in range(H)]).astype(o_ref.dtype)


def _slc_attention(b31, qa, selb, slck, slcv, tab, T):
    B = qa.shape[0]
    tq = FLASH_TILE
    H = NSA_HEADS
    tile = lambda w: pl.BlockSpec((1, tq, w), lambda b, n: (b, n, 0))
    return pl.pallas_call(
        _slc_kernel,
        grid=(B, T // tq),
        in_specs=[pl.BlockSpec(memory_space=pltpu.SMEM),
                  tile(256), tile(LANES),
                  pl.BlockSpec((1, T, LANES), lambda b, n: (b, 0, 0)),
                  pl.BlockSpec((1, T, LANES), lambda b, n: (b, 0, 0)),
                  pl.BlockSpec(tab.shape, lambda b, n: (0, 0, 0))],
        out_specs=tile(256),
        out_shape=jax.ShapeDtypeStruct((B, T, 256), BF16),
        scratch_shapes=[pltpu.VMEM((H * tq, LANES), F32), pltpu.VMEM((H * tq, LANES), F32)],
        compiler_params=_cparams(("arbitrary", "arbitrary")),
        name="slc_attention",
    )(b31, qa, selb, slck, slcv, tab)


def _band_kernel(n_heads, window, has_sink, n_sub, tq, *refs):
    if has_sink:
        q_ref, kv_ref, tab_ref, sink_ref, o_ref = refs
    else:
        q_ref, kv_ref, tab_ref, o_ref = refs
    span = window + tq
    lo_kv = lax.broadcasted_iota(jnp.int32, (span, LANES), 1) < HEAD_DIM
    lo_o = lax.broadcasted_iota(jnp.int32, (n_heads * tq, LANES), 1) < HEAD_DIM
    groups = range(n_sub)
    tiles = [pl.program_id(1) * n_sub + g for g in groups]
    kvs, dots = [], []
    for g in groups:
        k0 = pl.multiple_of(jnp.maximum(tiles[g] * tq - window, 0), tq)
        kvs.append(kv_ref[0, pl.ds(k0, span), :])
        qs = _head_slabs(q_ref[0, g * tq:(g + 1) * tq, :], n_heads)
        dots.append(_dot_nt(jnp.concatenate([q.astype(BF16) for q in qs], axis=0), kvs[g]))
    accs, ms = [], []
    for g in groups:
        s = dots[g] + tab_ref[jnp.minimum(tiles[g], window // tq)]
        m = _row_max_lanes(s)
        if has_sink:
            m = jnp.maximum(m, sink_ref[...])
        e = jnp.exp(_sub_lanes(s, m))
        ms.append(m)
        accs.append(_dot(e.astype(BF16), jnp.where(lo_kv, 1.0, kvs[g]).astype(BF16)))
    for g in groups:
        acc = accs[g]
        if has_sink:
            acc = acc + jnp.where(lo_o, jnp.exp(sink_ref[...] - ms[g]), 0.0)
        out = _merge_normalized([acc[h * tq:(h + 1) * tq] for h in range(n_heads)])
        o_ref[0, g * tq:(g + 1) * tq, :] = out.astype(o_ref.dtype)


def _band_attention(q, kv_pad, tab, sinks, n_heads, window, T, out_dtype):
    B = q.shape[0]
    width = n_heads * HEAD_DIM
    has_sink = sinks is not None
    nvar, tq = tab.shape[1], tab.shape[2]
    tab2 = jnp.transpose(tab, (1, 0, 2, 3)).reshape(nvar, n_heads * tq, window + tq)
    n_sub = (BAND_ROWS if has_sink else 2 * BAND_ROWS) // (n_heads * tq)
    rows = n_sub * tq
    in_specs = [pl.BlockSpec((1, rows, width), lambda b, n: (b, n, 0)),
                pl.BlockSpec((1, T, LANES), lambda b, n: (b, 0, 0)),
                pl.BlockSpec(tab2.shape, lambda b, n: (0, 0, 0))]
    args = (q, kv_pad, tab2)
    if has_sink:
        rep = jnp.broadcast_to(jnp.repeat(sinks, tq)[:, None], (n_heads * tq, LANES))
        in_specs.append(pl.BlockSpec(rep.shape, lambda b, n: (0, 0)))
        args = args + (rep,)
    return pl.pallas_call(
        functools.partial(_band_kernel, n_heads, window, has_sink, n_sub, tq),
        grid=(B, T // rows),
        in_specs=in_specs,
        out_specs=pl.BlockSpec((1, rows, width), lambda b, n: (b, n, 0)),
        out_shape=jax.ShapeDtypeStruct((B, T, width), out_dtype),
        compiler_params=_cparams(("arbitrary", "arbitrary")),
        name="band_sink" if has_sink else "band_window",
    )(*args)


def _mla_kernel(q_ref, k_ref, v_ref, o_ref, m_s, acc_s):
    tq = q_ref.shape[1]
    H = MLA_HEADS
    n = pl.program_id(1)
    m_s[...] = jnp.full(m_s.shape, -jnp.inf, F32)
    acc_s[...] = jnp.zeros(acc_s.shape, F32)

    def scores(k0, width=tq):
        k0 = pl.multiple_of(k0, tq)
        return k0, width, [_dot_nt(q_ref[0, :, LANES * h:LANES * (h + 1)],
                                   k_ref[0, pl.ds(k0, width), LANES * h:LANES * (h + 1)]) for h in range(H)]

    def update(k0, width, ss):
        ps, alphas = [], []
        for h in range(H):
            p, alpha = _softmax_tile(ss[h], slice(h * tq, (h + 1) * tq), m_s)
            ps.append(p)
            alphas.append(alpha)
        for h in range(H):
            rows = slice(h * tq, (h + 1) * tq)
            v = v_ref[0, pl.ds(k0, width), LANES * h:LANES * (h + 1)]
            acc_s[rows] = alphas[h] * acc_s[rows] + _dot(ps[h], v)

    def far_triple(i, carry):
        update(*scores(3 * i * tq, 3 * tq))
        return carry

    n_far = jnp.maximum(n - 1, 0)
    lax.fori_loop(0, n_far // 3, far_triple, 0)

    @pl.when(n_far % 3 == 1)
    def _():
        update(*scores((n_far - 1) * tq))

    @pl.when(n_far % 3 == 2)
    def _():
        update(*scores((n_far - 2) * tq, 2 * tq))

    def near_diagonal(k0, before):
        k0 = pl.multiple_of(k0, tq)
        half = tq // MLA_DIAG_PARTS
        parts = []
        for part in range(MLA_DIAG_PARTS):
            kw = before + half * (part + 1)
            r = slice(part * half, (part + 1) * half)
            parts.append((kw, [_dot_nt(q_ref[0, r, LANES * h:LANES * (h + 1)],
                                       k_ref[0, pl.ds(k0, kw), LANES * h:LANES * (h + 1)]) for h in range(H)]))
        for part, (kw, ss) in enumerate(parts):
            row = lax.broadcasted_iota(jnp.int32, (half, kw), 0)
            col = lax.broadcasted_iota(jnp.int32, (half, kw), 1)
            causal = col <= row + (before + part * half)
            ps, alphas = [], []
            for h in range(H):
                rows = slice(h * tq + part * half, h * tq + (part + 1) * half)
                p, alpha = _softmax_tile(jnp.where(causal, ss[h], NEG), rows, m_s)
                ps.append(p)
                alphas.append(alpha)
            for h in range(H):
                rows = slice(h * tq + part * half, h * tq + (part + 1) * half)
                v = v_ref[0, pl.ds(k0, kw), LANES * h:LANES * (h + 1)]
                acc_s[rows] = alphas[h] * acc_s[rows] + _dot(ps[h], v)

    @pl.when(n >= 1)
    def _():
        near_diagonal(jnp.maximum(n - 1, 0) * tq, tq)

    @pl.when(n == 0)
    def _():
        near_diagonal(0, 0)

    o_ref[0] = _merge_normalized([acc_s[h * tq:(h + 1) * tq] for h in range(H)]).astype(o_ref.dtype)


def _mla_attention(qm, km, vm, T):
    B = qm.shape[0]
    tq = FLASH_TILE
    H = MLA_HEADS
    return pl.pallas_call(
        _mla_kernel,
        grid=(B, T // tq),
        in_specs=[pl.BlockSpec((1, tq, 512), lambda b, n: (b, n, 0)),
                  pl.BlockSpec((1, T, 512), lambda b, n: (b, 0, 0)),
                  pl.BlockSpec((1, T, 512), lambda b, n: (b, 0, 0))],
        out_specs=pl.BlockSpec((1, tq, 256), lambda b, n: (b, n, 0)),
        out_shape=jax.ShapeDtypeStruct((B, T, 256), BF16),
        scratch_shapes=[pltpu.VMEM((H * tq, LANES), F32), pltpu.VMEM((H * tq, LANES), F32)],
        compiler_params=_cparams(("arbitrary", "arbitrary")),
        name="mla_attention",
    )(qm, km, vm)


def _out_kernel(x_ref, gpre_ref, wz_ref, ocmp_ref, oslc_ref, owin_ref, krb_ref, gexp_ref, ob_ref, oc_ref, w_ref,
                gpost_ref, o_ref):
    sub = x_ref.shape[0] // OUT_SUBTILES
    rs = [slice(t * sub, (t + 1) * sub) for t in range(OUT_SUBTILES)]
    w_a = NSA_HEADS * HEAD_DIM
    hs = [_rms(x_ref[r, :], gpre_ref[...]).astype(BF16) for r in rs]
    zs = [_dot(h, wz_ref[...]) for h in hs]
    mixeds = []
    for r, z in zip(rs, zs):
        sig = 1.0 / (1.0 + jnp.exp(-krb_ref[r, :]))
        hi = sig.astype(BF16)
        lo = (sig - hi.astype(F32)).astype(BF16)
        gates = _dot(jnp.concatenate([hi, lo], axis=1), gexp_ref[...])
        oa = None
        for j, br in enumerate((ocmp_ref, oslc_ref, owin_ref)):
            term = gates[:, w_a * j:w_a * (j + 1)] * br[r, :].astype(F32)
            oa = term if oa is None else oa + term
        heads = jnp.concatenate([oa, ob_ref[r, :].astype(F32), oc_ref[r, :].astype(F32)], axis=1)
        mixeds.append((heads * (z * (1.0 / (1.0 + jnp.exp(-z))))).astype(BF16))
    ys = [_dot(m, w_ref[...]) for m in mixeds]
    for r, y in zip(rs, ys):
        o_ref[r, :] = x_ref[r, :] + _rms(y, gpost_ref[...])


def _out_project(x2, gpre, wz, ocmp, oslc, owin, krb, gexp, ob, oc, w, gpost, tm):
    BT = x2.shape[0]
    row = lambda i: (i, 0)
    fix = lambda i: (0, 0)
    spec = lambda a: pl.BlockSpec((tm, a.shape[1]), row)
    full = lambda a: pl.BlockSpec(a.shape, fix)
    return pl.pallas_call(
        _out_kernel,
        grid=(BT // tm,),
        in_specs=[spec(x2), full(gpre), full(wz), spec(ocmp), spec(oslc), spec(owin), spec(krb), full(gexp),
                  spec(ob), spec(oc), full(w), full(gpost)],
        out_specs=pl.BlockSpec((tm, D_MODEL), row),
        out_shape=jax.ShapeDtypeStruct((BT, D_MODEL), F32),
        compiler_params=_cparams(("arbitrary",)),
        name="out_proj",
    )(x2, gpre, wz, ocmp, oslc, owin, krb, gexp, ob, oc, w, gpost)


def _split_in(w):
    outs, o = [], 0
    for s in IN_SIZES:
        outs.append(w[..., o:o + s])
        o += s
    return outs


def _relayout_kernel(w_ref, wp_ref, wz_ref):
    a_q, a_kv, a_g, b_cq, b_ckv, b_kr, c_q, c_kv, z = _split_in(w_ref[0])
    zeros = lambda n: jnp.zeros((a_q.shape[0], n), F32)
    krb = jnp.concatenate([zeros(HEAD_DIM), b_kr, a_g, zeros(LANES - HEAD_DIM - MLA_ROPE - 3 * NSA_HEADS)], axis=-1)
    wp_ref[0] = jnp.concatenate([a_q, a_kv, b_cq, b_ckv, krb, c_q, c_kv], axis=-1).astype(BF16)
    wz_ref[0] = z.astype(BF16)


def _prep_w_in(w):
    L, d, n = w.shape
    rt = 256
    return pl.pallas_call(
        _relayout_kernel,
        grid=(L, d // rt),
        in_specs=[pl.BlockSpec((1, rt, n), lambda l, r: (l, r, 0))],
        out_specs=[pl.BlockSpec((1, rt, C_Z), lambda l, r: (l, r, 0)),
                   pl.BlockSpec((1, rt, IN_SIZES[-1]), lambda l, r: (l, r, 0))],
        out_shape=[jax.ShapeDtypeStruct((L, d, C_Z), BF16), jax.ShapeDtypeStruct((L, d, IN_SIZES[-1]), BF16)],
        compiler_params=_cparams(("arbitrary", "arbitrary")),
        name="w_in_relayout",
    )(w)


def _prep_mla(w_uq, w_ukv):
    L = w_uq.shape[0]
    dq = MLA_NOPE + MLA_ROPE
    uq = w_uq.reshape(L, MLA_Q_RANK, MLA_HEADS, dq)
    uq = jnp.concatenate([uq, jnp.zeros((L, MLA_Q_RANK, MLA_HEADS, LANES - dq), uq.dtype)], axis=-1)
    ukv = w_ukv.reshape(L, MLA_KV_RANK, MLA_HEADS, MLA_NOPE + MLA_V)
    uk = jnp.concatenate([ukv[..., :MLA_NOPE], jnp.zeros((L, MLA_KV_RANK, MLA_HEADS, LANES - MLA_NOPE), ukv.dtype)], axis=-1)
    uv = jnp.concatenate([jnp.zeros((L, MLA_KV_RANK, MLA_HEADS, LANES - MLA_V), ukv.dtype), ukv[..., MLA_NOPE:]], axis=-1)
    return (uq.reshape(L, MLA_Q_RANK, MLA_HEADS * LANES).astype(BF16),
            uk.reshape(L, MLA_KV_RANK, MLA_HEADS * LANES).astype(BF16),
            uv.reshape(L, MLA_KV_RANK, MLA_HEADS * LANES).astype(BF16))


def _prep_compress(cmp_pos, cmp_w1, cmp_w2):
    L = cmp_pos.shape[0]
    pe = jnp.concatenate([cmp_pos[:, 0], cmp_pos[:, 1]], axis=-1)
    pet = pe[:, :CMP_STRIDE].reshape(L, 1, CMP_STRIDE * LANES)
    peb = pe[:, CMP_STRIDE:].reshape(L, 1, CMP_STRIDE * LANES)
    w1 = cmp_w1.reshape(L, 2, CMP_LEN, HEAD_DIM, CMP_HIDDEN)
    zero = jnp.zeros_like(w1[:, 0])
    w1k = jnp.concatenate([w1[:, 0], zero], axis=2)
    w1v = jnp.concatenate([zero, w1[:, 1]], axis=2)
    w1e = jnp.stack([w1k, w1v], axis=1)
    w1t = w1e[:, :, :CMP_STRIDE].reshape(L, 2, CMP_STRIDE * LANES, CMP_HIDDEN).astype(BF16)
    w1b = w1e[:, :, CMP_STRIDE:].reshape(L, 2, CMP_STRIDE * LANES, CMP_HIDDEN).astype(BF16)
    z2 = jnp.zeros_like(cmp_w2[:, 0])
    w2 = jnp.stack([jnp.concatenate([cmp_w2[:, 0], z2], axis=-1),
                    jnp.concatenate([z2, cmp_w2[:, 1]], axis=-1)], axis=1).astype(BF16)
    return pet, peb, w1t, w1b, w2


def _rope_tables(T):
    half = MLA_ROPE // 2
    inv = ROPE_THETA ** (-jnp.arange(half, dtype=F32) / half)
    ang = jnp.arange(T).astype(F32)[:, None] * inv[None, :]
    cos, sin = jnp.cos(ang), jnp.sin(ang)
    z = lambda n: jnp.zeros((T, n), F32)
    rc = jnp.concatenate([jnp.ones((T, MLA_NOPE), F32), cos, cos, z(LANES - MLA_NOPE - MLA_ROPE)], axis=1)
    rsa = jnp.concatenate([z(MLA_NOPE + half), sin, z(LANES - MLA_NOPE - MLA_ROPE)], axis=1)
    rsb = jnp.concatenate([z(MLA_NOPE), -sin, z(LANES - MLA_NOPE - half)], axis=1)
    return rc, rsa, rsb


def _static_tables(T):
    ns = T // SEL_BLOCK
    nc = T // CMP_STRIDE - 1
    ncp = T // CMP_STRIDE
    sstart = np.arange(ns) * SEL_BLOCK
    cstart = np.arange(nc) * CMP_STRIDE
    overlap = (np.clip(np.minimum(cstart[:, None] + CMP_LEN, sstart[None, :] + SEL_BLOCK)
                       - np.maximum(cstart[:, None], sstart[None, :]), 0, None) / CMP_STRIDE).astype(np.float32)
    ovt = np.zeros((ns, ncp), np.float32)
    ovt[:, :nc] = overlap.T
    eaug = np.zeros((T, LANES), np.float32)
    eaug[np.arange(T), HEAD_DIM + np.arange(T) // SEL_BLOCK] = 1.0
    gexp = np.zeros((2, LANES, 3 * NSA_HEADS * HEAD_DIM), np.float32)
    for h in range(NSA_HEADS):
        for j in range(3):
            c0 = NSA_HEADS * HEAD_DIM * j + HEAD_DIM * h
            gexp[:, GATE_LANE0 + 3 * h + j, c0:c0 + HEAD_DIM] = 1.0
    gexp = gexp.reshape(2 * LANES, -1)
    return jnp.asarray(ovt, BF16), jnp.asarray(eaug, BF16), jnp.asarray(gexp, BF16)


def kernel(x, w_in, w_out, norm_pre, norm_post, cmp_pos, cmp_w1, cmp_w2, mla_q_norm, mla_w_uq, mla_kv_norm,
           mla_w_ukv, swa_sinks, rel_bias):
    B, T, D = x.shape
    depth = w_in.shape[0]
    assert D == D_MODEL and T % (2 * FLASH_TILE) == 0 and T // SEL_BLOCK <= HEAD_DIM
    ncp = T // CMP_STRIDE

    w_in_p, w_z = _prep_w_in(w_in)
    wuq, wk, wv = _prep_mla(mla_w_uq, mla_w_ukv)
    pet, peb, w1t, w1b, w2 = _prep_compress(cmp_pos, cmp_w1, cmp_w2)
    w_out_b = w_out.astype(BF16)
    rope_c, rope_sa, rope_sb = _rope_tables(T)
    ovt, eaug, gexp = _static_tables(T)

    bc = _bias_table(rel_bias, 0, NSA_HEADS, ncp, T, CMP_STRIDE, 1, -(CMP_LEN - 1), 0, 1 << 30, 32)
    bc = jnp.transpose(bc.reshape(NSA_HEADS, ncp, T), (0, 2, 1))
    tab_near = _bias_table(rel_bias, 0, NSA_HEADS, FLASH_TILE, 2 * FLASH_TILE, 1, -1, FLASH_TILE, 0, 1 << 30, 256,
                           mult=LOG2E).reshape(NSA_HEADS, FLASH_TILE, 2 * FLASH_TILE)
    tab_win = _bias_table(rel_bias, 0, NSA_HEADS, WIN_TILE, NSA_WINDOW + WIN_TILE, 1, -1, 0, 0, NSA_WINDOW, WIN_TILE,
                          nblk=NSA_WINDOW // WIN_TILE + 1, blk_d0=WIN_TILE)
    tab_swa = _bias_table(rel_bias, NSA_HEADS, SWA_HEADS, Q_TILE, SWA_WINDOW + Q_TILE, 1, -1, 0, 0, SWA_WINDOW, Q_TILE,
                          nblk=SWA_WINDOW // Q_TILE + 1, blk_d0=Q_TILE)
    b31 = rel_bias[:NSA_HEADS, REL_BUCKETS - 1] * LOG2E

    x2 = x.reshape(B * T, D)
    for l in range(depth):
        (qa, qa2, cmpkv, slck, slcv, win, qm, km, vm, krb, cq, ckv) = _project(
            x2, norm_pre[l][None], w_in_p[l], mla_q_norm[l][None], wuq[l], mla_kv_norm[l][None], wk[l], wv[l],
            rope_c, rope_sa, rope_sb, eaug, T, PROJ_ROWS)
        r3 = lambda a: a.reshape(B, T, a.shape[-1])
        kvc = _compress(r3(cmpkv), pet[l], peb[l], w1t[l], w1b[l], w2[l])
        qa3 = r3(qa)
        ocmp, selb = _cmp_attention(qa3, kvc, bc, ovt, T)
        owin = _band_attention(qa3, r3(win), tab_win, None, NSA_HEADS, NSA_WINDOW, T, BF16)
        oslc = _slc_attention(b31, r3(qa2), selb, r3(slck), r3(slcv), tab_near, T)
        oc = _band_attention(r3(cq), r3(ckv), tab_swa, swa_sinks[l], SWA_HEADS, SWA_WINDOW, T, BF16)
        ob = _mla_attention(r3(qm), r3(km), r3(vm), T)
        flat = lambda a: a.reshape(B * T, a.shape[-1])
        x2 = _out_project(x2, norm_pre[l][None], w_z[l], flat(ocmp), flat(oslc), flat(owin), krb, gexp,
                          flat(ob), flat(oc), w_out_b[l], norm_post[l][None], OUT_ROWS)
    return x2.reshape(B, T, D)
```

```python
import functools
import math

import numpy as np
import jax
import jax.numpy as jnp
from jax import lax
from jax.experimental import pallas as pl
from jax.experimental.pallas import tpu as pltpu

F32 = jnp.float32
BF16 = jnp.bfloat16

D_MODEL = 1024
HEAD_DIM = 64
NSA_HEADS = 4
CMP_LEN = 32
CMP_STRIDE = 16
CMP_HIDDEN = 128
SEL_BLOCK = 64
SEL_TOP = 16
NSA_WINDOW = 512
MLA_HEADS = 4
MLA_Q_RANK = 256
MLA_KV_RANK = 128
MLA_NOPE = 64
MLA_ROPE = 32
MLA_V = 64
ROPE_THETA = 10000.0
SWA_HEADS = 8
SWA_WINDOW = 128
REL_BUCKETS = 32
REL_MAX_DIST = 512
NORM_EPS = 1e-6
NEG = -1e30
BIG = 1e9
LOG2E = math.log2(math.e)
MLA_SCALE2 = (MLA_NOPE + MLA_ROPE) ** -0.5 * LOG2E
IN_SIZES = (256, 384, 12, 256, 128, 32, 512, 128, 1024)

LANES = 128
Q_TILE = 128
WIN_TILE = 128
BAND_ROWS = 4096
PROJ_ROWS = 1024
OUT_SUBTILES = 4
OUT_ROWS = 1024
CMP_GROUP = 4
FLASH_TILE = 512
SLC_DIAG_PARTS = 4
MLA_DIAG_PARTS = 4
GATE_LANE0 = 96
VMEM_LIMIT = 56 * 1024 * 1024

C_AQ, C_CMP, C_SLC, C_WIN, C_BCQ, C_BCKV, C_KRB, C_CQ, C_CKV, C_Z = (
    0, 256, 384, 512, 640, 896, 1024, 1152, 1664, 1792)


def _dot(a, b):
    return jnp.dot(a, b, preferred_element_type=F32)


def _dot_nt(a, b):
    return lax.dot_general(a, b, (((1,), (1,)), ((), ())), preferred_element_type=F32)


def _bucket_thresholds():
    d = np.arange(0, 4 * REL_MAX_DIST)
    exact = REL_BUCKETS // 2
    large = exact + (np.log(np.maximum(d, 1).astype(np.float32) / np.float32(exact))
                     / np.float32(math.log(REL_MAX_DIST / exact)) * np.float32(REL_BUCKETS - exact)).astype(np.int32)
    b = np.where(d < exact, d, np.minimum(large, REL_BUCKETS - 1))
    assert np.all(np.diff(b) >= 0) and b[-1] == REL_BUCKETS - 1
    return [int(np.argmax(b >= k)) for k in range(REL_BUCKETS)]


_THRESH = _bucket_thresholds()


def _cparams(sem):
    return pltpu.CompilerParams(dimension_semantics=sem, vmem_limit_bytes=VMEM_LIMIT)


def _table_kernel(h0, row_step, sign, d0, blk_d0, lo, hi, mult, rb_ref, out_ref):
    h = pl.program_id(0) + h0
    blk = pl.program_id(1)
    rt, cc = out_ref.shape[2], out_ref.shape[3]
    base = d0 + blk * blk_d0 - sign * row_step * rt * pl.program_id(2)
    width = -(-(row_step * rt + cc) // LANES) * LANES
    w = lax.broadcasted_iota(jnp.int32, (8, width), 1)
    dist = base + sign * jnp.where(w < cc, w, w - width)
    acc = jnp.full(dist.shape, rb_ref[h, 0], F32)
    for k in range(1, REL_BUCKETS):
        acc = jnp.where(dist >= _THRESH[k], rb_ref[h, k], acc)
    row = jnp.where((dist >= lo) & (dist < hi), acc * mult, NEG)
    rolled = pltpu.roll(jnp.broadcast_to(row[:1], (rt, width)), 0, 1, stride=row_step, stride_axis=0)
    out_ref[0, 0] = rolled[:, :cc]


def _bias_table(rel_bias, h0, nh, rows, cols, row_step, sign, d0, lo, hi, rt, mult=1.0, nblk=1, blk_d0=0):
    return pl.pallas_call(
        functools.partial(_table_kernel, h0, row_step, sign, d0, blk_d0, lo, hi, mult),
        grid=(nh, nblk, rows // rt),
        in_specs=[pl.BlockSpec(memory_space=pltpu.SMEM)],
        out_specs=pl.BlockSpec((1, 1, rt, cols), lambda h, b, r: (h, b, r, 0)),
        out_shape=jax.ShapeDtypeStruct((nh, nblk, rows, cols), F32),
        compiler_params=_cparams(("arbitrary", "arbitrary", "arbitrary")),
        name="bias_table",
    )(rel_bias)


def _rms(v, g):
    return v * lax.rsqrt(jnp.mean(v * v, axis=-1, keepdims=True) + NORM_EPS) * g


def _proj_kernel(x_ref, gpre_ref, w_ref, qn_ref, wuq_ref, kvn_ref, wk_ref, wv_ref,
                 rc_ref, rsa_ref, rsb_ref, eaug_ref,
                 qa_ref, cmp_ref, slck_ref, slcv_ref, win_ref, qm_ref, km_ref, vm_ref,
                 krb_ref, cq_ref, ckv_ref):
    hb = _rms(x_ref[...], gpre_ref[...]).astype(BF16)
    bounds = ((C_AQ, C_CMP), (C_CMP, C_WIN), (C_WIN, C_KRB), (C_KRB, C_Z))
    chunks = {ab: _dot(hb, w_ref[:, ab[0]:ab[1]]) for ab in bounds}
    lane = lax.broadcasted_iota(jnp.int32, (hb.shape[0], LANES), 1)
    lo = lane < HEAD_DIM

    def proj(c0, c1):
        for (a, b), y in chunks.items():
            if a <= c0 and c1 <= b:
                return y[:, c0 - a:c1 - a]
        raise AssertionError((c0, c1))

    qa_ref[...] = (proj(C_AQ, C_CMP) * (0.125 * LOG2E)).astype(BF16)
    cmp_ref[...] = proj(C_CMP, C_SLC)
    slc = proj(C_SLC, C_WIN)
    slcv_ref[...] = jnp.where(lo, 1.0, slc).astype(BF16)
    slck_ref[...] = jnp.where(lo, slc, eaug_ref[...].astype(F32)).astype(BF16)
    win_ref[...] = proj(C_WIN, C_BCQ).astype(BF16)
    cq_ref[...] = (proj(C_CQ, C_CKV) * (0.125 * LOG2E)).astype(BF16)
    ckv_ref[...] = proj(C_CKV, C_Z).astype(BF16)

    rc, rsa, rsb = rc_ref[...], rsa_ref[...], rsb_ref[...]

    def rope(v):
        return v * rc + pltpu.roll(v, 16, 1) * rsa + pltpu.roll(v, 112, 1) * rsb

    cqn = _rms(proj(C_BCQ, C_BCKV), qn_ref[...]).astype(BF16)
    qm = _dot(cqn, wuq_ref[...])
    for h in range(MLA_HEADS):
        qh = rope(qm[:, LANES * h:LANES * (h + 1)]) * MLA_SCALE2
        qm_ref[:, LANES * h:LANES * (h + 1)] = qh.astype(BF16)

    krb = proj(C_KRB, C_CQ)
    krb_ref[...] = krb
    krr = jnp.where(lo | (lane >= HEAD_DIM + MLA_ROPE), 0.0, rope(krb))

    ckvn = _rms(proj(C_BCKV, C_KRB), kvn_ref[...]).astype(BF16)
    kn = _dot(ckvn, wk_ref[...])
    for h in range(MLA_HEADS):
        km_ref[:, LANES * h:LANES * (h + 1)] = (kn[:, LANES * h:LANES * (h + 1)] + krr).astype(BF16)
    vm = _dot(ckvn, wv_ref[...])
    for h in range(MLA_HEADS):
        vm_ref[:, LANES * h:LANES * (h + 1)] = jnp.where(lo, 1.0, vm[:, LANES * h:LANES * (h + 1)]).astype(BF16)


def _project(x2, gpre, w, qn, wuq, kvn, wk, wv, rope_c, rope_sa, rope_sb, eaug, T, tm):
    BT = x2.shape[0]
    nt = T // tm
    row = lambda i: (i, 0)
    fix = lambda i: (0, 0)
    pos = lambda i: (i % nt, 0)
    widths = [(256, BF16), (128, F32), (128, BF16), (128, BF16), (128, BF16), (512, BF16), (512, BF16),
              (512, BF16), (128, F32), (512, BF16), (128, BF16)]
    return pl.pallas_call(
        _proj_kernel,
        grid=(BT // tm,),
        in_specs=[pl.BlockSpec((tm, D_MODEL), row), pl.BlockSpec((1, D_MODEL), fix),
                  pl.BlockSpec(w.shape, fix), pl.BlockSpec(qn.shape, fix), pl.BlockSpec(wuq.shape, fix),
                  pl.BlockSpec(kvn.shape, fix), pl.BlockSpec(wk.shape, fix), pl.BlockSpec(wv.shape, fix),
                  pl.BlockSpec((tm, LANES), pos), pl.BlockSpec((tm, LANES), pos), pl.BlockSpec((tm, LANES), pos),
                  pl.BlockSpec((tm, LANES), pos)],
        out_specs=[pl.BlockSpec((tm, wd), row) for wd, _ in widths],
        out_shape=[jax.ShapeDtypeStruct((BT, wd), dt) for wd, dt in widths],
        compiler_params=_cparams(("arbitrary",)),
        name="proj",
    )(x2, gpre, w, qn, wuq, kvn, wk, wv, rope_c, rope_sa, rope_sb, eaug)


def _compress_kernel(c_ref, pet_ref, peb_ref, w1t_ref, w1b_ref, w2_ref, o_ref):
    nch = c_ref.shape[1] // CMP_STRIDE
    c = jnp.concatenate([c_ref[0, pl.ds(j, nch, stride=CMP_STRIDE), :] for j in range(CMP_STRIDE)], axis=1)
    top = (c + pet_ref[...]).astype(BF16)
    bot = (c + peb_ref[...]).astype(BF16)
    out = jnp.zeros((nch, LANES), F32)
    for i in range(2):
        pre = _dot(top, w1t_ref[i]) + pltpu.roll(_dot(bot, w1b_ref[i]), nch - 1, 0)
        hid = pre * (1.0 / (1.0 + jnp.exp(-pre)))
        out = out + _dot(hid.astype(BF16), w2_ref[i])
    o_ref[0] = out.astype(BF16)


def _compress(tokens, pet, peb, w1t, w1b, w2):
    B, T, width = tokens.shape
    nch = T // CMP_STRIDE
    full = lambda a: pl.BlockSpec(a.shape, lambda b: (0,) * a.ndim)
    return pl.pallas_call(
        _compress_kernel,
        grid=(B,),
        in_specs=[pl.BlockSpec((1, T, width), lambda b: (b, 0, 0)),
                  full(pet), full(peb), full(w1t), full(w1b), full(w2)],
        out_specs=pl.BlockSpec((1, nch, LANES), lambda b: (b, 0, 0)),
        out_shape=jax.ShapeDtypeStruct((B, nch, LANES), BF16),
        compiler_params=_cparams(("arbitrary",)),
        name="compress",
    )(tokens, pet, peb, w1t, w1b, w2)


def _head_slabs(qblk, n_heads):
    q = qblk.astype(F32)
    lo = lax.broadcasted_iota(jnp.int32, (q.shape[0], LANES), 1) < HEAD_DIM
    out = []
    for j in range(n_heads // 2):
        slab = q[:, LANES * j:LANES * (j + 1)]
        out.append(jnp.where(lo, slab, 0.0))
        out.append(jnp.where(lo, pltpu.roll(slab, HEAD_DIM, 1), 0.0))
    return out


def _merge_upper(accs):
    lo = lax.broadcasted_iota(jnp.int32, accs[0].shape, 1) < HEAD_DIM
    slabs = [jnp.where(lo, pltpu.roll(accs[2 * j], HEAD_DIM, 1), accs[2 * j + 1]) for j in range(len(accs) // 2)]
    return jnp.concatenate(slabs, axis=1)


def _merge_normalized(accs):
    lo = lax.broadcasted_iota(jnp.int32, accs[0].shape, 1) < HEAD_DIM
    slabs = []
    for j in range(len(accs) // 2):
        a, b = accs[2 * j], accs[2 * j + 1]
        out = jnp.where(lo, pltpu.roll(a, HEAD_DIM, 1), b)
        den = jnp.where(lo, a, pltpu.roll(b, HEAD_DIM, 1))
        slabs.append(out / den)
    return jnp.concatenate(slabs, axis=1)


def _row_max_lanes(s):
    return jnp.broadcast_to(jnp.max(s, axis=-1, keepdims=True), (s.shape[0], LANES))


def _sub_lanes(s, ref):
    return jnp.concatenate([s[:, LANES * i:LANES * (i + 1)] - ref for i in range(s.shape[1] // LANES)], axis=1)


def _topk_mask_t(imp, t0):
    ns, tq = imp.shape
    srow = lax.broadcasted_iota(jnp.int32, (ns, tq), 0)
    cur = (t0 + lax.broadcasted_iota(jnp.int32, (ns, tq), 1)) // SEL_BLOCK
    forced = (srow == 0) | (srow == cur) | (srow == cur - 1)
    x = jnp.where(forced, BIG, jnp.where(srow <= cur, imp, -BIG))
    sub = 8
    groups = [x[sub * v:sub * (v + 1)] for v in range(ns // sub)]
    rows_in = [srow[sub * v:sub * (v + 1)] for v in range(ns // sub)]
    cnts = [jnp.zeros((sub, tq), F32) for _ in groups]
    for sp in range(ns):
        other = jnp.broadcast_to(x[sp:sp + 1, :], (sub, tq))
        for v, xv in enumerate(groups):
            if sub * v > sp:
                beats = other >= xv
            elif sub * v + sub - 1 < sp:
                beats = other > xv
            else:
                beats = (other > xv) | ((other == xv) & (rows_in[v] > sp))
            cnts[v] = cnts[v] + jnp.where(beats, 1.0, 0.0)
    cnt = jnp.concatenate(cnts, axis=0)
    sel = (cnt < float(min(SEL_TOP, ns))) & (srow <= cur)
    return jnp.where(sel, 0.0, NEG)


def _cmp_kernel(q_ref, kvc_ref, bc_ref, ovt_ref, ocmp_ref, selb_ref):
    rows = CMP_GROUP * Q_TILE
    ncp, ns = kvc_ref.shape[1], ovt_ref.shape[0]
    step = pl.program_id(1)
    lo_step, nse = 0, SEL_TOP
    while lo_step * rows < ns * SEL_BLOCK:
        nse = min(nse, ns)
        hi_step = nse * SEL_BLOCK // rows
        ncw = min(ncp, -(-nse * (SEL_BLOCK // CMP_STRIDE) // LANES) * LANES)
        if hi_step > lo_step:
            pl.when((step >= lo_step) & (step < hi_step))(
                functools.partial(_cmp_body, ncw, nse, q_ref, kvc_ref, bc_ref, ovt_ref, ocmp_ref, selb_ref))
            lo_step = hi_step
        nse += SEL_TOP


def _cmp_body(ncw, ns, q_ref, kvc_ref, bc_ref, ovt_ref, ocmp_ref, selb_ref):
    tq = Q_TILE
    kvc = kvc_ref[0, :ncw, :]
    ovt = ovt_ref[:ns, :ncw]
    groups = range(CMP_GROUP)
    sls = [slice(g * tq, (g + 1) * tq) for g in groups]
    s_alls = []
    for g in groups:
        qs = _head_slabs(q_ref[0, sls[g], :], NSA_HEADS)
        s_alls.append(_dot_nt(jnp.concatenate([q.astype(BF16) for q in qs], axis=0), kvc))
    psums, o_alls = [], []
    for g in groups:
        psum = None
        ps = []
        for h in range(NSA_HEADS):
            s = s_alls[g][h * tq:(h + 1) * tq] + bc_ref[h, sls[g], :ncw]
            m = jnp.max(s, axis=-1, keepdims=True)
            e = jnp.exp2(s - m)
            l = jnp.sum(e, axis=-1, keepdims=True)
            p = e * jnp.where(m > 0.5 * NEG, 1.0 / l, 0.0)
            psum = p if psum is None else psum + p
            ps.append(p.astype(BF16))
        psums.append(psum)
        o_alls.append(_dot(jnp.concatenate(ps, axis=0), kvc))
    imps = []
    for g in groups:
        hi = psums[g].astype(BF16)
        lo = (psums[g] - hi.astype(F32)).astype(BF16)
        imps.append(_dot_nt(ovt, hi) + _dot_nt(ovt, lo))
    for g in groups:
        ocmp = _merge_upper([o_alls[g][h * tq:(h + 1) * tq] for h in range(NSA_HEADS)])
        ocmp_ref[0, sls[g], :] = ocmp.astype(ocmp_ref.dtype)
        sb = _topk_mask_t(imps[g], (pl.program_id(1) * CMP_GROUP + g) * tq)
        if ns < HEAD_DIM:
            sb = jnp.concatenate([sb, jnp.full((HEAD_DIM - ns, tq), NEG, F32)], axis=0)
        full = jnp.concatenate([jnp.zeros((HEAD_DIM, tq), F32), sb], axis=0)
        selb_ref[0, sls[g], :] = full.T.astype(BF16)


def _cmp_attention(qa, kvc, bc, ovt, T):
    B = qa.shape[0]
    ncp = kvc.shape[1]
    rows = CMP_GROUP * Q_TILE
    return pl.pallas_call(
        _cmp_kernel,
        grid=(B, T // rows),
        in_specs=[pl.BlockSpec((1, rows, 256), lambda b, n: (b, n, 0)),
                  pl.BlockSpec((1, ncp, LANES), lambda b, n: (b, 0, 0)),
                  pl.BlockSpec((NSA_HEADS, rows, ncp), lambda b, n: (0, n, 0)),
                  pl.BlockSpec(ovt.shape, lambda b, n: (0, 0))],
        out_specs=[pl.BlockSpec((1, rows, 256), lambda b, n: (b, n, 0)),
                   pl.BlockSpec((1, rows, LANES), lambda b, n: (b, n, 0))],
        out_shape=[jax.ShapeDtypeStruct((B, T, 256), BF16), jax.ShapeDtypeStruct((B, T, LANES), BF16)],
        compiler_params=_cparams(("arbitrary", "arbitrary")),
        name="cmp_select",
    )(qa, kvc, bc, ovt)


def _softmax_tile(s2, rows, m_s, shift=None):
    m_t = _row_max_lanes(s2)
    if shift is not None:
        m_t = m_t + shift
    m_old = m_s[rows]
    m_new = jnp.maximum(m_old, m_t)
    p = jnp.exp2(_sub_lanes(s2, m_new if shift is None else m_new - shift))
    alpha = jnp.exp2(m_old - m_new)
    m_s[rows] = m_new
    return p.astype(BF16), alpha


def _slc_kernel(b31_ref, q_ref, selb_ref, ka_ref, kv_ref, tab_ref, o_ref, m_s, acc_s):
    tq = q_ref.shape[1]
    H = NSA_HEADS
    n = pl.program_id(1)
    qs = _head_slabs(q_ref[0], H)
    selb = selb_ref[0].astype(F32)
    qst = jnp.concatenate([(qs[h] + selb).astype(BF16) for h in range(H)], axis=0)
    m_s[...] = jnp.full(m_s.shape, -jnp.inf, F32)
    acc_s[...] = jnp.zeros(acc_s.shape, F32)

    def scores(k0, width=tq):
        k0 = pl.multiple_of(k0, tq)
        return _dot_nt(qst, ka_ref[0, pl.ds(k0, width), :]), kv_ref[0, pl.ds(k0, width), :]

    def update(s_all, kv):
        ps, alphas = [], []
        for h in range(H):
            rows = slice(h * tq, (h + 1) * tq)
            p, alpha = _softmax_tile(s_all[rows], rows, m_s, shift=b31_ref[h])
            ps.append(p)
            alphas.append(alpha)
        pv = _dot(jnp.concatenate(ps, axis=0), kv)
        for h in range(H):
            rows = slice(h * tq, (h + 1) * tq)
            acc_s[rows] = alphas[h] * acc_s[rows] + pv[rows]

    def far_triple(i, carry):
        update(*scores(3 * i * tq, 3 * tq))
        return carry

    n_far = jnp.maximum(n - 1, 0)
    lax.fori_loop(0, n_far // 3, far_triple, 0)

    @pl.when(n_far % 3 == 1)
    def _():
        update(*scores((n_far - 1) * tq))

    @pl.when(n_far % 3 == 2)
    def _():
        update(*scores((n_far - 2) * tq, 2 * tq))

    def near_diagonal(k0, before):
        k0 = pl.multiple_of(k0, tq)
        half = tq // SLC_DIAG_PARTS
        parts = []
        for part in range(SLC_DIAG_PARTS):
            kw = before + half * (part + 1)
            q_part = jnp.concatenate([qst[h * tq + part * half:h * tq + (part + 1) * half] for h in range(H)], axis=0)
            parts.append((_dot_nt(q_part, ka_ref[0, pl.ds(k0, kw), :]), kv_ref[0, pl.ds(k0, kw), :], kw))
        for part, (s_all, kv, kw) in enumerate(parts):
            ps, alphas = [], []
            for h in range(H):
                rows = slice(h * tq + part * half, h * tq + (part + 1) * half)
                bias = tab_ref[h, part * half:(part + 1) * half, tq - before:tq - before + kw]
                p, alpha = _softmax_tile(s_all[h * half:(h + 1) * half] + bias, rows, m_s)
                ps.append(p)
                alphas.append(alpha)
            pv = _dot(jnp.concatenate(ps, axis=0), kv)
            for h in range(H):
                rows = slice(h * tq + part * half, h * tq + (part + 1) * half)
                acc_s[rows] = alphas[h] * acc_s[rows] + pv[h * half:(h + 1) * half]

    @pl.when(n >= 1)
    def _():
        near_diagonal(jnp.maximum(n - 1, 0) * tq, tq)

    @pl.when(n == 0)
    def _():
        near_diagonal(0, 0)

    o_ref[0] = _merge_normalized([acc_s[h * tq:(h + 1) * tq] for h in range(H)]).astype(o_ref.dtype)


def _slc_attention(b31, qa, selb, slck, slcv, tab, T):
    B = qa.shape[0]
    tq = FLASH_TILE
    H = NSA_HEADS
    tile = lambda w: pl.BlockSpec((1, tq, w), lambda b, n: (b, n, 0))
    return pl.pallas_call(
        _slc_kernel,
        grid=(B, T // tq),
        in_specs=[pl.BlockSpec(memory_space=pltpu.SMEM),
                  tile(256), tile(LANES),
                  pl.BlockSpec((1, T, LANES), lambda b, n: (b, 0, 0)),
                  pl.BlockSpec((1, T, LANES), lambda b, n: (b, 0, 0)),
                  pl.BlockSpec(tab.shape, lambda b, n: (0, 0, 0))],
        out_specs=tile(256),
        out_shape=jax.ShapeDtypeStruct((B, T, 256), BF16),
        scratch_shapes=[pltpu.VMEM((H * tq, LANES), F32), pltpu.VMEM((H * tq, LANES), F32)],
        compiler_params=_cparams(("arbitrary", "arbitrary")),
        name="slc_attention",
    )(b31, qa, selb, slck, slcv, tab)


def _band_kernel(n_heads, window, has_sink, n_sub, tq, *refs):
    if has_sink:
        q_ref, kv_ref, tab_ref, sink_ref, o_ref = refs
    else:
        q_ref, kv_ref, tab_ref, o_ref = refs
    span = window + tq
    lo_kv = lax.broadcasted_iota(jnp.int32, (span, LANES), 1) < HEAD_DIM
    lo_o = lax.broadcasted_iota(jnp.int32, (n_heads * tq, LANES), 1) < HEAD_DIM
    groups = range(n_sub)
    tiles = [pl.program_id(1) * n_sub + g for g in groups]
    kvs, dots = [], []
    for g in groups:
        k0 = pl.multiple_of(jnp.maximum(tiles[g] * tq - window, 0), tq)
        kvs.append(kv_ref[0, pl.ds(k0, span), :])
        qs = _head_slabs(q_ref[0, g * tq:(g + 1) * tq, :], n_heads)
        dots.append(_dot_nt(jnp.concatenate([q.astype(BF16) for q in qs], axis=0), kvs[g]))
    accs, ms = [], []
    for g in groups:
        s = dots[g] + tab_ref[jnp.minimum(tiles[g], window // tq)]
        m = _row_max_lanes(s)
        if has_sink:
            m = jnp.maximum(m, sink_ref[...])
        e = jnp.exp2(_sub_lanes(s, m))
        ms.append(m)
        accs.append(_dot(e.astype(BF16), jnp.where(lo_kv, 1.0, kvs[g]).astype(BF16)))
    for g in groups:
        acc = accs[g]
        if has_sink:
            acc = acc + jnp.where(lo_o, jnp.exp2(sink_ref[...] - ms[g]), 0.0)
        out = _merge_normalized([acc[h * tq:(h + 1) * tq] for h in range(n_heads)])
        o_ref[0, g * tq:(g + 1) * tq, :] = out.astype(o_ref.dtype)


def _band_attention(q, kv_pad, tab, sinks, n_heads, window, T, out_dtype):
    B = q.shape[0]
    width = n_heads * HEAD_DIM
    has_sink = sinks is not None
    nvar, tq = tab.shape[1], tab.shape[2]
    tab2 = jnp.transpose(tab, (1, 0, 2, 3)).reshape(nvar, n_heads * tq, window + tq)
    n_sub = (BAND_ROWS if has_sink else 2 * BAND_ROWS) // (n_heads * tq)
    rows = n_sub * tq
    in_specs = [pl.BlockSpec((1, rows, width), lambda b, n: (b, n, 0)),
                pl.BlockSpec((1, T, LANES), lambda b, n: (b, 0, 0)),
                pl.BlockSpec(tab2.shape, lambda b, n: (0, 0, 0))]
    args = (q, kv_pad, tab2)
    if has_sink:
        rep = jnp.broadcast_to(jnp.repeat(sinks, tq)[:, None], (n_heads * tq, LANES))
        in_specs.append(pl.BlockSpec(rep.shape, lambda b, n: (0, 0)))
        args = args + (rep,)
    return pl.pallas_call(
        functools.partial(_band_kernel, n_heads, window, has_sink, n_sub, tq),
        grid=(B, T // rows),
        in_specs=in_specs,
        out_specs=pl.BlockSpec((1, rows, width), lambda b, n: (b, n, 0)),
        out_shape=jax.ShapeDtypeStruct((B, T, width), out_dtype),
        compiler_params=_cparams(("arbitrary", "arbitrary")),
        name="band_sink" if has_sink else "band_window",
    )(*args)


def _mla_kernel(q_ref, k_ref, v_ref, o_ref, m_s, acc_s):
    tq = q_ref.shape[1]
    H = MLA_HEADS
    n = pl.program_id(1)
    m_s[...] = jnp.full(m_s.shape, -jnp.inf, F32)
    acc_s[...] = jnp.zeros(acc_s.shape, F32)

    def scores(k0, width=tq):
        k0 = pl.multiple_of(k0, tq)
        return k0, width, [_dot_nt(q_ref[0, :, LANES * h:LANES * (h + 1)],
                                   k_ref[0, pl.ds(k0, width), LANES * h:LANES * (h + 1)]) for h in range(H)]

    def update(k0, width, ss):
        ps, alphas = [], []
        for h in range(H):
            p, alpha = _softmax_tile(ss[h], slice(h * tq, (h + 1) * tq), m_s)
            ps.append(p)
            alphas.append(alpha)
        for h in range(H):
            rows = slice(h * tq, (h + 1) * tq)
            v = v_ref[0, pl.ds(k0, width), LANES * h:LANES * (h + 1)]
            acc_s[rows] = alphas[h] * acc_s[rows] + _dot(ps[h], v)

    def far_triple(i, carry):
        update(*scores(3 * i * tq, 3 * tq))
        return carry

    n_far = jnp.maximum(n - 1, 0)
    lax.fori_loop(0, n_far // 3, far_triple, 0)

    @pl.when(n_far % 3 == 1)
    def _():
        update(*scores((n_far - 1) * tq))

    @pl.when(n_far % 3 == 2)
    def _():
        update(*scores((n_far - 2) * tq, 2 * tq))

    def near_diagonal(k0, before):
        k0 = pl.multiple_of(k0, tq)
        half = tq // MLA_DIAG_PARTS
        parts = []
        for part in range(MLA_DIAG_PARTS):
            kw = before + half * (part + 1)
            r = slice(part * half, (part + 1) * half)
            parts.append((kw, [_dot_nt(q_ref[0, r, LANES * h:LANES * (h + 1)],
                                       k_ref[0, pl.ds(k0, kw), LANES * h:LANES * (h + 1)]) for h in range(H)]))
        for part, (kw, ss) in enumerate(parts):
            row = lax.broadcasted_iota(jnp.int32, (half, kw), 0)
            col = lax.broadcasted_iota(jnp.int32, (half, kw), 1)
            causal = col <= row + (before + part * half)
            ps, alphas = [], []
            for h in range(H):
                rows = slice(h * tq + part * half, h * tq + (part + 1) * half)
                p, alpha = _softmax_tile(jnp.where(causal, ss[h], NEG), rows, m_s)
                ps.append(p)
                alphas.append(alpha)
            for h in range(H):
                rows = slice(h * tq + part * half, h * tq + (part + 1) * half)
                v = v_ref[0, pl.ds(k0, kw), LANES * h:LANES * (h + 1)]
                acc_s[rows] = alphas[h] * acc_s[rows] + _dot(ps[h], v)

    @pl.when(n >= 1)
    def _():
        near_diagonal(jnp.maximum(n - 1, 0) * tq, tq)

    @pl.when(n == 0)
    def _():
        near_diagonal(0, 0)

    o_ref[0] = _merge_normalized([acc_s[h * tq:(h + 1) * tq] for h in range(H)]).astype(o_ref.dtype)


def _mla_attention(qm, km, vm, T):
    B = qm.shape[0]
    tq = FLASH_TILE
    H = MLA_HEADS
    return pl.pallas_call(
        _mla_kernel,
        grid=(B, T // tq),
        in_specs=[pl.BlockSpec((1, tq, 512), lambda b, n: (b, n, 0)),
                  pl.BlockSpec((1, T, 512), lambda b, n: (b, 0, 0)),
                  pl.BlockSpec((1, T, 512), lambda b, n: (b, 0, 0))],
        out_specs=pl.BlockSpec((1, tq, 256), lambda b, n: (b, n, 0)),
        out_shape=jax.ShapeDtypeStruct((B, T, 256), BF16),
        scratch_shapes=[pltpu.VMEM((H * tq, LANES), F32), pltpu.VMEM((H * tq, LANES), F32)],
        compiler_params=_cparams(("arbitrary", "arbitrary")),
        name="mla_attention",
    )(qm, km, vm)


def _out_kernel(x_ref, gpre_ref, wz_ref, ocmp_ref, oslc_ref, owin_ref, krb_ref, gexp_ref, ob_ref, oc_ref, w_ref,
                gpost_ref, o_ref):
    sub = x_ref.shape[0] // OUT_SUBTILES
    rs = [slice(t * sub, (t + 1) * sub) for t in range(OUT_SUBTILES)]
    w_a = NSA_HEADS * HEAD_DIM
    hs = [_rms(x_ref[r, :], gpre_ref[...]).astype(BF16) for r in rs]
    zs = [_dot(h, wz_ref[...]) for h in hs]
    mixeds = []
    for r, z in zip(rs, zs):
        sig = 1.0 / (1.0 + jnp.exp(-krb_ref[r, :]))
        hi = sig.astype(BF16)
        lo = (sig - hi.astype(F32)).astype(BF16)
        gates = _dot(jnp.concatenate([hi, lo], axis=1), gexp_ref[...])
        oa = None
        for j, br in enumerate((ocmp_ref, oslc_ref, owin_ref)):
            term = gates[:, w_a * j:w_a * (j + 1)] * br[r, :].astype(F32)
            oa = term if oa is None else oa + term
        heads = jnp.concatenate([oa, ob_ref[r, :].astype(F32), oc_ref[r, :].astype(F32)], axis=1)
        mixeds.append((heads * (z * (1.0 / (1.0 + jnp.exp(-z))))).astype(BF16))
    ys = [_dot(m, w_ref[...]) for m in mixeds]
    for r, y in zip(rs, ys):
        o_ref[r, :] = x_ref[r, :] + _rms(y, gpost_ref[...])


def _out_project(x2, gpre, wz, ocmp, oslc, owin, krb, gexp, ob, oc, w, gpost, tm):
    BT = x2.shape[0]
    row = lambda i: (i, 0)
    fix = lambda i: (0, 0)
    spec = lambda a: pl.BlockSpec((tm, a.shape[1]), row)
    full = lambda a: pl.BlockSpec(a.shape, fix)
    return pl.pallas_call(
        _out_kernel,
        grid=(BT // tm,),
        in_specs=[spec(x2), full(gpre), full(wz), spec(ocmp), spec(oslc), spec(owin), spec(krb), full(gexp),
                  spec(ob), spec(oc), full(w), full(gpost)],
        out_specs=pl.BlockSpec((tm, D_MODEL), row),
        out_shape=jax.ShapeDtypeStruct((BT, D_MODEL), F32),
        compiler_params=_cparams(("arbitrary",)),
        name="out_proj",
    )(x2, gpre, wz, ocmp, oslc, owin, krb, gexp, ob, oc, w, gpost)


def _split_in(w):
    outs, o = [], 0
    for s in IN_SIZES:
        outs.append(w[..., o:o + s])
        o += s
    return outs


def _relayout_kernel(w_ref, wp_ref, wz_ref):
    a_q, a_kv, a_g, b_cq, b_ckv, b_kr, c_q, c_kv, z = _split_in(w_ref[0])
    zeros = lambda n: jnp.zeros((a_q.shape[0], n), F32)
    krb = jnp.concatenate([zeros(HEAD_DIM), b_kr, a_g, zeros(LANES - HEAD_DIM - MLA_ROPE - 3 * NSA_HEADS)], axis=-1)
    wp_ref[0] = jnp.concatenate([a_q, a_kv, b_cq, b_ckv, krb, c_q, c_kv], axis=-1).astype(BF16)
    wz_ref[0] = z.astype(BF16)


def _prep_w_in(w):
    L, d, n = w.shape
    rt = 256
    return pl.pallas_call(
        _relayout_kernel,
        grid=(L, d // rt),
        in_specs=[pl.BlockSpec((1, rt, n), lambda l, r: (l, r, 0))],
        out_specs=[pl.BlockSpec((1, rt, C_Z), lambda l, r: (l, r, 0)),
                   pl.BlockSpec((1, rt, IN_SIZES[-1]), lambda l, r: (l, r, 0))],
        out_shape=[jax.ShapeDtypeStruct((L, d, C_Z), BF16), jax.ShapeDtypeStruct((L, d, IN_SIZES[-1]), BF16)],
        compiler_params=_cparams(("arbitrary", "arbitrary")),
        name="w_in_relayout",
    )(w)


def _prep_mla(w_uq, w_ukv):
    L = w_uq.shape[0]
    dq = MLA_NOPE + MLA_ROPE
    uq = w_uq.reshape(L, MLA_Q_RANK, MLA_HEADS, dq)
    uq = jnp.concatenate([uq, jnp.zeros((L, MLA_Q_RANK, MLA_HEADS, LANES - dq), uq.dtype)], axis=-1)
    ukv = w_ukv.reshape(L, MLA_KV_RANK, MLA_HEADS, MLA_NOPE + MLA_V)
    uk = jnp.concatenate([ukv[..., :MLA_NOPE], jnp.zeros((L, MLA_KV_RANK, MLA_HEADS, LANES - MLA_NOPE), ukv.dtype)], axis=-1)
    uv = jnp.concatenate([jnp.zeros((L, MLA_KV_RANK, MLA_HEADS, LANES - MLA_V), ukv.dtype), ukv[..., MLA_NOPE:]], axis=-1)
    return (uq.reshape(L, MLA_Q_RANK, MLA_HEADS * LANES).astype(BF16),
            uk.reshape(L, MLA_KV_RANK, MLA_HEADS * LANES).astype(BF16),
            uv.reshape(L, MLA_KV_RANK, MLA_HEADS * LANES).astype(BF16))


def _prep_compress(cmp_pos, cmp_w1, cmp_w2):
    L = cmp_pos.shape[0]
    pe = jnp.concatenate([cmp_pos[:, 0], cmp_pos[:, 1]], axis=-1)
    pet = pe[:, :CMP_STRIDE].reshape(L, 1, CMP_STRIDE * LANES)
    peb = pe[:, CMP_STRIDE:].reshape(L, 1, CMP_STRIDE * LANES)
    w1 = cmp_w1.reshape(L, 2, CMP_LEN, HEAD_DIM, CMP_HIDDEN)
    zero = jnp.zeros_like(w1[:, 0])
    w1k = jnp.concatenate([w1[:, 0], zero], axis=2)
    w1v = jnp.concatenate([zero, w1[:, 1]], axis=2)
    w1e = jnp.stack([w1k, w1v], axis=1)
    w1t = w1e[:, :, :CMP_STRIDE].reshape(L, 2, CMP_STRIDE * LANES, CMP_HIDDEN).astype(BF16)
    w1b = w1e[:, :, CMP_STRIDE:].reshape(L, 2, CMP_STRIDE * LANES, CMP_HIDDEN).astype(BF16)
    z2 = jnp.zeros_like(cmp_w2[:, 0])
    w2 = jnp.stack([jnp.concatenate([cmp_w2[:, 0], z2], axis=-1),
                    jnp.concatenate([z2, cmp_w2[:, 1]], axis=-1)], axis=1).astype(BF16)
    return pet, peb, w1t, w1b, w2


def _rope_tables(T):
    half = MLA_ROPE // 2
    inv = ROPE_THETA ** (-jnp.arange(half, dtype=F32) / half)
    ang = jnp.arange(T).astype(F32)[:, None] * inv[None, :]
    cos, sin = jnp.cos(ang), jnp.sin(ang)
    z = lambda n: jnp.zeros((T, n), F32)
    rc = jnp.concatenate([jnp.ones((T, MLA_NOPE), F32), cos, cos, z(LANES - MLA_NOPE - MLA_ROPE)], axis=1)
    rsa = jnp.concatenate([z(MLA_NOPE + half), sin, z(LANES - MLA_NOPE - MLA_ROPE)], axis=1)
    rsb = jnp.concatenate([z(MLA_NOPE), -sin, z(LANES - MLA_NOPE - half)], axis=1)
    return rc, rsa, rsb


def _static_tables(T):
    ns = T // SEL_BLOCK
    nc = T // CMP_STRIDE - 1
    ncp = T // CMP_STRIDE
    sstart = np.arange(ns) * SEL_BLOCK
    cstart = np.arange(nc) * CMP_STRIDE
    overlap = (np.clip(np.minimum(cstart[:, None] + CMP_LEN, sstart[None, :] + SEL_BLOCK)
                       - np.maximum(cstart[:, None], sstart[None, :]), 0, None) / CMP_STRIDE).astype(np.float32)
    ovt = np.zeros((ns, ncp), np.float32)
    ovt[:, :nc] = overlap.T
    eaug = np.zeros((T, LANES), np.float32)
    eaug[np.arange(T), HEAD_DIM + np.arange(T) // SEL_BLOCK] = 1.0
    gexp = np.zeros((2, LANES, 3 * NSA_HEADS * HEAD_DIM), np.float32)
    for h in range(NSA_HEADS):
        for j in range(3):
            c0 = NSA_HEADS * HEAD_DIM * j + HEAD_DIM * h
            gexp[:, GATE_LANE0 + 3 * h + j, c0:c0 + HEAD_DIM] = 1.0
    gexp = gexp.reshape(2 * LANES, -1)
    return jnp.asarray(ovt, BF16), jnp.asarray(eaug, BF16), jnp.asarray(gexp, BF16)


def kernel(x, w_in, w_out, norm_pre, norm_post, cmp_pos, cmp_w1, cmp_w2, mla_q_norm, mla_w_uq, mla_kv_norm,
           mla_w_ukv, swa_sinks, rel_bias):
    B, T, D = x.shape
    depth = w_in.shape[0]
    assert D == D_MODEL and T % (2 * FLASH_TILE) == 0 and T // SEL_BLOCK <= HEAD_DIM
    ncp = T // CMP_STRIDE

    w_in_p, w_z = _prep_w_in(w_in)
    wuq, wk, wv = _prep_mla(mla_w_uq, mla_w_ukv)
    pet, peb, w1t, w1b, w2 = _prep_compress(cmp_pos, cmp_w1, cmp_w2)
    w_out_b = w_out.astype(BF16)
    rope_c, rope_sa, rope_sb = _rope_tables(T)
    ovt, eaug, gexp = _static_tables(T)

    bc = _bias_table(rel_bias, 0, NSA_HEADS, ncp, T, CMP_STRIDE, 1, -(CMP_LEN - 1), 0, 1 << 30, 32, mult=LOG2E)
    bc = jnp.transpose(bc.reshape(NSA_HEADS, ncp, T), (0, 2, 1))
    tab_near = _bias_table(rel_bias, 0, NSA_HEADS, FLASH_TILE, 2 * FLASH_TILE, 1, -1, FLASH_TILE, 0, 1 << 30, 256,
                           mult=LOG2E).reshape(NSA_HEADS, FLASH_TILE, 2 * FLASH_TILE)
    tab_win = _bias_table(rel_bias, 0, NSA_HEADS, WIN_TILE, NSA_WINDOW + WIN_TILE, 1, -1, 0, 0, NSA_WINDOW, WIN_TILE,
                          mult=LOG2E, nblk=NSA_WINDOW // WIN_TILE + 1, blk_d0=WIN_TILE)
    tab_swa = _bias_table(rel_bias, NSA_HEADS, SWA_HEADS, Q_TILE, SWA_WINDOW + Q_TILE, 1, -1, 0, 0, SWA_WINDOW, Q_TILE,
                          mult=LOG2E, nblk=SWA_WINDOW // Q_TILE + 1, blk_d0=Q_TILE)
    b31 = rel_bias[:NSA_HEADS, REL_BUCKETS - 1] * LOG2E

    x2 = x.reshape(B * T, D)
    for l in range(depth):
        (qa, cmpkv, slck, slcv, win, qm, km, vm, krb, cq, ckv) = _project(
            x2, norm_pre[l][None], w_in_p[l], mla_q_norm[l][None], wuq[l], mla_kv_norm[l][None], wk[l], wv[l],
            rope_c, rope_sa, rope_sb, eaug, T, PROJ_ROWS)
        r3 = lambda a: a.reshape(B, T, a.shape[-1])
        kvc = _compress(r3(cmpkv), pet[l], peb[l], w1t[l], w1b[l], w2[l])
        qa3 = r3(qa)
        ocmp, selb = _cmp_attention(qa3, kvc, bc, ovt, T)
        owin = _band_attention(qa3, r3(win), tab_win, None, NSA_HEADS, NSA_WINDOW, T, BF16)
        oslc = _slc_attention(b31, qa3, selb, r3(slck), r3(slcv), tab_near, T)
        oc = _band_attention(r3(cq), r3(ckv), tab_swa, swa_sinks[l] * LOG2E, SWA_HEADS, SWA_WINDOW, T, BF16)
        ob = _mla_attention(r3(qm), r3(km), r3(vm), T)
        flat = lambda a: a.reshape(B * T, a.shape[-1])
        x2 = _out_project(x2, norm_pre[l][None], w_z[l], flat(ocmp), flat(oslc), flat(owin), krb, gexp,
                          flat(ob), flat(oc), w_out_b[l], norm_post[l][None], OUT_ROWS)
    return x2.reshape(B, T, D)
```

```python
import functools
import math

import numpy as np
import jax
import jax.numpy as jnp
from jax import lax
from jax.experimental import pallas as pl
from jax.experimental.pallas import tpu as pltpu

F32 = jnp.float32
BF16 = jnp.bfloat16

D_MODEL = 1024
HEAD_DIM = 64
NSA_HEADS = 4
CMP_LEN = 32
CMP_STRIDE = 16
CMP_HIDDEN = 128
SEL_BLOCK = 64
SEL_TOP = 16
NSA_WINDOW = 512
MLA_HEADS = 4
MLA_Q_RANK = 256
MLA_KV_RANK = 128
MLA_NOPE = 64
MLA_ROPE = 32
MLA_V = 64
ROPE_THETA = 10000.0
SWA_HEADS = 8
SWA_WINDOW = 128
REL_BUCKETS = 32
REL_MAX_DIST = 512
NORM_EPS = 1e-6
NEG = -1e30
BIG = 1e9
LOG2E = math.log2(math.e)
MLA_SCALE2 = (MLA_NOPE + MLA_ROPE) ** -0.5 * LOG2E
IN_SIZES = (256, 384, 12, 256, 128, 32, 512, 128, 1024)

LANES = 128
Q_TILE = 128
WIN_TILE = 128
BAND_ROWS = 4096
PROJ_ROWS = 1024
OUT_SUBTILES = 4
OUT_ROWS = 1024
CMP_GROUP = 4
FLASH_TILE = 512
SLC_DIAG_PARTS = 4
MLA_DIAG_PARTS = 4
GATE_LANE0 = 96
VMEM_LIMIT = 56 * 1024 * 1024

C_AQ, C_CMP, C_SLC, C_WIN, C_BCQ, C_BCKV, C_KRB, C_CQ, C_CKV, C_Z = (
    0, 256, 384, 512, 640, 896, 1024, 1152, 1664, 1792)


def _dot(a, b):
    return jnp.dot(a, b, preferred_element_type=F32)


def _dot_nt(a, b):
    return lax.dot_general(a, b, (((1,), (1,)), ((), ())), preferred_element_type=F32)


def _bucket_thresholds():
    d = np.arange(0, 4 * REL_MAX_DIST)
    exact = REL_BUCKETS // 2
    large = exact + (np.log(np.maximum(d, 1).astype(np.float32) / np.float32(exact))
                     / np.float32(math.log(REL_MAX_DIST / exact)) * np.float32(REL_BUCKETS - exact)).astype(np.int32)
    b = np.where(d < exact, d, np.minimum(large, REL_BUCKETS - 1))
    assert np.all(np.diff(b) >= 0) and b[-1] == REL_BUCKETS - 1
    return [int(np.argmax(b >= k)) for k in range(REL_BUCKETS)]


_THRESH = _bucket_thresholds()


def _cparams(sem):
    return pltpu.CompilerParams(dimension_semantics=sem, vmem_limit_bytes=VMEM_LIMIT)


def _table_kernel(h0, row_step, sign, d0, blk_d0, lo, hi, mult, rb_ref, out_ref):
    h = pl.program_id(0) + h0
    blk = pl.program_id(1)
    rt, cc = out_ref.shape[2], out_ref.shape[3]
    base = d0 + blk * blk_d0 - sign * row_step * rt * pl.program_id(2)
    width = -(-(row_step * rt + cc) // LANES) * LANES
    w = lax.broadcasted_iota(jnp.int32, (8, width), 1)
    dist = base + sign * jnp.where(w < cc, w, w - width)
    acc = jnp.full(dist.shape, rb_ref[h, 0], F32)
    for k in range(1, REL_BUCKETS):
        acc = jnp.where(dist >= _THRESH[k], rb_ref[h, k], acc)
    row = jnp.where((dist >= lo) & (dist < hi), acc * mult, NEG)
    rolled = pltpu.roll(jnp.broadcast_to(row[:1], (rt, width)), 0, 1, stride=row_step, stride_axis=0)
    out_ref[0, 0] = rolled[:, :cc]


def _bias_table(rel_bias, h0, nh, rows, cols, row_step, sign, d0, lo, hi, rt, mult=1.0, nblk=1, blk_d0=0):
    return pl.pallas_call(
        functools.partial(_table_kernel, h0, row_step, sign, d0, blk_d0, lo, hi, mult),
        grid=(nh, nblk, rows // rt),
        in_specs=[pl.BlockSpec(memory_space=pltpu.SMEM)],
        out_specs=pl.BlockSpec((1, 1, rt, cols), lambda h, b, r: (h, b, r, 0)),
        out_shape=jax.ShapeDtypeStruct((nh, nblk, rows, cols), F32),
        compiler_params=_cparams(("arbitrary", "arbitrary", "arbitrary")),
        name="bias_table",
    )(rel_bias)


def _rms(v, g):
    return v * lax.rsqrt(jnp.mean(v * v, axis=-1, keepdims=True) + NORM_EPS) * g


def _proj_kernel(x_ref, gpre_ref, w_ref, qn_ref, wuq_ref, kvn_ref, wk_ref, wv_ref,
                 rc_ref, rsa_ref, rsb_ref, eaug_ref,
                 qa_ref, cmp_ref, slck_ref, slcv_ref, win_ref, qm_ref, km_ref, vm_ref,
                 krb_ref, cq_ref, ckv_ref):
    hb = _rms(x_ref[...], gpre_ref[...]).astype(BF16)
    bounds = ((C_AQ, C_CMP), (C_CMP, C_WIN), (C_WIN, C_KRB), (C_KRB, C_Z))
    chunks = {ab: _dot(hb, w_ref[:, ab[0]:ab[1]]) for ab in bounds}
    lane = lax.broadcasted_iota(jnp.int32, (hb.shape[0], LANES), 1)
    lo = lane < HEAD_DIM

    def proj(c0, c1):
        for (a, b), y in chunks.items():
            if a <= c0 and c1 <= b:
                return y[:, c0 - a:c1 - a]
        raise AssertionError((c0, c1))

    qa_ref[...] = (proj(C_AQ, C_CMP) * (0.125 * LOG2E)).astype(BF16)
    cmp_ref[...] = proj(C_CMP, C_SLC)
    slc = proj(C_SLC, C_WIN)
    slcv_ref[...] = jnp.where(lo, 1.0, slc).astype(BF16)
    slck_ref[...] = jnp.where(lo, slc, eaug_ref[...].astype(F32)).astype(BF16)
    win_ref[...] = proj(C_WIN, C_BCQ).astype(BF16)
    cq_ref[...] = (proj(C_CQ, C_CKV) * (0.125 * LOG2E)).astype(BF16)
    ckv_ref[...] = proj(C_CKV, C_Z).astype(BF16)

    rc, rsa, rsb = rc_ref[...], rsa_ref[...], rsb_ref[...]

    def rope(v):
        return v * rc + pltpu.roll(v, 16, 1) * rsa + pltpu.roll(v, 112, 1) * rsb

    cqn = _rms(proj(C_BCQ, C_BCKV), qn_ref[...]).astype(BF16)
    qm = _dot(cqn, wuq_ref[...])
    for h in range(MLA_HEADS):
        qh = rope(qm[:, LANES * h:LANES * (h + 1)]) * MLA_SCALE2
        qm_ref[:, LANES * h:LANES * (h + 1)] = qh.astype(BF16)

    krb = proj(C_KRB, C_CQ)
    krb_ref[...] = krb
    krr = jnp.where(lo | (lane >= HEAD_DIM + MLA_ROPE), 0.0, rope(krb))

    ckvn = _rms(proj(C_BCKV, C_KRB), kvn_ref[...]).astype(BF16)
    kn = _dot(ckvn, wk_ref[...])
    for h in range(MLA_HEADS):
        km_ref[:, LANES * h:LANES * (h + 1)] = (kn[:, LANES * h:LANES * (h + 1)] + krr).astype(BF16)
    vm = _dot(ckvn, wv_ref[...])
    for h in range(MLA_HEADS):
        vm_ref[:, LANES * h:LANES * (h + 1)] = jnp.where(lo, 1.0, vm[:, LANES * h:LANES * (h + 1)]).astype(BF16)


def _project(x2, gpre, w, qn, wuq, kvn, wk, wv, rope_c, rope_sa, rope_sb, eaug, T, tm):
    BT = x2.shape[0]
    nt = T // tm
    row = lambda i: (i, 0)
    fix = lambda i: (0, 0)
    pos = lambda i: (i % nt, 0)
    widths = [(256, BF16), (128, F32), (128, BF16), (128, BF16), (128, BF16), (512, BF16), (512, BF16),
              (512, BF16), (128, F32), (512, BF16), (128, BF16)]
    return pl.pallas_call(
        _proj_kernel,
        grid=(BT // tm,),
        in_specs=[pl.BlockSpec((tm, D_MODEL), row), pl.BlockSpec((1, D_MODEL), fix),
                  pl.BlockSpec(w.shape, fix), pl.BlockSpec(qn.shape, fix), pl.BlockSpec(wuq.shape, fix),
                  pl.BlockSpec(kvn.shape, fix), pl.BlockSpec(wk.shape, fix), pl.BlockSpec(wv.shape, fix),
                  pl.BlockSpec((tm, LANES), pos), pl.BlockSpec((tm, LANES), pos), pl.BlockSpec((tm, LANES), pos),
                  pl.BlockSpec((tm, LANES), pos)],
        out_specs=[pl.BlockSpec((tm, wd), row) for wd, _ in widths],
        out_shape=[jax.ShapeDtypeStruct((BT, wd), dt) for wd, dt in widths],
        compiler_params=_cparams(("arbitrary",)),
        name="proj",
    )(x2, gpre, w, qn, wuq, kvn, wk, wv, rope_c, rope_sa, rope_sb, eaug)


def _compress_kernel(c_ref, pet_ref, peb_ref, w1t_ref, w1b_ref, w2_ref, o_ref):
    nch = c_ref.shape[1] // CMP_STRIDE
    c = jnp.concatenate([c_ref[0, pl.ds(j, nch, stride=CMP_STRIDE), :] for j in range(CMP_STRIDE)], axis=1)
    top = (c + pet_ref[...]).astype(BF16)
    bot = (c + peb_ref[...]).astype(BF16)
    out = jnp.zeros((nch, LANES), F32)
    for i in range(2):
        pre = _dot(top, w1t_ref[i]) + pltpu.roll(_dot(bot, w1b_ref[i]), nch - 1, 0)
        hid = pre * (1.0 / (1.0 + jnp.exp(-pre)))
        out = out + _dot(hid.astype(BF16), w2_ref[i])
    o_ref[0] = out.astype(BF16)


def _compress(tokens, pet, peb, w1t, w1b, w2):
    B, T, width = tokens.shape
    nch = T // CMP_STRIDE
    full = lambda a: pl.BlockSpec(a.shape, lambda b: (0,) * a.ndim)
    return pl.pallas_call(
        _compress_kernel,
        grid=(B,),
        in_specs=[pl.BlockSpec((1, T, width), lambda b: (b, 0, 0)),
                  full(pet), full(peb), full(w1t), full(w1b), full(w2)],
        out_specs=pl.BlockSpec((1, nch, LANES), lambda b: (b, 0, 0)),
        out_shape=jax.ShapeDtypeStruct((B, nch, LANES), BF16),
        compiler_params=_cparams(("arbitrary",)),
        name="compress",
    )(tokens, pet, peb, w1t, w1b, w2)


def _head_slabs(qblk, n_heads):
    q = qblk.astype(F32)
    lo = lax.broadcasted_iota(jnp.int32, (q.shape[0], LANES), 1) < HEAD_DIM
    out = []
    for j in range(n_heads // 2):
        slab = q[:, LANES * j:LANES * (j + 1)]
        out.append(jnp.where(lo, slab, 0.0))
        out.append(jnp.where(lo, pltpu.roll(slab, HEAD_DIM, 1), 0.0))
    return out


def _merge_upper(accs):
    lo = lax.broadcasted_iota(jnp.int32, accs[0].shape, 1) < HEAD_DIM
    slabs = [jnp.where(lo, pltpu.roll(accs[2 * j], HEAD_DIM, 1), accs[2 * j + 1]) for j in range(len(accs) // 2)]
    return jnp.concatenate(slabs, axis=1)


def _merge_normalized(accs):
    lo = lax.broadcasted_iota(jnp.int32, accs[0].shape, 1) < HEAD_DIM
    slabs = []
    for j in range(len(accs) // 2):
        a, b = accs[2 * j], accs[2 * j + 1]
        out = jnp.where(lo, pltpu.roll(a, HEAD_DIM, 1), b)
        den = jnp.where(lo, a, pltpu.roll(b, HEAD_DIM, 1))
        slabs.append(out / den)
    return jnp.concatenate(slabs, axis=1)


def _row_max_lanes(s):
    return jnp.broadcast_to(jnp.max(s, axis=-1, keepdims=True), (s.shape[0], LANES))


def _sub_lanes(s, ref):
    return jnp.concatenate([s[:, LANES * i:LANES * (i + 1)] - ref for i in range(s.shape[1] // LANES)], axis=1)


def _topk_mask_t(imp, t0):
    ns, tq = imp.shape
    srow = lax.broadcasted_iota(jnp.int32, (ns, tq), 0)
    cur = (t0 + lax.broadcasted_iota(jnp.int32, (ns, tq), 1)) // SEL_BLOCK
    forced = (srow == 0) | (srow == cur) | (srow == cur - 1)
    x = jnp.where(forced, BIG, jnp.where(srow <= cur, imp, -BIG))
    sub = 8
    groups = [x[sub * v:sub * (v + 1)] for v in range(ns // sub)]
    rows_in = [srow[sub * v:sub * (v + 1)] for v in range(ns // sub)]
    cnts = [jnp.zeros((sub, tq), F32) for _ in groups]
    for sp in range(ns):
        other = jnp.broadcast_to(x[sp:sp + 1, :], (sub, tq))
        for v, xv in enumerate(groups):
            if sub * v > sp:
                beats = other >= xv
            elif sub * v + sub - 1 < sp:
                beats = other > xv
            else:
                beats = (other > xv) | ((other == xv) & (rows_in[v] > sp))
            cnts[v] = cnts[v] + jnp.where(beats, 1.0, 0.0)
    cnt = jnp.concatenate(cnts, axis=0)
    sel = (cnt < float(min(SEL_TOP, ns))) & (srow <= cur)
    return jnp.where(sel, 0.0, NEG)


def _cmp_kernel(q_ref, kvc_ref, bc_ref, ovt_ref, ocmp_ref, selb_ref):
    rows = CMP_GROUP * Q_TILE
    ncp, ns = kvc_ref.shape[1], ovt_ref.shape[0]
    step = pl.program_id(1)
    lo_step, nse = 0, SEL_TOP
    while lo_step * rows < ns * SEL_BLOCK:
        nse = min(nse, ns)
        hi_step = nse * SEL_BLOCK // rows
        ncw = min(ncp, -(-nse * (SEL_BLOCK // CMP_STRIDE) // LANES) * LANES)
        if hi_step > lo_step:
            pl.when((step >= lo_step) & (step < hi_step))(
                functools.partial(_cmp_body, ncw, nse, q_ref, kvc_ref, bc_ref, ovt_ref, ocmp_ref, selb_ref))
            lo_step = hi_step
        nse += SEL_TOP


def _cmp_body(ncw, ns, q_ref, kvc_ref, bc_ref, ovt_ref, ocmp_ref, selb_ref):
    tq = Q_TILE
    kvc = kvc_ref[0, :ncw, :]
    ovt = ovt_ref[:ns, :ncw]
    groups = range(CMP_GROUP)
    sls = [slice(g * tq, (g + 1) * tq) for g in groups]
    s_alls = []
    for g in groups:
        qs = _head_slabs(q_ref[0, sls[g], :], NSA_HEADS)
        s_alls.append(_dot_nt(jnp.concatenate([q.astype(BF16) for q in qs], axis=0), kvc))
    psums, o_alls = [], []
    for g in groups:
        psum = None
        ps = []
        for h in range(NSA_HEADS):
            s = s_alls[g][h * tq:(h + 1) * tq] + bc_ref[h, sls[g], :ncw]
            m = jnp.max(s, axis=-1, keepdims=True)
            e = jnp.exp2(s - m)
            l = jnp.sum(e, axis=-1, keepdims=True)
            p = e * jnp.where(m > 0.5 * NEG, 1.0 / l, 0.0)
            psum = p if psum is None else psum + p
            ps.append(p.astype(BF16))
        psums.append(psum)
        o_alls.append(_dot(jnp.concatenate(ps, axis=0), kvc))
    imps = []
    for g in groups:
        hi = psums[g].astype(BF16)
        lo = (psums[g] - hi.astype(F32)).astype(BF16)
        imps.append(_dot_nt(ovt, hi) + _dot_nt(ovt, lo))
    for g in groups:
        ocmp = _merge_upper([o_alls[g][h * tq:(h + 1) * tq] for h in range(NSA_HEADS)])
        ocmp_ref[0, sls[g], :] = ocmp.astype(ocmp_ref.dtype)
        sb = _topk_mask_t(imps[g], (pl.program_id(1) * CMP_GROUP + g) * tq)
        if ns < HEAD_DIM:
            sb = jnp.concatenate([sb, jnp.full((HEAD_DIM - ns, tq), NEG, F32)], axis=0)
        full = jnp.concatenate([jnp.zeros((HEAD_DIM, tq), F32), sb], axis=0)
        selb_ref[0, sls[g], :] = full.T.astype(BF16)


def _cmp_attention(qa, kvc, bc, ovt, T):
    B = qa.shape[0]
    ncp = kvc.shape[1]
    rows = CMP_GROUP * Q_TILE
    return pl.pallas_call(
        _cmp_kernel,
        grid=(B, T // rows),
        in_specs=[pl.BlockSpec((1, rows, 256), lambda b, n: (b, n, 0)),
                  pl.BlockSpec((1, ncp, LANES), lambda b, n: (b, 0, 0)),
                  pl.BlockSpec((NSA_HEADS, rows, ncp), lambda b, n: (0, n, 0)),
                  pl.BlockSpec(ovt.shape, lambda b, n: (0, 0))],
        out_specs=[pl.BlockSpec((1, rows, 256), lambda b, n: (b, n, 0)),
                   pl.BlockSpec((1, rows, LANES), lambda b, n: (b, n, 0))],
        out_shape=[jax.ShapeDtypeStruct((B, T, 256), BF16), jax.ShapeDtypeStruct((B, T, LANES), BF16)],
        compiler_params=_cparams(("arbitrary", "arbitrary")),
        name="cmp_select",
    )(qa, kvc, bc, ovt)


def _softmax_tile(s2, rows, m_s, shift=None):
    m_t = _row_max_lanes(s2)
    if shift is not None:
        m_t = m_t + shift
    m_old = m_s[rows]
    m_new = jnp.maximum(m_old, m_t)
    p = jnp.exp2(_sub_lanes(s2, m_new if shift is None else m_new - shift))
    alpha = jnp.exp2(m_old - m_new)
    m_s[rows] = m_new
    return p.astype(BF16), alpha


def _slc_kernel(b31_ref, q_ref, selb_ref, ka_ref, kv_ref, tab_ref, o_ref, m_s, acc_s):
    tq = q_ref.shape[1]
    H = NSA_HEADS
    n = pl.program_id(1)
    qs = _head_slabs(q_ref[0], H)
    selb = selb_ref[0].astype(F32)
    qst = jnp.concatenate([(qs[h] + selb).astype(BF16) for h in range(H)], axis=0)
    m_s[...] = jnp.full(m_s.shape, -jnp.inf, F32)
    acc_s[...] = jnp.zeros(acc_s.shape, F32)

    def scores(k0, width=tq):
        k0 = pl.multiple_of(k0, tq)
        return _dot_nt(qst, ka_ref[0, pl.ds(k0, width), :]), kv_ref[0, pl.ds(k0, width), :]

    def update(s_all, kv):
        ps, alphas = [], []
        for h in range(H):
            rows = slice(h * tq, (h + 1) * tq)
            p, alpha = _softmax_tile(s_all[rows], rows, m_s, shift=b31_ref[h])
            ps.append(p)
            alphas.append(alpha)
        pv = _dot(jnp.concatenate(ps, axis=0), kv)
        for h in range(H):
            rows = slice(h * tq, (h + 1) * tq)
            acc_s[rows] = alphas[h] * acc_s[rows] + pv[rows]

    def far_triple(i, carry):
        update(*scores(3 * i * tq, 3 * tq))
        return carry

    n_far = jnp.maximum(n - 1, 0)
    lax.fori_loop(0, n_far // 3, far_triple, 0)

    @pl.when(n_far % 3 == 1)
    def _():
        update(*scores((n_far - 1) * tq))

    @pl.when(n_far % 3 == 2)
    def _():
        update(*scores((n_far - 2) * tq, 2 * tq))

    def near_diagonal(k0, before):
        k0 = pl.multiple_of(k0, tq)
        half = tq // SLC_DIAG_PARTS
        parts = []
        for part in range(SLC_DIAG_PARTS):
            kw = before + half * (part + 1)
            q_part = jnp.concatenate([qst[h * tq + part * half:h * tq + (part + 1) * half] for h in range(H)], axis=0)
            parts.append((_dot_nt(q_part, ka_ref[0, pl.ds(k0, kw), :]), kv_ref[0, pl.ds(k0, kw), :], kw))
        for part, (s_all, kv, kw) in enumerate(parts):
            ps, alphas = [], []
            for h in range(H):
                rows = slice(h * tq + part * half, h * tq + (part + 1) * half)
                bias = tab_ref[h, part * half:(part + 1) * half, tq - before:tq - before + kw]
                p, alpha = _softmax_tile(s_all[h * half:(h + 1) * half] + bias, rows, m_s)
                ps.append(p)
                alphas.append(alpha)
            pv = _dot(jnp.concatenate(ps, axis=0), kv)
            for h in range(H):
                rows = slice(h * tq + part * half, h * tq + (part + 1) * half)
                acc_s[rows] = alphas[h] * acc_s[rows] + pv[h * half:(h + 1) * half]

    @pl.when(n >= 1)
    def _():
        near_diagonal(jnp.maximum(n - 1, 0) * tq, tq)

    @pl.when(n == 0)
    def _():
        near_diagonal(0, 0)

    o_ref[0] = _merge_normalized([acc_s[h * tq:(h + 1) * tq] for h in range(H)]).astype(o_ref.dtype)


def _slc_attention(b31, qa, selb, slck, slcv, tab, T):
    B = qa.shape[0]
    tq = FLASH_TILE
    H = NSA_HEADS
    tile = lambda w: pl.BlockSpec((1, tq, w), lambda b, n: (b, n, 0))
    return pl.pallas_call(
        _slc_kernel,
        grid=(B, T // tq),
        in_specs=[pl.BlockSpec(memory_space=pltpu.SMEM),
                  tile(256), tile(LANES),
                  pl.BlockSpec((1, T, LANES), lambda b, n: (b, 0, 0)),
                  pl.BlockSpec((1, T, LANES), lambda b, n: (b, 0, 0)),
                  pl.BlockSpec(tab.shape, lambda b, n: (0, 0, 0))],
        out_specs=tile(256),
        out_shape=jax.ShapeDtypeStruct((B, T, 256), BF16),
        scratch_shapes=[pltpu.VMEM((H * tq, LANES), F32), pltpu.VMEM((H * tq, LANES), F32)],
        compiler_params=_cparams(("arbitrary", "arbitrary")),
        name="slc_attention",
    )(b31, qa, selb, slck, slcv, tab)


def _band_kernel(n_heads, window, has_sink, n_sub, tq, *refs):
    if has_sink:
        q_ref, kv_ref, tab_ref, sink_ref, o_ref = refs
    else:
        q_ref, kv_ref, tab_ref, o_ref = refs
    span = window + tq
    lo_kv = lax.broadcasted_iota(jnp.int32, (span, LANES), 1) < HEAD_DIM
    lo_o = lax.broadcasted_iota(jnp.int32, (n_heads * tq, LANES), 1) < HEAD_DIM
    groups = range(n_sub)
    tiles = [pl.program_id(1) * n_sub + g for g in groups]
    kvs, dots = [], []
    for g in groups:
        k0 = pl.multiple_of(jnp.maximum(tiles[g] * tq - window, 0), tq)
        kvs.append(kv_ref[0, pl.ds(k0, span), :])
        qs = _head_slabs(q_ref[0, g * tq:(g + 1) * tq, :], n_heads)
        dots.append(_dot_nt(jnp.concatenate([q.astype(BF16) for q in qs], axis=0), kvs[g]))
    accs, ms = [], []
    for g in groups:
        s = dots[g] + tab_ref[jnp.minimum(tiles[g], window // tq)]
        m = _row_max_lanes(s)
        if has_sink:
            m = jnp.maximum(m, sink_ref[...])
        e = jnp.exp2(_sub_lanes(s, m))
        ms.append(m)
        accs.append(_dot(e.astype(BF16), jnp.where(lo_kv, 1.0, kvs[g]).astype(BF16)))
    for g in groups:
        acc = accs[g]
        if has_sink:
            acc = acc + jnp.where(lo_o, jnp.exp2(sink_ref[...] - ms[g]), 0.0)
        out = _merge_normalized([acc[h * tq:(h + 1) * tq] for h in range(n_heads)])
        o_ref[0, g * tq:(g + 1) * tq, :] = out.astype(o_ref.dtype)


def _band_attention(q, kv_pad, tab, sinks, n_heads, window, T, out_dtype):
    B = q.shape[0]
    width = n_heads * HEAD_DIM
    has_sink = sinks is not None
    nvar, tq = tab.shape[1], tab.shape[2]
    tab2 = jnp.transpose(tab, (1, 0, 2, 3)).reshape(nvar, n_heads * tq, window + tq)
    n_sub = (BAND_ROWS if has_sink else 2 * BAND_ROWS) // (n_heads * tq)
    rows = n_sub * tq
    in_specs = [pl.BlockSpec((1, rows, width), lambda b, n: (b, n, 0)),
                pl.BlockSpec((1, T, LANES), lambda b, n: (b, 0, 0)),
                pl.BlockSpec(tab2.shape, lambda b, n: (0, 0, 0))]
    args = (q, kv_pad, tab2)
    if has_sink:
        rep = jnp.broadcast_to(jnp.repeat(sinks, tq)[:, None], (n_heads * tq, LANES))
        in_specs.append(pl.BlockSpec(rep.shape, lambda b, n: (0, 0)))
        args = args + (rep,)
    return pl.pallas_call(
        functools.partial(_band_kernel, n_heads, window, has_sink, n_sub, tq),
        grid=(B, T // rows),
        in_specs=in_specs,
        out_specs=pl.BlockSpec((1, rows, width), lambda b, n: (b, n, 0)),
        out_shape=jax.ShapeDtypeStruct((B, T, width), out_dtype),
        compiler_params=_cparams(("arbitrary", "arbitrary")),
        name="band_sink" if has_sink else "band_window",
    )(*args)


def _mla_kernel(q_ref, k_ref, v_ref, o_ref, m_s, acc_s):
    tq = q_ref.shape[1]
    H = MLA_HEADS
    n = pl.program_id(1)
    m_s[...] = jnp.full(m_s.shape, -jnp.inf, F32)
    acc_s[...] = jnp.zeros(acc_s.shape, F32)

    def scores(k0, width=tq):
        k0 = pl.multiple_of(k0, tq)
        return k0, width, [_dot_nt(q_ref[0, :, LANES * h:LANES * (h + 1)],
                                   k_ref[0, pl.ds(k0, width), LANES * h:LANES * (h + 1)]) for h in range(H)]

    def update(k0, width, ss):
        ps, alphas = [], []
        for h in range(H):
            p, alpha = _softmax_tile(ss[h], slice(h * tq, (h + 1) * tq), m_s)
            ps.append(p)
            alphas.append(alpha)
        for h in range(H):
            rows = slice(h * tq, (h + 1) * tq)
            v = v_ref[0, pl.ds(k0, width), LANES * h:LANES * (h + 1)]
            acc_s[rows] = alphas[h] * acc_s[rows] + _dot(ps[h], v)

    def far_triple(i, carry):
        update(*scores(3 * i * tq, 3 * tq))
        return carry

    n_far = jnp.maximum(n - 1, 0)
    lax.fori_loop(0, n_far // 3, far_triple, 0)

    @pl.when(n_far % 3 == 2)
    def _():
        update(*scores((n_far - 2) * tq, 2 * tq))

    def near_diagonal(k0, before):
        k0 = pl.multiple_of(k0, tq)
        half = tq // MLA_DIAG_PARTS
        parts = []
        for part in range(MLA_DIAG_PARTS):
            kw = before + half * (part + 1)
            r = slice(part * half, (part + 1) * half)
            parts.append((kw, [_dot_nt(q_ref[0, r, LANES * h:LANES * (h + 1)],
                                       k_ref[0, pl.ds(k0, kw), LANES * h:LANES * (h + 1)]) for h in range(H)]))
        for part, (kw, ss) in enumerate(parts):
            row = lax.broadcasted_iota(jnp.int32, (half, kw), 0)
            col = lax.broadcasted_iota(jnp.int32, (half, kw), 1)
            causal = col <= row + (before + part * half)
            ps, alphas = [], []
            for h in range(H):
                rows = slice(h * tq + part * half, h * tq + (part + 1) * half)
                p, alpha = _softmax_tile(jnp.where(causal, ss[h], NEG), rows, m_s)
                ps.append(p)
                alphas.append(alpha)
            for h in range(H):
                rows = slice(h * tq + part * half, h * tq + (part + 1) * half)
                v = v_ref[0, pl.ds(k0, kw), LANES * h:LANES * (h + 1)]
                acc_s[rows] = alphas[h] * acc_s[rows] + _dot(ps[h], v)

    @pl.when((n >= 1) & (n_far % 3 != 1))
    def _():
        near_diagonal(jnp.maximum(n - 1, 0) * tq, tq)

    @pl.when(n_far % 3 == 1)
    def _():
        near_diagonal(jnp.maximum(n - 2, 0) * tq, 2 * tq)

    @pl.when(n == 0)
    def _():
        near_diagonal(0, 0)

    o_ref[0] = _merge_normalized([acc_s[h * tq:(h + 1) * tq] for h in range(H)]).astype(o_ref.dtype)


def _mla_attention(qm, km, vm, T):
    B = qm.shape[0]
    tq = FLASH_TILE
    H = MLA_HEADS
    return pl.pallas_call(
        _mla_kernel,
        grid=(B, T // tq),
        in_specs=[pl.BlockSpec((1, tq, 512), lambda b, n: (b, n, 0)),
                  pl.BlockSpec((1, T, 512), lambda b, n: (b, 0, 0)),
                  pl.BlockSpec((1, T, 512), lambda b, n: (b, 0, 0))],
        out_specs=pl.BlockSpec((1, tq, 256), lambda b, n: (b, n, 0)),
        out_shape=jax.ShapeDtypeStruct((B, T, 256), BF16),
        scratch_shapes=[pltpu.VMEM((H * tq, LANES), F32), pltpu.VMEM((H * tq, LANES), F32)],
        compiler_params=_cparams(("arbitrary", "arbitrary")),
        name="mla_attention",
    )(qm, km, vm)


def _out_kernel(x_ref, gpre_ref, wz_ref, ocmp_ref, oslc_ref, owin_ref, krb_ref, gexp_ref, ob_ref, oc_ref, w_ref,
                gpost_ref, o_ref):
    sub = x_ref.shape[0] // OUT_SUBTILES
    rs = [slice(t * sub, (t + 1) * sub) for t in range(OUT_SUBTILES)]
    w_a = NSA_HEADS * HEAD_DIM
    hs = [_rms(x_ref[r, :], gpre_ref[...]).astype(BF16) for r in rs]
    zs = [_dot(h, wz_ref[...]) for h in hs]
    mixeds = []
    for r, z in zip(rs, zs):
        sig = 1.0 / (1.0 + jnp.exp(-krb_ref[r, :]))
        hi = sig.astype(BF16)
        lo = (sig - hi.astype(F32)).astype(BF16)
        gates = _dot(jnp.concatenate([hi, lo], axis=1), gexp_ref[...])
        oa = None
        for j, br in enumerate((ocmp_ref, oslc_ref, owin_ref)):
            term = gates[:, w_a * j:w_a * (j + 1)] * br[r, :].astype(F32)
            oa = term if oa is None else oa + term
        heads = jnp.concatenate([oa, ob_ref[r, :].astype(F32), oc_ref[r, :].astype(F32)], axis=1)
        mixeds.append((heads * (z * (1.0 / (1.0 + jnp.exp(-z))))).astype(BF16))
    ys = [_dot(m, w_ref[...]) for m in mixeds]
    for r, y in zip(rs, ys):
        o_ref[r, :] = x_ref[r, :] + _rms(y, gpost_ref[...])


def _out_project(x2, gpre, wz, ocmp, oslc, owin, krb, gexp, ob, oc, w, gpost, tm):
    BT = x2.shape[0]
    row = lambda i: (i, 0)
    fix = lambda i: (0, 0)
    spec = lambda a: pl.BlockSpec((tm, a.shape[1]), row)
    full = lambda a: pl.BlockSpec(a.shape, fix)
    return pl.pallas_call(
        _out_kernel,
        grid=(BT // tm,),
        in_specs=[spec(x2), full(gpre), full(wz), spec(ocmp), spec(oslc), spec(owin), spec(krb), full(gexp),
                  spec(ob), spec(oc), full(w), full(gpost)],
        out_specs=pl.BlockSpec((tm, D_MODEL), row),
        out_shape=jax.ShapeDtypeStruct((BT, D_MODEL), F32),
        compiler_params=_cparams(("arbitrary",)),
        name="out_proj",
    )(x2, gpre, wz, ocmp, oslc, owin, krb, gexp, ob, oc, w, gpost)


def _split_in(w):
    outs, o = [], 0
    for s in IN_SIZES:
        outs.append(w[..., o:o + s])
        o += s
    return outs


def _relayout_kernel(w_ref, wp_ref, wz_ref):
    a_q, a_kv, a_g, b_cq, b_ckv, b_kr, c_q, c_kv, z = _split_in(w_ref[0])
    zeros = lambda n: jnp.zeros((a_q.shape[0], n), F32)
    krb = jnp.concatenate([zeros(HEAD_DIM), b_kr, a_g, zeros(LANES - HEAD_DIM - MLA_ROPE - 3 * NSA_HEADS)], axis=-1)
    wp_ref[0] = jnp.concatenate([a_q, a_kv, b_cq, b_ckv, krb, c_q, c_kv], axis=-1).astype(BF16)
    wz_ref[0] = z.astype(BF16)


def _prep_w_in(w):
    L, d, n = w.shape
    rt = 256
    return pl.pallas_call(
        _relayout_kernel,
        grid=(L, d // rt),
        in_specs=[pl.BlockSpec((1, rt, n), lambda l, r: (l, r, 0))],
        out_specs=[pl.BlockSpec((1, rt, C_Z), lambda l, r: (l, r, 0)),
                   pl.BlockSpec((1, rt, IN_SIZES[-1]), lambda l, r: (l, r, 0))],
        out_shape=[jax.ShapeDtypeStruct((L, d, C_Z), BF16), jax.ShapeDtypeStruct((L, d, IN_SIZES[-1]), BF16)],
        compiler_params=_cparams(("arbitrary", "arbitrary")),
        name="w_in_relayout",
    )(w)


def _prep_mla(w_uq, w_ukv):
    L = w_uq.shape[0]
    dq = MLA_NOPE + MLA_ROPE
    uq = w_uq.reshape(L, MLA_Q_RANK, MLA_HEADS, dq)
    uq = jnp.concatenate([uq, jnp.zeros((L, MLA_Q_RANK, MLA_HEADS, LANES - dq), uq.dtype)], axis=-1)
    ukv = w_ukv.reshape(L, MLA_KV_RANK, MLA_HEADS, MLA_NOPE + MLA_V)
    uk = jnp.concatenate([ukv[..., :MLA_NOPE], jnp.zeros((L, MLA_KV_RANK, MLA_HEADS, LANES - MLA_NOPE), ukv.dtype)], axis=-1)
    uv = jnp.concatenate([jnp.zeros((L, MLA_KV_RANK, MLA_HEADS, LANES - MLA_V), ukv.dtype), ukv[..., MLA_NOPE:]], axis=-1)
    return (uq.reshape(L, MLA_Q_RANK, MLA_HEADS * LANES).astype(BF16),
            uk.reshape(L, MLA_KV_RANK, MLA_HEADS * LANES).astype(BF16),
            uv.reshape(L, MLA_KV_RANK, MLA_HEADS * LANES).astype(BF16))


def _prep_compress(cmp_pos, cmp_w1, cmp_w2):
    L = cmp_pos.shape[0]
    pe = jnp.concatenate([cmp_pos[:, 0], cmp_pos[:, 1]], axis=-1)
    pet = pe[:, :CMP_STRIDE].reshape(L, 1, CMP_STRIDE * LANES)
    peb = pe[:, CMP_STRIDE:].reshape(L, 1, CMP_STRIDE * LANES)
    w1 = cmp_w1.reshape(L, 2, CMP_LEN, HEAD_DIM, CMP_HIDDEN)
    zero = jnp.zeros_like(w1[:, 0])
    w1k = jnp.concatenate([w1[:, 0], zero], axis=2)
    w1v = jnp.concatenate([zero, w1[:, 1]], axis=2)
    w1e = jnp.stack([w1k, w1v], axis=1)
    w1t = w1e[:, :, :CMP_STRIDE].reshape(L, 2, CMP_STRIDE * LANES, CMP_HIDDEN).astype(BF16)
    w1b = w1e[:, :, CMP_STRIDE:].reshape(L, 2, CMP_STRIDE * LANES, CMP_HIDDEN).astype(BF16)
    z2 = jnp.zeros_like(cmp_w2[:, 0])
    w2 = jnp.stack([jnp.concatenate([cmp_w2[:, 0], z2], axis=-1),
                    jnp.concatenate([z2, cmp_w2[:, 1]], axis=-1)], axis=1).astype(BF16)
    return pet, peb, w1t, w1b, w2


def _rope_tables(T):
    half = MLA_ROPE // 2
    inv = ROPE_THETA ** (-jnp.arange(half, dtype=F32) / half)
    ang = jnp.arange(T).astype(F32)[:, None] * inv[None, :]
    cos, sin = jnp.cos(ang), jnp.sin(ang)
    z = lambda n: jnp.zeros((T, n), F32)
    rc = jnp.concatenate([jnp.ones((T, MLA_NOPE), F32), cos, cos, z(LANES - MLA_NOPE - MLA_ROPE)], axis=1)
    rsa = jnp.concatenate([z(MLA_NOPE + half), sin, z(LANES - MLA_NOPE - MLA_ROPE)], axis=1)
    rsb = jnp.concatenate([z(MLA_NOPE), -sin, z(LANES - MLA_NOPE - half)], axis=1)
    return rc, rsa, rsb


def _static_tables(T):
    ns = T // SEL_BLOCK
    nc = T // CMP_STRIDE - 1
    ncp = T // CMP_STRIDE
    sstart = np.arange(ns) * SEL_BLOCK
    cstart = np.arange(nc) * CMP_STRIDE
    overlap = (np.clip(np.minimum(cstart[:, None] + CMP_LEN, sstart[None, :] + SEL_BLOCK)
                       - np.maximum(cstart[:, None], sstart[None, :]), 0, None) / CMP_STRIDE).astype(np.float32)
    ovt = np.zeros((ns, ncp), np.float32)
    ovt[:, :nc] = overlap.T
    eaug = np.zeros((T, LANES), np.float32)
    eaug[np.arange(T), HEAD_DIM + np.arange(T) // SEL_BLOCK] = 1.0
    gexp = np.zeros((2, LANES, 3 * NSA_HEADS * HEAD_DIM), np.float32)
    for h in range(NSA_HEADS):
        for j in range(3):
            c0 = NSA_HEADS * HEAD_DIM * j + HEAD_DIM * h
            gexp[:, GATE_LANE0 + 3 * h + j, c0:c0 + HEAD_DIM] = 1.0
    gexp = gexp.reshape(2 * LANES, -1)
    return jnp.asarray(ovt, BF16), jnp.asarray(eaug, BF16), jnp.asarray(gexp, BF16)


def kernel(x, w_in, w_out, norm_pre, norm_post, cmp_pos, cmp_w1, cmp_w2, mla_q_norm, mla_w_uq, mla_kv_norm,
           mla_w_ukv, swa_sinks, rel_bias):
    B, T, D = x.shape
    depth = w_in.shape[0]
    assert D == D_MODEL and T % (2 * FLASH_TILE) == 0 and T // SEL_BLOCK <= HEAD_DIM
    ncp = T // CMP_STRIDE

    w_in_p, w_z = _prep_w_in(w_in)
    wuq, wk, wv = _prep_mla(mla_w_uq, mla_w_ukv)
    pet, peb, w1t, w1b, w2 = _prep_compress(cmp_pos, cmp_w1, cmp_w2)
    w_out_b = w_out.astype(BF16)
    rope_c, rope_sa, rope_sb = _rope_tables(T)
    ovt, eaug, gexp = _static_tables(T)

    bc = _bias_table(rel_bias, 0, NSA_HEADS, ncp, T, CMP_STRIDE, 1, -(CMP_LEN - 1), 0, 1 << 30, 32, mult=LOG2E)
    bc = jnp.transpose(bc.reshape(NSA_HEADS, ncp, T), (0, 2, 1))
    tab_near = _bias_table(rel_bias, 0, NSA_HEADS, FLASH_TILE, 2 * FLASH_TILE, 1, -1, FLASH_TILE, 0, 1 << 30, 256,
                           mult=LOG2E).reshape(NSA_HEADS, FLASH_TILE, 2 * FLASH_TILE)
    tab_win = _bias_table(rel_bias, 0, NSA_HEADS, WIN_TILE, NSA_WINDOW + WIN_TILE, 1, -1, 0, 0, NSA_WINDOW, WIN_TILE,
                          mult=LOG2E, nblk=NSA_WINDOW // WIN_TILE + 1, blk_d0=WIN_TILE)
    tab_swa = _bias_table(rel_bias, NSA_HEADS, SWA_HEADS, Q_TILE, SWA_WINDOW + Q_TILE, 1, -1, 0, 0, SWA_WINDOW, Q_TILE,
                          mult=LOG2E, nblk=SWA_WINDOW // Q_TILE + 1, blk_d0=Q_TILE)
    b31 = rel_bias[:NSA_HEADS, REL_BUCKETS - 1] * LOG2E

    x2 = x.reshape(B * T, D)
    for l in range(depth):
        (qa, cmpkv, slck, slcv, win, qm, km, vm, krb, cq, ckv) = _project(
            x2, norm_pre[l][None], w_in_p[l], mla_q_norm[l][None], wuq[l], mla_kv_norm[l][None], wk[l], wv[l],
            rope_c, rope_sa, rope_sb, eaug, T, PROJ_ROWS)
        r3 = lambda a: a.reshape(B, T, a.shape[-1])
        kvc = _compress(r3(cmpkv), pet[l], peb[l], w1t[l], w1b[l], w2[l])
        qa3 = r3(qa)
        ocmp, selb = _cmp_attention(qa3, kvc, bc, ovt, T)
        owin = _band_attention(qa3, r3(win), tab_win, None, NSA_HEADS, NSA_WINDOW, T, BF16)
        oslc = _slc_attention(b31, qa3, selb, r3(slck), r3(slcv), tab_near, T)
        oc = _band_attention(r3(cq), r3(ckv), tab_swa, swa_sinks[l] * LOG2E, SWA_HEADS, SWA_WINDOW, T, BF16)
        ob = _mla_attention(r3(qm), r3(km), r3(vm), T)
        flat = lambda a: a.reshape(B * T, a.shape[-1])
        x2 = _out_project(x2, norm_pre[l][None], w_z[l], flat(ocmp), flat(oslc), flat(owin), krb, gexp,
                          flat(ob), flat(oc), w_out_b[l], norm_post[l][None], OUT_ROWS)
    return x2.reshape(B, T, D)
```

```python
import functools
import math

import numpy as np
import jax
import jax.numpy as jnp
from jax import lax
from jax.experimental import pallas as pl
from jax.experimental.pallas import tpu as pltpu

F32 = jnp.float32
BF16 = jnp.bfloat16

D_MODEL = 1024
HEAD_DIM = 64
NSA_HEADS = 4
CMP_LEN = 32
CMP_STRIDE = 16
CMP_HIDDEN = 128
SEL_BLOCK = 64
SEL_TOP = 16
NSA_WINDOW = 512
MLA_HEADS = 4
MLA_Q_RANK = 256
MLA_KV_RANK = 128
MLA_NOPE = 64
MLA_ROPE = 32
MLA_V = 64
ROPE_THETA = 10000.0
SWA_HEADS = 8
SWA_WINDOW = 128
REL_BUCKETS = 32
REL_MAX_DIST = 512
NORM_EPS = 1e-6
NEG = -1e30
BIG = 1e9
LOG2E = math.log2(math.e)
MLA_SCALE2 = (MLA_NOPE + MLA_ROPE) ** -0.5 * LOG2E
IN_SIZES = (256, 384, 12, 256, 128, 32, 512, 128, 1024)

LANES = 128
Q_TILE = 128
WIN_TILE = 128
BAND_ROWS = 4096
PROJ_ROWS = 1024
OUT_SUBTILES = 4
OUT_ROWS = 1024
CMP_GROUP = 4
FLASH_TILE = 512
SLC_DIAG_PARTS = 4
MLA_DIAG_PARTS = 4
GATE_LANE0 = 96
VMEM_LIMIT = 56 * 1024 * 1024

C_AQ, C_CMP, C_SLC, C_WIN, C_BCQ, C_BCKV, C_KRB, C_CQ, C_CKV, C_Z = (
    0, 256, 384, 512, 640, 896, 1024, 1152, 1664, 1792)


def _dot(a, b):
    return jnp.dot(a, b, preferred_element_type=F32)


def _dot_nt(a, b):
    return lax.dot_general(a, b, (((1,), (1,)), ((), ())), preferred_element_type=F32)


def _bucket_thresholds():
    d = np.arange(0, 4 * REL_MAX_DIST)
    exact = REL_BUCKETS // 2
    large = exact + (np.log(np.maximum(d, 1).astype(np.float32) / np.float32(exact))
                     / np.float32(math.log(REL_MAX_DIST / exact)) * np.float32(REL_BUCKETS - exact)).astype(np.int32)
    b = np.where(d < exact, d, np.minimum(large, REL_BUCKETS - 1))
    assert np.all(np.diff(b) >= 0) and b[-1] == REL_BUCKETS - 1
    return [int(np.argmax(b >= k)) for k in range(REL_BUCKETS)]


_THRESH = _bucket_thresholds()


def _cparams(sem):
    return pltpu.CompilerParams(dimension_semantics=sem, vmem_limit_bytes=VMEM_LIMIT)


def _table_kernel(h0, row_step, sign, d0, blk_d0, lo, hi, mult, rb_ref, out_ref):
    h = pl.program_id(0) + h0
    blk = pl.program_id(1)
    rt, cc = out_ref.shape[2], out_ref.shape[3]
    base = d0 + blk * blk_d0 - sign * row_step * rt * pl.program_id(2)
    width = -(-(row_step * rt + cc) // LANES) * LANES
    w = lax.broadcasted_iota(jnp.int32, (8, width), 1)
    dist = base + sign * jnp.where(w < cc, w, w - width)
    acc = jnp.full(dist.shape, rb_ref[h, 0], F32)
    for k in range(1, REL_BUCKETS):
        acc = jnp.where(dist >= _THRESH[k], rb_ref[h, k], acc)
    row = jnp.where((dist >= lo) & (dist < hi), acc * mult, NEG)
    rolled = pltpu.roll(jnp.broadcast_to(row[:1], (rt, width)), 0, 1, stride=row_step, stride_axis=0)
    out_ref[0, 0] = rolled[:, :cc]


def _bias_table(rel_bias, h0, nh, rows, cols, row_step, sign, d0, lo, hi, rt, mult=1.0, nblk=1, blk_d0=0):
    return pl.pallas_call(
        functools.partial(_table_kernel, h0, row_step, sign, d0, blk_d0, lo, hi, mult),
        grid=(nh, nblk, rows // rt),
        in_specs=[pl.BlockSpec(memory_space=pltpu.SMEM)],
        out_specs=pl.BlockSpec((1, 1, rt, cols), lambda h, b, r: (h, b, r, 0)),
        out_shape=jax.ShapeDtypeStruct((nh, nblk, rows, cols), F32),
        compiler_params=_cparams(("arbitrary", "arbitrary", "arbitrary")),
        name="bias_table",
    )(rel_bias)


def _rms(v, g):
    return v * lax.rsqrt(jnp.mean(v * v, axis=-1, keepdims=True) + NORM_EPS) * g


def _proj_kernel(x_ref, gpre_ref, w_ref, qn_ref, wuq_ref, kvn_ref, wk_ref, wv_ref,
                 rc_ref, rsa_ref, rsb_ref, eaug_ref,
                 qa_ref, cmp_ref, slck_ref, slcv_ref, win_ref, qm_ref, km_ref, vm_ref,
                 krb_ref, cq_ref, ckv_ref):
    hb = _rms(x_ref[...], gpre_ref[...]).astype(BF16)
    bounds = ((C_AQ, C_CMP), (C_CMP, C_WIN), (C_WIN, C_KRB), (C_KRB, C_Z))
    chunks = {ab: _dot(hb, w_ref[:, ab[0]:ab[1]]) for ab in bounds}
    lane = lax.broadcasted_iota(jnp.int32, (hb.shape[0], LANES), 1)
    lo = lane < HEAD_DIM

    def proj(c0, c1):
        for (a, b), y in chunks.items():
            if a <= c0 and c1 <= b:
                return y[:, c0 - a:c1 - a]
        raise AssertionError((c0, c1))

    qa_ref[...] = (proj(C_AQ, C_CMP) * (0.125 * LOG2E)).astype(BF16)
    cmp_ref[...] = proj(C_CMP, C_SLC)
    slc = proj(C_SLC, C_WIN)
    slcv_ref[...] = jnp.where(lo, 1.0, slc).astype(BF16)
    slck_ref[...] = jnp.where(lo, slc, eaug_ref[...].astype(F32)).astype(BF16)
    win_ref[...] = proj(C_WIN, C_BCQ).astype(BF16)
    cq_ref[...] = (proj(C_CQ, C_CKV) * (0.125 * LOG2E)).astype(BF16)
    ckv_ref[...] = proj(C_CKV, C_Z).astype(BF16)

    rc, rsa, rsb = rc_ref[...], rsa_ref[...], rsb_ref[...]

    def rope(v):
        return v * rc + pltpu.roll(v, 16, 1) * rsa + pltpu.roll(v, 112, 1) * rsb

    cqn = _rms(proj(C_BCQ, C_BCKV), qn_ref[...]).astype(BF16)
    qm = _dot(cqn, wuq_ref[...])
    for h in range(MLA_HEADS):
        qh = rope(qm[:, LANES * h:LANES * (h + 1)]) * MLA_SCALE2
        qm_ref[:, LANES * h:LANES * (h + 1)] = qh.astype(BF16)

    krb = proj(C_KRB, C_CQ)
    krb_ref[...] = krb
    krr = jnp.where(lo | (lane >= HEAD_DIM + MLA_ROPE), 0.0, rope(krb))

    ckvn = _rms(proj(C_BCKV, C_KRB), kvn_ref[...]).astype(BF16)
    kn = _dot(ckvn, wk_ref[...])
    for h in range(MLA_HEADS):
        km_ref[:, LANES * h:LANES * (h + 1)] = (kn[:, LANES * h:LANES * (h + 1)] + krr).astype(BF16)
    vm = _dot(ckvn, wv_ref[...])
    for h in range(MLA_HEADS):
        vm_ref[:, LANES * h:LANES * (h + 1)] = jnp.where(lo, 1.0, vm[:, LANES * h:LANES * (h + 1)]).astype(BF16)


def _project(x2, gpre, w, qn, wuq, kvn, wk, wv, rope_c, rope_sa, rope_sb, eaug, T, tm):
    BT = x2.shape[0]
    nt = T // tm
    row = lambda i: (i, 0)
    fix = lambda i: (0, 0)
    pos = lambda i: (i % nt, 0)
    widths = [(256, BF16), (128, F32), (128, BF16), (128, BF16), (128, BF16), (512, BF16), (512, BF16),
              (512, BF16), (128, F32), (512, BF16), (128, BF16)]
    return pl.pallas_call(
        _proj_kernel,
        grid=(BT // tm,),
        in_specs=[pl.BlockSpec((tm, D_MODEL), row), pl.BlockSpec((1, D_MODEL), fix),
                  pl.BlockSpec(w.shape, fix), pl.BlockSpec(qn.shape, fix), pl.BlockSpec(wuq.shape, fix),
                  pl.BlockSpec(kvn.shape, fix), pl.BlockSpec(wk.shape, fix), pl.BlockSpec(wv.shape, fix),
                  pl.BlockSpec((tm, LANES), pos), pl.BlockSpec((tm, LANES), pos), pl.BlockSpec((tm, LANES), pos),
                  pl.BlockSpec((tm, LANES), pos)],
        out_specs=[pl.BlockSpec((tm, wd), row) for wd, _ in widths],
        out_shape=[jax.ShapeDtypeStruct((BT, wd), dt) for wd, dt in widths],
        compiler_params=_cparams(("arbitrary",)),
        name="proj",
    )(x2, gpre, w, qn, wuq, kvn, wk, wv, rope_c, rope_sa, rope_sb, eaug)


def _compress_kernel(c_ref, pet_ref, peb_ref, w1t_ref, w1b_ref, w2_ref, o_ref):
    nch = c_ref.shape[1] // CMP_STRIDE
    c = jnp.concatenate([c_ref[0, pl.ds(j, nch, stride=CMP_STRIDE), :] for j in range(CMP_STRIDE)], axis=1)
    top = (c + pet_ref[...]).astype(BF16)
    bot = (c + peb_ref[...]).astype(BF16)
    out = jnp.zeros((nch, LANES), F32)
    for i in range(2):
        pre = _dot(top, w1t_ref[i]) + pltpu.roll(_dot(bot, w1b_ref[i]), nch - 1, 0)
        hid = pre * (1.0 / (1.0 + jnp.exp(-pre)))
        out = out + _dot(hid.astype(BF16), w2_ref[i])
    o_ref[0] = out.astype(BF16)


def _compress(tokens, pet, peb, w1t, w1b, w2):
    B, T, width = tokens.shape
    nch = T // CMP_STRIDE
    full = lambda a: pl.BlockSpec(a.shape, lambda b: (0,) * a.ndim)
    return pl.pallas_call(
        _compress_kernel,
        grid=(B,),
        in_specs=[pl.BlockSpec((1, T, width), lambda b: (b, 0, 0)),
                  full(pet), full(peb), full(w1t), full(w1b), full(w2)],
        out_specs=pl.BlockSpec((1, nch, LANES), lambda b: (b, 0, 0)),
        out_shape=jax.ShapeDtypeStruct((B, nch, LANES), BF16),
        compiler_params=_cparams(("arbitrary",)),
        name="compress",
    )(tokens, pet, peb, w1t, w1b, w2)


def _head_slabs(qblk, n_heads):
    q = qblk.astype(F32)
    lo = lax.broadcasted_iota(jnp.int32, (q.shape[0], LANES), 1) < HEAD_DIM
    out = []
    for j in range(n_heads // 2):
        slab = q[:, LANES * j:LANES * (j + 1)]
        out.append(jnp.where(lo, slab, 0.0))
        out.append(jnp.where(lo, pltpu.roll(slab, HEAD_DIM, 1), 0.0))
    return out


def _merge_upper(accs):
    lo = lax.broadcasted_iota(jnp.int32, accs[0].shape, 1) < HEAD_DIM
    slabs = [jnp.where(lo, pltpu.roll(accs[2 * j], HEAD_DIM, 1), accs[2 * j + 1]) for j in range(len(accs) // 2)]
    return jnp.concatenate(slabs, axis=1)


def _merge_normalized(accs):
    lo = lax.broadcasted_iota(jnp.int32, accs[0].shape, 1) < HEAD_DIM
    slabs = []
    for j in range(len(accs) // 2):
        a, b = accs[2 * j], accs[2 * j + 1]
        out = jnp.where(lo, pltpu.roll(a, HEAD_DIM, 1), b)
        den = jnp.where(lo, a, pltpu.roll(b, HEAD_DIM, 1))
        slabs.append(out / den)
    return jnp.concatenate(slabs, axis=1)


def _row_max_lanes(s):
    return jnp.broadcast_to(jnp.max(s, axis=-1, keepdims=True), (s.shape[0], LANES))


def _sub_lanes(s, ref):
    return jnp.concatenate([s[:, LANES * i:LANES * (i + 1)] - ref for i in range(s.shape[1] // LANES)], axis=1)


def _topk_mask_t(imp, t0):
    ns, tq = imp.shape
    srow = lax.broadcasted_iota(jnp.int32, (ns, tq), 0)
    cur = (t0 + lax.broadcasted_iota(jnp.int32, (ns, tq), 1)) // SEL_BLOCK
    forced = (srow == 0) | (srow == cur) | (srow == cur - 1)
    x = jnp.where(forced, BIG, jnp.where(srow <= cur, imp, -BIG))
    sub = 8
    groups = [x[sub * v:sub * (v + 1)] for v in range(ns // sub)]
    rows_in = [srow[sub * v:sub * (v + 1)] for v in range(ns // sub)]
    cnts = [jnp.zeros((sub, tq), F32) for _ in groups]
    for sp in range(ns):
        other = jnp.broadcast_to(x[sp:sp + 1, :], (sub, tq))
        for v, xv in enumerate(groups):
            if sub * v > sp:
                beats = other >= xv
            elif sub * v + sub - 1 < sp:
                beats = other > xv
            else:
                beats = (other > xv) | ((other == xv) & (rows_in[v] > sp))
            cnts[v] = cnts[v] + jnp.where(beats, 1.0, 0.0)
    cnt = jnp.concatenate(cnts, axis=0)
    sel = (cnt < float(min(SEL_TOP, ns))) & (srow <= cur)
    return jnp.where(sel, 0.0, NEG)


def _cmp_kernel(q_ref, kvc_ref, bc_ref, ovt_ref, ocmp_ref, selb_ref):
    rows = CMP_GROUP * Q_TILE
    ncp, ns = kvc_ref.shape[1], ovt_ref.shape[0]
    step = pl.program_id(1)
    lo_step, nse = 0, SEL_TOP
    while lo_step * rows < ns * SEL_BLOCK:
        nse = min(nse, ns)
        hi_step = nse * SEL_BLOCK // rows
        ncw = min(ncp, -(-nse * (SEL_BLOCK // CMP_STRIDE) // LANES) * LANES)
        if hi_step > lo_step:
            pl.when((step >= lo_step) & (step < hi_step))(
                functools.partial(_cmp_body, ncw, nse, q_ref, kvc_ref, bc_ref, ovt_ref, ocmp_ref, selb_ref))
            lo_step = hi_step
        nse += SEL_TOP


def _cmp_body(ncw, ns, q_ref, kvc_ref, bc_ref, ovt_ref, ocmp_ref, selb_ref):
    tq = Q_TILE
    kvc = kvc_ref[0, :ncw, :]
    ovt = ovt_ref[:ns, :ncw]
    groups = range(CMP_GROUP)
    sls = [slice(g * tq, (g + 1) * tq) for g in groups]
    s_alls = []
    for g in groups:
        qs = _head_slabs(q_ref[0, sls[g], :], NSA_HEADS)
        s_alls.append(_dot_nt(jnp.concatenate([q.astype(BF16) for q in qs], axis=0), kvc))
    psums, o_alls = [], []
    for g in groups:
        psum = None
        ps = []
        for h in range(NSA_HEADS):
            s = s_alls[g][h * tq:(h + 1) * tq] + bc_ref[h, sls[g], :ncw]
            m = jnp.max(s, axis=-1, keepdims=True)
            e = jnp.exp2(s - m)
            l = jnp.sum(e, axis=-1, keepdims=True)
            p = e * jnp.where(m > 0.5 * NEG, 1.0 / l, 0.0)
            psum = p if psum is None else psum + p
            ps.append(p.astype(BF16))
        psums.append(psum)
        o_alls.append(_dot(jnp.concatenate(ps, axis=0), kvc))
    imps = []
    for g in groups:
        hi = psums[g].astype(BF16)
        lo = (psums[g] - hi.astype(F32)).astype(BF16)
        imps.append(_dot_nt(ovt, hi) + _dot_nt(ovt, lo))
    for g in groups:
        ocmp = _merge_upper([o_alls[g][h * tq:(h + 1) * tq] for h in range(NSA_HEADS)])
        ocmp_ref[0, sls[g], :] = ocmp.astype(ocmp_ref.dtype)
        sb = _topk_mask_t(imps[g], (pl.program_id(1) * CMP_GROUP + g) * tq)
        if ns < HEAD_DIM:
            sb = jnp.concatenate([sb, jnp.full((HEAD_DIM - ns, tq), NEG, F32)], axis=0)
        full = jnp.concatenate([jnp.zeros((HEAD_DIM, tq), F32), sb], axis=0)
        selb_ref[0, sls[g], :] = full.T.astype(BF16)


def _cmp_attention(qa, kvc, bc, ovt, T):
    B = qa.shape[0]
    ncp = kvc.shape[1]
    rows = CMP_GROUP * Q_TILE
    return pl.pallas_call(
        _cmp_kernel,
        grid=(B, T // rows),
        in_specs=[pl.BlockSpec((1, rows, 256), lambda b, n: (b, n, 0)),
                  pl.BlockSpec((1, ncp, LANES), lambda b, n: (b, 0, 0)),
                  pl.BlockSpec((NSA_HEADS, rows, ncp), lambda b, n: (0, n, 0)),
                  pl.BlockSpec(ovt.shape, lambda b, n: (0, 0))],
        out_specs=[pl.BlockSpec((1, rows, 256), lambda b, n: (b, n, 0)),
                   pl.BlockSpec((1, rows, LANES), lambda b, n: (b, n, 0))],
        out_shape=[jax.ShapeDtypeStruct((B, T, 256), BF16), jax.ShapeDtypeStruct((B, T, LANES), BF16)],
        compiler_params=_cparams(("arbitrary", "arbitrary")),
        name="cmp_select",
    )(qa, kvc, bc, ovt)


def _softmax_tile(s2, rows, m_s, shift=None):
    m_t = _row_max_lanes(s2)
    if shift is not None:
        m_t = m_t + shift
    m_old = m_s[rows]
    m_new = jnp.maximum(m_old, m_t)
    p = jnp.exp2(_sub_lanes(s2, m_new if shift is None else m_new - shift))
    alpha = jnp.exp2(m_old - m_new)
    m_s[rows] = m_new
    return p.astype(BF16), alpha


def _slc_kernel(b31_ref, q_ref, selb_ref, ka_ref, kv_ref, tab_ref, o_ref, m_s, acc_s):
    tq = q_ref.shape[1]
    H = NSA_HEADS
    n = pl.program_id(1)
    qs = _head_slabs(q_ref[0], H)
    selb = selb_ref[0].astype(F32)
    qst = jnp.concatenate([(qs[h] + selb).astype(BF16) for h in range(H)], axis=0)
    m_s[...] = jnp.full(m_s.shape, -jnp.inf, F32)
    acc_s[...] = jnp.zeros(acc_s.shape, F32)

    def scores(k0, width=tq):
        k0 = pl.multiple_of(k0, tq)
        return _dot_nt(qst, ka_ref[0, pl.ds(k0, width), :]), kv_ref[0, pl.ds(k0, width), :]

    def update(s_all, kv):
        ps, alphas = [], []
        for h in range(H):
            rows = slice(h * tq, (h + 1) * tq)
            p, alpha = _softmax_tile(s_all[rows], rows, m_s, shift=b31_ref[h])
            ps.append(p)
            alphas.append(alpha)
        pv = _dot(jnp.concatenate(ps, axis=0), kv)
        for h in range(H):
            rows = slice(h * tq, (h + 1) * tq)
            acc_s[rows] = alphas[h] * acc_s[rows] + pv[rows]

    def far_triple(i, carry):
        update(*scores(3 * i * tq, 3 * tq))
        return carry

    n_far = jnp.maximum(n - 1, 0)
    lax.fori_loop(0, n_far // 3, far_triple, 0)

    @pl.when(n_far % 3 == 1)
    def _():
        update(*scores((n_far - 1) * tq))

    @pl.when(n_far % 3 == 2)
    def _():
        update(*scores((n_far - 2) * tq, 2 * tq))

    def near_diagonal(k0, before):
        k0 = pl.multiple_of(k0, tq)
        half = tq // SLC_DIAG_PARTS
        parts = []
        for part in range(SLC_DIAG_PARTS):
            kw = before + half * (part + 1)
            q_part = jnp.concatenate([qst[h * tq + part * half:h * tq + (part + 1) * half] for h in range(H)], axis=0)
            parts.append((_dot_nt(q_part, ka_ref[0, pl.ds(k0, kw), :]), kv_ref[0, pl.ds(k0, kw), :], kw))
        for part, (s_all, kv, kw) in enumerate(parts):
            ps, alphas = [], []
            for h in range(H):
                rows = slice(h * tq + part * half, h * tq + (part + 1) * half)
                bias = tab_ref[h, part * half:(part + 1) * half, tq - before:tq - before + kw]
                p, alpha = _softmax_tile(s_all[h * half:(h + 1) * half] + bias, rows, m_s)
                ps.append(p)
                alphas.append(alpha)
            pv = _dot(jnp.concatenate(ps, axis=0), kv)
            for h in range(H):
                rows = slice(h * tq + part * half, h * tq + (part + 1) * half)
                acc_s[rows] = alphas[h] * acc_s[rows] + pv[h * half:(h + 1) * half]

    @pl.when(n >= 1)
    def _():
        near_diagonal(jnp.maximum(n - 1, 0) * tq, tq)

    @pl.when(n == 0)
    def _():
        near_diagonal(0, 0)

    o_ref[0] = _merge_normalized([acc_s[h * tq:(h + 1) * tq] for h in range(H)]).astype(o_ref.dtype)


def _slc_attention(b31, qa, selb, slck, slcv, tab, T):
    B = qa.shape[0]
    tq = FLASH_TILE
    H = NSA_HEADS
    tile = lambda w: pl.BlockSpec((1, tq, w), lambda b, n: (b, n, 0))
    return pl.pallas_call(
        _slc_kernel,
        grid=(B, T // tq),
        in_specs=[pl.BlockSpec(memory_space=pltpu.SMEM),
                  tile(256), tile(LANES),
                  pl.BlockSpec((1, T, LANES), lambda b, n: (b, 0, 0)),
                  pl.BlockSpec((1, T, LANES), lambda b, n: (b, 0, 0)),
                  pl.BlockSpec(tab.shape, lambda b, n: (0, 0, 0))],
        out_specs=tile(256),
        out_shape=jax.ShapeDtypeStruct((B, T, 256), BF16),
        scratch_shapes=[pltpu.VMEM((H * tq, LANES), F32), pltpu.VMEM((H * tq, LANES), F32)],
        compiler_params=_cparams(("arbitrary", "arbitrary")),
        name="slc_attention",
    )(b31, qa, selb, slck, slcv, tab)


def _band_kernel(n_heads, window, has_sink, n_sub, tq, *refs):
    if has_sink:
        q_ref, kv_ref, tab_ref, sink_ref, o_ref = refs
    else:
        q_ref, kv_ref, tab_ref, o_ref = refs
    span = window + tq
    lo_kv = lax.broadcasted_iota(jnp.int32, (span, LANES), 1) < HEAD_DIM
    lo_o = lax.broadcasted_iota(jnp.int32, (n_heads * tq, LANES), 1) < HEAD_DIM
    groups = range(n_sub)
    tiles = [pl.program_id(1) * n_sub + g for g in groups]
    kvs, dots = [], []
    for g in groups:
        k0 = pl.multiple_of(jnp.maximum(tiles[g] * tq - window, 0), tq)
        kvs.append(kv_ref[0, pl.ds(k0, span), :])
        qs = _head_slabs(q_ref[0, g * tq:(g + 1) * tq, :], n_heads)
        dots.append(_dot_nt(jnp.concatenate([q.astype(BF16) for q in qs], axis=0), kvs[g]))
    accs, ms = [], []
    for g in groups:
        s = dots[g] + tab_ref[jnp.minimum(tiles[g], window // tq)]
        m = _row_max_lanes(s)
        if has_sink:
            m = jnp.maximum(m, sink_ref[...])
        e = jnp.exp2(_sub_lanes(s, m))
        ms.append(m)
        accs.append(_dot(e.astype(BF16), jnp.where(lo_kv, 1.0, kvs[g]).astype(BF16)))
    for g in groups:
        acc = accs[g]
        if has_sink:
            acc = acc + jnp.where(lo_o, jnp.exp2(sink_ref[...] - ms[g]), 0.0)
        out = _merge_normalized([acc[h * tq:(h + 1) * tq] for h in range(n_heads)])
        o_ref[0, g * tq:(g + 1) * tq, :] = out.astype(o_ref.dtype)


def _band_attention(q, kv_pad, tab, sinks, n_heads, window, T, out_dtype):
    B = q.shape[0]
    width = n_heads * HEAD_DIM
    has_sink = sinks is not None
    nvar, tq = tab.shape[1], tab.shape[2]
    tab2 = jnp.transpose(tab, (1, 0, 2, 3)).reshape(nvar, n_heads * tq, window + tq)
    n_sub = (BAND_ROWS if has_sink else 2 * BAND_ROWS) // (n_heads * tq)
    rows = n_sub * tq
    in_specs = [pl.BlockSpec((1, rows, width), lambda b, n: (b, n, 0)),
                pl.BlockSpec((1, T, LANES), lambda b, n: (b, 0, 0)),
                pl.BlockSpec(tab2.shape, lambda b, n: (0, 0, 0))]
    args = (q, kv_pad, tab2)
    if has_sink:
        rep = jnp.broadcast_to(jnp.repeat(sinks, tq)[:, None], (n_heads * tq, LANES))
        in_specs.append(pl.BlockSpec(rep.shape, lambda b, n: (0, 0)))
        args = args + (rep,)
    return pl.pallas_call(
        functools.partial(_band_kernel, n_heads, window, has_sink, n_sub, tq),
        grid=(B, T // rows),
        in_specs=in_specs,
        out_specs=pl.BlockSpec((1, rows, width), lambda b, n: (b, n, 0)),
        out_shape=jax.ShapeDtypeStruct((B, T, width), out_dtype),
        compiler_params=_cparams(("arbitrary", "arbitrary")),
        name="band_sink" if has_sink else "band_window",
    )(*args)


def _mla_kernel(q_ref, k_ref, v_ref, o_ref, m_s, acc_s):
    tq = q_ref.shape[1]
    H = MLA_HEADS
    n = pl.program_id(1)
    m_s[...] = jnp.full(m_s.shape, -jnp.inf, F32)
    acc_s[...] = jnp.zeros(acc_s.shape, F32)

    def scores(k0, width=tq):
        k0 = pl.multiple_of(k0, tq)
        return k0, width, [_dot_nt(q_ref[0, :, LANES * h:LANES * (h + 1)],
                                   k_ref[0, pl.ds(k0, width), LANES * h:LANES * (h + 1)]) for h in range(H)]

    def update(k0, width, ss):
        ps, alphas = [], []
        for h in range(H):
            p, alpha = _softmax_tile(ss[h], slice(h * tq, (h + 1) * tq), m_s)
            ps.append(p)
            alphas.append(alpha)
        for h in range(H):
            rows = slice(h * tq, (h + 1) * tq)
            v = v_ref[0, pl.ds(k0, width), LANES * h:LANES * (h + 1)]
            acc_s[rows] = alphas[h] * acc_s[rows] + _dot(ps[h], v)

    def far_triple(i, carry):
        update(*scores(3 * i * tq, 3 * tq))
        return carry

    n_far = jnp.maximum(n - 1, 0)
    lax.fori_loop(0, n_far // 3, far_triple, 0)

    @pl.when(n_far % 3 == 1)
    def _():
        update(*scores((n_far - 1) * tq))

    @pl.when(n_far % 3 == 2)
    def _():
        update(*scores((n_far - 2) * tq, 2 * tq))

    def near_diagonal(k0, before):
        k0 = pl.multiple_of(k0, tq)
        half = tq // MLA_DIAG_PARTS
        parts = []
        for part in range(MLA_DIAG_PARTS):
            kw = before + half * (part + 1)
            r = slice(part * half, (part + 1) * half)
            parts.append((kw, [_dot_nt(q_ref[0, r, LANES * h:LANES * (h + 1)],
                                       k_ref[0, pl.ds(k0, kw), LANES * h:LANES * (h + 1)]) for h in range(H)]))
        for part, (kw, ss) in enumerate(parts):
            row = lax.broadcasted_iota(jnp.int32, (half, kw), 0)
            col = lax.broadcasted_iota(jnp.int32, (half, kw), 1)
            causal = col <= row + (before + part * half)
            ps, alphas = [], []
            for h in range(H):
                rows = slice(h * tq + part * half, h * tq + (part + 1) * half)
                p, alpha = _softmax_tile(jnp.where(causal, ss[h], NEG), rows, m_s)
                ps.append(p)
                alphas.append(alpha)
            for h in range(H):
                rows = slice(h * tq + part * half, h * tq + (part + 1) * half)
                v = v_ref[0, pl.ds(k0, kw), LANES * h:LANES * (h + 1)]
                acc_s[rows] = alphas[h] * acc_s[rows] + _dot(ps[h], v)

    @pl.when(n >= 1)
    def _():
        near_diagonal(jnp.maximum(n - 1, 0) * tq, tq)

    @pl.when(n == 0)
    def _():
        near_diagonal(0, 0)

    o_ref[0] = _merge_normalized([acc_s[h * tq:(h + 1) * tq] for h in range(H)]).astype(o_ref.dtype)


def _mla_attention(qm, km, vm, T):
    B = qm.shape[0]
    tq = FLASH_TILE
    H = MLA_HEADS
    return pl.pallas_call(
        _mla_kernel,
        grid=(B, T // tq),
        in_specs=[pl.BlockSpec((1, tq, 512), lambda b, n: (b, n, 0)),
                  pl.BlockSpec((1, T, 512), lambda b, n: (b, 0, 0)),
                  pl.BlockSpec((1, T, 512), lambda b, n: (b, 0, 0))],
        out_specs=pl.BlockSpec((1, tq, 256), lambda b, n: (b, n, 0)),
        out_shape=jax.ShapeDtypeStruct((B, T, 256), BF16),
        scratch_shapes=[pltpu.VMEM((H * tq, LANES), F32), pltpu.VMEM((H * tq, LANES), F32)],
        compiler_params=_cparams(("arbitrary", "arbitrary")),
        name="mla_attention",
    )(qm, km, vm)


def _out_kernel(x_ref, gpre_ref, wz_ref, ocmp_ref, oslc_ref, owin_ref, krb_ref, gexp_ref, ob_ref, oc_ref, w_ref,
                gpost_ref, o_ref):
    sub = x_ref.shape[0] // OUT_SUBTILES
    rs = [slice(t * sub, (t + 1) * sub) for t in range(OUT_SUBTILES)]
    w_a = NSA_HEADS * HEAD_DIM
    hs = [_rms(x_ref[r, :], gpre_ref[...]).astype(BF16) for r in rs]
    zs = [_dot(h, wz_ref[...]) for h in hs]
    mixeds = []
    for r, z in zip(rs, zs):
        sig = 1.0 / (1.0 + jnp.exp(-krb_ref[r, :]))
        hi = sig.astype(BF16)
        lo = (sig - hi.astype(F32)).astype(BF16)
        gates = _dot(jnp.concatenate([hi, lo], axis=1), gexp_ref[...])
        oa = None
        for j, br in enumerate((ocmp_ref, oslc_ref, owin_ref)):
            term = gates[:, w_a * j:w_a * (j + 1)] * br[r, :].astype(F32)
            oa = term if oa is None else oa + term
        heads = jnp.concatenate([oa, ob_ref[r, :].astype(F32), oc_ref[r, :].astype(F32)], axis=1)
        mixeds.append((heads * (z * (1.0 / (1.0 + jnp.exp(-z))))).astype(BF16))
    ys = [_dot(m, w_ref[...]) for m in mixeds]
    for r, y in zip(rs, ys):
        o_ref[r, :] = x_ref[r, :] + _rms(y, gpost_ref[...])


def _out_project(x2, gpre, wz, ocmp, oslc, owin, krb, gexp, ob, oc, w, gpost, tm):
    BT = x2.shape[0]
    row = lambda i: (i, 0)
    fix = lambda i: (0, 0)
    spec = lambda a: pl.BlockSpec((tm, a.shape[1]), row)
    full = lambda a: pl.BlockSpec(a.shape, fix)
    return pl.pallas_call(
        _out_kernel,
        grid=(BT // tm,),
        in_specs=[spec(x2), full(gpre), full(wz), spec(ocmp), spec(oslc), spec(owin), spec(krb), full(gexp),
                  spec(ob), spec(oc), full(w), full(gpost)],
        out_specs=pl.BlockSpec((tm, D_MODEL), row),
        out_shape=jax.ShapeDtypeStruct((BT, D_MODEL), F32),
        compiler_params=_cparams(("arbitrary",)),
        name="out_proj",
    )(x2, gpre, wz, ocmp, oslc, owin, krb, gexp, ob, oc, w, gpost)


def _split_in(w):
    outs, o = [], 0
    for s in IN_SIZES:
        outs.append(w[..., o:o + s])
        o += s
    return outs


def _relayout_kernel(w_ref, wp_ref, wz_ref):
    a_q, a_kv, a_g, b_cq, b_ckv, b_kr, c_q, c_kv, z = _split_in(w_ref[0].astype(F32))
    zeros = lambda n: jnp.zeros((a_q.shape[0], n), F32)
    krb = jnp.concatenate([zeros(HEAD_DIM), b_kr, a_g, zeros(LANES - HEAD_DIM - MLA_ROPE - 3 * NSA_HEADS)], axis=-1)
    wp_ref[0] = jnp.concatenate([a_q, a_kv, b_cq, b_ckv, krb, c_q, c_kv], axis=-1).astype(BF16)
    wz_ref[0] = z.astype(BF16)


def _prep_w_in(w):
    L, d, n = w.shape
    rt = 256
    return pl.pallas_call(
        _relayout_kernel,
        grid=(L, d // rt),
        in_specs=[pl.BlockSpec((1, rt, n), lambda l, r: (l, r, 0))],
        out_specs=[pl.BlockSpec((1, rt, C_Z), lambda l, r: (l, r, 0)),
                   pl.BlockSpec((1, rt, IN_SIZES[-1]), lambda l, r: (l, r, 0))],
        out_shape=[jax.ShapeDtypeStruct((L, d, C_Z), BF16), jax.ShapeDtypeStruct((L, d, IN_SIZES[-1]), BF16)],
        compiler_params=_cparams(("arbitrary", "arbitrary")),
        name="w_in_relayout",
    )(w)


def _prep_mla(w_uq, w_ukv):
    L = w_uq.shape[0]
    dq = MLA_NOPE + MLA_ROPE
    uq = w_uq.reshape(L, MLA_Q_RANK, MLA_HEADS, dq)
    uq = jnp.concatenate([uq, jnp.zeros((L, MLA_Q_RANK, MLA_HEADS, LANES - dq), uq.dtype)], axis=-1)
    ukv = w_ukv.reshape(L, MLA_KV_RANK, MLA_HEADS, MLA_NOPE + MLA_V)
    uk = jnp.concatenate([ukv[..., :MLA_NOPE], jnp.zeros((L, MLA_KV_RANK, MLA_HEADS, LANES - MLA_NOPE), ukv.dtype)], axis=-1)
    uv = jnp.concatenate([jnp.zeros((L, MLA_KV_RANK, MLA_HEADS, LANES - MLA_V), ukv.dtype), ukv[..., MLA_NOPE:]], axis=-1)
    return (uq.reshape(L, MLA_Q_RANK, MLA_HEADS * LANES).astype(BF16),
            uk.reshape(L, MLA_KV_RANK, MLA_HEADS * LANES).astype(BF16),
            uv.reshape(L, MLA_KV_RANK, MLA_HEADS * LANES).astype(BF16))


def _prep_compress(cmp_pos, cmp_w1, cmp_w2):
    L = cmp_pos.shape[0]
    pe = jnp.concatenate([cmp_pos[:, 0], cmp_pos[:, 1]], axis=-1)
    pet = pe[:, :CMP_STRIDE].reshape(L, 1, CMP_STRIDE * LANES)
    peb = pe[:, CMP_STRIDE:].reshape(L, 1, CMP_STRIDE * LANES)
    w1 = cmp_w1.reshape(L, 2, CMP_LEN, HEAD_DIM, CMP_HIDDEN)
    zero = jnp.zeros_like(w1[:, 0])
    w1k = jnp.concatenate([w1[:, 0], zero], axis=2)
    w1v = jnp.concatenate([zero, w1[:, 1]], axis=2)
    w1e = jnp.stack([w1k, w1v], axis=1)
    w1t = w1e[:, :, :CMP_STRIDE].reshape(L, 2, CMP_STRIDE * LANES, CMP_HIDDEN).astype(BF16)
    w1b = w1e[:, :, CMP_STRIDE:].reshape(L, 2, CMP_STRIDE * LANES, CMP_HIDDEN).astype(BF16)
    z2 = jnp.zeros_like(cmp_w2[:, 0])
    w2 = jnp.stack([jnp.concatenate([cmp_w2[:, 0], z2], axis=-1),
                    jnp.concatenate([z2, cmp_w2[:, 1]], axis=-1)], axis=1).astype(BF16)
    return pet, peb, w1t, w1b, w2


def _rope_tables(T):
    half = MLA_ROPE // 2
    inv = ROPE_THETA ** (-jnp.arange(half, dtype=F32) / half)
    ang = jnp.arange(T).astype(F32)[:, None] * inv[None, :]
    cos, sin = jnp.cos(ang), jnp.sin(ang)
    z = lambda n: jnp.zeros((T, n), F32)
    rc = jnp.concatenate([jnp.ones((T, MLA_NOPE), F32), cos, cos, z(LANES - MLA_NOPE - MLA_ROPE)], axis=1)
    rsa = jnp.concatenate([z(MLA_NOPE + half), sin, z(LANES - MLA_NOPE - MLA_ROPE)], axis=1)
    rsb = jnp.concatenate([z(MLA_NOPE), -sin, z(LANES - MLA_NOPE - half)], axis=1)
    return rc, rsa, rsb


def _static_tables(T):
    ns = T // SEL_BLOCK
    nc = T // CMP_STRIDE - 1
    ncp = T // CMP_STRIDE
    sstart = np.arange(ns) * SEL_BLOCK
    cstart = np.arange(nc) * CMP_STRIDE
    overlap = (np.clip(np.minimum(cstart[:, None] + CMP_LEN, sstart[None, :] + SEL_BLOCK)
                       - np.maximum(cstart[:, None], sstart[None, :]), 0, None) / CMP_STRIDE).astype(np.float32)
    ovt = np.zeros((ns, ncp), np.float32)
    ovt[:, :nc] = overlap.T
    eaug = np.zeros((T, LANES), np.float32)
    eaug[np.arange(T), HEAD_DIM + np.arange(T) // SEL_BLOCK] = 1.0
    gexp = np.zeros((2, LANES, 3 * NSA_HEADS * HEAD_DIM), np.float32)
    for h in range(NSA_HEADS):
        for j in range(3):
            c0 = NSA_HEADS * HEAD_DIM * j + HEAD_DIM * h
            gexp[:, GATE_LANE0 + 3 * h + j, c0:c0 + HEAD_DIM] = 1.0
    gexp = gexp.reshape(2 * LANES, -1)
    return jnp.asarray(ovt, BF16), jnp.asarray(eaug, BF16), jnp.asarray(gexp, BF16)


def kernel(x, w_in, w_out, norm_pre, norm_post, cmp_pos, cmp_w1, cmp_w2, mla_q_norm, mla_w_uq, mla_kv_norm,
           mla_w_ukv, swa_sinks, rel_bias):
    B, T, D = x.shape
    depth = w_in.shape[0]
    assert D == D_MODEL and T % (2 * FLASH_TILE) == 0 and T // SEL_BLOCK <= HEAD_DIM
    ncp = T // CMP_STRIDE

    w_in_p, w_z = _prep_w_in(w_in.astype(BF16))
    wuq, wk, wv = _prep_mla(mla_w_uq, mla_w_ukv)
    pet, peb, w1t, w1b, w2 = _prep_compress(cmp_pos, cmp_w1, cmp_w2)
    w_out_b = w_out.astype(BF16)
    rope_c, rope_sa, rope_sb = _rope_tables(T)
    ovt, eaug, gexp = _static_tables(T)

    bc = _bias_table(rel_bias, 0, NSA_HEADS, ncp, T, CMP_STRIDE, 1, -(CMP_LEN - 1), 0, 1 << 30, 32, mult=LOG2E)
    bc = jnp.transpose(bc.reshape(NSA_HEADS, ncp, T), (0, 2, 1))
    tab_near = _bias_table(rel_bias, 0, NSA_HEADS, FLASH_TILE, 2 * FLASH_TILE, 1, -1, FLASH_TILE, 0, 1 << 30, 256,
                           mult=LOG2E).reshape(NSA_HEADS, FLASH_TILE, 2 * FLASH_TILE)
    tab_win = _bias_table(rel_bias, 0, NSA_HEADS, WIN_TILE, NSA_WINDOW + WIN_TILE, 1, -1, 0, 0, NSA_WINDOW, WIN_TILE,
                          mult=LOG2E, nblk=NSA_WINDOW // WIN_TILE + 1, blk_d0=WIN_TILE)
    tab_swa = _bias_table(rel_bias, NSA_HEADS, SWA_HEADS, Q_TILE, SWA_WINDOW + Q_TILE, 1, -1, 0, 0, SWA_WINDOW, Q_TILE,
                          mult=LOG2E, nblk=SWA_WINDOW // Q_TILE + 1, blk_d0=Q_TILE)
    b31 = rel_bias[:NSA_HEADS, REL_BUCKETS - 1] * LOG2E

    x2 = x.reshape(B * T, D)
    for l in range(depth):
        (qa, cmpkv, slck, slcv, win, qm, km, vm, krb, cq, ckv) = _project(
            x2, norm_pre[l][None], w_in_p[l], mla_q_norm[l][None], wuq[l], mla_kv_norm[l][None], wk[l], wv[l],
            rope_c, rope_sa, rope_sb, eaug, T, PROJ_ROWS)
        r3 = lambda a: a.reshape(B, T, a.shape[-1])
        kvc = _compress(r3(cmpkv), pet[l], peb[l], w1t[l], w1b[l], w2[l])
        qa3 = r3(qa)
        ocmp, selb = _cmp_attention(qa3, kvc, bc, ovt, T)
        owin = _band_attention(qa3, r3(win), tab_win, None, NSA_HEADS, NSA_WINDOW, T, BF16)
        oslc = _slc_attention(b31, qa3, selb, r3(slck), r3(slcv), tab_near, T)
        oc = _band_attention(r3(cq), r3(ckv), tab_swa, swa_sinks[l] * LOG2E, SWA_HEADS, SWA_WINDOW, T, BF16)
        ob = _mla_attention(r3(qm), r3(km), r3(vm), T)
        flat = lambda a: a.reshape(B * T, a.shape[-1])
        x2 = _out_project(x2, norm_pre[l][None], w_z[l], flat(ocmp), flat(oslc), flat(owin), krb, gexp,
                          flat(ob), flat(oc), w_out_b[l], norm_post[l][None], OUT_ROWS)
    return x2.reshape(B, T, D)
```

```python
import functools
import math

import numpy as np
import jax
import jax.numpy as jnp
from jax import lax
from jax.experimental import pallas as pl
from jax.experimental.pallas import tpu as pltpu

F32 = jnp.float32
BF16 = jnp.bfloat16

D_MODEL = 1024
HEAD_DIM = 64
NSA_HEADS = 4
CMP_LEN = 32
CMP_STRIDE = 16
CMP_HIDDEN = 128
SEL_BLOCK = 64
SEL_TOP = 16
NSA_WINDOW = 512
MLA_HEADS = 4
MLA_Q_RANK = 256
MLA_KV_RANK = 128
MLA_NOPE = 64
MLA_ROPE = 32
MLA_V = 64
ROPE_THETA = 10000.0
SWA_HEADS = 8
SWA_WINDOW = 128
REL_BUCKETS = 32
REL_MAX_DIST = 512
NORM_EPS = 1e-6
NEG = -1e30
BIG = 1e9
LOG2E = math.log2(math.e)
MLA_SCALE2 = (MLA_NOPE + MLA_ROPE) ** -0.5 * LOG2E
IN_SIZES = (256, 384, 12, 256, 128, 32, 512, 128, 1024)

LANES = 128
Q_TILE = 128
WIN_TILE = 128
BAND_ROWS = 4096
PROJ_ROWS = 1024
OUT_SUBTILES = 4
OUT_ROWS = 1024
CMP_GROUP = 4
FLASH_TILE = 512
SLC_DIAG_PARTS = 4
MLA_DIAG_PARTS = 4
GATE_LANE0 = 96
VMEM_LIMIT = 56 * 1024 * 1024

C_AQ, C_CMP, C_SLC, C_WIN, C_BCQ, C_BCKV, C_KRB, C_CQ, C_CKV, C_Z = (
    0, 256, 384, 512, 640, 896, 1024, 1152, 1664, 1792)


def _dot(a, b):
    return jnp.dot(a, b, preferred_element_type=F32)


def _dot_nt(a, b):
    return lax.dot_general(a, b, (((1,), (1,)), ((), ())), preferred_element_type=F32)


def _bucket_thresholds():
    d = np.arange(0, 4 * REL_MAX_DIST)
    exact = REL_BUCKETS // 2
    large = exact + (np.log(np.maximum(d, 1).astype(np.float32) / np.float32(exact))
                     / np.float32(math.log(REL_MAX_DIST / exact)) * np.float32(REL_BUCKETS - exact)).astype(np.int32)
    b = np.where(d < exact, d, np.minimum(large, REL_BUCKETS - 1))
    assert np.all(np.diff(b) >= 0) and b[-1] == REL_BUCKETS - 1
    return [int(np.argmax(b >= k)) for k in range(REL_BUCKETS)]


_THRESH = _bucket_thresholds()


def _cparams(sem):
    return pltpu.CompilerParams(dimension_semantics=sem, vmem_limit_bytes=VMEM_LIMIT)


def _table_kernel(h0, row_step, sign, d0, blk_d0, lo, hi, mult, rb_ref, out_ref):
    h = pl.program_id(0) + h0
    blk = pl.program_id(1)
    rt, cc = out_ref.shape[2], out_ref.shape[3]
    base = d0 + blk * blk_d0 - sign * row_step * rt * pl.program_id(2)
    width = -(-(row_step * rt + cc) // LANES) * LANES
    w = lax.broadcasted_iota(jnp.int32, (8, width), 1)
    dist = base + sign * jnp.where(w < cc, w, w - width)
    acc = jnp.full(dist.shape, rb_ref[h, 0], F32)
    for k in range(1, REL_BUCKETS):
        acc = jnp.where(dist >= _THRESH[k], rb_ref[h, k], acc)
    row = jnp.where((dist >= lo) & (dist < hi), acc * mult, NEG)
    rolled = pltpu.roll(jnp.broadcast_to(row[:1], (rt, width)), 0, 1, stride=row_step, stride_axis=0)
    out_ref[0, 0] = rolled[:, :cc]


def _bias_table(rel_bias, h0, nh, rows, cols, row_step, sign, d0, lo, hi, rt, mult=1.0, nblk=1, blk_d0=0):
    return pl.pallas_call(
        functools.partial(_table_kernel, h0, row_step, sign, d0, blk_d0, lo, hi, mult),
        grid=(nh, nblk, rows // rt),
        in_specs=[pl.BlockSpec(memory_space=pltpu.SMEM)],
        out_specs=pl.BlockSpec((1, 1, rt, cols), lambda h, b, r: (h, b, r, 0)),
        out_shape=jax.ShapeDtypeStruct((nh, nblk, rows, cols), F32),
        compiler_params=_cparams(("arbitrary", "arbitrary", "arbitrary")),
        name="bias_table",
    )(rel_bias)


def _rms(v, g):
    return v * lax.rsqrt(jnp.mean(v * v, axis=-1, keepdims=True) + NORM_EPS) * g


def _proj_kernel(x_ref, gpre_ref, w_ref, qn_ref, wuq_ref, kvn_ref, wk_ref, wv_ref,
                 rc_ref, rsa_ref, rsb_ref, eaug_ref,
                 qa_ref, cmp_ref, slck_ref, slcv_ref, win_ref, qm_ref, km_ref, vm_ref,
                 krb_ref, cq_ref, ckv_ref):
    hb = _rms(x_ref[...], gpre_ref[...]).astype(BF16)
    bounds = ((C_AQ, C_CMP), (C_CMP, C_WIN), (C_WIN, C_KRB), (C_KRB, C_Z))
    chunks = {ab: _dot_nt(hb, w_ref[ab[0]:ab[1], :]) for ab in bounds}
    lane = lax.broadcasted_iota(jnp.int32, (hb.shape[0], LANES), 1)
    lo = lane < HEAD_DIM

    def proj(c0, c1):
        for (a, b), y in chunks.items():
            if a <= c0 and c1 <= b:
                return y[:, c0 - a:c1 - a]
        raise AssertionError((c0, c1))

    qa_ref[...] = (proj(C_AQ, C_CMP) * (0.125 * LOG2E)).astype(BF16)
    cmp_ref[...] = proj(C_CMP, C_SLC)
    slc = proj(C_SLC, C_WIN)
    slcv_ref[...] = jnp.where(lo, 1.0, slc).astype(BF16)
    slck_ref[...] = jnp.where(lo, slc, eaug_ref[...].astype(F32)).astype(BF16)
    win_ref[...] = proj(C_WIN, C_BCQ).astype(BF16)
    cq_ref[...] = (proj(C_CQ, C_CKV) * (0.125 * LOG2E)).astype(BF16)
    ckv_ref[...] = proj(C_CKV, C_Z).astype(BF16)

    rc, rsa, rsb = rc_ref[...], rsa_ref[...], rsb_ref[...]

    def rope(v):
        return v * rc + pltpu.roll(v, 16, 1) * rsa + pltpu.roll(v, 112, 1) * rsb

    cqn = _rms(proj(C_BCQ, C_BCKV), qn_ref[...]).astype(BF16)
    qm = _dot(cqn, wuq_ref[...])
    for h in range(MLA_HEADS):
        qh = rope(qm[:, LANES * h:LANES * (h + 1)]) * MLA_SCALE2
        qm_ref[:, LANES * h:LANES * (h + 1)] = qh.astype(BF16)

    krb = proj(C_KRB, C_CQ)
    krb_ref[...] = krb
    krr = jnp.where(lo | (lane >= HEAD_DIM + MLA_ROPE), 0.0, rope(krb))

    ckvn = _rms(proj(C_BCKV, C_KRB), kvn_ref[...]).astype(BF16)
    kn = _dot(ckvn, wk_ref[...])
    for h in range(MLA_HEADS):
        km_ref[:, LANES * h:LANES * (h + 1)] = (kn[:, LANES * h:LANES * (h + 1)] + krr).astype(BF16)
    vm = _dot(ckvn, wv_ref[...])
    for h in range(MLA_HEADS):
        vm_ref[:, LANES * h:LANES * (h + 1)] = jnp.where(lo, 1.0, vm[:, LANES * h:LANES * (h + 1)]).astype(BF16)


def _project(x2, gpre, w, qn, wuq, kvn, wk, wv, rope_c, rope_sa, rope_sb, eaug, T, tm):
    BT = x2.shape[0]
    nt = T // tm
    row = lambda i: (i, 0)
    fix = lambda i: (0, 0)
    pos = lambda i: (i % nt, 0)
    widths = [(256, BF16), (128, F32), (128, BF16), (128, BF16), (128, BF16), (512, BF16), (512, BF16),
              (512, BF16), (128, F32), (512, BF16), (128, BF16)]
    return pl.pallas_call(
        _proj_kernel,
        grid=(BT // tm,),
        in_specs=[pl.BlockSpec((tm, D_MODEL), row), pl.BlockSpec((1, D_MODEL), fix),
                  pl.BlockSpec(w.shape, fix), pl.BlockSpec(qn.shape, fix), pl.BlockSpec(wuq.shape, fix),
                  pl.BlockSpec(kvn.shape, fix), pl.BlockSpec(wk.shape, fix), pl.BlockSpec(wv.shape, fix),
                  pl.BlockSpec((tm, LANES), pos), pl.BlockSpec((tm, LANES), pos), pl.BlockSpec((tm, LANES), pos),
                  pl.BlockSpec((tm, LANES), pos)],
        out_specs=[pl.BlockSpec((tm, wd), row) for wd, _ in widths],
        out_shape=[jax.ShapeDtypeStruct((BT, wd), dt) for wd, dt in widths],
        compiler_params=_cparams(("arbitrary",)),
        name="proj",
    )(x2, gpre, w, qn, wuq, kvn, wk, wv, rope_c, rope_sa, rope_sb, eaug)


def _compress_kernel(c_ref, pet_ref, peb_ref, w1t_ref, w1b_ref, w2_ref, o_ref):
    nch = c_ref.shape[1] // CMP_STRIDE
    c = jnp.concatenate([c_ref[0, pl.ds(j, nch, stride=CMP_STRIDE), :] for j in range(CMP_STRIDE)], axis=1)
    top = (c + pet_ref[...]).astype(BF16)
    bot = (c + peb_ref[...]).astype(BF16)
    out = jnp.zeros((nch, LANES), F32)
    for i in range(2):
        pre = _dot(top, w1t_ref[i]) + pltpu.roll(_dot(bot, w1b_ref[i]), nch - 1, 0)
        hid = pre * (1.0 / (1.0 + jnp.exp(-pre)))
        out = out + _dot(hid.astype(BF16), w2_ref[i])
    o_ref[0] = out.astype(BF16)


def _compress(tokens, pet, peb, w1t, w1b, w2):
    B, T, width = tokens.shape
    nch = T // CMP_STRIDE
    full = lambda a: pl.BlockSpec(a.shape, lambda b: (0,) * a.ndim)
    return pl.pallas_call(
        _compress_kernel,
        grid=(B,),
        in_specs=[pl.BlockSpec((1, T, width), lambda b: (b, 0, 0)),
                  full(pet), full(peb), full(w1t), full(w1b), full(w2)],
        out_specs=pl.BlockSpec((1, nch, LANES), lambda b: (b, 0, 0)),
        out_shape=jax.ShapeDtypeStruct((B, nch, LANES), BF16),
        compiler_params=_cparams(("arbitrary",)),
        name="compress",
    )(tokens, pet, peb, w1t, w1b, w2)


def _head_slabs(qblk, n_heads):
    q = qblk.astype(F32)
    lo = lax.broadcasted_iota(jnp.int32, (q.shape[0], LANES), 1) < HEAD_DIM
    out = []
    for j in range(n_heads // 2):
        slab = q[:, LANES * j:LANES * (j + 1)]
        out.append(jnp.where(lo, slab, 0.0))
        out.append(jnp.where(lo, pltpu.roll(slab, HEAD_DIM, 1), 0.0))
    return out


def _merge_upper(accs):
    lo = lax.broadcasted_iota(jnp.int32, accs[0].shape, 1) < HEAD_DIM
    slabs = [jnp.where(lo, pltpu.roll(accs[2 * j], HEAD_DIM, 1), accs[2 * j + 1]) for j in range(len(accs) // 2)]
    return jnp.concatenate(slabs, axis=1)


def _merge_normalized(accs):
    lo = lax.broadcasted_iota(jnp.int32, accs[0].shape, 1) < HEAD_DIM
    slabs = []
    for j in range(len(accs) // 2):
        a, b = accs[2 * j], accs[2 * j + 1]
        out = jnp.where(lo, pltpu.roll(a, HEAD_DIM, 1), b)
        den = jnp.where(lo, a, pltpu.roll(b, HEAD_DIM, 1))
        slabs.append(out / den)
    return jnp.concatenate(slabs, axis=1)


def _row_max_lanes(s):
    return jnp.broadcast_to(jnp.max(s, axis=-1, keepdims=True), (s.shape[0], LANES))


def _sub_lanes(s, ref):
    return jnp.concatenate([s[:, LANES * i:LANES * (i + 1)] - ref for i in range(s.shape[1] // LANES)], axis=1)


def _topk_mask_t(imp, t0):
    ns, tq = imp.shape
    srow = lax.broadcasted_iota(jnp.int32, (ns, tq), 0)
    cur = (t0 + lax.broadcasted_iota(jnp.int32, (ns, tq), 1)) // SEL_BLOCK
    forced = (srow == 0) | (srow == cur) | (srow == cur - 1)
    x = jnp.where(forced, BIG, jnp.where(srow <= cur, imp, -BIG))
    sub = 8
    groups = [x[sub * v:sub * (v + 1)] for v in range(ns // sub)]
    rows_in = [srow[sub * v:sub * (v + 1)] for v in range(ns // sub)]
    cnts = [jnp.zeros((sub, tq), F32) for _ in groups]
    for sp in range(ns):
        other = jnp.broadcast_to(x[sp:sp + 1, :], (sub, tq))
        for v, xv in enumerate(groups):
            if sub * v > sp:
                beats = other >= xv
            elif sub * v + sub - 1 < sp:
                beats = other > xv
            else:
                beats = (other > xv) | ((other == xv) & (rows_in[v] > sp))
            cnts[v] = cnts[v] + jnp.where(beats, 1.0, 0.0)
    cnt = jnp.concatenate(cnts, axis=0)
    sel = (cnt < float(min(SEL_TOP, ns))) & (srow <= cur)
    return jnp.where(sel, 0.0, NEG)


def _cmp_kernel(q_ref, kvc_ref, bc_ref, ovt_ref, ocmp_ref, selb_ref):
    rows = CMP_GROUP * Q_TILE
    ncp, ns = kvc_ref.shape[1], ovt_ref.shape[0]
    step = pl.program_id(1)
    lo_step, nse = 0, SEL_TOP
    while lo_step * rows < ns * SEL_BLOCK:
        nse = min(nse, ns)
        hi_step = nse * SEL_BLOCK // rows
        ncw = min(ncp, -(-nse * (SEL_BLOCK // CMP_STRIDE) // LANES) * LANES)
        if hi_step > lo_step:
            pl.when((step >= lo_step) & (step < hi_step))(
                functools.partial(_cmp_body, ncw, nse, q_ref, kvc_ref, bc_ref, ovt_ref, ocmp_ref, selb_ref))
            lo_step = hi_step
        nse += SEL_TOP


def _cmp_body(ncw, ns, q_ref, kvc_ref, bc_ref, ovt_ref, ocmp_ref, selb_ref):
    tq = Q_TILE
    kvc = kvc_ref[0, :ncw, :]
    ovt = ovt_ref[:ns, :ncw]
    groups = range(CMP_GROUP)
    sls = [slice(g * tq, (g + 1) * tq) for g in groups]
    s_alls = []
    for g in groups:
        qs = _head_slabs(q_ref[0, sls[g], :], NSA_HEADS)
        s_alls.append(_dot_nt(jnp.concatenate([q.astype(BF16) for q in qs], axis=0), kvc))
    psums, o_alls = [], []
    for g in groups:
        psum = None
        ps = []
        for h in range(NSA_HEADS):
            s = s_alls[g][h * tq:(h + 1) * tq] + bc_ref[h, sls[g], :ncw]
            m = jnp.max(s, axis=-1, keepdims=True)
            e = jnp.exp2(s - m)
            l = jnp.sum(e, axis=-1, keepdims=True)
            p = e * jnp.where(m > 0.5 * NEG, 1.0 / l, 0.0)
            psum = p if psum is None else psum + p
            ps.append(p.astype(BF16))
        psums.append(psum)
        o_alls.append(_dot(jnp.concatenate(ps, axis=0), kvc))
    imps = []
    for g in groups:
        hi = psums[g].astype(BF16)
        lo = (psums[g] - hi.astype(F32)).astype(BF16)
        imps.append(_dot_nt(ovt, hi) + _dot_nt(ovt, lo))
    for g in groups:
        ocmp = _merge_upper([o_alls[g][h * tq:(h + 1) * tq] for h in range(NSA_HEADS)])
        ocmp_ref[0, sls[g], :] = ocmp.astype(ocmp_ref.dtype)
        sb = _topk_mask_t(imps[g], (pl.program_id(1) * CMP_GROUP + g) * tq)
        if ns < HEAD_DIM:
            sb = jnp.concatenate([sb, jnp.full((HEAD_DIM - ns, tq), NEG, F32)], axis=0)
        full = jnp.concatenate([jnp.zeros((HEAD_DIM, tq), F32), sb], axis=0)
        selb_ref[0, sls[g], :] = full.T.astype(BF16)


def _cmp_attention(qa, kvc, bc, ovt, T):
    B = qa.shape[0]
    ncp = kvc.shape[1]
    rows = CMP_GROUP * Q_TILE
    return pl.pallas_call(
        _cmp_kernel,
        grid=(B, T // rows),
        in_specs=[pl.BlockSpec((1, rows, 256), lambda b, n: (b, n, 0)),
                  pl.BlockSpec((1, ncp, LANES), lambda b, n: (b, 0, 0)),
                  pl.BlockSpec((NSA_HEADS, rows, ncp), lambda b, n: (0, n, 0)),
                  pl.BlockSpec(ovt.shape, lambda b, n: (0, 0))],
        out_specs=[pl.BlockSpec((1, rows, 256), lambda b, n: (b, n, 0)),
                   pl.BlockSpec((1, rows, LANES), lambda b, n: (b, n, 0))],
        out_shape=[jax.ShapeDtypeStruct((B, T, 256), BF16), jax.ShapeDtypeStruct((B, T, LANES), BF16)],
        compiler_params=_cparams(("arbitrary", "arbitrary")),
        name="cmp_select",
    )(qa, kvc, bc, ovt)


def _softmax_tile(s2, rows, m_s, shift=None):
    m_t = _row_max_lanes(s2)
    if shift is not None:
        m_t = m_t + shift
    m_old = m_s[rows]
    m_new = jnp.maximum(m_old, m_t)
    p = jnp.exp2(_sub_lanes(s2, m_new if shift is None else m_new - shift))
    alpha = jnp.exp2(m_old - m_new)
    m_s[rows] = m_new
    return p.astype(BF16), alpha


def _slc_kernel(b31_ref, q_ref, selb_ref, ka_ref, kv_ref, tab_ref, o_ref, m_s, acc_s):
    tq = q_ref.shape[1]
    H = NSA_HEADS
    n = pl.program_id(1)
    qs = _head_slabs(q_ref[0], H)
    selb = selb_ref[0].astype(F32)
    qst = jnp.concatenate([(qs[h] + selb).astype(BF16) for h in range(H)], axis=0)
    m_s[...] = jnp.full(m_s.shape, -jnp.inf, F32)
    acc_s[...] = jnp.zeros(acc_s.shape, F32)

    def scores(k0, width=tq):
        k0 = pl.multiple_of(k0, tq)
        return _dot_nt(qst, ka_ref[0, pl.ds(k0, width), :]), kv_ref[0, pl.ds(k0, width), :]

    def update(s_all, kv):
        ps, alphas = [], []
        for h in range(H):
            rows = slice(h * tq, (h + 1) * tq)
            p, alpha = _softmax_tile(s_all[rows], rows, m_s, shift=b31_ref[h])
            ps.append(p)
            alphas.append(alpha)
        pv = _dot(jnp.concatenate(ps, axis=0), kv)
        for h in range(H):
            rows = slice(h * tq, (h + 1) * tq)
            acc_s[rows] = alphas[h] * acc_s[rows] + pv[rows]

    def far_triple(i, carry):
        update(*scores(3 * i * tq, 3 * tq))
        return carry

    n_far = jnp.maximum(n - 1, 0)
    lax.fori_loop(0, n_far // 3, far_triple, 0)

    @pl.when(n_far % 3 == 1)
    def _():
        update(*scores((n_far - 1) * tq))

    @pl.when(n_far % 3 == 2)
    def _():
        update(*scores((n_far - 2) * tq, 2 * tq))

    def near_diagonal(k0, before):
        k0 = pl.multiple_of(k0, tq)
        half = tq // SLC_DIAG_PARTS
        parts = []
        for part in range(SLC_DIAG_PARTS):
            kw = before + half * (part + 1)
            q_part = jnp.concatenate([qst[h * tq + part * half:h * tq + (part + 1) * half] for h in range(H)], axis=0)
            parts.append((_dot_nt(q_part, ka_ref[0, pl.ds(k0, kw), :]), kv_ref[0, pl.ds(k0, kw), :], kw))
        for part, (s_all, kv, kw) in enumerate(parts):
            ps, alphas = [], []
            for h in range(H):
                rows = slice(h * tq + part * half, h * tq + (part + 1) * half)
                bias = tab_ref[h, part * half:(part + 1) * half, tq - before:tq - before + kw]
                p, alpha = _softmax_tile(s_all[h * half:(h + 1) * half] + bias, rows, m_s)
                ps.append(p)
                alphas.append(alpha)
            pv = _dot(jnp.concatenate(ps, axis=0), kv)
            for h in range(H):
                rows = slice(h * tq + part * half, h * tq + (part + 1) * half)
                acc_s[rows] = alphas[h] * acc_s[rows] + pv[h * half:(h + 1) * half]

    @pl.when(n >= 1)
    def _():
        near_diagonal(jnp.maximum(n - 1, 0) * tq, tq)

    @pl.when(n == 0)
    def _():
        near_diagonal(0, 0)

    o_ref[0] = _merge_normalized([acc_s[h * tq:(h + 1) * tq] for h in range(H)]).astype(o_ref.dtype)


def _slc_attention(b31, qa, selb, slck, slcv, tab, T):
    B = qa.shape[0]
    tq = FLASH_TILE
    H = NSA_HEADS
    tile = lambda w: pl.BlockSpec((1, tq, w), lambda b, n: (b, n, 0))
    return pl.pallas_call(
        _slc_kernel,
        grid=(B, T // tq),
        in_specs=[pl.BlockSpec(memory_space=pltpu.SMEM),
                  tile(256), tile(LANES),
                  pl.BlockSpec((1, T, LANES), lambda b, n: (b, 0, 0)),
                  pl.BlockSpec((1, T, LANES), lambda b, n: (b, 0, 0)),
                  pl.BlockSpec(tab.shape, lambda b, n: (0, 0, 0))],
        out_specs=tile(256),
        out_shape=jax.ShapeDtypeStruct((B, T, 256), BF16),
        scratch_shapes=[pltpu.VMEM((H * tq, LANES), F32), pltpu.VMEM((H * tq, LANES), F32)],
        compiler_params=_cparams(("arbitrary", "arbitrary")),
        name="slc_attention",
    )(b31, qa, selb, slck, slcv, tab)


def _band_kernel(n_heads, window, has_sink, n_sub, tq, *refs):
    if has_sink:
        q_ref, kv_ref, tab_ref, sink_ref, o_ref = refs
    else:
        q_ref, kv_ref, tab_ref, o_ref = refs
    span = window + tq
    lo_kv = lax.broadcasted_iota(jnp.int32, (span, LANES), 1) < HEAD_DIM
    lo_o = lax.broadcasted_iota(jnp.int32, (n_heads * tq, LANES), 1) < HEAD_DIM
    groups = range(n_sub)
    tiles = [pl.program_id(1) * n_sub + g for g in groups]
    kvs, dots = [], []
    for g in groups:
        k0 = pl.multiple_of(jnp.maximum(tiles[g] * tq - window, 0), tq)
        kvs.append(kv_ref[0, pl.ds(k0, span), :])
        qs = _head_slabs(q_ref[0, g * tq:(g + 1) * tq, :], n_heads)
        dots.append(_dot_nt(jnp.concatenate([q.astype(BF16) for q in qs], axis=0), kvs[g]))
    accs, ms = [], []
    for g in groups:
        s = dots[g] + tab_ref[jnp.minimum(tiles[g], window // tq)]
        m = _row_max_lanes(s)
        if has_sink:
            m = jnp.maximum(m, sink_ref[...])
        e = jnp.exp2(_sub_lanes(s, m))
        ms.append(m)
        accs.append(_dot(e.astype(BF16), jnp.where(lo_kv, 1.0, kvs[g]).astype(BF16)))
    for g in groups:
        acc = accs[g]
        if has_sink:
            acc = acc + jnp.where(lo_o, jnp.exp2(sink_ref[...] - ms[g]), 0.0)
        out = _merge_normalized([acc[h * tq:(h + 1) * tq] for h in range(n_heads)])
        o_ref[0, g * tq:(g + 1) * tq, :] = out.astype(o_ref.dtype)


def _band_attention(q, kv_pad, tab, sinks, n_heads, window, T, out_dtype):
    B = q.shape[0]
    width = n_heads * HEAD_DIM
    has_sink = sinks is not None
    nvar, tq = tab.shape[1], tab.shape[2]
    tab2 = jnp.transpose(tab, (1, 0, 2, 3)).reshape(nvar, n_heads * tq, window + tq)
    n_sub = (BAND_ROWS if has_sink else 2 * BAND_ROWS) // (n_heads * tq)
    rows = n_sub * tq
    in_specs = [pl.BlockSpec((1, rows, width), lambda b, n: (b, n, 0)),
                pl.BlockSpec((1, T, LANES), lambda b, n: (b, 0, 0)),
                pl.BlockSpec(tab2.shape, lambda b, n: (0, 0, 0))]
    args = (q, kv_pad, tab2)
    if has_sink:
        rep = jnp.broadcast_to(jnp.repeat(sinks, tq)[:, None], (n_heads * tq, LANES))
        in_specs.append(pl.BlockSpec(rep.shape, lambda b, n: (0, 0)))
        args = args + (rep,)
    return pl.pallas_call(
        functools.partial(_band_kernel, n_heads, window, has_sink, n_sub, tq),
        grid=(B, T // rows),
        in_specs=in_specs,
        out_specs=pl.BlockSpec((1, rows, width), lambda b, n: (b, n, 0)),
        out_shape=jax.ShapeDtypeStruct((B, T, width), out_dtype),
        compiler_params=_cparams(("arbitrary", "arbitrary")),
        name="band_sink" if has_sink else "band_window",
    )(*args)


def _mla_kernel(q_ref, k_ref, v_ref, o_ref, m_s, acc_s):
    tq = q_ref.shape[1]
    H = MLA_HEADS
    n = pl.program_id(1)
    m_s[...] = jnp.full(m_s.shape, -jnp.inf, F32)
    acc_s[...] = jnp.zeros(acc_s.shape, F32)

    def scores(k0, width=tq):
        k0 = pl.multiple_of(k0, tq)
        return k0, width, [_dot_nt(q_ref[0, :, LANES * h:LANES * (h + 1)],
                                   k_ref[0, pl.ds(k0, width), LANES * h:LANES * (h + 1)]) for h in range(H)]

    def update(k0, width, ss):
        ps, alphas = [], []
        for h in range(H):
            p, alpha = _softmax_tile(ss[h], slice(h * tq, (h + 1) * tq), m_s)
            ps.append(p)
            alphas.append(alpha)
        for h in range(H):
            rows = slice(h * tq, (h + 1) * tq)
            v = v_ref[0, pl.ds(k0, width), LANES * h:LANES * (h + 1)]
            acc_s[rows] = alphas[h] * acc_s[rows] + _dot(ps[h], v)

    def far_triple(i, carry):
        update(*scores(3 * i * tq, 3 * tq))
        return carry

    n_far = jnp.maximum(n - 1, 0)
    lax.fori_loop(0, n_far // 3, far_triple, 0)

    @pl.when(n_far % 3 == 1)
    def _():
        update(*scores((n_far - 1) * tq))

    @pl.when(n_far % 3 == 2)
    def _():
        update(*scores((n_far - 2) * tq, 2 * tq))

    def near_diagonal(k0, before):
        k0 = pl.multiple_of(k0, tq)
        half = tq // MLA_DIAG_PARTS
        parts = []
        for part in range(MLA_DIAG_PARTS):
            kw = before + half * (part + 1)
            r = slice(part * half, (part + 1) * half)
            parts.append((kw, [_dot_nt(q_ref[0, r, LANES * h:LANES * (h + 1)],
                                       k_ref[0, pl.ds(k0, kw), LANES * h:LANES * (h + 1)]) for h in range(H)]))
        for part, (kw, ss) in enumerate(parts):
            row = lax.broadcasted_iota(jnp.int32, (half, kw), 0)
            col = lax.broadcasted_iota(jnp.int32, (half, kw), 1)
            causal = col <= row + (before + part * half)
            ps, alphas = [], []
            for h in range(H):
                rows = slice(h * tq + part * half, h * tq + (part + 1) * half)
                p, alpha = _softmax_tile(jnp.where(causal, ss[h], NEG), rows, m_s)
                ps.append(p)
                alphas.append(alpha)
            for h in range(H):
                rows = slice(h * tq + part * half, h * tq + (part + 1) * half)
                v = v_ref[0, pl.ds(k0, kw), LANES * h:LANES * (h + 1)]
                acc_s[rows] = alphas[h] * acc_s[rows] + _dot(ps[h], v)

    @pl.when(n >= 1)
    def _():
        near_diagonal(jnp.maximum(n - 1, 0) * tq, tq)

    @pl.when(n == 0)
    def _():
        near_diagonal(0, 0)

    o_ref[0] = _merge_normalized([acc_s[h * tq:(h + 1) * tq] for h in range(H)]).astype(o_ref.dtype)


def _mla_attention(qm, km, vm, T):
    B = qm.shape[0]
    tq = FLASH_TILE
    H = MLA_HEADS
    return pl.pallas_call(
        _mla_kernel,
        grid=(B, T // tq),
        in_specs=[pl.BlockSpec((1, tq, 512), lambda b, n: (b, n, 0)),
                  pl.BlockSpec((1, T, 512), lambda b, n: (b, 0, 0)),
                  pl.BlockSpec((1, T, 512), lambda b, n: (b, 0, 0))],
        out_specs=pl.BlockSpec((1, tq, 256), lambda b, n: (b, n, 0)),
        out_shape=jax.ShapeDtypeStruct((B, T, 256), BF16),
        scratch_shapes=[pltpu.VMEM((H * tq, LANES), F32), pltpu.VMEM((H * tq, LANES), F32)],
        compiler_params=_cparams(("arbitrary", "arbitrary")),
        name="mla_attention",
    )(qm, km, vm)


def _out_kernel(x_ref, gpre_ref, wz_ref, ocmp_ref, oslc_ref, owin_ref, krb_ref, gexp_ref, ob_ref, oc_ref, w_ref,
                gpost_ref, o_ref):
    sub = x_ref.shape[0] // OUT_SUBTILES
    rs = [slice(t * sub, (t + 1) * sub) for t in range(OUT_SUBTILES)]
    w_a = NSA_HEADS * HEAD_DIM
    hs = [_rms(x_ref[r, :], gpre_ref[...]).astype(BF16) for r in rs]
    zs = [_dot_nt(h, wz_ref[...]) for h in hs]
    mixeds = []
    for r, z in zip(rs, zs):
        sig = 1.0 / (1.0 + jnp.exp(-krb_ref[r, :]))
        hi = sig.astype(BF16)
        lo = (sig - hi.astype(F32)).astype(BF16)
        gates = _dot(jnp.concatenate([hi, lo], axis=1), gexp_ref[...])
        oa = None
        for j, br in enumerate((ocmp_ref, oslc_ref, owin_ref)):
            term = gates[:, w_a * j:w_a * (j + 1)] * br[r, :].astype(F32)
            oa = term if oa is None else oa + term
        heads = jnp.concatenate([oa, ob_ref[r, :].astype(F32), oc_ref[r, :].astype(F32)], axis=1)
        mixeds.append((heads * (z * (1.0 / (1.0 + jnp.exp(-z))))).astype(BF16))
    ys = [_dot(m, w_ref[...]) for m in mixeds]
    for r, y in zip(rs, ys):
        o_ref[r, :] = x_ref[r, :] + _rms(y, gpost_ref[...])


def _out_project(x2, gpre, wz, ocmp, oslc, owin, krb, gexp, ob, oc, w, gpost, tm):
    BT = x2.shape[0]
    row = lambda i: (i, 0)
    fix = lambda i: (0, 0)
    spec = lambda a: pl.BlockSpec((tm, a.shape[1]), row)
    full = lambda a: pl.BlockSpec(a.shape, fix)
    return pl.pallas_call(
        _out_kernel,
        grid=(BT // tm,),
        in_specs=[spec(x2), full(gpre), full(wz), spec(ocmp), spec(oslc), spec(owin), spec(krb), full(gexp),
                  spec(ob), spec(oc), full(w), full(gpost)],
        out_specs=pl.BlockSpec((tm, D_MODEL), row),
        out_shape=jax.ShapeDtypeStruct((BT, D_MODEL), F32),
        compiler_params=_cparams(("arbitrary",)),
        name="out_proj",
    )(x2, gpre, wz, ocmp, oslc, owin, krb, gexp, ob, oc, w, gpost)


def _split_in(w):
    outs, o = [], 0
    for s in IN_SIZES:
        outs.append(w[..., o:o + s])
        o += s
    return outs


def _relayout_kernel(w_ref, wp_ref, wz_ref):
    offs = np.cumsum((0,) + tuple(IN_SIZES))
    seg = {name: (int(offs[i]), int(offs[i + 1])) for i, name in enumerate(
        ("a_q", "a_kv", "a_g", "b_cq", "b_ckv", "b_kr", "c_q", "c_kv", "z"))}
    dst = (("a_q", C_AQ), ("a_kv", C_CMP), ("b_cq", C_BCQ), ("b_ckv", C_BCKV), ("b_kr", C_KRB + HEAD_DIM),
           ("c_q", C_CQ), ("c_kv", C_CKV))
    n, L, ct = w_ref.shape
    g0, n_gate, tail = seg["a_g"][0], 3 * NSA_HEADS, LANES - GATE_LANE0
    gate_row = lax.broadcasted_iota(jnp.int32, (tail, ct), 0) < n_gate
    rows = w_ref.reshape(n * L, ct)
    pick = lambda a, b, l: rows[pl.ds(L * a + l, b - a, stride=L), :]
    for l in range(L):
        wp_ref[l, C_KRB:C_KRB + HEAD_DIM, :] = jnp.zeros((HEAD_DIM, ct), BF16)
        for name, c0 in dst:
            a, b = seg[name]
            wp_ref[l, c0:c0 + (b - a), :] = pick(a, b, l).astype(BF16)
        wp_ref[l, C_KRB + GATE_LANE0:C_CQ, :] = jnp.where(gate_row, pick(g0, g0 + tail, l), 0.0).astype(BF16)
        wz_ref[l] = pick(seg["z"][0], seg["z"][1], l).astype(BF16)


def _prep_w_in(w):
    L, d, n = w.shape
    ct = LANES
    return pl.pallas_call(
        _relayout_kernel,
        grid=(d // ct,),
        in_specs=[pl.BlockSpec((n, L, ct), lambda r: (0, 0, r))],
        out_specs=[pl.BlockSpec((L, C_Z, ct), lambda r: (0, 0, r)),
                   pl.BlockSpec((L, IN_SIZES[-1], ct), lambda r: (0, 0, r))],
        out_shape=[jax.ShapeDtypeStruct((L, C_Z, d), BF16), jax.ShapeDtypeStruct((L, IN_SIZES[-1], d), BF16)],
        compiler_params=_cparams(("arbitrary",)),
        name="w_in_relayout",
    )(jnp.transpose(w, (2, 0, 1)))


def _prep_mla(w_uq, w_ukv):
    L = w_uq.shape[0]
    dq = MLA_NOPE + MLA_ROPE
    uq = w_uq.reshape(L, MLA_Q_RANK, MLA_HEADS, dq)
    uq = jnp.concatenate([uq, jnp.zeros((L, MLA_Q_RANK, MLA_HEADS, LANES - dq), uq.dtype)], axis=-1)
    ukv = w_ukv.reshape(L, MLA_KV_RANK, MLA_HEADS, MLA_NOPE + MLA_V)
    uk = jnp.concatenate([ukv[..., :MLA_NOPE], jnp.zeros((L, MLA_KV_RANK, MLA_HEADS, LANES - MLA_NOPE), ukv.dtype)], axis=-1)
    uv = jnp.concatenate([jnp.zeros((L, MLA_KV_RANK, MLA_HEADS, LANES - MLA_V), ukv.dtype), ukv[..., MLA_NOPE:]], axis=-1)
    return (uq.reshape(L, MLA_Q_RANK, MLA_HEADS * LANES).astype(BF16),
            uk.reshape(L, MLA_KV_RANK, MLA_HEADS * LANES).astype(BF16),
            uv.reshape(L, MLA_KV_RANK, MLA_HEADS * LANES).astype(BF16))


def _prep_compress(cmp_pos, cmp_w1, cmp_w2):
    L = cmp_pos.shape[0]
    pe = jnp.concatenate([cmp_pos[:, 0], cmp_pos[:, 1]], axis=-1)
    pet = pe[:, :CMP_STRIDE].reshape(L, 1, CMP_STRIDE * LANES)
    peb = pe[:, CMP_STRIDE:].reshape(L, 1, CMP_STRIDE * LANES)
    w1 = cmp_w1.reshape(L, 2, CMP_LEN, HEAD_DIM, CMP_HIDDEN)
    zero = jnp.zeros_like(w1[:, 0])
    w1k = jnp.concatenate([w1[:, 0], zero], axis=2)
    w1v = jnp.concatenate([zero, w1[:, 1]], axis=2)
    w1e = jnp.stack([w1k, w1v], axis=1)
    w1t = w1e[:, :, :CMP_STRIDE].reshape(L, 2, CMP_STRIDE * LANES, CMP_HIDDEN).astype(BF16)
    w1b = w1e[:, :, CMP_STRIDE:].reshape(L, 2, CMP_STRIDE * LANES, CMP_HIDDEN).astype(BF16)
    z2 = jnp.zeros_like(cmp_w2[:, 0])
    w2 = jnp.stack([jnp.concatenate([cmp_w2[:, 0], z2], axis=-1),
                    jnp.concatenate([z2, cmp_w2[:, 1]], axis=-1)], axis=1).astype(BF16)
    return pet, peb, w1t, w1b, w2


def _rope_tables(T):
    half = MLA_ROPE // 2
    inv = ROPE_THETA ** (-jnp.arange(half, dtype=F32) / half)
    ang = jnp.arange(T).astype(F32)[:, None] * inv[None, :]
    cos, sin = jnp.cos(ang), jnp.sin(ang)
    z = lambda n: jnp.zeros((T, n), F32)
    rc = jnp.concatenate([jnp.ones((T, MLA_NOPE), F32), cos, cos, z(LANES - MLA_NOPE - MLA_ROPE)], axis=1)
    rsa = jnp.concatenate([z(MLA_NOPE + half), sin, z(LANES - MLA_NOPE - MLA_ROPE)], axis=1)
    rsb = jnp.concatenate([z(MLA_NOPE), -sin, z(LANES - MLA_NOPE - half)], axis=1)
    return rc, rsa, rsb


def _static_tables(T):
    ns = T // SEL_BLOCK
    nc = T // CMP_STRIDE - 1
    ncp = T // CMP_STRIDE
    sstart = np.arange(ns) * SEL_BLOCK
    cstart = np.arange(nc) * CMP_STRIDE
    overlap = (np.clip(np.minimum(cstart[:, None] + CMP_LEN, sstart[None, :] + SEL_BLOCK)
                       - np.maximum(cstart[:, None], sstart[None, :]), 0, None) / CMP_STRIDE).astype(np.float32)
    ovt = np.zeros((ns, ncp), np.float32)
    ovt[:, :nc] = overlap.T
    eaug = np.zeros((T, LANES), np.float32)
    eaug[np.arange(T), HEAD_DIM + np.arange(T) // SEL_BLOCK] = 1.0
    gexp = np.zeros((2, LANES, 3 * NSA_HEADS * HEAD_DIM), np.float32)
    for h in range(NSA_HEADS):
        for j in range(3):
            c0 = NSA_HEADS * HEAD_DIM * j + HEAD_DIM * h
            gexp[:, GATE_LANE0 + 3 * h + j, c0:c0 + HEAD_DIM] = 1.0
    gexp = gexp.reshape(2 * LANES, -1)
    return jnp.asarray(ovt, BF16), jnp.asarray(eaug, BF16), jnp.asarray(gexp, BF16)


def kernel(x, w_in, w_out, norm_pre, norm_post, cmp_pos, cmp_w1, cmp_w2, mla_q_norm, mla_w_uq, mla_kv_norm,
           mla_w_ukv, swa_sinks, rel_bias):
    B, T, D = x.shape
    depth = w_in.shape[0]
    assert D == D_MODEL and T % (2 * FLASH_TILE) == 0 and T // SEL_BLOCK <= HEAD_DIM
    ncp = T // CMP_STRIDE

    w_in_p, w_z = _prep_w_in(w_in)
    wuq, wk, wv = _prep_mla(mla_w_uq, mla_w_ukv)
    pet, peb, w1t, w1b, w2 = _prep_compress(cmp_pos, cmp_w1, cmp_w2)
    w_out_b = w_out.astype(BF16)
    rope_c, rope_sa, rope_sb = _rope_tables(T)
    ovt, eaug, gexp = _static_tables(T)

    bc = _bias_table(rel_bias, 0, NSA_HEADS, ncp, T, CMP_STRIDE, 1, -(CMP_LEN - 1), 0, 1 << 30, 32, mult=LOG2E)
    bc = jnp.transpose(bc.reshape(NSA_HEADS, ncp, T), (0, 2, 1))
    tab_near = _bias_table(rel_bias, 0, NSA_HEADS, FLASH_TILE, 2 * FLASH_TILE, 1, -1, FLASH_TILE, 0, 1 << 30, 256,
                           mult=LOG2E).reshape(NSA_HEADS, FLASH_TILE, 2 * FLASH_TILE)
    tab_win = _bias_table(rel_bias, 0, NSA_HEADS, WIN_TILE, NSA_WINDOW + WIN_TILE, 1, -1, 0, 0, NSA_WINDOW, WIN_TILE,
                          mult=LOG2E, nblk=NSA_WINDOW // WIN_TILE + 1, blk_d0=WIN_TILE)
    tab_swa = _bias_table(rel_bias, NSA_HEADS, SWA_HEADS, Q_TILE, SWA_WINDOW + Q_TILE, 1, -1, 0, 0, SWA_WINDOW, Q_TILE,
                          mult=LOG2E, nblk=SWA_WINDOW // Q_TILE + 1, blk_d0=Q_TILE)
    b31 = rel_bias[:NSA_HEADS, REL_BUCKETS - 1] * LOG2E

    x2 = x.reshape(B * T, D)
    for l in range(depth):
        (qa, cmpkv, slck, slcv, win, qm, km, vm, krb, cq, ckv) = _project(
            x2, norm_pre[l][None], w_in_p[l], mla_q_norm[l][None], wuq[l], mla_kv_norm[l][None], wk[l], wv[l],
            rope_c, rope_sa, rope_sb, eaug, T, PROJ_ROWS)
        r3 = lambda a: a.reshape(B, T, a.shape[-1])
        kvc = _compress(r3(cmpkv), pet[l], peb[l], w1t[l], w1b[l], w2[l])
        qa3 = r3(qa)
        ocmp, selb = _cmp_attention(qa3, kvc, bc, ovt, T)
        owin = _band_attention(qa3, r3(win), tab_win, None, NSA_HEADS, NSA_WINDOW, T, BF16)
        oslc = _slc_attention(b31, qa3, selb, r3(slck), r3(slcv), tab_near, T)
        oc = _band_attention(r3(cq), r3(ckv), tab_swa, swa_sinks[l] * LOG2E, SWA_HEADS, SWA_WINDOW, T, BF16)
        ob = _mla_attention(r3(qm), r3(km), r3(vm), T)
        flat = lambda a: a.reshape(B * T, a.shape[-1])
        x2 = _out_project(x2, norm_pre[l][None], w_z[l], flat(ocmp), flat(oslc), flat(owin), krb, gexp,
                          flat(ob), flat(oc), w_out_b[l], norm_post[l][None], OUT_ROWS)
    return x2.reshape(B, T, D)
```

```python
import functools
import math

import numpy as np
import jax
import jax.numpy as jnp
from jax import lax
from jax.experimental import pallas as pl
from jax.experimental.pallas import tpu as pltpu

F32 = jnp.float32
BF16 = jnp.bfloat16

D_MODEL = 1024
HEAD_DIM = 64
NSA_HEADS = 4
CMP_LEN = 32
CMP_STRIDE = 16
CMP_HIDDEN = 128
SEL_BLOCK = 64
SEL_TOP = 16
NSA_WINDOW = 512
MLA_HEADS = 4
MLA_Q_RANK = 256
MLA_KV_RANK = 128
MLA_NOPE = 64
MLA_ROPE = 32
MLA_V = 64
ROPE_THETA = 10000.0
SWA_HEADS = 8
SWA_WINDOW = 128
REL_BUCKETS = 32
REL_MAX_DIST = 512
NORM_EPS = 1e-6
NEG = -1e30
BIG = 1e9
LOG2E = math.log2(math.e)
MLA_SCALE2 = (MLA_NOPE + MLA_ROPE) ** -0.5 * LOG2E
IN_SIZES = (256, 384, 12, 256, 128, 32, 512, 128, 1024)

LANES = 128
Q_TILE = 128
WIN_TILE = 128
BAND_ROWS = 4096
PROJ_ROWS = 1024
OUT_SUBTILES = 4
OUT_ROWS = 1024
CMP_GROUP = 4
FLASH_TILE = 512
SLC_DIAG_PARTS = 4
MLA_DIAG_PARTS = 4
GATE_LANE0 = 96
VMEM_LIMIT = 56 * 1024 * 1024

C_AQ, C_CMP, C_SLC, C_WIN, C_BCQ, C_BCKV, C_KRB, C_CQ, C_CKV, C_Z = (
    0, 256, 384, 512, 640, 896, 1024, 1152, 1664, 1792)


def _dot(a, b):
    return jnp.dot(a, b, preferred_element_type=F32)


def _dot_nt(a, b):
    return lax.dot_general(a, b, (((1,), (1,)), ((), ())), preferred_element_type=F32)


def _bucket_thresholds():
    d = np.arange(0, 4 * REL_MAX_DIST)
    exact = REL_BUCKETS // 2
    large = exact + (np.log(np.maximum(d, 1).astype(np.float32) / np.float32(exact))
                     / np.float32(math.log(REL_MAX_DIST / exact)) * np.float32(REL_BUCKETS - exact)).astype(np.int32)
    b = np.where(d < exact, d, np.minimum(large, REL_BUCKETS - 1))
    assert np.all(np.diff(b) >= 0) and b[-1] == REL_BUCKETS - 1
    return [int(np.argmax(b >= k)) for k in range(REL_BUCKETS)]


_THRESH = _bucket_thresholds()


def _cparams(sem):
    return pltpu.CompilerParams(dimension_semantics=sem, vmem_limit_bytes=VMEM_LIMIT)


def _table_kernel(h0, row_step, sign, d0, blk_d0, lo, hi, mult, rb_ref, out_ref):
    h = pl.program_id(0) + h0
    blk = pl.program_id(1)
    rt, cc = out_ref.shape[2], out_ref.shape[3]
    base = d0 + blk * blk_d0 - sign * row_step * rt * pl.program_id(2)
    width = -(-(row_step * rt + cc) // LANES) * LANES
    w = lax.broadcasted_iota(jnp.int32, (8, width), 1)
    dist = base + sign * jnp.where(w < cc, w, w - width)
    acc = jnp.full(dist.shape, rb_ref[h, 0], F32)
    for k in range(1, REL_BUCKETS):
        acc = jnp.where(dist >= _THRESH[k], rb_ref[h, k], acc)
    row = jnp.where((dist >= lo) & (dist < hi), acc * mult, NEG)
    rolled = pltpu.roll(jnp.broadcast_to(row[:1], (rt, width)), 0, 1, stride=row_step, stride_axis=0)
    out_ref[0, 0] = rolled[:, :cc]


def _bias_table(rel_bias, h0, nh, rows, cols, row_step, sign, d0, lo, hi, rt, mult=1.0, nblk=1, blk_d0=0):
    return pl.pallas_call(
        functools.partial(_table_kernel, h0, row_step, sign, d0, blk_d0, lo, hi, mult),
        grid=(nh, nblk, rows // rt),
        in_specs=[pl.BlockSpec(memory_space=pltpu.SMEM)],
        out_specs=pl.BlockSpec((1, 1, rt, cols), lambda h, b, r: (h, b, r, 0)),
        out_shape=jax.ShapeDtypeStruct((nh, nblk, rows, cols), F32),
        compiler_params=_cparams(("arbitrary", "arbitrary", "arbitrary")),
        name="bias_table",
    )(rel_bias)


def _rms(v, g):
    return v * lax.rsqrt(jnp.mean(v * v, axis=-1, keepdims=True) + NORM_EPS) * g


def _proj_kernel(x_ref, gpre_ref, w_ref, qn_ref, wuq_ref, kvn_ref, wk_ref, wv_ref,
                 rc_ref, rsa_ref, rsb_ref, eaug_ref,
                 qa_ref, cmp_ref, slck_ref, slcv_ref, win_ref, qm_ref, km_ref, vm_ref,
                 krb_ref, cq_ref, ckv_ref):
    hb = _rms(x_ref[...], gpre_ref[...]).astype(BF16)
    bounds = ((C_AQ, C_CMP), (C_CMP, C_WIN), (C_WIN, C_KRB), (C_KRB, C_Z))
    chunks = {ab: _dot_nt(hb, w_ref[ab[0]:ab[1], :]) for ab in bounds}
    lane = lax.broadcasted_iota(jnp.int32, (hb.shape[0], LANES), 1)
    lo = lane < HEAD_DIM

    def proj(c0, c1):
        for (a, b), y in chunks.items():
            if a <= c0 and c1 <= b:
                return y[:, c0 - a:c1 - a]
        raise AssertionError((c0, c1))

    qa_ref[...] = (proj(C_AQ, C_CMP) * (0.125 * LOG2E)).astype(BF16)
    cmp_ref[...] = proj(C_CMP, C_SLC)
    slc = proj(C_SLC, C_WIN)
    slcv_ref[...] = jnp.where(lo, 1.0, slc).astype(BF16)
    slck_ref[...] = jnp.where(lo, slc, eaug_ref[...].astype(F32)).astype(BF16)
    win_ref[...] = proj(C_WIN, C_BCQ).astype(BF16)
    cq_ref[...] = (proj(C_CQ, C_CKV) * (0.125 * LOG2E)).astype(BF16)
    ckv_ref[...] = proj(C_CKV, C_Z).astype(BF16)

    rc, rsa, rsb = rc_ref[...], rsa_ref[...], rsb_ref[...]

    def rope(v):
        return v * rc + pltpu.roll(v, 16, 1) * rsa + pltpu.roll(v, 112, 1) * rsb

    cqn = _rms(proj(C_BCQ, C_BCKV), qn_ref[...]).astype(BF16)
    qm = _dot(cqn, wuq_ref[...])
    for h in range(MLA_HEADS):
        qh = rope(qm[:, LANES * h:LANES * (h + 1)]) * MLA_SCALE2
        qm_ref[:, LANES * h:LANES * (h + 1)] = qh.astype(BF16)

    krb = proj(C_KRB, C_CQ)
    krb_ref[...] = krb
    krr = jnp.where(lo | (lane >= HEAD_DIM + MLA_ROPE), 0.0, rope(krb))

    ckvn = _rms(proj(C_BCKV, C_KRB), kvn_ref[...]).astype(BF16)
    kn = _dot(ckvn, wk_ref[...])
    for h in range(MLA_HEADS):
        km_ref[:, LANES * h:LANES * (h + 1)] = (kn[:, LANES * h:LANES * (h + 1)] + krr).astype(BF16)
    vm = _dot(ckvn, wv_ref[...])
    for h in range(MLA_HEADS):
        vm_ref[:, LANES * h:LANES * (h + 1)] = jnp.where(lo, 1.0, vm[:, LANES * h:LANES * (h + 1)]).astype(BF16)


def _project(x2, gpre, w, qn, wuq, kvn, wk, wv, rope_c, rope_sa, rope_sb, eaug, T, tm):
    BT = x2.shape[0]
    nt = T // tm
    row = lambda i: (i, 0)
    fix = lambda i: (0, 0)
    pos = lambda i: (i % nt, 0)
    widths = [(256, BF16), (128, F32), (128, BF16), (128, BF16), (128, BF16), (512, BF16), (512, BF16),
              (512, BF16), (128, F32), (512, BF16), (128, BF16)]
    return pl.pallas_call(
        _proj_kernel,
        grid=(BT // tm,),
        in_specs=[pl.BlockSpec((tm, D_MODEL), row), pl.BlockSpec((1, D_MODEL), fix),
                  pl.BlockSpec(w.shape, fix), pl.BlockSpec(qn.shape, fix), pl.BlockSpec(wuq.shape, fix),
                  pl.BlockSpec(kvn.shape, fix), pl.BlockSpec(wk.shape, fix), pl.BlockSpec(wv.shape, fix),
                  pl.BlockSpec((tm, LANES), pos), pl.BlockSpec((tm, LANES), pos), pl.BlockSpec((tm, LANES), pos),
                  pl.BlockSpec((tm, LANES), pos)],
        out_specs=[pl.BlockSpec((tm, wd), row) for wd, _ in widths],
        out_shape=[jax.ShapeDtypeStruct((BT, wd), dt) for wd, dt in widths],
        compiler_params=_cparams(("arbitrary",)),
        name="proj",
    )(x2, gpre, w, qn, wuq, kvn, wk, wv, rope_c, rope_sa, rope_sb, eaug)


def _compress_kernel(c_ref, pet_ref, peb_ref, w1t_ref, w1b_ref, w2_ref, o_ref):
    nch = c_ref.shape[1] // CMP_STRIDE
    c = jnp.concatenate([c_ref[0, pl.ds(j, nch, stride=CMP_STRIDE), :] for j in range(CMP_STRIDE)], axis=1)
    top = (c + pet_ref[...]).astype(BF16)
    bot = (c + peb_ref[...]).astype(BF16)
    out = jnp.zeros((nch, LANES), F32)
    for i in range(2):
        pre = _dot(top, w1t_ref[i]) + pltpu.roll(_dot(bot, w1b_ref[i]), nch - 1, 0)
        hid = pre * (1.0 / (1.0 + jnp.exp(-pre)))
        out = out + _dot(hid.astype(BF16), w2_ref[i])
    o_ref[0] = out.astype(BF16)


def _compress(tokens, pet, peb, w1t, w1b, w2):
    B, T, width = tokens.shape
    nch = T // CMP_STRIDE
    full = lambda a: pl.BlockSpec(a.shape, lambda b: (0,) * a.ndim)
    return pl.pallas_call(
        _compress_kernel,
        grid=(B,),
        in_specs=[pl.BlockSpec((1, T, width), lambda b: (b, 0, 0)),
                  full(pet), full(peb), full(w1t), full(w1b), full(w2)],
        out_specs=pl.BlockSpec((1, nch, LANES), lambda b: (b, 0, 0)),
        out_shape=jax.ShapeDtypeStruct((B, nch, LANES), BF16),
        compiler_params=_cparams(("arbitrary",)),
        name="compress",
    )(tokens, pet, peb, w1t, w1b, w2)


def _head_slabs(qblk, n_heads):
    q = qblk.astype(F32)
    lo = lax.broadcasted_iota(jnp.int32, (q.shape[0], LANES), 1) < HEAD_DIM
    out = []
    for j in range(n_heads // 2):
        slab = q[:, LANES * j:LANES * (j + 1)]
        out.append(jnp.where(lo, slab, 0.0))
        out.append(jnp.where(lo, pltpu.roll(slab, HEAD_DIM, 1), 0.0))
    return out


def _merge_upper(accs):
    lo = lax.broadcasted_iota(jnp.int32, accs[0].shape, 1) < HEAD_DIM
    slabs = [jnp.where(lo, pltpu.roll(accs[2 * j], HEAD_DIM, 1), accs[2 * j + 1]) for j in range(len(accs) // 2)]
    return jnp.concatenate(slabs, axis=1)


def _merge_normalized(accs):
    lo = lax.broadcasted_iota(jnp.int32, accs[0].shape, 1) < HEAD_DIM
    slabs = []
    for j in range(len(accs) // 2):
        a, b = accs[2 * j], accs[2 * j + 1]
        out = jnp.where(lo, pltpu.roll(a, HEAD_DIM, 1), b)
        den = jnp.where(lo, a, pltpu.roll(b, HEAD_DIM, 1))
        slabs.append(out / den)
    return jnp.concatenate(slabs, axis=1)


def _row_max_lanes(s):
    return jnp.broadcast_to(jnp.max(s, axis=-1, keepdims=True), (s.shape[0], LANES))


def _sub_lanes(s, ref):
    return jnp.concatenate([s[:, LANES * i:LANES * (i + 1)] - ref for i in range(s.shape[1] // LANES)], axis=1)


def _topk_mask_t(imp, t0):
    ns, tq = imp.shape
    srow = lax.broadcasted_iota(jnp.int32, (ns, tq), 0)
    cur = (t0 + lax.broadcasted_iota(jnp.int32, (ns, tq), 1)) // SEL_BLOCK
    forced = (srow == 0) | (srow == cur) | (srow == cur - 1)
    x = jnp.where(forced, BIG, jnp.where(srow <= cur, imp, -BIG))
    sub = 8
    groups = [x[sub * v:sub * (v + 1)] for v in range(ns // sub)]
    rows_in = [srow[sub * v:sub * (v + 1)] for v in range(ns // sub)]
    cnts = [jnp.zeros((sub, tq), F32) for _ in groups]
    for sp in range(ns):
        other = jnp.broadcast_to(x[sp:sp + 1, :], (sub, tq))
        for v, xv in enumerate(groups):
            if sub * v > sp:
                beats = other >= xv
            elif sub * v + sub - 1 < sp:
                beats = other > xv
            else:
                beats = (other > xv) | ((other == xv) & (rows_in[v] > sp))
            cnts[v] = cnts[v] + jnp.where(beats, 1.0, 0.0)
    cnt = jnp.concatenate(cnts, axis=0)
    sel = (cnt < float(min(SEL_TOP, ns))) & (srow <= cur)
    return jnp.where(sel, 0.0, NEG)


def _cmp_kernel(q_ref, kvc_ref, bc_ref, ovt_ref, ocmp_ref, selb_ref):
    rows = CMP_GROUP * Q_TILE
    ncp, ns = kvc_ref.shape[1], ovt_ref.shape[0]
    step = pl.program_id(1)
    lo_step, nse = 0, SEL_TOP
    while lo_step * rows < ns * SEL_BLOCK:
        nse = min(nse, ns)
        hi_step = nse * SEL_BLOCK // rows
        ncw = min(ncp, -(-nse * (SEL_BLOCK // CMP_STRIDE) // LANES) * LANES)
        if hi_step > lo_step:
            pl.when((step >= lo_step) & (step < hi_step))(
                functools.partial(_cmp_body, ncw, nse, q_ref, kvc_ref, bc_ref, ovt_ref, ocmp_ref, selb_ref))
            lo_step = hi_step
        nse += SEL_TOP


def _cmp_body(ncw, ns, q_ref, kvc_ref, bc_ref, ovt_ref, ocmp_ref, selb_ref):
    tq = Q_TILE
    kvc = kvc_ref[0, :ncw, :]
    ovt = ovt_ref[:ns, :ncw]
    groups = range(CMP_GROUP)
    sls = [slice(g * tq, (g + 1) * tq) for g in groups]
    s_alls = []
    for g in groups:
        qs = _head_slabs(q_ref[0, sls[g], :], NSA_HEADS)
        s_alls.append(_dot_nt(jnp.concatenate([q.astype(BF16) for q in qs], axis=0), kvc))
    psums, o_alls = [], []
    for g in groups:
        psum = None
        ps = []
        for h in range(NSA_HEADS):
            s = s_alls[g][h * tq:(h + 1) * tq] + bc_ref[h, sls[g], :ncw]
            m = jnp.max(s, axis=-1, keepdims=True)
            e = jnp.exp2(s - m)
            l = jnp.sum(e, axis=-1, keepdims=True)
            p = e * jnp.where(m > 0.5 * NEG, 1.0 / l, 0.0)
            psum = p if psum is None else psum + p
            ps.append(p.astype(BF16))
        psums.append(psum)
        o_alls.append(_dot(jnp.concatenate(ps, axis=0), kvc))
    imps = []
    for g in groups:
        hi = psums[g].astype(BF16)
        lo = (psums[g] - hi.astype(F32)).astype(BF16)
        imps.append(_dot_nt(ovt, hi) + _dot_nt(ovt, lo))
    for g in groups:
        ocmp = _merge_upper([o_alls[g][h * tq:(h + 1) * tq] for h in range(NSA_HEADS)])
        ocmp_ref[0, sls[g], :] = ocmp.astype(ocmp_ref.dtype)
        sb = _topk_mask_t(imps[g], (pl.program_id(1) * CMP_GROUP + g) * tq)
        if ns < HEAD_DIM:
            sb = jnp.concatenate([sb, jnp.full((HEAD_DIM - ns, tq), NEG, F32)], axis=0)
        full = jnp.concatenate([jnp.zeros((HEAD_DIM, tq), F32), sb], axis=0)
        selb_ref[0, sls[g], :] = full.T.astype(BF16)


def _cmp_attention(qa, kvc, bc, ovt, T):
    B = qa.shape[0]
    ncp = kvc.shape[1]
    rows = CMP_GROUP * Q_TILE
    return pl.pallas_call(
        _cmp_kernel,
        grid=(B, T // rows),
        in_specs=[pl.BlockSpec((1, rows, 256), lambda b, n: (b, n, 0)),
                  pl.BlockSpec((1, ncp, LANES), lambda b, n: (b, 0, 0)),
                  pl.BlockSpec((NSA_HEADS, rows, ncp), lambda b, n: (0, n, 0)),
                  pl.BlockSpec(ovt.shape, lambda b, n: (0, 0))],
        out_specs=[pl.BlockSpec((1, rows, 256), lambda b, n: (b, n, 0)),
                   pl.BlockSpec((1, rows, LANES), lambda b, n: (b, n, 0))],
        out_shape=[jax.ShapeDtypeStruct((B, T, 256), BF16), jax.ShapeDtypeStruct((B, T, LANES), BF16)],
        compiler_params=_cparams(("arbitrary", "arbitrary")),
        name="cmp_select",
    )(qa, kvc, bc, ovt)


def _softmax_tile(s2, rows, m_s, shift=None):
    m_t = _row_max_lanes(s2)
    if shift is not None:
        m_t = m_t + shift
    m_old = m_s[rows]
    m_new = jnp.maximum(m_old, m_t)
    p = jnp.exp2(_sub_lanes(s2, m_new if shift is None else m_new - shift))
    alpha = jnp.exp2(m_old - m_new)
    m_s[rows] = m_new
    return p.astype(BF16), alpha


def _slc_kernel(b31_ref, q_ref, selb_ref, ka_ref, kv_ref, tab_ref, o_ref, m_s, acc_s):
    tq = q_ref.shape[1]
    H = NSA_HEADS
    n = pl.program_id(1)
    qs = _head_slabs(q_ref[0], H)
    selb = selb_ref[0].astype(F32)
    qst = jnp.concatenate([(qs[h] + selb).astype(BF16) for h in range(H)], axis=0)
    m_s[...] = jnp.full(m_s.shape, -jnp.inf, F32)
    acc_s[...] = jnp.zeros(acc_s.shape, F32)

    def scores(k0, width=tq):
        k0 = pl.multiple_of(k0, tq)
        return _dot_nt(qst, ka_ref[0, pl.ds(k0, width), :]), kv_ref[0, pl.ds(k0, width), :]

    def update(s_all, kv):
        ps, alphas = [], []
        for h in range(H):
            rows = slice(h * tq, (h + 1) * tq)
            p, alpha = _softmax_tile(s_all[rows], rows, m_s, shift=b31_ref[h])
            ps.append(p)
            alphas.append(alpha)
        pv = _dot(jnp.concatenate(ps, axis=0), kv)
        for h in range(H):
            rows = slice(h * tq, (h + 1) * tq)
            acc_s[rows] = alphas[h] * acc_s[rows] + pv[rows]

    def far_triple(i, carry):
        update(*scores(3 * i * tq, 3 * tq))
        return carry

    n_far = jnp.maximum(n - 1, 0)
    lax.fori_loop(0, n_far // 3, far_triple, 0)

    @pl.when(n_far % 3 == 1)
    def _():
        update(*scores((n_far - 1) * tq))

    @pl.when(n_far % 3 == 2)
    def _():
        update(*scores((n_far - 2) * tq, 2 * tq))

    def near_diagonal(k0, before):
        k0 = pl.multiple_of(k0, tq)
        half = tq // SLC_DIAG_PARTS
        parts = []
        for part in range(SLC_DIAG_PARTS):
            kw = before + half * (part + 1)
            q_part = jnp.concatenate([qst[h * tq + part * half:h * tq + (part + 1) * half] for h in range(H)], axis=0)
            parts.append((_dot_nt(q_part, ka_ref[0, pl.ds(k0, kw), :]), kv_ref[0, pl.ds(k0, kw), :], kw))
        for part, (s_all, kv, kw) in enumerate(parts):
            ps, alphas = [], []
            for h in range(H):
                rows = slice(h * tq + part * half, h * tq + (part + 1) * half)
                bias = tab_ref[h, part * half:(part + 1) * half, tq - before:tq - before + kw]
                p, alpha = _softmax_tile(s_all[h * half:(h + 1) * half] + bias, rows, m_s)
                ps.append(p)
                alphas.append(alpha)
            pv = _dot(jnp.concatenate(ps, axis=0), kv)
            for h in range(H):
                rows = slice(h * tq + part * half, h * tq + (part + 1) * half)
                acc_s[rows] = alphas[h] * acc_s[rows] + pv[h * half:(h + 1) * half]

    @pl.when(n >= 1)
    def _():
        near_diagonal(jnp.maximum(n - 1, 0) * tq, tq)

    @pl.when(n == 0)
    def _():
        near_diagonal(0, 0)

    o_ref[0] = _merge_normalized([acc_s[h * tq:(h + 1) * tq] for h in range(H)]).astype(o_ref.dtype)


def _slc_attention(b31, qa, selb, slck, slcv, tab, T):
    B = qa.shape[0]
    tq = FLASH_TILE
    H = NSA_HEADS
    tile = lambda w: pl.BlockSpec((1, tq, w), lambda b, n: (b, n, 0))
    return pl.pallas_call(
        _slc_kernel,
        grid=(B, T // tq),
        in_specs=[pl.BlockSpec(memory_space=pltpu.SMEM),
                  tile(256), tile(LANES),
                  pl.BlockSpec((1, T, LANES), lambda b, n: (b, 0, 0)),
                  pl.BlockSpec((1, T, LANES), lambda b, n: (b, 0, 0)),
                  pl.BlockSpec(tab.shape, lambda b, n: (0, 0, 0))],
        out_specs=tile(256),
        out_shape=jax.ShapeDtypeStruct((B, T, 256), BF16),
        scratch_shapes=[pltpu.VMEM((H * tq, LANES), F32), pltpu.VMEM((H * tq, LANES), F32)],
        compiler_params=_cparams(("arbitrary", "arbitrary")),
        name="slc_attention",
    )(b31, qa, selb, slck, slcv, tab)


def _band_kernel(n_heads, window, has_sink, n_sub, tq, *refs):
    if has_sink:
        q_ref, kv_ref, tab_ref, sink_ref, o_ref = refs
    else:
        q_ref, kv_ref, tab_ref, o_ref = refs
    span = window + tq
    lo_kv = lax.broadcasted_iota(jnp.int32, (span, LANES), 1) < HEAD_DIM
    lo_o = lax.broadcasted_iota(jnp.int32, (n_heads * tq, LANES), 1) < HEAD_DIM
    groups = range(n_sub)
    tiles = [pl.program_id(1) * n_sub + g for g in groups]
    kvs, dots = [], []
    for g in groups:
        k0 = pl.multiple_of(jnp.maximum(tiles[g] * tq - window, 0), tq)
        kvs.append(kv_ref[0, pl.ds(k0, span), :])
        qs = _head_slabs(q_ref[0, g * tq:(g + 1) * tq, :], n_heads)
        dots.append(_dot_nt(jnp.concatenate([q.astype(BF16) for q in qs], axis=0), kvs[g]))
    accs, ms = [], []
    for g in groups:
        s = dots[g] + tab_ref[jnp.minimum(tiles[g], window // tq)]
        m = _row_max_lanes(s)
        if has_sink:
            m = jnp.maximum(m, sink_ref[...])
        e = jnp.exp2(_sub_lanes(s, m))
        ms.append(m)
        accs.append(_dot(e.astype(BF16), jnp.where(lo_kv, 1.0, kvs[g]).astype(BF16)))
    for g in groups:
        acc = accs[g]
        if has_sink:
            acc = acc + jnp.where(lo_o, jnp.exp2(sink_ref[...] - ms[g]), 0.0)
        out = _merge_normalized([acc[h * tq:(h + 1) * tq] for h in range(n_heads)])
        o_ref[0, g * tq:(g + 1) * tq, :] = out.astype(o_ref.dtype)


def _band_attention(q, kv_pad, tab, sinks, n_heads, window, T, out_dtype):
    B = q.shape[0]
    width = n_heads * HEAD_DIM
    has_sink = sinks is not None
    nvar, tq = tab.shape[1], tab.shape[2]
    tab2 = jnp.transpose(tab, (1, 0, 2, 3)).reshape(nvar, n_heads * tq, window + tq)
    n_sub = (BAND_ROWS if has_sink else 2 * BAND_ROWS) // (n_heads * tq)
    rows = n_sub * tq
    in_specs = [pl.BlockSpec((1, rows, width), lambda b, n: (b, n, 0)),
                pl.BlockSpec((1, T, LANES), lambda b, n: (b, 0, 0)),
                pl.BlockSpec(tab2.shape, lambda b, n: (0, 0, 0))]
    args = (q, kv_pad, tab2)
    if has_sink:
        rep = jnp.broadcast_to(jnp.repeat(sinks, tq)[:, None], (n_heads * tq, LANES))
        in_specs.append(pl.BlockSpec(rep.shape, lambda b, n: (0, 0)))
        args = args + (rep,)
    return pl.pallas_call(
        functools.partial(_band_kernel, n_heads, window, has_sink, n_sub, tq),
        grid=(B, T // rows),
        in_specs=in_specs,
        out_specs=pl.BlockSpec((1, rows, width), lambda b, n: (b, n, 0)),
        out_shape=jax.ShapeDtypeStruct((B, T, width), out_dtype),
        compiler_params=_cparams(("arbitrary", "arbitrary")),
        name="band_sink" if has_sink else "band_window",
    )(*args)


def _mla_kernel(q_ref, k_ref, v_ref, o_ref, m_s, acc_s):
    tq = q_ref.shape[1]
    H = MLA_HEADS
    n = pl.program_id(1)
    m_s[...] = jnp.full(m_s.shape, -jnp.inf, F32)
    acc_s[...] = jnp.zeros(acc_s.shape, F32)

    def scores(k0, width=tq):
        k0 = pl.multiple_of(k0, tq)
        return k0, width, [_dot_nt(q_ref[0, :, LANES * h:LANES * (h + 1)],
                                   k_ref[0, pl.ds(k0, width), LANES * h:LANES * (h + 1)]) for h in range(H)]

    def update(k0, width, ss):
        ps, alphas = [], []
        for h in range(H):
            p, alpha = _softmax_tile(ss[h], slice(h * tq, (h + 1) * tq), m_s)
            ps.append(p)
            alphas.append(alpha)
        for h in range(H):
            rows = slice(h * tq, (h + 1) * tq)
            v = v_ref[0, pl.ds(k0, width), LANES * h:LANES * (h + 1)]
            acc_s[rows] = alphas[h] * acc_s[rows] + _dot(ps[h], v)

    def far_triple(i, carry):
        update(*scores(3 * i * tq, 3 * tq))
        return carry

    n_far = jnp.maximum(n - 1, 0)
    lax.fori_loop(0, n_far // 3, far_triple, 0)

    @pl.when(n_far % 3 == 1)
    def _():
        update(*scores((n_far - 1) * tq))

    @pl.when(n_far % 3 == 2)
    def _():
        update(*scores((n_far - 2) * tq, 2 * tq))

    def near_diagonal(k0, before):
        k0 = pl.multiple_of(k0, tq)
        half = tq // MLA_DIAG_PARTS
        parts = []
        for part in range(MLA_DIAG_PARTS):
            kw = before + half * (part + 1)
            r = slice(part * half, (part + 1) * half)
            parts.append((kw, [_dot_nt(q_ref[0, r, LANES * h:LANES * (h + 1)],
                                       k_ref[0, pl.ds(k0, kw), LANES * h:LANES * (h + 1)]) for h in range(H)]))
        for part, (kw, ss) in enumerate(parts):
            row = lax.broadcasted_iota(jnp.int32, (half, kw), 0)
            col = lax.broadcasted_iota(jnp.int32, (half, kw), 1)
            causal = col <= row + (before + part * half)
            ps, alphas = [], []
            for h in range(H):
                rows = slice(h * tq + part * half, h * tq + (part + 1) * half)
                p, alpha = _softmax_tile(jnp.where(causal, ss[h], NEG), rows, m_s)
                ps.append(p)
                alphas.append(alpha)
            for h in range(H):
                rows = slice(h * tq + part * half, h * tq + (part + 1) * half)
                v = v_ref[0, pl.ds(k0, kw), LANES * h:LANES * (h + 1)]
                acc_s[rows] = alphas[h] * acc_s[rows] + _dot(ps[h], v)

    @pl.when(n >= 1)
    def _():
        near_diagonal(jnp.maximum(n - 1, 0) * tq, tq)

    @pl.when(n == 0)
    def _():
        near_diagonal(0, 0)

    o_ref[0] = _merge_normalized([acc_s[h * tq:(h + 1) * tq] for h in range(H)]).astype(o_ref.dtype)


def _mla_attention(qm, km, vm, T):
    B = qm.shape[0]
    tq = FLASH_TILE
    H = MLA_HEADS
    return pl.pallas_call(
        _mla_kernel,
        grid=(B, T // tq),
        in_specs=[pl.BlockSpec((1, tq, 512), lambda b, n: (b, n, 0)),
                  pl.BlockSpec((1, T, 512), lambda b, n: (b, 0, 0)),
                  pl.BlockSpec((1, T, 512), lambda b, n: (b, 0, 0))],
        out_specs=pl.BlockSpec((1, tq, 256), lambda b, n: (b, n, 0)),
        out_shape=jax.ShapeDtypeStruct((B, T, 256), BF16),
        scratch_shapes=[pltpu.VMEM((H * tq, LANES), F32), pltpu.VMEM((H * tq, LANES), F32)],
        compiler_params=_cparams(("arbitrary", "arbitrary")),
        name="mla_attention",
    )(qm, km, vm)


def _out_kernel(x_ref, gpre_ref, wz_ref, ocmp_ref, oslc_ref, owin_ref, krb_ref, gexp_ref, ob_ref, oc_ref, w_ref,
                gpost_ref, o_ref):
    sub = x_ref.shape[0] // OUT_SUBTILES
    rs = [slice(t * sub, (t + 1) * sub) for t in range(OUT_SUBTILES)]
    w_a = NSA_HEADS * HEAD_DIM
    hs = [_rms(x_ref[r, :], gpre_ref[...]).astype(BF16) for r in rs]
    zs = [_dot_nt(h, wz_ref[...]) for h in hs]
    mixeds = []
    for r, z in zip(rs, zs):
        sig = 1.0 / (1.0 + jnp.exp(-krb_ref[r, :]))
        hi = sig.astype(BF16)
        lo = (sig - hi.astype(F32)).astype(BF16)
        gates = _dot(jnp.concatenate([hi, lo], axis=1), gexp_ref[...])
        oa = None
        for j, br in enumerate((ocmp_ref, oslc_ref, owin_ref)):
            term = gates[:, w_a * j:w_a * (j + 1)] * br[r, :].astype(F32)
            oa = term if oa is None else oa + term
        heads = jnp.concatenate([oa, ob_ref[r, :].astype(F32), oc_ref[r, :].astype(F32)], axis=1)
        mixeds.append((heads * (z * (1.0 / (1.0 + jnp.exp(-z))))).astype(BF16))
    ys = [_dot(m, w_ref[...]) for m in mixeds]
    for r, y in zip(rs, ys):
        o_ref[r, :] = x_ref[r, :] + _rms(y, gpost_ref[...])


def _out_project(x2, gpre, wz, ocmp, oslc, owin, krb, gexp, ob, oc, w, gpost, tm):
    BT = x2.shape[0]
    row = lambda i: (i, 0)
    fix = lambda i: (0, 0)
    spec = lambda a: pl.BlockSpec((tm, a.shape[1]), row)
    full = lambda a: pl.BlockSpec(a.shape, fix)
    return pl.pallas_call(
        _out_kernel,
        grid=(BT // tm,),
        in_specs=[spec(x2), full(gpre), full(wz), spec(ocmp), spec(oslc), spec(owin), spec(krb), full(gexp),
                  spec(ob), spec(oc), full(w), full(gpost)],
        out_specs=pl.BlockSpec((tm, D_MODEL), row),
        out_shape=jax.ShapeDtypeStruct((BT, D_MODEL), F32),
        compiler_params=_cparams(("arbitrary",)),
        name="out_proj",
    )(x2, gpre, wz, ocmp, oslc, owin, krb, gexp, ob, oc, w, gpost)


def _relayout_kernel(w_ref, wp_ref, wz_ref):
    offs = np.cumsum((0,) + tuple(IN_SIZES))
    seg = {name: (int(offs[i]), int(offs[i + 1])) for i, name in enumerate(
        ("a_q", "a_kv", "a_g", "b_cq", "b_ckv", "b_kr", "c_q", "c_kv", "z"))}
    dst = (("a_q", C_AQ), ("a_kv", C_CMP), ("b_cq", C_BCQ), ("b_ckv", C_BCKV), ("b_kr", C_KRB + HEAD_DIM),
           ("c_q", C_CQ), ("c_kv", C_CKV))
    n, L, ct = w_ref.shape
    g0, n_gate, tail = seg["a_g"][0], 3 * NSA_HEADS, LANES - GATE_LANE0
    gate_row = lax.broadcasted_iota(jnp.int32, (tail, ct), 0) < n_gate
    rows = w_ref.reshape(n * L, ct)
    pick = lambda a, b, l: rows[pl.ds(L * a + l, b - a, stride=L), :]
    for l in range(L):
        wp_ref[l, C_KRB:C_KRB + HEAD_DIM, :] = jnp.zeros((HEAD_DIM, ct), BF16)
        for name, c0 in dst:
            a, b = seg[name]
            wp_ref[l, c0:c0 + (b - a), :] = pick(a, b, l).astype(BF16)
        wp_ref[l, C_KRB + GATE_LANE0:C_CQ, :] = jnp.where(gate_row, pick(g0, g0 + tail, l), 0.0).astype(BF16)
        wz_ref[l] = pick(seg["z"][0], seg["z"][1], l).astype(BF16)


def _prep_w_in(w):
    L, d, n = w.shape
    ct = LANES
    return pl.pallas_call(
        _relayout_kernel,
        grid=(d // ct,),
        in_specs=[pl.BlockSpec((n, L, ct), lambda r: (0, 0, r))],
        out_specs=[pl.BlockSpec((L, C_Z, ct), lambda r: (0, 0, r)),
                   pl.BlockSpec((L, IN_SIZES[-1], ct), lambda r: (0, 0, r))],
        out_shape=[jax.ShapeDtypeStruct((L, C_Z, d), BF16), jax.ShapeDtypeStruct((L, IN_SIZES[-1], d), BF16)],
        compiler_params=_cparams(("arbitrary",)),
        name="w_in_relayout",
    )(jnp.transpose(w, (2, 0, 1)))


def _prep_mla(w_uq, w_ukv):
    L = w_uq.shape[0]
    dq = MLA_NOPE + MLA_ROPE
    uq = w_uq.reshape(L, MLA_Q_RANK, MLA_HEADS, dq)
    uq = jnp.concatenate([uq, jnp.zeros((L, MLA_Q_RANK, MLA_HEADS, LANES - dq), uq.dtype)], axis=-1)
    ukv = w_ukv.reshape(L, MLA_KV_RANK, MLA_HEADS, MLA_NOPE + MLA_V)
    uk = jnp.concatenate([ukv[..., :MLA_NOPE], jnp.zeros((L, MLA_KV_RANK, MLA_HEADS, LANES - MLA_NOPE), ukv.dtype)], axis=-1)
    uv = jnp.concatenate([jnp.zeros((L, MLA_KV_RANK, MLA_HEADS, LANES - MLA_V), ukv.dtype), ukv[..., MLA_NOPE:]], axis=-1)
    return (uq.reshape(L, MLA_Q_RANK, MLA_HEADS * LANES).astype(BF16),
            uk.reshape(L, MLA_KV_RANK, MLA_HEADS * LANES).astype(BF16),
            uv.reshape(L, MLA_KV_RANK, MLA_HEADS * LANES).astype(BF16))


def _prep_compress(cmp_pos, cmp_w1, cmp_w2):
    L = cmp_pos.shape[0]
    pe = jnp.concatenate([cmp_pos[:, 0], cmp_pos[:, 1]], axis=-1)
    pet = pe[:, :CMP_STRIDE].reshape(L, 1, CMP_STRIDE * LANES)
    peb = pe[:, CMP_STRIDE:].reshape(L, 1, CMP_STRIDE * LANES)
    w1 = cmp_w1.reshape(L, 2, CMP_LEN, HEAD_DIM, CMP_HIDDEN)
    zero = jnp.zeros_like(w1[:, 0])
    w1k = jnp.concatenate([w1[:, 0], zero], axis=2)
    w1v = jnp.concatenate([zero, w1[:, 1]], axis=2)
    w1e = jnp.stack([w1k, w1v], axis=1)
    w1t = w1e[:, :, :CMP_STRIDE].reshape(L, 2, CMP_STRIDE * LANES, CMP_HIDDEN).astype(BF16)
    w1b = w1e[:, :, CMP_STRIDE:].reshape(L, 2, CMP_STRIDE * LANES, CMP_HIDDEN).astype(BF16)
    z2 = jnp.zeros_like(cmp_w2[:, 0])
    w2 = jnp.stack([jnp.concatenate([cmp_w2[:, 0], z2], axis=-1),
                    jnp.concatenate([z2, cmp_w2[:, 1]], axis=-1)], axis=1).astype(BF16)
    return pet, peb, w1t, w1b, w2


def _rope_tables(T):
    half = MLA_ROPE // 2
    inv = ROPE_THETA ** (-jnp.arange(half, dtype=F32) / half)
    ang = jnp.arange(T).astype(F32)[:, None] * inv[None, :]
    cos, sin = jnp.cos(ang), jnp.sin(ang)
    z = lambda n: jnp.zeros((T, n), F32)
    rc = jnp.concatenate([jnp.ones((T, MLA_NOPE), F32), cos, cos, z(LANES - MLA_NOPE - MLA_ROPE)], axis=1)
    rsa = jnp.concatenate([z(MLA_NOPE + half), sin, z(LANES - MLA_NOPE - MLA_ROPE)], axis=1)
    rsb = jnp.concatenate([z(MLA_NOPE), -sin, z(LANES - MLA_NOPE - half)], axis=1)
    return rc, rsa, rsb


def _static_tables(T):
    ns = T // SEL_BLOCK
    nc = T // CMP_STRIDE - 1
    ncp = T // CMP_STRIDE
    sstart = np.arange(ns) * SEL_BLOCK
    cstart = np.arange(nc) * CMP_STRIDE
    overlap = (np.clip(np.minimum(cstart[:, None] + CMP_LEN, sstart[None, :] + SEL_BLOCK)
                       - np.maximum(cstart[:, None], sstart[None, :]), 0, None) / CMP_STRIDE).astype(np.float32)
    ovt = np.zeros((ns, ncp), np.float32)
    ovt[:, :nc] = overlap.T
    eaug = np.zeros((T, LANES), np.float32)
    eaug[np.arange(T), HEAD_DIM + np.arange(T) // SEL_BLOCK] = 1.0
    gexp = np.zeros((2, LANES, 3 * NSA_HEADS * HEAD_DIM), np.float32)
    for h in range(NSA_HEADS):
        for j in range(3):
            c0 = NSA_HEADS * HEAD_DIM * j + HEAD_DIM * h
            gexp[:, GATE_LANE0 + 3 * h + j, c0:c0 + HEAD_DIM] = 1.0
    gexp = gexp.reshape(2 * LANES, -1)
    return jnp.asarray(ovt, BF16), jnp.asarray(eaug, BF16), jnp.asarray(gexp, BF16)


def kernel(x, w_in, w_out, norm_pre, norm_post, cmp_pos, cmp_w1, cmp_w2, mla_q_norm, mla_w_uq, mla_kv_norm,
           mla_w_ukv, swa_sinks, rel_bias):
    B, T, D = x.shape
    depth = w_in.shape[0]
    assert D == D_MODEL and T % (2 * FLASH_TILE) == 0 and T // SEL_BLOCK <= HEAD_DIM
    ncp = T // CMP_STRIDE

    w_in_p, w_z = _prep_w_in(w_in)
    wuq, wk, wv = _prep_mla(mla_w_uq, mla_w_ukv)
    pet, peb, w1t, w1b, w2 = _prep_compress(cmp_pos, cmp_w1, cmp_w2)
    w_out_b = w_out.astype(BF16)
    rope_c, rope_sa, rope_sb = _rope_tables(T)
    ovt, eaug, gexp = _static_tables(T)

    bc = _bias_table(rel_bias, 0, NSA_HEADS, ncp, T, CMP_STRIDE, 1, -(CMP_LEN - 1), 0, 1 << 30, 32, mult=LOG2E)
    bc = jnp.transpose(bc.reshape(NSA_HEADS, ncp, T), (0, 2, 1))
    tab_near = _bias_table(rel_bias, 0, NSA_HEADS, FLASH_TILE, 2 * FLASH_TILE, 1, -1, FLASH_TILE, 0, 1 << 30, 256,
                           mult=LOG2E).reshape(NSA_HEADS, FLASH_TILE, 2 * FLASH_TILE)
    tab_win = _bias_table(rel_bias, 0, NSA_HEADS, WIN_TILE, NSA_WINDOW + WIN_TILE, 1, -1, 0, 0, NSA_WINDOW, WIN_TILE,
                          mult=LOG2E, nblk=NSA_WINDOW // WIN_TILE + 1, blk_d0=WIN_TILE)
    tab_swa = _bias_table(rel_bias, NSA_HEADS, SWA_HEADS, Q_TILE, SWA_WINDOW + Q_TILE, 1, -1, 0, 0, SWA_WINDOW, Q_TILE,
                          mult=LOG2E, nblk=SWA_WINDOW // Q_TILE + 1, blk_d0=Q_TILE)
    b31 = rel_bias[:NSA_HEADS, REL_BUCKETS - 1] * LOG2E

    x2 = x.reshape(B * T, D)
    for l in range(depth):
        (qa, cmpkv, slck, slcv, win, qm, km, vm, krb, cq, ckv) = _project(
            x2, norm_pre[l][None], w_in_p[l], mla_q_norm[l][None], wuq[l], mla_kv_norm[l][None], wk[l], wv[l],
            rope_c, rope_sa, rope_sb, eaug, T, PROJ_ROWS)
        r3 = lambda a: a.reshape(B, T, a.shape[-1])
        kvc = _compress(r3(cmpkv), pet[l], peb[l], w1t[l], w1b[l], w2[l])
        qa3 = r3(qa)
        ocmp, selb = _cmp_attention(qa3, kvc, bc, ovt, T)
        owin = _band_attention(qa3, r3(win), tab_win, None, NSA_HEADS, NSA_WINDOW, T, BF16)
        oslc = _slc_attention(b31, qa3, selb, r3(slck), r3(slcv), tab_near, T)
        oc = _band_attention(r3(cq), r3(ckv), tab_swa, swa_sinks[l] * LOG2E, SWA_HEADS, SWA_WINDOW, T, BF16)
        ob = _mla_attention(r3(qm), r3(km), r3(vm), T)
        flat = lambda a: a.reshape(B * T, a.shape[-1])
        x2 = _out_project(x2, norm_pre[l][None], w_z[l], flat(ocmp), flat(oslc), flat(owin), krb, gexp,
                          flat(ob), flat(oc), w_out_b[l], norm_post[l][None], OUT_ROWS)
    return x2.reshape(B, T, D)
```

```python
import functools
import math

import numpy as np
import jax
import jax.numpy as jnp
from jax import lax
from jax.experimental import pallas as pl
from jax.experimental.pallas import tpu as pltpu

F32 = jnp.float32
BF16 = jnp.bfloat16

D_MODEL = 1024
HEAD_DIM = 64
NSA_HEADS = 4
CMP_LEN = 32
CMP_STRIDE = 16
CMP_HIDDEN = 128
SEL_BLOCK = 64
SEL_TOP = 16
NSA_WINDOW = 512
MLA_HEADS = 4
MLA_Q_RANK = 256
MLA_KV_RANK = 128
MLA_NOPE = 64
MLA_ROPE = 32
MLA_V = 64
ROPE_THETA = 10000.0
SWA_HEADS = 8
SWA_WINDOW = 128
REL_BUCKETS = 32
REL_MAX_DIST = 512
NORM_EPS = 1e-6
NEG = -1e30
BIG = 1e9
LOG2E = math.log2(math.e)
MLA_SCALE2 = (MLA_NOPE + MLA_ROPE) ** -0.5 * LOG2E
IN_SIZES = (256, 384, 12, 256, 128, 32, 512, 128, 1024)

LANES = 128
Q_TILE = 128
WIN_TILE = 128
BAND_ROWS = 4096
PROJ_ROWS = 1024
OUT_SUBTILES = 4
OUT_ROWS = 1024
CMP_GROUP = 4
FLASH_TILE = 512
SLC_DIAG_PARTS = 4
MLA_DIAG_PARTS = 4
GATE_LANE0 = 96
VMEM_LIMIT = 56 * 1024 * 1024

C_AQ, C_CMP, C_SLC, C_WIN, C_BCQ, C_BCKV, C_KRB, C_CQ, C_CKV, C_Z = (
    0, 256, 384, 512, 640, 896, 1024, 1152, 1664, 1792)


def _dot(a, b):
    return jnp.dot(a, b, preferred_element_type=F32)


def _dot_nt(a, b):
    return lax.dot_general(a, b, (((1,), (1,)), ((), ())), preferred_element_type=F32)


def _bucket_thresholds():
    d = np.arange(0, 4 * REL_MAX_DIST)
    exact = REL_BUCKETS // 2
    large = exact + (np.log(np.maximum(d, 1).astype(np.float32) / np.float32(exact))
                     / np.float32(math.log(REL_MAX_DIST / exact)) * np.float32(REL_BUCKETS - exact)).astype(np.int32)
    b = np.where(d < exact, d, np.minimum(large, REL_BUCKETS - 1))
    assert np.all(np.diff(b) >= 0) and b[-1] == REL_BUCKETS - 1
    return [int(np.argmax(b >= k)) for k in range(REL_BUCKETS)]


_THRESH = _bucket_thresholds()


def _cparams(sem):
    return pltpu.CompilerParams(dimension_semantics=sem, vmem_limit_bytes=VMEM_LIMIT)


def _table_kernel(h0, row_step, sign, d0, blk_d0, lo, hi, mult, rb_ref, out_ref):
    h = pl.program_id(0) + h0
    blk = pl.program_id(1)
    rt, cc = out_ref.shape[2], out_ref.shape[3]
    base = d0 + blk * blk_d0 - sign * row_step * rt * pl.program_id(2)
    width = -(-(row_step * rt + cc) // LANES) * LANES
    w = lax.broadcasted_iota(jnp.int32, (8, width), 1)
    dist = base + sign * jnp.where(w < cc, w, w - width)
    acc = jnp.full(dist.shape, rb_ref[h, 0], F32)
    for k in range(1, REL_BUCKETS):
        acc = jnp.where(dist >= _THRESH[k], rb_ref[h, k], acc)
    row = jnp.where((dist >= lo) & (dist < hi), acc * mult, NEG)
    rolled = pltpu.roll(jnp.broadcast_to(row[:1], (rt, width)), 0, 1, stride=row_step, stride_axis=0)
    out_ref[0, 0] = rolled[:, :cc]


def _bias_table(rel_bias, h0, nh, rows, cols, row_step, sign, d0, lo, hi, rt, mult=1.0, nblk=1, blk_d0=0):
    return pl.pallas_call(
        functools.partial(_table_kernel, h0, row_step, sign, d0, blk_d0, lo, hi, mult),
        grid=(nh, nblk, rows // rt),
        in_specs=[pl.BlockSpec(memory_space=pltpu.SMEM)],
        out_specs=pl.BlockSpec((1, 1, rt, cols), lambda h, b, r: (h, b, r, 0)),
        out_shape=jax.ShapeDtypeStruct((nh, nblk, rows, cols), F32),
        compiler_params=_cparams(("arbitrary", "arbitrary", "arbitrary")),
        name="bias_table",
    )(rel_bias)


def _rms(v, g):
    return v * lax.rsqrt(jnp.mean(v * v, axis=-1, keepdims=True) + NORM_EPS) * g


def _proj_kernel(x_ref, gpre_ref, w_ref, qn_ref, wuq_ref, kvn_ref, wk_ref, wv_ref,
                 rc_ref, rsa_ref, rsb_ref, eaug_ref,
                 qa_ref, cmp_ref, slck_ref, slcv_ref, win_ref, qm_ref, km_ref, vm_ref,
                 krb_ref, cq_ref, ckv_ref):
    hb = _rms(x_ref[...], gpre_ref[...]).astype(BF16)
    bounds = ((C_AQ, C_CMP), (C_CMP, C_WIN), (C_WIN, C_KRB), (C_KRB, C_Z))
    chunks = {ab: _dot_nt(hb, w_ref[ab[0]:ab[1], :]) for ab in bounds}
    lane = lax.broadcasted_iota(jnp.int32, (hb.shape[0], LANES), 1)
    lo = lane < HEAD_DIM

    def proj(c0, c1):
        for (a, b), y in chunks.items():
            if a <= c0 and c1 <= b:
                return y[:, c0 - a:c1 - a]
        raise AssertionError((c0, c1))

    qa_ref[...] = (proj(C_AQ, C_CMP) * (0.125 * LOG2E)).astype(BF16)
    cmp_ref[...] = proj(C_CMP, C_SLC)
    slc = proj(C_SLC, C_WIN)
    slcv_ref[...] = jnp.where(lo, 1.0, slc).astype(BF16)
    slck_ref[...] = jnp.where(lo, slc, eaug_ref[...].astype(F32)).astype(BF16)
    win_ref[...] = proj(C_WIN, C_BCQ).astype(BF16)
    cq_ref[...] = (proj(C_CQ, C_CKV) * (0.125 * LOG2E)).astype(BF16)
    ckv_ref[...] = proj(C_CKV, C_Z).astype(BF16)

    rc, rsa, rsb = rc_ref[...], rsa_ref[...], rsb_ref[...]

    def rope(v):
        return v * rc + pltpu.roll(v, 16, 1) * rsa + pltpu.roll(v, 112, 1) * rsb

    cqn = _rms(proj(C_BCQ, C_BCKV), qn_ref[...]).astype(BF16)
    qm = _dot(cqn, wuq_ref[...])
    for h in range(MLA_HEADS):
        qh = rope(qm[:, LANES * h:LANES * (h + 1)]) * MLA_SCALE2
        qm_ref[:, LANES * h:LANES * (h + 1)] = qh.astype(BF16)

    krb = proj(C_KRB, C_CQ)
    krb_ref[...] = krb
    krr = jnp.where(lo | (lane >= HEAD_DIM + MLA_ROPE), 0.0, rope(krb))

    ckvn = _rms(proj(C_BCKV, C_KRB), kvn_ref[...]).astype(BF16)
    kn = _dot(ckvn, wk_ref[...])
    for h in range(MLA_HEADS):
        km_ref[:, LANES * h:LANES * (h + 1)] = (kn[:, LANES * h:LANES * (h + 1)] + krr).astype(BF16)
    vm = _dot(ckvn, wv_ref[...])
    for h in range(MLA_HEADS):
        vm_ref[:, LANES * h:LANES * (h + 1)] = jnp.where(lo, 1.0, vm[:, LANES * h:LANES * (h + 1)]).astype(BF16)


def _project(x2, gpre, w_all, layer, qn, wuq, kvn, wk, wv, rope_c, rope_sa, rope_sb, eaug, T, tm):
    BT = x2.shape[0]
    nt = T // tm
    row = lambda i: (i, 0)
    fix = lambda i: (0, 0)
    pos = lambda i: (i % nt, 0)
    of_layer = lambda a: pl.BlockSpec((None,) + a.shape[1:], lambda i: (layer, 0, 0))
    widths = [(256, BF16), (128, F32), (128, BF16), (128, BF16), (128, BF16), (512, BF16), (512, BF16),
              (512, BF16), (128, F32), (512, BF16), (128, BF16)]
    return pl.pallas_call(
        _proj_kernel,
        grid=(BT // tm,),
        in_specs=[pl.BlockSpec((tm, D_MODEL), row), pl.BlockSpec((1, D_MODEL), fix),
                  of_layer(w_all), pl.BlockSpec(qn.shape, fix), pl.BlockSpec(wuq.shape, fix),
                  pl.BlockSpec(kvn.shape, fix), pl.BlockSpec(wk.shape, fix), pl.BlockSpec(wv.shape, fix),
                  pl.BlockSpec((tm, LANES), pos), pl.BlockSpec((tm, LANES), pos), pl.BlockSpec((tm, LANES), pos),
                  pl.BlockSpec((tm, LANES), pos)],
        out_specs=[pl.BlockSpec((tm, wd), row) for wd, _ in widths],
        out_shape=[jax.ShapeDtypeStruct((BT, wd), dt) for wd, dt in widths],
        compiler_params=_cparams(("arbitrary",)),
        name="proj",
    )(x2, gpre, w_all, qn, wuq, kvn, wk, wv, rope_c, rope_sa, rope_sb, eaug)


def _compress_kernel(c_ref, pet_ref, peb_ref, w1t_ref, w1b_ref, w2_ref, o_ref):
    nch = c_ref.shape[1] // CMP_STRIDE
    c = jnp.concatenate([c_ref[0, pl.ds(j, nch, stride=CMP_STRIDE), :] for j in range(CMP_STRIDE)], axis=1)
    top = (c + pet_ref[...]).astype(BF16)
    bot = (c + peb_ref[...]).astype(BF16)
    out = jnp.zeros((nch, LANES), F32)
    for i in range(2):
        pre = _dot(top, w1t_ref[i]) + pltpu.roll(_dot(bot, w1b_ref[i]), nch - 1, 0)
        hid = pre * (1.0 / (1.0 + jnp.exp(-pre)))
        out = out + _dot(hid.astype(BF16), w2_ref[i])
    o_ref[0] = out.astype(BF16)


def _compress(tokens, pet, peb, w1t, w1b, w2):
    B, T, width = tokens.shape
    nch = T // CMP_STRIDE
    full = lambda a: pl.BlockSpec(a.shape, lambda b: (0,) * a.ndim)
    return pl.pallas_call(
        _compress_kernel,
        grid=(B,),
        in_specs=[pl.BlockSpec((1, T, width), lambda b: (b, 0, 0)),
                  full(pet), full(peb), full(w1t), full(w1b), full(w2)],
        out_specs=pl.BlockSpec((1, nch, LANES), lambda b: (b, 0, 0)),
        out_shape=jax.ShapeDtypeStruct((B, nch, LANES), BF16),
        compiler_params=_cparams(("arbitrary",)),
        name="compress",
    )(tokens, pet, peb, w1t, w1b, w2)


def _head_slabs(qblk, n_heads):
    q = qblk.astype(F32)
    lo = lax.broadcasted_iota(jnp.int32, (q.shape[0], LANES), 1) < HEAD_DIM
    out = []
    for j in range(n_heads // 2):
        slab = q[:, LANES * j:LANES * (j + 1)]
        out.append(jnp.where(lo, slab, 0.0))
        out.append(jnp.where(lo, pltpu.roll(slab, HEAD_DIM, 1), 0.0))
    return out


def _merge_upper(accs):
    lo = lax.broadcasted_iota(jnp.int32, accs[0].shape, 1) < HEAD_DIM
    slabs = [jnp.where(lo, pltpu.roll(accs[2 * j], HEAD_DIM, 1), accs[2 * j + 1]) for j in range(len(accs) // 2)]
    return jnp.concatenate(slabs, axis=1)


def _merge_normalized(accs):
    lo = lax.broadcasted_iota(jnp.int32, accs[0].shape, 1) < HEAD_DIM
    slabs = []
    for j in range(len(accs) // 2):
        a, b = accs[2 * j], accs[2 * j + 1]
        out = jnp.where(lo, pltpu.roll(a, HEAD_DIM, 1), b)
        den = jnp.where(lo, a, pltpu.roll(b, HEAD_DIM, 1))
        slabs.append(out / den)
    return jnp.concatenate(slabs, axis=1)


def _row_max_lanes(s):
    return jnp.broadcast_to(jnp.max(s, axis=-1, keepdims=True), (s.shape[0], LANES))


def _sub_lanes(s, ref):
    return jnp.concatenate([s[:, LANES * i:LANES * (i + 1)] - ref for i in range(s.shape[1] // LANES)], axis=1)


def _topk_mask_t(imp, t0):
    ns, tq = imp.shape
    srow = lax.broadcasted_iota(jnp.int32, (ns, tq), 0)
    cur = (t0 + lax.broadcasted_iota(jnp.int32, (ns, tq), 1)) // SEL_BLOCK
    forced = (srow == 0) | (srow == cur) | (srow == cur - 1)
    x = jnp.where(forced, BIG, jnp.where(srow <= cur, imp, -BIG))
    sub = 8
    groups = [x[sub * v:sub * (v + 1)] for v in range(ns // sub)]
    rows_in = [srow[sub * v:sub * (v + 1)] for v in range(ns // sub)]
    cnts = [jnp.zeros((sub, tq), F32) for _ in groups]
    for sp in range(ns):
        other = jnp.broadcast_to(x[sp:sp + 1, :], (sub, tq))
        for v, xv in enumerate(groups):
            if sub * v > sp:
                beats = other >= xv
            elif sub * v + sub - 1 < sp:
                beats = other > xv
            else:
                beats = (other > xv) | ((other == xv) & (rows_in[v] > sp))
            cnts[v] = cnts[v] + jnp.where(beats, 1.0, 0.0)
    cnt = jnp.concatenate(cnts, axis=0)
    sel = (cnt < float(min(SEL_TOP, ns))) & (srow <= cur)
    return jnp.where(sel, 0.0, NEG)


def _cmp_kernel(q_ref, kvc_ref, bc_ref, ovt_ref, ocmp_ref, selb_ref):
    rows = CMP_GROUP * Q_TILE
    ncp, ns = kvc_ref.shape[1], ovt_ref.shape[0]
    step = pl.program_id(1)
    lo_step, nse = 0, SEL_TOP
    while lo_step * rows < ns * SEL_BLOCK:
        nse = min(nse, ns)
        hi_step = nse * SEL_BLOCK // rows
        ncw = min(ncp, -(-nse * (SEL_BLOCK // CMP_STRIDE) // LANES) * LANES)
        if hi_step > lo_step:
            pl.when((step >= lo_step) & (step < hi_step))(
                functools.partial(_cmp_body, ncw, nse, q_ref, kvc_ref, bc_ref, ovt_ref, ocmp_ref, selb_ref))
            lo_step = hi_step
        nse += SEL_TOP


def _cmp_body(ncw, ns, q_ref, kvc_ref, bc_ref, ovt_ref, ocmp_ref, selb_ref):
    tq = Q_TILE
    kvc = kvc_ref[0, :ncw, :]
    ovt = ovt_ref[:ns, :ncw]
    groups = range(CMP_GROUP)
    sls = [slice(g * tq, (g + 1) * tq) for g in groups]
    s_alls = []
    for g in groups:
        qs = _head_slabs(q_ref[0, sls[g], :], NSA_HEADS)
        s_alls.append(_dot_nt(jnp.concatenate([q.astype(BF16) for q in qs], axis=0), kvc))
    psums, o_alls = [], []
    for g in groups:
        psum = None
        ps = []
        for h in range(NSA_HEADS):
            s = s_alls[g][h * tq:(h + 1) * tq] + bc_ref[h, sls[g], :ncw]
            m = jnp.max(s, axis=-1, keepdims=True)
            e = jnp.exp2(s - m)
            l = jnp.sum(e, axis=-1, keepdims=True)
            p = e * jnp.where(m > 0.5 * NEG, 1.0 / l, 0.0)
            psum = p if psum is None else psum + p
            ps.append(p.astype(BF16))
        psums.append(psum)
        o_alls.append(_dot(jnp.concatenate(ps, axis=0), kvc))
    imps = []
    for g in groups:
        hi = psums[g].astype(BF16)
        lo = (psums[g] - hi.astype(F32)).astype(BF16)
        imps.append(_dot_nt(ovt, hi) + _dot_nt(ovt, lo))
    for g in groups:
        ocmp = _merge_upper([o_alls[g][h * tq:(h + 1) * tq] for h in range(NSA_HEADS)])
        ocmp_ref[0, sls[g], :] = ocmp.astype(ocmp_ref.dtype)
        sb = _topk_mask_t(imps[g], (pl.program_id(1) * CMP_GROUP + g) * tq)
        if ns < HEAD_DIM:
            sb = jnp.concatenate([sb, jnp.full((HEAD_DIM - ns, tq), NEG, F32)], axis=0)
        full = jnp.concatenate([jnp.zeros((HEAD_DIM, tq), F32), sb], axis=0)
        selb_ref[0, sls[g], :] = full.T.astype(BF16)


def _cmp_attention(qa, kvc, bc, ovt, T):
    B = qa.shape[0]
    ncp = kvc.shape[1]
    rows = CMP_GROUP * Q_TILE
    return pl.pallas_call(
        _cmp_kernel,
        grid=(B, T // rows),
        in_specs=[pl.BlockSpec((1, rows, 256), lambda b, n: (b, n, 0)),
                  pl.BlockSpec((1, ncp, LANES), lambda b, n: (b, 0, 0)),
                  pl.BlockSpec((NSA_HEADS, rows, ncp), lambda b, n: (0, n, 0)),
                  pl.BlockSpec(ovt.shape, lambda b, n: (0, 0))],
        out_specs=[pl.BlockSpec((1, rows, 256), lambda b, n: (b, n, 0)),
                   pl.BlockSpec((1, rows, LANES), lambda b, n: (b, n, 0))],
        out_shape=[jax.ShapeDtypeStruct((B, T, 256), BF16), jax.ShapeDtypeStruct((B, T, LANES), BF16)],
        compiler_params=_cparams(("arbitrary", "arbitrary")),
        name="cmp_select",
    )(qa, kvc, bc, ovt)


def _softmax_tile(s2, rows, m_s, shift=None):
    m_t = _row_max_lanes(s2)
    if shift is not None:
        m_t = m_t + shift
    m_old = m_s[rows]
    m_new = jnp.maximum(m_old, m_t)
    p = jnp.exp2(_sub_lanes(s2, m_new if shift is None else m_new - shift))
    alpha = jnp.exp2(m_old - m_new)
    m_s[rows] = m_new
    return p.astype(BF16), alpha


def _slc_kernel(b31_ref, q_ref, selb_ref, ka_ref, kv_ref, tab_ref, o_ref, m_s, acc_s):
    tq = q_ref.shape[1]
    H = NSA_HEADS
    n = pl.program_id(1)
    qs = _head_slabs(q_ref[0], H)
    selb = selb_ref[0].astype(F32)
    qst = jnp.concatenate([(qs[h] + selb).astype(BF16) for h in range(H)], axis=0)
    m_s[...] = jnp.full(m_s.shape, -jnp.inf, F32)
    acc_s[...] = jnp.zeros(acc_s.shape, F32)

    def scores(k0, width=tq):
        k0 = pl.multiple_of(k0, tq)
        return _dot_nt(qst, ka_ref[0, pl.ds(k0, width), :]), kv_ref[0, pl.ds(k0, width), :]

    def update(s_all, kv):
        ps, alphas = [], []
        for h in range(H):
            rows = slice(h * tq, (h + 1) * tq)
            p, alpha = _softmax_tile(s_all[rows], rows, m_s, shift=b31_ref[h])
            ps.append(p)
            alphas.append(alpha)
        pv = _dot(jnp.concatenate(ps, axis=0), kv)
        for h in range(H):
            rows = slice(h * tq, (h + 1) * tq)
            acc_s[rows] = alphas[h] * acc_s[rows] + pv[rows]

    def far_triple(i, carry):
        update(*scores(3 * i * tq, 3 * tq))
        return carry

    n_far = jnp.maximum(n - 1, 0)
    lax.fori_loop(0, n_far // 3, far_triple, 0)

    @pl.when(n_far % 3 == 1)
    def _():
        update(*scores((n_far - 1) * tq))

    @pl.when(n_far % 3 == 2)
    def _():
        update(*scores((n_far - 2) * tq, 2 * tq))

    def near_diagonal(k0, before):
        k0 = pl.multiple_of(k0, tq)
        half = tq // SLC_DIAG_PARTS
        parts = []
        for part in range(SLC_DIAG_PARTS):
            kw = before + half * (part + 1)
            q_part = jnp.concatenate([qst[h * tq + part * half:h * tq + (part + 1) * half] for h in range(H)], axis=0)
            parts.append((_dot_nt(q_part, ka_ref[0, pl.ds(k0, kw), :]), kv_ref[0, pl.ds(k0, kw), :], kw))
        for part, (s_all, kv, kw) in enumerate(parts):
            ps, alphas = [], []
            for h in range(H):
                rows = slice(h * tq + part * half, h * tq + (part + 1) * half)
                bias = tab_ref[h, part * half:(part + 1) * half, tq - before:tq - before + kw]
                p, alpha = _softmax_tile(s_all[h * half:(h + 1) * half] + bias, rows, m_s)
                ps.append(p)
                alphas.append(alpha)
            pv = _dot(jnp.concatenate(ps, axis=0), kv)
            for h in range(H):
                rows = slice(h * tq + part * half, h * tq + (part + 1) * half)
                acc_s[rows] = alphas[h] * acc_s[rows] + pv[h * half:(h + 1) * half]

    @pl.when(n >= 1)
    def _():
        near_diagonal(jnp.maximum(n - 1, 0) * tq, tq)

    @pl.when(n == 0)
    def _():
        near_diagonal(0, 0)

    o_ref[0] = _merge_normalized([acc_s[h * tq:(h + 1) * tq] for h in range(H)]).astype(o_ref.dtype)


def _slc_attention(b31, qa, selb, slck, slcv, tab, T):
    B = qa.shape[0]
    tq = FLASH_TILE
    H = NSA_HEADS
    tile = lambda w: pl.BlockSpec((1, tq, w), lambda b, n: (b, n, 0))
    return pl.pallas_call(
        _slc_kernel,
        grid=(B, T // tq),
        in_specs=[pl.BlockSpec(memory_space=pltpu.SMEM),
                  tile(256), tile(LANES),
                  pl.BlockSpec((1, T, LANES), lambda b, n: (b, 0, 0)),
                  pl.BlockSpec((1, T, LANES), lambda b, n: (b, 0, 0)),
                  pl.BlockSpec(tab.shape, lambda b, n: (0, 0, 0))],
        out_specs=tile(256),
        out_shape=jax.ShapeDtypeStruct((B, T, 256), BF16),
        scratch_shapes=[pltpu.VMEM((H * tq, LANES), F32), pltpu.VMEM((H * tq, LANES), F32)],
        compiler_params=_cparams(("arbitrary", "arbitrary")),
        name="slc_attention",
    )(b31, qa, selb, slck, slcv, tab)


def _band_kernel(n_heads, window, has_sink, n_sub, tq, *refs):
    if has_sink:
        q_ref, kv_ref, tab_ref, sink_ref, o_ref = refs
    else:
        q_ref, kv_ref, tab_ref, o_ref = refs
    span = window + tq
    lo_kv = lax.broadcasted_iota(jnp.int32, (span, LANES), 1) < HEAD_DIM
    lo_o = lax.broadcasted_iota(jnp.int32, (n_heads * tq, LANES), 1) < HEAD_DIM
    groups = range(n_sub)
    tiles = [pl.program_id(1) * n_sub + g for g in groups]
    kvs, dots = [], []
    for g in groups:
        k0 = pl.multiple_of(jnp.maximum(tiles[g] * tq - window, 0), tq)
        kvs.append(kv_ref[0, pl.ds(k0, span), :])
        qs = _head_slabs(q_ref[0, g * tq:(g + 1) * tq, :], n_heads)
        dots.append(_dot_nt(jnp.concatenate([q.astype(BF16) for q in qs], axis=0), kvs[g]))
    accs, ms = [], []
    for g in groups:
        s = dots[g] + tab_ref[jnp.minimum(tiles[g], window // tq)]
        m = _row_max_lanes(s)
        if has_sink:
            m = jnp.maximum(m, sink_ref[...])
        e = jnp.exp2(_sub_lanes(s, m))
        ms.append(m)
        accs.append(_dot(e.astype(BF16), jnp.where(lo_kv, 1.0, kvs[g]).astype(BF16)))
    for g in groups:
        acc = accs[g]
        if has_sink:
            acc = acc + jnp.where(lo_o, jnp.exp2(sink_ref[...] - ms[g]), 0.0)
        out = _merge_normalized([acc[h * tq:(h + 1) * tq] for h in range(n_heads)])
        o_ref[0, g * tq:(g + 1) * tq, :] = out.astype(o_ref.dtype)


def _band_attention(q, kv_pad, tab, sinks, n_heads, window, T, out_dtype):
    B = q.shape[0]
    width = n_heads * HEAD_DIM
    has_sink = sinks is not None
    nvar, tq = tab.shape[1], tab.shape[2]
    tab2 = jnp.transpose(tab, (1, 0, 2, 3)).reshape(nvar, n_heads * tq, window + tq)
    n_sub = (BAND_ROWS if has_sink else 2 * BAND_ROWS) // (n_heads * tq)
    rows = n_sub * tq
    in_specs = [pl.BlockSpec((1, rows, width), lambda b, n: (b, n, 0)),
                pl.BlockSpec((1, T, LANES), lambda b, n: (b, 0, 0)),
                pl.BlockSpec(tab2.shape, lambda b, n: (0, 0, 0))]
    args = (q, kv_pad, tab2)
    if has_sink:
        rep = jnp.broadcast_to(jnp.repeat(sinks, tq)[:, None], (n_heads * tq, LANES))
        in_specs.append(pl.BlockSpec(rep.shape, lambda b, n: (0, 0)))
        args = args + (rep,)
    return pl.pallas_call(
        functools.partial(_band_kernel, n_heads, window, has_sink, n_sub, tq),
        grid=(B, T // rows),
        in_specs=in_specs,
        out_specs=pl.BlockSpec((1, rows, width), lambda b, n: (b, n, 0)),
        out_shape=jax.ShapeDtypeStruct((B, T, width), out_dtype),
        compiler_params=_cparams(("arbitrary", "arbitrary")),
        name="band_sink" if has_sink else "band_window",
    )(*args)


def _mla_kernel(q_ref, k_ref, v_ref, o_ref, m_s, acc_s):
    tq = q_ref.shape[1]
    H = MLA_HEADS
    n = pl.program_id(1)
    m_s[...] = jnp.full(m_s.shape, -jnp.inf, F32)
    acc_s[...] = jnp.zeros(acc_s.shape, F32)

    def scores(k0, width=tq):
        k0 = pl.multiple_of(k0, tq)
        return k0, width, [_dot_nt(q_ref[0, :, LANES * h:LANES * (h + 1)],
                                   k_ref[0, pl.ds(k0, width), LANES * h:LANES * (h + 1)]) for h in range(H)]

    def update(k0, width, ss):
        ps, alphas = [], []
        for h in range(H):
            p, alpha = _softmax_tile(ss[h], slice(h * tq, (h + 1) * tq), m_s)
            ps.append(p)
            alphas.append(alpha)
        for h in range(H):
            rows = slice(h * tq, (h + 1) * tq)
            v = v_ref[0, pl.ds(k0, width), LANES * h:LANES * (h + 1)]
            acc_s[rows] = alphas[h] * acc_s[rows] + _dot(ps[h], v)

    def far_triple(i, carry):
        update(*scores(3 * i * tq, 3 * tq))
        return carry

    n_far = jnp.maximum(n - 1, 0)
    lax.fori_loop(0, n_far // 3, far_triple, 0)

    @pl.when(n_far % 3 == 1)
    def _():
        update(*scores((n_far - 1) * tq))

    @pl.when(n_far % 3 == 2)
    def _():
        update(*scores((n_far - 2) * tq, 2 * tq))

    def near_diagonal(k0, before):
        k0 = pl.multiple_of(k0, tq)
        half = tq // MLA_DIAG_PARTS
        parts = []
        for part in range(MLA_DIAG_PARTS):
            kw = before + half * (part + 1)
            r = slice(part * half, (part + 1) * half)
            parts.append((kw, [_dot_nt(q_ref[0, r, LANES * h:LANES * (h + 1)],
                                       k_ref[0, pl.ds(k0, kw), LANES * h:LANES * (h + 1)]) for h in range(H)]))
        for part, (kw, ss) in enumerate(parts):
            row = lax.broadcasted_iota(jnp.int32, (half, kw), 0)
            col = lax.broadcasted_iota(jnp.int32, (half, kw), 1)
            causal = col <= row + (before + part * half)
            ps, alphas = [], []
            for h in range(H):
                rows = slice(h * tq + part * half, h * tq + (part + 1) * half)
                p, alpha = _softmax_tile(jnp.where(causal, ss[h], NEG), rows, m_s)
                ps.append(p)
                alphas.append(alpha)
            for h in range(H):
                rows = slice(h * tq + part * half, h * tq + (part + 1) * half)
                v = v_ref[0, pl.ds(k0, kw), LANES * h:LANES * (h + 1)]
                acc_s[rows] = alphas[h] * acc_s[rows] + _dot(ps[h], v)

    @pl.when(n >= 1)
    def _():
        near_diagonal(jnp.maximum(n - 1, 0) * tq, tq)

    @pl.when(n == 0)
    def _():
        near_diagonal(0, 0)

    o_ref[0] = _merge_normalized([acc_s[h * tq:(h + 1) * tq] for h in range(H)]).astype(o_ref.dtype)


def _mla_attention(qm, km, vm, T):
    B = qm.shape[0]
    tq = FLASH_TILE
    H = MLA_HEADS
    return pl.pallas_call(
        _mla_kernel,
        grid=(B, T // tq),
        in_specs=[pl.BlockSpec((1, tq, 512), lambda b, n: (b, n, 0)),
                  pl.BlockSpec((1, T, 512), lambda b, n: (b, 0, 0)),
                  pl.BlockSpec((1, T, 512), lambda b, n: (b, 0, 0))],
        out_specs=pl.BlockSpec((1, tq, 256), lambda b, n: (b, n, 0)),
        out_shape=jax.ShapeDtypeStruct((B, T, 256), BF16),
        scratch_shapes=[pltpu.VMEM((H * tq, LANES), F32), pltpu.VMEM((H * tq, LANES), F32)],
        compiler_params=_cparams(("arbitrary", "arbitrary")),
        name="mla_attention",
    )(qm, km, vm)


def _out_kernel(x_ref, gpre_ref, wz_ref, ocmp_ref, oslc_ref, owin_ref, krb_ref, gexp_ref, ob_ref, oc_ref, w_ref,
                gpost_ref, o_ref):
    sub = x_ref.shape[0] // OUT_SUBTILES
    rs = [slice(t * sub, (t + 1) * sub) for t in range(OUT_SUBTILES)]
    w_a = NSA_HEADS * HEAD_DIM
    hs = [_rms(x_ref[r, :], gpre_ref[...]).astype(BF16) for r in rs]
    zs = [_dot_nt(h, wz_ref[...]) for h in hs]
    mixeds = []
    for r, z in zip(rs, zs):
        sig = 1.0 / (1.0 + jnp.exp(-krb_ref[r, :]))
        hi = sig.astype(BF16)
        lo = (sig - hi.astype(F32)).astype(BF16)
        gates = _dot(jnp.concatenate([hi, lo], axis=1), gexp_ref[...])
        oa = None
        for j, br in enumerate((ocmp_ref, oslc_ref, owin_ref)):
            term = gates[:, w_a * j:w_a * (j + 1)] * br[r, :].astype(F32)
            oa = term if oa is None else oa + term
        heads = jnp.concatenate([oa, ob_ref[r, :].astype(F32), oc_ref[r, :].astype(F32)], axis=1)
        mixeds.append((heads * (z * (1.0 / (1.0 + jnp.exp(-z))))).astype(BF16))
    ys = [_dot(m, w_ref[...]) for m in mixeds]
    for r, y in zip(rs, ys):
        o_ref[r, :] = x_ref[r, :] + _rms(y, gpost_ref[...])


def _out_project(x2, gpre, wz_all, layer, ocmp, oslc, owin, krb, gexp, ob, oc, w_all, gpost, tm):
    BT = x2.shape[0]
    row = lambda i: (i, 0)
    fix = lambda i: (0, 0)
    spec = lambda a: pl.BlockSpec((tm, a.shape[1]), row)
    full = lambda a: pl.BlockSpec(a.shape, fix)
    of_layer = lambda a: pl.BlockSpec((None,) + a.shape[1:], lambda i: (layer, 0, 0))
    return pl.pallas_call(
        _out_kernel,
        grid=(BT // tm,),
        in_specs=[spec(x2), full(gpre), of_layer(wz_all), spec(ocmp), spec(oslc), spec(owin), spec(krb), full(gexp),
                  spec(ob), spec(oc), of_layer(w_all), full(gpost)],
        out_specs=pl.BlockSpec((tm, D_MODEL), row),
        out_shape=jax.ShapeDtypeStruct((BT, D_MODEL), F32),
        compiler_params=_cparams(("arbitrary",)),
        name="out_proj",
    )(x2, gpre, wz_all, ocmp, oslc, owin, krb, gexp, ob, oc, w_all, gpost)


def _relayout_kernel(w_ref, wp_ref, wz_ref):
    offs = np.cumsum((0,) + tuple(IN_SIZES))
    seg = {name: (int(offs[i]), int(offs[i + 1])) for i, name in enumerate(
        ("a_q", "a_kv", "a_g", "b_cq", "b_ckv", "b_kr", "c_q", "c_kv", "z"))}
    dst = (("a_q", C_AQ), ("a_kv", C_CMP), ("b_cq", C_BCQ), ("b_ckv", C_BCKV), ("b_kr", C_KRB + HEAD_DIM),
           ("c_q", C_CQ), ("c_kv", C_CKV))
    n, L, ct = w_ref.shape
    g0, n_gate, tail = seg["a_g"][0], 3 * NSA_HEADS, LANES - GATE_LANE0
    gate_row = lax.broadcasted_iota(jnp.int32, (tail, ct), 0) < n_gate
    rows = w_ref.reshape(n * L, ct)
    pick = lambda a, b, l: rows[pl.ds(L * a + l, b - a, stride=L), :]
    for l in range(L):
        wp_ref[l, C_KRB:C_KRB + HEAD_DIM, :] = jnp.zeros((HEAD_DIM, ct), BF16)
        for name, c0 in dst:
            a, b = seg[name]
            wp_ref[l, c0:c0 + (b - a), :] = pick(a, b, l).astype(BF16)
        wp_ref[l, C_KRB + GATE_LANE0:C_CQ, :] = jnp.where(gate_row, pick(g0, g0 + tail, l), 0.0).astype(BF16)
        wz_ref[l] = pick(seg["z"][0], seg["z"][1], l).astype(BF16)


def _prep_w_in(w):
    L, d, n = w.shape
    ct = LANES
    return pl.pallas_call(
        _relayout_kernel,
        grid=(d // ct,),
        in_specs=[pl.BlockSpec((n, L, ct), lambda r: (0, 0, r))],
        out_specs=[pl.BlockSpec((L, C_Z, ct), lambda r: (0, 0, r)),
                   pl.BlockSpec((L, IN_SIZES[-1], ct), lambda r: (0, 0, r))],
        out_shape=[jax.ShapeDtypeStruct((L, C_Z, d), BF16), jax.ShapeDtypeStruct((L, IN_SIZES[-1], d), BF16)],
        compiler_params=_cparams(("arbitrary",)),
        name="w_in_relayout",
    )(jnp.transpose(w, (2, 0, 1)))


def _prep_mla(w_uq, w_ukv):
    L = w_uq.shape[0]
    dq = MLA_NOPE + MLA_ROPE
    uq = w_uq.reshape(L, MLA_Q_RANK, MLA_HEADS, dq)
    uq = jnp.concatenate([uq, jnp.zeros((L, MLA_Q_RANK, MLA_HEADS, LANES - dq), uq.dtype)], axis=-1)
    ukv = w_ukv.reshape(L, MLA_KV_RANK, MLA_HEADS, MLA_NOPE + MLA_V)
    uk = jnp.concatenate([ukv[..., :MLA_NOPE], jnp.zeros((L, MLA_KV_RANK, MLA_HEADS, LANES - MLA_NOPE), ukv.dtype)], axis=-1)
    uv = jnp.concatenate([jnp.zeros((L, MLA_KV_RANK, MLA_HEADS, LANES - MLA_V), ukv.dtype), ukv[..., MLA_NOPE:]], axis=-1)
    return (uq.reshape(L, MLA_Q_RANK, MLA_HEADS * LANES).astype(BF16),
            uk.reshape(L, MLA_KV_RANK, MLA_HEADS * LANES).astype(BF16),
            uv.reshape(L, MLA_KV_RANK, MLA_HEADS * LANES).astype(BF16))


def _prep_compress(cmp_pos, cmp_w1, cmp_w2):
    L = cmp_pos.shape[0]
    pe = jnp.concatenate([cmp_pos[:, 0], cmp_pos[:, 1]], axis=-1)
    pet = pe[:, :CMP_STRIDE].reshape(L, 1, CMP_STRIDE * LANES)
    peb = pe[:, CMP_STRIDE:].reshape(L, 1, CMP_STRIDE * LANES)
    w1 = cmp_w1.reshape(L, 2, CMP_LEN, HEAD_DIM, CMP_HIDDEN)
    zero = jnp.zeros_like(w1[:, 0])
    w1k = jnp.concatenate([w1[:, 0], zero], axis=2)
    w1v = jnp.concatenate([zero, w1[:, 1]], axis=2)
    w1e = jnp.stack([w1k, w1v], axis=1)
    w1t = w1e[:, :, :CMP_STRIDE].reshape(L, 2, CMP_STRIDE * LANES, CMP_HIDDEN).astype(BF16)
    w1b = w1e[:, :, CMP_STRIDE:].reshape(L, 2, CMP_STRIDE * LANES, CMP_HIDDEN).astype(BF16)
    z2 = jnp.zeros_like(cmp_w2[:, 0])
    w2 = jnp.stack([jnp.concatenate([cmp_w2[:, 0], z2], axis=-1),
                    jnp.concatenate([z2, cmp_w2[:, 1]], axis=-1)], axis=1).astype(BF16)
    return pet, peb, w1t, w1b, w2


def _rope_tables(T):
    half = MLA_ROPE // 2
    inv = ROPE_THETA ** (-jnp.arange(half, dtype=F32) / half)
    ang = jnp.arange(T).astype(F32)[:, None] * inv[None, :]
    cos, sin = jnp.cos(ang), jnp.sin(ang)
    z = lambda n: jnp.zeros((T, n), F32)
    rc = jnp.concatenate([jnp.ones((T, MLA_NOPE), F32), cos, cos, z(LANES - MLA_NOPE - MLA_ROPE)], axis=1)
    rsa = jnp.concatenate([z(MLA_NOPE + half), sin, z(LANES - MLA_NOPE - MLA_ROPE)], axis=1)
    rsb = jnp.concatenate([z(MLA_NOPE), -sin, z(LANES - MLA_NOPE - half)], axis=1)
    return rc, rsa, rsb


def _static_tables(T):
    ns = T // SEL_BLOCK
    nc = T // CMP_STRIDE - 1
    ncp = T // CMP_STRIDE
    sstart = np.arange(ns) * SEL_BLOCK
    cstart = np.arange(nc) * CMP_STRIDE
    overlap = (np.clip(np.minimum(cstart[:, None] + CMP_LEN, sstart[None, :] + SEL_BLOCK)
                       - np.maximum(cstart[:, None], sstart[None, :]), 0, None) / CMP_STRIDE).astype(np.float32)
    ovt = np.zeros((ns, ncp), np.float32)
    ovt[:, :nc] = overlap.T
    eaug = np.zeros((T, LANES), np.float32)
    eaug[np.arange(T), HEAD_DIM + np.arange(T) // SEL_BLOCK] = 1.0
    gexp = np.zeros((2, LANES, 3 * NSA_HEADS * HEAD_DIM), np.float32)
    for h in range(NSA_HEADS):
        for j in range(3):
            c0 = NSA_HEADS * HEAD_DIM * j + HEAD_DIM * h
            gexp[:, GATE_LANE0 + 3 * h + j, c0:c0 + HEAD_DIM] = 1.0
    gexp = gexp.reshape(2 * LANES, -1)
    return jnp.asarray(ovt, BF16), jnp.asarray(eaug, BF16), jnp.asarray(gexp, BF16)


def kernel(x, w_in, w_out, norm_pre, norm_post, cmp_pos, cmp_w1, cmp_w2, mla_q_norm, mla_w_uq, mla_kv_norm,
           mla_w_ukv, swa_sinks, rel_bias):
    B, T, D = x.shape
    depth = w_in.shape[0]
    assert D == D_MODEL and T % (2 * FLASH_TILE) == 0 and T // SEL_BLOCK <= HEAD_DIM
    ncp = T // CMP_STRIDE

    w_in_p, w_z = _prep_w_in(w_in)
    wuq, wk, wv = _prep_mla(mla_w_uq, mla_w_ukv)
    pet, peb, w1t, w1b, w2 = _prep_compress(cmp_pos, cmp_w1, cmp_w2)
    w_out_b = w_out.astype(BF16)
    rope_c, rope_sa, rope_sb = _rope_tables(T)
    ovt, eaug, gexp = _static_tables(T)

    bc = _bias_table(rel_bias, 0, NSA_HEADS, ncp, T, CMP_STRIDE, 1, -(CMP_LEN - 1), 0, 1 << 30, 32, mult=LOG2E)
    bc = jnp.transpose(bc.reshape(NSA_HEADS, ncp, T), (0, 2, 1))
    tab_near = _bias_table(rel_bias, 0, NSA_HEADS, FLASH_TILE, 2 * FLASH_TILE, 1, -1, FLASH_TILE, 0, 1 << 30, 256,
                           mult=LOG2E).reshape(NSA_HEADS, FLASH_TILE, 2 * FLASH_TILE)
    tab_win = _bias_table(rel_bias, 0, NSA_HEADS, WIN_TILE, NSA_WINDOW + WIN_TILE, 1, -1, 0, 0, NSA_WINDOW, WIN_TILE,
                          mult=LOG2E, nblk=NSA_WINDOW // WIN_TILE + 1, blk_d0=WIN_TILE)
    tab_swa = _bias_table(rel_bias, NSA_HEADS, SWA_HEADS, Q_TILE, SWA_WINDOW + Q_TILE, 1, -1, 0, 0, SWA_WINDOW, Q_TILE,
                          mult=LOG2E, nblk=SWA_WINDOW // Q_TILE + 1, blk_d0=Q_TILE)
    b31 = rel_bias[:NSA_HEADS, REL_BUCKETS - 1] * LOG2E

    x2 = x.reshape(B * T, D)
    for l in range(depth):
        (qa, cmpkv, slck, slcv, win, qm, km, vm, krb, cq, ckv) = _project(
            x2, norm_pre[l][None], w_in_p, l, mla_q_norm[l][None], wuq[l], mla_kv_norm[l][None], wk[l], wv[l],
            rope_c, rope_sa, rope_sb, eaug, T, PROJ_ROWS)
        r3 = lambda a: a.reshape(B, T, a.shape[-1])
        kvc = _compress(r3(cmpkv), pet[l], peb[l], w1t[l], w1b[l], w2[l])
        qa3 = r3(qa)
        ocmp, selb = _cmp_attention(qa3, kvc, bc, ovt, T)
        owin = _band_attention(qa3, r3(win), tab_win, None, NSA_HEADS, NSA_WINDOW, T, BF16)
        oslc = _slc_attention(b31, qa3, selb, r3(slck), r3(slcv), tab_near, T)
        oc = _band_attention(r3(cq), r3(ckv), tab_swa, swa_sinks[l] * LOG2E, SWA_HEADS, SWA_WINDOW, T, BF16)
        ob = _mla_attention(r3(qm), r3(km), r3(vm), T)
        flat = lambda a: a.reshape(B * T, a.shape[-1])
        x2 = _out_project(x2, norm_pre[l][None], w_z, l, flat(ocmp), flat(oslc), flat(owin), krb, gexp,
                          flat(ob), flat(oc), w_out_b, norm_post[l][None], OUT_ROWS)
    return x2.reshape(B, T, D)
```

```python
import functools
import math

import numpy as np
import jax
import jax.numpy as jnp
from jax import lax
from jax.experimental import pallas as pl
from jax.experimental.pallas import tpu as pltpu

F32 = jnp.float32
BF16 = jnp.bfloat16

D_MODEL = 1024
HEAD_DIM = 64
NSA_HEADS = 4
CMP_LEN = 32
CMP_STRIDE = 16
CMP_HIDDEN = 128
SEL_BLOCK = 64
SEL_TOP = 16
NSA_WINDOW = 512
MLA_HEADS = 4
MLA_Q_RANK = 256
MLA_KV_RANK = 128
MLA_NOPE = 64
MLA_ROPE = 32
MLA_V = 64
ROPE_THETA = 10000.0
SWA_HEADS = 8
SWA_WINDOW = 128
REL_BUCKETS = 32
REL_MAX_DIST = 512
NORM_EPS = 1e-6
NEG = -1e30
BIG = 1e9
LOG2E = math.log2(math.e)
MLA_SCALE2 = (MLA_NOPE + MLA_ROPE) ** -0.5 * LOG2E
IN_SIZES = (256, 384, 12, 256, 128, 32, 512, 128, 1024)

LANES = 128
Q_TILE = 128
WIN_TILE = 128
BAND_ROWS = 4096
PROJ_ROWS = 1024
OUT_SUBTILES = 4
OUT_ROWS = 1024
CMP_GROUP = 4
FLASH_TILE = 512
SLC_DIAG_PARTS = 4
MLA_DIAG_PARTS = 4
GATE_LANE0 = 96
VMEM_LIMIT = 56 * 1024 * 1024

C_AQ, C_CMP, C_SLC, C_WIN, C_BCQ, C_BCKV, C_KRB, C_CQ, C_CKV, C_Z = (
    0, 256, 384, 512, 640, 896, 1024, 1152, 1664, 1792)


def _dot(a, b):
    return jnp.dot(a, b, preferred_element_type=F32)


def _dot_nt(a, b):
    return lax.dot_general(a, b, (((1,), (1,)), ((), ())), preferred_element_type=F32)


def _bucket_thresholds():
    d = np.arange(0, 4 * REL_MAX_DIST)
    exact = REL_BUCKETS // 2
    large = exact + (np.log(np.maximum(d, 1).astype(np.float32) / np.float32(exact))
                     / np.float32(math.log(REL_MAX_DIST / exact)) * np.float32(REL_BUCKETS - exact)).astype(np.int32)
    b = np.where(d < exact, d, np.minimum(large, REL_BUCKETS - 1))
    assert np.all(np.diff(b) >= 0) and b[-1] == REL_BUCKETS - 1
    return [int(np.argmax(b >= k)) for k in range(REL_BUCKETS)]


_THRESH = _bucket_thresholds()


def _cparams(sem):
    return pltpu.CompilerParams(dimension_semantics=sem, vmem_limit_bytes=VMEM_LIMIT)


def _table_kernel(h0, row_step, sign, d0, blk_d0, lo, hi, mult, transposed, rb_ref, out_ref):
    h = pl.program_id(0) + h0
    blk = pl.program_id(1)
    if transposed:
        cc, rt = out_ref.shape[2], out_ref.shape[3]
        base = d0 + blk * blk_d0 + sign * cc * pl.program_id(2)
    else:
        rt, cc = out_ref.shape[2], out_ref.shape[3]
        base = d0 + blk * blk_d0 - sign * row_step * rt * pl.program_id(2)
    width = -(-(row_step * rt + cc) // LANES) * LANES
    w = lax.broadcasted_iota(jnp.int32, (8, width), 1)
    dist = base + sign * jnp.where(w < cc, w, w - width)
    acc = jnp.full(dist.shape, rb_ref[h, 0], F32)
    for k in range(1, REL_BUCKETS):
        acc = jnp.where(dist >= _THRESH[k], rb_ref[h, k], acc)
    row = jnp.where((dist >= lo) & (dist < hi), acc * mult, NEG)
    rolled = pltpu.roll(jnp.broadcast_to(row[:1], (rt, width)), 0, 1, stride=row_step, stride_axis=0)
    out_ref[0, 0] = rolled[:, :cc].T if transposed else rolled[:, :cc]


def _bias_table(rel_bias, h0, nh, rows, cols, row_step, sign, d0, lo, hi, rt, mult=1.0, nblk=1, blk_d0=0,
                transposed=False):
    tile, shape = ((rt, rows), (cols, rows)) if transposed else ((rt, cols), (rows, cols))
    return pl.pallas_call(
        functools.partial(_table_kernel, h0, row_step, sign, d0, blk_d0, lo, hi, mult, transposed),
        grid=(nh, nblk, shape[0] // tile[0]),
        in_specs=[pl.BlockSpec(memory_space=pltpu.SMEM)],
        out_specs=pl.BlockSpec((1, 1) + tile, lambda h, b, r: (h, b, r, 0)),
        out_shape=jax.ShapeDtypeStruct((nh, nblk) + shape, F32),
        compiler_params=_cparams(("arbitrary", "arbitrary", "arbitrary")),
        name="bias_table",
    )(rel_bias)


def _rms(v, g):
    return v * lax.rsqrt(jnp.mean(v * v, axis=-1, keepdims=True) + NORM_EPS) * g


def _proj_kernel(x_ref, gpre_ref, w_ref, qn_ref, wuq_ref, kvn_ref, wk_ref, wv_ref,
                 rc_ref, rsa_ref, rsb_ref, eaug_ref,
                 qa_ref, cmp_ref, slck_ref, slcv_ref, win_ref, qm_ref, km_ref, vm_ref,
                 krb_ref, cq_ref, ckv_ref):
    hb = _rms(x_ref[...], gpre_ref[...]).astype(BF16)
    bounds = ((C_AQ, C_CMP), (C_CMP, C_WIN), (C_WIN, C_KRB), (C_KRB, C_Z))
    chunks = {ab: _dot_nt(hb, w_ref[ab[0]:ab[1], :]) for ab in bounds}
    lane = lax.broadcasted_iota(jnp.int32, (hb.shape[0], LANES), 1)
    lo = lane < HEAD_DIM

    def proj(c0, c1):
        for (a, b), y in chunks.items():
            if a <= c0 and c1 <= b:
                return y[:, c0 - a:c1 - a]
        raise AssertionError((c0, c1))

    qa_ref[...] = (proj(C_AQ, C_CMP) * (0.125 * LOG2E)).astype(BF16)
    cmp_ref[...] = proj(C_CMP, C_SLC)
    slc = proj(C_SLC, C_WIN)
    slcv_ref[...] = jnp.where(lo, 1.0, slc).astype(BF16)
    slck_ref[...] = jnp.where(lo, slc, eaug_ref[...].astype(F32)).astype(BF16)
    win_ref[...] = proj(C_WIN, C_BCQ).astype(BF16)
    cq_ref[...] = (proj(C_CQ, C_CKV) * (0.125 * LOG2E)).astype(BF16)
    ckv_ref[...] = proj(C_CKV, C_Z).astype(BF16)

    rc, rsa, rsb = rc_ref[...], rsa_ref[...], rsb_ref[...]

    def rope(v):
        return v * rc + pltpu.roll(v, 16, 1) * rsa + pltpu.roll(v, 112, 1) * rsb

    cqn = _rms(proj(C_BCQ, C_BCKV), qn_ref[...]).astype(BF16)
    qm = _dot(cqn, wuq_ref[...])
    for h in range(MLA_HEADS):
        qh = rope(qm[:, LANES * h:LANES * (h + 1)]) * MLA_SCALE2
        qm_ref[:, LANES * h:LANES * (h + 1)] = qh.astype(BF16)

    krb = proj(C_KRB, C_CQ)
    krb_ref[...] = krb
    krr = jnp.where(lo | (lane >= HEAD_DIM + MLA_ROPE), 0.0, rope(krb))

    ckvn = _rms(proj(C_BCKV, C_KRB), kvn_ref[...]).astype(BF16)
    kn = _dot(ckvn, wk_ref[...])
    for h in range(MLA_HEADS):
        km_ref[:, LANES * h:LANES * (h + 1)] = (kn[:, LANES * h:LANES * (h + 1)] + krr).astype(BF16)
    vm = _dot(ckvn, wv_ref[...])
    for h in range(MLA_HEADS):
        vm_ref[:, LANES * h:LANES * (h + 1)] = jnp.where(lo, 1.0, vm[:, LANES * h:LANES * (h + 1)]).astype(BF16)


def _project(x2, gpre, w_all, layer, qn, wuq, kvn, wk, wv, rope_c, rope_sa, rope_sb, eaug, T, tm):
    BT = x2.shape[0]
    nt = T // tm
    row = lambda i: (i, 0)
    fix = lambda i: (0, 0)
    pos = lambda i: (i % nt, 0)
    of_layer = lambda a: pl.BlockSpec((None,) + a.shape[1:], lambda i: (layer, 0, 0))
    widths = [(256, BF16), (128, F32), (128, BF16), (128, BF16), (128, BF16), (512, BF16), (512, BF16),
              (512, BF16), (128, F32), (512, BF16), (128, BF16)]
    return pl.pallas_call(
        _proj_kernel,
        grid=(BT // tm,),
        in_specs=[pl.BlockSpec((tm, D_MODEL), row), pl.BlockSpec((1, D_MODEL), fix),
                  of_layer(w_all), pl.BlockSpec(qn.shape, fix), pl.BlockSpec(wuq.shape, fix),
                  pl.BlockSpec(kvn.shape, fix), pl.BlockSpec(wk.shape, fix), pl.BlockSpec(wv.shape, fix),
                  pl.BlockSpec((tm, LANES), pos), pl.BlockSpec((tm, LANES), pos), pl.BlockSpec((tm, LANES), pos),
                  pl.BlockSpec((tm, LANES), pos)],
        out_specs=[pl.BlockSpec((tm, wd), row) for wd, _ in widths],
        out_shape=[jax.ShapeDtypeStruct((BT, wd), dt) for wd, dt in widths],
        compiler_params=_cparams(("arbitrary",)),
        name="proj",
    )(x2, gpre, w_all, qn, wuq, kvn, wk, wv, rope_c, rope_sa, rope_sb, eaug)


def _compress_kernel(c_ref, pet_ref, peb_ref, w1t_ref, w1b_ref, w2_ref, o_ref):
    nch = c_ref.shape[1] // CMP_STRIDE
    c = jnp.concatenate([c_ref[0, pl.ds(j, nch, stride=CMP_STRIDE), :] for j in range(CMP_STRIDE)], axis=1)
    top = (c + pet_ref[...]).astype(BF16)
    bot = (c + peb_ref[...]).astype(BF16)
    out = jnp.zeros((nch, LANES), F32)
    for i in range(2):
        pre = _dot(top, w1t_ref[i]) + pltpu.roll(_dot(bot, w1b_ref[i]), nch - 1, 0)
        hid = pre * (1.0 / (1.0 + jnp.exp(-pre)))
        out = out + _dot(hid.astype(BF16), w2_ref[i])
    o_ref[0] = out.astype(BF16)


def _compress(tokens, pet, peb, w1t, w1b, w2):
    B, T, width = tokens.shape
    nch = T // CMP_STRIDE
    full = lambda a: pl.BlockSpec(a.shape, lambda b: (0,) * a.ndim)
    return pl.pallas_call(
        _compress_kernel,
        grid=(B,),
        in_specs=[pl.BlockSpec((1, T, width), lambda b: (b, 0, 0)),
                  full(pet), full(peb), full(w1t), full(w1b), full(w2)],
        out_specs=pl.BlockSpec((1, nch, LANES), lambda b: (b, 0, 0)),
        out_shape=jax.ShapeDtypeStruct((B, nch, LANES), BF16),
        compiler_params=_cparams(("arbitrary",)),
        name="compress",
    )(tokens, pet, peb, w1t, w1b, w2)


def _head_slabs(qblk, n_heads):
    q = qblk.astype(F32)
    lo = lax.broadcasted_iota(jnp.int32, (q.shape[0], LANES), 1) < HEAD_DIM
    out = []
    for j in range(n_heads // 2):
        slab = q[:, LANES * j:LANES * (j + 1)]
        out.append(jnp.where(lo, slab, 0.0))
        out.append(jnp.where(lo, pltpu.roll(slab, HEAD_DIM, 1), 0.0))
    return out


def _merge_upper(accs):
    lo = lax.broadcasted_iota(jnp.int32, accs[0].shape, 1) < HEAD_DIM
    slabs = [jnp.where(lo, pltpu.roll(accs[2 * j], HEAD_DIM, 1), accs[2 * j + 1]) for j in range(len(accs) // 2)]
    return jnp.concatenate(slabs, axis=1)


def _merge_normalized(accs):
    lo = lax.broadcasted_iota(jnp.int32, accs[0].shape, 1) < HEAD_DIM
    slabs = []
    for j in range(len(accs) // 2):
        a, b = accs[2 * j], accs[2 * j + 1]
        out = jnp.where(lo, pltpu.roll(a, HEAD_DIM, 1), b)
        den = jnp.where(lo, a, pltpu.roll(b, HEAD_DIM, 1))
        slabs.append(out / den)
    return jnp.concatenate(slabs, axis=1)


def _row_max_lanes(s):
    return jnp.broadcast_to(jnp.max(s, axis=-1, keepdims=True), (s.shape[0], LANES))


def _sub_lanes(s, ref):
    return jnp.concatenate([s[:, LANES * i:LANES * (i + 1)] - ref for i in range(s.shape[1] // LANES)], axis=1)


def _topk_mask_t(imp, t0):
    ns, tq = imp.shape
    srow = lax.broadcasted_iota(jnp.int32, (ns, tq), 0)
    cur = (t0 + lax.broadcasted_iota(jnp.int32, (ns, tq), 1)) // SEL_BLOCK
    forced = (srow == 0) | (srow == cur) | (srow == cur - 1)
    x = jnp.where(forced, BIG, jnp.where(srow <= cur, imp, -BIG))
    sub = 8
    groups = [x[sub * v:sub * (v + 1)] for v in range(ns // sub)]
    rows_in = [srow[sub * v:sub * (v + 1)] for v in range(ns // sub)]
    cnts = [jnp.zeros((sub, tq), F32) for _ in groups]
    for sp in range(ns):
        other = jnp.broadcast_to(x[sp:sp + 1, :], (sub, tq))
        for v, xv in enumerate(groups):
            if sub * v > sp:
                beats = other >= xv
            elif sub * v + sub - 1 < sp:
                beats = other > xv
            else:
                beats = (other > xv) | ((other == xv) & (rows_in[v] > sp))
            cnts[v] = cnts[v] + jnp.where(beats, 1.0, 0.0)
    cnt = jnp.concatenate(cnts, axis=0)
    sel = (cnt < float(min(SEL_TOP, ns))) & (srow <= cur)
    return jnp.where(sel, 0.0, NEG)


def _cmp_kernel(q_ref, kvc_ref, bc_ref, ovt_ref, ocmp_ref, selb_ref):
    rows = CMP_GROUP * Q_TILE
    ncp, ns = kvc_ref.shape[1], ovt_ref.shape[0]
    step = pl.program_id(1)
    lo_step, nse = 0, SEL_TOP
    while lo_step * rows < ns * SEL_BLOCK:
        nse = min(nse, ns)
        hi_step = nse * SEL_BLOCK // rows
        ncw = min(ncp, -(-nse * (SEL_BLOCK // CMP_STRIDE) // LANES) * LANES)
        if hi_step > lo_step:
            pl.when((step >= lo_step) & (step < hi_step))(
                functools.partial(_cmp_body, ncw, nse, q_ref, kvc_ref, bc_ref, ovt_ref, ocmp_ref, selb_ref))
            lo_step = hi_step
        nse += SEL_TOP


def _cmp_body(ncw, ns, q_ref, kvc_ref, bc_ref, ovt_ref, ocmp_ref, selb_ref):
    tq = Q_TILE
    kvc = kvc_ref[0, :ncw, :]
    ovt = ovt_ref[:ns, :ncw]
    groups = range(CMP_GROUP)
    sls = [slice(g * tq, (g + 1) * tq) for g in groups]
    s_alls = []
    for g in groups:
        qs = _head_slabs(q_ref[0, sls[g], :], NSA_HEADS)
        s_alls.append(_dot_nt(jnp.concatenate([q.astype(BF16) for q in qs], axis=0), kvc))
    psums, o_alls = [], []
    for g in groups:
        psum = None
        ps = []
        for h in range(NSA_HEADS):
            s = s_alls[g][h * tq:(h + 1) * tq] + bc_ref[h, sls[g], :ncw]
            m = jnp.max(s, axis=-1, keepdims=True)
            e = jnp.exp2(s - m)
            l = jnp.sum(e, axis=-1, keepdims=True)
            p = e * jnp.where(m > 0.5 * NEG, 1.0 / l, 0.0)
            psum = p if psum is None else psum + p
            ps.append(p.astype(BF16))
        psums.append(psum)
        o_alls.append(_dot(jnp.concatenate(ps, axis=0), kvc))
    imps = []
    for g in groups:
        hi = psums[g].astype(BF16)
        lo = (psums[g] - hi.astype(F32)).astype(BF16)
        imps.append(_dot_nt(ovt, hi) + _dot_nt(ovt, lo))
    for g in groups:
        ocmp = _merge_upper([o_alls[g][h * tq:(h + 1) * tq] for h in range(NSA_HEADS)])
        ocmp_ref[0, sls[g], :] = ocmp.astype(ocmp_ref.dtype)
        sb = _topk_mask_t(imps[g], (pl.program_id(1) * CMP_GROUP + g) * tq)
        if ns < HEAD_DIM:
            sb = jnp.concatenate([sb, jnp.full((HEAD_DIM - ns, tq), NEG, F32)], axis=0)
        full = jnp.concatenate([jnp.zeros((HEAD_DIM, tq), F32), sb], axis=0)
        selb_ref[0, sls[g], :] = full.T.astype(BF16)


def _cmp_attention(qa, kvc, bc, ovt, T):
    B = qa.shape[0]
    ncp = kvc.shape[1]
    rows = CMP_GROUP * Q_TILE
    return pl.pallas_call(
        _cmp_kernel,
        grid=(B, T // rows),
        in_specs=[pl.BlockSpec((1, rows, 256), lambda b, n: (b, n, 0)),
                  pl.BlockSpec((1, ncp, LANES), lambda b, n: (b, 0, 0)),
                  pl.BlockSpec((NSA_HEADS, rows, ncp), lambda b, n: (0, n, 0)),
                  pl.BlockSpec(ovt.shape, lambda b, n: (0, 0))],
        out_specs=[pl.BlockSpec((1, rows, 256), lambda b, n: (b, n, 0)),
                   pl.BlockSpec((1, rows, LANES), lambda b, n: (b, n, 0))],
        out_shape=[jax.ShapeDtypeStruct((B, T, 256), BF16), jax.ShapeDtypeStruct((B, T, LANES), BF16)],
        compiler_params=_cparams(("arbitrary", "arbitrary")),
        name="cmp_select",
    )(qa, kvc, bc, ovt)


def _softmax_tile(s2, rows, m_s, shift=None):
    m_t = _row_max_lanes(s2)
    if shift is not None:
        m_t = m_t + shift
    m_old = m_s[rows]
    m_new = jnp.maximum(m_old, m_t)
    p = jnp.exp2(_sub_lanes(s2, m_new if shift is None else m_new - shift))
    alpha = jnp.exp2(m_old - m_new)
    m_s[rows] = m_new
    return p.astype(BF16), alpha


def _slc_kernel(b31_ref, q_ref, selb_ref, ka_ref, kv_ref, tab_ref, o_ref, m_s, acc_s):
    tq = q_ref.shape[1]
    H = NSA_HEADS
    n = pl.program_id(1)
    qs = _head_slabs(q_ref[0], H)
    selb = selb_ref[0].astype(F32)
    qst = jnp.concatenate([(qs[h] + selb).astype(BF16) for h in range(H)], axis=0)
    m_s[...] = jnp.full(m_s.shape, -jnp.inf, F32)
    acc_s[...] = jnp.zeros(acc_s.shape, F32)

    def scores(k0, width=tq):
        k0 = pl.multiple_of(k0, tq)
        return _dot_nt(qst, ka_ref[0, pl.ds(k0, width), :]), kv_ref[0, pl.ds(k0, width), :]

    def update(s_all, kv):
        ps, alphas = [], []
        for h in range(H):
            rows = slice(h * tq, (h + 1) * tq)
            p, alpha = _softmax_tile(s_all[rows], rows, m_s, shift=b31_ref[h])
            ps.append(p)
            alphas.append(alpha)
        pv = _dot(jnp.concatenate(ps, axis=0), kv)
        for h in range(H):
            rows = slice(h * tq, (h + 1) * tq)
            acc_s[rows] = alphas[h] * acc_s[rows] + pv[rows]

    def far_triple(i, carry):
        update(*scores(3 * i * tq, 3 * tq))
        return carry

    n_far = jnp.maximum(n - 1, 0)
    lax.fori_loop(0, n_far // 3, far_triple, 0)

    @pl.when(n_far % 3 == 1)
    def _():
        update(*scores((n_far - 1) * tq))

    @pl.when(n_far % 3 == 2)
    def _():
        update(*scores((n_far - 2) * tq, 2 * tq))

    def near_diagonal(k0, before):
        k0 = pl.multiple_of(k0, tq)
        half = tq // SLC_DIAG_PARTS
        parts = []
        for part in range(SLC_DIAG_PARTS):
            kw = before + half * (part + 1)
            q_part = jnp.concatenate([qst[h * tq + part * half:h * tq + (part + 1) * half] for h in range(H)], axis=0)
            parts.append((_dot_nt(q_part, ka_ref[0, pl.ds(k0, kw), :]), kv_ref[0, pl.ds(k0, kw), :], kw))
        for part, (s_all, kv, kw) in enumerate(parts):
            ps, alphas = [], []
            for h in range(H):
                rows = slice(h * tq + part * half, h * tq + (part + 1) * half)
                bias = tab_ref[h, part * half:(part + 1) * half, tq - before:tq - before + kw]
                p, alpha = _softmax_tile(s_all[h * half:(h + 1) * half] + bias, rows, m_s)
                ps.append(p)
                alphas.append(alpha)
            pv = _dot(jnp.concatenate(ps, axis=0), kv)
            for h in range(H):
                rows = slice(h * tq + part * half, h * tq + (part + 1) * half)
                acc_s[rows] = alphas[h] * acc_s[rows] + pv[h * half:(h + 1) * half]

    @pl.when(n >= 1)
    def _():
        near_diagonal(jnp.maximum(n - 1, 0) * tq, tq)

    @pl.when(n == 0)
    def _():
        near_diagonal(0, 0)

    o_ref[0] = _merge_normalized([acc_s[h * tq:(h + 1) * tq] for h in range(H)]).astype(o_ref.dtype)


def _slc_attention(b31, qa, selb, slck, slcv, tab, T):
    B = qa.shape[0]
    tq = FLASH_TILE
    H = NSA_HEADS
    tile = lambda w: pl.BlockSpec((1, tq, w), lambda b, n: (b, n, 0))
    return pl.pallas_call(
        _slc_kernel,
        grid=(B, T // tq),
        in_specs=[pl.BlockSpec(memory_space=pltpu.SMEM),
                  tile(256), tile(LANES),
                  pl.BlockSpec((1, T, LANES), lambda b, n: (b, 0, 0)),
                  pl.BlockSpec((1, T, LANES), lambda b, n: (b, 0, 0)),
                  pl.BlockSpec(tab.shape, lambda b, n: (0, 0, 0))],
        out_specs=tile(256),
        out_shape=jax.ShapeDtypeStruct((B, T, 256), BF16),
        scratch_shapes=[pltpu.VMEM((H * tq, LANES), F32), pltpu.VMEM((H * tq, LANES), F32)],
        compiler_params=_cparams(("arbitrary", "arbitrary")),
        name="slc_attention",
    )(b31, qa, selb, slck, slcv, tab)


def _band_kernel(n_heads, window, has_sink, n_sub, tq, *refs):
    if has_sink:
        q_ref, kv_ref, tab_ref, sink_ref, o_ref = refs
    else:
        q_ref, kv_ref, tab_ref, o_ref = refs
    span = window + tq
    lo_kv = lax.broadcasted_iota(jnp.int32, (span, LANES), 1) < HEAD_DIM
    lo_o = lax.broadcasted_iota(jnp.int32, (n_heads * tq, LANES), 1) < HEAD_DIM
    groups = range(n_sub)
    tiles = [pl.program_id(1) * n_sub + g for g in groups]
    kvs, dots = [], []
    for g in groups:
        k0 = pl.multiple_of(jnp.maximum(tiles[g] * tq - window, 0), tq)
        kvs.append(kv_ref[0, pl.ds(k0, span), :])
        qs = _head_slabs(q_ref[0, g * tq:(g + 1) * tq, :], n_heads)
        dots.append(_dot_nt(jnp.concatenate([q.astype(BF16) for q in qs], axis=0), kvs[g]))
    accs, ms = [], []
    for g in groups:
        s = dots[g] + tab_ref[jnp.minimum(tiles[g], window // tq)]
        m = _row_max_lanes(s)
        if has_sink:
            m = jnp.maximum(m, sink_ref[...])
        e = jnp.exp2(_sub_lanes(s, m))
        ms.append(m)
        accs.append(_dot(e.astype(BF16), jnp.where(lo_kv, 1.0, kvs[g]).astype(BF16)))
    for g in groups:
        acc = accs[g]
        if has_sink:
            acc = acc + jnp.where(lo_o, jnp.exp2(sink_ref[...] - ms[g]), 0.0)
        out = _merge_normalized([acc[h * tq:(h + 1) * tq] for h in range(n_heads)])
        o_ref[0, g * tq:(g + 1) * tq, :] = out.astype(o_ref.dtype)


def _band_attention(q, kv_pad, tab, sinks, n_heads, window, T, out_dtype):
    B = q.shape[0]
    width = n_heads * HEAD_DIM
    has_sink = sinks is not None
    nvar, tq = tab.shape[1], tab.shape[2]
    tab2 = jnp.transpose(tab, (1, 0, 2, 3)).reshape(nvar, n_heads * tq, window + tq)
    n_sub = (BAND_ROWS if has_sink else 2 * BAND_ROWS) // (n_heads * tq)
    rows = n_sub * tq
    in_specs = [pl.BlockSpec((1, rows, width), lambda b, n: (b, n, 0)),
                pl.BlockSpec((1, T, LANES), lambda b, n: (b, 0, 0)),
                pl.BlockSpec(tab2.shape, lambda b, n: (0, 0, 0))]
    args = (q, kv_pad, tab2)
    if has_sink:
        rep = jnp.broadcast_to(jnp.repeat(sinks, tq)[:, None], (n_heads * tq, LANES))
        in_specs.append(pl.BlockSpec(rep.shape, lambda b, n: (0, 0)))
        args = args + (rep,)
    return pl.pallas_call(
        functools.partial(_band_kernel, n_heads, window, has_sink, n_sub, tq),
        grid=(B, T // rows),
        in_specs=in_specs,
        out_specs=pl.BlockSpec((1, rows, width), lambda b, n: (b, n, 0)),
        out_shape=jax.ShapeDtypeStruct((B, T, width), out_dtype),
        compiler_params=_cparams(("arbitrary", "arbitrary")),
        name="band_sink" if has_sink else "band_window",
    )(*args)


def _mla_kernel(q_ref, k_ref, v_ref, o_ref, m_s, acc_s):
    tq = q_ref.shape[1]
    H = MLA_HEADS
    n = pl.program_id(1)
    m_s[...] = jnp.full(m_s.shape, -jnp.inf, F32)
    acc_s[...] = jnp.zeros(acc_s.shape, F32)

    def scores(k0, width=tq):
        k0 = pl.multiple_of(k0, tq)
        return k0, width, [_dot_nt(q_ref[0, :, LANES * h:LANES * (h + 1)],
                                   k_ref[0, pl.ds(k0, width), LANES * h:LANES * (h + 1)]) for h in range(H)]

    def update(k0, width, ss):
        ps, alphas = [], []
        for h in range(H):
            p, alpha = _softmax_tile(ss[h], slice(h * tq, (h + 1) * tq), m_s)
            ps.append(p)
            alphas.append(alpha)
        for h in range(H):
            rows = slice(h * tq, (h + 1) * tq)
            v = v_ref[0, pl.ds(k0, width), LANES * h:LANES * (h + 1)]
            acc_s[rows] = alphas[h] * acc_s[rows] + _dot(ps[h], v)

    def far_triple(i, carry):
        update(*scores(3 * i * tq, 3 * tq))
        return carry

    n_far = jnp.maximum(n - 1, 0)
    lax.fori_loop(0, n_far // 3, far_triple, 0)

    @pl.when(n_far % 3 == 1)
    def _():
        update(*scores((n_far - 1) * tq))

    @pl.when(n_far % 3 == 2)
    def _():
        update(*scores((n_far - 2) * tq, 2 * tq))

    def near_diagonal(k0, before):
        k0 = pl.multiple_of(k0, tq)
        half = tq // MLA_DIAG_PARTS
        parts = []
        for part in range(MLA_DIAG_PARTS):
            kw = before + half * (part + 1)
            r = slice(part * half, (part + 1) * half)
            parts.append((kw, [_dot_nt(q_ref[0, r, LANES * h:LANES * (h + 1)],
                                       k_ref[0, pl.ds(k0, kw), LANES * h:LANES * (h + 1)]) for h in range(H)]))
        for part, (kw, ss) in enumerate(parts):
            row = lax.broadcasted_iota(jnp.int32, (half, kw), 0)
            col = lax.broadcasted_iota(jnp.int32, (half, kw), 1)
            causal = col <= row + (before + part * half)
            ps, alphas = [], []
            for h in range(H):
                rows = slice(h * tq + part * half, h * tq + (part + 1) * half)
                p, alpha = _softmax_tile(jnp.where(causal, ss[h], NEG), rows, m_s)
                ps.append(p)
                alphas.append(alpha)
            for h in range(H):
                rows = slice(h * tq + part * half, h * tq + (part + 1) * half)
                v = v_ref[0, pl.ds(k0, kw), LANES * h:LANES * (h + 1)]
                acc_s[rows] = alphas[h] * acc_s[rows] + _dot(ps[h], v)

    @pl.when(n >= 1)
    def _():
        near_diagonal(jnp.maximum(n - 1, 0) * tq, tq)

    @pl.when(n == 0)
    def _():
        near_diagonal(0, 0)

    o_ref[0] = _merge_normalized([acc_s[h * tq:(h + 1) * tq] for h in range(H)]).astype(o_ref.dtype)


def _mla_attention(qm, km, vm, T):
    B = qm.shape[0]
    tq = FLASH_TILE
    H = MLA_HEADS
    return pl.pallas_call(
        _mla_kernel,
        grid=(B, T // tq),
        in_specs=[pl.BlockSpec((1, tq, 512), lambda b, n: (b, n, 0)),
                  pl.BlockSpec((1, T, 512), lambda b, n: (b, 0, 0)),
                  pl.BlockSpec((1, T, 512), lambda b, n: (b, 0, 0))],
        out_specs=pl.BlockSpec((1, tq, 256), lambda b, n: (b, n, 0)),
        out_shape=jax.ShapeDtypeStruct((B, T, 256), BF16),
        scratch_shapes=[pltpu.VMEM((H * tq, LANES), F32), pltpu.VMEM((H * tq, LANES), F32)],
        compiler_params=_cparams(("arbitrary", "arbitrary")),
        name="mla_attention",
    )(qm, km, vm)


def _out_kernel(x_ref, gpre_ref, wz_ref, ocmp_ref, oslc_ref, owin_ref, krb_ref, gexp_ref, ob_ref, oc_ref, w_ref,
                gpost_ref, o_ref):
    sub = x_ref.shape[0] // OUT_SUBTILES
    rs = [slice(t * sub, (t + 1) * sub) for t in range(OUT_SUBTILES)]
    w_a = NSA_HEADS * HEAD_DIM
    hs = [_rms(x_ref[r, :], gpre_ref[...]).astype(BF16) for r in rs]
    zs = [_dot_nt(h, wz_ref[...]) for h in hs]
    mixeds = []
    for r, z in zip(rs, zs):
        sig = 1.0 / (1.0 + jnp.exp(-krb_ref[r, :]))
        hi = sig.astype(BF16)
        lo = (sig - hi.astype(F32)).astype(BF16)
        gates = _dot(jnp.concatenate([hi, lo], axis=1), gexp_ref[...])
        oa = None
        for j, br in enumerate((ocmp_ref, oslc_ref, owin_ref)):
            term = gates[:, w_a * j:w_a * (j + 1)] * br[r, :].astype(F32)
            oa = term if oa is None else oa + term
        heads = jnp.concatenate([oa, ob_ref[r, :].astype(F32), oc_ref[r, :].astype(F32)], axis=1)
        mixeds.append((heads * (z * (1.0 / (1.0 + jnp.exp(-z))))).astype(BF16))
    ys = [_dot(m, w_ref[...]) for m in mixeds]
    for r, y in zip(rs, ys):
        o_ref[r, :] = x_ref[r, :] + _rms(y, gpost_ref[...])


def _out_project(x2, gpre, wz_all, layer, ocmp, oslc, owin, krb, gexp, ob, oc, w_all, gpost, tm):
    BT = x2.shape[0]
    row = lambda i: (i, 0)
    fix = lambda i: (0, 0)
    spec = lambda a: pl.BlockSpec((tm, a.shape[1]), row)
    full = lambda a: pl.BlockSpec(a.shape, fix)
    of_layer = lambda a: pl.BlockSpec((None,) + a.shape[1:], lambda i: (layer, 0, 0))
    return pl.pallas_call(
        _out_kernel,
        grid=(BT // tm,),
        in_specs=[spec(x2), full(gpre), of_layer(wz_all), spec(ocmp), spec(oslc), spec(owin), spec(krb), full(gexp),
                  spec(ob), spec(oc), of_layer(w_all), full(gpost)],
        out_specs=pl.BlockSpec((tm, D_MODEL), row),
        out_shape=jax.ShapeDtypeStruct((BT, D_MODEL), F32),
        compiler_params=_cparams(("arbitrary",)),
        name="out_proj",
    )(x2, gpre, wz_all, ocmp, oslc, owin, krb, gexp, ob, oc, w_all, gpost)


def _relayout_kernel(w_ref, wp_ref, wz_ref):
    offs = np.cumsum((0,) + tuple(IN_SIZES))
    seg = {name: (int(offs[i]), int(offs[i + 1])) for i, name in enumerate(
        ("a_q", "a_kv", "a_g", "b_cq", "b_ckv", "b_kr", "c_q", "c_kv", "z"))}
    dst = (("a_q", C_AQ), ("a_kv", C_CMP), ("b_cq", C_BCQ), ("b_ckv", C_BCKV), ("b_kr", C_KRB + HEAD_DIM),
           ("c_q", C_CQ), ("c_kv", C_CKV))
    n, L, ct = w_ref.shape
    g0, n_gate, tail = seg["a_g"][0], 3 * NSA_HEADS, LANES - GATE_LANE0
    gate_row = lax.broadcasted_iota(jnp.int32, (tail, ct), 0) < n_gate
    rows = w_ref.reshape(n * L, ct)
    pick = lambda a, b, l: rows[pl.ds(L * a + l, b - a, stride=L), :]
    for l in range(L):
        wp_ref[l, C_KRB:C_KRB + HEAD_DIM, :] = jnp.zeros((HEAD_DIM, ct), BF16)
        for name, c0 in dst:
            a, b = seg[name]
            wp_ref[l, c0:c0 + (b - a), :] = pick(a, b, l).astype(BF16)
        wp_ref[l, C_KRB + GATE_LANE0:C_CQ, :] = jnp.where(gate_row, pick(g0, g0 + tail, l), 0.0).astype(BF16)
        wz_ref[l] = pick(seg["z"][0], seg["z"][1], l).astype(BF16)


def _prep_w_in(w):
    L, d, n = w.shape
    ct = LANES
    return pl.pallas_call(
        _relayout_kernel,
        grid=(d // ct,),
        in_specs=[pl.BlockSpec((n, L, ct), lambda r: (0, 0, r))],
        out_specs=[pl.BlockSpec((L, C_Z, ct), lambda r: (0, 0, r)),
                   pl.BlockSpec((L, IN_SIZES[-1], ct), lambda r: (0, 0, r))],
        out_shape=[jax.ShapeDtypeStruct((L, C_Z, d), BF16), jax.ShapeDtypeStruct((L, IN_SIZES[-1], d), BF16)],
        compiler_params=_cparams(("arbitrary",)),
        name="w_in_relayout",
    )(jnp.transpose(w, (2, 0, 1)))


def _prep_mla(w_uq, w_ukv):
    L = w_uq.shape[0]
    dq = MLA_NOPE + MLA_ROPE
    uq = w_uq.reshape(L, MLA_Q_RANK, MLA_HEADS, dq)
    uq = jnp.concatenate([uq, jnp.zeros((L, MLA_Q_RANK, MLA_HEADS, LANES - dq), uq.dtype)], axis=-1)
    ukv = w_ukv.reshape(L, MLA_KV_RANK, MLA_HEADS, MLA_NOPE + MLA_V)
    uk = jnp.concatenate([ukv[..., :MLA_NOPE], jnp.zeros((L, MLA_KV_RANK, MLA_HEADS, LANES - MLA_NOPE), ukv.dtype)], axis=-1)
    uv = jnp.concatenate([jnp.zeros((L, MLA_KV_RANK, MLA_HEADS, LANES - MLA_V), ukv.dtype), ukv[..., MLA_NOPE:]], axis=-1)
    return (uq.reshape(L, MLA_Q_RANK, MLA_HEADS * LANES).astype(BF16),
            uk.reshape(L, MLA_KV_RANK, MLA_HEADS * LANES).astype(BF16),
            uv.reshape(L, MLA_KV_RANK, MLA_HEADS * LANES).astype(BF16))


def _prep_compress(cmp_pos, cmp_w1, cmp_w2):
    L = cmp_pos.shape[0]
    pe = jnp.concatenate([cmp_pos[:, 0], cmp_pos[:, 1]], axis=-1)
    pet = pe[:, :CMP_STRIDE].reshape(L, 1, CMP_STRIDE * LANES)
    peb = pe[:, CMP_STRIDE:].reshape(L, 1, CMP_STRIDE * LANES)
    w1 = cmp_w1.reshape(L, 2, CMP_LEN, HEAD_DIM, CMP_HIDDEN)
    zero = jnp.zeros_like(w1[:, 0])
    w1k = jnp.concatenate([w1[:, 0], zero], axis=2)
    w1v = jnp.concatenate([zero, w1[:, 1]], axis=2)
    w1e = jnp.stack([w1k, w1v], axis=1)
    w1t = w1e[:, :, :CMP_STRIDE].reshape(L, 2, CMP_STRIDE * LANES, CMP_HIDDEN).astype(BF16)
    w1b = w1e[:, :, CMP_STRIDE:].reshape(L, 2, CMP_STRIDE * LANES, CMP_HIDDEN).astype(BF16)
    z2 = jnp.zeros_like(cmp_w2[:, 0])
    w2 = jnp.stack([jnp.concatenate([cmp_w2[:, 0], z2], axis=-1),
                    jnp.concatenate([z2, cmp_w2[:, 1]], axis=-1)], axis=1).astype(BF16)
    return pet, peb, w1t, w1b, w2


def _rope_tables(T):
    half = MLA_ROPE // 2
    inv = ROPE_THETA ** (-jnp.arange(half, dtype=F32) / half)
    ang = jnp.arange(T).astype(F32)[:, None] * inv[None, :]
    cos, sin = jnp.cos(ang), jnp.sin(ang)
    z = lambda n: jnp.zeros((T, n), F32)
    rc = jnp.concatenate([jnp.ones((T, MLA_NOPE), F32), cos, cos, z(LANES - MLA_NOPE - MLA_ROPE)], axis=1)
    rsa = jnp.concatenate([z(MLA_NOPE + half), sin, z(LANES - MLA_NOPE - MLA_ROPE)], axis=1)
    rsb = jnp.concatenate([z(MLA_NOPE), -sin, z(LANES - MLA_NOPE - half)], axis=1)
    return rc, rsa, rsb


def _static_tables(T):
    ns = T // SEL_BLOCK
    nc = T // CMP_STRIDE - 1
    ncp = T // CMP_STRIDE
    sstart = np.arange(ns) * SEL_BLOCK
    cstart = np.arange(nc) * CMP_STRIDE
    overlap = (np.clip(np.minimum(cstart[:, None] + CMP_LEN, sstart[None, :] + SEL_BLOCK)
                       - np.maximum(cstart[:, None], sstart[None, :]), 0, None) / CMP_STRIDE).astype(np.float32)
    ovt = np.zeros((ns, ncp), np.float32)
    ovt[:, :nc] = overlap.T
    eaug = np.zeros((T, LANES), np.float32)
    eaug[np.arange(T), HEAD_DIM + np.arange(T) // SEL_BLOCK] = 1.0
    gexp = np.zeros((2, LANES, 3 * NSA_HEADS * HEAD_DIM), np.float32)
    for h in range(NSA_HEADS):
        for j in range(3):
            c0 = NSA_HEADS * HEAD_DIM * j + HEAD_DIM * h
            gexp[:, GATE_LANE0 + 3 * h + j, c0:c0 + HEAD_DIM] = 1.0
    gexp = gexp.reshape(2 * LANES, -1)
    return jnp.asarray(ovt, BF16), jnp.asarray(eaug, BF16), jnp.asarray(gexp, BF16)


def kernel(x, w_in, w_out, norm_pre, norm_post, cmp_pos, cmp_w1, cmp_w2, mla_q_norm, mla_w_uq, mla_kv_norm,
           mla_w_ukv, swa_sinks, rel_bias):
    B, T, D = x.shape
    depth = w_in.shape[0]
    assert D == D_MODEL and T % (2 * FLASH_TILE) == 0 and T // SEL_BLOCK <= HEAD_DIM
    ncp = T // CMP_STRIDE

    w_in_p, w_z = _prep_w_in(w_in)
    wuq, wk, wv = _prep_mla(mla_w_uq, mla_w_ukv)
    pet, peb, w1t, w1b, w2 = _prep_compress(cmp_pos, cmp_w1, cmp_w2)
    w_out_b = w_out.astype(BF16)
    rope_c, rope_sa, rope_sb = _rope_tables(T)
    ovt, eaug, gexp = _static_tables(T)

    bc = _bias_table(rel_bias, 0, NSA_HEADS, ncp, T, CMP_STRIDE, 1, -(CMP_LEN - 1), 0, 1 << 30, 2 * FLASH_TILE,
                     mult=LOG2E, transposed=True).reshape(NSA_HEADS, T, ncp)
    tab_near = _bias_table(rel_bias, 0, NSA_HEADS, FLASH_TILE, 2 * FLASH_TILE, 1, -1, FLASH_TILE, 0, 1 << 30, 256,
                           mult=LOG2E).reshape(NSA_HEADS, FLASH_TILE, 2 * FLASH_TILE)
    tab_win = _bias_table(rel_bias, 0, NSA_HEADS, WIN_TILE, NSA_WINDOW + WIN_TILE, 1, -1, 0, 0, NSA_WINDOW, WIN_TILE,
                          mult=LOG2E, nblk=NSA_WINDOW // WIN_TILE + 1, blk_d0=WIN_TILE)
    tab_swa = _bias_table(rel_bias, NSA_HEADS, SWA_HEADS, Q_TILE, SWA_WINDOW + Q_TILE, 1, -1, 0, 0, SWA_WINDOW, Q_TILE,
                          mult=LOG2E, nblk=SWA_WINDOW // Q_TILE + 1, blk_d0=Q_TILE)
    b31 = rel_bias[:NSA_HEADS, REL_BUCKETS - 1] * LOG2E

    x2 = x.reshape(B * T, D)
    for l in range(depth):
        (qa, cmpkv, slck, slcv, win, qm, km, vm, krb, cq, ckv) = _project(
            x2, norm_pre[l][None], w_in_p, l, mla_q_norm[l][None], wuq[l], mla_kv_norm[l][None], wk[l], wv[l],
            rope_c, rope_sa, rope_sb, eaug, T, PROJ_ROWS)
        r3 = lambda a: a.reshape(B, T, a.shape[-1])
        kvc = _compress(r3(cmpkv), pet[l], peb[l], w1t[l], w1b[l], w2[l])
        qa3 = r3(qa)
        ocmp, selb = _cmp_attention(qa3, kvc, bc, ovt, T)
        owin = _band_attention(qa3, r3(win), tab_win, None, NSA_HEADS, NSA_WINDOW, T, BF16)
        oslc = _slc_attention(b31, qa3, selb, r3(slck), r3(slcv), tab_near, T)
        oc = _band_attention(r3(cq), r3(ckv), tab_swa, swa_sinks[l] * LOG2E, SWA_HEADS, SWA_WINDOW, T, BF16)
        ob = _mla_attention(r3(qm), r3(km), r3(vm), T)
        flat = lambda a: a.reshape(B * T, a.shape[-1])
        x2 = _out_project(x2, norm_pre[l][None], w_z, l, flat(ocmp), flat(oslc), flat(owin), krb, gexp,
                          flat(ob), flat(oc), w_out_b, norm_post[l][None], OUT_ROWS)
    return x2.reshape(B, T, D)
```
